```python
import math
import jax, jax.numpy as jnp
from jax import lax
import numpy as np

D_MODEL = 1024
BATCH = 8
SEQ = 16384
DEPTH = 2

N_MIXERS = 2
N_ATTN_LAYERS = (DEPTH + N_MIXERS - 1) // N_MIXERS
N_SSM_LAYERS = DEPTH // N_MIXERS

ATTN_HEAD_DIM = 64
ATTN_HEADS = D_MODEL // ATTN_HEAD_DIM
Q_BLOCK = 128
SSM_EXPAND = 2
SSM_D_INNER = SSM_EXPAND * D_MODEL
SSM_HEAD_DIM = 64
SSM_HEADS = SSM_D_INNER // SSM_HEAD_DIM
SSM_GROUPS = 8
SSM_STATE = 128
SSM_CONV = 4
SSM_CHUNK = 128
SSM_XBC = SSM_D_INNER + 2 * SSM_GROUPS * SSM_STATE
FFN_DIM = 2816
FFN_CONV = 3
PLE_DIM = 256

LN_EPS = 1e-5
RMS_EPS = 1e-5
DEEPNORM_ALPHA = (2 * DEPTH) ** 0.25
DEEPNORM_BETA = (8 * DEPTH) ** -0.25

kernel_name = "fox_ssd_interleaved_deepnorm_trunk"


def layer_norm(x, g, b):
    xf = x.astype(jnp.float32)
    mu = jnp.mean(xf, axis=-1, keepdims=True)
    var = jnp.mean(jnp.square(xf - mu), axis=-1, keepdims=True)
    return ((xf - mu) * lax.rsqrt(var + LN_EPS) * g + b).astype(x.dtype)


def causal_dwconv(u, w, b):
    K = w.shape[0]
    S = u.shape[1]
    up = jnp.pad(u, ((0, 0), (K - 1, 0), (0, 0)))
    out = b
    for k in range(K):
        out = out + up[:, k:k + S] * w[k]
    return out


def fox_mixer(x, w_in, b_f, w_out):
    Bsz, S, _ = x.shape
    H, Dh = ATTN_HEADS, ATTN_HEAD_DIM
    proj = x @ w_in
    q, k, v, f_logit = jnp.split(proj, [D_MODEL, 2 * D_MODEL, 3 * D_MODEL], axis=-1)
    q = q.reshape(Bsz, S, H, Dh).transpose(0, 2, 1, 3)
    k = k.reshape(Bsz, S, H, Dh).transpose(0, 2, 1, 3)
    v = v.reshape(Bsz, S, H, Dh).transpose(0, 2, 1, 3)
    log_f = jax.nn.log_sigmoid(f_logit.astype(jnp.float32) + b_f)
    c = jnp.cumsum(log_f, axis=1).transpose(0, 2, 1)
    nb = S // Q_BLOCK
    qb = q.reshape(Bsz, H, nb, Q_BLOCK, Dh).transpose(2, 0, 1, 3, 4)
    cqb = c.reshape(Bsz, H, nb, Q_BLOCK).transpose(2, 0, 1, 3)
    kpos = jnp.arange(S)
    scale = 1.0 / math.sqrt(Dh)

    def block(args):
        qi, cqi, bi = args
        s = jnp.einsum('bhqd,bhkd->bhqk', qi, k).astype(jnp.float32) * scale
        s = s + cqi[..., None] - c[:, :, None, :]
        qpos = bi * Q_BLOCK + jnp.arange(Q_BLOCK)
        s = jnp.where(kpos[None, :] <= qpos[:, None], s, -jnp.inf)
        pr = jax.nn.softmax(s, axis=-1)
        return jnp.einsum('bhqk,bhkd->bhqd', pr.astype(v.dtype), v)

    o = lax.map(block, (qb, cqb, jnp.arange(nb)))
    o = o.transpose(1, 0, 3, 2, 4).reshape(Bsz, S, D_MODEL)
    return o @ w_out


def ssd_scan(xh, dt, A, Bm, Cm):
    Bsz, S = xh.shape[0], xh.shape[1]
    G, R, P, N, Q = SSM_GROUPS, SSM_HEADS // SSM_GROUPS, SSM_HEAD_DIM, SSM_STATE, SSM_CHUNK
    nc = S // Q

    def to_chunks(t):
        return jnp.moveaxis(t.reshape((Bsz, nc, Q) + t.shape[2:]), 1, 0)

    xc = to_chunks(xh.reshape(Bsz, S, G, R, P))
    dtc = to_chunks(dt.reshape(Bsz, S, G, R))
    Bc = to_chunks(Bm)
    Cc = to_chunks(Cm)
    A_g = A.reshape(G, R)
    tri = jnp.tril(jnp.ones((Q, Q), dtype=bool))[None, :, :, None, None]

    def step(state, inp):
        x_, dt_, B_, C_ = inp
        acum = jnp.cumsum(dt_ * A_g, axis=1)
        seg = acum[:, :, None] - acum[:, None, :]
        L = jnp.exp(jnp.where(tri, seg, -jnp.inf))
        CB = jnp.einsum('btgn,bsgn->btsg', C_, B_)
        y_intra = jnp.einsum('btsg,btsgr,bsgr,bsgrp->btgrp', CB, L, dt_, x_)
        y_inter = jnp.einsum('btgn,bgrpn,btgr->btgrp', C_, state, jnp.exp(acum))
        w_end = jnp.exp(acum[:, -1:] - acum) * dt_
        new_state = state * jnp.exp(acum[:, -1])[..., None, None] + \
            jnp.einsum('bsgn,bsgr,bsgrp->bgrpn', B_, w_end, x_)
        return new_state, (y_intra + y_inter).astype(jnp.float32)

    state0 = jnp.zeros((Bsz, G, R, P, N), jnp.float32)
    _, y = lax.scan(step, state0, (xc, dtc, Bc, Cc))
    return jnp.moveaxis(y, 0, 1).reshape(Bsz, S, SSM_HEADS, P)


def ssd_mixer(x, w_in, conv_w, conv_b, dt_bias, A_log, D_skip, norm_w, w_out):
    Bsz, S, _ = x.shape
    GN = SSM_GROUPS * SSM_STATE
    proj = x @ w_in
    z, xBC, dt_raw = jnp.split(proj, [SSM_D_INNER, SSM_D_INNER + SSM_XBC], axis=-1)
    xBC = jax.nn.silu(causal_dwconv(xBC, conv_w, conv_b))
    xs, Bm, Cm = jnp.split(xBC, [SSM_D_INNER, SSM_D_INNER + GN], axis=-1)
    dt = jax.nn.softplus(dt_raw.astype(jnp.float32) + dt_bias)
    A = -jnp.exp(A_log.astype(jnp.float32))
    xh = xs.reshape(Bsz, S, SSM_HEADS, SSM_HEAD_DIM)
    y = ssd_scan(xh, dt, A,
                 Bm.reshape(Bsz, S, SSM_GROUPS, SSM_STATE),
                 Cm.reshape(Bsz, S, SSM_GROUPS, SSM_STATE))
    y = y + D_skip[:, None] * xh
    y = y.reshape(Bsz, S, SSM_D_INNER) * jax.nn.silu(z.astype(jnp.float32))
    yg = y.reshape(Bsz, S, SSM_GROUPS, SSM_D_INNER // SSM_GROUPS)
    yg = yg * lax.rsqrt(jnp.mean(jnp.square(yg), axis=-1, keepdims=True) + RMS_EPS)
    y = yg.reshape(Bsz, S, SSM_D_INNER) * norm_w
    return y.astype(x.dtype) @ w_out


def conv_ffn(x, w_up, conv_w, conv_b, w_down):
    u, g = jnp.split(x @ w_up, [FFN_DIM], axis=-1)
    g = causal_dwconv(g, conv_w, conv_b)
    return (jax.nn.gelu(g, approximate=False) * u) @ w_down


def _fwd_setup_inputs(seed: int = 0) -> dict:
    key = jax.random.key(seed)
    ks = iter(jax.random.split(key, 40))

    def nrm(shape, scale):
        return jax.random.normal(next(ks), shape, jnp.float32) * scale

    D, H = D_MODEL, ATTN_HEADS
    NA, NB = N_ATTN_LAYERS, N_SSM_LAYERS
    beta = DEEPNORM_BETA
    x = nrm((BATCH, SEQ, D), 1.0)
    p = nrm((DEPTH, BATCH, SEQ, PLE_DIM), 1.0)
    attn_w_in = nrm((NA, D, 3 * D + H), D ** -0.5)
    attn_w_in = attn_w_in.at[:, :, 2 * D:3 * D].multiply(beta)
    attn_b_f = jax.random.uniform(next(ks), (NA, H), jnp.float32, 1.0, 6.0)
    attn_w_out = nrm((NA, D, D), beta * D ** -0.5)
    ssm_in_dim = 2 * SSM_D_INNER + 2 * SSM_GROUPS * SSM_STATE + SSM_HEADS
    ssm_w_in = nrm((NB, D, ssm_in_dim), D ** -0.5)
    ssm_conv_w = nrm((NB, SSM_CONV, SSM_XBC), SSM_CONV ** -0.5)
    ssm_conv_b = nrm((NB, SSM_XBC), 0.02)
    dt0 = jnp.exp(jax.random.uniform(next(ks), (NB, SSM_HEADS), jnp.float32,
                                     math.log(1e-3), math.log(1e-1)))
    ssm_dt_bias = dt0 + jnp.log(-jnp.expm1(-dt0))
    ssm_A_log = jnp.log(jax.random.uniform(next(ks), (NB, SSM_HEADS), jnp.float32, 1.0, 16.0))
    ssm_D = 1.0 + nrm((NB, SSM_HEADS), 0.1)
    ssm_norm_w = 1.0 + nrm((NB, SSM_D_INNER), 0.05)
    ssm_w_out = nrm((NB, SSM_D_INNER, D), beta * SSM_D_INNER ** -0.5)
    ln_mix_g = 1.0 + nrm((DEPTH, D), 0.05)
    ln_mix_b = nrm((DEPTH, D), 0.02)
    ffn_w_up = nrm((DEPTH, D, 2 * FFN_DIM), D ** -0.5)
    ffn_conv_w = nrm((DEPTH, FFN_CONV, FFN_DIM), FFN_CONV ** -0.5)
    ffn_conv_b = nrm((DEPTH, FFN_DIM), 0.02)
    ffn_w_down = nrm((DEPTH, FFN_DIM, D), beta * FFN_DIM ** -0.5)
    ln_ffn_g = 1.0 + nrm((DEPTH, D), 0.05)
    ln_ffn_b = nrm((DEPTH, D), 0.02)
    ple_w_proj = nrm((DEPTH, PLE_DIM, D), beta * PLE_DIM ** -0.5)
    ple_w_gate = nrm((DEPTH, D, D), D ** -0.5)
    ple_b_gate = nrm((DEPTH, D), 0.02)
    return {"x": x, "p": p,
            "attn_w_in": attn_w_in, "attn_b_f": attn_b_f, "attn_w_out": attn_w_out,
            "ssm_w_in": ssm_w_in, "ssm_conv_w": ssm_conv_w, "ssm_conv_b": ssm_conv_b,
            "ssm_dt_bias": ssm_dt_bias, "ssm_A_log": ssm_A_log, "ssm_D": ssm_D,
            "ssm_norm_w": ssm_norm_w, "ssm_w_out": ssm_w_out,
            "ln_mix_g": ln_mix_g, "ln_mix_b": ln_mix_b,
            "ffn_w_up": ffn_w_up, "ffn_conv_w": ffn_conv_w, "ffn_conv_b": ffn_conv_b,
            "ffn_w_down": ffn_w_down, "ln_ffn_g": ln_ffn_g, "ln_ffn_b": ln_ffn_b,
            "ple_w_proj": ple_w_proj, "ple_w_gate": ple_w_gate, "ple_b_gate": ple_b_gate}


def _fwd_reference(x, p, attn_w_in, attn_b_f, attn_w_out, ssm_w_in, ssm_conv_w, ssm_conv_b,
              ssm_dt_bias, ssm_A_log, ssm_D, ssm_norm_w, ssm_w_out, ln_mix_g, ln_mix_b,
              ffn_w_up, ffn_conv_w, ffn_conv_b, ffn_w_down, ln_ffn_g, ln_ffn_b,
              ple_w_proj, ple_w_gate, ple_b_gate):
    for i in range(DEPTH):
        j = i // N_MIXERS
        if i % N_MIXERS == 0:
            mix = fox_mixer(x, attn_w_in[j], attn_b_f[j], attn_w_out[j])
        else:
            mix = ssd_mixer(x, ssm_w_in[j], ssm_conv_w[j], ssm_conv_b[j], ssm_dt_bias[j],
                            ssm_A_log[j], ssm_D[j], ssm_norm_w[j], ssm_w_out[j])
        x = layer_norm(DEEPNORM_ALPHA * x + mix, ln_mix_g[i], ln_mix_b[i])
        ffn = conv_ffn(x, ffn_w_up[i], ffn_conv_w[i], ffn_conv_b[i], ffn_w_down[i])
        x = layer_norm(DEEPNORM_ALPHA * x + ffn, ln_ffn_g[i], ln_ffn_b[i])
        gate = jax.nn.sigmoid(x @ ple_w_gate[i] + ple_b_gate[i])
        x = x + gate * (p[i] @ ple_w_proj[i])
    return x


import jax as _jax
import jax.numpy as _jnp

TWIN_FORMAT = 'train_step'
FWD_PARAMS = ['x', 'p', 'attn_w_in', 'attn_b_f', 'attn_w_out', 'ssm_w_in', 'ssm_conv_w', 'ssm_conv_b', 'ssm_dt_bias', 'ssm_A_log', 'ssm_D', 'ssm_norm_w', 'ssm_w_out', 'ln_mix_g', 'ln_mix_b', 'ffn_w_up', 'ffn_conv_w', 'ffn_conv_b', 'ffn_w_down', 'ln_ffn_g', 'ln_ffn_b', 'ple_w_proj', 'ple_w_gate', 'ple_b_gate']
TWIN_WEIGHTS = ['attn_w_in', 'attn_b_f', 'attn_w_out', 'ssm_w_in', 'ssm_conv_w', 'ssm_conv_b', 'ssm_dt_bias', 'ssm_A_log', 'ssm_D', 'ssm_norm_w', 'ssm_w_out', 'ln_mix_g', 'ln_mix_b', 'ffn_w_up', 'ffn_conv_w', 'ffn_conv_b', 'ffn_w_down', 'ln_ffn_g', 'ln_ffn_b', 'ple_w_proj', 'ple_w_gate', 'ple_b_gate']
TWIN_DIFF_INPUT = 'x'
TWIN_INPUTS = ['x', 'p', 'attn_w_in', 'attn_b_f', 'attn_w_out', 'ssm_w_in', 'ssm_conv_w', 'ssm_conv_b', 'ssm_dt_bias', 'ssm_A_log', 'ssm_D', 'ssm_norm_w', 'ssm_w_out', 'ln_mix_g', 'ln_mix_b', 'ffn_w_up', 'ffn_conv_w', 'ffn_conv_b', 'ffn_w_down', 'ln_ffn_g', 'ln_ffn_b', 'ple_w_proj', 'ple_w_gate', 'ple_b_gate', 'loss_target', 'm_attn_w_in', 'm_attn_b_f', 'm_attn_w_out', 'm_ssm_w_in', 'm_ssm_conv_w', 'm_ssm_conv_b', 'm_ssm_dt_bias', 'm_ssm_A_log', 'm_ssm_D', 'm_ssm_norm_w', 'm_ssm_w_out', 'm_ln_mix_g', 'm_ln_mix_b', 'm_ffn_w_up', 'm_ffn_conv_w', 'm_ffn_conv_b', 'm_ffn_w_down', 'm_ln_ffn_g', 'm_ln_ffn_b', 'm_ple_w_proj', 'm_ple_w_gate', 'm_ple_b_gate', 'v_attn_w_in', 'v_attn_b_f', 'v_attn_w_out', 'v_ssm_w_in', 'v_ssm_conv_w', 'v_ssm_conv_b', 'v_ssm_dt_bias', 'v_ssm_A_log', 'v_ssm_D', 'v_ssm_norm_w', 'v_ssm_w_out', 'v_ln_mix_g', 'v_ln_mix_b', 'v_ffn_w_up', 'v_ffn_conv_w', 'v_ffn_conv_b', 'v_ffn_w_down', 'v_ln_ffn_g', 'v_ln_ffn_b', 'v_ple_w_proj', 'v_ple_w_gate', 'v_ple_b_gate']
TWIN_OUTPUTS = ['loss', 'grad_x', 'grad_attn_w_in', 'grad_attn_b_f', 'grad_attn_w_out', 'grad_ssm_w_in', 'grad_ssm_conv_w', 'grad_ssm_conv_b', 'grad_ssm_dt_bias', 'grad_ssm_A_log', 'grad_ssm_D', 'grad_ssm_norm_w', 'grad_ssm_w_out', 'grad_ln_mix_g', 'grad_ln_mix_b', 'grad_ffn_w_up', 'grad_ffn_conv_w', 'grad_ffn_conv_b', 'grad_ffn_w_down', 'grad_ln_ffn_g', 'grad_ln_ffn_b', 'grad_ple_w_proj', 'grad_ple_w_gate', 'grad_ple_b_gate', 'delta_attn_w_in', 'delta_attn_b_f', 'delta_attn_w_out', 'delta_ssm_w_in', 'delta_ssm_conv_w', 'delta_ssm_conv_b', 'delta_ssm_dt_bias', 'delta_ssm_A_log', 'delta_ssm_D', 'delta_ssm_norm_w', 'delta_ssm_w_out', 'delta_ln_mix_g', 'delta_ln_mix_b', 'delta_ffn_w_up', 'delta_ffn_conv_w', 'delta_ffn_conv_b', 'delta_ffn_w_down', 'delta_ln_ffn_g', 'delta_ln_ffn_b', 'delta_ple_w_proj', 'delta_ple_w_gate', 'delta_ple_b_gate', 'new_m_attn_w_in', 'new_m_attn_b_f', 'new_m_attn_w_out', 'new_m_ssm_w_in', 'new_m_ssm_conv_w', 'new_m_ssm_conv_b', 'new_m_ssm_dt_bias', 'new_m_ssm_A_log', 'new_m_ssm_D', 'new_m_ssm_norm_w', 'new_m_ssm_w_out', 'new_m_ln_mix_g', 'new_m_ln_mix_b', 'new_m_ffn_w_up', 'new_m_ffn_conv_w', 'new_m_ffn_conv_b', 'new_m_ffn_w_down', 'new_m_ln_ffn_g', 'new_m_ln_ffn_b', 'new_m_ple_w_proj', 'new_m_ple_w_gate', 'new_m_ple_b_gate', 'new_v_attn_w_in', 'new_v_attn_b_f', 'new_v_attn_w_out', 'new_v_ssm_w_in', 'new_v_ssm_conv_w', 'new_v_ssm_conv_b', 'new_v_ssm_dt_bias', 'new_v_ssm_A_log', 'new_v_ssm_D', 'new_v_ssm_norm_w', 'new_v_ssm_w_out', 'new_v_ln_mix_g', 'new_v_ln_mix_b', 'new_v_ffn_w_up', 'new_v_ffn_conv_w', 'new_v_ffn_conv_b', 'new_v_ffn_w_down', 'new_v_ln_ffn_g', 'new_v_ln_ffn_b', 'new_v_ple_w_proj', 'new_v_ple_w_gate', 'new_v_ple_b_gate']
TWIN_LEAF_KINDS = {'loss': 'loss', 'grad_x': 'grad_x', 'grad_attn_w_in': 'grad_w', 'grad_attn_b_f': 'grad_w', 'grad_attn_w_out': 'grad_w', 'grad_ssm_w_in': 'grad_w', 'grad_ssm_conv_w': 'grad_w', 'grad_ssm_conv_b': 'grad_w', 'grad_ssm_dt_bias': 'grad_w', 'grad_ssm_A_log': 'grad_w', 'grad_ssm_D': 'grad_w', 'grad_ssm_norm_w': 'grad_w', 'grad_ssm_w_out': 'grad_w', 'grad_ln_mix_g': 'grad_w', 'grad_ln_mix_b': 'grad_w', 'grad_ffn_w_up': 'grad_w', 'grad_ffn_conv_w': 'grad_w', 'grad_ffn_conv_b': 'grad_w', 'grad_ffn_w_down': 'grad_w', 'grad_ln_ffn_g': 'grad_w', 'grad_ln_ffn_b': 'grad_w', 'grad_ple_w_proj': 'grad_w', 'grad_ple_w_gate': 'grad_w', 'grad_ple_b_gate': 'grad_w', 'delta_attn_w_in': 'delta_w', 'delta_attn_b_f': 'delta_w', 'delta_attn_w_out': 'delta_w', 'delta_ssm_w_in': 'delta_w', 'delta_ssm_conv_w': 'delta_w', 'delta_ssm_conv_b': 'delta_w', 'delta_ssm_dt_bias': 'delta_w', 'delta_ssm_A_log': 'delta_w', 'delta_ssm_D': 'delta_w', 'delta_ssm_norm_w': 'delta_w', 'delta_ssm_w_out': 'delta_w', 'delta_ln_mix_g': 'delta_w', 'delta_ln_mix_b': 'delta_w', 'delta_ffn_w_up': 'delta_w', 'delta_ffn_conv_w': 'delta_w', 'delta_ffn_conv_b': 'delta_w', 'delta_ffn_w_down': 'delta_w', 'delta_ln_ffn_g': 'delta_w', 'delta_ln_ffn_b': 'delta_w', 'delta_ple_w_proj': 'delta_w', 'delta_ple_w_gate': 'delta_w', 'delta_ple_b_gate': 'delta_w', 'new_m_attn_w_in': 'new_m', 'new_m_attn_b_f': 'new_m', 'new_m_attn_w_out': 'new_m', 'new_m_ssm_w_in': 'new_m', 'new_m_ssm_conv_w': 'new_m', 'new_m_ssm_conv_b': 'new_m', 'new_m_ssm_dt_bias': 'new_m', 'new_m_ssm_A_log': 'new_m', 'new_m_ssm_D': 'new_m', 'new_m_ssm_norm_w': 'new_m', 'new_m_ssm_w_out': 'new_m', 'new_m_ln_mix_g': 'new_m', 'new_m_ln_mix_b': 'new_m', 'new_m_ffn_w_up': 'new_m', 'new_m_ffn_conv_w': 'new_m', 'new_m_ffn_conv_b': 'new_m', 'new_m_ffn_w_down': 'new_m', 'new_m_ln_ffn_g': 'new_m', 'new_m_ln_ffn_b': 'new_m', 'new_m_ple_w_proj': 'new_m', 'new_m_ple_w_gate': 'new_m', 'new_m_ple_b_gate': 'new_m', 'new_v_attn_w_in': 'new_v', 'new_v_attn_b_f': 'new_v', 'new_v_attn_w_out': 'new_v', 'new_v_ssm_w_in': 'new_v', 'new_v_ssm_conv_w': 'new_v', 'new_v_ssm_conv_b': 'new_v', 'new_v_ssm_dt_bias': 'new_v', 'new_v_ssm_A_log': 'new_v', 'new_v_ssm_D': 'new_v', 'new_v_ssm_norm_w': 'new_v', 'new_v_ssm_w_out': 'new_v', 'new_v_ln_mix_g': 'new_v', 'new_v_ln_mix_b': 'new_v', 'new_v_ffn_w_up': 'new_v', 'new_v_ffn_conv_w': 'new_v', 'new_v_ffn_conv_b': 'new_v', 'new_v_ffn_w_down': 'new_v', 'new_v_ln_ffn_g': 'new_v', 'new_v_ln_ffn_b': 'new_v', 'new_v_ple_w_proj': 'new_v', 'new_v_ple_w_gate': 'new_v', 'new_v_ple_b_gate': 'new_v'}


def _forward(args):
    return _fwd_reference(*[args[k] for k in FWD_PARAMS])


def _output_shape():
    def fwd():
        inp = _fwd_setup_inputs(0)
        return _fwd_reference(*[inp[k] for k in FWD_PARAMS])
    out = _jax.eval_shape(fwd)
    return out.shape, out.dtype

N_MICROBATCH = 1
ADAM_LR = 0.001
ADAM_B1 = 0.9
ADAM_B2 = 0.999
ADAM_EPS = 1e-08
ADAM_WD = 0.01
ADAM_STEP = 10
PER_EXAMPLE_BATCH_AXIS = {'x': 0, 'p': 1, 'loss_target': 0}
SHARED_INPUTS = []
_WEIGHT_DTYPES = {'attn_w_in': _jnp.float32, 'attn_b_f': _jnp.float32, 'attn_w_out': _jnp.float32, 'ssm_w_in': _jnp.float32, 'ssm_conv_w': _jnp.float32, 'ssm_conv_b': _jnp.float32, 'ssm_dt_bias': _jnp.float32, 'ssm_A_log': _jnp.float32, 'ssm_D': _jnp.float32, 'ssm_norm_w': _jnp.float32, 'ssm_w_out': _jnp.float32, 'ln_mix_g': _jnp.float32, 'ln_mix_b': _jnp.float32, 'ffn_w_up': _jnp.float32, 'ffn_conv_w': _jnp.float32, 'ffn_conv_b': _jnp.float32, 'ffn_w_down': _jnp.float32, 'ln_ffn_g': _jnp.float32, 'ln_ffn_b': _jnp.float32, 'ple_w_proj': _jnp.float32, 'ple_w_gate': _jnp.float32, 'ple_b_gate': _jnp.float32}
MOMENT_SCALE = {'attn_w_in': 3.271524e-02, 'attn_b_f': 2.136605e-01, 'attn_w_out': 5.017612e-02, 'ssm_w_in': 7.536267e-02, 'ssm_conv_w': 8.146255e-02, 'ssm_conv_b': 2.131616e-01, 'ssm_dt_bias': 2.032794e-01, 'ssm_A_log': 2.815628e-01, 'ssm_D': 4.635732e-01, 'ssm_norm_w': 1.476441e-01, 'ssm_w_out': 3.809674e-01, 'ln_mix_g': 1.038341e+01, 'ln_mix_b': 3.505613e+00, 'ffn_w_up': 5.431367e-02, 'ffn_conv_w': 5.488417e-02, 'ffn_conv_b': 5.534582e-02, 'ffn_w_down': 1.798326e-01, 'ln_ffn_g': 9.271571e+01, 'ln_ffn_b': 7.812870e+00, 'ple_w_proj': 4.694275e-01, 'ple_w_gate': 1.047371e-01, 'ple_b_gate': 2.315506e+00}


def _to_microbatches(a, axis):
    t = _jnp.moveaxis(a, axis, 0)
    t = t.reshape((N_MICROBATCH, t.shape[0] // N_MICROBATCH) + t.shape[1:])
    return _jnp.moveaxis(t, 1, axis + 1)


def setup_inputs(seed: int = 0) -> dict:
    inp = _fwd_setup_inputs(seed)
    key = _jax.random.fold_in(_jax.random.key(seed), 7919)
    shape, _ = _output_shape()
    out = dict(inp)
    out["loss_target"] = _jax.random.normal(_jax.random.fold_in(key, 0), shape, _jnp.float32)
    for i, name in enumerate(TWIN_WEIGHTS):
        w = inp[name].astype(_jnp.float32)
        if MOMENT_SCALE is None:
            s = _jnp.sqrt(_jnp.mean(_jnp.square(w)) + 1e-30)
        else:
            s = MOMENT_SCALE[name]
        km, kv = _jax.random.split(_jax.random.fold_in(key, i + 1))
        out[name] = w
        out["m_" + name] = s * _jax.random.normal(km, w.shape, _jnp.float32)
        out["v_" + name] = (s * s) * _jax.random.uniform(kv, w.shape, _jnp.float32, 0.5, 1.5)
    if N_MICROBATCH > 1:
        for name, axis in PER_EXAMPLE_BATCH_AXIS.items():
            out[name] = _to_microbatches(out[name], axis)
    return {'x': out['x'], 'p': out['p'], 'attn_w_in': out['attn_w_in'], 'attn_b_f': out['attn_b_f'], 'attn_w_out': out['attn_w_out'], 'ssm_w_in': out['ssm_w_in'], 'ssm_conv_w': out['ssm_conv_w'], 'ssm_conv_b': out['ssm_conv_b'], 'ssm_dt_bias': out['ssm_dt_bias'], 'ssm_A_log': out['ssm_A_log'], 'ssm_D': out['ssm_D'], 'ssm_norm_w': out['ssm_norm_w'], 'ssm_w_out': out['ssm_w_out'], 'ln_mix_g': out['ln_mix_g'], 'ln_mix_b': out['ln_mix_b'], 'ffn_w_up': out['ffn_w_up'], 'ffn_conv_w': out['ffn_conv_w'], 'ffn_conv_b': out['ffn_conv_b'], 'ffn_w_down': out['ffn_w_down'], 'ln_ffn_g': out['ln_ffn_g'], 'ln_ffn_b': out['ln_ffn_b'], 'ple_w_proj': out['ple_w_proj'], 'ple_w_gate': out['ple_w_gate'], 'ple_b_gate': out['ple_b_gate'], 'loss_target': out['loss_target'], 'm_attn_w_in': out['m_attn_w_in'], 'm_attn_b_f': out['m_attn_b_f'], 'm_attn_w_out': out['m_attn_w_out'], 'm_ssm_w_in': out['m_ssm_w_in'], 'm_ssm_conv_w': out['m_ssm_conv_w'], 'm_ssm_conv_b': out['m_ssm_conv_b'], 'm_ssm_dt_bias': out['m_ssm_dt_bias'], 'm_ssm_A_log': out['m_ssm_A_log'], 'm_ssm_D': out['m_ssm_D'], 'm_ssm_norm_w': out['m_ssm_norm_w'], 'm_ssm_w_out': out['m_ssm_w_out'], 'm_ln_mix_g': out['m_ln_mix_g'], 'm_ln_mix_b': out['m_ln_mix_b'], 'm_ffn_w_up': out['m_ffn_w_up'], 'm_ffn_conv_w': out['m_ffn_conv_w'], 'm_ffn_conv_b': out['m_ffn_conv_b'], 'm_ffn_w_down': out['m_ffn_w_down'], 'm_ln_ffn_g': out['m_ln_ffn_g'], 'm_ln_ffn_b': out['m_ln_ffn_b'], 'm_ple_w_proj': out['m_ple_w_proj'], 'm_ple_w_gate': out['m_ple_w_gate'], 'm_ple_b_gate': out['m_ple_b_gate'], 'v_attn_w_in': out['v_attn_w_in'], 'v_attn_b_f': out['v_attn_b_f'], 'v_attn_w_out': out['v_attn_w_out'], 'v_ssm_w_in': out['v_ssm_w_in'], 'v_ssm_conv_w': out['v_ssm_conv_w'], 'v_ssm_conv_b': out['v_ssm_conv_b'], 'v_ssm_dt_bias': out['v_ssm_dt_bias'], 'v_ssm_A_log': out['v_ssm_A_log'], 'v_ssm_D': out['v_ssm_D'], 'v_ssm_norm_w': out['v_ssm_norm_w'], 'v_ssm_w_out': out['v_ssm_w_out'], 'v_ln_mix_g': out['v_ln_mix_g'], 'v_ln_mix_b': out['v_ln_mix_b'], 'v_ffn_w_up': out['v_ffn_w_up'], 'v_ffn_conv_w': out['v_ffn_conv_w'], 'v_ffn_conv_b': out['v_ffn_conv_b'], 'v_ffn_w_down': out['v_ffn_w_down'], 'v_ln_ffn_g': out['v_ln_ffn_g'], 'v_ln_ffn_b': out['v_ln_ffn_b'], 'v_ple_w_proj': out['v_ple_w_proj'], 'v_ple_w_gate': out['v_ple_w_gate'], 'v_ple_b_gate': out['v_ple_b_gate']}


def _loss(weights, diff, rest, loss_target):
    with _jax.named_scope("forward"):
        args = {**rest, TWIN_DIFF_INPUT: diff, **{k: w.astype(_WEIGHT_DTYPES[k]) for k, w in weights.items()}}
        y = _forward(args)
    with _jax.named_scope("loss_head"):
        err = _jnp.square(y.astype(_jnp.float32) - loss_target)
        return 0.5 * _jnp.sum(_jnp.mean(err, axis=-1)) if err.ndim else 0.5 * err


def _adamw(w, g, m, v):
    m = ADAM_B1 * m + (1.0 - ADAM_B1) * g
    v = ADAM_B2 * v + (1.0 - ADAM_B2) * _jnp.square(g)
    m_hat = m / (1.0 - ADAM_B1 ** ADAM_STEP)
    v_hat = v / (1.0 - ADAM_B2 ** ADAM_STEP)
    delta = -ADAM_LR * (m_hat / (_jnp.sqrt(v_hat) + ADAM_EPS) + ADAM_WD * w)
    return delta, m, v


def reference(x, p, attn_w_in, attn_b_f, attn_w_out, ssm_w_in, ssm_conv_w, ssm_conv_b, ssm_dt_bias, ssm_A_log, ssm_D, ssm_norm_w, ssm_w_out, ln_mix_g, ln_mix_b, ffn_w_up, ffn_conv_w, ffn_conv_b, ffn_w_down, ln_ffn_g, ln_ffn_b, ple_w_proj, ple_w_gate, ple_b_gate, loss_target, m_attn_w_in, m_attn_b_f, m_attn_w_out, m_ssm_w_in, m_ssm_conv_w, m_ssm_conv_b, m_ssm_dt_bias, m_ssm_A_log, m_ssm_D, m_ssm_norm_w, m_ssm_w_out, m_ln_mix_g, m_ln_mix_b, m_ffn_w_up, m_ffn_conv_w, m_ffn_conv_b, m_ffn_w_down, m_ln_ffn_g, m_ln_ffn_b, m_ple_w_proj, m_ple_w_gate, m_ple_b_gate, v_attn_w_in, v_attn_b_f, v_attn_w_out, v_ssm_w_in, v_ssm_conv_w, v_ssm_conv_b, v_ssm_dt_bias, v_ssm_A_log, v_ssm_D, v_ssm_norm_w, v_ssm_w_out, v_ln_mix_g, v_ln_mix_b, v_ffn_w_up, v_ffn_conv_w, v_ffn_conv_b, v_ffn_w_down, v_ln_ffn_g, v_ln_ffn_b, v_ple_w_proj, v_ple_w_gate, v_ple_b_gate):
    given = dict(x=x, p=p, attn_w_in=attn_w_in, attn_b_f=attn_b_f, attn_w_out=attn_w_out, ssm_w_in=ssm_w_in, ssm_conv_w=ssm_conv_w, ssm_conv_b=ssm_conv_b, ssm_dt_bias=ssm_dt_bias, ssm_A_log=ssm_A_log, ssm_D=ssm_D, ssm_norm_w=ssm_norm_w, ssm_w_out=ssm_w_out, ln_mix_g=ln_mix_g, ln_mix_b=ln_mix_b, ffn_w_up=ffn_w_up, ffn_conv_w=ffn_conv_w, ffn_conv_b=ffn_conv_b, ffn_w_down=ffn_w_down, ln_ffn_g=ln_ffn_g, ln_ffn_b=ln_ffn_b, ple_w_proj=ple_w_proj, ple_w_gate=ple_w_gate, ple_b_gate=ple_b_gate, loss_target=loss_target, m_attn_w_in=m_attn_w_in, m_attn_b_f=m_attn_b_f, m_attn_w_out=m_attn_w_out, m_ssm_w_in=m_ssm_w_in, m_ssm_conv_w=m_ssm_conv_w, m_ssm_conv_b=m_ssm_conv_b, m_ssm_dt_bias=m_ssm_dt_bias, m_ssm_A_log=m_ssm_A_log, m_ssm_D=m_ssm_D, m_ssm_norm_w=m_ssm_norm_w, m_ssm_w_out=m_ssm_w_out, m_ln_mix_g=m_ln_mix_g, m_ln_mix_b=m_ln_mix_b, m_ffn_w_up=m_ffn_w_up, m_ffn_conv_w=m_ffn_conv_w, m_ffn_conv_b=m_ffn_conv_b, m_ffn_w_down=m_ffn_w_down, m_ln_ffn_g=m_ln_ffn_g, m_ln_ffn_b=m_ln_ffn_b, m_ple_w_proj=m_ple_w_proj, m_ple_w_gate=m_ple_w_gate, m_ple_b_gate=m_ple_b_gate, v_attn_w_in=v_attn_w_in, v_attn_b_f=v_attn_b_f, v_attn_w_out=v_attn_w_out, v_ssm_w_in=v_ssm_w_in, v_ssm_conv_w=v_ssm_conv_w, v_ssm_conv_b=v_ssm_conv_b, v_ssm_dt_bias=v_ssm_dt_bias, v_ssm_A_log=v_ssm_A_log, v_ssm_D=v_ssm_D, v_ssm_norm_w=v_ssm_norm_w, v_ssm_w_out=v_ssm_w_out, v_ln_mix_g=v_ln_mix_g, v_ln_mix_b=v_ln_mix_b, v_ffn_w_up=v_ffn_w_up, v_ffn_conv_w=v_ffn_conv_w, v_ffn_conv_b=v_ffn_conv_b, v_ffn_w_down=v_ffn_w_down, v_ln_ffn_g=v_ln_ffn_g, v_ln_ffn_b=v_ln_ffn_b, v_ple_w_proj=v_ple_w_proj, v_ple_w_gate=v_ple_w_gate, v_ple_b_gate=v_ple_b_gate)
    weights = {n: given[n] for n in TWIN_WEIGHTS}
    shared = {n: given[n] for n in SHARED_INPUTS}
    per_example = {n: given[n] for n in ['x', 'p']}
    grad_fn = _jax.value_and_grad(_loss, argnums=(0, 1))

    def one_microbatch(ex, loss_target):
        ex = dict(ex)
        diff = ex.pop(TWIN_DIFF_INPUT)
        return grad_fn(weights, diff, {**shared, **ex}, loss_target)

    if N_MICROBATCH == 1:
        loss, (grad_w, grad_x) = one_microbatch(per_example, given["loss_target"])
    else:
        def body(carry, xs):
            loss_sum, grad_sum = carry
            l_k, (gw_k, gx_k) = one_microbatch(xs[0], xs[1])
            with _jax.named_scope("update"):
                return (loss_sum + l_k, _jax.tree.map(_jnp.add, grad_sum, gw_k)), gx_k

        init = (_jnp.zeros((), _jnp.float32), _jax.tree.map(_jnp.zeros_like, weights))
        (loss, grad_w), grad_x = _jax.lax.scan(body, init, (per_example, given["loss_target"]))
    with _jax.named_scope("update"):
        delta_w, new_m, new_v = {}, {}, {}
        for n in TWIN_WEIGHTS:
            delta_w[n], new_m[n], new_v[n] = _adamw(weights[n], grad_w[n], given["m_" + n], given["v_" + n])
    return (loss, grad_x, *[grad_w[n] for n in TWIN_WEIGHTS], *[delta_w[n] for n in TWIN_WEIGHTS],
            *[new_m[n] for n in TWIN_WEIGHTS], *[new_v[n] for n in TWIN_WEIGHTS])
```

```python
import functools
import math

import jax
import jax.numpy as jnp
from jax import lax
from jax.experimental import pallas as pl
from jax.experimental.pallas import tpu as pltpu

F32 = jnp.float32
BF16 = jnp.bfloat16
MESH = pl.DeviceIdType.MESH

D_MODEL = 1024
ATTN_HEADS = 16
HEAD_PAIRS = ATTN_HEADS // 2
SSM_D_INNER = 2048
SSM_HEADS = 32
SSM_GROUPS = 8
SSM_STATE = 128
SSM_CONV = 4
SSM_CHUNK = 128
SSM_XBC = SSM_D_INNER + 2 * SSM_GROUPS * SSM_STATE
FFN_DIM = 2816
FFN_CONV = 3
DEPTH = 2
LN_EPS = 1e-5
RMS_EPS = 1e-5
DEEPNORM_ALPHA = (2 * DEPTH) ** 0.25
ADAM_LR = 0.001
ADAM_B1 = 0.9
ADAM_B2 = 0.999
ADAM_EPS = 1e-08
ADAM_WD = 0.01
ADAM_STEP = 10

LANE = 128
SUBLANE = 8
HALO = SUBLANE
NEG = -1e30
ATTN_IN_PAD = 3 * D_MODEL + LANE
SSM_IN_PAD = 2 * SSM_D_INNER + 2 * SSM_GROUPS * SSM_STATE + LANE
PACK_COLS = 1024
PACK_ROW_ALIGN = 512

ATTN_BLOCK = 512
ROW_BLOCK = 256
CUM_BLOCK = 256


def _params(sem, vmem_mb=48):
    return pltpu.CompilerParams(dimension_semantics=sem, vmem_limit_bytes=vmem_mb * 2 ** 20)


def _pick(n, target, mult=LANE):
    best = None
    d = mult
    while d <= min(n, target):
        if n % d == 0:
            best = d
        d += mult
    return n if best is None else best


def _sigmoid(x):
    return 1.0 / (1.0 + jnp.exp(-x))


def _log1p(u):
    w = 1.0 + u
    return jnp.where(w == 1.0, u, jnp.log(w) * (u / (w - 1.0)))


def _softplus(x):
    return jnp.maximum(x, 0.0) + _log1p(jnp.exp(-jnp.abs(x)))


def _split3(x):
    hi = x.astype(BF16)
    r1 = x - hi.astype(F32)
    mid = r1.astype(BF16)
    lo = (r1 - mid.astype(F32)).astype(BF16)
    return hi, mid, lo


def _tri_matmul(tri, x):
    out = None
    for part in _split3(x):
        t = jnp.dot(tri, part, preferred_element_type=F32)
        out = t if out is None else out + t
    return out


def _tri(n, lower):
    r = lax.broadcasted_iota(jnp.int32, (n, n), 0)
    c = lax.broadcasted_iota(jnp.int32, (n, n), 1)
    return jnp.where((c <= r) if lower else (c >= r), 1.0, 0.0).astype(BF16)


def _mm(a, b, *, name, ta=False, tb=False, add=None, add_scale=1.0, bm=1024, bn=1024, bk=512):
    if ta:
        K, M = a.shape
    else:
        M, K = a.shape
    if tb:
        N, Kb = b.shape
    else:
        Kb, N = b.shape
    assert K == Kb, (a.shape, b.shape, ta, tb)
    bm, bn, bk = _pick(M, bm), _pick(N, bn), _pick(K, bk)
    nk = K // bk
    a_spec = pl.BlockSpec((bk, bm), lambda i, j, k: (k, i)) if ta else pl.BlockSpec((bm, bk), lambda i, j, k: (i, k))
    b_spec = pl.BlockSpec((bn, bk), lambda i, j, k: (j, k)) if tb else pl.BlockSpec((bk, bn), lambda i, j, k: (k, j))
    o_spec = pl.BlockSpec((bm, bn), lambda i, j, k: (i, j))
    dims = (((0 if ta else 1,), (1 if tb else 0,)), ((), ()))
    has_add = add is not None

    def kern(*refs):
        if has_add:
            a_ref, b_ref, add_ref, o_ref, acc_ref = refs
        else:
            a_ref, b_ref, o_ref, acc_ref = refs
        k = pl.program_id(2)

        @pl.when(k == 0)
        def _():
            acc_ref[...] = jnp.zeros_like(acc_ref)

        acc_ref[...] += lax.dot_general(a_ref[...].astype(BF16), b_ref[...].astype(BF16), dims,
                                        preferred_element_type=F32)

        @pl.when(k == nk - 1)
        def _():
            r = acc_ref[...]
            if has_add:
                r = r + add_scale * add_ref[...]
            o_ref[...] = r

    ins = [a, b] + ([add] if has_add else [])
    in_specs = [a_spec, b_spec] + ([o_spec] if has_add else [])
    return pl.pallas_call(
        kern, name=name, grid=(M // bm, N // bn, nk),
        in_specs=in_specs, out_specs=o_spec,
        out_shape=jax.ShapeDtypeStruct((M, N), F32),
        scratch_shapes=[pltpu.VMEM((bm, bn), F32)],
        compiler_params=_params(("parallel", "parallel", "arbitrary")),
    )(*ins)


def _ln_stats(z):
    mu = jnp.mean(z, axis=-1, keepdims=True)
    zc = z - mu
    var = jnp.mean(zc * zc, axis=-1, keepdims=True)
    return zc, lax.rsqrt(var + LN_EPS)


def _ln_fwd(x, r, g, b, *, name):
    T, D = x.shape
    bt = _pick(T, ROW_BLOCK, SUBLANE)

    def kern(x_ref, r_ref, g_ref, b_ref, z_ref, h_ref):
        z = DEEPNORM_ALPHA * x_ref[...] + r_ref[...]
        zc, rstd = _ln_stats(z)
        z_ref[...] = z
        h_ref[...] = zc * rstd * g_ref[...] + b_ref[...]

    row = pl.BlockSpec((bt, D), lambda i: (i, 0))
    vec = pl.BlockSpec((1, D), lambda i: (0, 0))
    return pl.pallas_call(
        kern, name=name, grid=(T // bt,), in_specs=[row, row, vec, vec], out_specs=[row, row],
        out_shape=[jax.ShapeDtypeStruct((T, D), F32)] * 2,
        compiler_params=_params(("parallel",)),
    )(x, r, g, b)


def _ln_bwd(dy, z, g, *, name):
    T, D = z.shape
    bt = _pick(T, ROW_BLOCK, SUBLANE)

    def kern(dy_ref, z_ref, g_ref, dz_ref, dg_ref, db_ref):
        i = pl.program_id(0)
        zc, rstd = _ln_stats(z_ref[...])
        xhat = zc * rstd
        dyv = dy_ref[...]
        dxh = dyv * g_ref[...]
        m1 = jnp.mean(dxh, axis=-1, keepdims=True)
        m2 = jnp.mean(dxh * xhat, axis=-1, keepdims=True)
        dz_ref[...] = rstd * (dxh - m1 - xhat * m2)

        @pl.when(i == 0)
        def _():
            dg_ref[...] = jnp.zeros_like(dg_ref)
            db_ref[...] = jnp.zeros_like(db_ref)

        dg_ref[...] += jnp.sum(dyv * xhat, axis=0, keepdims=True)
        db_ref[...] += jnp.sum(dyv, axis=0, keepdims=True)

    row = pl.BlockSpec((bt, D), lambda i: (i, 0))
    vec = pl.BlockSpec((1, D), lambda i: (0, 0))
    return pl.pallas_call(
        kern, name=name, grid=(T // bt,), in_specs=[row, row, vec], out_specs=[row, vec, vec],
        out_shape=[jax.ShapeDtypeStruct((T, D), F32), jax.ShapeDtypeStruct((1, D), F32),
                   jax.ShapeDtypeStruct((1, D), F32)],
        compiler_params=_params(("arbitrary",)),
    )(dy, z, g)


def _conv_past(ext_ref, cw_ref, K, bt):
    out = None
    for k in range(K):
        term = cw_ref[k:k + 1, :] * ext_ref[pl.ds(HALO - (K - 1) + k, bt), :]
        out = term if out is None else out + term
    return out


def _fill_ext_past(ext_ref, halo_ref, cur, i, bt):
    ext_ref[pl.ds(0, HALO), :] = jnp.where(i > 0, halo_ref[...], 0.0)
    ext_ref[pl.ds(HALO, bt), :] = cur


def _halo_prev(bt, bc, off):
    return pl.BlockSpec((HALO, bc), lambda i, j: (jnp.maximum(i * (bt // HALO) - 1, 0), j + off))


def _gelu(x):
    return 0.5 * x * (1.0 + lax.erf(x * (1.0 / math.sqrt(2.0))))


def _gelu_grad(x):
    return 0.5 * (1.0 + lax.erf(x * (1.0 / math.sqrt(2.0)))) + x * jnp.exp(-0.5 * x * x) * (1.0 / math.sqrt(2.0 * math.pi))


def _ffn_act_fwd(up, cw, cb, *, name):
    T, F2 = up.shape
    F = F2 // 2
    bt = _pick(T, ROW_BLOCK, SUBLANE)
    bc = _pick(F, 1408)
    nb = F // bc

    def kern(u_ref, g_ref, halo_ref, cw_ref, cb_ref, a_ref, ext_ref):
        i = pl.program_id(0)
        _fill_ext_past(ext_ref, halo_ref, g_ref[...], i, bt)
        gc = cb_ref[...] + _conv_past(ext_ref, cw_ref, FFN_CONV, bt)
        a_ref[...] = _gelu(gc) * u_ref[...]

    return pl.pallas_call(
        kern, name=name, grid=(T // bt, nb),
        in_specs=[pl.BlockSpec((bt, bc), lambda i, j: (i, j)),
                  pl.BlockSpec((bt, bc), lambda i, j: (i, j + nb)),
                  _halo_prev(bt, bc, nb),
                  pl.BlockSpec((FFN_CONV, bc), lambda i, j: (0, j)),
                  pl.BlockSpec((1, bc), lambda i, j: (0, j))],
        out_specs=pl.BlockSpec((bt, bc), lambda i, j: (i, j)),
        out_shape=jax.ShapeDtypeStruct((T, F), F32),
        scratch_shapes=[pltpu.VMEM((bt + HALO, bc), F32)],
        compiler_params=_params(("parallel", "parallel")),
    )(up, up, up, cw, cb)


def _ffn_act_bwd(da, up, cw, cb, *, name):
    T, F2 = up.shape
    F = F2 // 2
    bt = _pick(T, ROW_BLOCK, SUBLANE)
    bc = _pick(F, 1408)
    nb = F // bc
    K = FFN_CONV

    def kern(da_ref, u_ref, g_ref, halo_ref, cw_ref, cb_ref, du_ref, dgc_ref, dcb_ref, dcw_ref, ext_ref):
        i = pl.program_id(1)
        _fill_ext_past(ext_ref, halo_ref, g_ref[...], i, bt)
        gc = cb_ref[...] + _conv_past(ext_ref, cw_ref, K, bt)
        dav = da_ref[...]
        du_ref[...] = dav * _gelu(gc)
        dgc = dav * u_ref[...] * _gelu_grad(gc)
        dgc_ref[...] = dgc

        @pl.when(i == 0)
        def _():
            dcb_ref[...] = jnp.zeros_like(dcb_ref)
            dcw_ref[...] = jnp.zeros_like(dcw_ref)

        dcb_ref[...] += jnp.sum(dgc, axis=0, keepdims=True)
        for k in range(K):
            dcw_ref[k:k + 1, :] += jnp.sum(dgc * ext_ref[pl.ds(HALO - (K - 1) + k, bt), :], axis=0, keepdims=True)

    blk = pl.BlockSpec((bt, bc), lambda j, i: (i, j))
    return pl.pallas_call(
        kern, name=name, grid=(nb, T // bt),
        in_specs=[blk, blk,
                  pl.BlockSpec((bt, bc), lambda j, i: (i, j + nb)),
                  pl.BlockSpec((HALO, bc), lambda j, i: (jnp.maximum(i * (bt // HALO) - 1, 0), j + nb)),
                  pl.BlockSpec((K, bc), lambda j, i: (0, j)),
                  pl.BlockSpec((1, bc), lambda j, i: (0, j))],
        out_specs=[blk, blk, pl.BlockSpec((1, bc), lambda j, i: (0, j)), pl.BlockSpec((K, bc), lambda j, i: (0, j))],
        out_shape=[jax.ShapeDtypeStruct((T, F), F32), jax.ShapeDtypeStruct((T, F), F32),
                   jax.ShapeDtypeStruct((1, F), F32), jax.ShapeDtypeStruct((K, F), F32)],
        scratch_shapes=[pltpu.VMEM((bt + HALO, bc), F32)],
        compiler_params=_params(("parallel", "arbitrary")),
    )(da, up, up, up, cw, cb)


def _dwconv_bwd_data(dgc, cw, K, *, name):
    T, C = dgc.shape
    bt = _pick(T, ROW_BLOCK, SUBLANE)
    bc = _pick(C, 1408)
    nt = T // bt
    last_halo = T // HALO - 1

    def kern(d_ref, halo_ref, cw_ref, o_ref, ext_ref):
        i = pl.program_id(0)
        ext_ref[pl.ds(0, bt), :] = d_ref[...]
        ext_ref[pl.ds(bt, HALO), :] = jnp.where(i < nt - 1, halo_ref[...], 0.0)
        out = None
        for k in range(K):
            term = cw_ref[k:k + 1, :] * ext_ref[pl.ds(K - 1 - k, bt), :]
            out = term if out is None else out + term
        o_ref[...] = out

    return pl.pallas_call(
        kern, name=name, grid=(nt, C // bc),
        in_specs=[pl.BlockSpec((bt, bc), lambda i, j: (i, j)),
                  pl.BlockSpec((HALO, bc), lambda i, j: (jnp.minimum((i + 1) * (bt // HALO), last_halo), j)),
                  pl.BlockSpec((K, bc), lambda i, j: (0, j))],
        out_specs=pl.BlockSpec((bt, bc), lambda i, j: (i, j)),
        out_shape=jax.ShapeDtypeStruct((T, C), F32),
        scratch_shapes=[pltpu.VMEM((bt + HALO, bc), F32)],
        compiler_params=_params(("parallel", "parallel")),
    )(dgc, dgc, cw)


def _ple_fwd(h, G, bg, pp, *, name):
    T, D = h.shape
    bt = _pick(T, ROW_BLOCK, SUBLANE)

    def kern(h_ref, G_ref, bg_ref, pp_ref, o_ref):
        o_ref[...] = h_ref[...] + _sigmoid(G_ref[...] + bg_ref[...]) * pp_ref[...]

    row = pl.BlockSpec((bt, D), lambda i: (i, 0))
    vec = pl.BlockSpec((1, D), lambda i: (0, 0))
    return pl.pallas_call(
        kern, name=name, grid=(T // bt,), in_specs=[row, row, vec, row], out_specs=row,
        out_shape=jax.ShapeDtypeStruct((T, D), F32), compiler_params=_params(("parallel",)),
    )(h, G, bg, pp)


def _ple_bwd(dx, G, bg, pp, *, name):
    T, D = dx.shape
    bt = _pick(T, ROW_BLOCK, SUBLANE)

    def kern(dx_ref, G_ref, bg_ref, pp_ref, dG_ref, dpp_ref, dbg_ref):
        i = pl.program_id(0)
        gate = _sigmoid(G_ref[...] + bg_ref[...])
        dxv = dx_ref[...]
        dG = dxv * pp_ref[...] * gate * (1.0 - gate)
        dG_ref[...] = dG
        dpp_ref[...] = dxv * gate

        @pl.when(i == 0)
        def _():
            dbg_ref[...] = jnp.zeros_like(dbg_ref)

        dbg_ref[...] += jnp.sum(dG, axis=0, keepdims=True)

    row = pl.BlockSpec((bt, D), lambda i: (i, 0))
    vec = pl.BlockSpec((1, D), lambda i: (0, 0))
    return pl.pallas_call(
        kern, name=name, grid=(T // bt,), in_specs=[row, row, vec, row], out_specs=[row, row, vec],
        out_shape=[jax.ShapeDtypeStruct((T, D), F32), jax.ShapeDtypeStruct((T, D), F32),
                   jax.ShapeDtypeStruct((1, D), F32)],
        compiler_params=_params(("arbitrary",)),
    )(dx, G, bg, pp)


def _fox_gate_fwd(proj, bf, *, name):
    T = proj.shape[0]
    bt = _pick(T, CUM_BLOCK, SUBLANE)
    fcol = 3 * D_MODEL // LANE

    def kern(f_ref, bf_ref, c_ref, carry_ref):
        i = pl.program_id(0)

        @pl.when(i == 0)
        def _():
            carry_ref[...] = jnp.zeros_like(carry_ref)

        x = f_ref[...] + bf_ref[...]
        lf = jnp.minimum(x, 0.0) - _log1p(jnp.exp(-jnp.abs(x)))
        cs = _tri_matmul(_tri(bt, True), lf) + carry_ref[...]
        c_ref[...] = cs
        carry_ref[...] = cs[bt - 1:bt, :]

    return pl.pallas_call(
        kern, name=name, grid=(T // bt,),
        in_specs=[pl.BlockSpec((bt, LANE), lambda i: (i, fcol)), pl.BlockSpec((1, LANE), lambda i: (0, 0))],
        out_specs=pl.BlockSpec((bt, LANE), lambda i: (i, 0)),
        out_shape=jax.ShapeDtypeStruct((T, LANE), F32),
        scratch_shapes=[pltpu.VMEM((1, LANE), F32)],
        compiler_params=_params(("arbitrary",)),
    )(proj, bf)


def _fox_gate_bwd(dc, proj, bf, *, name):
    T = proj.shape[0]
    bt = _pick(T, CUM_BLOCK, SUBLANE)
    nb = T // bt
    fcol = 3 * D_MODEL // LANE

    def kern(dc_ref, f_ref, bf_ref, df_ref, dbf_ref, carry_ref):
        i = pl.program_id(0)

        @pl.when(i == 0)
        def _():
            carry_ref[...] = jnp.zeros_like(carry_ref)
            dbf_ref[...] = jnp.zeros_like(dbf_ref)

        dlf = _tri_matmul(_tri(bt, False), dc_ref[...]) + carry_ref[...]
        carry_ref[...] = dlf[0:1, :]
        x = f_ref[...] + bf_ref[...]
        lane = lax.broadcasted_iota(jnp.int32, (bt, LANE), 1)
        df = jnp.where(lane < ATTN_HEADS, dlf / (1.0 + jnp.exp(x)), 0.0)
        df_ref[...] = df
        dbf_ref[...] += jnp.sum(df, axis=0, keepdims=True)

    return pl.pallas_call(
        kern, name=name, grid=(nb,),
        in_specs=[pl.BlockSpec((bt, LANE), lambda i: (nb - 1 - i, 0)),
                  pl.BlockSpec((bt, LANE), lambda i: (nb - 1 - i, fcol)),
                  pl.BlockSpec((1, LANE), lambda i: (0, 0))],
        out_specs=[pl.BlockSpec((bt, LANE), lambda i: (nb - 1 - i, 0)), pl.BlockSpec((1, LANE), lambda i: (0, 0))],
        out_shape=[jax.ShapeDtypeStruct((T, LANE), F32), jax.ShapeDtypeStruct((1, LANE), F32)],
        scratch_shapes=[pltpu.VMEM((1, LANE), F32)],
        compiler_params=_params(("arbitrary",)),
    )(dc, proj, bf)


_NT = (((1,), (1,)), ((), ()))
_TN = (((0,), (0,)), ((), ()))


def _dot(a, b, dims=None):
    if dims is None:
        return jnp.dot(a, b, preferred_element_type=F32)
    return lax.dot_general(a, b, dims, preferred_element_type=F32)


def _attn_fwd(proj, cT, *, name):
    T = proj.shape[0]
    tb = _pick(T, ATTN_BLOCK)
    nq = T // tb
    half = LANE // 2

    def kern(q_ref, k_ref, v_ref, c_ref, o_ref, lse_ref, qm_ref, m_ref, l_ref, acc_ref):
        qi = pl.program_id(1)
        ki = pl.program_id(2)
        lo = lax.broadcasted_iota(jnp.int32, (tb, LANE), 1) < half

        @pl.when(ki == 0)
        def _():
            q = q_ref[...] * 0.125
            qm_ref[0] = jnp.where(lo, q, 0.0).astype(BF16)
            qm_ref[1] = jnp.where(lo, 0.0, q).astype(BF16)
            m_ref[...] = jnp.full_like(m_ref, NEG)
            l_ref[...] = jnp.zeros_like(l_ref)
            acc_ref[...] = jnp.zeros_like(acc_ref)

        def step(diag):
            kb = k_ref[...].astype(BF16)
            v = v_ref[...]
            alphas = []
            pv = None
            for h in range(2):
                vm = jnp.where(lo, v, 0.0) if h == 0 else jnp.where(lo, 0.0, v)
                s = _dot(qm_ref[h], kb, _NT) - c_ref[h:h + 1, :]
                if diag:
                    r = lax.broadcasted_iota(jnp.int32, (tb, tb), 0)
                    c = lax.broadcasted_iota(jnp.int32, (tb, tb), 1)
                    s = jnp.where(c <= r, s, NEG)
                m_prev = m_ref[h]
                m_new = jnp.maximum(m_prev, jnp.max(s, axis=1, keepdims=True))
                p = jnp.exp(s - m_new)
                alpha = jnp.exp(m_prev - m_new)
                l_ref[h] = alpha * l_ref[h] + jnp.sum(p, axis=1, keepdims=True)
                m_ref[h] = m_new
                alphas.append(alpha)
                t = _dot(p.astype(BF16), vm.astype(BF16))
                pv = t if pv is None else pv + t
            acc_ref[...] = acc_ref[...] * jnp.where(lo, alphas[0], alphas[1]) + pv

        @pl.when(ki < qi)
        def _():
            step(False)

        @pl.when(ki == qi)
        def _():
            step(True)
            l0, l1 = l_ref[0], l_ref[1]
            o_ref[...] = acc_ref[...] * jnp.where(lo, 1.0 / l0, 1.0 / l1)
            lse_ref[...] = jnp.where(lo, m_ref[0] + jnp.log(l0), m_ref[1] + jnp.log(l1))

    kcol, vcol = D_MODEL // LANE, 2 * D_MODEL // LANE
    qspec = pl.BlockSpec((tb, LANE), lambda hp, qi, ki: (qi, hp))
    return pl.pallas_call(
        kern, name=name, grid=(HEAD_PAIRS, nq, nq),
        in_specs=[qspec,
                  pl.BlockSpec((tb, LANE), lambda hp, qi, ki: (jnp.minimum(ki, qi), kcol + hp)),
                  pl.BlockSpec((tb, LANE), lambda hp, qi, ki: (jnp.minimum(ki, qi), vcol + hp)),
                  pl.BlockSpec((None, 2, tb), lambda hp, qi, ki: (hp, 0, jnp.minimum(ki, qi)))],
        out_specs=[qspec, qspec],
        out_shape=[jax.ShapeDtypeStruct((T, D_MODEL), F32)] * 2,
        scratch_shapes=[pltpu.VMEM((2, tb, LANE), BF16), pltpu.VMEM((2, tb, 1), F32),
                        pltpu.VMEM((2, tb, 1), F32), pltpu.VMEM((tb, LANE), F32)],
        compiler_params=_params(("parallel", "parallel", "arbitrary")),
    )(proj, proj, proj, cT)


def _attn_bwd(proj, do, o, lse, cT, *, name):
    T = proj.shape[0]
    tb = _pick(T, ATTN_BLOCK)
    nq = T // tb
    half = LANE // 2

    def kern(q_ref, k_ref, v_ref, do_ref, o_ref, lse_ref, c_ref, dq_ref, dk_ref, dv_ref, dc_ref, dcq_ref):
        ki = pl.program_id(1)
        qi = pl.program_id(2)
        lo = lax.broadcasted_iota(jnp.int32, (tb, LANE), 1) < half

        @pl.when(jnp.logical_and(ki == 0, qi == 0))
        def _():
            dq_ref[...] = jnp.zeros_like(dq_ref)
            dcq_ref[...] = jnp.zeros_like(dcq_ref)

        @pl.when(qi == 0)
        def _():
            dk_ref[...] = jnp.zeros_like(dk_ref)
            dv_ref[...] = jnp.zeros_like(dv_ref)
            dc_ref[...] = jnp.zeros_like(dc_ref)

        def step(diag):
            q = q_ref[...] * 0.125
            k = k_ref[...]
            kb = k.astype(BF16)
            vb = v_ref[...].astype(BF16)
            dov = do_ref[...]
            prod = dov * o_ref[...]
            lse = lse_ref[...]
            dq = None
            dk = None
            dv = None
            row_sums = []
            for h in range(2):
                hm = lo if h == 0 else jnp.logical_not(lo)
                qm = jnp.where(hm, q, 0.0).astype(BF16)
                dom = jnp.where(hm, dov, 0.0).astype(BF16)
                km = jnp.where(hm, k, 0.0).astype(BF16)
                dlt = jnp.sum(jnp.where(hm, prod, 0.0), axis=1, keepdims=True)
                lse_h = lse[:, h * half:h * half + 1]
                s = _dot(qm, kb, _NT) - c_ref[h:h + 1, :]
                if diag:
                    r = lax.broadcasted_iota(jnp.int32, (tb, tb), 0)
                    c = lax.broadcasted_iota(jnp.int32, (tb, tb), 1)
                    s = jnp.where(c <= r, s, NEG)
                p = jnp.exp(s - lse_h)
                dp = _dot(dom, vb, _NT)
                ds = p * (dp - dlt)
                dc_ref[h:h + 1, :] -= jnp.sum(ds, axis=0, keepdims=True)
                row_sums.append(jnp.sum(ds, axis=1, keepdims=True))
                dsb = ds.astype(BF16)
                tv = _dot(p.astype(BF16), dom, _TN)
                tk = _dot(dsb, qm, _TN)
                tq = _dot(dsb, km)
                dv = tv if dv is None else dv + tv
                dk = tk if dk is None else dk + tk
                dq = tq if dq is None else dq + tq
            dv_ref[...] += dv
            dk_ref[...] += dk
            rows = pl.ds(pl.multiple_of(qi * tb, tb), tb)
            dq_ref[rows, :] += dq * 0.125
            dcq_ref[rows, :] += jnp.where(lo, row_sums[0], row_sums[1])

        @pl.when(qi > ki)
        def _():
            step(False)

        @pl.when(qi == ki)
        def _():
            step(True)

    kcol, vcol = D_MODEL // LANE, 2 * D_MODEL // LANE
    qspec = pl.BlockSpec((tb, LANE), lambda hp, ki, qi: (jnp.maximum(qi, ki), hp))
    kvout = pl.BlockSpec((tb, LANE), lambda hp, ki, qi: (ki, hp))
    cspec = pl.BlockSpec((None, 2, tb), lambda hp, ki, qi: (hp, 0, ki))
    qacc = pl.BlockSpec((T, LANE), lambda hp, ki, qi: (0, hp))
    return pl.pallas_call(
        kern, name=name, grid=(HEAD_PAIRS, nq, nq),
        in_specs=[qspec,
                  pl.BlockSpec((tb, LANE), lambda hp, ki, qi: (ki, kcol + hp)),
                  pl.BlockSpec((tb, LANE), lambda hp, ki, qi: (ki, vcol + hp)),
                  qspec, qspec, qspec, cspec],
        out_specs=[qacc, kvout, kvout, cspec, qacc],
        out_shape=[jax.ShapeDtypeStruct((T, D_MODEL), F32)] * 3 + [jax.ShapeDtypeStruct((HEAD_PAIRS, 2, T), F32),
                                                                   jax.ShapeDtypeStruct((T, D_MODEL), F32)],
        compiler_params=_params(("parallel", "arbitrary", "arbitrary"), vmem_mb=56),
    )(proj, proj, proj, do, o, lse, cT)


def _ssd_dt_fwd(proj, dt_bias, a_log, *, name):
    T = proj.shape[0]
    Q = SSM_CHUNK
    col = (2 * SSM_D_INNER + 2 * SSM_GROUPS * SSM_STATE) // LANE

    def kern(raw_ref, b_ref, al_ref, dt_ref, ac_ref):
        dt = _softplus(raw_ref[...] + b_ref[...])
        dt_ref[...] = dt
        ac_ref[...] = _tri_matmul(_tri(Q, True), dt * (-jnp.exp(al_ref[...])))

    vec = pl.BlockSpec((1, LANE), lambda i: (0, 0))
    blk = pl.BlockSpec((Q, LANE), lambda i: (i, 0))
    return pl.pallas_call(
        kern, name=name, grid=(T // Q,),
        in_specs=[pl.BlockSpec((Q, LANE), lambda i: (i, col)), vec, vec], out_specs=[blk, blk],
        out_shape=[jax.ShapeDtypeStruct((T, LANE), F32)] * 2,
        compiler_params=_params(("parallel",)),
    )(proj, dt_bias, a_log)


def _ssd_dt_bwd(da_a, da_b, ddt_a, ddt_b, dt, proj, dt_bias, a_log, *, name):
    T = proj.shape[0]
    Q = SSM_CHUNK
    col = (2 * SSM_D_INNER + 2 * SSM_GROUPS * SSM_STATE) // LANE

    def kern(daa_ref, dab_ref, dda_ref, ddb_ref, dt_ref, raw_ref, b_ref, al_ref, draw_ref, dal_ref, db_ref, acc_ref):
        i = pl.program_id(0)

        @pl.when(i == 0)
        def _():
            acc_ref[...] = jnp.zeros_like(acc_ref)
            db_ref[...] = jnp.zeros_like(db_ref)

        A = -jnp.exp(al_ref[...])
        ddA = _tri_matmul(_tri(Q, False), daa_ref[...] + dab_ref[...])
        ddt = dda_ref[...] + ddb_ref[...] + ddA * A
        acc_ref[...] += jnp.sum(ddA * dt_ref[...], axis=0, keepdims=True)
        lane = lax.broadcasted_iota(jnp.int32, (Q, LANE), 1)
        draw = jnp.where(lane < SSM_HEADS, ddt * _sigmoid(raw_ref[...] + b_ref[...]), 0.0)
        draw_ref[...] = draw
        db_ref[...] += jnp.sum(draw, axis=0, keepdims=True)
        dal_ref[...] = acc_ref[...] * A

    vec = pl.BlockSpec((1, LANE), lambda i: (0, 0))
    blk = pl.BlockSpec((Q, LANE), lambda i: (i, 0))
    return pl.pallas_call(
        kern, name=name, grid=(T // Q,),
        in_specs=[blk, blk, blk, blk, blk, pl.BlockSpec((Q, LANE), lambda i: (i, col)), vec, vec],
        out_specs=[blk, vec, vec],
        out_shape=[jax.ShapeDtypeStruct((T, LANE), F32), jax.ShapeDtypeStruct((1, LANE), F32),
                   jax.ShapeDtypeStruct((1, LANE), F32)],
        scratch_shapes=[pltpu.VMEM((1, LANE), F32)],
        compiler_params=_params(("arbitrary",)),
    )(da_a, da_b, ddt_a, ddt_b, dt, proj, dt_bias, a_log)


def _conv_silu_fwd(proj, cw, cb, *, name):
    T = proj.shape[0]
    C = SSM_XBC
    bt = _pick(T, ROW_BLOCK, SUBLANE)
    bc = 1024
    off = SSM_D_INNER // bc

    def kern(x_ref, halo_ref, cw_ref, cb_ref, o_ref, ext_ref):
        i = pl.program_id(0)
        _fill_ext_past(ext_ref, halo_ref, x_ref[...], i, bt)
        pre = cb_ref[...] + _conv_past(ext_ref, cw_ref, SSM_CONV, bt)
        o_ref[...] = pre * _sigmoid(pre)

    return pl.pallas_call(
        kern, name=name, grid=(T // bt, C // bc),
        in_specs=[pl.BlockSpec((bt, bc), lambda i, j: (i, j + off)), _halo_prev(bt, bc, off),
                  pl.BlockSpec((SSM_CONV, bc), lambda i, j: (0, j)), pl.BlockSpec((1, bc), lambda i, j: (0, j))],
        out_specs=pl.BlockSpec((bt, bc), lambda i, j: (i, j)),
        out_shape=jax.ShapeDtypeStruct((T, C), F32),
        scratch_shapes=[pltpu.VMEM((bt + HALO, bc), F32)],
        compiler_params=_params(("parallel", "parallel")),
    )(proj, proj, cw, cb)


def _conv_silu_bwd(dxbc, proj, cw, cb, *, name):
    T = proj.shape[0]
    C = SSM_XBC
    K = SSM_CONV
    bt = _pick(T, ROW_BLOCK, SUBLANE)
    bc = 1024
    off = SSM_D_INNER // bc

    def kern(d_ref, x_ref, halo_ref, cw_ref, cb_ref, dpre_ref, dcb_ref, dcw_ref, ext_ref):
        i = pl.program_id(1)
        _fill_ext_past(ext_ref, halo_ref, x_ref[...], i, bt)
        pre = cb_ref[...] + _conv_past(ext_ref, cw_ref, K, bt)
        sg = _sigmoid(pre)
        dpre = d_ref[...] * sg * (1.0 + pre * (1.0 - sg))
        dpre_ref[...] = dpre

        @pl.when(i == 0)
        def _():
            dcb_ref[...] = jnp.zeros_like(dcb_ref)
            dcw_ref[...] = jnp.zeros_like(dcw_ref)

        dcb_ref[...] += jnp.sum(dpre, axis=0, keepdims=True)
        for k in range(K):
            dcw_ref[k:k + 1, :] += jnp.sum(dpre * ext_ref[pl.ds(HALO - (K - 1) + k, bt), :], axis=0, keepdims=True)

    blk = pl.BlockSpec((bt, bc), lambda j, i: (i, j))
    return pl.pallas_call(
        kern, name=name, grid=(C // bc, T // bt),
        in_specs=[blk, pl.BlockSpec((bt, bc), lambda j, i: (i, j + off)),
                  pl.BlockSpec((HALO, bc), lambda j, i: (jnp.maximum(i * (bt // HALO) - 1, 0), j + off)),
                  pl.BlockSpec((K, bc), lambda j, i: (0, j)), pl.BlockSpec((1, bc), lambda j, i: (0, j))],
        out_specs=[blk, pl.BlockSpec((1, bc), lambda j, i: (0, j)), pl.BlockSpec((K, bc), lambda j, i: (0, j))],
        out_shape=[jax.ShapeDtypeStruct((T, C), F32), jax.ShapeDtypeStruct((1, C), F32),
                   jax.ShapeDtypeStruct((K, C), F32)],
        scratch_shapes=[pltpu.VMEM((bt + HALO, bc), F32)],
        compiler_params=_params(("parallel", "arbitrary")),
    )(dxbc, proj, proj, cw, cb)


_GP = SSM_D_INNER // SSM_GROUPS
_HPG = SSM_HEADS // SSM_GROUPS
_PH = SSM_D_INNER // SSM_HEADS


def _head_masks(rows):
    lane = lax.broadcasted_iota(jnp.int32, (rows, _GP), 1)
    return [jnp.logical_and(lane >= r * _PH, lane < (r + 1) * _PH) for r in range(_HPG)]


def _ssd_specs(idx):
    Q, N = SSM_CHUNK, SSM_STATE
    bcol, ccol = SSM_D_INNER // N, SSM_D_INNER // N + SSM_GROUPS
    return dict(
        x=pl.BlockSpec((Q, _GP), lambda j, g: (idx(j), g)),
        B=pl.BlockSpec((Q, N), lambda j, g: (idx(j), bcol + g)),
        C=pl.BlockSpec((Q, N), lambda j, g: (idx(j), ccol + g)),
        col=pl.BlockSpec((Q, LANE), lambda j, g: (idx(j), g)),
        row=pl.BlockSpec((None, _HPG, Q), lambda j, g: (g, 0, idx(j))),
        st=pl.BlockSpec((N, _GP), lambda j, g: (idx(j), g)),
    )


def _ssd_scan_fwd(xbc, dtc, acc_, dtr, acr, *, name):
    T = xbc.shape[0]
    Q, N = SSM_CHUNK, SSM_STATE
    nc = T // Q
    sp = _ssd_specs(lambda j: j)

    def kern(x_ref, B_ref, C_ref, dtc_ref, ac_ref, dtr_ref, ar_ref, ys_ref, st_ref, state_ref):
        j = pl.program_id(0)
        g = pl.program_id(1)

        @pl.when(j == 0)
        def _():
            state_ref[g] = jnp.zeros((N, _GP), F32)

        S = state_ref[g]
        st_ref[...] = S
        x = x_ref[...]
        xb = x.astype(BF16)
        Bb = B_ref[...].astype(BF16)
        Cb = C_ref[...].astype(BF16)
        CB = _dot(Cb, Bb, _NT)
        r_i = lax.broadcasted_iota(jnp.int32, (Q, Q), 0)
        c_i = lax.broadcasted_iota(jnp.int32, (Q, Q), 1)
        tri = c_i <= r_i
        masks = _head_masks(Q)
        masks1 = _head_masks(1)
        y = jnp.zeros((Q, _GP), F32)
        El = jnp.zeros((Q, _GP), F32)
        Wl = jnp.zeros((Q, _GP), F32)
        decl = jnp.zeros((1, _GP), F32)
        for r in range(_HPG):
            a_c = ac_ref[:, r:r + 1]
            a_r = ar_ref[r:r + 1, :]
            dt_c = dtc_ref[:, r:r + 1]
            dt_r = dtr_ref[r:r + 1, :]
            L = jnp.exp(jnp.where(tri, a_c - a_r, NEG))
            W = CB * L * dt_r
            y = jnp.where(masks[r], _dot(W.astype(BF16), xb), y)
            a_q = a_c[Q - 1:Q, :]
            El = jnp.where(masks[r], jnp.exp(a_c), El)
            Wl = jnp.where(masks[r], jnp.exp(a_q - a_c) * dt_c, Wl)
            decl = jnp.where(masks1[r], jnp.exp(a_q), decl)
        ys_ref[...] = y + _dot(Cb, S.astype(BF16)) * El
        state_ref[g] = S * decl + _dot(Bb, (x * Wl).astype(BF16), _TN)

    return pl.pallas_call(
        kern, name=name, grid=(nc, SSM_GROUPS),
        in_specs=[sp["x"], sp["B"], sp["C"], sp["col"], sp["col"], sp["row"], sp["row"]],
        out_specs=[sp["x"], sp["st"]],
        out_shape=[jax.ShapeDtypeStruct((T, SSM_D_INNER), F32), jax.ShapeDtypeStruct((nc * N, SSM_D_INNER), F32)],
        scratch_shapes=[pltpu.VMEM((SSM_GROUPS, N, _GP), F32)],
        compiler_params=_params(("arbitrary", "arbitrary")),
    )(xbc, xbc, xbc, dtc, acc_, dtr, acr)


def _ssd_scan_bwd(xbc, dys, dskip, st, dtc, acc_, dtr, acr, *, name):
    T = xbc.shape[0]
    Q, N = SSM_CHUNK, SSM_STATE
    nc = T // Q
    sp = _ssd_specs(lambda j: nc - 1 - j)

    def kern(x_ref, B_ref, C_ref, dy_ref, dsk_ref, st_ref, dtc_ref, ac_ref, dtr_ref, ar_ref,
             dx_ref, dB_ref, dC_ref, dac_ref, dar_ref, ddc_ref, ddr_ref, dstate_ref):
        j = pl.program_id(0)
        g = pl.program_id(1)

        @pl.when(j == 0)
        def _():
            dstate_ref[g] = jnp.zeros((N, _GP), F32)

        dS = dstate_ref[g]
        dSb = dS.astype(BF16)
        S = st_ref[...]
        Sb = S.astype(BF16)
        x = x_ref[...]
        xb = x.astype(BF16)
        Bb = B_ref[...].astype(BF16)
        Cb = C_ref[...].astype(BF16)
        dy = dy_ref[...]
        CB = _dot(Cb, Bb, _NT)
        BdS = _dot(Bb, dSb)
        hx = BdS * x
        yd = _dot(Cb, Sb) * dy
        dSS = dS * S
        r_i = lax.broadcasted_iota(jnp.int32, (Q, Q), 0)
        c_i = lax.broadcasted_iota(jnp.int32, (Q, Q), 1)
        tri = c_i <= r_i
        last_row = lax.broadcasted_iota(jnp.int32, (Q, 1), 0) == Q - 1
        lane128 = lax.broadcasted_iota(jnp.int32, (Q, LANE), 1)
        masks = _head_masks(Q)
        masksN = _head_masks(N)
        masks1 = _head_masks(1)
        zeros = jnp.zeros((Q, _GP), F32)
        dxi, El, Wl = zeros, zeros, zeros
        decl = jnp.zeros((1, _GP), F32)
        dBacc = jnp.zeros((Q, N), F32)
        dCacc = jnp.zeros((Q, N), F32)
        dacol = jnp.zeros((Q, LANE), F32)
        ddcol = jnp.zeros((Q, LANE), F32)
        for r in range(_HPG):
            hm = masks[r]
            a_c = ac_ref[:, r:r + 1]
            a_r = ar_ref[r:r + 1, :]
            dt_c = dtc_ref[:, r:r + 1]
            dt_r = dtr_ref[r:r + 1, :]
            L = jnp.exp(jnp.where(tri, a_c - a_r, NEG))
            GL = CB * L
            W = GL * dt_r
            dym = jnp.where(hm, dy, 0.0).astype(BF16)
            dW = _dot(dym, xb, _NT)
            E = dW * W
            da_c = jnp.sum(E, axis=1, keepdims=True)
            dar_ref[r:r + 1, :] = -jnp.sum(E, axis=0, keepdims=True)
            ddr_ref[r:r + 1, :] = jnp.sum(dW * GL, axis=0, keepdims=True)
            dGb = (dW * L * dt_r).astype(BF16)
            dCacc = dCacc + _dot(dGb, Bb)
            dBacc = dBacc + _dot(dGb, Cb, _TN)
            dxi = dxi + _dot(W.astype(BF16), dym, _TN)
            a_q = a_c[Q - 1:Q, :]
            e_c = jnp.exp(a_c)
            eq_c = jnp.exp(a_q - a_c)
            w_c = eq_c * dt_c
            ydr = jnp.sum(jnp.where(hm, yd, 0.0), axis=1, keepdims=True) * e_c
            h_c = jnp.sum(jnp.where(hm, hx, 0.0), axis=1, keepdims=True)
            hw = h_c * w_c
            dss = jnp.sum(jnp.sum(jnp.where(masksN[r], dSS, 0.0), axis=1, keepdims=True), axis=0, keepdims=True)
            s_q = jnp.sum(hw, axis=0, keepdims=True) + jnp.exp(a_q) * dss
            da_c = da_c + ydr - hw + jnp.where(last_row, s_q, 0.0)
            dacol = jnp.where(lane128 == r, da_c, dacol)
            ddcol = jnp.where(lane128 == r, h_c * eq_c, ddcol)
            El = jnp.where(hm, e_c, El)
            Wl = jnp.where(hm, w_c, Wl)
            decl = jnp.where(masks1[r], jnp.exp(a_q), decl)
        dx_ref[...] = dxi + BdS * Wl + dsk_ref[...]
        dB_ref[...] = dBacc + _dot((x * Wl).astype(BF16), dSb, _NT)
        dyE = (dy * El).astype(BF16)
        dC_ref[...] = dCacc + _dot(dyE, Sb, _NT)
        dac_ref[...] = dacol
        ddc_ref[...] = ddcol
        dstate_ref[g] = dS * decl + _dot(Cb, dyE, _TN)

    idx = lambda j: nc - 1 - j
    bcblk = pl.BlockSpec((Q, N), lambda j, g: (idx(j), g))
    return pl.pallas_call(
        kern, name=name, grid=(nc, SSM_GROUPS),
        in_specs=[sp["x"], sp["B"], sp["C"], sp["x"], sp["x"], sp["st"], sp["col"], sp["col"], sp["row"], sp["row"]],
        out_specs=[sp["x"], bcblk, bcblk, sp["col"], sp["row"], sp["col"], sp["row"]],
        out_shape=[jax.ShapeDtypeStruct((T, SSM_D_INNER), F32),
                   jax.ShapeDtypeStruct((T, SSM_GROUPS * N), F32), jax.ShapeDtypeStruct((T, SSM_GROUPS * N), F32),
                   jax.ShapeDtypeStruct((T, SSM_GROUPS * LANE), F32), jax.ShapeDtypeStruct((SSM_GROUPS, _HPG, T), F32),
                   jax.ShapeDtypeStruct((T, SSM_GROUPS * LANE), F32), jax.ShapeDtypeStruct((SSM_GROUPS, _HPG, T), F32)],
        scratch_shapes=[pltpu.VMEM((SSM_GROUPS, N, _GP), F32)],
        compiler_params=_params(("arbitrary", "arbitrary")),
    )(xbc, xbc, xbc, dys, dskip, st, dtc, acc_, dtr, acr)


def _gate_norm_fwd(ys, xbc, proj, d_exp, norm_w, *, name):
    T = ys.shape[0]
    bt = _pick(T, ROW_BLOCK, SUBLANE)

    def kern(ys_ref, x_ref, z_ref, d_ref, w_ref, o_ref):
        z = z_ref[...]
        yz = (ys_ref[...] + d_ref[...] * x_ref[...]) * (z * _sigmoid(z))
        rstd = lax.rsqrt(jnp.mean(yz * yz, axis=-1, keepdims=True) + RMS_EPS)
        o_ref[...] = yz * rstd * w_ref[...]

    blk = pl.BlockSpec((bt, _GP), lambda i, g: (i, g))
    vec = pl.BlockSpec((1, _GP), lambda i, g: (0, g))
    return pl.pallas_call(
        kern, name=name, grid=(T // bt, SSM_GROUPS), in_specs=[blk, blk, blk, vec, vec], out_specs=blk,
        out_shape=jax.ShapeDtypeStruct((T, SSM_D_INNER), F32), compiler_params=_params(("parallel", "parallel")),
    )(ys, xbc, proj, d_exp, norm_w)


def _gate_norm_bwd(dyn, ys, xbc, proj, d_exp, norm_w, *, name):
    T = ys.shape[0]
    bt = _pick(T, ROW_BLOCK, SUBLANE)

    def kern(dyn_ref, ys_ref, x_ref, z_ref, d_ref, w_ref, dz_ref, dys_ref, dsk_ref, dw_ref, dd_ref):
        i = pl.program_id(1)
        z = z_ref[...]
        x = x_ref[...]
        sg = _sigmoid(z)
        sz = z * sg
        y = ys_ref[...] + d_ref[...] * x
        yz = y * sz
        rstd = lax.rsqrt(jnp.mean(yz * yz, axis=-1, keepdims=True) + RMS_EPS)
        yhat = yz * rstd
        dynv = dyn_ref[...]
        gg = dynv * w_ref[...]
        dyz = rstd * (gg - yhat * jnp.mean(gg * yhat, axis=-1, keepdims=True))
        dy = dyz * sz
        dz_ref[...] = dyz * y * sg * (1.0 + z * (1.0 - sg))
        dys_ref[...] = dy
        dsk_ref[...] = dy * d_ref[...]

        @pl.when(i == 0)
        def _():
            dw_ref[...] = jnp.zeros_like(dw_ref)
            dd_ref[...] = jnp.zeros_like(dd_ref)

        dw_ref[...] += jnp.sum(dynv * yhat, axis=0, keepdims=True)
        dd_ref[...] += jnp.sum(dy * x, axis=0, keepdims=True)

    blk = pl.BlockSpec((bt, _GP), lambda g, i: (i, g))
    vec = pl.BlockSpec((1, _GP), lambda g, i: (0, g))
    act = jax.ShapeDtypeStruct((T, SSM_D_INNER), F32)
    par = jax.ShapeDtypeStruct((1, SSM_D_INNER), F32)
    return pl.pallas_call(
        kern, name=name, grid=(SSM_GROUPS, T // bt), in_specs=[blk, blk, blk, blk, vec, vec],
        out_specs=[blk, blk, blk, vec, vec], out_shape=[act, act, act, par, par],
        compiler_params=_params(("parallel", "arbitrary")),
    )(dyn, ys, xbc, proj, d_exp, norm_w)


def _loss_head(y, target, *, name):
    T, D = y.shape
    bt = _pick(T, ROW_BLOCK, SUBLANE)

    def kern(y_ref, t_ref, l_ref, dy_ref):
        i = pl.program_id(0)
        err = y_ref[...] - t_ref[...]
        dy_ref[...] = err * (1.0 / D)

        @pl.when(i == 0)
        def _():
            l_ref[...] = jnp.zeros_like(l_ref)

        l_ref[...] += jnp.sum(err * err, axis=0, keepdims=True) * (0.5 / D)

    row = pl.BlockSpec((bt, D), lambda i: (i, 0))
    vec = pl.BlockSpec((1, D), lambda i: (0, 0))
    return pl.pallas_call(
        kern, name=name, grid=(T // bt,), in_specs=[row, row], out_specs=[vec, row],
        out_shape=[jax.ShapeDtypeStruct((1, D), F32), jax.ShapeDtypeStruct((T, D), F32)],
        compiler_params=_params(("arbitrary",)),
    )(y, target)


def _adamw(w, g, m, v, *, name):
    R, C = w.shape
    br = _pick(R, 512, SUBLANE)

    def kern(w_ref, g_ref, m_ref, v_ref, d_ref, nm_ref, nv_ref):
        gv = g_ref[...]
        nm = ADAM_B1 * m_ref[...] + (1.0 - ADAM_B1) * gv
        nv = ADAM_B2 * v_ref[...] + (1.0 - ADAM_B2) * (gv * gv)
        m_hat = nm / (1.0 - ADAM_B1 ** ADAM_STEP)
        v_hat = nv / (1.0 - ADAM_B2 ** ADAM_STEP)
        d_ref[...] = -ADAM_LR * (m_hat / (jnp.sqrt(v_hat) + ADAM_EPS) + ADAM_WD * w_ref[...])
        nm_ref[...] = nm
        nv_ref[...] = nv

    blk = pl.BlockSpec((br, C), lambda i: (i, 0))
    return pl.pallas_call(
        kern, name=name, grid=(R // br,), in_specs=[blk] * 4, out_specs=[blk] * 3,
        out_shape=[jax.ShapeDtypeStruct((R, C), F32)] * 3, compiler_params=_params(("parallel",)),
    )(w, g, m, v)


def _add2(a, b, *, name):
    shape = a.shape
    a2, b2 = a.reshape(-1, shape[-1]), b.reshape(-1, shape[-1])
    R, C = a2.shape
    br = _pick(R, 512, SUBLANE)

    def kern(a_ref, b_ref, o_ref):
        o_ref[...] = a_ref[...] + b_ref[...]

    blk = pl.BlockSpec((br, C), lambda i: (i, 0))
    return pl.pallas_call(
        kern, name=name, grid=(R // br,), in_specs=[blk, blk], out_specs=blk,
        out_shape=jax.ShapeDtypeStruct((R, C), F32), compiler_params=_params(("parallel",)),
    )(a2, b2).reshape(shape)


def _sum4(buf, *, name):
    _, R, C = buf.shape
    br = _pick(R, 512, SUBLANE)

    def kern(b_ref, o_ref):
        o_ref[...] = ((b_ref[0] + b_ref[1]) + b_ref[2]) + b_ref[3]

    return pl.pallas_call(
        kern, name=name, grid=(R // br,), in_specs=[pl.BlockSpec((4, br, C), lambda i: (0, i, 0))],
        out_specs=pl.BlockSpec((br, C), lambda i: (i, 0)),
        out_shape=jax.ShapeDtypeStruct((R, C), F32), compiler_params=_params(("parallel",)),
    )(buf)


_ANY = pl.BlockSpec(memory_space=pl.ANY)


def _place():
    x, y, c = lax.axis_index("x"), lax.axis_index("y"), lax.axis_index("c")
    other_chips = [(1 - x, y), (x, 1 - y), (1 - x, 1 - y)]
    return x, y, c, other_chips


def _gather_chips(w, *, name):
    R, C = w.shape
    H = R // 2

    def body(w_ref, out_ref, send_sems, recv_sems, local_sem):
        x, y, c, chips = _place()
        me_chip = 2 * x + y
        sib = (x, y, 1 - c)

        def rows(chip, hc):
            return out_ref.at[chip, pl.ds(hc * H, H), :]

        def copy(k, blk, to, src=None):
            return pltpu.make_async_remote_copy(
                src_ref=blk if src is None else src, dst_ref=blk, send_sem=send_sems.at[k], recv_sem=recv_sems.at[k],
                device_id=to, device_id_type=MESH)

        mine = pltpu.make_async_copy(w_ref, out_ref.at[me_chip], local_sem)
        mine.start()
        first = [copy(j, rows(me_chip, c), (cx, cy, c), src=w_ref.at[pl.ds(c * H, H), :])
                 for j, (cx, cy) in enumerate(chips)]
        for cp in first:
            cp.start()
        passed = []
        for j, (cx, cy) in enumerate(chips):
            blk = rows(2 * cx + cy, c)
            copy(j, blk, (cx, cy, c)).wait_recv()
            fw = copy(3 + j, blk, sib)
            fw.start()
            passed.append(fw)
        for j, (cx, cy) in enumerate(chips):
            copy(3 + j, rows(2 * cx + cy, 1 - c), sib).wait_recv()
        for cp in first + passed:
            cp.wait_send()
        mine.wait()

    return pl.pallas_call(
        body, name=name, in_specs=[_ANY], out_specs=_ANY,
        out_shape=jax.ShapeDtypeStruct((4, R, C), w.dtype),
        scratch_shapes=[pltpu.SemaphoreType.DMA((6,)), pltpu.SemaphoreType.DMA((6,)), pltpu.SemaphoreType.DMA],
    )(w)


def _pair_swap(v, *, name):
    def body(v_ref, out_ref, send_sem, recv_sem):
        x, y, c, _ = _place()
        cp = pltpu.make_async_remote_copy(src_ref=v_ref, dst_ref=out_ref, send_sem=send_sem, recv_sem=recv_sem,
                                          device_id=(x, y, 1 - c), device_id_type=MESH)
        cp.start()
        cp.wait()

    return pl.pallas_call(
        body, name=name, in_specs=[_ANY], out_specs=_ANY, out_shape=jax.ShapeDtypeStruct(v.shape, v.dtype),
        scratch_shapes=[pltpu.SemaphoreType.DMA, pltpu.SemaphoreType.DMA],
    )(v)


def _chip_exchange(pv, *, name):
    def body(p_ref, out_ref, send_sems, recv_sems, local_sem):
        x, y, c, chips = _place()
        me_chip = 2 * x + y
        mine = pltpu.make_async_copy(p_ref.at[me_chip], out_ref.at[me_chip], local_sem)
        mine.start()
        sends = []
        for j, (cx, cy) in enumerate(chips):
            cp = pltpu.make_async_remote_copy(
                src_ref=p_ref.at[2 * cx + cy], dst_ref=out_ref.at[me_chip], send_sem=send_sems.at[j],
                recv_sem=recv_sems.at[j], device_id=(cx, cy, c), device_id_type=MESH)
            cp.start()
            sends.append(cp)
        for j, (cx, cy) in enumerate(chips):
            blk = out_ref.at[2 * cx + cy]
            pltpu.make_async_remote_copy(src_ref=blk, dst_ref=blk, send_sem=send_sems.at[j], recv_sem=recv_sems.at[j],
                                         device_id=(cx, cy, c), device_id_type=MESH).wait_recv()
        for cp in sends:
            cp.wait_send()
        mine.wait()

    return pl.pallas_call(
        body, name=name, in_specs=[_ANY], out_specs=_ANY, out_shape=jax.ShapeDtypeStruct(pv.shape, pv.dtype),
        scratch_shapes=[pltpu.SemaphoreType.DMA((3,)), pltpu.SemaphoreType.DMA((3,)), pltpu.SemaphoreType.DMA],
    )(pv)


def _pair_gather(f, *, name):
    H, C = f.shape

    def body(f_ref, out_ref, send_sem, recv_sem, local_sem):
        x, y, c, _ = _place()
        mine_rows = out_ref.at[pl.ds(c * H, H), :]
        mine = pltpu.make_async_copy(f_ref, mine_rows, local_sem)
        mine.start()
        cp = pltpu.make_async_remote_copy(src_ref=f_ref, dst_ref=mine_rows, send_sem=send_sem, recv_sem=recv_sem,
                                          device_id=(x, y, 1 - c), device_id_type=MESH)
        cp.start()
        theirs = out_ref.at[pl.ds((1 - c) * H, H), :]
        pltpu.make_async_remote_copy(src_ref=theirs, dst_ref=theirs, send_sem=send_sem, recv_sem=recv_sem,
                                     device_id=(x, y, 1 - c), device_id_type=MESH).wait_recv()
        cp.wait_send()
        mine.wait()

    return pl.pallas_call(
        body, name=name, in_specs=[_ANY], out_specs=_ANY, out_shape=jax.ShapeDtypeStruct((2 * H, C), f.dtype),
        scratch_shapes=[pltpu.SemaphoreType.DMA, pltpu.SemaphoreType.DMA, pltpu.SemaphoreType.DMA],
    )(f)


WEIGHTS = [
    ("attn_w_in", 2), ("attn_b_f", None), ("attn_w_out", 1), ("ssm_w_in", 2), ("ssm_conv_w", 2), ("ssm_conv_b", 1),
    ("ssm_dt_bias", None), ("ssm_A_log", None), ("ssm_D", None), ("ssm_norm_w", 1), ("ssm_w_out", 1),
    ("ln_mix_g", None), ("ln_mix_b", None), ("ffn_w_up", 2), ("ffn_conv_w", 2), ("ffn_conv_b", None),
    ("ffn_w_down", 1), ("ln_ffn_g", None), ("ln_ffn_b", None), ("ple_w_proj", 2), ("ple_w_gate", 1),
    ("ple_b_gate", None),
]
N_CHIPS = 4


def _pack(arrays):
    parts = []
    total = 0
    for a in arrays:
        n = a.size
        pad = (-n) % PACK_COLS
        flat = a.reshape(-1)
        parts.append(jnp.pad(flat, (0, pad)) if pad else flat)
        total += n + pad
    rows = total // PACK_COLS
    rpad = (-rows) % PACK_ROW_ALIGN
    if rpad:
        parts.append(jnp.zeros((rpad * PACK_COLS,), F32))
    return jnp.concatenate(parts).reshape(rows + rpad, PACK_COLS)


def _unpack(buf, shapes):
    flat = buf.reshape(-1)
    out = []
    off = 0
    for s in shapes:
        n = math.prod(s)
        out.append(flat[off:off + n].reshape(s))
        off += n + ((-n) % PACK_COLS)
    return out


def _col_layout(a):
    T = a.shape[0]
    g = a[:, :SSM_HEADS].reshape(T, SSM_GROUPS, _HPG)
    return jnp.pad(g, ((0, 0), (0, 0), (0, LANE - _HPG))).reshape(T, SSM_GROUPS * LANE)


def _row_layout(a):
    T = a.shape[0]
    return a[:, :SSM_HEADS].T.reshape(SSM_GROUPS, _HPG, T)


def _from_col_layout(a):
    T = a.shape[0]
    v = a.reshape(T, SSM_GROUPS, LANE)[:, :, :_HPG].reshape(T, SSM_HEADS)
    return jnp.pad(v, ((0, 0), (0, LANE - SSM_HEADS)))


def _from_row_layout(a):
    T = a.shape[-1]
    v = a.reshape(SSM_HEADS, T).T
    return jnp.pad(v, ((0, 0), (0, LANE - SSM_HEADS)))


def _pad_lanes(v, n=LANE):
    return jnp.pad(v, (0, n - v.shape[0])).reshape(1, n)


def _local_step(x, p, target, W):
    T = x.shape[0]
    row = lambda v: v.reshape(1, -1)
    attn_in = jnp.pad(W["attn_w_in"][0], ((0, 0), (0, ATTN_IN_PAD - W["attn_w_in"].shape[2])))
    ssm_in = jnp.pad(W["ssm_w_in"][0], ((0, 0), (0, SSM_IN_PAD - W["ssm_w_in"].shape[2])))
    bf = _pad_lanes(W["attn_b_f"][0])
    dt_bias = _pad_lanes(W["ssm_dt_bias"][0])
    a_log = _pad_lanes(W["ssm_A_log"][0])
    d_exp = jnp.repeat(W["ssm_D"][0], _PH).reshape(1, SSM_D_INNER)
    norm_w = row(W["ssm_norm_w"][0])
    G = {}

    def ffn_ple_fwd(i, xin, mix, tag):
        s = {}
        s["z1"], s["h1"] = _ln_fwd(xin, mix, row(W["ln_mix_g"][i]), row(W["ln_mix_b"][i]), name=f"ln_mix_fwd{tag}")
        s["up"] = _mm(s["h1"], W["ffn_w_up"][i], name=f"ffn_up{tag}")
        s["a"] = _ffn_act_fwd(s["up"], W["ffn_conv_w"][i], row(W["ffn_conv_b"][i]), name=f"ffn_act_fwd{tag}")
        ffn = _mm(s["a"], W["ffn_w_down"][i], name=f"ffn_down{tag}")
        s["z2"], s["h2"] = _ln_fwd(s["h1"], ffn, row(W["ln_ffn_g"][i]), row(W["ln_ffn_b"][i]), name=f"ln_ffn_fwd{tag}")
        s["G"] = _mm(s["h2"], W["ple_w_gate"][i], name=f"ple_gate_mm{tag}")
        s["pp"] = _mm(p[i], W["ple_w_proj"][i], name=f"ple_proj_mm{tag}")
        out = _ple_fwd(s["h2"], s["G"], row(W["ple_b_gate"][i]), s["pp"], name=f"ple_fwd{tag}")
        return out, s

    def ffn_ple_bwd(i, dx, s, tag):
        g = {}
        dG, dpp, g["ple_b_gate"] = _ple_bwd(dx, s["G"], row(W["ple_b_gate"][i]), s["pp"], name=f"ple_bwd{tag}")
        g["ple_w_gate"] = _mm(s["h2"], dG, ta=True, name=f"ple_gate_dw{tag}")
        g["ple_w_proj"] = _mm(p[i], dpp, ta=True, name=f"ple_proj_dw{tag}")
        dh2 = _mm(dG, W["ple_w_gate"][i], tb=True, add=dx, name=f"ple_gate_dx{tag}")
        dz2, g["ln_ffn_g"], g["ln_ffn_b"] = _ln_bwd(dh2, s["z2"], row(W["ln_ffn_g"][i]), name=f"ln_ffn_bwd{tag}")
        da = _mm(dz2, W["ffn_w_down"][i], tb=True, name=f"ffn_down_dx{tag}")
        g["ffn_w_down"] = _mm(s["a"], dz2, ta=True, name=f"ffn_down_dw{tag}")
        du, dgc, g["ffn_conv_b"], g["ffn_conv_w"] = _ffn_act_bwd(
            da, s["up"], W["ffn_conv_w"][i], row(W["ffn_conv_b"][i]), name=f"ffn_act_bwd{tag}")
        dgpre = _dwconv_bwd_data(dgc, W["ffn_conv_w"][i], FFN_CONV, name=f"ffn_conv_bwd{tag}")
        dup = jnp.concatenate([du.astype(BF16), dgpre.astype(BF16)], axis=1)
        g["ffn_w_up"] = _mm(s["h1"], dup, ta=True, name=f"ffn_up_dw{tag}")
        dh1 = _mm(dup, W["ffn_w_up"][i], tb=True, add=dz2, add_scale=DEEPNORM_ALPHA, name=f"ffn_up_dx{tag}")
        dz1, g["ln_mix_g"], g["ln_mix_b"] = _ln_bwd(dh1, s["z1"], row(W["ln_mix_g"][i]), name=f"ln_mix_bwd{tag}")
        return dz1, g

    proj0 = _mm(x, attn_in, name="attn_in")
    c_col = _fox_gate_fwd(proj0, bf, name="fox_gate_fwd")
    cT = c_col[:, :ATTN_HEADS].T.reshape(HEAD_PAIRS, 2, T)
    o, lse = _attn_fwd(proj0, cT, name="attn_fwd")
    mix0 = _mm(o, W["attn_w_out"][0], name="attn_out")
    x1, s0 = ffn_ple_fwd(0, x, mix0, "0")

    proj1 = _mm(x1, ssm_in, name="ssm_in")
    dt, acum = _ssd_dt_fwd(proj1, dt_bias, a_log, name="ssd_dt_fwd")
    xbc = _conv_silu_fwd(proj1, W["ssm_conv_w"][0], row(W["ssm_conv_b"][0]), name="ssd_conv_fwd")
    dtc, acc_, dtr, acr = _col_layout(dt), _col_layout(acum), _row_layout(dt), _row_layout(acum)
    ys, states = _ssd_scan_fwd(xbc, dtc, acc_, dtr, acr, name="ssd_scan_fwd")
    yn = _gate_norm_fwd(ys, xbc, proj1, d_exp, norm_w, name="ssd_gate_norm_fwd")
    mix1 = _mm(yn, W["ssm_w_out"][0], name="ssm_out")
    x2, s1 = ffn_ple_fwd(1, x1, mix1, "1")

    lpart, dy = _loss_head(x2, target, name="loss_head")
    loss = jnp.sum(lpart)

    dz1, g1 = ffn_ple_bwd(1, dy, s1, "1")
    G["ssm_w_out"] = _mm(yn, dz1, ta=True, name="ssm_out_dw")[None]
    dyn = _mm(dz1, W["ssm_w_out"][0], tb=True, name="ssm_out_dx")
    dzg, dys, dskip, dnw, dde = _gate_norm_bwd(dyn, ys, xbc, proj1, d_exp, norm_w, name="ssd_gate_norm_bwd")
    G["ssm_norm_w"] = dnw
    G["ssm_D"] = dde.reshape(SSM_HEADS, _PH).sum(axis=1)[None]
    dxs, dB, dC, dac, dar, ddc, ddr = _ssd_scan_bwd(xbc, dys, dskip, states, dtc, acc_, dtr, acr, name="ssd_scan_bwd")
    draw, dal, ddb = _ssd_dt_bwd(_from_col_layout(dac), _from_row_layout(dar), _from_col_layout(ddc),
                                 _from_row_layout(ddr), dt, proj1, dt_bias, a_log, name="ssd_dt_bwd")
    G["ssm_A_log"] = dal[:, :SSM_HEADS]
    G["ssm_dt_bias"] = ddb[:, :SSM_HEADS]
    dxbc = jnp.concatenate([dxs, dB, dC], axis=1)
    dpre, G["ssm_conv_b"], dcw = _conv_silu_bwd(dxbc, proj1, W["ssm_conv_w"][0], row(W["ssm_conv_b"][0]),
                                                name="ssd_conv_bwd")
    G["ssm_conv_w"] = dcw[None]
    dxbc_pre = _dwconv_bwd_data(dpre, W["ssm_conv_w"][0], SSM_CONV, name="ssd_conv_bwd_data")
    dproj1 = jnp.concatenate([dzg.astype(BF16), dxbc_pre.astype(BF16), draw.astype(BF16)], axis=1)
    G["ssm_w_in"] = _mm(x1, dproj1, ta=True, name="ssm_in_dw")[None, :, :W["ssm_w_in"].shape[2]]
    dx1 = _mm(dproj1, ssm_in, tb=True, add=dz1, add_scale=DEEPNORM_ALPHA, name="ssm_in_dx")

    dz0, g0 = ffn_ple_bwd(0, dx1, s0, "0")
    G["attn_w_out"] = _mm(o, dz0, ta=True, name="attn_out_dw")[None]
    do = _mm(dz0, W["attn_w_out"][0], tb=True, name="attn_out_dx")
    dq, dk, dv, dcT, dcq = _attn_bwd(proj0, do, o, lse, cT, name="attn_bwd")
    dcq = dcq.reshape(T, HEAD_PAIRS, 2, LANE // 2)[:, :, :, 0].reshape(T, ATTN_HEADS)
    dc_col = jnp.pad(dcT.reshape(ATTN_HEADS, T).T + dcq, ((0, 0), (0, LANE - ATTN_HEADS)))
    dfl, dbf = _fox_gate_bwd(dc_col, proj0, bf, name="fox_gate_bwd")
    G["attn_b_f"] = dbf[:, :ATTN_HEADS]
    dproj0 = jnp.concatenate([dq.astype(BF16), dk.astype(BF16), dv.astype(BF16), dfl.astype(BF16)], axis=1)
    G["attn_w_in"] = _mm(x, dproj0, ta=True, name="attn_in_dw")[None, :, :W["attn_w_in"].shape[2]]
    grad_x = _mm(dproj0, attn_in, tb=True, add=dz0, add_scale=DEEPNORM_ALPHA, name="attn_in_dx")

    for k in g0:
        G[k] = jnp.stack([g0[k].reshape(W[k].shape[1:]), g1[k].reshape(W[k].shape[1:])])
    return loss, grad_x, G


def kernel(x, p, attn_w_in, attn_b_f, attn_w_out, ssm_w_in, ssm_conv_w, ssm_conv_b, ssm_dt_bias, ssm_A_log, ssm_D, ssm_norm_w, ssm_w_out, ln_mix_g, ln_mix_b, ffn_w_up, ffn_conv_w, ffn_conv_b, ffn_w_down, ln_ffn_g, ln_ffn_b, ple_w_proj, ple_w_gate, ple_b_gate, loss_target, m_attn_w_in, m_attn_b_f, m_attn_w_out, m_ssm_w_in, m_ssm_conv_w, m_ssm_conv_b, m_ssm_dt_bias, m_ssm_A_log, m_ssm_D, m_ssm_norm_w, m_ssm_w_out, m_ln_mix_g, m_ln_mix_b, m_ffn_w_up, m_ffn_conv_w, m_ffn_conv_b, m_ffn_w_down, m_ln_ffn_g, m_ln_ffn_b, m_ple_w_proj, m_ple_w_gate, m_ple_b_gate, v_attn_w_in, v_attn_b_f, v_attn_w_out, v_ssm_w_in, v_ssm_conv_w, v_ssm_conv_b, v_ssm_dt_bias, v_ssm_A_log, v_ssm_D, v_ssm_norm_w, v_ssm_w_out, v_ln_mix_g, v_ln_mix_b, v_ffn_w_up, v_ffn_conv_w, v_ffn_conv_b, v_ffn_w_down, v_ln_ffn_g, v_ln_ffn_b, v_ple_w_proj, v_ple_w_gate, v_ple_b_gate):
    names = [n for n, _ in WEIGHTS]
    axes = dict(WEIGHTS)
    w_loc = dict(zip(names, [attn_w_in, attn_b_f, attn_w_out, ssm_w_in, ssm_conv_w, ssm_conv_b, ssm_dt_bias, ssm_A_log, ssm_D, ssm_norm_w, ssm_w_out, ln_mix_g, ln_mix_b, ffn_w_up, ffn_conv_w, ffn_conv_b, ffn_w_down, ln_ffn_g, ln_ffn_b, ple_w_proj, ple_w_gate, ple_b_gate]))
    m_loc = dict(zip(names, [m_attn_w_in, m_attn_b_f, m_attn_w_out, m_ssm_w_in, m_ssm_conv_w, m_ssm_conv_b, m_ssm_dt_bias, m_ssm_A_log, m_ssm_D, m_ssm_norm_w, m_ssm_w_out, m_ln_mix_g, m_ln_mix_b, m_ffn_w_up, m_ffn_conv_w, m_ffn_conv_b, m_ffn_w_down, m_ln_ffn_g, m_ln_ffn_b, m_ple_w_proj, m_ple_w_gate, m_ple_b_gate]))
    v_loc = dict(zip(names, [v_attn_w_in, v_attn_b_f, v_attn_w_out, v_ssm_w_in, v_ssm_conv_w, v_ssm_conv_b, v_ssm_dt_bias, v_ssm_A_log, v_ssm_D, v_ssm_norm_w, v_ssm_w_out, v_ln_mix_g, v_ln_mix_b, v_ffn_w_up, v_ffn_conv_w, v_ffn_conv_b, v_ffn_w_down, v_ln_ffn_g, v_ln_ffn_b, v_ple_w_proj, v_ple_w_gate, v_ple_b_gate]))
    sharded = [n for n in names if axes[n] is not None]

    gathered = _gather_chips(_pack([w_loc[n] for n in sharded]), name="gather_weights")
    W = dict(w_loc)
    per_chip = [_unpack(gathered[k], [w_loc[n].shape for n in sharded]) for k in range(N_CHIPS)]
    for i, n in enumerate(sharded):
        W[n] = jnp.concatenate([per_chip[k][i] for k in range(N_CHIPS)], axis=axes[n])

    loss, grad_x, G = _local_step(x[0], p[:, 0], loss_target[0], W)
    loss = lax.psum(loss, ("x", "y", "c"))

    def slot(k):
        parts = []
        for n in names:
            g = G[n].reshape(W[n].shape)
            if axes[n] is not None:
                size = w_loc[n].shape[axes[n]]
                g = lax.slice_in_dim(g, k * size, (k + 1) * size, axis=axes[n])
            parts.append(g)
        return _pack(parts)

    contrib = jnp.stack([slot(k) for k in range(N_CHIPS)])
    R = contrib.shape[1]
    H = R // 2
    c = lax.axis_index("c")
    keep = lax.dynamic_slice_in_dim(contrib, c * H, H, axis=1)
    give = lax.dynamic_slice_in_dim(contrib, (1 - c) * H, H, axis=1)
    pair = _add2(keep, _pair_swap(give, name="grad_pair_swap"), name="grad_pair_sum")
    half = _sum4(_chip_exchange(pair, name="grad_chip_exchange"), name="grad_chip_sum")
    gflat = _pair_gather(half, name="grad_pair_gather")

    shapes = [w_loc[n].shape for n in names]
    delta, new_m, new_v = _adamw(_pack([w_loc[n] for n in names]), gflat, _pack([m_loc[n] for n in names]),
                                 _pack([v_loc[n] for n in names]), name="adamw")
    return (loss, grad_x[None], *_unpack(gflat, shapes), *_unpack(delta, shapes), *_unpack(new_m, shapes),
            *_unpack(new_v, shapes))
```

```python
import functools
import math

import jax
import jax.numpy as jnp
from jax import lax
from jax.experimental import pallas as pl
from jax.experimental.pallas import tpu as pltpu

F32 = jnp.float32
BF16 = jnp.bfloat16
MESH = pl.DeviceIdType.MESH

D_MODEL = 1024
ATTN_HEADS = 16
HEAD_PAIRS = ATTN_HEADS // 2
SSM_D_INNER = 2048
SSM_HEADS = 32
SSM_GROUPS = 8
SSM_STATE = 128
SSM_CONV = 4
SSM_CHUNK = 128
SSM_XBC = SSM_D_INNER + 2 * SSM_GROUPS * SSM_STATE
FFN_DIM = 2816
FFN_CONV = 3
DEPTH = 2
LN_EPS = 1e-5
RMS_EPS = 1e-5
DEEPNORM_ALPHA = (2 * DEPTH) ** 0.25
ADAM_LR = 0.001
ADAM_B1 = 0.9
ADAM_B2 = 0.999
ADAM_EPS = 1e-08
ADAM_WD = 0.01
ADAM_STEP = 10

LANE = 128
SUBLANE = 8
HALO = SUBLANE
NEG = -1e30
ATTN_IN_PAD = 3 * D_MODEL + LANE
SSM_IN_PAD = 2 * SSM_D_INNER + 2 * SSM_GROUPS * SSM_STATE + LANE
PACK_COLS = 1024
PACK_ROW_ALIGN = 512

ATTN_BLOCK = 512
ROW_BLOCK = 256
CUM_BLOCK = 256


def _params(sem, vmem_mb=48):
    return pltpu.CompilerParams(dimension_semantics=sem, vmem_limit_bytes=vmem_mb * 2 ** 20)


def _pick(n, target, mult=LANE):
    best = None
    d = mult
    while d <= min(n, target):
        if n % d == 0:
            best = d
        d += mult
    return n if best is None else best


def _sigmoid(x):
    return 1.0 / (1.0 + jnp.exp(-x))


def _log1p(u):
    w = 1.0 + u
    return jnp.where(w == 1.0, u, jnp.log(w) * (u / (w - 1.0)))


def _softplus(x):
    return jnp.maximum(x, 0.0) + _log1p(jnp.exp(-jnp.abs(x)))


def _split3(x):
    hi = x.astype(BF16)
    r1 = x - hi.astype(F32)
    mid = r1.astype(BF16)
    lo = (r1 - mid.astype(F32)).astype(BF16)
    return hi, mid, lo


def _tri_matmul(tri, x):
    out = None
    for part in _split3(x):
        t = jnp.dot(tri, part, preferred_element_type=F32)
        out = t if out is None else out + t
    return out


def _tri(n, lower):
    r = lax.broadcasted_iota(jnp.int32, (n, n), 0)
    c = lax.broadcasted_iota(jnp.int32, (n, n), 1)
    return jnp.where((c <= r) if lower else (c >= r), 1.0, 0.0).astype(BF16)


def _mm(a, b, *, name, ta=False, tb=False, add=None, add_scale=1.0, bm=1024, bn=1024, bk=512):
    if ta:
        K, M = a.shape
    else:
        M, K = a.shape
    if tb:
        N, Kb = b.shape
    else:
        Kb, N = b.shape
    assert K == Kb, (a.shape, b.shape, ta, tb)
    bm, bn, bk = _pick(M, bm), _pick(N, bn), _pick(K, bk)
    nk = K // bk
    a_spec = pl.BlockSpec((bk, bm), lambda i, j, k: (k, i)) if ta else pl.BlockSpec((bm, bk), lambda i, j, k: (i, k))
    b_spec = pl.BlockSpec((bn, bk), lambda i, j, k: (j, k)) if tb else pl.BlockSpec((bk, bn), lambda i, j, k: (k, j))
    o_spec = pl.BlockSpec((bm, bn), lambda i, j, k: (i, j))
    dims = (((0 if ta else 1,), (1 if tb else 0,)), ((), ()))
    has_add = add is not None

    def kern(*refs):
        if has_add:
            a_ref, b_ref, add_ref, o_ref, acc_ref = refs
        else:
            a_ref, b_ref, o_ref, acc_ref = refs
        k = pl.program_id(2)

        @pl.when(k == 0)
        def _():
            acc_ref[...] = jnp.zeros_like(acc_ref)

        acc_ref[...] += lax.dot_general(a_ref[...].astype(BF16), b_ref[...].astype(BF16), dims,
                                        preferred_element_type=F32)

        @pl.when(k == nk - 1)
        def _():
            r = acc_ref[...]
            if has_add:
                r = r + add_scale * add_ref[...]
            o_ref[...] = r

    ins = [a, b] + ([add] if has_add else [])
    in_specs = [a_spec, b_spec] + ([o_spec] if has_add else [])
    return pl.pallas_call(
        kern, name=name, grid=(M // bm, N // bn, nk),
        in_specs=in_specs, out_specs=o_spec,
        out_shape=jax.ShapeDtypeStruct((M, N), F32),
        scratch_shapes=[pltpu.VMEM((bm, bn), F32)],
        compiler_params=_params(("parallel", "parallel", "arbitrary")),
    )(*ins)


def _ln_stats(z):
    mu = jnp.mean(z, axis=-1, keepdims=True)
    zc = z - mu
    var = jnp.mean(zc * zc, axis=-1, keepdims=True)
    return zc, lax.rsqrt(var + LN_EPS)


def _ln_fwd(x, r, g, b, *, name):
    T, D = x.shape
    bt = _pick(T, ROW_BLOCK, SUBLANE)

    def kern(x_ref, r_ref, g_ref, b_ref, z_ref, h_ref):
        z = DEEPNORM_ALPHA * x_ref[...] + r_ref[...]
        zc, rstd = _ln_stats(z)
        z_ref[...] = z
        h_ref[...] = zc * rstd * g_ref[...] + b_ref[...]

    row = pl.BlockSpec((bt, D), lambda i: (i, 0))
    vec = pl.BlockSpec((1, D), lambda i: (0, 0))
    return pl.pallas_call(
        kern, name=name, grid=(T // bt,), in_specs=[row, row, vec, vec], out_specs=[row, row],
        out_shape=[jax.ShapeDtypeStruct((T, D), F32)] * 2,
        compiler_params=_params(("parallel",)),
    )(x, r, g, b)


def _ln_bwd(dy, z, g, *, name):
    T, D = z.shape
    bt = _pick(T, ROW_BLOCK, SUBLANE)

    def kern(dy_ref, z_ref, g_ref, dz_ref, dg_ref, db_ref):
        i = pl.program_id(0)
        zc, rstd = _ln_stats(z_ref[...])
        xhat = zc * rstd
        dyv = dy_ref[...]
        dxh = dyv * g_ref[...]
        m1 = jnp.mean(dxh, axis=-1, keepdims=True)
        m2 = jnp.mean(dxh * xhat, axis=-1, keepdims=True)
        dz_ref[...] = rstd * (dxh - m1 - xhat * m2)

        @pl.when(i == 0)
        def _():
            dg_ref[...] = jnp.zeros_like(dg_ref)
            db_ref[...] = jnp.zeros_like(db_ref)

        dg_ref[...] += jnp.sum(dyv * xhat, axis=0, keepdims=True)
        db_ref[...] += jnp.sum(dyv, axis=0, keepdims=True)

    row = pl.BlockSpec((bt, D), lambda i: (i, 0))
    vec = pl.BlockSpec((1, D), lambda i: (0, 0))
    return pl.pallas_call(
        kern, name=name, grid=(T // bt,), in_specs=[row, row, vec], out_specs=[row, vec, vec],
        out_shape=[jax.ShapeDtypeStruct((T, D), F32), jax.ShapeDtypeStruct((1, D), F32),
                   jax.ShapeDtypeStruct((1, D), F32)],
        compiler_params=_params(("arbitrary",)),
    )(dy, z, g)


def _conv_past(ext_ref, cw_ref, K, bt):
    out = None
    for k in range(K):
        term = cw_ref[k:k + 1, :] * ext_ref[pl.ds(HALO - (K - 1) + k, bt), :]
        out = term if out is None else out + term
    return out


def _fill_ext_past(ext_ref, halo_ref, cur, i, bt):
    ext_ref[pl.ds(0, HALO), :] = jnp.where(i > 0, halo_ref[...], 0.0)
    ext_ref[pl.ds(HALO, bt), :] = cur


def _halo_prev(bt, bc, off):
    return pl.BlockSpec((HALO, bc), lambda i, j: (jnp.maximum(i * (bt // HALO) - 1, 0), j + off))


def _gelu(x):
    return 0.5 * x * (1.0 + lax.erf(x * (1.0 / math.sqrt(2.0))))


def _gelu_grad(x):
    return 0.5 * (1.0 + lax.erf(x * (1.0 / math.sqrt(2.0)))) + x * jnp.exp(-0.5 * x * x) * (1.0 / math.sqrt(2.0 * math.pi))


def _ffn_act_fwd(up, cw, cb, *, name):
    T, F2 = up.shape
    F = F2 // 2
    bt = _pick(T, ROW_BLOCK, SUBLANE)
    bc = _pick(F, 1408)
    nb = F // bc

    def kern(u_ref, g_ref, halo_ref, cw_ref, cb_ref, a_ref, ext_ref):
        i = pl.program_id(0)
        _fill_ext_past(ext_ref, halo_ref, g_ref[...], i, bt)
        gc = cb_ref[...] + _conv_past(ext_ref, cw_ref, FFN_CONV, bt)
        a_ref[...] = _gelu(gc) * u_ref[...]

    return pl.pallas_call(
        kern, name=name, grid=(T // bt, nb),
        in_specs=[pl.BlockSpec((bt, bc), lambda i, j: (i, j)),
                  pl.BlockSpec((bt, bc), lambda i, j: (i, j + nb)),
                  _halo_prev(bt, bc, nb),
                  pl.BlockSpec((FFN_CONV, bc), lambda i, j: (0, j)),
                  pl.BlockSpec((1, bc), lambda i, j: (0, j))],
        out_specs=pl.BlockSpec((bt, bc), lambda i, j: (i, j)),
        out_shape=jax.ShapeDtypeStruct((T, F), F32),
        scratch_shapes=[pltpu.VMEM((bt + HALO, bc), F32)],
        compiler_params=_params(("parallel", "parallel")),
    )(up, up, up, cw, cb)


def _ffn_act_bwd(da, up, cw, cb, *, name):
    T, F2 = up.shape
    F = F2 // 2
    bt = _pick(T, ROW_BLOCK, SUBLANE)
    bc = _pick(F, 1408)
    nb = F // bc
    K = FFN_CONV

    def kern(da_ref, u_ref, g_ref, halo_ref, cw_ref, cb_ref, du_ref, dgc_ref, dcb_ref, dcw_ref, ext_ref):
        i = pl.program_id(1)
        _fill_ext_past(ext_ref, halo_ref, g_ref[...], i, bt)
        gc = cb_ref[...] + _conv_past(ext_ref, cw_ref, K, bt)
        dav = da_ref[...]
        du_ref[...] = dav * _gelu(gc)
        dgc = dav * u_ref[...] * _gelu_grad(gc)
        dgc_ref[...] = dgc

        @pl.when(i == 0)
        def _():
            dcb_ref[...] = jnp.zeros_like(dcb_ref)
            dcw_ref[...] = jnp.zeros_like(dcw_ref)

        dcb_ref[...] += jnp.sum(dgc, axis=0, keepdims=True)
        for k in range(K):
            dcw_ref[k:k + 1, :] += jnp.sum(dgc * ext_ref[pl.ds(HALO - (K - 1) + k, bt), :], axis=0, keepdims=True)

    blk = pl.BlockSpec((bt, bc), lambda j, i: (i, j))
    return pl.pallas_call(
        kern, name=name, grid=(nb, T // bt),
        in_specs=[blk, blk,
                  pl.BlockSpec((bt, bc), lambda j, i: (i, j + nb)),
                  pl.BlockSpec((HALO, bc), lambda j, i: (jnp.maximum(i * (bt // HALO) - 1, 0), j + nb)),
                  pl.BlockSpec((K, bc), lambda j, i: (0, j)),
                  pl.BlockSpec((1, bc), lambda j, i: (0, j))],
        out_specs=[blk, blk, pl.BlockSpec((1, bc), lambda j, i: (0, j)), pl.BlockSpec((K, bc), lambda j, i: (0, j))],
        out_shape=[jax.ShapeDtypeStruct((T, F), F32), jax.ShapeDtypeStruct((T, F), F32),
                   jax.ShapeDtypeStruct((1, F), F32), jax.ShapeDtypeStruct((K, F), F32)],
        scratch_shapes=[pltpu.VMEM((bt + HALO, bc), F32)],
        compiler_params=_params(("parallel", "arbitrary")),
    )(da, up, up, up, cw, cb)


def _dwconv_bwd_data(dgc, cw, K, *, name):
    T, C = dgc.shape
    bt = _pick(T, ROW_BLOCK, SUBLANE)
    bc = _pick(C, 1408)
    nt = T // bt
    last_halo = T // HALO - 1

    def kern(d_ref, halo_ref, cw_ref, o_ref, ext_ref):
        i = pl.program_id(0)
        ext_ref[pl.ds(0, bt), :] = d_ref[...]
        ext_ref[pl.ds(bt, HALO), :] = jnp.where(i < nt - 1, halo_ref[...], 0.0)
        out = None
        for k in range(K):
            term = cw_ref[k:k + 1, :] * ext_ref[pl.ds(K - 1 - k, bt), :]
            out = term if out is None else out + term
        o_ref[...] = out

    return pl.pallas_call(
        kern, name=name, grid=(nt, C // bc),
        in_specs=[pl.BlockSpec((bt, bc), lambda i, j: (i, j)),
                  pl.BlockSpec((HALO, bc), lambda i, j: (jnp.minimum((i + 1) * (bt // HALO), last_halo), j)),
                  pl.BlockSpec((K, bc), lambda i, j: (0, j))],
        out_specs=pl.BlockSpec((bt, bc), lambda i, j: (i, j)),
        out_shape=jax.ShapeDtypeStruct((T, C), F32),
        scratch_shapes=[pltpu.VMEM((bt + HALO, bc), F32)],
        compiler_params=_params(("parallel", "parallel")),
    )(dgc, dgc, cw)


def _ple_fwd(h, G, bg, pp, *, name):
    T, D = h.shape
    bt = _pick(T, ROW_BLOCK, SUBLANE)

    def kern(h_ref, G_ref, bg_ref, pp_ref, o_ref):
        o_ref[...] = h_ref[...] + _sigmoid(G_ref[...] + bg_ref[...]) * pp_ref[...]

    row = pl.BlockSpec((bt, D), lambda i: (i, 0))
    vec = pl.BlockSpec((1, D), lambda i: (0, 0))
    return pl.pallas_call(
        kern, name=name, grid=(T // bt,), in_specs=[row, row, vec, row], out_specs=row,
        out_shape=jax.ShapeDtypeStruct((T, D), F32), compiler_params=_params(("parallel",)),
    )(h, G, bg, pp)


def _ple_bwd(dx, G, bg, pp, *, name):
    T, D = dx.shape
    bt = _pick(T, ROW_BLOCK, SUBLANE)

    def kern(dx_ref, G_ref, bg_ref, pp_ref, dG_ref, dpp_ref, dbg_ref):
        i = pl.program_id(0)
        gate = _sigmoid(G_ref[...] + bg_ref[...])
        dxv = dx_ref[...]
        dG = dxv * pp_ref[...] * gate * (1.0 - gate)
        dG_ref[...] = dG
        dpp_ref[...] = dxv * gate

        @pl.when(i == 0)
        def _():
            dbg_ref[...] = jnp.zeros_like(dbg_ref)

        dbg_ref[...] += jnp.sum(dG, axis=0, keepdims=True)

    row = pl.BlockSpec((bt, D), lambda i: (i, 0))
    vec = pl.BlockSpec((1, D), lambda i: (0, 0))
    return pl.pallas_call(
        kern, name=name, grid=(T // bt,), in_specs=[row, row, vec, row], out_specs=[row, row, vec],
        out_shape=[jax.ShapeDtypeStruct((T, D), F32), jax.ShapeDtypeStruct((T, D), F32),
                   jax.ShapeDtypeStruct((1, D), F32)],
        compiler_params=_params(("arbitrary",)),
    )(dx, G, bg, pp)


def _fox_gate_fwd(proj, bf, *, name):
    T = proj.shape[0]
    bt = _pick(T, CUM_BLOCK, SUBLANE)
    fcol = 3 * D_MODEL // LANE

    def kern(f_ref, bf_ref, c_ref, carry_ref):
        i = pl.program_id(0)

        @pl.when(i == 0)
        def _():
            carry_ref[...] = jnp.zeros_like(carry_ref)

        x = f_ref[...] + bf_ref[...]
        lf = jnp.minimum(x, 0.0) - _log1p(jnp.exp(-jnp.abs(x)))
        cs = _tri_matmul(_tri(bt, True), lf) + carry_ref[...]
        c_ref[...] = cs
        carry_ref[...] = cs[bt - 1:bt, :]

    return pl.pallas_call(
        kern, name=name, grid=(T // bt,),
        in_specs=[pl.BlockSpec((bt, LANE), lambda i: (i, fcol)), pl.BlockSpec((1, LANE), lambda i: (0, 0))],
        out_specs=pl.BlockSpec((bt, LANE), lambda i: (i, 0)),
        out_shape=jax.ShapeDtypeStruct((T, LANE), F32),
        scratch_shapes=[pltpu.VMEM((1, LANE), F32)],
        compiler_params=_params(("arbitrary",)),
    )(proj, bf)


def _fox_gate_bwd(dc, proj, bf, *, name):
    T = proj.shape[0]
    bt = _pick(T, CUM_BLOCK, SUBLANE)
    nb = T // bt
    fcol = 3 * D_MODEL // LANE

    def kern(dc_ref, f_ref, bf_ref, df_ref, dbf_ref, carry_ref):
        i = pl.program_id(0)

        @pl.when(i == 0)
        def _():
            carry_ref[...] = jnp.zeros_like(carry_ref)
            dbf_ref[...] = jnp.zeros_like(dbf_ref)

        dlf = _tri_matmul(_tri(bt, False), dc_ref[...]) + carry_ref[...]
        carry_ref[...] = dlf[0:1, :]
        x = f_ref[...] + bf_ref[...]
        lane = lax.broadcasted_iota(jnp.int32, (bt, LANE), 1)
        df = jnp.where(lane < ATTN_HEADS, dlf / (1.0 + jnp.exp(x)), 0.0)
        df_ref[...] = df
        dbf_ref[...] += jnp.sum(df, axis=0, keepdims=True)

    return pl.pallas_call(
        kern, name=name, grid=(nb,),
        in_specs=[pl.BlockSpec((bt, LANE), lambda i: (nb - 1 - i, 0)),
                  pl.BlockSpec((bt, LANE), lambda i: (nb - 1 - i, fcol)),
                  pl.BlockSpec((1, LANE), lambda i: (0, 0))],
        out_specs=[pl.BlockSpec((bt, LANE), lambda i: (nb - 1 - i, 0)), pl.BlockSpec((1, LANE), lambda i: (0, 0))],
        out_shape=[jax.ShapeDtypeStruct((T, LANE), F32), jax.ShapeDtypeStruct((1, LANE), F32)],
        scratch_shapes=[pltpu.VMEM((1, LANE), F32)],
        compiler_params=_params(("arbitrary",)),
    )(dc, proj, bf)


_NT = (((1,), (1,)), ((), ()))
_TN = (((0,), (0,)), ((), ()))


def _dot(a, b, dims=None):
    if dims is None:
        return jnp.dot(a, b, preferred_element_type=F32)
    return lax.dot_general(a, b, dims, preferred_element_type=F32)


LOG2E = 1.0 / math.log(2.0)
LN2 = math.log(2.0)
Q_SCALE = 0.125 * LOG2E
HALF = LANE // 2
L_LANE = (HALF, 0)


def _attn_prep(proj, *, name):
    T = proj.shape[0]
    bt = _pick(T, ATTN_BLOCK)

    def kern(q_ref, k_ref, v_ref, qa_ref, qb_ref, kk_ref, ka_ref, kb_ref, vv_ref, va_ref, vb_ref):
        lane = lax.broadcasted_iota(jnp.int32, (bt, LANE), 1)
        lo = lane < HALF
        q = q_ref[...] * Q_SCALE
        k = k_ref[...]
        v = v_ref[...]
        qa_ref[...] = jnp.where(lo, q, 0.0).astype(BF16)
        qb_ref[...] = jnp.where(lo, 0.0, q).astype(BF16)
        kk_ref[...] = k.astype(BF16)
        ka_ref[...] = jnp.where(lo, k, 0.0).astype(BF16)
        kb_ref[...] = jnp.where(lo, 0.0, k).astype(BF16)
        vv_ref[...] = v.astype(BF16)
        va_ref[...] = jnp.where(lo, v, jnp.where(lane == L_LANE[0], 1.0, 0.0)).astype(BF16)
        vb_ref[...] = jnp.where(lo, jnp.where(lane == L_LANE[1], 1.0, 0.0), v).astype(BF16)

    kcol, vcol = D_MODEL // LANE, 2 * D_MODEL // LANE
    out = pl.BlockSpec((bt, LANE), lambda i, hp: (i, hp))
    return pl.pallas_call(
        kern, name=name, grid=(T // bt, HEAD_PAIRS),
        in_specs=[out, pl.BlockSpec((bt, LANE), lambda i, hp: (i, kcol + hp)),
                  pl.BlockSpec((bt, LANE), lambda i, hp: (i, vcol + hp))],
        out_specs=[out] * 8, out_shape=[jax.ShapeDtypeStruct((T, D_MODEL), BF16)] * 8,
        compiler_params=_params(("parallel", "parallel")),
    )(proj, proj, proj)


def _attn_fwd(qa, qb, kk, va, vb, cT, *, name):
    T = qa.shape[0]
    tb = _pick(T, ATTN_BLOCK)
    nq = T // tb
    rep = tb // LANE

    def kern(qa_ref, qb_ref, k_ref, va_ref, vb_ref, c_ref, o_ref, lsea_ref, lseb_ref, m_ref, acc_ref):
        qi = pl.program_id(1)
        ki = pl.program_id(2)

        @pl.when(ki == 0)
        def _():
            m_ref[...] = jnp.full_like(m_ref, NEG)
            acc_ref[...] = jnp.zeros_like(acc_ref)

        def step(diag):
            k = k_ref[...]
            for h, (q_ref, v_ref) in enumerate(((qa_ref, va_ref), (qb_ref, vb_ref))):
                s = _dot(q_ref[...], k, _NT) - c_ref[h:h + 1, :]
                if diag:
                    r = lax.broadcasted_iota(jnp.int32, (tb, tb), 0)
                    c = lax.broadcasted_iota(jnp.int32, (tb, tb), 1)
                    s = jnp.where(c <= r, s, NEG)
                m_prev = m_ref[h]
                m_new = jnp.maximum(m_prev, jnp.max(s, axis=1, keepdims=True))
                p = jnp.exp2(s - jnp.tile(m_new, (1, rep)))
                acc_ref[h] = acc_ref[h] * jnp.exp2(m_prev - m_new) + _dot(p.astype(BF16), v_ref[...])
                m_ref[h] = m_new

        @pl.when(ki < qi)
        def _():
            step(False)

        @pl.when(ki == qi)
        def _():
            step(True)
            lo = lax.broadcasted_iota(jnp.int32, (tb, LANE), 1) < HALF
            a0, a1 = acc_ref[0], acc_ref[1]
            l0 = a0[:, L_LANE[0]:L_LANE[0] + 1]
            l1 = a1[:, L_LANE[1]:L_LANE[1] + 1]
            o_ref[...] = jnp.where(lo, a0 / l0, a1 / l1)
            lsea_ref[...] = m_ref[0] + jnp.log(l0) * LOG2E
            lseb_ref[...] = m_ref[1] + jnp.log(l1) * LOG2E

    qspec = pl.BlockSpec((tb, LANE), lambda hp, qi, ki: (qi, hp))
    kspec = pl.BlockSpec((tb, LANE), lambda hp, qi, ki: (jnp.minimum(ki, qi), hp))
    return pl.pallas_call(
        kern, name=name, grid=(HEAD_PAIRS, nq, nq),
        in_specs=[qspec, qspec, kspec, kspec, kspec,
                  pl.BlockSpec((None, 2, tb), lambda hp, qi, ki: (hp, 0, jnp.minimum(ki, qi)))],
        out_specs=[qspec, qspec, qspec],
        out_shape=[jax.ShapeDtypeStruct((T, D_MODEL), F32)] * 3,
        scratch_shapes=[pltpu.VMEM((2, tb, LANE), F32), pltpu.VMEM((2, tb, LANE), F32)],
        compiler_params=_params(("parallel", "parallel", "arbitrary")),
    )(qa, qb, kk, va, vb, cT)


def _attn_bwd_prep(do, o, *, name):
    T, D = do.shape
    bt = _pick(T, ATTN_BLOCK)

    def kern(do_ref, o_ref, doa_ref, dob_ref, dlta_ref, dltb_ref):
        lo = lax.broadcasted_iota(jnp.int32, (bt, LANE), 1) < HALF
        dov = do_ref[...]
        prod = dov * o_ref[...]
        doa_ref[...] = jnp.where(lo, dov, 0.0).astype(BF16)
        dob_ref[...] = jnp.where(lo, 0.0, dov).astype(BF16)
        dlta_ref[...] = jnp.broadcast_to(jnp.sum(jnp.where(lo, prod, 0.0), axis=1, keepdims=True), (bt, LANE))
        dltb_ref[...] = jnp.broadcast_to(jnp.sum(jnp.where(lo, 0.0, prod), axis=1, keepdims=True), (bt, LANE))

    blk = pl.BlockSpec((bt, LANE), lambda i, hp: (i, hp))
    return pl.pallas_call(
        kern, name=name, grid=(T // bt, HEAD_PAIRS), in_specs=[blk, blk], out_specs=[blk] * 4,
        out_shape=[jax.ShapeDtypeStruct((T, D), BF16)] * 2 + [jax.ShapeDtypeStruct((T, D), F32)] * 2,
        compiler_params=_params(("parallel", "parallel")),
    )(do, o)


def _attn_bwd(qa, qb, kk, ka, kb, vv, doa, dob, lsea, lseb, dlta, dltb, cT, *, name):
    T = qa.shape[0]
    tb = _pick(T, ATTN_BLOCK)
    nq = T // tb
    rep = tb // LANE

    def kern(qa_ref, qb_ref, k_ref, ka_ref, kb_ref, v_ref, doa_ref, dob_ref, lsea_ref, lseb_ref, dlta_ref, dltb_ref,
             c_ref, dq_ref, dk_ref, dv_ref, dc_ref, dcq_ref):
        ki = pl.program_id(1)
        qi = pl.program_id(2)

        @pl.when(jnp.logical_and(ki == 0, qi == 0))
        def _():
            dq_ref[...] = jnp.zeros_like(dq_ref)
            dcq_ref[...] = jnp.zeros_like(dcq_ref)

        @pl.when(qi == 0)
        def _():
            dk_ref[...] = jnp.zeros_like(dk_ref)
            dv_ref[...] = jnp.zeros_like(dv_ref)
            dc_ref[...] = jnp.zeros_like(dc_ref)

        def step(diag):
            k = k_ref[...]
            v = v_ref[...]
            dq = None
            dk = None
            dv = None
            row_sums = []
            heads = ((qa_ref, ka_ref, doa_ref, lsea_ref, dlta_ref), (qb_ref, kb_ref, dob_ref, lseb_ref, dltb_ref))
            for h, (q_ref, km_ref, do_ref, lse_ref, dlt_ref) in enumerate(heads):
                q = q_ref[...]
                dom = do_ref[...]
                s = _dot(q, k, _NT) - c_ref[h:h + 1, :]
                if diag:
                    r = lax.broadcasted_iota(jnp.int32, (tb, tb), 0)
                    c = lax.broadcasted_iota(jnp.int32, (tb, tb), 1)
                    s = jnp.where(c <= r, s, NEG)
                p = jnp.exp2(s - jnp.tile(lse_ref[...], (1, rep)))
                ds = p * (_dot(dom, v, _NT) - jnp.tile(dlt_ref[...], (1, rep)))
                dc_ref[h:h + 1, :] -= jnp.sum(ds, axis=0, keepdims=True)
                row_sums.append(jnp.sum(ds, axis=1, keepdims=True))
                dsb = ds.astype(BF16)
                tv = _dot(p.astype(BF16), dom, _TN)
                tk = _dot(dsb, q, _TN)
                tq = _dot(dsb, km_ref[...])
                dv = tv if dv is None else dv + tv
                dk = tk if dk is None else dk + tk
                dq = tq if dq is None else dq + tq
            dv_ref[...] += dv
            dk_ref[...] += dk * LN2
            rows = pl.ds(pl.multiple_of(qi * tb, tb), tb)
            dq_ref[rows, :] += dq * 0.125
            lo = lax.broadcasted_iota(jnp.int32, (tb, LANE), 1) < HALF
            dcq_ref[rows, :] += jnp.where(lo, row_sums[0], row_sums[1])

        @pl.when(qi > ki)
        def _():
            step(False)

        @pl.when(qi == ki)
        def _():
            step(True)

    qspec = pl.BlockSpec((tb, LANE), lambda hp, ki, qi: (jnp.maximum(qi, ki), hp))
    kspec = pl.BlockSpec((tb, LANE), lambda hp, ki, qi: (ki, hp))
    cspec = pl.BlockSpec((None, 2, tb), lambda hp, ki, qi: (hp, 0, ki))
    qacc = pl.BlockSpec((T, LANE), lambda hp, ki, qi: (0, hp))
    return pl.pallas_call(
        kern, name=name, grid=(HEAD_PAIRS, nq, nq),
        in_specs=[qspec, qspec, kspec, kspec, kspec, kspec, qspec, qspec, qspec, qspec, qspec, qspec, cspec],
        out_specs=[qacc, kspec, kspec, cspec, qacc],
        out_shape=[jax.ShapeDtypeStruct((T, D_MODEL), F32)] * 3 + [jax.ShapeDtypeStruct((HEAD_PAIRS, 2, T), F32),
                                                                   jax.ShapeDtypeStruct((T, D_MODEL), F32)],
        compiler_params=_params(("parallel", "arbitrary", "arbitrary"), vmem_mb=56),
    )(qa, qb, kk, ka, kb, vv, doa, dob, lsea, lseb, dlta, dltb, cT)


def _ssd_dt_fwd(proj, dt_bias, a_log, *, name):
    T = proj.shape[0]
    Q = SSM_CHUNK
    col = (2 * SSM_D_INNER + 2 * SSM_GROUPS * SSM_STATE) // LANE

    def kern(raw_ref, b_ref, al_ref, dt_ref, ac_ref):
        dt = _softplus(raw_ref[...] + b_ref[...])
        dt_ref[...] = dt
        ac_ref[...] = _tri_matmul(_tri(Q, True), dt * (-jnp.exp(al_ref[...])))

    vec = pl.BlockSpec((1, LANE), lambda i: (0, 0))
    blk = pl.BlockSpec((Q, LANE), lambda i: (i, 0))
    return pl.pallas_call(
        kern, name=name, grid=(T // Q,),
        in_specs=[pl.BlockSpec((Q, LANE), lambda i: (i, col)), vec, vec], out_specs=[blk, blk],
        out_shape=[jax.ShapeDtypeStruct((T, LANE), F32)] * 2,
        compiler_params=_params(("parallel",)),
    )(proj, dt_bias, a_log)


def _ssd_dt_bwd(da_a, da_b, ddt_a, ddt_b, dt, proj, dt_bias, a_log, *, name):
    T = proj.shape[0]
    Q = SSM_CHUNK
    col = (2 * SSM_D_INNER + 2 * SSM_GROUPS * SSM_STATE) // LANE

    def kern(daa_ref, dab_ref, dda_ref, ddb_ref, dt_ref, raw_ref, b_ref, al_ref, draw_ref, dal_ref, db_ref, acc_ref):
        i = pl.program_id(0)

        @pl.when(i == 0)
        def _():
            acc_ref[...] = jnp.zeros_like(acc_ref)
            db_ref[...] = jnp.zeros_like(db_ref)

        A = -jnp.exp(al_ref[...])
        ddA = _tri_matmul(_tri(Q, False), daa_ref[...] + dab_ref[...])
        ddt = dda_ref[...] + ddb_ref[...] + ddA * A
        acc_ref[...] += jnp.sum(ddA * dt_ref[...], axis=0, keepdims=True)
        lane = lax.broadcasted_iota(jnp.int32, (Q, LANE), 1)
        draw = jnp.where(lane < SSM_HEADS, ddt * _sigmoid(raw_ref[...] + b_ref[...]), 0.0)
        draw_ref[...] = draw
        db_ref[...] += jnp.sum(draw, axis=0, keepdims=True)
        dal_ref[...] = acc_ref[...] * A

    vec = pl.BlockSpec((1, LANE), lambda i: (0, 0))
    blk = pl.BlockSpec((Q, LANE), lambda i: (i, 0))
    return pl.pallas_call(
        kern, name=name, grid=(T // Q,),
        in_specs=[blk, blk, blk, blk, blk, pl.BlockSpec((Q, LANE), lambda i: (i, col)), vec, vec],
        out_specs=[blk, vec, vec],
        out_shape=[jax.ShapeDtypeStruct((T, LANE), F32), jax.ShapeDtypeStruct((1, LANE), F32),
                   jax.ShapeDtypeStruct((1, LANE), F32)],
        scratch_shapes=[pltpu.VMEM((1, LANE), F32)],
        compiler_params=_params(("arbitrary",)),
    )(da_a, da_b, ddt_a, ddt_b, dt, proj, dt_bias, a_log)


def _conv_silu_fwd(proj, cw, cb, *, name):
    T = proj.shape[0]
    C = SSM_XBC
    bt = _pick(T, ROW_BLOCK, SUBLANE)
    bc = 1024
    off = SSM_D_INNER // bc

    def kern(x_ref, halo_ref, cw_ref, cb_ref, o_ref, ext_ref):
        i = pl.program_id(0)
        _fill_ext_past(ext_ref, halo_ref, x_ref[...], i, bt)
        pre = cb_ref[...] + _conv_past(ext_ref, cw_ref, SSM_CONV, bt)
        o_ref[...] = pre * _sigmoid(pre)

    return pl.pallas_call(
        kern, name=name, grid=(T // bt, C // bc),
        in_specs=[pl.BlockSpec((bt, bc), lambda i, j: (i, j + off)), _halo_prev(bt, bc, off),
                  pl.BlockSpec((SSM_CONV, bc), lambda i, j: (0, j)), pl.BlockSpec((1, bc), lambda i, j: (0, j))],
        out_specs=pl.BlockSpec((bt, bc), lambda i, j: (i, j)),
        out_shape=jax.ShapeDtypeStruct((T, C), F32),
        scratch_shapes=[pltpu.VMEM((bt + HALO, bc), F32)],
        compiler_params=_params(("parallel", "parallel")),
    )(proj, proj, cw, cb)


def _conv_silu_bwd(dxbc, proj, cw, cb, *, name):
    T = proj.shape[0]
    C = SSM_XBC
    K = SSM_CONV
    bt = _pick(T, ROW_BLOCK, SUBLANE)
    bc = 1024
    off = SSM_D_INNER // bc

    def kern(d_ref, x_ref, halo_ref, cw_ref, cb_ref, dpre_ref, dcb_ref, dcw_ref, ext_ref):
        i = pl.program_id(1)
        _fill_ext_past(ext_ref, halo_ref, x_ref[...], i, bt)
        pre = cb_ref[...] + _conv_past(ext_ref, cw_ref, K, bt)
        sg = _sigmoid(pre)
        dpre = d_ref[...] * sg * (1.0 + pre * (1.0 - sg))
        dpre_ref[...] = dpre

        @pl.when(i == 0)
        def _():
            dcb_ref[...] = jnp.zeros_like(dcb_ref)
            dcw_ref[...] = jnp.zeros_like(dcw_ref)

        dcb_ref[...] += jnp.sum(dpre, axis=0, keepdims=True)
        for k in range(K):
            dcw_ref[k:k + 1, :] += jnp.sum(dpre * ext_ref[pl.ds(HALO - (K - 1) + k, bt), :], axis=0, keepdims=True)

    blk = pl.BlockSpec((bt, bc), lambda j, i: (i, j))
    return pl.pallas_call(
        kern, name=name, grid=(C // bc, T // bt),
        in_specs=[blk, pl.BlockSpec((bt, bc), lambda j, i: (i, j + off)),
                  pl.BlockSpec((HALO, bc), lambda j, i: (jnp.maximum(i * (bt // HALO) - 1, 0), j + off)),
                  pl.BlockSpec((K, bc), lambda j, i: (0, j)), pl.BlockSpec((1, bc), lambda j, i: (0, j))],
        out_specs=[blk, pl.BlockSpec((1, bc), lambda j, i: (0, j)), pl.BlockSpec((K, bc), lambda j, i: (0, j))],
        out_shape=[jax.ShapeDtypeStruct((T, C), F32), jax.ShapeDtypeStruct((1, C), F32),
                   jax.ShapeDtypeStruct((K, C), F32)],
        scratch_shapes=[pltpu.VMEM((bt + HALO, bc), F32)],
        compiler_params=_params(("parallel", "arbitrary")),
    )(dxbc, proj, proj, cw, cb)


_GP = SSM_D_INNER // SSM_GROUPS
_HPG = SSM_HEADS // SSM_GROUPS
_PH = SSM_D_INNER // SSM_HEADS


def _head_masks(rows):
    lane = lax.broadcasted_iota(jnp.int32, (rows, _GP), 1)
    return [jnp.logical_and(lane >= r * _PH, lane < (r + 1) * _PH) for r in range(_HPG)]


def _ssd_specs(idx):
    Q, N = SSM_CHUNK, SSM_STATE
    bcol, ccol = SSM_D_INNER // N, SSM_D_INNER // N + SSM_GROUPS
    return dict(
        x=pl.BlockSpec((Q, _GP), lambda j, g: (idx(j), g)),
        B=pl.BlockSpec((Q, N), lambda j, g: (idx(j), bcol + g)),
        C=pl.BlockSpec((Q, N), lambda j, g: (idx(j), ccol + g)),
        col=pl.BlockSpec((Q, LANE), lambda j, g: (idx(j), g)),
        row=pl.BlockSpec((None, _HPG, Q), lambda j, g: (g, 0, idx(j))),
        st=pl.BlockSpec((N, _GP), lambda j, g: (idx(j), g)),
    )


def _ssd_scan_fwd(xbc, dtc, acc_, dtr, acr, *, name):
    T = xbc.shape[0]
    Q, N = SSM_CHUNK, SSM_STATE
    nc = T // Q
    sp = _ssd_specs(lambda j: j)

    def kern(x_ref, B_ref, C_ref, dtc_ref, ac_ref, dtr_ref, ar_ref, ys_ref, st_ref, state_ref):
        j = pl.program_id(0)
        g = pl.program_id(1)

        @pl.when(j == 0)
        def _():
            state_ref[g] = jnp.zeros((N, _GP), F32)

        S = state_ref[g]
        st_ref[...] = S
        x = x_ref[...]
        xb = x.astype(BF16)
        Bb = B_ref[...].astype(BF16)
        Cb = C_ref[...].astype(BF16)
        CB = _dot(Cb, Bb, _NT)
        r_i = lax.broadcasted_iota(jnp.int32, (Q, Q), 0)
        c_i = lax.broadcasted_iota(jnp.int32, (Q, Q), 1)
        tri = c_i <= r_i
        masks = _head_masks(Q)
        masks1 = _head_masks(1)
        y = jnp.zeros((Q, _GP), F32)
        El = jnp.zeros((Q, _GP), F32)
        Wl = jnp.zeros((Q, _GP), F32)
        decl = jnp.zeros((1, _GP), F32)
        for r in range(_HPG):
            a_c = ac_ref[:, r:r + 1]
            a_r = ar_ref[r:r + 1, :]
            dt_c = dtc_ref[:, r:r + 1]
            dt_r = dtr_ref[r:r + 1, :]
            L = jnp.exp(jnp.where(tri, a_c - a_r, NEG))
            W = CB * L * dt_r
            y = jnp.where(masks[r], _dot(W.astype(BF16), xb), y)
            a_q = a_c[Q - 1:Q, :]
            El = jnp.where(masks[r], jnp.exp(a_c), El)
            Wl = jnp.where(masks[r], jnp.exp(a_q - a_c) * dt_c, Wl)
            decl = jnp.where(masks1[r], jnp.exp(a_q), decl)
        ys_ref[...] = y + _dot(Cb, S.astype(BF16)) * El
        state_ref[g] = S * decl + _dot(Bb, (x * Wl).astype(BF16), _TN)

    return pl.pallas_call(
        kern, name=name, grid=(nc, SSM_GROUPS),
        in_specs=[sp["x"], sp["B"], sp["C"], sp["col"], sp["col"], sp["row"], sp["row"]],
        out_specs=[sp["x"], sp["st"]],
        out_shape=[jax.ShapeDtypeStruct((T, SSM_D_INNER), F32), jax.ShapeDtypeStruct((nc * N, SSM_D_INNER), F32)],
        scratch_shapes=[pltpu.VMEM((SSM_GROUPS, N, _GP), F32)],
        compiler_params=_params(("arbitrary", "arbitrary")),
    )(xbc, xbc, xbc, dtc, acc_, dtr, acr)


def _ssd_scan_bwd(xbc, dys, dskip, st, dtc, acc_, dtr, acr, *, name):
    T = xbc.shape[0]
    Q, N = SSM_CHUNK, SSM_STATE
    nc = T // Q
    sp = _ssd_specs(lambda j: nc - 1 - j)

    def kern(x_ref, B_ref, C_ref, dy_ref, dsk_ref, st_ref, dtc_ref, ac_ref, dtr_ref, ar_ref,
             dx_ref, dB_ref, dC_ref, dac_ref, dar_ref, ddc_ref, ddr_ref, dstate_ref):
        j = pl.program_id(0)
        g = pl.program_id(1)

        @pl.when(j == 0)
        def _():
            dstate_ref[g] = jnp.zeros((N, _GP), F32)

        dS = dstate_ref[g]
        dSb = dS.astype(BF16)
        S = st_ref[...]
        Sb = S.astype(BF16)
        x = x_ref[...]
        xb = x.astype(BF16)
        Bb = B_ref[...].astype(BF16)
        Cb = C_ref[...].astype(BF16)
        dy = dy_ref[...]
        CB = _dot(Cb, Bb, _NT)
        BdS = _dot(Bb, dSb)
        hx = BdS * x
        yd = _dot(Cb, Sb) * dy
        dSS = dS * S
        r_i = lax.broadcasted_iota(jnp.int32, (Q, Q), 0)
        c_i = lax.broadcasted_iota(jnp.int32, (Q, Q), 1)
        tri = c_i <= r_i
        last_row = lax.broadcasted_iota(jnp.int32, (Q, 1), 0) == Q - 1
        lane128 = lax.broadcasted_iota(jnp.int32, (Q, LANE), 1)
        masks = _head_masks(Q)
        masksN = _head_masks(N)
        masks1 = _head_masks(1)
        zeros = jnp.zeros((Q, _GP), F32)
        dxi, El, Wl = zeros, zeros, zeros
        decl = jnp.zeros((1, _GP), F32)
        dBacc = jnp.zeros((Q, N), F32)
        dCacc = jnp.zeros((Q, N), F32)
        dacol = jnp.zeros((Q, LANE), F32)
        ddcol = jnp.zeros((Q, LANE), F32)
        for r in range(_HPG):
            hm = masks[r]
            a_c = ac_ref[:, r:r + 1]
            a_r = ar_ref[r:r + 1, :]
            dt_c = dtc_ref[:, r:r + 1]
            dt_r = dtr_ref[r:r + 1, :]
            L = jnp.exp(jnp.where(tri, a_c - a_r, NEG))
            GL = CB * L
            W = GL * dt_r
            dym = jnp.where(hm, dy, 0.0).astype(BF16)
            dW = _dot(dym, xb, _NT)
            E = dW * W
            da_c = jnp.sum(E, axis=1, keepdims=True)
            dar_ref[r:r + 1, :] = -jnp.sum(E, axis=0, keepdims=True)
            ddr_ref[r:r + 1, :] = jnp.sum(dW * GL, axis=0, keepdims=True)
            dGb = (dW * L * dt_r).astype(BF16)
            dCacc = dCacc + _dot(dGb, Bb)
            dBacc = dBacc + _dot(dGb, Cb, _TN)
            dxi = dxi + _dot(W.astype(BF16), dym, _TN)
            a_q = a_c[Q - 1:Q, :]
            e_c = jnp.exp(a_c)
            eq_c = jnp.exp(a_q - a_c)
            w_c = eq_c * dt_c
            ydr = jnp.sum(jnp.where(hm, yd, 0.0), axis=1, keepdims=True) * e_c
            h_c = jnp.sum(jnp.where(hm, hx, 0.0), axis=1, keepdims=True)
            hw = h_c * w_c
            dss = jnp.sum(jnp.sum(jnp.where(masksN[r], dSS, 0.0), axis=1, keepdims=True), axis=0, keepdims=True)
            s_q = jnp.sum(hw, axis=0, keepdims=True) + jnp.exp(a_q) * dss
            da_c = da_c + ydr - hw + jnp.where(last_row, s_q, 0.0)
            dacol = jnp.where(lane128 == r, da_c, dacol)
            ddcol = jnp.where(lane128 == r, h_c * eq_c, ddcol)
            El = jnp.where(hm, e_c, El)
            Wl = jnp.where(hm, w_c, Wl)
            decl = jnp.where(masks1[r], jnp.exp(a_q), decl)
        dx_ref[...] = dxi + BdS * Wl + dsk_ref[...]
        dB_ref[...] = dBacc + _dot((x * Wl).astype(BF16), dSb, _NT)
        dyE = (dy * El).astype(BF16)
        dC_ref[...] = dCacc + _dot(dyE, Sb, _NT)
        dac_ref[...] = dacol
        ddc_ref[...] = ddcol
        dstate_ref[g] = dS * decl + _dot(Cb, dyE, _TN)

    idx = lambda j: nc - 1 - j
    bcblk = pl.BlockSpec((Q, N), lambda j, g: (idx(j), g))
    return pl.pallas_call(
        kern, name=name, grid=(nc, SSM_GROUPS),
        in_specs=[sp["x"], sp["B"], sp["C"], sp["x"], sp["x"], sp["st"], sp["col"], sp["col"], sp["row"], sp["row"]],
        out_specs=[sp["x"], bcblk, bcblk, sp["col"], sp["row"], sp["col"], sp["row"]],
        out_shape=[jax.ShapeDtypeStruct((T, SSM_D_INNER), F32),
                   jax.ShapeDtypeStruct((T, SSM_GROUPS * N), F32), jax.ShapeDtypeStruct((T, SSM_GROUPS * N), F32),
                   jax.ShapeDtypeStruct((T, SSM_GROUPS * LANE), F32), jax.ShapeDtypeStruct((SSM_GROUPS, _HPG, T), F32),
                   jax.ShapeDtypeStruct((T, SSM_GROUPS * LANE), F32), jax.ShapeDtypeStruct((SSM_GROUPS, _HPG, T), F32)],
        scratch_shapes=[pltpu.VMEM((SSM_GROUPS, N, _GP), F32)],
        compiler_params=_params(("arbitrary", "arbitrary")),
    )(xbc, xbc, xbc, dys, dskip, st, dtc, acc_, dtr, acr)


def _gate_norm_fwd(ys, xbc, proj, d_exp, norm_w, *, name):
    T = ys.shape[0]
    bt = _pick(T, ROW_BLOCK, SUBLANE)

    def kern(ys_ref, x_ref, z_ref, d_ref, w_ref, o_ref):
        z = z_ref[...]
        yz = (ys_ref[...] + d_ref[...] * x_ref[...]) * (z * _sigmoid(z))
        rstd = lax.rsqrt(jnp.mean(yz * yz, axis=-1, keepdims=True) + RMS_EPS)
        o_ref[...] = yz * rstd * w_ref[...]

    blk = pl.BlockSpec((bt, _GP), lambda i, g: (i, g))
    vec = pl.BlockSpec((1, _GP), lambda i, g: (0, g))
    return pl.pallas_call(
        kern, name=name, grid=(T // bt, SSM_GROUPS), in_specs=[blk, blk, blk, vec, vec], out_specs=blk,
        out_shape=jax.ShapeDtypeStruct((T, SSM_D_INNER), F32), compiler_params=_params(("parallel", "parallel")),
    )(ys, xbc, proj, d_exp, norm_w)


def _gate_norm_bwd(dyn, ys, xbc, proj, d_exp, norm_w, *, name):
    T = ys.shape[0]
    bt = _pick(T, ROW_BLOCK, SUBLANE)

    def kern(dyn_ref, ys_ref, x_ref, z_ref, d_ref, w_ref, dz_ref, dys_ref, dsk_ref, dw_ref, dd_ref):
        i = pl.program_id(1)
        z = z_ref[...]
        x = x_ref[...]
        sg = _sigmoid(z)
        sz = z * sg
        y = ys_ref[...] + d_ref[...] * x
        yz = y * sz
        rstd = lax.rsqrt(jnp.mean(yz * yz, axis=-1, keepdims=True) + RMS_EPS)
        yhat = yz * rstd
        dynv = dyn_ref[...]
        gg = dynv * w_ref[...]
        dyz = rstd * (gg - yhat * jnp.mean(gg * yhat, axis=-1, keepdims=True))
        dy = dyz * sz
        dz_ref[...] = dyz * y * sg * (1.0 + z * (1.0 - sg))
        dys_ref[...] = dy
        dsk_ref[...] = dy * d_ref[...]

        @pl.when(i == 0)
        def _():
            dw_ref[...] = jnp.zeros_like(dw_ref)
            dd_ref[...] = jnp.zeros_like(dd_ref)

        dw_ref[...] += jnp.sum(dynv * yhat, axis=0, keepdims=True)
        dd_ref[...] += jnp.sum(dy * x, axis=0, keepdims=True)

    blk = pl.BlockSpec((bt, _GP), lambda g, i: (i, g))
    vec = pl.BlockSpec((1, _GP), lambda g, i: (0, g))
    act = jax.ShapeDtypeStruct((T, SSM_D_INNER), F32)
    par = jax.ShapeDtypeStruct((1, SSM_D_INNER), F32)
    return pl.pallas_call(
        kern, name=name, grid=(SSM_GROUPS, T // bt), in_specs=[blk, blk, blk, blk, vec, vec],
        out_specs=[blk, blk, blk, vec, vec], out_shape=[act, act, act, par, par],
        compiler_params=_params(("parallel", "arbitrary")),
    )(dyn, ys, xbc, proj, d_exp, norm_w)


def _loss_head(y, target, *, name):
    T, D = y.shape
    bt = _pick(T, ROW_BLOCK, SUBLANE)

    def kern(y_ref, t_ref, l_ref, dy_ref):
        i = pl.program_id(0)
        err = y_ref[...] - t_ref[...]
        dy_ref[...] = err * (1.0 / D)

        @pl.when(i == 0)
        def _():
            l_ref[...] = jnp.zeros_like(l_ref)

        l_ref[...] += jnp.sum(err * err, axis=0, keepdims=True) * (0.5 / D)

    row = pl.BlockSpec((bt, D), lambda i: (i, 0))
    vec = pl.BlockSpec((1, D), lambda i: (0, 0))
    return pl.pallas_call(
        kern, name=name, grid=(T // bt,), in_specs=[row, row], out_specs=[vec, row],
        out_shape=[jax.ShapeDtypeStruct((1, D), F32), jax.ShapeDtypeStruct((T, D), F32)],
        compiler_params=_params(("arbitrary",)),
    )(y, target)


def _adamw(w, g, m, v, *, name):
    R, C = w.shape
    br = _pick(R, 512, SUBLANE)

    def kern(w_ref, g_ref, m_ref, v_ref, d_ref, nm_ref, nv_ref):
        gv = g_ref[...]
        nm = ADAM_B1 * m_ref[...] + (1.0 - ADAM_B1) * gv
        nv = ADAM_B2 * v_ref[...] + (1.0 - ADAM_B2) * (gv * gv)
        m_hat = nm / (1.0 - ADAM_B1 ** ADAM_STEP)
        v_hat = nv / (1.0 - ADAM_B2 ** ADAM_STEP)
        d_ref[...] = -ADAM_LR * (m_hat / (jnp.sqrt(v_hat) + ADAM_EPS) + ADAM_WD * w_ref[...])
        nm_ref[...] = nm
        nv_ref[...] = nv

    blk = pl.BlockSpec((br, C), lambda i: (i, 0))
    return pl.pallas_call(
        kern, name=name, grid=(R // br,), in_specs=[blk] * 4, out_specs=[blk] * 3,
        out_shape=[jax.ShapeDtypeStruct((R, C), F32)] * 3, compiler_params=_params(("parallel",)),
    )(w, g, m, v)


def _add2(a, b, *, name):
    shape = a.shape
    a2, b2 = a.reshape(-1, shape[-1]), b.reshape(-1, shape[-1])
    R, C = a2.shape
    br = _pick(R, 512, SUBLANE)

    def kern(a_ref, b_ref, o_ref):
        o_ref[...] = a_ref[...] + b_ref[...]

    blk = pl.BlockSpec((br, C), lambda i: (i, 0))
    return pl.pallas_call(
        kern, name=name, grid=(R // br,), in_specs=[blk, blk], out_specs=blk,
        out_shape=jax.ShapeDtypeStruct((R, C), F32), compiler_params=_params(("parallel",)),
    )(a2, b2).reshape(shape)


def _sum4(buf, *, name):
    _, R, C = buf.shape
    br = _pick(R, 512, SUBLANE)

    def kern(b_ref, o_ref):
        o_ref[...] = ((b_ref[0] + b_ref[1]) + b_ref[2]) + b_ref[3]

    return pl.pallas_call(
        kern, name=name, grid=(R // br,), in_specs=[pl.BlockSpec((4, br, C), lambda i: (0, i, 0))],
        out_specs=pl.BlockSpec((br, C), lambda i: (i, 0)),
        out_shape=jax.ShapeDtypeStruct((R, C), F32), compiler_params=_params(("parallel",)),
    )(buf)


_ANY = pl.BlockSpec(memory_space=pl.ANY)


def _place():
    x, y, c = lax.axis_index("x"), lax.axis_index("y"), lax.axis_index("c")
    other_chips = [(1 - x, y), (x, 1 - y), (1 - x, 1 - y)]
    return x, y, c, other_chips


def _gather_chips(w, *, name):
    R, C = w.shape
    H = R // 2

    def body(w_ref, out_ref, send_sems, recv_sems, local_sem):
        x, y, c, chips = _place()
        me_chip = 2 * x + y
        sib = (x, y, 1 - c)

        def rows(chip, hc):
            return out_ref.at[chip, pl.ds(hc * H, H), :]

        def copy(k, blk, to, src=None):
            return pltpu.make_async_remote_copy(
                src_ref=blk if src is None else src, dst_ref=blk, send_sem=send_sems.at[k], recv_sem=recv_sems.at[k],
                device_id=to, device_id_type=MESH)

        mine = pltpu.make_async_copy(w_ref, out_ref.at[me_chip], local_sem)
        mine.start()
        first = [copy(j, rows(me_chip, c), (cx, cy, c), src=w_ref.at[pl.ds(c * H, H), :])
                 for j, (cx, cy) in enumerate(chips)]
        for cp in first:
            cp.start()
        passed = []
        for j, (cx, cy) in enumerate(chips):
            blk = rows(2 * cx + cy, c)
            copy(j, blk, (cx, cy, c)).wait_recv()
            fw = copy(3 + j, blk, sib)
            fw.start()
            passed.append(fw)
        for j, (cx, cy) in enumerate(chips):
            copy(3 + j, rows(2 * cx + cy, 1 - c), sib).wait_recv()
        for cp in first + passed:
            cp.wait_send()
        mine.wait()

    return pl.pallas_call(
        body, name=name, in_specs=[_ANY], out_specs=_ANY,
        out_shape=jax.ShapeDtypeStruct((4, R, C), w.dtype),
        scratch_shapes=[pltpu.SemaphoreType.DMA((6,)), pltpu.SemaphoreType.DMA((6,)), pltpu.SemaphoreType.DMA],
    )(w)


def _pair_swap(v, *, name):
    def body(v_ref, out_ref, send_sem, recv_sem):
        x, y, c, _ = _place()
        cp = pltpu.make_async_remote_copy(src_ref=v_ref, dst_ref=out_ref, send_sem=send_sem, recv_sem=recv_sem,
                                          device_id=(x, y, 1 - c), device_id_type=MESH)
        cp.start()
        cp.wait()

    return pl.pallas_call(
        body, name=name, in_specs=[_ANY], out_specs=_ANY, out_shape=jax.ShapeDtypeStruct(v.shape, v.dtype),
        scratch_shapes=[pltpu.SemaphoreType.DMA, pltpu.SemaphoreType.DMA],
    )(v)


def _chip_exchange(pv, *, name):
    def body(p_ref, out_ref, send_sems, recv_sems, local_sem):
        x, y, c, chips = _place()
        me_chip = 2 * x + y
        mine = pltpu.make_async_copy(p_ref.at[me_chip], out_ref.at[me_chip], local_sem)
        mine.start()
        sends = []
        for j, (cx, cy) in enumerate(chips):
            cp = pltpu.make_async_remote_copy(
                src_ref=p_ref.at[2 * cx + cy], dst_ref=out_ref.at[me_chip], send_sem=send_sems.at[j],
                recv_sem=recv_sems.at[j], device_id=(cx, cy, c), device_id_type=MESH)
            cp.start()
            sends.append(cp)
        for j, (cx, cy) in enumerate(chips):
            blk = out_ref.at[2 * cx + cy]
            pltpu.make_async_remote_copy(src_ref=blk, dst_ref=blk, send_sem=send_sems.at[j], recv_sem=recv_sems.at[j],
                                         device_id=(cx, cy, c), device_id_type=MESH).wait_recv()
        for cp in sends:
            cp.wait_send()
        mine.wait()

    return pl.pallas_call(
        body, name=name, in_specs=[_ANY], out_specs=_ANY, out_shape=jax.ShapeDtypeStruct(pv.shape, pv.dtype),
        scratch_shapes=[pltpu.SemaphoreType.DMA((3,)), pltpu.SemaphoreType.DMA((3,)), pltpu.SemaphoreType.DMA],
    )(pv)


def _pair_gather(f, *, name):
    H, C = f.shape

    def body(f_ref, out_ref, send_sem, recv_sem, local_sem):
        x, y, c, _ = _place()
        mine_rows = out_ref.at[pl.ds(c * H, H), :]
        mine = pltpu.make_async_copy(f_ref, mine_rows, local_sem)
        mine.start()
        cp = pltpu.make_async_remote_copy(src_ref=f_ref, dst_ref=mine_rows, send_sem=send_sem, recv_sem=recv_sem,
                                          device_id=(x, y, 1 - c), device_id_type=MESH)
        cp.start()
        theirs = out_ref.at[pl.ds((1 - c) * H, H), :]
        pltpu.make_async_remote_copy(src_ref=theirs, dst_ref=theirs, send_sem=send_sem, recv_sem=recv_sem,
                                     device_id=(x, y, 1 - c), device_id_type=MESH).wait_recv()
        cp.wait_send()
        mine.wait()

    return pl.pallas_call(
        body, name=name, in_specs=[_ANY], out_specs=_ANY, out_shape=jax.ShapeDtypeStruct((2 * H, C), f.dtype),
        scratch_shapes=[pltpu.SemaphoreType.DMA, pltpu.SemaphoreType.DMA, pltpu.SemaphoreType.DMA],
    )(f)


WEIGHTS = [
    ("attn_w_in", 2), ("attn_b_f", None), ("attn_w_out", 1), ("ssm_w_in", 2), ("ssm_conv_w", 2), ("ssm_conv_b", 1),
    ("ssm_dt_bias", None), ("ssm_A_log", None), ("ssm_D", None), ("ssm_norm_w", 1), ("ssm_w_out", 1),
    ("ln_mix_g", None), ("ln_mix_b", None), ("ffn_w_up", 2), ("ffn_conv_w", 2), ("ffn_conv_b", None),
    ("ffn_w_down", 1), ("ln_ffn_g", None), ("ln_ffn_b", None), ("ple_w_proj", 2), ("ple_w_gate", 1),
    ("ple_b_gate", None),
]
N_CHIPS = 4


def _pack(arrays):
    parts = []
    total = 0
    for a in arrays:
        n = a.size
        pad = (-n) % PACK_COLS
        flat = a.reshape(-1)
        parts.append(jnp.pad(flat, (0, pad)) if pad else flat)
        total += n + pad
    rows = total // PACK_COLS
    rpad = (-rows) % PACK_ROW_ALIGN
    if rpad:
        parts.append(jnp.zeros((rpad * PACK_COLS,), F32))
    return jnp.concatenate(parts).reshape(rows + rpad, PACK_COLS)


def _unpack(buf, shapes):
    flat = buf.reshape(-1)
    out = []
    off = 0
    for s in shapes:
        n = math.prod(s)
        out.append(flat[off:off + n].reshape(s))
        off += n + ((-n) % PACK_COLS)
    return out


def _col_layout(a):
    T = a.shape[0]
    g = a[:, :SSM_HEADS].reshape(T, SSM_GROUPS, _HPG)
    return jnp.pad(g, ((0, 0), (0, 0), (0, LANE - _HPG))).reshape(T, SSM_GROUPS * LANE)


def _row_layout(a):
    T = a.shape[0]
    return a[:, :SSM_HEADS].T.reshape(SSM_GROUPS, _HPG, T)


def _from_col_layout(a):
    T = a.shape[0]
    v = a.reshape(T, SSM_GROUPS, LANE)[:, :, :_HPG].reshape(T, SSM_HEADS)
    return jnp.pad(v, ((0, 0), (0, LANE - SSM_HEADS)))


def _from_row_layout(a):
    T = a.shape[-1]
    v = a.reshape(SSM_HEADS, T).T
    return jnp.pad(v, ((0, 0), (0, LANE - SSM_HEADS)))


def _pad_lanes(v, n=LANE):
    return jnp.pad(v, (0, n - v.shape[0])).reshape(1, n)


def _local_step(x, p, target, W):
    T = x.shape[0]
    row = lambda v: v.reshape(1, -1)
    attn_in = jnp.pad(W["attn_w_in"][0], ((0, 0), (0, ATTN_IN_PAD - W["attn_w_in"].shape[2])))
    ssm_in = jnp.pad(W["ssm_w_in"][0], ((0, 0), (0, SSM_IN_PAD - W["ssm_w_in"].shape[2])))
    bf = _pad_lanes(W["attn_b_f"][0])
    dt_bias = _pad_lanes(W["ssm_dt_bias"][0])
    a_log = _pad_lanes(W["ssm_A_log"][0])
    d_exp = jnp.repeat(W["ssm_D"][0], _PH).reshape(1, SSM_D_INNER)
    norm_w = row(W["ssm_norm_w"][0])
    G = {}

    def ffn_ple_fwd(i, xin, mix, tag):
        s = {}
        s["z1"], s["h1"] = _ln_fwd(xin, mix, row(W["ln_mix_g"][i]), row(W["ln_mix_b"][i]), name=f"ln_mix_fwd{tag}")
        s["up"] = _mm(s["h1"], W["ffn_w_up"][i], name=f"ffn_up{tag}")
        s["a"] = _ffn_act_fwd(s["up"], W["ffn_conv_w"][i], row(W["ffn_conv_b"][i]), name=f"ffn_act_fwd{tag}")
        ffn = _mm(s["a"], W["ffn_w_down"][i], name=f"ffn_down{tag}")
        s["z2"], s["h2"] = _ln_fwd(s["h1"], ffn, row(W["ln_ffn_g"][i]), row(W["ln_ffn_b"][i]), name=f"ln_ffn_fwd{tag}")
        s["G"] = _mm(s["h2"], W["ple_w_gate"][i], name=f"ple_gate_mm{tag}")
        s["pp"] = _mm(p[i], W["ple_w_proj"][i], name=f"ple_proj_mm{tag}")
        out = _ple_fwd(s["h2"], s["G"], row(W["ple_b_gate"][i]), s["pp"], name=f"ple_fwd{tag}")
        return out, s

    def ffn_ple_bwd(i, dx, s, tag):
        g = {}
        dG, dpp, g["ple_b_gate"] = _ple_bwd(dx, s["G"], row(W["ple_b_gate"][i]), s["pp"], name=f"ple_bwd{tag}")
        g["ple_w_gate"] = _mm(s["h2"], dG, ta=True, name=f"ple_gate_dw{tag}")
        g["ple_w_proj"] = _mm(p[i], dpp, ta=True, name=f"ple_proj_dw{tag}")
        dh2 = _mm(dG, W["ple_w_gate"][i], tb=True, add=dx, name=f"ple_gate_dx{tag}")
        dz2, g["ln_ffn_g"], g["ln_ffn_b"] = _ln_bwd(dh2, s["z2"], row(W["ln_ffn_g"][i]), name=f"ln_ffn_bwd{tag}")
        da = _mm(dz2, W["ffn_w_down"][i], tb=True, name=f"ffn_down_dx{tag}")
        g["ffn_w_down"] = _mm(s["a"], dz2, ta=True, name=f"ffn_down_dw{tag}")
        du, dgc, g["ffn_conv_b"], g["ffn_conv_w"] = _ffn_act_bwd(
            da, s["up"], W["ffn_conv_w"][i], row(W["ffn_conv_b"][i]), name=f"ffn_act_bwd{tag}")
        dgpre = _dwconv_bwd_data(dgc, W["ffn_conv_w"][i], FFN_CONV, name=f"ffn_conv_bwd{tag}")
        dup = jnp.concatenate([du.astype(BF16), dgpre.astype(BF16)], axis=1)
        g["ffn_w_up"] = _mm(s["h1"], dup, ta=True, name=f"ffn_up_dw{tag}")
        dh1 = _mm(dup, W["ffn_w_up"][i], tb=True, add=dz2, add_scale=DEEPNORM_ALPHA, name=f"ffn_up_dx{tag}")
        dz1, g["ln_mix_g"], g["ln_mix_b"] = _ln_bwd(dh1, s["z1"], row(W["ln_mix_g"][i]), name=f"ln_mix_bwd{tag}")
        return dz1, g

    proj0 = _mm(x, attn_in, name="attn_in")
    c_col = _fox_gate_fwd(proj0, bf, name="fox_gate_fwd")
    cT = (c_col[:, :ATTN_HEADS] * LOG2E).T.reshape(HEAD_PAIRS, 2, T)
    qa, qb, kk, ka, kb, vv, va, vb = _attn_prep(proj0, name="attn_prep")
    o, lsea, lseb = _attn_fwd(qa, qb, kk, va, vb, cT, name="attn_fwd")
    mix0 = _mm(o, W["attn_w_out"][0], name="attn_out")
    x1, s0 = ffn_ple_fwd(0, x, mix0, "0")

    proj1 = _mm(x1, ssm_in, name="ssm_in")
    dt, acum = _ssd_dt_fwd(proj1, dt_bias, a_log, name="ssd_dt_fwd")
    xbc = _conv_silu_fwd(proj1, W["ssm_conv_w"][0], row(W["ssm_conv_b"][0]), name="ssd_conv_fwd")
    dtc, acc_, dtr, acr = _col_layout(dt), _col_layout(acum), _row_layout(dt), _row_layout(acum)
    ys, states = _ssd_scan_fwd(xbc, dtc, acc_, dtr, acr, name="ssd_scan_fwd")
    yn = _gate_norm_fwd(ys, xbc, proj1, d_exp, norm_w, name="ssd_gate_norm_fwd")
    mix1 = _mm(yn, W["ssm_w_out"][0], name="ssm_out")
    x2, s1 = ffn_ple_fwd(1, x1, mix1, "1")

    lpart, dy = _loss_head(x2, target, name="loss_head")
    loss = jnp.sum(lpart)

    dz1, g1 = ffn_ple_bwd(1, dy, s1, "1")
    G["ssm_w_out"] = _mm(yn, dz1, ta=True, name="ssm_out_dw")[None]
    dyn = _mm(dz1, W["ssm_w_out"][0], tb=True, name="ssm_out_dx")
    dzg, dys, dskip, dnw, dde = _gate_norm_bwd(dyn, ys, xbc, proj1, d_exp, norm_w, name="ssd_gate_norm_bwd")
    G["ssm_norm_w"] = dnw
    G["ssm_D"] = dde.reshape(SSM_HEADS, _PH).sum(axis=1)[None]
    dxs, dB, dC, dac, dar, ddc, ddr = _ssd_scan_bwd(xbc, dys, dskip, states, dtc, acc_, dtr, acr, name="ssd_scan_bwd")
    draw, dal, ddb = _ssd_dt_bwd(_from_col_layout(dac), _from_row_layout(dar), _from_col_layout(ddc),
                                 _from_row_layout(ddr), dt, proj1, dt_bias, a_log, name="ssd_dt_bwd")
    G["ssm_A_log"] = dal[:, :SSM_HEADS]
    G["ssm_dt_bias"] = ddb[:, :SSM_HEADS]
    dxbc = jnp.concatenate([dxs, dB, dC], axis=1)
    dpre, G["ssm_conv_b"], dcw = _conv_silu_bwd(dxbc, proj1, W["ssm_conv_w"][0], row(W["ssm_conv_b"][0]),
                                                name="ssd_conv_bwd")
    G["ssm_conv_w"] = dcw[None]
    dxbc_pre = _dwconv_bwd_data(dpre, W["ssm_conv_w"][0], SSM_CONV, name="ssd_conv_bwd_data")
    dproj1 = jnp.concatenate([dzg.astype(BF16), dxbc_pre.astype(BF16), draw.astype(BF16)], axis=1)
    G["ssm_w_in"] = _mm(x1, dproj1, ta=True, name="ssm_in_dw")[None, :, :W["ssm_w_in"].shape[2]]
    dx1 = _mm(dproj1, ssm_in, tb=True, add=dz1, add_scale=DEEPNORM_ALPHA, name="ssm_in_dx")

    dz0, g0 = ffn_ple_bwd(0, dx1, s0, "0")
    G["attn_w_out"] = _mm(o, dz0, ta=True, name="attn_out_dw")[None]
    do = _mm(dz0, W["attn_w_out"][0], tb=True, name="attn_out_dx")
    doa, dob, dlta, dltb = _attn_bwd_prep(do, o, name="attn_bwd_prep")
    dq, dk, dv, dcT, dcq = _attn_bwd(qa, qb, kk, ka, kb, vv, doa, dob, lsea, lseb, dlta, dltb, cT, name="attn_bwd")
    dcq = dcq.reshape(T, HEAD_PAIRS, 2, HALF)[:, :, :, 0].reshape(T, ATTN_HEADS)
    dc_col = jnp.pad(dcT.reshape(ATTN_HEADS, T).T + dcq, ((0, 0), (0, LANE - ATTN_HEADS)))
    dfl, dbf = _fox_gate_bwd(dc_col, proj0, bf, name="fox_gate_bwd")
    G["attn_b_f"] = dbf[:, :ATTN_HEADS]
    dproj0 = jnp.concatenate([dq.astype(BF16), dk.astype(BF16), dv.astype(BF16), dfl.astype(BF16)], axis=1)
    G["attn_w_in"] = _mm(x, dproj0, ta=True, name="attn_in_dw")[None, :, :W["attn_w_in"].shape[2]]
    grad_x = _mm(dproj0, attn_in, tb=True, add=dz0, add_scale=DEEPNORM_ALPHA, name="attn_in_dx")

    for k in g0:
        G[k] = jnp.stack([g0[k].reshape(W[k].shape[1:]), g1[k].reshape(W[k].shape[1:])])
    return loss, grad_x, G


def kernel(x, p, attn_w_in, attn_b_f, attn_w_out, ssm_w_in, ssm_conv_w, ssm_conv_b, ssm_dt_bias, ssm_A_log, ssm_D, ssm_norm_w, ssm_w_out, ln_mix_g, ln_mix_b, ffn_w_up, ffn_conv_w, ffn_conv_b, ffn_w_down, ln_ffn_g, ln_ffn_b, ple_w_proj, ple_w_gate, ple_b_gate, loss_target, m_attn_w_in, m_attn_b_f, m_attn_w_out, m_ssm_w_in, m_ssm_conv_w, m_ssm_conv_b, m_ssm_dt_bias, m_ssm_A_log, m_ssm_D, m_ssm_norm_w, m_ssm_w_out, m_ln_mix_g, m_ln_mix_b, m_ffn_w_up, m_ffn_conv_w, m_ffn_conv_b, m_ffn_w_down, m_ln_ffn_g, m_ln_ffn_b, m_ple_w_proj, m_ple_w_gate, m_ple_b_gate, v_attn_w_in, v_attn_b_f, v_attn_w_out, v_ssm_w_in, v_ssm_conv_w, v_ssm_conv_b, v_ssm_dt_bias, v_ssm_A_log, v_ssm_D, v_ssm_norm_w, v_ssm_w_out, v_ln_mix_g, v_ln_mix_b, v_ffn_w_up, v_ffn_conv_w, v_ffn_conv_b, v_ffn_w_down, v_ln_ffn_g, v_ln_ffn_b, v_ple_w_proj, v_ple_w_gate, v_ple_b_gate):
    names = [n for n, _ in WEIGHTS]
    axes = dict(WEIGHTS)
    w_loc = dict(zip(names, [attn_w_in, attn_b_f, attn_w_out, ssm_w_in, ssm_conv_w, ssm_conv_b, ssm_dt_bias, ssm_A_log, ssm_D, ssm_norm_w, ssm_w_out, ln_mix_g, ln_mix_b, ffn_w_up, ffn_conv_w, ffn_conv_b, ffn_w_down, ln_ffn_g, ln_ffn_b, ple_w_proj, ple_w_gate, ple_b_gate]))
    m_loc = dict(zip(names, [m_attn_w_in, m_attn_b_f, m_attn_w_out, m_ssm_w_in, m_ssm_conv_w, m_ssm_conv_b, m_ssm_dt_bias, m_ssm_A_log, m_ssm_D, m_ssm_norm_w, m_ssm_w_out, m_ln_mix_g, m_ln_mix_b, m_ffn_w_up, m_ffn_conv_w, m_ffn_conv_b, m_ffn_w_down, m_ln_ffn_g, m_ln_ffn_b, m_ple_w_proj, m_ple_w_gate, m_ple_b_gate]))
    v_loc = dict(zip(names, [v_attn_w_in, v_attn_b_f, v_attn_w_out, v_ssm_w_in, v_ssm_conv_w, v_ssm_conv_b, v_ssm_dt_bias, v_ssm_A_log, v_ssm_D, v_ssm_norm_w, v_ssm_w_out, v_ln_mix_g, v_ln_mix_b, v_ffn_w_up, v_ffn_conv_w, v_ffn_conv_b, v_ffn_w_down, v_ln_ffn_g, v_ln_ffn_b, v_ple_w_proj, v_ple_w_gate, v_ple_b_gate]))
    sharded = [n for n in names if axes[n] is not None]

    gathered = _gather_chips(_pack([w_loc[n] for n in sharded]), name="gather_weights")
    W = dict(w_loc)
    per_chip = [_unpack(gathered[k], [w_loc[n].shape for n in sharded]) for k in range(N_CHIPS)]
    for i, n in enumerate(sharded):
        W[n] = jnp.concatenate([per_chip[k][i] for k in range(N_CHIPS)], axis=axes[n])

    loss, grad_x, G = _local_step(x[0], p[:, 0], loss_target[0], W)
    loss = lax.psum(loss, ("x", "y", "c"))

    def slot(k):
        parts = []
        for n in names:
            g = G[n].reshape(W[n].shape)
            if axes[n] is not None:
                size = w_loc[n].shape[axes[n]]
                g = lax.slice_in_dim(g, k * size, (k + 1) * size, axis=axes[n])
            parts.append(g)
        return _pack(parts)

    contrib = jnp.stack([slot(k) for k in range(N_CHIPS)])
    R = contrib.shape[1]
    H = R // 2
    c = lax.axis_index("c")
    keep = lax.dynamic_slice_in_dim(contrib, c * H, H, axis=1)
    give = lax.dynamic_slice_in_dim(contrib, (1 - c) * H, H, axis=1)
    pair = _add2(keep, _pair_swap(give, name="grad_pair_swap"), name="grad_pair_sum")
    half = _sum4(_chip_exchange(pair, name="grad_chip_exchange"), name="grad_chip_sum")
    gflat = _pair_gather(half, name="grad_pair_gather")

    shapes = [w_loc[n].shape for n in names]
    delta, new_m, new_v = _adamw(_pack([w_loc[n] for n in names]), gflat, _pack([m_loc[n] for n in names]),
                                 _pack([v_loc[n] for n in names]), name="adamw")
    return (loss, grad_x[None], *_unpack(gflat, shapes), *_unpack(delta, shapes), *_unpack(new_m, shapes),
            *_unpack(new_v, shapes))
```

```python
import functools
import math

import jax
import jax.numpy as jnp
from jax import lax
from jax.experimental import pallas as pl
from jax.experimental.pallas import tpu as pltpu

F32 = jnp.float32
BF16 = jnp.bfloat16
MESH = pl.DeviceIdType.MESH

D_MODEL = 1024
ATTN_HEADS = 16
HEAD_PAIRS = ATTN_HEADS // 2
SSM_D_INNER = 2048
SSM_HEADS = 32
SSM_GROUPS = 8
SSM_STATE = 128
SSM_CONV = 4
SSM_CHUNK = 128
SSM_XBC = SSM_D_INNER + 2 * SSM_GROUPS * SSM_STATE
FFN_DIM = 2816
FFN_CONV = 3
DEPTH = 2
LN_EPS = 1e-5
RMS_EPS = 1e-5
DEEPNORM_ALPHA = (2 * DEPTH) ** 0.25
ADAM_LR = 0.001
ADAM_B1 = 0.9
ADAM_B2 = 0.999
ADAM_EPS = 1e-08
ADAM_WD = 0.01
ADAM_STEP = 10

LANE = 128
SUBLANE = 8
HALO = SUBLANE
NEG = -1e30
ATTN_IN_PAD = 3 * D_MODEL + LANE
SSM_IN_PAD = 2 * SSM_D_INNER + 2 * SSM_GROUPS * SSM_STATE + LANE
PACK_COLS = 1024
PACK_ROW_ALIGN = 512

ATTN_BLOCK = 512
ROW_BLOCK = 256
CUM_BLOCK = 256


def _params(sem, vmem_mb=48):
    return pltpu.CompilerParams(dimension_semantics=sem, vmem_limit_bytes=vmem_mb * 2 ** 20)


def _pick(n, target, mult=LANE):
    best = None
    d = mult
    while d <= min(n, target):
        if n % d == 0:
            best = d
        d += mult
    return n if best is None else best


def _sigmoid(x):
    return 1.0 / (1.0 + jnp.exp(-x))


def _log1p(u):
    w = 1.0 + u
    return jnp.where(w == 1.0, u, jnp.log(w) * (u / (w - 1.0)))


def _softplus(x):
    return jnp.maximum(x, 0.0) + _log1p(jnp.exp(-jnp.abs(x)))


def _split3(x):
    hi = x.astype(BF16)
    r1 = x - hi.astype(F32)
    mid = r1.astype(BF16)
    lo = (r1 - mid.astype(F32)).astype(BF16)
    return hi, mid, lo


def _tri_matmul(tri, x):
    out = None
    for part in _split3(x):
        t = jnp.dot(tri, part, preferred_element_type=F32)
        out = t if out is None else out + t
    return out


def _tri(n, lower):
    r = lax.broadcasted_iota(jnp.int32, (n, n), 0)
    c = lax.broadcasted_iota(jnp.int32, (n, n), 1)
    return jnp.where((c <= r) if lower else (c >= r), 1.0, 0.0).astype(BF16)


_ANY = pl.BlockSpec(memory_space=pl.ANY)
MM_OUT_BLOCK_BYTES = 13 * 2 ** 20
MM_FULL_K = 2048


def _mm(a, b, *, name, ta=False, tb=False, add=None, add_scale=1.0):
    if ta:
        K, M = a.shape
    else:
        M, K = a.shape
    if tb:
        N, Kb = b.shape
    else:
        Kb, N = b.shape
    assert K == Kb, (a.shape, b.shape, ta, tb)
    if ta:
        assert add is None
        bm = _pick(M, 2816)
        bn = _pick(N, MM_OUT_BLOCK_BYTES // (4 * bm))
        bk = _pick(K, 512)
    else:
        bm = _pick(M, 1024)
        bn = _pick(N, 512)
        bk = K if K <= MM_FULL_K else _pick(K, 1024)
    nk = K // bk
    a_spec = pl.BlockSpec((bk, bm), lambda i, j, k: (k, i)) if ta else pl.BlockSpec((bm, bk), lambda i, j, k: (i, k))
    b_spec = pl.BlockSpec((bn, bk), lambda i, j, k: (j, k)) if tb else pl.BlockSpec((bk, bn), lambda i, j, k: (k, j))
    o_spec = pl.BlockSpec((bm, bn), lambda i, j, k: (i, j))
    dims = (((0 if ta else 1,), (1 if tb else 0,)), ((), ()))
    has_add = add is not None
    use_acc = nk > 1 and not ta

    def kern(*refs):
        a_ref, b_ref = refs[0], refs[1]
        add_ref = refs[2] if has_add else None
        o_ref = refs[3] if has_add else refs[2]
        k = pl.program_id(2)
        part = lax.dot_general(a_ref[...].astype(BF16), b_ref[...].astype(BF16), dims, preferred_element_type=F32)

        def finish(r):
            return r + add_scale * add_ref[...] if has_add else r

        if nk == 1:
            o_ref[...] = finish(part)
        elif not use_acc:
            @pl.when(k == 0)
            def _():
                o_ref[...] = part

            @pl.when(k > 0)
            def _():
                o_ref[...] += part
        else:
            acc_ref = refs[-1]

            @pl.when(k == 0)
            def _():
                acc_ref[...] = part

            @pl.when(jnp.logical_and(k > 0, k < nk - 1))
            def _():
                acc_ref[...] += part

            @pl.when(k == nk - 1)
            def _():
                o_ref[...] = finish(acc_ref[...] + part)

    ins = [a, b] + ([add] if has_add else [])
    in_specs = [a_spec, b_spec] + ([o_spec] if has_add else [])
    return pl.pallas_call(
        kern, name=name, grid=(M // bm, N // bn, nk),
        in_specs=in_specs, out_specs=o_spec,
        out_shape=jax.ShapeDtypeStruct((M, N), F32),
        scratch_shapes=[pltpu.VMEM((bm, bn), F32)] if use_acc else [],
        compiler_params=_params(("parallel", "parallel", "arbitrary"), vmem_mb=56),
    )(*ins)


def _ln_stats(z):
    mu = jnp.mean(z, axis=-1, keepdims=True)
    zc = z - mu
    var = jnp.mean(zc * zc, axis=-1, keepdims=True)
    return zc, lax.rsqrt(var + LN_EPS)


def _ln_fwd(x, r, g, b, *, name):
    T, D = x.shape
    bt = _pick(T, ROW_BLOCK, SUBLANE)

    def kern(x_ref, r_ref, g_ref, b_ref, z_ref, h_ref, hb_ref):
        z = DEEPNORM_ALPHA * x_ref[...] + r_ref[...]
        zc, rstd = _ln_stats(z)
        h = zc * rstd * g_ref[...] + b_ref[...]
        z_ref[...] = z
        h_ref[...] = h
        hb_ref[...] = h.astype(BF16)

    row = pl.BlockSpec((bt, D), lambda i: (i, 0))
    vec = pl.BlockSpec((1, D), lambda i: (0, 0))
    return pl.pallas_call(
        kern, name=name, grid=(T // bt,), in_specs=[row, row, vec, vec], out_specs=[row, row, row],
        out_shape=[jax.ShapeDtypeStruct((T, D), F32)] * 2 + [jax.ShapeDtypeStruct((T, D), BF16)],
        compiler_params=_params(("parallel",)),
    )(x, r, g, b)


def _ln_bwd(dy, z, g, *, name):
    T, D = z.shape
    bt = _pick(T, ROW_BLOCK, SUBLANE)

    def kern(dy_ref, z_ref, g_ref, dz_ref, dzb_ref, dg_ref, db_ref):
        i = pl.program_id(0)
        zc, rstd = _ln_stats(z_ref[...])
        xhat = zc * rstd
        dyv = dy_ref[...]
        dxh = dyv * g_ref[...]
        m1 = jnp.mean(dxh, axis=-1, keepdims=True)
        m2 = jnp.mean(dxh * xhat, axis=-1, keepdims=True)
        dz = rstd * (dxh - m1 - xhat * m2)
        dz_ref[...] = dz
        dzb_ref[...] = dz.astype(BF16)

        @pl.when(i == 0)
        def _():
            dg_ref[...] = jnp.zeros_like(dg_ref)
            db_ref[...] = jnp.zeros_like(db_ref)

        dg_ref[...] += jnp.sum(dyv * xhat, axis=0, keepdims=True)
        db_ref[...] += jnp.sum(dyv, axis=0, keepdims=True)

    row = pl.BlockSpec((bt, D), lambda i: (i, 0))
    vec = pl.BlockSpec((1, D), lambda i: (0, 0))
    return pl.pallas_call(
        kern, name=name, grid=(T // bt,), in_specs=[row, row, vec], out_specs=[row, row, vec, vec],
        out_shape=[jax.ShapeDtypeStruct((T, D), F32), jax.ShapeDtypeStruct((T, D), BF16),
                   jax.ShapeDtypeStruct((1, D), F32), jax.ShapeDtypeStruct((1, D), F32)],
        compiler_params=_params(("arbitrary",)),
    )(dy, z, g)


def _conv_past(ext_ref, cw_ref, K, bt):
    out = None
    for k in range(K):
        term = cw_ref[k:k + 1, :] * ext_ref[pl.ds(HALO - (K - 1) + k, bt), :]
        out = term if out is None else out + term
    return out


def _fill_ext_past(ext_ref, halo_ref, cur, i, bt):
    ext_ref[pl.ds(0, HALO), :] = jnp.where(i > 0, halo_ref[...], 0.0)
    ext_ref[pl.ds(HALO, bt), :] = cur


def _halo_prev(bt, bc, off):
    return pl.BlockSpec((HALO, bc), lambda i, j: (jnp.maximum(i * (bt // HALO) - 1, 0), j + off))


def _gelu(x):
    return 0.5 * x * (1.0 + lax.erf(x * (1.0 / math.sqrt(2.0))))


def _gelu_grad(x):
    return 0.5 * (1.0 + lax.erf(x * (1.0 / math.sqrt(2.0)))) + x * jnp.exp(-0.5 * x * x) * (1.0 / math.sqrt(2.0 * math.pi))


def _ffn_act_fwd(up, cw, cb, *, name):
    T, F2 = up.shape
    F = F2 // 2
    bt = _pick(T, ROW_BLOCK, SUBLANE)
    bc = _pick(F, 1408)
    nb = F // bc

    def kern(u_ref, g_ref, halo_ref, cw_ref, cb_ref, a_ref, ext_ref):
        i = pl.program_id(0)
        _fill_ext_past(ext_ref, halo_ref, g_ref[...], i, bt)
        gc = cb_ref[...] + _conv_past(ext_ref, cw_ref, FFN_CONV, bt)
        a_ref[...] = (_gelu(gc) * u_ref[...]).astype(BF16)

    return pl.pallas_call(
        kern, name=name, grid=(T // bt, nb),
        in_specs=[pl.BlockSpec((bt, bc), lambda i, j: (i, j)),
                  pl.BlockSpec((bt, bc), lambda i, j: (i, j + nb)),
                  _halo_prev(bt, bc, nb),
                  pl.BlockSpec((FFN_CONV, bc), lambda i, j: (0, j)),
                  pl.BlockSpec((1, bc), lambda i, j: (0, j))],
        out_specs=pl.BlockSpec((bt, bc), lambda i, j: (i, j)),
        out_shape=jax.ShapeDtypeStruct((T, F), BF16),
        scratch_shapes=[pltpu.VMEM((bt + HALO, bc), F32)],
        compiler_params=_params(("parallel", "parallel")),
    )(up, up, up, cw, cb)


def _ffn_act_bwd(da, up, cw, cb, *, name):
    T, F2 = up.shape
    F = F2 // 2
    bt = _pick(T, ROW_BLOCK, SUBLANE)
    bc = _pick(F, 1408)
    nb = F // bc
    K = FFN_CONV

    def kern(da_ref, u_ref, g_ref, halo_ref, cw_ref, cb_ref, du_ref, dgc_ref, dcb_ref, dcw_ref, ext_ref):
        i = pl.program_id(1)
        _fill_ext_past(ext_ref, halo_ref, g_ref[...], i, bt)
        gc = cb_ref[...] + _conv_past(ext_ref, cw_ref, K, bt)
        dav = da_ref[...]
        du_ref[...] = (dav * _gelu(gc)).astype(BF16)
        dgc = dav * u_ref[...] * _gelu_grad(gc)
        dgc_ref[...] = dgc

        @pl.when(i == 0)
        def _():
            dcb_ref[...] = jnp.zeros_like(dcb_ref)
            dcw_ref[...] = jnp.zeros_like(dcw_ref)

        dcb_ref[...] += jnp.sum(dgc, axis=0, keepdims=True)
        for k in range(K):
            dcw_ref[k:k + 1, :] += jnp.sum(dgc * ext_ref[pl.ds(HALO - (K - 1) + k, bt), :], axis=0, keepdims=True)

    blk = pl.BlockSpec((bt, bc), lambda j, i: (i, j))
    return pl.pallas_call(
        kern, name=name, grid=(nb, T // bt),
        in_specs=[blk, blk,
                  pl.BlockSpec((bt, bc), lambda j, i: (i, j + nb)),
                  pl.BlockSpec((HALO, bc), lambda j, i: (jnp.maximum(i * (bt // HALO) - 1, 0), j + nb)),
                  pl.BlockSpec((K, bc), lambda j, i: (0, j)),
                  pl.BlockSpec((1, bc), lambda j, i: (0, j))],
        out_specs=[blk, blk, pl.BlockSpec((1, bc), lambda j, i: (0, j)), pl.BlockSpec((K, bc), lambda j, i: (0, j))],
        out_shape=[jax.ShapeDtypeStruct((T, F2), BF16), jax.ShapeDtypeStruct((T, F), F32),
                   jax.ShapeDtypeStruct((1, F), F32), jax.ShapeDtypeStruct((K, F), F32)],
        scratch_shapes=[pltpu.VMEM((bt + HALO, bc), F32)],
        compiler_params=_params(("parallel", "arbitrary")),
    )(da, up, up, up, cw, cb)


def _dwconv_bwd_data(dgc, cw, K, into, col, *, name):
    T, C = dgc.shape
    bt = _pick(T, ROW_BLOCK, SUBLANE)
    bc = _pick(C, 1408)
    nt = T // bt
    last_halo = T // HALO - 1
    off = col // bc
    assert off * bc == col

    def kern(d_ref, halo_ref, cw_ref, into_ref, o_ref, ext_ref):
        i = pl.program_id(0)
        ext_ref[pl.ds(0, bt), :] = d_ref[...]
        ext_ref[pl.ds(bt, HALO), :] = jnp.where(i < nt - 1, halo_ref[...], 0.0)
        out = None
        for k in range(K):
            term = cw_ref[k:k + 1, :] * ext_ref[pl.ds(K - 1 - k, bt), :]
            out = term if out is None else out + term
        o_ref[...] = out.astype(o_ref.dtype)

    return pl.pallas_call(
        kern, name=name, grid=(nt, C // bc),
        in_specs=[pl.BlockSpec((bt, bc), lambda i, j: (i, j)),
                  pl.BlockSpec((HALO, bc), lambda i, j: (jnp.minimum((i + 1) * (bt // HALO), last_halo), j)),
                  pl.BlockSpec((K, bc), lambda i, j: (0, j)), _ANY],
        out_specs=pl.BlockSpec((bt, bc), lambda i, j: (i, j + off)),
        out_shape=jax.ShapeDtypeStruct(into.shape, into.dtype), input_output_aliases={3: 0},
        scratch_shapes=[pltpu.VMEM((bt + HALO, bc), F32)],
        compiler_params=_params(("parallel", "parallel")),
    )(dgc, dgc, cw, into)


def _ple_fwd(h, G, bg, pp, *, name):
    T, D = h.shape
    bt = _pick(T, ROW_BLOCK, SUBLANE)

    def kern(h_ref, G_ref, bg_ref, pp_ref, o_ref, ob_ref):
        out = h_ref[...] + _sigmoid(G_ref[...] + bg_ref[...]) * pp_ref[...]
        o_ref[...] = out
        ob_ref[...] = out.astype(BF16)

    row = pl.BlockSpec((bt, D), lambda i: (i, 0))
    vec = pl.BlockSpec((1, D), lambda i: (0, 0))
    return pl.pallas_call(
        kern, name=name, grid=(T // bt,), in_specs=[row, row, vec, row], out_specs=[row, row],
        out_shape=[jax.ShapeDtypeStruct((T, D), F32), jax.ShapeDtypeStruct((T, D), BF16)],
        compiler_params=_params(("parallel",)),
    )(h, G, bg, pp)


def _ple_bwd(dx, G, bg, pp, *, name):
    T, D = dx.shape
    bt = _pick(T, ROW_BLOCK, SUBLANE)

    def kern(dx_ref, G_ref, bg_ref, pp_ref, dG_ref, dpp_ref, dbg_ref):
        i = pl.program_id(0)
        gate = _sigmoid(G_ref[...] + bg_ref[...])
        dxv = dx_ref[...]
        dG = dxv * pp_ref[...] * gate * (1.0 - gate)
        dG_ref[...] = dG.astype(BF16)
        dpp_ref[...] = (dxv * gate).astype(BF16)

        @pl.when(i == 0)
        def _():
            dbg_ref[...] = jnp.zeros_like(dbg_ref)

        dbg_ref[...] += jnp.sum(dG, axis=0, keepdims=True)

    row = pl.BlockSpec((bt, D), lambda i: (i, 0))
    vec = pl.BlockSpec((1, D), lambda i: (0, 0))
    return pl.pallas_call(
        kern, name=name, grid=(T // bt,), in_specs=[row, row, vec, row], out_specs=[row, row, vec],
        out_shape=[jax.ShapeDtypeStruct((T, D), BF16), jax.ShapeDtypeStruct((T, D), BF16),
                   jax.ShapeDtypeStruct((1, D), F32)],
        compiler_params=_params(("arbitrary",)),
    )(dx, G, bg, pp)


def _fox_gate_fwd(proj, bf, *, name):
    T = proj.shape[0]
    bt = _pick(T, CUM_BLOCK, SUBLANE)
    fcol = 3 * D_MODEL // LANE

    def kern(f_ref, bf_ref, c_ref, carry_ref):
        i = pl.program_id(0)

        @pl.when(i == 0)
        def _():
            carry_ref[...] = jnp.zeros_like(carry_ref)

        x = f_ref[...] + bf_ref[...]
        lf = jnp.minimum(x, 0.0) - _log1p(jnp.exp(-jnp.abs(x)))
        cs = _tri_matmul(_tri(bt, True), lf) + carry_ref[...]
        c_ref[...] = cs
        carry_ref[...] = cs[bt - 1:bt, :]

    return pl.pallas_call(
        kern, name=name, grid=(T // bt,),
        in_specs=[pl.BlockSpec((bt, LANE), lambda i: (i, fcol)), pl.BlockSpec((1, LANE), lambda i: (0, 0))],
        out_specs=pl.BlockSpec((bt, LANE), lambda i: (i, 0)),
        out_shape=jax.ShapeDtypeStruct((T, LANE), F32),
        scratch_shapes=[pltpu.VMEM((1, LANE), F32)],
        compiler_params=_params(("arbitrary",)),
    )(proj, bf)


def _fox_gate_bwd(dc, proj, bf, *, name):
    T = proj.shape[0]
    bt = _pick(T, CUM_BLOCK, SUBLANE)
    nb = T // bt
    fcol = 3 * D_MODEL // LANE

    def kern(dc_ref, f_ref, bf_ref, df_ref, dbf_ref, carry_ref):
        i = pl.program_id(0)

        @pl.when(i == 0)
        def _():
            carry_ref[...] = jnp.zeros_like(carry_ref)
            dbf_ref[...] = jnp.zeros_like(dbf_ref)

        dlf = _tri_matmul(_tri(bt, False), dc_ref[...]) + carry_ref[...]
        carry_ref[...] = dlf[0:1, :]
        x = f_ref[...] + bf_ref[...]
        lane = lax.broadcasted_iota(jnp.int32, (bt, LANE), 1)
        df = jnp.where(lane < ATTN_HEADS, dlf / (1.0 + jnp.exp(x)), 0.0)
        df_ref[...] = df
        dbf_ref[...] += jnp.sum(df, axis=0, keepdims=True)

    return pl.pallas_call(
        kern, name=name, grid=(nb,),
        in_specs=[pl.BlockSpec((bt, LANE), lambda i: (nb - 1 - i, 0)),
                  pl.BlockSpec((bt, LANE), lambda i: (nb - 1 - i, fcol)),
                  pl.BlockSpec((1, LANE), lambda i: (0, 0))],
        out_specs=[pl.BlockSpec((bt, LANE), lambda i: (nb - 1 - i, 0)), pl.BlockSpec((1, LANE), lambda i: (0, 0))],
        out_shape=[jax.ShapeDtypeStruct((T, LANE), F32), jax.ShapeDtypeStruct((1, LANE), F32)],
        scratch_shapes=[pltpu.VMEM((1, LANE), F32)],
        compiler_params=_params(("arbitrary",)),
    )(dc, proj, bf)


_NT = (((1,), (1,)), ((), ()))
_TN = (((0,), (0,)), ((), ()))


def _dot(a, b, dims=None):
    if dims is None:
        return jnp.dot(a, b, preferred_element_type=F32)
    return lax.dot_general(a, b, dims, preferred_element_type=F32)


LOG2E = 1.0 / math.log(2.0)
LN2 = math.log(2.0)
Q_SCALE = 0.125 * LOG2E
HALF = LANE // 2
L_LANE = (HALF, 0)


def _attn_prep(proj, *, name):
    T = proj.shape[0]
    bt = _pick(T, ATTN_BLOCK)

    def kern(q_ref, k_ref, v_ref, qa_ref, qb_ref, kk_ref, ka_ref, kb_ref, vv_ref, va_ref, vb_ref):
        lane = lax.broadcasted_iota(jnp.int32, (bt, LANE), 1)
        lo = lane < HALF
        q = q_ref[...] * Q_SCALE
        k = k_ref[...]
        v = v_ref[...]
        qa_ref[...] = jnp.where(lo, q, 0.0).astype(BF16)
        qb_ref[...] = jnp.where(lo, 0.0, q).astype(BF16)
        kk_ref[...] = k.astype(BF16)
        ka_ref[...] = jnp.where(lo, k, 0.0).astype(BF16)
        kb_ref[...] = jnp.where(lo, 0.0, k).astype(BF16)
        vv_ref[...] = v.astype(BF16)
        va_ref[...] = jnp.where(lo, v, jnp.where(lane == L_LANE[0], 1.0, 0.0)).astype(BF16)
        vb_ref[...] = jnp.where(lo, jnp.where(lane == L_LANE[1], 1.0, 0.0), v).astype(BF16)

    kcol, vcol = D_MODEL // LANE, 2 * D_MODEL // LANE
    out = pl.BlockSpec((bt, LANE), lambda i, hp: (i, hp))
    return pl.pallas_call(
        kern, name=name, grid=(T // bt, HEAD_PAIRS),
        in_specs=[out, pl.BlockSpec((bt, LANE), lambda i, hp: (i, kcol + hp)),
                  pl.BlockSpec((bt, LANE), lambda i, hp: (i, vcol + hp))],
        out_specs=[out] * 8, out_shape=[jax.ShapeDtypeStruct((T, D_MODEL), BF16)] * 8,
        compiler_params=_params(("parallel", "parallel")),
    )(proj, proj, proj)


def _attn_fwd(qa, qb, kk, va, vb, cT, *, name):
    T = qa.shape[0]
    tb = _pick(T, ATTN_BLOCK)
    nq = T // tb
    rep = tb // LANE

    def kern(qa_ref, qb_ref, k_ref, va_ref, vb_ref, c_ref, o_ref, ob_ref, lsea_ref, lseb_ref, m_ref, acc_ref):
        qi = pl.program_id(1)
        ki = pl.program_id(2)

        @pl.when(ki == 0)
        def _():
            m_ref[...] = jnp.full_like(m_ref, NEG)
            acc_ref[...] = jnp.zeros_like(acc_ref)

        def step(diag):
            k = k_ref[...]
            for h, (q_ref, v_ref) in enumerate(((qa_ref, va_ref), (qb_ref, vb_ref))):
                s = _dot(q_ref[...], k, _NT) - c_ref[h:h + 1, :]
                if diag:
                    r = lax.broadcasted_iota(jnp.int32, (tb, tb), 0)
                    c = lax.broadcasted_iota(jnp.int32, (tb, tb), 1)
                    s = jnp.where(c <= r, s, NEG)
                m_prev = m_ref[h]
                m_new = jnp.maximum(m_prev, jnp.max(s, axis=1, keepdims=True))
                p = jnp.exp2(s - jnp.tile(m_new, (1, rep)))
                acc_ref[h] = acc_ref[h] * jnp.exp2(m_prev - m_new) + _dot(p.astype(BF16), v_ref[...])
                m_ref[h] = m_new

        @pl.when(ki < qi)
        def _():
            step(False)

        @pl.when(ki == qi)
        def _():
            step(True)
            lo = lax.broadcasted_iota(jnp.int32, (tb, LANE), 1) < HALF
            a0, a1 = acc_ref[0], acc_ref[1]
            l0 = a0[:, L_LANE[0]:L_LANE[0] + 1]
            l1 = a1[:, L_LANE[1]:L_LANE[1] + 1]
            o = jnp.where(lo, a0 / l0, a1 / l1)
            o_ref[...] = o
            ob_ref[...] = o.astype(BF16)
            lsea_ref[...] = m_ref[0] + jnp.log(l0) * LOG2E
            lseb_ref[...] = m_ref[1] + jnp.log(l1) * LOG2E

    qspec = pl.BlockSpec((tb, LANE), lambda hp, qi, ki: (qi, hp))
    kspec = pl.BlockSpec((tb, LANE), lambda hp, qi, ki: (jnp.minimum(ki, qi), hp))
    return pl.pallas_call(
        kern, name=name, grid=(HEAD_PAIRS, nq, nq),
        in_specs=[qspec, qspec, kspec, kspec, kspec,
                  pl.BlockSpec((None, 2, tb), lambda hp, qi, ki: (hp, 0, jnp.minimum(ki, qi)))],
        out_specs=[qspec, qspec, qspec, qspec],
        out_shape=[jax.ShapeDtypeStruct((T, D_MODEL), F32), jax.ShapeDtypeStruct((T, D_MODEL), BF16),
                   jax.ShapeDtypeStruct((T, D_MODEL), F32), jax.ShapeDtypeStruct((T, D_MODEL), F32)],
        scratch_shapes=[pltpu.VMEM((2, tb, LANE), F32), pltpu.VMEM((2, tb, LANE), F32)],
        compiler_params=_params(("parallel", "parallel", "arbitrary")),
    )(qa, qb, kk, va, vb, cT)


def _attn_bwd_prep(do, o, *, name):
    T, D = do.shape
    bt = _pick(T, ATTN_BLOCK)

    def kern(do_ref, o_ref, doa_ref, dob_ref, dlta_ref, dltb_ref):
        lo = lax.broadcasted_iota(jnp.int32, (bt, LANE), 1) < HALF
        dov = do_ref[...]
        prod = dov * o_ref[...]
        doa_ref[...] = jnp.where(lo, dov, 0.0).astype(BF16)
        dob_ref[...] = jnp.where(lo, 0.0, dov).astype(BF16)
        dlta_ref[...] = jnp.broadcast_to(jnp.sum(jnp.where(lo, prod, 0.0), axis=1, keepdims=True), (bt, LANE))
        dltb_ref[...] = jnp.broadcast_to(jnp.sum(jnp.where(lo, 0.0, prod), axis=1, keepdims=True), (bt, LANE))

    blk = pl.BlockSpec((bt, LANE), lambda i, hp: (i, hp))
    return pl.pallas_call(
        kern, name=name, grid=(T // bt, HEAD_PAIRS), in_specs=[blk, blk], out_specs=[blk] * 4,
        out_shape=[jax.ShapeDtypeStruct((T, D), BF16)] * 2 + [jax.ShapeDtypeStruct((T, D), F32)] * 2,
        compiler_params=_params(("parallel", "parallel")),
    )(do, o)


def _attn_bwd(qa, qb, kk, ka, kb, vv, doa, dob, lsea, lseb, dlta, dltb, cT, *, name):
    T = qa.shape[0]
    tb = _pick(T, ATTN_BLOCK)
    nq = T // tb
    rep = tb // LANE

    def kern(qa_ref, qb_ref, k_ref, ka_ref, kb_ref, v_ref, doa_ref, dob_ref, lsea_ref, lseb_ref, dlta_ref, dltb_ref,
             c_ref, dq_ref, dk_ref, dv_ref, dc_ref, dcq_ref):
        ki = pl.program_id(1)
        qi = pl.program_id(2)

        @pl.when(jnp.logical_and(ki == 0, qi == 0))
        def _():
            dq_ref[...] = jnp.zeros_like(dq_ref)
            dcq_ref[...] = jnp.zeros_like(dcq_ref)

        @pl.when(qi == 0)
        def _():
            dk_ref[...] = jnp.zeros_like(dk_ref)
            dv_ref[...] = jnp.zeros_like(dv_ref)
            dc_ref[...] = jnp.zeros_like(dc_ref)

        def step(diag):
            k = k_ref[...]
            v = v_ref[...]
            dq = None
            dk = None
            dv = None
            row_sums = []
            heads = ((qa_ref, ka_ref, doa_ref, lsea_ref, dlta_ref), (qb_ref, kb_ref, dob_ref, lseb_ref, dltb_ref))
            for h, (q_ref, km_ref, do_ref, lse_ref, dlt_ref) in enumerate(heads):
                q = q_ref[...]
                dom = do_ref[...]
                s = _dot(q, k, _NT) - c_ref[h:h + 1, :]
                if diag:
                    r = lax.broadcasted_iota(jnp.int32, (tb, tb), 0)
                    c = lax.broadcasted_iota(jnp.int32, (tb, tb), 1)
                    s = jnp.where(c <= r, s, NEG)
                p = jnp.exp2(s - jnp.tile(lse_ref[...], (1, rep)))
                ds = p * (_dot(dom, v, _NT) - jnp.tile(dlt_ref[...], (1, rep)))
                dc_ref[h:h + 1, :] -= jnp.sum(ds, axis=0, keepdims=True)
                row_sums.append(jnp.sum(ds, axis=1, keepdims=True))
                dsb = ds.astype(BF16)
                tv = _dot(p.astype(BF16), dom, _TN)
                tk = _dot(dsb, q, _TN)
                tq = _dot(dsb, km_ref[...])
                dv = tv if dv is None else dv + tv
                dk = tk if dk is None else dk + tk
                dq = tq if dq is None else dq + tq
            dv_ref[...] += dv
            dk_ref[...] += dk * LN2
            rows = pl.ds(pl.multiple_of(qi * tb, tb), tb)
            dq_ref[rows, :] += dq * 0.125
            lo = lax.broadcasted_iota(jnp.int32, (tb, LANE), 1) < HALF
            dcq_ref[rows, :] += jnp.where(lo, row_sums[0], row_sums[1])

        @pl.when(qi > ki)
        def _():
            step(False)

        @pl.when(qi == ki)
        def _():
            step(True)

    qspec = pl.BlockSpec((tb, LANE), lambda hp, ki, qi: (jnp.maximum(qi, ki), hp))
    kspec = pl.BlockSpec((tb, LANE), lambda hp, ki, qi: (ki, hp))
    cspec = pl.BlockSpec((None, 2, tb), lambda hp, ki, qi: (hp, 0, ki))
    qacc = pl.BlockSpec((T, LANE), lambda hp, ki, qi: (0, hp))
    return pl.pallas_call(
        kern, name=name, grid=(HEAD_PAIRS, nq, nq),
        in_specs=[qspec, qspec, kspec, kspec, kspec, kspec, qspec, qspec, qspec, qspec, qspec, qspec, cspec],
        out_specs=[qacc, kspec, kspec, cspec, qacc],
        out_shape=[jax.ShapeDtypeStruct((T, D_MODEL), F32)] * 3 + [jax.ShapeDtypeStruct((HEAD_PAIRS, 2, T), F32),
                                                                   jax.ShapeDtypeStruct((T, D_MODEL), F32)],
        compiler_params=_params(("parallel", "arbitrary", "arbitrary"), vmem_mb=56),
    )(qa, qb, kk, ka, kb, vv, doa, dob, lsea, lseb, dlta, dltb, cT)


def _ssd_dt_fwd(proj, dt_bias, a_log, *, name):
    T = proj.shape[0]
    Q = SSM_CHUNK
    col = (2 * SSM_D_INNER + 2 * SSM_GROUPS * SSM_STATE) // LANE

    def kern(raw_ref, b_ref, al_ref, dt_ref, ac_ref):
        dt = _softplus(raw_ref[...] + b_ref[...])
        dt_ref[...] = dt
        ac_ref[...] = _tri_matmul(_tri(Q, True), dt * (-jnp.exp(al_ref[...])))

    vec = pl.BlockSpec((1, LANE), lambda i: (0, 0))
    blk = pl.BlockSpec((Q, LANE), lambda i: (i, 0))
    return pl.pallas_call(
        kern, name=name, grid=(T // Q,),
        in_specs=[pl.BlockSpec((Q, LANE), lambda i: (i, col)), vec, vec], out_specs=[blk, blk],
        out_shape=[jax.ShapeDtypeStruct((T, LANE), F32)] * 2,
        compiler_params=_params(("parallel",)),
    )(proj, dt_bias, a_log)


def _ssd_dt_bwd(da_a, da_b, ddt_a, ddt_b, dt, proj, dt_bias, a_log, into, *, name):
    T = proj.shape[0]
    Q = SSM_CHUNK
    col = (2 * SSM_D_INNER + 2 * SSM_GROUPS * SSM_STATE) // LANE

    def kern(daa_ref, dab_ref, dda_ref, ddb_ref, dt_ref, raw_ref, b_ref, al_ref, into_ref, draw_ref, dal_ref, db_ref,
             acc_ref):
        i = pl.program_id(0)

        @pl.when(i == 0)
        def _():
            acc_ref[...] = jnp.zeros_like(acc_ref)
            db_ref[...] = jnp.zeros_like(db_ref)

        A = -jnp.exp(al_ref[...])
        ddA = _tri_matmul(_tri(Q, False), daa_ref[...] + dab_ref[...])
        ddt = dda_ref[...] + ddb_ref[...] + ddA * A
        acc_ref[...] += jnp.sum(ddA * dt_ref[...], axis=0, keepdims=True)
        lane = lax.broadcasted_iota(jnp.int32, (Q, LANE), 1)
        draw = jnp.where(lane < SSM_HEADS, ddt * _sigmoid(raw_ref[...] + b_ref[...]), 0.0)
        draw_ref[...] = draw.astype(BF16)
        db_ref[...] += jnp.sum(draw, axis=0, keepdims=True)
        dal_ref[...] = acc_ref[...] * A

    vec = pl.BlockSpec((1, LANE), lambda i: (0, 0))
    blk = pl.BlockSpec((Q, LANE), lambda i: (i, 0))
    return pl.pallas_call(
        kern, name=name, grid=(T // Q,),
        in_specs=[blk, blk, blk, blk, blk, pl.BlockSpec((Q, LANE), lambda i: (i, col)), vec, vec, _ANY],
        out_specs=[pl.BlockSpec((Q, LANE), lambda i: (i, col)), vec, vec],
        out_shape=[jax.ShapeDtypeStruct(into.shape, into.dtype), jax.ShapeDtypeStruct((1, LANE), F32),
                   jax.ShapeDtypeStruct((1, LANE), F32)],
        input_output_aliases={8: 0},
        scratch_shapes=[pltpu.VMEM((1, LANE), F32)],
        compiler_params=_params(("arbitrary",)),
    )(da_a, da_b, ddt_a, ddt_b, dt, proj, dt_bias, a_log, into)


def _conv_silu_fwd(proj, cw, cb, *, name):
    T = proj.shape[0]
    C = SSM_XBC
    bt = _pick(T, ROW_BLOCK, SUBLANE)
    bc = 1024
    off = SSM_D_INNER // bc

    def kern(x_ref, halo_ref, cw_ref, cb_ref, o_ref, ext_ref):
        i = pl.program_id(0)
        _fill_ext_past(ext_ref, halo_ref, x_ref[...], i, bt)
        pre = cb_ref[...] + _conv_past(ext_ref, cw_ref, SSM_CONV, bt)
        o_ref[...] = pre * _sigmoid(pre)

    return pl.pallas_call(
        kern, name=name, grid=(T // bt, C // bc),
        in_specs=[pl.BlockSpec((bt, bc), lambda i, j: (i, j + off)), _halo_prev(bt, bc, off),
                  pl.BlockSpec((SSM_CONV, bc), lambda i, j: (0, j)), pl.BlockSpec((1, bc), lambda i, j: (0, j))],
        out_specs=pl.BlockSpec((bt, bc), lambda i, j: (i, j)),
        out_shape=jax.ShapeDtypeStruct((T, C), F32),
        scratch_shapes=[pltpu.VMEM((bt + HALO, bc), F32)],
        compiler_params=_params(("parallel", "parallel")),
    )(proj, proj, cw, cb)


def _conv_silu_bwd(dxbc, proj, cw, cb, *, name):
    T = proj.shape[0]
    C = SSM_XBC
    K = SSM_CONV
    bt = _pick(T, ROW_BLOCK, SUBLANE)
    bc = 1024
    off = SSM_D_INNER // bc

    def kern(d_ref, x_ref, halo_ref, cw_ref, cb_ref, dpre_ref, dcb_ref, dcw_ref, ext_ref):
        i = pl.program_id(1)
        _fill_ext_past(ext_ref, halo_ref, x_ref[...], i, bt)
        pre = cb_ref[...] + _conv_past(ext_ref, cw_ref, K, bt)
        sg = _sigmoid(pre)
        dpre = d_ref[...] * sg * (1.0 + pre * (1.0 - sg))
        dpre_ref[...] = dpre

        @pl.when(i == 0)
        def _():
            dcb_ref[...] = jnp.zeros_like(dcb_ref)
            dcw_ref[...] = jnp.zeros_like(dcw_ref)

        dcb_ref[...] += jnp.sum(dpre, axis=0, keepdims=True)
        for k in range(K):
            dcw_ref[k:k + 1, :] += jnp.sum(dpre * ext_ref[pl.ds(HALO - (K - 1) + k, bt), :], axis=0, keepdims=True)

    blk = pl.BlockSpec((bt, bc), lambda j, i: (i, j))
    return pl.pallas_call(
        kern, name=name, grid=(C // bc, T // bt),
        in_specs=[blk, pl.BlockSpec((bt, bc), lambda j, i: (i, j + off)),
                  pl.BlockSpec((HALO, bc), lambda j, i: (jnp.maximum(i * (bt // HALO) - 1, 0), j + off)),
                  pl.BlockSpec((K, bc), lambda j, i: (0, j)), pl.BlockSpec((1, bc), lambda j, i: (0, j))],
        out_specs=[blk, pl.BlockSpec((1, bc), lambda j, i: (0, j)), pl.BlockSpec((K, bc), lambda j, i: (0, j))],
        out_shape=[jax.ShapeDtypeStruct((T, C), F32), jax.ShapeDtypeStruct((1, C), F32),
                   jax.ShapeDtypeStruct((K, C), F32)],
        scratch_shapes=[pltpu.VMEM((bt + HALO, bc), F32)],
        compiler_params=_params(("parallel", "arbitrary")),
    )(dxbc, proj, proj, cw, cb)


_GP = SSM_D_INNER // SSM_GROUPS
_HPG = SSM_HEADS // SSM_GROUPS
_PH = SSM_D_INNER // SSM_HEADS


def _head_masks(rows):
    lane = lax.broadcasted_iota(jnp.int32, (rows, _GP), 1)
    return [jnp.logical_and(lane >= r * _PH, lane < (r + 1) * _PH) for r in range(_HPG)]


def _ssd_specs(idx):
    Q, N = SSM_CHUNK, SSM_STATE
    bcol, ccol = SSM_D_INNER // N, SSM_D_INNER // N + SSM_GROUPS
    return dict(
        x=pl.BlockSpec((Q, _GP), lambda j, g: (idx(j), g)),
        B=pl.BlockSpec((Q, N), lambda j, g: (idx(j), bcol + g)),
        C=pl.BlockSpec((Q, N), lambda j, g: (idx(j), ccol + g)),
        col=pl.BlockSpec((Q, LANE), lambda j, g: (idx(j), g)),
        row=pl.BlockSpec((None, _HPG, Q), lambda j, g: (g, 0, idx(j))),
        st=pl.BlockSpec((N, _GP), lambda j, g: (idx(j), g)),
    )


def _ssd_scan_fwd(xbc, dtc, acc_, dtr, acr, *, name):
    T = xbc.shape[0]
    Q, N = SSM_CHUNK, SSM_STATE
    nc = T // Q
    sp = _ssd_specs(lambda j: j)

    def kern(x_ref, B_ref, C_ref, dtc_ref, ac_ref, dtr_ref, ar_ref, ys_ref, st_ref, state_ref):
        j = pl.program_id(0)
        g = pl.program_id(1)

        @pl.when(j == 0)
        def _():
            state_ref[g] = jnp.zeros((N, _GP), F32)

        S = state_ref[g]
        st_ref[...] = S
        x = x_ref[...]
        xb = x.astype(BF16)
        Bb = B_ref[...].astype(BF16)
        Cb = C_ref[...].astype(BF16)
        CB = _dot(Cb, Bb, _NT)
        r_i = lax.broadcasted_iota(jnp.int32, (Q, Q), 0)
        c_i = lax.broadcasted_iota(jnp.int32, (Q, Q), 1)
        tri = c_i <= r_i
        masks = _head_masks(Q)
        masks1 = _head_masks(1)
        y = jnp.zeros((Q, _GP), F32)
        El = jnp.zeros((Q, _GP), F32)
        Wl = jnp.zeros((Q, _GP), F32)
        decl = jnp.zeros((1, _GP), F32)
        for r in range(_HPG):
            a_c = ac_ref[:, r:r + 1]
            a_r = ar_ref[r:r + 1, :]
            dt_c = dtc_ref[:, r:r + 1]
            dt_r = dtr_ref[r:r + 1, :]
            L = jnp.exp(jnp.where(tri, a_c - a_r, NEG))
            W = CB * L * dt_r
            y = jnp.where(masks[r], _dot(W.astype(BF16), xb), y)
            a_q = a_c[Q - 1:Q, :]
            El = jnp.where(masks[r], jnp.exp(a_c), El)
            Wl = jnp.where(masks[r], jnp.exp(a_q - a_c) * dt_c, Wl)
            decl = jnp.where(masks1[r], jnp.exp(a_q), decl)
        ys_ref[...] = y + _dot(Cb, S.astype(BF16)) * El
        state_ref[g] = S * decl + _dot(Bb, (x * Wl).astype(BF16), _TN)

    return pl.pallas_call(
        kern, name=name, grid=(nc, SSM_GROUPS),
        in_specs=[sp["x"], sp["B"], sp["C"], sp["col"], sp["col"], sp["row"], sp["row"]],
        out_specs=[sp["x"], sp["st"]],
        out_shape=[jax.ShapeDtypeStruct((T, SSM_D_INNER), F32), jax.ShapeDtypeStruct((nc * N, SSM_D_INNER), F32)],
        scratch_shapes=[pltpu.VMEM((SSM_GROUPS, N, _GP), F32)],
        compiler_params=_params(("arbitrary", "arbitrary")),
    )(xbc, xbc, xbc, dtc, acc_, dtr, acr)


def _ssd_scan_bwd(xbc, dys, dskip, st, dtc, acc_, dtr, acr, *, name):
    T = xbc.shape[0]
    Q, N = SSM_CHUNK, SSM_STATE
    nc = T // Q
    sp = _ssd_specs(lambda j: nc - 1 - j)

    def kern(x_ref, B_ref, C_ref, dy_ref, dsk_ref, st_ref, dtc_ref, ac_ref, dtr_ref, ar_ref,
             dx_ref, dB_ref, dC_ref, dac_ref, dar_ref, ddc_ref, ddr_ref, dstate_ref):
        j = pl.program_id(0)
        g = pl.program_id(1)

        @pl.when(j == 0)
        def _():
            dstate_ref[g] = jnp.zeros((N, _GP), F32)

        dS = dstate_ref[g]
        dSb = dS.astype(BF16)
        S = st_ref[...]
        Sb = S.astype(BF16)
        x = x_ref[...]
        xb = x.astype(BF16)
        Bb = B_ref[...].astype(BF16)
        Cb = C_ref[...].astype(BF16)
        dy = dy_ref[...]
        CB = _dot(Cb, Bb, _NT)
        BdS = _dot(Bb, dSb)
        hx = BdS * x
        yd = _dot(Cb, Sb) * dy
        dSS = dS * S
        r_i = lax.broadcasted_iota(jnp.int32, (Q, Q), 0)
        c_i = lax.broadcasted_iota(jnp.int32, (Q, Q), 1)
        tri = c_i <= r_i
        last_row = lax.broadcasted_iota(jnp.int32, (Q, 1), 0) == Q - 1
        lane128 = lax.broadcasted_iota(jnp.int32, (Q, LANE), 1)
        masks = _head_masks(Q)
        masksN = _head_masks(N)
        masks1 = _head_masks(1)
        zeros = jnp.zeros((Q, _GP), F32)
        dxi, El, Wl = zeros, zeros, zeros
        decl = jnp.zeros((1, _GP), F32)
        dBacc = jnp.zeros((Q, N), F32)
        dCacc = jnp.zeros((Q, N), F32)
        dacol = jnp.zeros((Q, LANE), F32)
        ddcol = jnp.zeros((Q, LANE), F32)
        for r in range(_HPG):
            hm = masks[r]
            a_c = ac_ref[:, r:r + 1]
            a_r = ar_ref[r:r + 1, :]
            dt_c = dtc_ref[:, r:r + 1]
            dt_r = dtr_ref[r:r + 1, :]
            L = jnp.exp(jnp.where(tri, a_c - a_r, NEG))
            GL = CB * L
            W = GL * dt_r
            dym = jnp.where(hm, dy, 0.0).astype(BF16)
            dW = _dot(dym, xb, _NT)
            E = dW * W
            da_c = jnp.sum(E, axis=1, keepdims=True)
            dar_ref[r:r + 1, :] = -jnp.sum(E, axis=0, keepdims=True)
            ddr_ref[r:r + 1, :] = jnp.sum(dW * GL, axis=0, keepdims=True)
            dGb = (dW * L * dt_r).astype(BF16)
            dCacc = dCacc + _dot(dGb, Bb)
            dBacc = dBacc + _dot(dGb, Cb, _TN)
            dxi = dxi + _dot(W.astype(BF16), dym, _TN)
            a_q = a_c[Q - 1:Q, :]
            e_c = jnp.exp(a_c)
            eq_c = jnp.exp(a_q - a_c)
            w_c = eq_c * dt_c
            ydr = jnp.sum(jnp.where(hm, yd, 0.0), axis=1, keepdims=True) * e_c
            h_c = jnp.sum(jnp.where(hm, hx, 0.0), axis=1, keepdims=True)
            hw = h_c * w_c
            dss = jnp.sum(jnp.sum(jnp.where(masksN[r], dSS, 0.0), axis=1, keepdims=True), axis=0, keepdims=True)
            s_q = jnp.sum(hw, axis=0, keepdims=True) + jnp.exp(a_q) * dss
            da_c = da_c + ydr - hw + jnp.where(last_row, s_q, 0.0)
            dacol = jnp.where(lane128 == r, da_c, dacol)
            ddcol = jnp.where(lane128 == r, h_c * eq_c, ddcol)
            El = jnp.where(hm, e_c, El)
            Wl = jnp.where(hm, w_c, Wl)
            decl = jnp.where(masks1[r], jnp.exp(a_q), decl)
        dx_ref[...] = dxi + BdS * Wl + dsk_ref[...]
        dB_ref[...] = dBacc + _dot((x * Wl).astype(BF16), dSb, _NT)
        dyE = (dy * El).astype(BF16)
        dC_ref[...] = dCacc + _dot(dyE, Sb, _NT)
        dac_ref[...] = dacol
        ddc_ref[...] = ddcol
        dstate_ref[g] = dS * decl + _dot(Cb, dyE, _TN)

    idx = lambda j: nc - 1 - j
    bcblk = pl.BlockSpec((Q, N), lambda j, g: (idx(j), g))
    return pl.pallas_call(
        kern, name=name, grid=(nc, SSM_GROUPS),
        in_specs=[sp["x"], sp["B"], sp["C"], sp["x"], sp["x"], sp["st"], sp["col"], sp["col"], sp["row"], sp["row"]],
        out_specs=[sp["x"], bcblk, bcblk, sp["col"], sp["row"], sp["col"], sp["row"]],
        out_shape=[jax.ShapeDtypeStruct((T, SSM_D_INNER), F32),
                   jax.ShapeDtypeStruct((T, SSM_GROUPS * N), F32), jax.ShapeDtypeStruct((T, SSM_GROUPS * N), F32),
                   jax.ShapeDtypeStruct((T, SSM_GROUPS * LANE), F32), jax.ShapeDtypeStruct((SSM_GROUPS, _HPG, T), F32),
                   jax.ShapeDtypeStruct((T, SSM_GROUPS * LANE), F32), jax.ShapeDtypeStruct((SSM_GROUPS, _HPG, T), F32)],
        scratch_shapes=[pltpu.VMEM((SSM_GROUPS, N, _GP), F32)],
        compiler_params=_params(("arbitrary", "arbitrary")),
    )(xbc, xbc, xbc, dys, dskip, st, dtc, acc_, dtr, acr)


def _gate_norm_fwd(ys, xbc, proj, d_exp, norm_w, *, name):
    T = ys.shape[0]
    bt = _pick(T, ROW_BLOCK, SUBLANE)

    def kern(ys_ref, x_ref, z_ref, d_ref, w_ref, o_ref):
        z = z_ref[...]
        yz = (ys_ref[...] + d_ref[...] * x_ref[...]) * (z * _sigmoid(z))
        rstd = lax.rsqrt(jnp.mean(yz * yz, axis=-1, keepdims=True) + RMS_EPS)
        o_ref[...] = (yz * rstd * w_ref[...]).astype(BF16)

    blk = pl.BlockSpec((bt, _GP), lambda i, g: (i, g))
    vec = pl.BlockSpec((1, _GP), lambda i, g: (0, g))
    return pl.pallas_call(
        kern, name=name, grid=(T // bt, SSM_GROUPS), in_specs=[blk, blk, blk, vec, vec], out_specs=blk,
        out_shape=jax.ShapeDtypeStruct((T, SSM_D_INNER), BF16), compiler_params=_params(("parallel", "parallel")),
    )(ys, xbc, proj, d_exp, norm_w)


def _gate_norm_bwd(dyn, ys, xbc, proj, d_exp, norm_w, *, name):
    T = ys.shape[0]
    bt = _pick(T, ROW_BLOCK, SUBLANE)

    def kern(dyn_ref, ys_ref, x_ref, z_ref, d_ref, w_ref, dz_ref, dys_ref, dsk_ref, dw_ref, dd_ref):
        i = pl.program_id(1)
        z = z_ref[...]
        x = x_ref[...]
        sg = _sigmoid(z)
        sz = z * sg
        y = ys_ref[...] + d_ref[...] * x
        yz = y * sz
        rstd = lax.rsqrt(jnp.mean(yz * yz, axis=-1, keepdims=True) + RMS_EPS)
        yhat = yz * rstd
        dynv = dyn_ref[...]
        gg = dynv * w_ref[...]
        dyz = rstd * (gg - yhat * jnp.mean(gg * yhat, axis=-1, keepdims=True))
        dy = dyz * sz
        dz_ref[...] = (dyz * y * sg * (1.0 + z * (1.0 - sg))).astype(BF16)
        dys_ref[...] = dy
        dsk_ref[...] = dy * d_ref[...]

        @pl.when(i == 0)
        def _():
            dw_ref[...] = jnp.zeros_like(dw_ref)
            dd_ref[...] = jnp.zeros_like(dd_ref)

        dw_ref[...] += jnp.sum(dynv * yhat, axis=0, keepdims=True)
        dd_ref[...] += jnp.sum(dy * x, axis=0, keepdims=True)

    blk = pl.BlockSpec((bt, _GP), lambda g, i: (i, g))
    vec = pl.BlockSpec((1, _GP), lambda g, i: (0, g))
    act = jax.ShapeDtypeStruct((T, SSM_D_INNER), F32)
    par = jax.ShapeDtypeStruct((1, SSM_D_INNER), F32)
    return pl.pallas_call(
        kern, name=name, grid=(SSM_GROUPS, T // bt), in_specs=[blk, blk, blk, blk, vec, vec],
        out_specs=[blk, blk, blk, vec, vec],
        out_shape=[jax.ShapeDtypeStruct((T, SSM_IN_PAD), BF16), act, act, par, par],
        compiler_params=_params(("parallel", "arbitrary")),
    )(dyn, ys, xbc, proj, d_exp, norm_w)


def _loss_head(y, target, *, name):
    T, D = y.shape
    bt = _pick(T, ROW_BLOCK, SUBLANE)

    def kern(y_ref, t_ref, l_ref, dy_ref):
        i = pl.program_id(0)
        err = y_ref[...] - t_ref[...]
        dy_ref[...] = err * (1.0 / D)

        @pl.when(i == 0)
        def _():
            l_ref[...] = jnp.zeros_like(l_ref)

        l_ref[...] += jnp.sum(err * err, axis=0, keepdims=True) * (0.5 / D)

    row = pl.BlockSpec((bt, D), lambda i: (i, 0))
    vec = pl.BlockSpec((1, D), lambda i: (0, 0))
    return pl.pallas_call(
        kern, name=name, grid=(T // bt,), in_specs=[row, row], out_specs=[vec, row],
        out_shape=[jax.ShapeDtypeStruct((1, D), F32), jax.ShapeDtypeStruct((T, D), F32)],
        compiler_params=_params(("arbitrary",)),
    )(y, target)


def _adamw(w, g, m, v, *, name):
    R, C = w.shape
    br = _pick(R, 512, SUBLANE)

    def kern(w_ref, g_ref, m_ref, v_ref, d_ref, nm_ref, nv_ref):
        gv = g_ref[...]
        nm = ADAM_B1 * m_ref[...] + (1.0 - ADAM_B1) * gv
        nv = ADAM_B2 * v_ref[...] + (1.0 - ADAM_B2) * (gv * gv)
        m_hat = nm / (1.0 - ADAM_B1 ** ADAM_STEP)
        v_hat = nv / (1.0 - ADAM_B2 ** ADAM_STEP)
        d_ref[...] = -ADAM_LR * (m_hat / (jnp.sqrt(v_hat) + ADAM_EPS) + ADAM_WD * w_ref[...])
        nm_ref[...] = nm
        nv_ref[...] = nv

    blk = pl.BlockSpec((br, C), lambda i: (i, 0))
    return pl.pallas_call(
        kern, name=name, grid=(R // br,), in_specs=[blk] * 4, out_specs=[blk] * 3,
        out_shape=[jax.ShapeDtypeStruct((R, C), F32)] * 3, compiler_params=_params(("parallel",)),
    )(w, g, m, v)


def _add2(a, b, *, name):
    shape = a.shape
    a2, b2 = a.reshape(-1, shape[-1]), b.reshape(-1, shape[-1])
    R, C = a2.shape
    br = _pick(R, 512, SUBLANE)

    def kern(a_ref, b_ref, o_ref):
        o_ref[...] = a_ref[...] + b_ref[...]

    blk = pl.BlockSpec((br, C), lambda i: (i, 0))
    return pl.pallas_call(
        kern, name=name, grid=(R // br,), in_specs=[blk, blk], out_specs=blk,
        out_shape=jax.ShapeDtypeStruct((R, C), F32), compiler_params=_params(("parallel",)),
    )(a2, b2).reshape(shape)


def _sum4(buf, *, name):
    _, R, C = buf.shape
    br = _pick(R, 512, SUBLANE)

    def kern(b_ref, o_ref):
        o_ref[...] = ((b_ref[0] + b_ref[1]) + b_ref[2]) + b_ref[3]

    return pl.pallas_call(
        kern, name=name, grid=(R // br,), in_specs=[pl.BlockSpec((4, br, C), lambda i: (0, i, 0))],
        out_specs=pl.BlockSpec((br, C), lambda i: (i, 0)),
        out_shape=jax.ShapeDtypeStruct((R, C), F32), compiler_params=_params(("parallel",)),
    )(buf)


def _place():
    x, y, c = lax.axis_index("x"), lax.axis_index("y"), lax.axis_index("c")
    other_chips = [(1 - x, y), (x, 1 - y), (1 - x, 1 - y)]
    return x, y, c, other_chips


def _gather_chips(w, *, name):
    R, C = w.shape
    H = R // 2

    def body(w_ref, out_ref, send_sems, recv_sems, local_sem):
        x, y, c, chips = _place()
        me_chip = 2 * x + y
        sib = (x, y, 1 - c)

        def rows(chip, hc):
            return out_ref.at[chip, pl.ds(hc * H, H), :]

        def copy(k, blk, to, src=None):
            return pltpu.make_async_remote_copy(
                src_ref=blk if src is None else src, dst_ref=blk, send_sem=send_sems.at[k], recv_sem=recv_sems.at[k],
                device_id=to, device_id_type=MESH)

        mine = pltpu.make_async_copy(w_ref, out_ref.at[me_chip], local_sem)
        mine.start()
        first = [copy(j, rows(me_chip, c), (cx, cy, c), src=w_ref.at[pl.ds(c * H, H), :])
                 for j, (cx, cy) in enumerate(chips)]
        for cp in first:
            cp.start()
        passed = []
        for j, (cx, cy) in enumerate(chips):
            blk = rows(2 * cx + cy, c)
            copy(j, blk, (cx, cy, c)).wait_recv()
            fw = copy(3 + j, blk, sib)
            fw.start()
            passed.append(fw)
        for j, (cx, cy) in enumerate(chips):
            copy(3 + j, rows(2 * cx + cy, 1 - c), sib).wait_recv()
        for cp in first + passed:
            cp.wait_send()
        mine.wait()

    return pl.pallas_call(
        body, name=name, in_specs=[_ANY], out_specs=_ANY,
        out_shape=jax.ShapeDtypeStruct((4, R, C), w.dtype),
        scratch_shapes=[pltpu.SemaphoreType.DMA((6,)), pltpu.SemaphoreType.DMA((6,)), pltpu.SemaphoreType.DMA],
    )(w)


def _pair_swap(v, *, name):
    def body(v_ref, out_ref, send_sem, recv_sem):
        x, y, c, _ = _place()
        cp = pltpu.make_async_remote_copy(src_ref=v_ref, dst_ref=out_ref, send_sem=send_sem, recv_sem=recv_sem,
                                          device_id=(x, y, 1 - c), device_id_type=MESH)
        cp.start()
        cp.wait()

    return pl.pallas_call(
        body, name=name, in_specs=[_ANY], out_specs=_ANY, out_shape=jax.ShapeDtypeStruct(v.shape, v.dtype),
        scratch_shapes=[pltpu.SemaphoreType.DMA, pltpu.SemaphoreType.DMA],
    )(v)


def _chip_exchange(pv, *, name):
    def body(p_ref, out_ref, send_sems, recv_sems, local_sem):
        x, y, c, chips = _place()
        me_chip = 2 * x + y
        mine = pltpu.make_async_copy(p_ref.at[me_chip], out_ref.at[me_chip], local_sem)
        mine.start()
        sends = []
        for j, (cx, cy) in enumerate(chips):
            cp = pltpu.make_async_remote_copy(
                src_ref=p_ref.at[2 * cx + cy], dst_ref=out_ref.at[me_chip], send_sem=send_sems.at[j],
                recv_sem=recv_sems.at[j], device_id=(cx, cy, c), device_id_type=MESH)
            cp.start()
            sends.append(cp)
        for j, (cx, cy) in enumerate(chips):
            blk = out_ref.at[2 * cx + cy]
            pltpu.make_async_remote_copy(src_ref=blk, dst_ref=blk, send_sem=send_sems.at[j], recv_sem=recv_sems.at[j],
                                         device_id=(cx, cy, c), device_id_type=MESH).wait_recv()
        for cp in sends:
            cp.wait_send()
        mine.wait()

    return pl.pallas_call(
        body, name=name, in_specs=[_ANY], out_specs=_ANY, out_shape=jax.ShapeDtypeStruct(pv.shape, pv.dtype),
        scratch_shapes=[pltpu.SemaphoreType.DMA((3,)), pltpu.SemaphoreType.DMA((3,)), pltpu.SemaphoreType.DMA],
    )(pv)


def _pair_gather(f, *, name):
    H, C = f.shape

    def body(f_ref, out_ref, send_sem, recv_sem, local_sem):
        x, y, c, _ = _place()
        mine_rows = out_ref.at[pl.ds(c * H, H), :]
        mine = pltpu.make_async_copy(f_ref, mine_rows, local_sem)
        mine.start()
        cp = pltpu.make_async_remote_copy(src_ref=f_ref, dst_ref=mine_rows, send_sem=send_sem, recv_sem=recv_sem,
                                          device_id=(x, y, 1 - c), device_id_type=MESH)
        cp.start()
        theirs = out_ref.at[pl.ds((1 - c) * H, H), :]
        pltpu.make_async_remote_copy(src_ref=theirs, dst_ref=theirs, send_sem=send_sem, recv_sem=recv_sem,
                                     device_id=(x, y, 1 - c), device_id_type=MESH).wait_recv()
        cp.wait_send()
        mine.wait()

    return pl.pallas_call(
        body, name=name, in_specs=[_ANY], out_specs=_ANY, out_shape=jax.ShapeDtypeStruct((2 * H, C), f.dtype),
        scratch_shapes=[pltpu.SemaphoreType.DMA, pltpu.SemaphoreType.DMA, pltpu.SemaphoreType.DMA],
    )(f)


WEIGHTS = [
    ("attn_w_in", 2), ("attn_b_f", None), ("attn_w_out", 1), ("ssm_w_in", 2), ("ssm_conv_w", 2), ("ssm_conv_b", 1),
    ("ssm_dt_bias", None), ("ssm_A_log", None), ("ssm_D", None), ("ssm_norm_w", 1), ("ssm_w_out", 1),
    ("ln_mix_g", None), ("ln_mix_b", None), ("ffn_w_up", 2), ("ffn_conv_w", 2), ("ffn_conv_b", None),
    ("ffn_w_down", 1), ("ln_ffn_g", None), ("ln_ffn_b", None), ("ple_w_proj", 2), ("ple_w_gate", 1),
    ("ple_b_gate", None),
]
N_CHIPS = 4
MATMUL_WEIGHTS = ("attn_w_in", "attn_w_out", "ssm_w_in", "ssm_w_out", "ffn_w_up", "ffn_w_down", "ple_w_proj",
                  "ple_w_gate")


def _pack(arrays):
    parts = []
    total = 0
    for a in arrays:
        n = a.size
        pad = (-n) % PACK_COLS
        flat = a.reshape(-1)
        parts.append(jnp.pad(flat, (0, pad)) if pad else flat)
        total += n + pad
    rows = total // PACK_COLS
    rpad = (-rows) % PACK_ROW_ALIGN
    if rpad:
        parts.append(jnp.zeros((rpad * PACK_COLS,), arrays[0].dtype))
    return jnp.concatenate(parts).reshape(rows + rpad, PACK_COLS)


def _unpack(buf, shapes):
    flat = buf.reshape(-1)
    out = []
    off = 0
    for s in shapes:
        n = math.prod(s)
        out.append(flat[off:off + n].reshape(s))
        off += n + ((-n) % PACK_COLS)
    return out


def _col_layout(a):
    T = a.shape[0]
    g = a[:, :SSM_HEADS].reshape(T, SSM_GROUPS, _HPG)
    return jnp.pad(g, ((0, 0), (0, 0), (0, LANE - _HPG))).reshape(T, SSM_GROUPS * LANE)


def _row_layout(a):
    T = a.shape[0]
    return a[:, :SSM_HEADS].T.reshape(SSM_GROUPS, _HPG, T)


def _from_col_layout(a):
    T = a.shape[0]
    v = a.reshape(T, SSM_GROUPS, LANE)[:, :, :_HPG].reshape(T, SSM_HEADS)
    return jnp.pad(v, ((0, 0), (0, LANE - SSM_HEADS)))


def _from_row_layout(a):
    T = a.shape[-1]
    v = a.reshape(SSM_HEADS, T).T
    return jnp.pad(v, ((0, 0), (0, LANE - SSM_HEADS)))


def _pad_lanes(v, n=LANE):
    return jnp.pad(v, (0, n - v.shape[0])).reshape(1, n)


def _local_step(x, p, target, W):
    T = x.shape[0]
    row = lambda v: v.reshape(1, -1)
    attn_in = jnp.pad(W["attn_w_in"][0], ((0, 0), (0, ATTN_IN_PAD - W["attn_w_in"].shape[2])))
    ssm_in = jnp.pad(W["ssm_w_in"][0], ((0, 0), (0, SSM_IN_PAD - W["ssm_w_in"].shape[2])))
    bf = _pad_lanes(W["attn_b_f"][0])
    dt_bias = _pad_lanes(W["ssm_dt_bias"][0])
    a_log = _pad_lanes(W["ssm_A_log"][0])
    d_exp = jnp.repeat(W["ssm_D"][0], _PH).reshape(1, SSM_D_INNER)
    norm_w = row(W["ssm_norm_w"][0])
    G = {}

    def ffn_ple_fwd(i, xin, mix, tag):
        s = {}
        s["z1"], s["h1"], s["h1b"] = _ln_fwd(xin, mix, row(W["ln_mix_g"][i]), row(W["ln_mix_b"][i]),
                                             name=f"ln_mix_fwd{tag}")
        s["up"] = _mm(s["h1b"], W["ffn_w_up"][i], name=f"ffn_up{tag}")
        s["a"] = _ffn_act_fwd(s["up"], W["ffn_conv_w"][i], row(W["ffn_conv_b"][i]), name=f"ffn_act_fwd{tag}")
        ffn = _mm(s["a"], W["ffn_w_down"][i], name=f"ffn_down{tag}")
        s["z2"], s["h2"], s["h2b"] = _ln_fwd(s["h1"], ffn, row(W["ln_ffn_g"][i]), row(W["ln_ffn_b"][i]),
                                             name=f"ln_ffn_fwd{tag}")
        s["G"] = _mm(s["h2b"], W["ple_w_gate"][i], name=f"ple_gate_mm{tag}")
        s["pp"] = _mm(pb[i], W["ple_w_proj"][i], name=f"ple_proj_mm{tag}")
        out, outb = _ple_fwd(s["h2"], s["G"], row(W["ple_b_gate"][i]), s["pp"], name=f"ple_fwd{tag}")
        return out, outb, s

    def ffn_ple_bwd(i, dx, s, tag):
        g = {}
        dG, dpp, g["ple_b_gate"] = _ple_bwd(dx, s["G"], row(W["ple_b_gate"][i]), s["pp"], name=f"ple_bwd{tag}")
        g["ple_w_gate"] = _mm(s["h2b"], dG, ta=True, name=f"ple_gate_dw{tag}")
        g["ple_w_proj"] = _mm(pb[i], dpp, ta=True, name=f"ple_proj_dw{tag}")
        dh2 = _mm(dG, W["ple_w_gate"][i], tb=True, add=dx, name=f"ple_gate_dx{tag}")
        dz2, dz2b, g["ln_ffn_g"], g["ln_ffn_b"] = _ln_bwd(dh2, s["z2"], row(W["ln_ffn_g"][i]), name=f"ln_ffn_bwd{tag}")
        da = _mm(dz2b, W["ffn_w_down"][i], tb=True, name=f"ffn_down_dx{tag}")
        g["ffn_w_down"] = _mm(s["a"], dz2b, ta=True, name=f"ffn_down_dw{tag}")
        dup, dgc, g["ffn_conv_b"], g["ffn_conv_w"] = _ffn_act_bwd(
            da, s["up"], W["ffn_conv_w"][i], row(W["ffn_conv_b"][i]), name=f"ffn_act_bwd{tag}")
        dup = _dwconv_bwd_data(dgc, W["ffn_conv_w"][i], FFN_CONV, dup, FFN_DIM, name=f"ffn_conv_bwd{tag}")
        g["ffn_w_up"] = _mm(s["h1b"], dup, ta=True, name=f"ffn_up_dw{tag}")
        dh1 = _mm(dup, W["ffn_w_up"][i], tb=True, add=dz2, add_scale=DEEPNORM_ALPHA, name=f"ffn_up_dx{tag}")
        dz1, dz1b, g["ln_mix_g"], g["ln_mix_b"] = _ln_bwd(dh1, s["z1"], row(W["ln_mix_g"][i]), name=f"ln_mix_bwd{tag}")
        return dz1, dz1b, g

    xb = x.astype(BF16)
    pb = p.astype(BF16)
    proj0 = _mm(xb, attn_in, name="attn_in")
    c_col = _fox_gate_fwd(proj0, bf, name="fox_gate_fwd")
    cT = (c_col[:, :ATTN_HEADS] * LOG2E).T.reshape(HEAD_PAIRS, 2, T)
    qa, qb, kk, ka, kb, vv, va, vb = _attn_prep(proj0, name="attn_prep")
    o, ob, lsea, lseb = _attn_fwd(qa, qb, kk, va, vb, cT, name="attn_fwd")
    mix0 = _mm(ob, W["attn_w_out"][0], name="attn_out")
    x1, x1b, s0 = ffn_ple_fwd(0, x, mix0, "0")

    proj1 = _mm(x1b, ssm_in, name="ssm_in")
    dt, acum = _ssd_dt_fwd(proj1, dt_bias, a_log, name="ssd_dt_fwd")
    xbc = _conv_silu_fwd(proj1, W["ssm_conv_w"][0], row(W["ssm_conv_b"][0]), name="ssd_conv_fwd")
    dtc, acc_, dtr, acr = _col_layout(dt), _col_layout(acum), _row_layout(dt), _row_layout(acum)
    ys, states = _ssd_scan_fwd(xbc, dtc, acc_, dtr, acr, name="ssd_scan_fwd")
    yn = _gate_norm_fwd(ys, xbc, proj1, d_exp, norm_w, name="ssd_gate_norm_fwd")
    mix1 = _mm(yn, W["ssm_w_out"][0], name="ssm_out")
    x2, _, s1 = ffn_ple_fwd(1, x1, mix1, "1")

    lpart, dy = _loss_head(x2, target, name="loss_head")
    loss = jnp.sum(lpart)

    dz1, dz1b, g1 = ffn_ple_bwd(1, dy, s1, "1")
    G["ssm_w_out"] = _mm(yn, dz1b, ta=True, name="ssm_out_dw")[None]
    dyn = _mm(dz1b, W["ssm_w_out"][0], tb=True, name="ssm_out_dx")
    dproj1, dys, dskip, dnw, dde = _gate_norm_bwd(dyn, ys, xbc, proj1, d_exp, norm_w, name="ssd_gate_norm_bwd")
    G["ssm_norm_w"] = dnw
    G["ssm_D"] = dde.reshape(SSM_HEADS, _PH).sum(axis=1)[None]
    dxs, dB, dC, dac, dar, ddc, ddr = _ssd_scan_bwd(xbc, dys, dskip, states, dtc, acc_, dtr, acr, name="ssd_scan_bwd")
    dproj1, dal, ddb = _ssd_dt_bwd(_from_col_layout(dac), _from_row_layout(dar), _from_col_layout(ddc),
                                   _from_row_layout(ddr), dt, proj1, dt_bias, a_log, dproj1, name="ssd_dt_bwd")
    G["ssm_A_log"] = dal[:, :SSM_HEADS]
    G["ssm_dt_bias"] = ddb[:, :SSM_HEADS]
    dxbc = jnp.concatenate([dxs, dB, dC], axis=1)
    dpre, G["ssm_conv_b"], dcw = _conv_silu_bwd(dxbc, proj1, W["ssm_conv_w"][0], row(W["ssm_conv_b"][0]),
                                                name="ssd_conv_bwd")
    G["ssm_conv_w"] = dcw[None]
    dproj1 = _dwconv_bwd_data(dpre, W["ssm_conv_w"][0], SSM_CONV, dproj1, SSM_D_INNER, name="ssd_conv_bwd_data")
    G["ssm_w_in"] = _mm(x1b, dproj1, ta=True, name="ssm_in_dw")[None, :, :W["ssm_w_in"].shape[2]]
    dx1 = _mm(dproj1, ssm_in, tb=True, add=dz1, add_scale=DEEPNORM_ALPHA, name="ssm_in_dx")

    dz0, dz0b, g0 = ffn_ple_bwd(0, dx1, s0, "0")
    G["attn_w_out"] = _mm(ob, dz0b, ta=True, name="attn_out_dw")[None]
    do = _mm(dz0b, W["attn_w_out"][0], tb=True, name="attn_out_dx")
    doa, dob, dlta, dltb = _attn_bwd_prep(do, o, name="attn_bwd_prep")
    dq, dk, dv, dcT, dcq = _attn_bwd(qa, qb, kk, ka, kb, vv, doa, dob, lsea, lseb, dlta, dltb, cT, name="attn_bwd")
    dcq = dcq.reshape(T, HEAD_PAIRS, 2, HALF)[:, :, :, 0].reshape(T, ATTN_HEADS)
    dc_col = jnp.pad(dcT.reshape(ATTN_HEADS, T).T + dcq, ((0, 0), (0, LANE - ATTN_HEADS)))
    dfl, dbf = _fox_gate_bwd(dc_col, proj0, bf, name="fox_gate_bwd")
    G["attn_b_f"] = dbf[:, :ATTN_HEADS]
    dproj0 = jnp.concatenate([dq.astype(BF16), dk.astype(BF16), dv.astype(BF16), dfl.astype(BF16)], axis=1)
    G["attn_w_in"] = _mm(xb, dproj0, ta=True, name="attn_in_dw")[None, :, :W["attn_w_in"].shape[2]]
    grad_x = _mm(dproj0, attn_in, tb=True, add=dz0, add_scale=DEEPNORM_ALPHA, name="attn_in_dx")

    for k in g0:
        G[k] = jnp.stack([g0[k].reshape(W[k].shape[1:]), g1[k].reshape(W[k].shape[1:])])
    return loss, grad_x, G


def kernel(x, p, attn_w_in, attn_b_f, attn_w_out, ssm_w_in, ssm_conv_w, ssm_conv_b, ssm_dt_bias, ssm_A_log, ssm_D, ssm_norm_w, ssm_w_out, ln_mix_g, ln_mix_b, ffn_w_up, ffn_conv_w, ffn_conv_b, ffn_w_down, ln_ffn_g, ln_ffn_b, ple_w_proj, ple_w_gate, ple_b_gate, loss_target, m_attn_w_in, m_attn_b_f, m_attn_w_out, m_ssm_w_in, m_ssm_conv_w, m_ssm_conv_b, m_ssm_dt_bias, m_ssm_A_log, m_ssm_D, m_ssm_norm_w, m_ssm_w_out, m_ln_mix_g, m_ln_mix_b, m_ffn_w_up, m_ffn_conv_w, m_ffn_conv_b, m_ffn_w_down, m_ln_ffn_g, m_ln_ffn_b, m_ple_w_proj, m_ple_w_gate, m_ple_b_gate, v_attn_w_in, v_attn_b_f, v_attn_w_out, v_ssm_w_in, v_ssm_conv_w, v_ssm_conv_b, v_ssm_dt_bias, v_ssm_A_log, v_ssm_D, v_ssm_norm_w, v_ssm_w_out, v_ln_mix_g, v_ln_mix_b, v_ffn_w_up, v_ffn_conv_w, v_ffn_conv_b, v_ffn_w_down, v_ln_ffn_g, v_ln_ffn_b, v_ple_w_proj, v_ple_w_gate, v_ple_b_gate):
    names = [n for n, _ in WEIGHTS]
    axes = dict(WEIGHTS)
    w_loc = dict(zip(names, [attn_w_in, attn_b_f, attn_w_out, ssm_w_in, ssm_conv_w, ssm_conv_b, ssm_dt_bias, ssm_A_log, ssm_D, ssm_norm_w, ssm_w_out, ln_mix_g, ln_mix_b, ffn_w_up, ffn_conv_w, ffn_conv_b, ffn_w_down, ln_ffn_g, ln_ffn_b, ple_w_proj, ple_w_gate, ple_b_gate]))
    m_loc = dict(zip(names, [m_attn_w_in, m_attn_b_f, m_attn_w_out, m_ssm_w_in, m_ssm_conv_w, m_ssm_conv_b, m_ssm_dt_bias, m_ssm_A_log, m_ssm_D, m_ssm_norm_w, m_ssm_w_out, m_ln_mix_g, m_ln_mix_b, m_ffn_w_up, m_ffn_conv_w, m_ffn_conv_b, m_ffn_w_down, m_ln_ffn_g, m_ln_ffn_b, m_ple_w_proj, m_ple_w_gate, m_ple_b_gate]))
    v_loc = dict(zip(names, [v_attn_w_in, v_attn_b_f, v_attn_w_out, v_ssm_w_in, v_ssm_conv_w, v_ssm_conv_b, v_ssm_dt_bias, v_ssm_A_log, v_ssm_D, v_ssm_norm_w, v_ssm_w_out, v_ln_mix_g, v_ln_mix_b, v_ffn_w_up, v_ffn_conv_w, v_ffn_conv_b, v_ffn_w_down, v_ln_ffn_g, v_ln_ffn_b, v_ple_w_proj, v_ple_w_gate, v_ple_b_gate]))
    sharded = [n for n in names if axes[n] is not None]
    matrices = [n for n in sharded if n in MATMUL_WEIGHTS]

    def wire(n):
        if n in matrices:
            return w_loc[n].astype(BF16)
        return lax.bitcast_convert_type(w_loc[n], BF16)

    wired = [wire(n) for n in sharded]
    gathered = _gather_chips(_pack(wired), name="gather_weights")
    W = dict(w_loc)
    per_chip = [_unpack(gathered[k], [w.shape for w in wired]) for k in range(N_CHIPS)]
    for i, n in enumerate(sharded):
        pieces = [per_chip[k][i] for k in range(N_CHIPS)]
        if n not in matrices:
            pieces = [lax.bitcast_convert_type(q, F32) for q in pieces]
        W[n] = jnp.concatenate(pieces, axis=axes[n])

    loss, grad_x, G = _local_step(x[0], p[:, 0], loss_target[0], W)
    loss = lax.psum(loss, ("x", "y", "c"))

    def slot(k):
        parts = []
        for n in names:
            g = G[n].reshape(W[n].shape)
            if axes[n] is not None:
                size = w_loc[n].shape[axes[n]]
                g = lax.slice_in_dim(g, k * size, (k + 1) * size, axis=axes[n])
            parts.append(g)
        return _pack(parts)

    contrib = jnp.stack([slot(k) for k in range(N_CHIPS)])
    R = contrib.shape[1]
    H = R // 2
    c = lax.axis_index("c")
    keep = lax.dynamic_slice_in_dim(contrib, c * H, H, axis=1)
    give = lax.dynamic_slice_in_dim(contrib, (1 - c) * H, H, axis=1)
    pair = _add2(keep, _pair_swap(give, name="grad_pair_swap"), name="grad_pair_sum")
    half = _sum4(_chip_exchange(pair, name="grad_chip_exchange"), name="grad_chip_sum")
    gflat = _pair_gather(half, name="grad_pair_gather")

    shapes = [w_loc[n].shape for n in names]
    delta, new_m, new_v = _adamw(_pack([w_loc[n] for n in names]), gflat, _pack([m_loc[n] for n in names]),
                                 _pack([v_loc[n] for n in names]), name="adamw")
    return (loss, grad_x[None], *_unpack(gflat, shapes), *_unpack(delta, shapes), *_unpack(new_m, shapes),
            *_unpack(new_v, shapes))
```

```python
import functools
import math

import jax
import jax.numpy as jnp
from jax import lax
from jax.experimental import pallas as pl
from jax.experimental.pallas import tpu as pltpu

F32 = jnp.float32
BF16 = jnp.bfloat16
MESH = pl.DeviceIdType.MESH

D_MODEL = 1024
ATTN_HEADS = 16
HEAD_PAIRS = ATTN_HEADS // 2
SSM_D_INNER = 2048
SSM_HEADS = 32
SSM_GROUPS = 8
SSM_STATE = 128
SSM_CONV = 4
SSM_CHUNK = 128
SSM_XBC = SSM_D_INNER + 2 * SSM_GROUPS * SSM_STATE
FFN_DIM = 2816
FFN_CONV = 3
DEPTH = 2
LN_EPS = 1e-5
RMS_EPS = 1e-5
DEEPNORM_ALPHA = (2 * DEPTH) ** 0.25
ADAM_LR = 0.001
ADAM_B1 = 0.9
ADAM_B2 = 0.999
ADAM_EPS = 1e-08
ADAM_WD = 0.01
ADAM_STEP = 10

LANE = 128
SUBLANE = 8
HALO = SUBLANE
NEG = -1e30
ATTN_IN_PAD = 3 * D_MODEL + LANE
SSM_IN_PAD = 2 * SSM_D_INNER + 2 * SSM_GROUPS * SSM_STATE + LANE
PACK_COLS = 1024
PACK_ROW_ALIGN = 512

ATTN_BLOCK = 1024
ROW_BLOCK = 512
NARROW_ROW_BLOCK = 1024
CUM_BLOCK = 256


def _params(sem, vmem_mb=48):
    return pltpu.CompilerParams(dimension_semantics=sem, vmem_limit_bytes=vmem_mb * 2 ** 20)


def _pick(n, target, mult=LANE):
    best = None
    d = mult
    while d <= min(n, target):
        if n % d == 0:
            best = d
        d += mult
    return n if best is None else best


def _sigmoid(x):
    return 1.0 / (1.0 + jnp.exp(-x))


def _log1p(u):
    w = 1.0 + u
    return jnp.where(w == 1.0, u, jnp.log(w) * (u / (w - 1.0)))


def _softplus(x):
    return jnp.maximum(x, 0.0) + _log1p(jnp.exp(-jnp.abs(x)))


def _split3(x):
    hi = x.astype(BF16)
    r1 = x - hi.astype(F32)
    mid = r1.astype(BF16)
    lo = (r1 - mid.astype(F32)).astype(BF16)
    return hi, mid, lo


def _tri_matmul(tri, x):
    out = None
    for part in _split3(x):
        t = jnp.dot(tri, part, preferred_element_type=F32)
        out = t if out is None else out + t
    return out


def _tri(n, lower):
    r = lax.broadcasted_iota(jnp.int32, (n, n), 0)
    c = lax.broadcasted_iota(jnp.int32, (n, n), 1)
    return jnp.where((c <= r) if lower else (c >= r), 1.0, 0.0).astype(BF16)


_ANY = pl.BlockSpec(memory_space=pl.ANY)
MM_OUT_BLOCK_BYTES = 13 * 2 ** 20
MM_FULL_K = 3200


def _mm(a, b, *, name, ta=False, tb=False, add=None, add_scale=1.0, out_dtype=F32):
    if ta:
        K, M = a.shape
    else:
        M, K = a.shape
    if tb:
        N, Kb = b.shape
    else:
        Kb, N = b.shape
    assert K == Kb, (a.shape, b.shape, ta, tb)
    if ta:
        assert add is None and out_dtype == F32
        bm = _pick(M, 2816)
        bn = _pick(N, MM_OUT_BLOCK_BYTES // (4 * bm))
        bk = _pick(K, 512)
    else:
        bm = _pick(M, 1024)
        bn = _pick(N, 1024)
        bk = K if K <= MM_FULL_K else _pick(K, MM_FULL_K)
    nk = K // bk
    a_spec = pl.BlockSpec((bk, bm), lambda i, j, k: (k, i)) if ta else pl.BlockSpec((bm, bk), lambda i, j, k: (i, k))
    b_spec = pl.BlockSpec((bn, bk), lambda i, j, k: (j, k)) if tb else pl.BlockSpec((bk, bn), lambda i, j, k: (k, j))
    o_spec = pl.BlockSpec((bm, bn), lambda i, j, k: (i, j))
    dims = (((0 if ta else 1,), (1 if tb else 0,)), ((), ()))
    has_add = add is not None
    use_acc = nk > 1 and not ta

    def kern(*refs):
        a_ref, b_ref = refs[0], refs[1]
        add_ref = refs[2] if has_add else None
        o_ref = refs[3] if has_add else refs[2]
        k = pl.program_id(2)
        part = lax.dot_general(a_ref[...].astype(BF16), b_ref[...].astype(BF16), dims, preferred_element_type=F32)

        def finish(r):
            return (r + add_scale * add_ref[...] if has_add else r).astype(out_dtype)

        if nk == 1:
            o_ref[...] = finish(part)
        elif not use_acc:
            @pl.when(k == 0)
            def _():
                o_ref[...] = part

            @pl.when(k > 0)
            def _():
                o_ref[...] += part
        else:
            acc_ref = refs[-1]

            @pl.when(k == 0)
            def _():
                acc_ref[...] = part

            @pl.when(jnp.logical_and(k > 0, k < nk - 1))
            def _():
                acc_ref[...] += part

            @pl.when(k == nk - 1)
            def _():
                o_ref[...] = finish(acc_ref[...] + part)

    ins = [a, b] + ([add] if has_add else [])
    in_specs = [a_spec, b_spec] + ([o_spec] if has_add else [])
    return pl.pallas_call(
        kern, name=name, grid=(M // bm, N // bn, nk),
        in_specs=in_specs, out_specs=o_spec,
        out_shape=jax.ShapeDtypeStruct((M, N), out_dtype),
        scratch_shapes=[pltpu.VMEM((bm, bn), F32)] if use_acc else [],
        compiler_params=_params(("parallel", "parallel", "arbitrary"), vmem_mb=56),
    )(*ins)


def _ln_stats(z):
    mu = jnp.mean(z, axis=-1, keepdims=True)
    zc = z - mu
    var = jnp.mean(zc * zc, axis=-1, keepdims=True)
    return zc, lax.rsqrt(var + LN_EPS)


def _ln_fwd(x, r, g, b, *, name):
    T, D = x.shape
    bt = _pick(T, ROW_BLOCK, SUBLANE)

    def kern(x_ref, r_ref, g_ref, b_ref, z_ref, h_ref, hb_ref):
        z = DEEPNORM_ALPHA * x_ref[...] + r_ref[...]
        zc, rstd = _ln_stats(z)
        h = zc * rstd * g_ref[...] + b_ref[...]
        z_ref[...] = z
        h_ref[...] = h
        hb_ref[...] = h.astype(BF16)

    row = pl.BlockSpec((bt, D), lambda i: (i, 0))
    vec = pl.BlockSpec((1, D), lambda i: (0, 0))
    return pl.pallas_call(
        kern, name=name, grid=(T // bt,), in_specs=[row, row, vec, vec], out_specs=[row, row, row],
        out_shape=[jax.ShapeDtypeStruct((T, D), F32)] * 2 + [jax.ShapeDtypeStruct((T, D), BF16)],
        compiler_params=_params(("parallel",)),
    )(x, r, g, b)


def _ln_bwd(dy, z, g, *, name):
    T, D = z.shape
    bt = _pick(T, ROW_BLOCK, SUBLANE)

    def kern(dy_ref, z_ref, g_ref, dz_ref, dzb_ref, dg_ref, db_ref):
        i = pl.program_id(0)
        zc, rstd = _ln_stats(z_ref[...])
        xhat = zc * rstd
        dyv = dy_ref[...]
        dxh = dyv * g_ref[...]
        m1 = jnp.mean(dxh, axis=-1, keepdims=True)
        m2 = jnp.mean(dxh * xhat, axis=-1, keepdims=True)
        dz = rstd * (dxh - m1 - xhat * m2)
        dz_ref[...] = dz
        dzb_ref[...] = dz.astype(BF16)

        @pl.when(i == 0)
        def _():
            dg_ref[...] = jnp.zeros_like(dg_ref)
            db_ref[...] = jnp.zeros_like(db_ref)

        dg_ref[...] += jnp.sum(dyv * xhat, axis=0, keepdims=True)
        db_ref[...] += jnp.sum(dyv, axis=0, keepdims=True)

    row = pl.BlockSpec((bt, D), lambda i: (i, 0))
    vec = pl.BlockSpec((1, D), lambda i: (0, 0))
    return pl.pallas_call(
        kern, name=name, grid=(T // bt,), in_specs=[row, row, vec], out_specs=[row, row, vec, vec],
        out_shape=[jax.ShapeDtypeStruct((T, D), F32), jax.ShapeDtypeStruct((T, D), BF16),
                   jax.ShapeDtypeStruct((1, D), F32), jax.ShapeDtypeStruct((1, D), F32)],
        compiler_params=_params(("arbitrary",)),
    )(dy, z, g)


def _conv_past(ext_ref, cw_ref, K, bt):
    out = None
    for k in range(K):
        term = cw_ref[k:k + 1, :] * ext_ref[pl.ds(HALO - (K - 1) + k, bt), :]
        out = term if out is None else out + term
    return out


def _fill_ext_past(ext_ref, halo_ref, cur, i, bt):
    ext_ref[pl.ds(0, HALO), :] = jnp.where(i > 0, halo_ref[...], 0.0)
    ext_ref[pl.ds(HALO, bt), :] = cur


def _halo_prev(bt, bc, off):
    return pl.BlockSpec((HALO, bc), lambda i, j: (jnp.maximum(i * (bt // HALO) - 1, 0), j + off))


def _gelu(x):
    return 0.5 * x * (1.0 + lax.erf(x * (1.0 / math.sqrt(2.0))))


def _gelu_grad(x):
    return 0.5 * (1.0 + lax.erf(x * (1.0 / math.sqrt(2.0)))) + x * jnp.exp(-0.5 * x * x) * (1.0 / math.sqrt(2.0 * math.pi))


def _ffn_act_fwd(up, cw, cb, *, name):
    T, F2 = up.shape
    F = F2 // 2
    bt = _pick(T, ROW_BLOCK, SUBLANE)
    bc = _pick(F, 1408)
    nb = F // bc

    def kern(u_ref, g_ref, halo_ref, cw_ref, cb_ref, a_ref, ext_ref):
        i = pl.program_id(0)
        _fill_ext_past(ext_ref, halo_ref, g_ref[...], i, bt)
        gc = cb_ref[...] + _conv_past(ext_ref, cw_ref, FFN_CONV, bt)
        a_ref[...] = (_gelu(gc) * u_ref[...]).astype(BF16)

    return pl.pallas_call(
        kern, name=name, grid=(T // bt, nb),
        in_specs=[pl.BlockSpec((bt, bc), lambda i, j: (i, j)),
                  pl.BlockSpec((bt, bc), lambda i, j: (i, j + nb)),
                  _halo_prev(bt, bc, nb),
                  pl.BlockSpec((FFN_CONV, bc), lambda i, j: (0, j)),
                  pl.BlockSpec((1, bc), lambda i, j: (0, j))],
        out_specs=pl.BlockSpec((bt, bc), lambda i, j: (i, j)),
        out_shape=jax.ShapeDtypeStruct((T, F), BF16),
        scratch_shapes=[pltpu.VMEM((bt + HALO, bc), F32)],
        compiler_params=_params(("parallel", "parallel")),
    )(up, up, up, cw, cb)


def _ffn_act_bwd(da, up, cw, cb, *, name):
    T, F2 = up.shape
    F = F2 // 2
    bt = _pick(T, ROW_BLOCK, SUBLANE)
    bc = _pick(F, 1408)
    nb = F // bc
    K = FFN_CONV

    def kern(da_ref, u_ref, g_ref, halo_ref, cw_ref, cb_ref, du_ref, dgc_ref, dcb_ref, dcw_ref, ext_ref):
        i = pl.program_id(1)
        _fill_ext_past(ext_ref, halo_ref, g_ref[...], i, bt)
        gc = cb_ref[...] + _conv_past(ext_ref, cw_ref, K, bt)
        dav = da_ref[...]
        du_ref[...] = (dav * _gelu(gc)).astype(BF16)
        dgc = dav * u_ref[...] * _gelu_grad(gc)
        dgc_ref[...] = dgc

        @pl.when(i == 0)
        def _():
            dcb_ref[...] = jnp.zeros_like(dcb_ref)
            dcw_ref[...] = jnp.zeros_like(dcw_ref)

        dcb_ref[...] += jnp.sum(dgc, axis=0, keepdims=True)
        for k in range(K):
            dcw_ref[k:k + 1, :] += jnp.sum(dgc * ext_ref[pl.ds(HALO - (K - 1) + k, bt), :], axis=0, keepdims=True)

    blk = pl.BlockSpec((bt, bc), lambda j, i: (i, j))
    return pl.pallas_call(
        kern, name=name, grid=(nb, T // bt),
        in_specs=[blk, blk,
                  pl.BlockSpec((bt, bc), lambda j, i: (i, j + nb)),
                  pl.BlockSpec((HALO, bc), lambda j, i: (jnp.maximum(i * (bt // HALO) - 1, 0), j + nb)),
                  pl.BlockSpec((K, bc), lambda j, i: (0, j)),
                  pl.BlockSpec((1, bc), lambda j, i: (0, j))],
        out_specs=[blk, blk, pl.BlockSpec((1, bc), lambda j, i: (0, j)), pl.BlockSpec((K, bc), lambda j, i: (0, j))],
        out_shape=[jax.ShapeDtypeStruct((T, F2), BF16), jax.ShapeDtypeStruct((T, F), F32),
                   jax.ShapeDtypeStruct((1, F), F32), jax.ShapeDtypeStruct((K, F), F32)],
        scratch_shapes=[pltpu.VMEM((bt + HALO, bc), F32)],
        compiler_params=_params(("parallel", "arbitrary")),
    )(da, up, up, up, cw, cb)


def _dwconv_bwd_data(dgc, cw, K, into, col, *, name):
    T, C = dgc.shape
    bt = _pick(T, ROW_BLOCK, SUBLANE)
    bc = _pick(C, 1408)
    nt = T // bt
    last_halo = T // HALO - 1
    off = col // bc
    assert off * bc == col

    def kern(d_ref, halo_ref, cw_ref, into_ref, o_ref, ext_ref):
        i = pl.program_id(0)
        ext_ref[pl.ds(0, bt), :] = d_ref[...]
        ext_ref[pl.ds(bt, HALO), :] = jnp.where(i < nt - 1, halo_ref[...], 0.0)
        out = None
        for k in range(K):
            term = cw_ref[k:k + 1, :] * ext_ref[pl.ds(K - 1 - k, bt), :]
            out = term if out is None else out + term
        o_ref[...] = out.astype(o_ref.dtype)

    return pl.pallas_call(
        kern, name=name, grid=(nt, C // bc),
        in_specs=[pl.BlockSpec((bt, bc), lambda i, j: (i, j)),
                  pl.BlockSpec((HALO, bc), lambda i, j: (jnp.minimum((i + 1) * (bt // HALO), last_halo), j)),
                  pl.BlockSpec((K, bc), lambda i, j: (0, j)), _ANY],
        out_specs=pl.BlockSpec((bt, bc), lambda i, j: (i, j + off)),
        out_shape=jax.ShapeDtypeStruct(into.shape, into.dtype), input_output_aliases={3: 0},
        scratch_shapes=[pltpu.VMEM((bt + HALO, bc), F32)],
        compiler_params=_params(("parallel", "parallel")),
    )(dgc, dgc, cw, into)


def _ple_fwd(h, G, bg, pp, *, name):
    T, D = h.shape
    bt = _pick(T, ROW_BLOCK, SUBLANE)

    def kern(h_ref, G_ref, bg_ref, pp_ref, o_ref, ob_ref):
        out = h_ref[...] + _sigmoid(G_ref[...] + bg_ref[...]) * pp_ref[...]
        o_ref[...] = out
        ob_ref[...] = out.astype(BF16)

    row = pl.BlockSpec((bt, D), lambda i: (i, 0))
    vec = pl.BlockSpec((1, D), lambda i: (0, 0))
    return pl.pallas_call(
        kern, name=name, grid=(T // bt,), in_specs=[row, row, vec, row], out_specs=[row, row],
        out_shape=[jax.ShapeDtypeStruct((T, D), F32), jax.ShapeDtypeStruct((T, D), BF16)],
        compiler_params=_params(("parallel",)),
    )(h, G, bg, pp)


def _ple_bwd(dx, G, bg, pp, *, name):
    T, D = dx.shape
    bt = _pick(T, ROW_BLOCK, SUBLANE)

    def kern(dx_ref, G_ref, bg_ref, pp_ref, dG_ref, dpp_ref, dbg_ref):
        i = pl.program_id(0)
        gate = _sigmoid(G_ref[...] + bg_ref[...])
        dxv = dx_ref[...]
        dG = dxv * pp_ref[...] * gate * (1.0 - gate)
        dG_ref[...] = dG.astype(BF16)
        dpp_ref[...] = (dxv * gate).astype(BF16)

        @pl.when(i == 0)
        def _():
            dbg_ref[...] = jnp.zeros_like(dbg_ref)

        dbg_ref[...] += jnp.sum(dG, axis=0, keepdims=True)

    row = pl.BlockSpec((bt, D), lambda i: (i, 0))
    vec = pl.BlockSpec((1, D), lambda i: (0, 0))
    return pl.pallas_call(
        kern, name=name, grid=(T // bt,), in_specs=[row, row, vec, row], out_specs=[row, row, vec],
        out_shape=[jax.ShapeDtypeStruct((T, D), BF16), jax.ShapeDtypeStruct((T, D), BF16),
                   jax.ShapeDtypeStruct((1, D), F32)],
        compiler_params=_params(("arbitrary",)),
    )(dx, G, bg, pp)


def _fox_gate_fwd(proj, bf, *, name):
    T = proj.shape[0]
    bt = _pick(T, CUM_BLOCK, SUBLANE)
    fcol = 3 * D_MODEL // LANE

    def kern(f_ref, bf_ref, c_ref, carry_ref):
        i = pl.program_id(0)

        @pl.when(i == 0)
        def _():
            carry_ref[...] = jnp.zeros_like(carry_ref)

        x = f_ref[...] + bf_ref[...]
        lf = jnp.minimum(x, 0.0) - _log1p(jnp.exp(-jnp.abs(x)))
        cs = _tri_matmul(_tri(bt, True), lf) + carry_ref[...]
        c_ref[...] = cs
        carry_ref[...] = cs[bt - 1:bt, :]

    return pl.pallas_call(
        kern, name=name, grid=(T // bt,),
        in_specs=[pl.BlockSpec((bt, LANE), lambda i: (i, fcol)), pl.BlockSpec((1, LANE), lambda i: (0, 0))],
        out_specs=pl.BlockSpec((bt, LANE), lambda i: (i, 0)),
        out_shape=jax.ShapeDtypeStruct((T, LANE), F32),
        scratch_shapes=[pltpu.VMEM((1, LANE), F32)],
        compiler_params=_params(("arbitrary",)),
    )(proj, bf)


def _fox_gate_bwd(dc, proj, bf, *, name):
    T = proj.shape[0]
    bt = _pick(T, CUM_BLOCK, SUBLANE)
    nb = T // bt
    fcol = 3 * D_MODEL // LANE

    def kern(dc_ref, f_ref, bf_ref, df_ref, dbf_ref, carry_ref):
        i = pl.program_id(0)

        @pl.when(i == 0)
        def _():
            carry_ref[...] = jnp.zeros_like(carry_ref)
            dbf_ref[...] = jnp.zeros_like(dbf_ref)

        dlf = _tri_matmul(_tri(bt, False), dc_ref[...]) + carry_ref[...]
        carry_ref[...] = dlf[0:1, :]
        x = f_ref[...] + bf_ref[...]
        lane = lax.broadcasted_iota(jnp.int32, (bt, LANE), 1)
        df = jnp.where(lane < ATTN_HEADS, dlf / (1.0 + jnp.exp(x)), 0.0)
        df_ref[...] = df
        dbf_ref[...] += jnp.sum(df, axis=0, keepdims=True)

    return pl.pallas_call(
        kern, name=name, grid=(nb,),
        in_specs=[pl.BlockSpec((bt, LANE), lambda i: (nb - 1 - i, 0)),
                  pl.BlockSpec((bt, LANE), lambda i: (nb - 1 - i, fcol)),
                  pl.BlockSpec((1, LANE), lambda i: (0, 0))],
        out_specs=[pl.BlockSpec((bt, LANE), lambda i: (nb - 1 - i, 0)), pl.BlockSpec((1, LANE), lambda i: (0, 0))],
        out_shape=[jax.ShapeDtypeStruct((T, LANE), F32), jax.ShapeDtypeStruct((1, LANE), F32)],
        scratch_shapes=[pltpu.VMEM((1, LANE), F32)],
        compiler_params=_params(("arbitrary",)),
    )(dc, proj, bf)


_NT = (((1,), (1,)), ((), ()))
_TN = (((0,), (0,)), ((), ()))


def _dot(a, b, dims=None):
    if dims is None:
        return jnp.dot(a, b, preferred_element_type=F32)
    return lax.dot_general(a, b, dims, preferred_element_type=F32)


LOG2E = 1.0 / math.log(2.0)
LN2 = math.log(2.0)
Q_SCALE = 0.125 * LOG2E
HALF = LANE // 2
L_LANE = (HALF, 0)


def _attn_prep(proj, *, name):
    T = proj.shape[0]
    bt = _pick(T, ATTN_BLOCK)

    def kern(q_ref, k_ref, v_ref, qa_ref, qb_ref, kk_ref, ka_ref, kb_ref, vv_ref, va_ref, vb_ref):
        lane = lax.broadcasted_iota(jnp.int32, (bt, LANE), 1)
        lo = lane < HALF
        q = q_ref[...] * Q_SCALE
        k = k_ref[...]
        v = v_ref[...]
        qa_ref[...] = jnp.where(lo, q, 0.0).astype(BF16)
        qb_ref[...] = jnp.where(lo, 0.0, q).astype(BF16)
        kk_ref[...] = k.astype(BF16)
        ka_ref[...] = jnp.where(lo, k, 0.0).astype(BF16)
        kb_ref[...] = jnp.where(lo, 0.0, k).astype(BF16)
        vv_ref[...] = v.astype(BF16)
        va_ref[...] = jnp.where(lo, v, jnp.where(lane == L_LANE[0], 1.0, 0.0)).astype(BF16)
        vb_ref[...] = jnp.where(lo, jnp.where(lane == L_LANE[1], 1.0, 0.0), v).astype(BF16)

    kcol, vcol = D_MODEL // LANE, 2 * D_MODEL // LANE
    out = pl.BlockSpec((bt, LANE), lambda i, hp: (i, hp))
    return pl.pallas_call(
        kern, name=name, grid=(T // bt, HEAD_PAIRS),
        in_specs=[out, pl.BlockSpec((bt, LANE), lambda i, hp: (i, kcol + hp)),
                  pl.BlockSpec((bt, LANE), lambda i, hp: (i, vcol + hp))],
        out_specs=[out] * 8, out_shape=[jax.ShapeDtypeStruct((T, D_MODEL), BF16)] * 8,
        compiler_params=_params(("parallel", "parallel")),
    )(proj, proj, proj)


def _attn_fwd(qa, qb, kk, va, vb, cT, *, name):
    T = qa.shape[0]
    tb = _pick(T, ATTN_BLOCK)
    nq = T // tb
    rep = tb // LANE

    def kern(qa_ref, qb_ref, k_ref, va_ref, vb_ref, c_ref, o_ref, ob_ref, lsea_ref, lseb_ref, m_ref, acc_ref):
        qi = pl.program_id(1)
        ki = pl.program_id(2)

        @pl.when(ki == 0)
        def _():
            m_ref[...] = jnp.full_like(m_ref, NEG)
            acc_ref[...] = jnp.zeros_like(acc_ref)

        def step(diag):
            k = k_ref[...]
            for h, (q_ref, v_ref) in enumerate(((qa_ref, va_ref), (qb_ref, vb_ref))):
                s = _dot(q_ref[...], k, _NT) - c_ref[h:h + 1, :]
                if diag:
                    r = lax.broadcasted_iota(jnp.int32, (tb, tb), 0)
                    c = lax.broadcasted_iota(jnp.int32, (tb, tb), 1)
                    s = jnp.where(c <= r, s, NEG)
                m_prev = m_ref[h]
                m_new = jnp.maximum(m_prev, jnp.max(s, axis=1, keepdims=True))
                p = jnp.exp2(s - jnp.tile(m_new, (1, rep)))
                acc_ref[h] = acc_ref[h] * jnp.exp2(m_prev - m_new) + _dot(p.astype(BF16), v_ref[...])
                m_ref[h] = m_new

        @pl.when(ki < qi)
        def _():
            step(False)

        @pl.when(ki == qi)
        def _():
            step(True)
            lo = lax.broadcasted_iota(jnp.int32, (tb, LANE), 1) < HALF
            a0, a1 = acc_ref[0], acc_ref[1]
            l0 = a0[:, L_LANE[0]:L_LANE[0] + 1]
            l1 = a1[:, L_LANE[1]:L_LANE[1] + 1]
            o = jnp.where(lo, a0 / l0, a1 / l1)
            o_ref[...] = o
            ob_ref[...] = o.astype(BF16)
            lsea_ref[...] = m_ref[0] + jnp.log(l0) * LOG2E
            lseb_ref[...] = m_ref[1] + jnp.log(l1) * LOG2E

    qspec = pl.BlockSpec((tb, LANE), lambda hp, qi, ki: (qi, hp))
    kspec = pl.BlockSpec((tb, LANE), lambda hp, qi, ki: (jnp.minimum(ki, qi), hp))
    return pl.pallas_call(
        kern, name=name, grid=(HEAD_PAIRS, nq, nq),
        in_specs=[qspec, qspec, kspec, kspec, kspec,
                  pl.BlockSpec((None, 2, tb), lambda hp, qi, ki: (hp, 0, jnp.minimum(ki, qi)))],
        out_specs=[qspec, qspec, qspec, qspec],
        out_shape=[jax.ShapeDtypeStruct((T, D_MODEL), F32), jax.ShapeDtypeStruct((T, D_MODEL), BF16),
                   jax.ShapeDtypeStruct((T, D_MODEL), F32), jax.ShapeDtypeStruct((T, D_MODEL), F32)],
        scratch_shapes=[pltpu.VMEM((2, tb, LANE), F32), pltpu.VMEM((2, tb, LANE), F32)],
        compiler_params=_params(("parallel", "parallel", "arbitrary")),
    )(qa, qb, kk, va, vb, cT)


def _attn_bwd_prep(do, o, *, name):
    T, D = do.shape
    bt = _pick(T, ATTN_BLOCK)

    def kern(do_ref, o_ref, doa_ref, dob_ref, dlta_ref, dltb_ref):
        lo = lax.broadcasted_iota(jnp.int32, (bt, LANE), 1) < HALF
        dov = do_ref[...]
        prod = dov * o_ref[...]
        doa_ref[...] = jnp.where(lo, dov, 0.0).astype(BF16)
        dob_ref[...] = jnp.where(lo, 0.0, dov).astype(BF16)
        dlta_ref[...] = jnp.broadcast_to(jnp.sum(jnp.where(lo, prod, 0.0), axis=1, keepdims=True), (bt, LANE))
        dltb_ref[...] = jnp.broadcast_to(jnp.sum(jnp.where(lo, 0.0, prod), axis=1, keepdims=True), (bt, LANE))

    blk = pl.BlockSpec((bt, LANE), lambda i, hp: (i, hp))
    return pl.pallas_call(
        kern, name=name, grid=(T // bt, HEAD_PAIRS), in_specs=[blk, blk], out_specs=[blk] * 4,
        out_shape=[jax.ShapeDtypeStruct((T, D), BF16)] * 2 + [jax.ShapeDtypeStruct((T, D), F32)] * 2,
        compiler_params=_params(("parallel", "parallel")),
    )(do, o)


def _attn_bwd(qa, qb, kk, ka, kb, vv, doa, dob, lsea, lseb, dlta, dltb, cT, *, name):
    T = qa.shape[0]
    tb = _pick(T, ATTN_BLOCK)
    nq = T // tb
    rep = tb // LANE

    def kern(qa_ref, qb_ref, k_ref, ka_ref, kb_ref, v_ref, doa_ref, dob_ref, lsea_ref, lseb_ref, dlta_ref, dltb_ref,
             c_ref, dq_ref, dk_ref, dv_ref, dc_ref, dcq_ref):
        ki = pl.program_id(1)
        qi = pl.program_id(2)

        @pl.when(jnp.logical_and(ki == 0, qi == 0))
        def _():
            dq_ref[...] = jnp.zeros_like(dq_ref)
            dcq_ref[...] = jnp.zeros_like(dcq_ref)

        @pl.when(qi == 0)
        def _():
            dk_ref[...] = jnp.zeros_like(dk_ref)
            dv_ref[...] = jnp.zeros_like(dv_ref)
            dc_ref[...] = jnp.zeros_like(dc_ref)

        def step(diag):
            k = k_ref[...]
            v = v_ref[...]
            dq = None
            dk = None
            dv = None
            row_sums = []
            heads = ((qa_ref, ka_ref, doa_ref, lsea_ref, dlta_ref), (qb_ref, kb_ref, dob_ref, lseb_ref, dltb_ref))
            for h, (q_ref, km_ref, do_ref, lse_ref, dlt_ref) in enumerate(heads):
                q = q_ref[...]
                dom = do_ref[...]
                s = _dot(q, k, _NT) - c_ref[h:h + 1, :]
                if diag:
                    r = lax.broadcasted_iota(jnp.int32, (tb, tb), 0)
                    c = lax.broadcasted_iota(jnp.int32, (tb, tb), 1)
                    s = jnp.where(c <= r, s, NEG)
                p = jnp.exp2(s - jnp.tile(lse_ref[...], (1, rep)))
                ds = p * (_dot(dom, v, _NT) - jnp.tile(dlt_ref[...], (1, rep)))
                dc_ref[h:h + 1, :] -= jnp.sum(ds, axis=0, keepdims=True)
                row_sums.append(jnp.sum(ds, axis=1, keepdims=True))
                dsb = ds.astype(BF16)
                tv = _dot(p.astype(BF16), dom, _TN)
                tk = _dot(dsb, q, _TN)
                tq = _dot(dsb, km_ref[...])
                dv = tv if dv is None else dv + tv
                dk = tk if dk is None else dk + tk
                dq = tq if dq is None else dq + tq
            dv_ref[...] += dv
            dk_ref[...] += dk * LN2
            rows = pl.ds(pl.multiple_of(qi * tb, tb), tb)
            dq_ref[rows, :] += dq * 0.125
            lo = lax.broadcasted_iota(jnp.int32, (tb, LANE), 1) < HALF
            dcq_ref[rows, :] += jnp.where(lo, row_sums[0], row_sums[1])

        @pl.when(qi > ki)
        def _():
            step(False)

        @pl.when(qi == ki)
        def _():
            step(True)

    qspec = pl.BlockSpec((tb, LANE), lambda hp, ki, qi: (jnp.maximum(qi, ki), hp))
    kspec = pl.BlockSpec((tb, LANE), lambda hp, ki, qi: (ki, hp))
    cspec = pl.BlockSpec((None, 2, tb), lambda hp, ki, qi: (hp, 0, ki))
    qacc = pl.BlockSpec((T, LANE), lambda hp, ki, qi: (0, hp), pipeline_mode=pl.Buffered(1))
    return pl.pallas_call(
        kern, name=name, grid=(HEAD_PAIRS, nq, nq),
        in_specs=[qspec, qspec, kspec, kspec, kspec, kspec, qspec, qspec, qspec, qspec, qspec, qspec, cspec],
        out_specs=[qacc, kspec, kspec, cspec, qacc],
        out_shape=[jax.ShapeDtypeStruct((T, D_MODEL), F32)] * 3 + [jax.ShapeDtypeStruct((HEAD_PAIRS, 2, T), F32),
                                                                   jax.ShapeDtypeStruct((T, D_MODEL), F32)],
        compiler_params=_params(("parallel", "arbitrary", "arbitrary"), vmem_mb=56),
    )(qa, qb, kk, ka, kb, vv, doa, dob, lsea, lseb, dlta, dltb, cT)


def _ssd_dt_fwd(proj, dt_bias, a_log, *, name):
    T = proj.shape[0]
    Q = SSM_CHUNK
    col = (2 * SSM_D_INNER + 2 * SSM_GROUPS * SSM_STATE) // LANE

    def kern(raw_ref, b_ref, al_ref, dt_ref, ac_ref):
        dt = _softplus(raw_ref[...] + b_ref[...])
        dt_ref[...] = dt
        ac_ref[...] = _tri_matmul(_tri(Q, True), dt * (-jnp.exp(al_ref[...])))

    vec = pl.BlockSpec((1, LANE), lambda i: (0, 0))
    blk = pl.BlockSpec((Q, LANE), lambda i: (i, 0))
    return pl.pallas_call(
        kern, name=name, grid=(T // Q,),
        in_specs=[pl.BlockSpec((Q, LANE), lambda i: (i, col)), vec, vec], out_specs=[blk, blk],
        out_shape=[jax.ShapeDtypeStruct((T, LANE), F32)] * 2,
        compiler_params=_params(("parallel",)),
    )(proj, dt_bias, a_log)


def _ssd_dt_bwd(da_a, da_b, ddt_a, ddt_b, dt, proj, dt_bias, a_log, into, *, name):
    T = proj.shape[0]
    Q = SSM_CHUNK
    col = (2 * SSM_D_INNER + 2 * SSM_GROUPS * SSM_STATE) // LANE

    def kern(daa_ref, dab_ref, dda_ref, ddb_ref, dt_ref, raw_ref, b_ref, al_ref, into_ref, draw_ref, dal_ref, db_ref,
             acc_ref):
        i = pl.program_id(0)

        @pl.when(i == 0)
        def _():
            acc_ref[...] = jnp.zeros_like(acc_ref)
            db_ref[...] = jnp.zeros_like(db_ref)

        A = -jnp.exp(al_ref[...])
        ddA = _tri_matmul(_tri(Q, False), daa_ref[...] + dab_ref[...])
        ddt = dda_ref[...] + ddb_ref[...] + ddA * A
        acc_ref[...] += jnp.sum(ddA * dt_ref[...], axis=0, keepdims=True)
        lane = lax.broadcasted_iota(jnp.int32, (Q, LANE), 1)
        draw = jnp.where(lane < SSM_HEADS, ddt * _sigmoid(raw_ref[...] + b_ref[...]), 0.0)
        draw_ref[...] = draw.astype(BF16)
        db_ref[...] += jnp.sum(draw, axis=0, keepdims=True)
        dal_ref[...] = acc_ref[...] * A

    vec = pl.BlockSpec((1, LANE), lambda i: (0, 0))
    blk = pl.BlockSpec((Q, LANE), lambda i: (i, 0))
    return pl.pallas_call(
        kern, name=name, grid=(T // Q,),
        in_specs=[blk, blk, blk, blk, blk, pl.BlockSpec((Q, LANE), lambda i: (i, col)), vec, vec, _ANY],
        out_specs=[pl.BlockSpec((Q, LANE), lambda i: (i, col)), vec, vec],
        out_shape=[jax.ShapeDtypeStruct(into.shape, into.dtype), jax.ShapeDtypeStruct((1, LANE), F32),
                   jax.ShapeDtypeStruct((1, LANE), F32)],
        input_output_aliases={8: 0},
        scratch_shapes=[pltpu.VMEM((1, LANE), F32)],
        compiler_params=_params(("arbitrary",)),
    )(da_a, da_b, ddt_a, ddt_b, dt, proj, dt_bias, a_log, into)


def _conv_silu_fwd(proj, cw, cb, *, name):
    T = proj.shape[0]
    C = SSM_XBC
    bt = _pick(T, ROW_BLOCK, SUBLANE)
    bc = 1024
    off = SSM_D_INNER // bc

    def kern(x_ref, halo_ref, cw_ref, cb_ref, o_ref, ext_ref):
        i = pl.program_id(0)
        _fill_ext_past(ext_ref, halo_ref, x_ref[...], i, bt)
        pre = cb_ref[...] + _conv_past(ext_ref, cw_ref, SSM_CONV, bt)
        o_ref[...] = pre * _sigmoid(pre)

    return pl.pallas_call(
        kern, name=name, grid=(T // bt, C // bc),
        in_specs=[pl.BlockSpec((bt, bc), lambda i, j: (i, j + off)), _halo_prev(bt, bc, off),
                  pl.BlockSpec((SSM_CONV, bc), lambda i, j: (0, j)), pl.BlockSpec((1, bc), lambda i, j: (0, j))],
        out_specs=pl.BlockSpec((bt, bc), lambda i, j: (i, j)),
        out_shape=jax.ShapeDtypeStruct((T, C), F32),
        scratch_shapes=[pltpu.VMEM((bt + HALO, bc), F32)],
        compiler_params=_params(("parallel", "parallel")),
    )(proj, proj, cw, cb)


def _conv_silu_bwd(dxbc, proj, cw, cb, *, name):
    T = proj.shape[0]
    C = SSM_XBC
    K = SSM_CONV
    bt = _pick(T, ROW_BLOCK, SUBLANE)
    bc = 1024
    off = SSM_D_INNER // bc

    def kern(d_ref, x_ref, halo_ref, cw_ref, cb_ref, dpre_ref, dcb_ref, dcw_ref, ext_ref):
        i = pl.program_id(1)
        _fill_ext_past(ext_ref, halo_ref, x_ref[...], i, bt)
        pre = cb_ref[...] + _conv_past(ext_ref, cw_ref, K, bt)
        sg = _sigmoid(pre)
        dpre = d_ref[...] * sg * (1.0 + pre * (1.0 - sg))
        dpre_ref[...] = dpre

        @pl.when(i == 0)
        def _():
            dcb_ref[...] = jnp.zeros_like(dcb_ref)
            dcw_ref[...] = jnp.zeros_like(dcw_ref)

        dcb_ref[...] += jnp.sum(dpre, axis=0, keepdims=True)
        for k in range(K):
            dcw_ref[k:k + 1, :] += jnp.sum(dpre * ext_ref[pl.ds(HALO - (K - 1) + k, bt), :], axis=0, keepdims=True)

    blk = pl.BlockSpec((bt, bc), lambda j, i: (i, j))
    return pl.pallas_call(
        kern, name=name, grid=(C // bc, T // bt),
        in_specs=[blk, pl.BlockSpec((bt, bc), lambda j, i: (i, j + off)),
                  pl.BlockSpec((HALO, bc), lambda j, i: (jnp.maximum(i * (bt // HALO) - 1, 0), j + off)),
                  pl.BlockSpec((K, bc), lambda j, i: (0, j)), pl.BlockSpec((1, bc), lambda j, i: (0, j))],
        out_specs=[blk, pl.BlockSpec((1, bc), lambda j, i: (0, j)), pl.BlockSpec((K, bc), lambda j, i: (0, j))],
        out_shape=[jax.ShapeDtypeStruct((T, C), F32), jax.ShapeDtypeStruct((1, C), F32),
                   jax.ShapeDtypeStruct((K, C), F32)],
        scratch_shapes=[pltpu.VMEM((bt + HALO, bc), F32)],
        compiler_params=_params(("parallel", "arbitrary")),
    )(dxbc, proj, proj, cw, cb)


_GP = SSM_D_INNER // SSM_GROUPS
_HPG = SSM_HEADS // SSM_GROUPS
_PH = SSM_D_INNER // SSM_HEADS


def _head_masks(rows):
    lane = lax.broadcasted_iota(jnp.int32, (rows, _GP), 1)
    return [jnp.logical_and(lane >= r * _PH, lane < (r + 1) * _PH) for r in range(_HPG)]


def _ssd_specs(idx):
    Q, N = SSM_CHUNK, SSM_STATE
    bcol, ccol = SSM_D_INNER // N, SSM_D_INNER // N + SSM_GROUPS
    return dict(
        x=pl.BlockSpec((Q, _GP), lambda j, g: (idx(j), g)),
        B=pl.BlockSpec((Q, N), lambda j, g: (idx(j), bcol + g)),
        C=pl.BlockSpec((Q, N), lambda j, g: (idx(j), ccol + g)),
        col=pl.BlockSpec((Q, LANE), lambda j, g: (idx(j), g)),
        row=pl.BlockSpec((None, _HPG, Q), lambda j, g: (g, 0, idx(j))),
        st=pl.BlockSpec((N, _GP), lambda j, g: (idx(j), g)),
    )


def _ssd_scan_fwd(xbc, dtc, acc_, dtr, acr, *, name):
    T = xbc.shape[0]
    Q, N = SSM_CHUNK, SSM_STATE
    nc = T // Q
    sp = _ssd_specs(lambda j: j)

    def kern(x_ref, B_ref, C_ref, dtc_ref, ac_ref, dtr_ref, ar_ref, ys_ref, st_ref, state_ref):
        j = pl.program_id(0)
        g = pl.program_id(1)

        @pl.when(j == 0)
        def _():
            state_ref[g] = jnp.zeros((N, _GP), F32)

        S = state_ref[g]
        st_ref[...] = S
        x = x_ref[...]
        xb = x.astype(BF16)
        Bb = B_ref[...].astype(BF16)
        Cb = C_ref[...].astype(BF16)
        CB = _dot(Cb, Bb, _NT)
        r_i = lax.broadcasted_iota(jnp.int32, (Q, Q), 0)
        c_i = lax.broadcasted_iota(jnp.int32, (Q, Q), 1)
        tri = c_i <= r_i
        masks = _head_masks(Q)
        masks1 = _head_masks(1)
        y = jnp.zeros((Q, _GP), F32)
        El = jnp.zeros((Q, _GP), F32)
        Wl = jnp.zeros((Q, _GP), F32)
        decl = jnp.zeros((1, _GP), F32)
        for r in range(_HPG):
            a_c = ac_ref[:, r:r + 1]
            a_r = ar_ref[r:r + 1, :]
            dt_c = dtc_ref[:, r:r + 1]
            dt_r = dtr_ref[r:r + 1, :]
            L = jnp.exp(jnp.where(tri, a_c - a_r, NEG))
            W = CB * L * dt_r
            y = jnp.where(masks[r], _dot(W.astype(BF16), xb), y)
            a_q = a_c[Q - 1:Q, :]
            El = jnp.where(masks[r], jnp.exp(a_c), El)
            Wl = jnp.where(masks[r], jnp.exp(a_q - a_c) * dt_c, Wl)
            decl = jnp.where(masks1[r], jnp.exp(a_q), decl)
        ys_ref[...] = y + _dot(Cb, S.astype(BF16)) * El
        state_ref[g] = S * decl + _dot(Bb, (x * Wl).astype(BF16), _TN)

    return pl.pallas_call(
        kern, name=name, grid=(nc, SSM_GROUPS),
        in_specs=[sp["x"], sp["B"], sp["C"], sp["col"], sp["col"], sp["row"], sp["row"]],
        out_specs=[sp["x"], sp["st"]],
        out_shape=[jax.ShapeDtypeStruct((T, SSM_D_INNER), F32), jax.ShapeDtypeStruct((nc * N, SSM_D_INNER), F32)],
        scratch_shapes=[pltpu.VMEM((SSM_GROUPS, N, _GP), F32)],
        compiler_params=_params(("arbitrary", "arbitrary")),
    )(xbc, xbc, xbc, dtc, acc_, dtr, acr)


def _ssd_scan_bwd(xbc, dys, dskip, st, dtc, acc_, dtr, acr, *, name):
    T = xbc.shape[0]
    Q, N = SSM_CHUNK, SSM_STATE
    nc = T // Q
    sp = _ssd_specs(lambda j: nc - 1 - j)

    def kern(x_ref, B_ref, C_ref, dy_ref, dsk_ref, st_ref, dtc_ref, ac_ref, dtr_ref, ar_ref,
             dx_ref, dB_ref, dC_ref, dac_ref, dar_ref, ddc_ref, ddr_ref, dstate_ref):
        j = pl.program_id(0)
        g = pl.program_id(1)

        @pl.when(j == 0)
        def _():
            dstate_ref[g] = jnp.zeros((N, _GP), F32)

        dS = dstate_ref[g]
        dSb = dS.astype(BF16)
        S = st_ref[...]
        Sb = S.astype(BF16)
        x = x_ref[...]
        xb = x.astype(BF16)
        Bb = B_ref[...].astype(BF16)
        Cb = C_ref[...].astype(BF16)
        dy = dy_ref[...]
        CB = _dot(Cb, Bb, _NT)
        BdS = _dot(Bb, dSb)
        hx = BdS * x
        yd = _dot(Cb, Sb) * dy
        dSS = dS * S
        r_i = lax.broadcasted_iota(jnp.int32, (Q, Q), 0)
        c_i = lax.broadcasted_iota(jnp.int32, (Q, Q), 1)
        tri = c_i <= r_i
        last_row = lax.broadcasted_iota(jnp.int32, (Q, 1), 0) == Q - 1
        lane128 = lax.broadcasted_iota(jnp.int32, (Q, LANE), 1)
        masks = _head_masks(Q)
        masksN = _head_masks(N)
        masks1 = _head_masks(1)
        zeros = jnp.zeros((Q, _GP), F32)
        dxi, El, Wl = zeros, zeros, zeros
        decl = jnp.zeros((1, _GP), F32)
        dBacc = jnp.zeros((Q, N), F32)
        dCacc = jnp.zeros((Q, N), F32)
        dacol = jnp.zeros((Q, LANE), F32)
        ddcol = jnp.zeros((Q, LANE), F32)
        for r in range(_HPG):
            hm = masks[r]
            a_c = ac_ref[:, r:r + 1]
            a_r = ar_ref[r:r + 1, :]
            dt_c = dtc_ref[:, r:r + 1]
            dt_r = dtr_ref[r:r + 1, :]
            L = jnp.exp(jnp.where(tri, a_c - a_r, NEG))
            GL = CB * L
            W = GL * dt_r
            dym = jnp.where(hm, dy, 0.0).astype(BF16)
            dW = _dot(dym, xb, _NT)
            E = dW * W
            da_c = jnp.sum(E, axis=1, keepdims=True)
            dar_ref[r:r + 1, :] = -jnp.sum(E, axis=0, keepdims=True)
            ddr_ref[r:r + 1, :] = jnp.sum(dW * GL, axis=0, keepdims=True)
            dGb = (dW * L * dt_r).astype(BF16)
            dCacc = dCacc + _dot(dGb, Bb)
            dBacc = dBacc + _dot(dGb, Cb, _TN)
            dxi = dxi + _dot(W.astype(BF16), dym, _TN)
            a_q = a_c[Q - 1:Q, :]
            e_c = jnp.exp(a_c)
            eq_c = jnp.exp(a_q - a_c)
            w_c = eq_c * dt_c
            ydr = jnp.sum(jnp.where(hm, yd, 0.0), axis=1, keepdims=True) * e_c
            h_c = jnp.sum(jnp.where(hm, hx, 0.0), axis=1, keepdims=True)
            hw = h_c * w_c
            dss = jnp.sum(jnp.sum(jnp.where(masksN[r], dSS, 0.0), axis=1, keepdims=True), axis=0, keepdims=True)
            s_q = jnp.sum(hw, axis=0, keepdims=True) + jnp.exp(a_q) * dss
            da_c = da_c + ydr - hw + jnp.where(last_row, s_q, 0.0)
            dacol = jnp.where(lane128 == r, da_c, dacol)
            ddcol = jnp.where(lane128 == r, h_c * eq_c, ddcol)
            El = jnp.where(hm, e_c, El)
            Wl = jnp.where(hm, w_c, Wl)
            decl = jnp.where(masks1[r], jnp.exp(a_q), decl)
        dx_ref[...] = dxi + BdS * Wl + dsk_ref[...]
        dB_ref[...] = dBacc + _dot((x * Wl).astype(BF16), dSb, _NT)
        dyE = (dy * El).astype(BF16)
        dC_ref[...] = dCacc + _dot(dyE, Sb, _NT)
        dac_ref[...] = dacol
        ddc_ref[...] = ddcol
        dstate_ref[g] = dS * decl + _dot(Cb, dyE, _TN)

    idx = lambda j: nc - 1 - j
    bcblk = pl.BlockSpec((Q, N), lambda j, g: (idx(j), g))
    return pl.pallas_call(
        kern, name=name, grid=(nc, SSM_GROUPS),
        in_specs=[sp["x"], sp["B"], sp["C"], sp["x"], sp["x"], sp["st"], sp["col"], sp["col"], sp["row"], sp["row"]],
        out_specs=[sp["x"], bcblk, bcblk, sp["col"], sp["row"], sp["col"], sp["row"]],
        out_shape=[jax.ShapeDtypeStruct((T, SSM_D_INNER), F32),
                   jax.ShapeDtypeStruct((T, SSM_GROUPS * N), F32), jax.ShapeDtypeStruct((T, SSM_GROUPS * N), F32),
                   jax.ShapeDtypeStruct((T, SSM_GROUPS * LANE), F32), jax.ShapeDtypeStruct((SSM_GROUPS, _HPG, T), F32),
                   jax.ShapeDtypeStruct((T, SSM_GROUPS * LANE), F32), jax.ShapeDtypeStruct((SSM_GROUPS, _HPG, T), F32)],
        scratch_shapes=[pltpu.VMEM((SSM_GROUPS, N, _GP), F32)],
        compiler_params=_params(("arbitrary", "arbitrary")),
    )(xbc, xbc, xbc, dys, dskip, st, dtc, acc_, dtr, acr)


def _gate_norm_fwd(ys, xbc, proj, d_exp, norm_w, *, name):
    T = ys.shape[0]
    bt = _pick(T, NARROW_ROW_BLOCK, SUBLANE)

    def kern(ys_ref, x_ref, z_ref, d_ref, w_ref, o_ref):
        z = z_ref[...]
        yz = (ys_ref[...] + d_ref[...] * x_ref[...]) * (z * _sigmoid(z))
        rstd = lax.rsqrt(jnp.mean(yz * yz, axis=-1, keepdims=True) + RMS_EPS)
        o_ref[...] = (yz * rstd * w_ref[...]).astype(BF16)

    blk = pl.BlockSpec((bt, _GP), lambda i, g: (i, g))
    vec = pl.BlockSpec((1, _GP), lambda i, g: (0, g))
    return pl.pallas_call(
        kern, name=name, grid=(T // bt, SSM_GROUPS), in_specs=[blk, blk, blk, vec, vec], out_specs=blk,
        out_shape=jax.ShapeDtypeStruct((T, SSM_D_INNER), BF16), compiler_params=_params(("parallel", "parallel")),
    )(ys, xbc, proj, d_exp, norm_w)


def _gate_norm_bwd(dyn, ys, xbc, proj, d_exp, norm_w, *, name):
    T = ys.shape[0]
    bt = _pick(T, NARROW_ROW_BLOCK, SUBLANE)

    def kern(dyn_ref, ys_ref, x_ref, z_ref, d_ref, w_ref, dz_ref, dys_ref, dsk_ref, dw_ref, dd_ref):
        i = pl.program_id(1)
        z = z_ref[...]
        x = x_ref[...]
        sg = _sigmoid(z)
        sz = z * sg
        y = ys_ref[...] + d_ref[...] * x
        yz = y * sz
        rstd = lax.rsqrt(jnp.mean(yz * yz, axis=-1, keepdims=True) + RMS_EPS)
        yhat = yz * rstd
        dynv = dyn_ref[...]
        gg = dynv * w_ref[...]
        dyz = rstd * (gg - yhat * jnp.mean(gg * yhat, axis=-1, keepdims=True))
        dy = dyz * sz
        dz_ref[...] = (dyz * y * sg * (1.0 + z * (1.0 - sg))).astype(BF16)
        dys_ref[...] = dy
        dsk_ref[...] = dy * d_ref[...]

        @pl.when(i == 0)
        def _():
            dw_ref[...] = jnp.zeros_like(dw_ref)
            dd_ref[...] = jnp.zeros_like(dd_ref)

        dw_ref[...] += jnp.sum(dynv * yhat, axis=0, keepdims=True)
        dd_ref[...] += jnp.sum(dy * x, axis=0, keepdims=True)

    blk = pl.BlockSpec((bt, _GP), lambda g, i: (i, g))
    vec = pl.BlockSpec((1, _GP), lambda g, i: (0, g))
    act = jax.ShapeDtypeStruct((T, SSM_D_INNER), F32)
    par = jax.ShapeDtypeStruct((1, SSM_D_INNER), F32)
    return pl.pallas_call(
        kern, name=name, grid=(SSM_GROUPS, T // bt), in_specs=[blk, blk, blk, blk, vec, vec],
        out_specs=[blk, blk, blk, vec, vec],
        out_shape=[jax.ShapeDtypeStruct((T, SSM_IN_PAD), BF16), act, act, par, par],
        compiler_params=_params(("parallel", "arbitrary")),
    )(dyn, ys, xbc, proj, d_exp, norm_w)


def _loss_head(y, target, *, name):
    T, D = y.shape
    bt = _pick(T, ROW_BLOCK, SUBLANE)

    def kern(y_ref, t_ref, l_ref, dy_ref):
        i = pl.program_id(0)
        err = y_ref[...] - t_ref[...]
        dy_ref[...] = err * (1.0 / D)

        @pl.when(i == 0)
        def _():
            l_ref[...] = jnp.zeros_like(l_ref)

        l_ref[...] += jnp.sum(err * err, axis=0, keepdims=True) * (0.5 / D)

    row = pl.BlockSpec((bt, D), lambda i: (i, 0))
    vec = pl.BlockSpec((1, D), lambda i: (0, 0))
    return pl.pallas_call(
        kern, name=name, grid=(T // bt,), in_specs=[row, row], out_specs=[vec, row],
        out_shape=[jax.ShapeDtypeStruct((1, D), F32), jax.ShapeDtypeStruct((T, D), F32)],
        compiler_params=_params(("arbitrary",)),
    )(y, target)


def _adamw(w, g, m, v, *, name):
    R, C = w.shape
    br = _pick(R, 512, SUBLANE)

    def kern(w_ref, g_ref, m_ref, v_ref, d_ref, nm_ref, nv_ref):
        gv = g_ref[...]
        nm = ADAM_B1 * m_ref[...] + (1.0 - ADAM_B1) * gv
        nv = ADAM_B2 * v_ref[...] + (1.0 - ADAM_B2) * (gv * gv)
        m_hat = nm / (1.0 - ADAM_B1 ** ADAM_STEP)
        v_hat = nv / (1.0 - ADAM_B2 ** ADAM_STEP)
        d_ref[...] = -ADAM_LR * (m_hat / (jnp.sqrt(v_hat) + ADAM_EPS) + ADAM_WD * w_ref[...])
        nm_ref[...] = nm
        nv_ref[...] = nv

    blk = pl.BlockSpec((br, C), lambda i: (i, 0))
    return pl.pallas_call(
        kern, name=name, grid=(R // br,), in_specs=[blk] * 4, out_specs=[blk] * 3,
        out_shape=[jax.ShapeDtypeStruct((R, C), F32)] * 3, compiler_params=_params(("parallel",)),
    )(w, g, m, v)


def _add2(a, b, *, name):
    shape = a.shape
    a2, b2 = a.reshape(-1, shape[-1]), b.reshape(-1, shape[-1])
    R, C = a2.shape
    br = _pick(R, 512, SUBLANE)

    def kern(a_ref, b_ref, o_ref):
        o_ref[...] = a_ref[...] + b_ref[...]

    blk = pl.BlockSpec((br, C), lambda i: (i, 0))
    return pl.pallas_call(
        kern, name=name, grid=(R // br,), in_specs=[blk, blk], out_specs=blk,
        out_shape=jax.ShapeDtypeStruct((R, C), F32), compiler_params=_params(("parallel",)),
    )(a2, b2).reshape(shape)


def _sum4(buf, *, name):
    _, R, C = buf.shape
    br = _pick(R, 512, SUBLANE)

    def kern(b_ref, o_ref):
        o_ref[...] = ((b_ref[0] + b_ref[1]) + b_ref[2]) + b_ref[3]

    return pl.pallas_call(
        kern, name=name, grid=(R // br,), in_specs=[pl.BlockSpec((4, br, C), lambda i: (0, i, 0))],
        out_specs=pl.BlockSpec((br, C), lambda i: (i, 0)),
        out_shape=jax.ShapeDtypeStruct((R, C), F32), compiler_params=_params(("parallel",)),
    )(buf)


def _place():
    x, y, c = lax.axis_index("x"), lax.axis_index("y"), lax.axis_index("c")
    other_chips = [(1 - x, y), (x, 1 - y), (1 - x, 1 - y)]
    return x, y, c, other_chips


def _gather_chips(w, *, name):
    R, C = w.shape
    H = R // 2

    def body(w_ref, out_ref, send_sems, recv_sems, local_sem):
        x, y, c, chips = _place()
        me_chip = 2 * x + y
        sib = (x, y, 1 - c)

        def rows(chip, hc):
            return out_ref.at[chip, pl.ds(hc * H, H), :]

        def copy(k, blk, to, src=None):
            return pltpu.make_async_remote_copy(
                src_ref=blk if src is None else src, dst_ref=blk, send_sem=send_sems.at[k], recv_sem=recv_sems.at[k],
                device_id=to, device_id_type=MESH)

        mine = pltpu.make_async_copy(w_ref, out_ref.at[me_chip], local_sem)
        mine.start()
        first = [copy(j, rows(me_chip, c), (cx, cy, c), src=w_ref.at[pl.ds(c * H, H), :])
                 for j, (cx, cy) in enumerate(chips)]
        for cp in first:
            cp.start()
        passed = []
        for j, (cx, cy) in enumerate(chips):
            blk = rows(2 * cx + cy, c)
            copy(j, blk, (cx, cy, c)).wait_recv()
            fw = copy(3 + j, blk, sib)
            fw.start()
            passed.append(fw)
        for j, (cx, cy) in enumerate(chips):
            copy(3 + j, rows(2 * cx + cy, 1 - c), sib).wait_recv()
        for cp in first + passed:
            cp.wait_send()
        mine.wait()

    return pl.pallas_call(
        body, name=name, in_specs=[_ANY], out_specs=_ANY,
        out_shape=jax.ShapeDtypeStruct((4, R, C), w.dtype),
        scratch_shapes=[pltpu.SemaphoreType.DMA((6,)), pltpu.SemaphoreType.DMA((6,)), pltpu.SemaphoreType.DMA],
    )(w)


def _pair_swap(v, *, name):
    def body(v_ref, out_ref, send_sem, recv_sem):
        x, y, c, _ = _place()
        cp = pltpu.make_async_remote_copy(src_ref=v_ref, dst_ref=out_ref, send_sem=send_sem, recv_sem=recv_sem,
                                          device_id=(x, y, 1 - c), device_id_type=MESH)
        cp.start()
        cp.wait()

    return pl.pallas_call(
        body, name=name, in_specs=[_ANY], out_specs=_ANY, out_shape=jax.ShapeDtypeStruct(v.shape, v.dtype),
        scratch_shapes=[pltpu.SemaphoreType.DMA, pltpu.SemaphoreType.DMA],
    )(v)


def _chip_exchange(pv, *, name):
    def body(p_ref, out_ref, send_sems, recv_sems, local_sem):
        x, y, c, chips = _place()
        me_chip = 2 * x + y
        mine = pltpu.make_async_copy(p_ref.at[me_chip], out_ref.at[me_chip], local_sem)
        mine.start()
        sends = []
        for j, (cx, cy) in enumerate(chips):
            cp = pltpu.make_async_remote_copy(
                src_ref=p_ref.at[2 * cx + cy], dst_ref=out_ref.at[me_chip], send_sem=send_sems.at[j],
                recv_sem=recv_sems.at[j], device_id=(cx, cy, c), device_id_type=MESH)
            cp.start()
            sends.append(cp)
        for j, (cx, cy) in enumerate(chips):
            blk = out_ref.at[2 * cx + cy]
            pltpu.make_async_remote_copy(src_ref=blk, dst_ref=blk, send_sem=send_sems.at[j], recv_sem=recv_sems.at[j],
                                         device_id=(cx, cy, c), device_id_type=MESH).wait_recv()
        for cp in sends:
            cp.wait_send()
        mine.wait()

    return pl.pallas_call(
        body, name=name, in_specs=[_ANY], out_specs=_ANY, out_shape=jax.ShapeDtypeStruct(pv.shape, pv.dtype),
        scratch_shapes=[pltpu.SemaphoreType.DMA((3,)), pltpu.SemaphoreType.DMA((3,)), pltpu.SemaphoreType.DMA],
    )(pv)


def _pair_gather(f, *, name):
    H, C = f.shape

    def body(f_ref, out_ref, send_sem, recv_sem, local_sem):
        x, y, c, _ = _place()
        mine_rows = out_ref.at[pl.ds(c * H, H), :]
        mine = pltpu.make_async_copy(f_ref, mine_rows, local_sem)
        mine.start()
        cp = pltpu.make_async_remote_copy(src_ref=f_ref, dst_ref=mine_rows, send_sem=send_sem, recv_sem=recv_sem,
                                          device_id=(x, y, 1 - c), device_id_type=MESH)
        cp.start()
        theirs = out_ref.at[pl.ds((1 - c) * H, H), :]
        pltpu.make_async_remote_copy(src_ref=theirs, dst_ref=theirs, send_sem=send_sem, recv_sem=recv_sem,
                                     device_id=(x, y, 1 - c), device_id_type=MESH).wait_recv()
        cp.wait_send()
        mine.wait()

    return pl.pallas_call(
        body, name=name, in_specs=[_ANY], out_specs=_ANY, out_shape=jax.ShapeDtypeStruct((2 * H, C), f.dtype),
        scratch_shapes=[pltpu.SemaphoreType.DMA, pltpu.SemaphoreType.DMA, pltpu.SemaphoreType.DMA],
    )(f)


WEIGHTS = [
    ("attn_w_in", 2), ("attn_b_f", None), ("attn_w_out", 1), ("ssm_w_in", 2), ("ssm_conv_w", 2), ("ssm_conv_b", 1),
    ("ssm_dt_bias", None), ("ssm_A_log", None), ("ssm_D", None), ("ssm_norm_w", 1), ("ssm_w_out", 1),
    ("ln_mix_g", None), ("ln_mix_b", None), ("ffn_w_up", 2), ("ffn_conv_w", 2), ("ffn_conv_b", None),
    ("ffn_w_down", 1), ("ln_ffn_g", None), ("ln_ffn_b", None), ("ple_w_proj", 2), ("ple_w_gate", 1),
    ("ple_b_gate", None),
]
N_CHIPS = 4
MATMUL_WEIGHTS = ("attn_w_in", "attn_w_out", "ssm_w_in", "ssm_w_out", "ffn_w_up", "ffn_w_down", "ple_w_proj",
                  "ple_w_gate")


def _pack(arrays):
    parts = []
    total = 0
    for a in arrays:
        n = a.size
        pad = (-n) % PACK_COLS
        flat = a.reshape(-1)
        parts.append(jnp.pad(flat, (0, pad)) if pad else flat)
        total += n + pad
    rows = total // PACK_COLS
    rpad = (-rows) % PACK_ROW_ALIGN
    if rpad:
        parts.append(jnp.zeros((rpad * PACK_COLS,), arrays[0].dtype))
    return jnp.concatenate(parts).reshape(rows + rpad, PACK_COLS)


def _unpack(buf, shapes):
    flat = buf.reshape(-1)
    out = []
    off = 0
    for s in shapes:
        n = math.prod(s)
        out.append(flat[off:off + n].reshape(s))
        off += n + ((-n) % PACK_COLS)
    return out


def _col_layout(a):
    T = a.shape[0]
    g = a[:, :SSM_HEADS].reshape(T, SSM_GROUPS, _HPG)
    return jnp.pad(g, ((0, 0), (0, 0), (0, LANE - _HPG))).reshape(T, SSM_GROUPS * LANE)


def _row_layout(a):
    T = a.shape[0]
    return a[:, :SSM_HEADS].T.reshape(SSM_GROUPS, _HPG, T)


def _from_col_layout(a):
    T = a.shape[0]
    v = a.reshape(T, SSM_GROUPS, LANE)[:, :, :_HPG].reshape(T, SSM_HEADS)
    return jnp.pad(v, ((0, 0), (0, LANE - SSM_HEADS)))


def _from_row_layout(a):
    T = a.shape[-1]
    v = a.reshape(SSM_HEADS, T).T
    return jnp.pad(v, ((0, 0), (0, LANE - SSM_HEADS)))


def _pad_lanes(v, n=LANE):
    return jnp.pad(v, (0, n - v.shape[0])).reshape(1, n)


def _local_step(x, p, target, W):
    T = x.shape[0]
    row = lambda v: v.reshape(1, -1)
    attn_in = jnp.pad(W["attn_w_in"][0], ((0, 0), (0, ATTN_IN_PAD - W["attn_w_in"].shape[2])))
    ssm_in = jnp.pad(W["ssm_w_in"][0], ((0, 0), (0, SSM_IN_PAD - W["ssm_w_in"].shape[2])))
    bf = _pad_lanes(W["attn_b_f"][0])
    dt_bias = _pad_lanes(W["ssm_dt_bias"][0])
    a_log = _pad_lanes(W["ssm_A_log"][0])
    d_exp = jnp.repeat(W["ssm_D"][0], _PH).reshape(1, SSM_D_INNER)
    norm_w = row(W["ssm_norm_w"][0])
    G = {}

    def ffn_ple_fwd(i, xin, mix, tag):
        s = {}
        s["z1"], s["h1"], s["h1b"] = _ln_fwd(xin, mix, row(W["ln_mix_g"][i]), row(W["ln_mix_b"][i]),
                                             name=f"ln_mix_fwd{tag}")
        s["up"] = _mm(s["h1b"], W["ffn_w_up"][i], name=f"ffn_up{tag}")
        s["a"] = _ffn_act_fwd(s["up"], W["ffn_conv_w"][i], row(W["ffn_conv_b"][i]), name=f"ffn_act_fwd{tag}")
        ffn = _mm(s["a"], W["ffn_w_down"][i], name=f"ffn_down{tag}")
        s["z2"], s["h2"], s["h2b"] = _ln_fwd(s["h1"], ffn, row(W["ln_ffn_g"][i]), row(W["ln_ffn_b"][i]),
                                             name=f"ln_ffn_fwd{tag}")
        s["G"] = _mm(s["h2b"], W["ple_w_gate"][i], name=f"ple_gate_mm{tag}")
        s["pp"] = _mm(pb[i], W["ple_w_proj"][i], name=f"ple_proj_mm{tag}")
        out, outb = _ple_fwd(s["h2"], s["G"], row(W["ple_b_gate"][i]), s["pp"], name=f"ple_fwd{tag}")
        return out, outb, s

    def ffn_ple_bwd(i, dx, s, tag):
        g = {}
        dG, dpp, g["ple_b_gate"] = _ple_bwd(dx, s["G"], row(W["ple_b_gate"][i]), s["pp"], name=f"ple_bwd{tag}")
        g["ple_w_gate"] = _mm(s["h2b"], dG, ta=True, name=f"ple_gate_dw{tag}")
        g["ple_w_proj"] = _mm(pb[i], dpp, ta=True, name=f"ple_proj_dw{tag}")
        dh2 = _mm(dG, W["ple_w_gate"][i], tb=True, add=dx, name=f"ple_gate_dx{tag}")
        dz2, dz2b, g["ln_ffn_g"], g["ln_ffn_b"] = _ln_bwd(dh2, s["z2"], row(W["ln_ffn_g"][i]), name=f"ln_ffn_bwd{tag}")
        da = _mm(dz2b, W["ffn_w_down"][i], tb=True, out_dtype=BF16, name=f"ffn_down_dx{tag}")
        g["ffn_w_down"] = _mm(s["a"], dz2b, ta=True, name=f"ffn_down_dw{tag}")
        dup, dgc, g["ffn_conv_b"], g["ffn_conv_w"] = _ffn_act_bwd(
            da, s["up"], W["ffn_conv_w"][i], row(W["ffn_conv_b"][i]), name=f"ffn_act_bwd{tag}")
        dup = _dwconv_bwd_data(dgc, W["ffn_conv_w"][i], FFN_CONV, dup, FFN_DIM, name=f"ffn_conv_bwd{tag}")
        g["ffn_w_up"] = _mm(s["h1b"], dup, ta=True, name=f"ffn_up_dw{tag}")
        dh1 = _mm(dup, W["ffn_w_up"][i], tb=True, add=dz2, add_scale=DEEPNORM_ALPHA, name=f"ffn_up_dx{tag}")
        dz1, dz1b, g["ln_mix_g"], g["ln_mix_b"] = _ln_bwd(dh1, s["z1"], row(W["ln_mix_g"][i]), name=f"ln_mix_bwd{tag}")
        return dz1, dz1b, g

    xb = x.astype(BF16)
    pb = p.astype(BF16)
    proj0 = _mm(xb, attn_in, name="attn_in")
    c_col = _fox_gate_fwd(proj0, bf, name="fox_gate_fwd")
    cT = (c_col[:, :ATTN_HEADS] * LOG2E).T.reshape(HEAD_PAIRS, 2, T)
    qa, qb, kk, ka, kb, vv, va, vb = _attn_prep(proj0, name="attn_prep")
    o, ob, lsea, lseb = _attn_fwd(qa, qb, kk, va, vb, cT, name="attn_fwd")
    mix0 = _mm(ob, W["attn_w_out"][0], name="attn_out")
    x1, x1b, s0 = ffn_ple_fwd(0, x, mix0, "0")

    proj1 = _mm(x1b, ssm_in, name="ssm_in")
    dt, acum = _ssd_dt_fwd(proj1, dt_bias, a_log, name="ssd_dt_fwd")
    xbc = _conv_silu_fwd(proj1, W["ssm_conv_w"][0], row(W["ssm_conv_b"][0]), name="ssd_conv_fwd")
    dtc, acc_, dtr, acr = _col_layout(dt), _col_layout(acum), _row_layout(dt), _row_layout(acum)
    ys, states = _ssd_scan_fwd(xbc, dtc, acc_, dtr, acr, name="ssd_scan_fwd")
    yn = _gate_norm_fwd(ys, xbc, proj1, d_exp, norm_w, name="ssd_gate_norm_fwd")
    mix1 = _mm(yn, W["ssm_w_out"][0], name="ssm_out")
    x2, _, s1 = ffn_ple_fwd(1, x1, mix1, "1")

    lpart, dy = _loss_head(x2, target, name="loss_head")
    loss = jnp.sum(lpart)

    dz1, dz1b, g1 = ffn_ple_bwd(1, dy, s1, "1")
    G["ssm_w_out"] = _mm(yn, dz1b, ta=True, name="ssm_out_dw")[None]
    dyn = _mm(dz1b, W["ssm_w_out"][0], tb=True, name="ssm_out_dx")
    dproj1, dys, dskip, dnw, dde = _gate_norm_bwd(dyn, ys, xbc, proj1, d_exp, norm_w, name="ssd_gate_norm_bwd")
    G["ssm_norm_w"] = dnw
    G["ssm_D"] = dde.reshape(SSM_HEADS, _PH).sum(axis=1)[None]
    dxs, dB, dC, dac, dar, ddc, ddr = _ssd_scan_bwd(xbc, dys, dskip, states, dtc, acc_, dtr, acr, name="ssd_scan_bwd")
    dproj1, dal, ddb = _ssd_dt_bwd(_from_col_layout(dac), _from_row_layout(dar), _from_col_layout(ddc),
                                   _from_row_layout(ddr), dt, proj1, dt_bias, a_log, dproj1, name="ssd_dt_bwd")
    G["ssm_A_log"] = dal[:, :SSM_HEADS]
    G["ssm_dt_bias"] = ddb[:, :SSM_HEADS]
    dxbc = jnp.concatenate([dxs, dB, dC], axis=1)
    dpre, G["ssm_conv_b"], dcw = _conv_silu_bwd(dxbc, proj1, W["ssm_conv_w"][0], row(W["ssm_conv_b"][0]),
                                                name="ssd_conv_bwd")
    G["ssm_conv_w"] = dcw[None]
    dproj1 = _dwconv_bwd_data(dpre, W["ssm_conv_w"][0], SSM_CONV, dproj1, SSM_D_INNER, name="ssd_conv_bwd_data")
    G["ssm_w_in"] = _mm(x1b, dproj1, ta=True, name="ssm_in_dw")[None, :, :W["ssm_w_in"].shape[2]]
    dx1 = _mm(dproj1, ssm_in, tb=True, add=dz1, add_scale=DEEPNORM_ALPHA, name="ssm_in_dx")

    dz0, dz0b, g0 = ffn_ple_bwd(0, dx1, s0, "0")
    G["attn_w_out"] = _mm(ob, dz0b, ta=True, name="attn_out_dw")[None]
    do = _mm(dz0b, W["attn_w_out"][0], tb=True, name="attn_out_dx")
    doa, dob, dlta, dltb = _attn_bwd_prep(do, o, name="attn_bwd_prep")
    dq, dk, dv, dcT, dcq = _attn_bwd(qa, qb, kk, ka, kb, vv, doa, dob, lsea, lseb, dlta, dltb, cT, name="attn_bwd")
    dcq = dcq.reshape(T, HEAD_PAIRS, 2, HALF)[:, :, :, 0].reshape(T, ATTN_HEADS)
    dc_col = jnp.pad(dcT.reshape(ATTN_HEADS, T).T + dcq, ((0, 0), (0, LANE - ATTN_HEADS)))
    dfl, dbf = _fox_gate_bwd(dc_col, proj0, bf, name="fox_gate_bwd")
    G["attn_b_f"] = dbf[:, :ATTN_HEADS]
    dproj0 = jnp.concatenate([dq.astype(BF16), dk.astype(BF16), dv.astype(BF16), dfl.astype(BF16)], axis=1)
    G["attn_w_in"] = _mm(xb, dproj0, ta=True, name="attn_in_dw")[None, :, :W["attn_w_in"].shape[2]]
    grad_x = _mm(dproj0, attn_in, tb=True, add=dz0, add_scale=DEEPNORM_ALPHA, name="attn_in_dx")

    for k in g0:
        G[k] = jnp.stack([g0[k].reshape(W[k].shape[1:]), g1[k].reshape(W[k].shape[1:])])
    return loss, grad_x, G


def kernel(x, p, attn_w_in, attn_b_f, attn_w_out, ssm_w_in, ssm_conv_w, ssm_conv_b, ssm_dt_bias, ssm_A_log, ssm_D, ssm_norm_w, ssm_w_out, ln_mix_g, ln_mix_b, ffn_w_up, ffn_conv_w, ffn_conv_b, ffn_w_down, ln_ffn_g, ln_ffn_b, ple_w_proj, ple_w_gate, ple_b_gate, loss_target, m_attn_w_in, m_attn_b_f, m_attn_w_out, m_ssm_w_in, m_ssm_conv_w, m_ssm_conv_b, m_ssm_dt_bias, m_ssm_A_log, m_ssm_D, m_ssm_norm_w, m_ssm_w_out, m_ln_mix_g, m_ln_mix_b, m_ffn_w_up, m_ffn_conv_w, m_ffn_conv_b, m_ffn_w_down, m_ln_ffn_g, m_ln_ffn_b, m_ple_w_proj, m_ple_w_gate, m_ple_b_gate, v_attn_w_in, v_attn_b_f, v_attn_w_out, v_ssm_w_in, v_ssm_conv_w, v_ssm_conv_b, v_ssm_dt_bias, v_ssm_A_log, v_ssm_D, v_ssm_norm_w, v_ssm_w_out, v_ln_mix_g, v_ln_mix_b, v_ffn_w_up, v_ffn_conv_w, v_ffn_conv_b, v_ffn_w_down, v_ln_ffn_g, v_ln_ffn_b, v_ple_w_proj, v_ple_w_gate, v_ple_b_gate):
    names = [n for n, _ in WEIGHTS]
    axes = dict(WEIGHTS)
    w_loc = dict(zip(names, [attn_w_in, attn_b_f, attn_w_out, ssm_w_in, ssm_conv_w, ssm_conv_b, ssm_dt_bias, ssm_A_log, ssm_D, ssm_norm_w, ssm_w_out, ln_mix_g, ln_mix_b, ffn_w_up, ffn_conv_w, ffn_conv_b, ffn_w_down, ln_ffn_g, ln_ffn_b, ple_w_proj, ple_w_gate, ple_b_gate]))
    m_loc = dict(zip(names, [m_attn_w_in, m_attn_b_f, m_attn_w_out, m_ssm_w_in, m_ssm_conv_w, m_ssm_conv_b, m_ssm_dt_bias, m_ssm_A_log, m_ssm_D, m_ssm_norm_w, m_ssm_w_out, m_ln_mix_g, m_ln_mix_b, m_ffn_w_up, m_ffn_conv_w, m_ffn_conv_b, m_ffn_w_down, m_ln_ffn_g, m_ln_ffn_b, m_ple_w_proj, m_ple_w_gate, m_ple_b_gate]))
    v_loc = dict(zip(names, [v_attn_w_in, v_attn_b_f, v_attn_w_out, v_ssm_w_in, v_ssm_conv_w, v_ssm_conv_b, v_ssm_dt_bias, v_ssm_A_log, v_ssm_D, v_ssm_norm_w, v_ssm_w_out, v_ln_mix_g, v_ln_mix_b, v_ffn_w_up, v_ffn_conv_w, v_ffn_conv_b, v_ffn_w_down, v_ln_ffn_g, v_ln_ffn_b, v_ple_w_proj, v_ple_w_gate, v_ple_b_gate]))
    sharded = [n for n in names if axes[n] is not None]
    matrices = [n for n in sharded if n in MATMUL_WEIGHTS]

    def wire(n):
        if n in matrices:
            return w_loc[n].astype(BF16)
        return lax.bitcast_convert_type(w_loc[n], BF16)

    wired = [wire(n) for n in sharded]
    gathered = _gather_chips(_pack(wired), name="gather_weights")
    W = dict(w_loc)
    per_chip = [_unpack(gathered[k], [w.shape for w in wired]) for k in range(N_CHIPS)]
    for i, n in enumerate(sharded):
        pieces = [per_chip[k][i] for k in range(N_CHIPS)]
        if n not in matrices:
            pieces = [lax.bitcast_convert_type(q, F32) for q in pieces]
        W[n] = jnp.concatenate(pieces, axis=axes[n])

    loss, grad_x, G = _local_step(x[0], p[:, 0], loss_target[0], W)
    loss = lax.psum(loss, ("x", "y", "c"))

    def slot(k):
        parts = []
        for n in names:
            g = G[n].reshape(W[n].shape)
            if axes[n] is not None:
                size = w_loc[n].shape[axes[n]]
                g = lax.slice_in_dim(g, k * size, (k + 1) * size, axis=axes[n])
            parts.append(g)
        return _pack(parts)

    contrib = jnp.stack([slot(k) for k in range(N_CHIPS)])
    R = contrib.shape[1]
    H = R // 2
    c = lax.axis_index("c")
    keep = lax.dynamic_slice_in_dim(contrib, c * H, H, axis=1)
    give = lax.dynamic_slice_in_dim(contrib, (1 - c) * H, H, axis=1)
    pair = _add2(keep, _pair_swap(give, name="grad_pair_swap"), name="grad_pair_sum")
    half = _sum4(_chip_exchange(pair, name="grad_chip_exchange"), name="grad_chip_sum")
    gflat = _pair_gather(half, name="grad_pair_gather")

    shapes = [w_loc[n].shape for n in names]
    delta, new_m, new_v = _adamw(_pack([w_loc[n] for n in names]), gflat, _pack([m_loc[n] for n in names]),
                                 _pack([v_loc[n] for n in names]), name="adamw")
    return (loss, grad_x[None], *_unpack(gflat, shapes), *_unpack(delta, shapes), *_unpack(new_m, shapes),
            *_unpack(new_v, shapes))
```

```python
import functools
import math

import jax
import jax.numpy as jnp
from jax import lax
from jax.experimental import pallas as pl
from jax.experimental.pallas import tpu as pltpu

F32 = jnp.float32
BF16 = jnp.bfloat16
MESH = pl.DeviceIdType.MESH

D_MODEL = 1024
ATTN_HEADS = 16
HEAD_PAIRS = ATTN_HEADS // 2
SSM_D_INNER = 2048
SSM_HEADS = 32
SSM_GROUPS = 8
SSM_STATE = 128
SSM_CONV = 4
SSM_CHUNK = 128
SSM_XBC = SSM_D_INNER + 2 * SSM_GROUPS * SSM_STATE
FFN_DIM = 2816
FFN_CONV = 3
DEPTH = 2
LN_EPS = 1e-5
RMS_EPS = 1e-5
DEEPNORM_ALPHA = (2 * DEPTH) ** 0.25
ADAM_LR = 0.001
ADAM_B1 = 0.9
ADAM_B2 = 0.999
ADAM_EPS = 1e-08
ADAM_WD = 0.01
ADAM_STEP = 10

LANE = 128
SUBLANE = 8
HALO = SUBLANE
NEG = -1e30
ATTN_IN_PAD = 3 * D_MODEL + LANE
SSM_IN_PAD = 2 * SSM_D_INNER + 2 * SSM_GROUPS * SSM_STATE + LANE
PACK_COLS = 1024
PACK_ROW_ALIGN = 512

ATTN_BLOCK = 1024
ROW_BLOCK = 512
NARROW_ROW_BLOCK = 1024
CUM_BLOCK = 256


def _params(sem, vmem_mb=48):
    return pltpu.CompilerParams(dimension_semantics=sem, vmem_limit_bytes=vmem_mb * 2 ** 20)


def _pick(n, target, mult=LANE):
    best = None
    d = mult
    while d <= min(n, target):
        if n % d == 0:
            best = d
        d += mult
    return n if best is None else best


def _sigmoid(x):
    return 1.0 / (1.0 + jnp.exp(-x))


def _log1p(u):
    w = 1.0 + u
    return jnp.where(w == 1.0, u, jnp.log(w) * (u / (w - 1.0)))


def _softplus(x):
    return jnp.maximum(x, 0.0) + _log1p(jnp.exp(-jnp.abs(x)))


def _split3(x):
    hi = x.astype(BF16)
    r1 = x - hi.astype(F32)
    mid = r1.astype(BF16)
    lo = (r1 - mid.astype(F32)).astype(BF16)
    return hi, mid, lo


def _tri_matmul(tri, x):
    out = None
    for part in _split3(x):
        t = jnp.dot(tri, part, preferred_element_type=F32)
        out = t if out is None else out + t
    return out


def _tri(n, lower):
    r = lax.broadcasted_iota(jnp.int32, (n, n), 0)
    c = lax.broadcasted_iota(jnp.int32, (n, n), 1)
    return jnp.where((c <= r) if lower else (c >= r), 1.0, 0.0).astype(BF16)


_ANY = pl.BlockSpec(memory_space=pl.ANY)
MM_OUT_BLOCK_BYTES = 13 * 2 ** 20
MM_IN_BLOCK_BYTES = 4 * 2 ** 20
MM_FULL_K = 3200


def _mm(a, b, *, name, ta=False, tb=False, add=None, add_scale=1.0, out_dtype=F32):
    if ta:
        K, M = a.shape
    else:
        M, K = a.shape
    if tb:
        N, Kb = b.shape
    else:
        Kb, N = b.shape
    assert K == Kb, (a.shape, b.shape, ta, tb)
    if ta:
        assert add is None and out_dtype == F32
        bm = _pick(M, 2816)
        bn = _pick(N, MM_OUT_BLOCK_BYTES // (4 * bm))
        bk = _pick(K, max(512, MM_IN_BLOCK_BYTES // (2 * max(bm, bn))))
    else:
        bm = _pick(M, 1024)
        bn = _pick(N, 1536)
        bk = K if K <= MM_FULL_K else _pick(K, MM_FULL_K)
    nk = K // bk
    a_spec = pl.BlockSpec((bk, bm), lambda i, j, k: (k, i)) if ta else pl.BlockSpec((bm, bk), lambda i, j, k: (i, k))
    b_spec = pl.BlockSpec((bn, bk), lambda i, j, k: (j, k)) if tb else pl.BlockSpec((bk, bn), lambda i, j, k: (k, j))
    o_spec = pl.BlockSpec((bm, bn), lambda i, j, k: (i, j))
    dims = (((0 if ta else 1,), (1 if tb else 0,)), ((), ()))
    has_add = add is not None
    use_acc = nk > 1 and not ta

    def kern(*refs):
        a_ref, b_ref = refs[0], refs[1]
        add_ref = refs[2] if has_add else None
        o_ref = refs[3] if has_add else refs[2]
        k = pl.program_id(2)
        part = lax.dot_general(a_ref[...].astype(BF16), b_ref[...].astype(BF16), dims, preferred_element_type=F32)

        def finish(r):
            return (r + add_scale * add_ref[...] if has_add else r).astype(out_dtype)

        if nk == 1:
            o_ref[...] = finish(part)
        elif not use_acc:
            @pl.when(k == 0)
            def _():
                o_ref[...] = part

            @pl.when(k > 0)
            def _():
                o_ref[...] += part
        else:
            acc_ref = refs[-1]

            @pl.when(k == 0)
            def _():
                acc_ref[...] = part

            @pl.when(jnp.logical_and(k > 0, k < nk - 1))
            def _():
                acc_ref[...] += part

            @pl.when(k == nk - 1)
            def _():
                o_ref[...] = finish(acc_ref[...] + part)

    ins = [a, b] + ([add] if has_add else [])
    in_specs = [a_spec, b_spec] + ([o_spec] if has_add else [])
    return pl.pallas_call(
        kern, name=name, grid=(M // bm, N // bn, nk),
        in_specs=in_specs, out_specs=o_spec,
        out_shape=jax.ShapeDtypeStruct((M, N), out_dtype),
        scratch_shapes=[pltpu.VMEM((bm, bn), F32)] if use_acc else [],
        compiler_params=_params(("parallel", "parallel", "arbitrary"), vmem_mb=56),
    )(*ins)


def _ln_stats(z):
    mu = jnp.mean(z, axis=-1, keepdims=True)
    zc = z - mu
    var = jnp.mean(zc * zc, axis=-1, keepdims=True)
    return zc, lax.rsqrt(var + LN_EPS)


def _ln_fwd(x, r, g, b, *, name):
    T, D = x.shape
    bt = _pick(T, ROW_BLOCK, SUBLANE)

    def kern(x_ref, r_ref, g_ref, b_ref, z_ref, h_ref, hb_ref):
        z = DEEPNORM_ALPHA * x_ref[...] + r_ref[...]
        zc, rstd = _ln_stats(z)
        h = zc * rstd * g_ref[...] + b_ref[...]
        z_ref[...] = z
        h_ref[...] = h
        hb_ref[...] = h.astype(BF16)

    row = pl.BlockSpec((bt, D), lambda i: (i, 0))
    vec = pl.BlockSpec((1, D), lambda i: (0, 0))
    return pl.pallas_call(
        kern, name=name, grid=(T // bt,), in_specs=[row, row, vec, vec], out_specs=[row, row, row],
        out_shape=[jax.ShapeDtypeStruct((T, D), F32)] * 2 + [jax.ShapeDtypeStruct((T, D), BF16)],
        compiler_params=_params(("parallel",)),
    )(x, r, g, b)


def _ln_bwd(dy, z, g, *, name):
    T, D = z.shape
    bt = _pick(T, ROW_BLOCK, SUBLANE)

    def kern(dy_ref, z_ref, g_ref, dz_ref, dzb_ref, dg_ref, db_ref):
        i = pl.program_id(0)
        zc, rstd = _ln_stats(z_ref[...])
        xhat = zc * rstd
        dyv = dy_ref[...]
        dxh = dyv * g_ref[...]
        m1 = jnp.mean(dxh, axis=-1, keepdims=True)
        m2 = jnp.mean(dxh * xhat, axis=-1, keepdims=True)
        dz = rstd * (dxh - m1 - xhat * m2)
        dz_ref[...] = dz
        dzb_ref[...] = dz.astype(BF16)

        @pl.when(i == 0)
        def _():
            dg_ref[...] = jnp.zeros_like(dg_ref)
            db_ref[...] = jnp.zeros_like(db_ref)

        dg_ref[...] += jnp.sum(dyv * xhat, axis=0, keepdims=True)
        db_ref[...] += jnp.sum(dyv, axis=0, keepdims=True)

    row = pl.BlockSpec((bt, D), lambda i: (i, 0))
    vec = pl.BlockSpec((1, D), lambda i: (0, 0))
    return pl.pallas_call(
        kern, name=name, grid=(T // bt,), in_specs=[row, row, vec], out_specs=[row, row, vec, vec],
        out_shape=[jax.ShapeDtypeStruct((T, D), F32), jax.ShapeDtypeStruct((T, D), BF16),
                   jax.ShapeDtypeStruct((1, D), F32), jax.ShapeDtypeStruct((1, D), F32)],
        compiler_params=_params(("arbitrary",)),
    )(dy, z, g)


def _conv_past(ext_ref, cw_ref, K, bt):
    out = None
    for k in range(K):
        term = cw_ref[k:k + 1, :] * ext_ref[pl.ds(HALO - (K - 1) + k, bt), :]
        out = term if out is None else out + term
    return out


def _fill_ext_past(ext_ref, halo_ref, cur, i, bt):
    ext_ref[pl.ds(0, HALO), :] = jnp.where(i > 0, halo_ref[...], 0.0)
    ext_ref[pl.ds(HALO, bt), :] = cur


def _halo_prev(bt, bc, off):
    return pl.BlockSpec((HALO, bc), lambda i, j: (jnp.maximum(i * (bt // HALO) - 1, 0), j + off))


_ERF_P = 0.3275911
_ERF_A = (0.254829592, -0.284496736, 1.421413741, -1.453152027, 1.061405429)


def _cdf_parts(x):
    z = jnp.abs(x) * (1.0 / math.sqrt(2.0))
    t = 1.0 / (1.0 + _ERF_P * z)
    e = jnp.exp(-0.5 * x * x)
    a1, a2, a3, a4, a5 = _ERF_A
    tail = 0.5 * (t * (a1 + t * (a2 + t * (a3 + t * (a4 + t * a5))))) * e
    return jnp.where(x < 0.0, tail, 1.0 - tail), e


def _gelu(x):
    return x * _cdf_parts(x)[0]


def _gelu_and_grad(x):
    cdf, e = _cdf_parts(x)
    return x * cdf, cdf + x * e * (1.0 / math.sqrt(2.0 * math.pi))


def _ffn_act_fwd(up, cw, cb, *, name):
    T, F2 = up.shape
    F = F2 // 2
    bt = _pick(T, ROW_BLOCK, SUBLANE)
    bc = _pick(F, 1408)
    nb = F // bc

    def kern(u_ref, g_ref, halo_ref, cw_ref, cb_ref, a_ref, ext_ref):
        i = pl.program_id(0)
        _fill_ext_past(ext_ref, halo_ref, g_ref[...], i, bt)
        gc = cb_ref[...] + _conv_past(ext_ref, cw_ref, FFN_CONV, bt)
        a_ref[...] = (_gelu(gc) * u_ref[...]).astype(BF16)

    return pl.pallas_call(
        kern, name=name, grid=(T // bt, nb),
        in_specs=[pl.BlockSpec((bt, bc), lambda i, j: (i, j)),
                  pl.BlockSpec((bt, bc), lambda i, j: (i, j + nb)),
                  _halo_prev(bt, bc, nb),
                  pl.BlockSpec((FFN_CONV, bc), lambda i, j: (0, j)),
                  pl.BlockSpec((1, bc), lambda i, j: (0, j))],
        out_specs=pl.BlockSpec((bt, bc), lambda i, j: (i, j)),
        out_shape=jax.ShapeDtypeStruct((T, F), BF16),
        scratch_shapes=[pltpu.VMEM((bt + HALO, bc), F32)],
        compiler_params=_params(("parallel", "parallel")),
    )(up, up, up, cw, cb)


def _ffn_act_bwd(da, up, cw, cb, *, name):
    T, F2 = up.shape
    F = F2 // 2
    bt = _pick(T, ROW_BLOCK, SUBLANE)
    bc = _pick(F, 1408)
    nb = F // bc
    K = FFN_CONV

    def kern(da_ref, u_ref, g_ref, halo_ref, cw_ref, cb_ref, du_ref, dgc_ref, dcb_ref, dcw_ref, ext_ref):
        i = pl.program_id(1)
        _fill_ext_past(ext_ref, halo_ref, g_ref[...], i, bt)
        gc = cb_ref[...] + _conv_past(ext_ref, cw_ref, K, bt)
        dav = da_ref[...]
        act, act_grad = _gelu_and_grad(gc)
        du_ref[...] = (dav * act).astype(BF16)
        dgc = dav * u_ref[...] * act_grad
        dgc_ref[...] = dgc

        @pl.when(i == 0)
        def _():
            dcb_ref[...] = jnp.zeros_like(dcb_ref)
            dcw_ref[...] = jnp.zeros_like(dcw_ref)

        dcb_ref[...] += jnp.sum(dgc, axis=0, keepdims=True)
        for k in range(K):
            dcw_ref[k:k + 1, :] += jnp.sum(dgc * ext_ref[pl.ds(HALO - (K - 1) + k, bt), :], axis=0, keepdims=True)

    blk = pl.BlockSpec((bt, bc), lambda j, i: (i, j))
    return pl.pallas_call(
        kern, name=name, grid=(nb, T // bt),
        in_specs=[blk, blk,
                  pl.BlockSpec((bt, bc), lambda j, i: (i, j + nb)),
                  pl.BlockSpec((HALO, bc), lambda j, i: (jnp.maximum(i * (bt // HALO) - 1, 0), j + nb)),
                  pl.BlockSpec((K, bc), lambda j, i: (0, j)),
                  pl.BlockSpec((1, bc), lambda j, i: (0, j))],
        out_specs=[blk, blk, pl.BlockSpec((1, bc), lambda j, i: (0, j)), pl.BlockSpec((K, bc), lambda j, i: (0, j))],
        out_shape=[jax.ShapeDtypeStruct((T, F2), BF16), jax.ShapeDtypeStruct((T, F), F32),
                   jax.ShapeDtypeStruct((1, F), F32), jax.ShapeDtypeStruct((K, F), F32)],
        scratch_shapes=[pltpu.VMEM((bt + HALO, bc), F32)],
        compiler_params=_params(("parallel", "arbitrary")),
    )(da, up, up, up, cw, cb)


def _dwconv_bwd_data(dgc, cw, K, into, col, *, name):
    T, C = dgc.shape
    bt = _pick(T, ROW_BLOCK, SUBLANE)
    bc = _pick(C, 1408)
    nt = T // bt
    last_halo = T // HALO - 1
    off = col // bc
    assert off * bc == col

    def kern(d_ref, halo_ref, cw_ref, into_ref, o_ref, ext_ref):
        i = pl.program_id(0)
        ext_ref[pl.ds(0, bt), :] = d_ref[...]
        ext_ref[pl.ds(bt, HALO), :] = jnp.where(i < nt - 1, halo_ref[...], 0.0)
        out = None
        for k in range(K):
            term = cw_ref[k:k + 1, :] * ext_ref[pl.ds(K - 1 - k, bt), :]
            out = term if out is None else out + term
        o_ref[...] = out.astype(o_ref.dtype)

    return pl.pallas_call(
        kern, name=name, grid=(nt, C // bc),
        in_specs=[pl.BlockSpec((bt, bc), lambda i, j: (i, j)),
                  pl.BlockSpec((HALO, bc), lambda i, j: (jnp.minimum((i + 1) * (bt // HALO), last_halo), j)),
                  pl.BlockSpec((K, bc), lambda i, j: (0, j)), _ANY],
        out_specs=pl.BlockSpec((bt, bc), lambda i, j: (i, j + off)),
        out_shape=jax.ShapeDtypeStruct(into.shape, into.dtype), input_output_aliases={3: 0},
        scratch_shapes=[pltpu.VMEM((bt + HALO, bc), F32)],
        compiler_params=_params(("parallel", "parallel")),
    )(dgc, dgc, cw, into)


def _ple_fwd(h, G, bg, pp, *, name):
    T, D = h.shape
    bt = _pick(T, ROW_BLOCK, SUBLANE)

    def kern(h_ref, G_ref, bg_ref, pp_ref, o_ref, ob_ref):
        out = h_ref[...] + _sigmoid(G_ref[...] + bg_ref[...]) * pp_ref[...]
        o_ref[...] = out
        ob_ref[...] = out.astype(BF16)

    row = pl.BlockSpec((bt, D), lambda i: (i, 0))
    vec = pl.BlockSpec((1, D), lambda i: (0, 0))
    return pl.pallas_call(
        kern, name=name, grid=(T // bt,), in_specs=[row, row, vec, row], out_specs=[row, row],
        out_shape=[jax.ShapeDtypeStruct((T, D), F32), jax.ShapeDtypeStruct((T, D), BF16)],
        compiler_params=_params(("parallel",)),
    )(h, G, bg, pp)


def _ple_bwd(dx, G, bg, pp, *, name):
    T, D = dx.shape
    bt = _pick(T, ROW_BLOCK, SUBLANE)

    def kern(dx_ref, G_ref, bg_ref, pp_ref, dG_ref, dpp_ref, dbg_ref):
        i = pl.program_id(0)
        gate = _sigmoid(G_ref[...] + bg_ref[...])
        dxv = dx_ref[...]
        dG = dxv * pp_ref[...] * gate * (1.0 - gate)
        dG_ref[...] = dG.astype(BF16)
        dpp_ref[...] = (dxv * gate).astype(BF16)

        @pl.when(i == 0)
        def _():
            dbg_ref[...] = jnp.zeros_like(dbg_ref)

        dbg_ref[...] += jnp.sum(dG, axis=0, keepdims=True)

    row = pl.BlockSpec((bt, D), lambda i: (i, 0))
    vec = pl.BlockSpec((1, D), lambda i: (0, 0))
    return pl.pallas_call(
        kern, name=name, grid=(T // bt,), in_specs=[row, row, vec, row], out_specs=[row, row, vec],
        out_shape=[jax.ShapeDtypeStruct((T, D), BF16), jax.ShapeDtypeStruct((T, D), BF16),
                   jax.ShapeDtypeStruct((1, D), F32)],
        compiler_params=_params(("arbitrary",)),
    )(dx, G, bg, pp)


def _fox_gate_fwd(proj, bf, *, name):
    T = proj.shape[0]
    bt = _pick(T, CUM_BLOCK, SUBLANE)
    fcol = 3 * D_MODEL // LANE

    def kern(f_ref, bf_ref, c_ref, carry_ref):
        i = pl.program_id(0)

        @pl.when(i == 0)
        def _():
            carry_ref[...] = jnp.zeros_like(carry_ref)

        x = f_ref[...] + bf_ref[...]
        lf = jnp.minimum(x, 0.0) - _log1p(jnp.exp(-jnp.abs(x)))
        cs = _tri_matmul(_tri(bt, True), lf) + carry_ref[...]
        c_ref[...] = cs
        carry_ref[...] = cs[bt - 1:bt, :]

    return pl.pallas_call(
        kern, name=name, grid=(T // bt,),
        in_specs=[pl.BlockSpec((bt, LANE), lambda i: (i, fcol)), pl.BlockSpec((1, LANE), lambda i: (0, 0))],
        out_specs=pl.BlockSpec((bt, LANE), lambda i: (i, 0)),
        out_shape=jax.ShapeDtypeStruct((T, LANE), F32),
        scratch_shapes=[pltpu.VMEM((1, LANE), F32)],
        compiler_params=_params(("arbitrary",)),
    )(proj, bf)


def _fox_gate_bwd(dc, proj, bf, *, name):
    T = proj.shape[0]
    bt = _pick(T, CUM_BLOCK, SUBLANE)
    nb = T // bt
    fcol = 3 * D_MODEL // LANE

    def kern(dc_ref, f_ref, bf_ref, df_ref, dbf_ref, carry_ref):
        i = pl.program_id(0)

        @pl.when(i == 0)
        def _():
            carry_ref[...] = jnp.zeros_like(carry_ref)
            dbf_ref[...] = jnp.zeros_like(dbf_ref)

        dlf = _tri_matmul(_tri(bt, False), dc_ref[...]) + carry_ref[...]
        carry_ref[...] = dlf[0:1, :]
        x = f_ref[...] + bf_ref[...]
        lane = lax.broadcasted_iota(jnp.int32, (bt, LANE), 1)
        df = jnp.where(lane < ATTN_HEADS, dlf / (1.0 + jnp.exp(x)), 0.0)
        df_ref[...] = df
        dbf_ref[...] += jnp.sum(df, axis=0, keepdims=True)

    return pl.pallas_call(
        kern, name=name, grid=(nb,),
        in_specs=[pl.BlockSpec((bt, LANE), lambda i: (nb - 1 - i, 0)),
                  pl.BlockSpec((bt, LANE), lambda i: (nb - 1 - i, fcol)),
                  pl.BlockSpec((1, LANE), lambda i: (0, 0))],
        out_specs=[pl.BlockSpec((bt, LANE), lambda i: (nb - 1 - i, 0)), pl.BlockSpec((1, LANE), lambda i: (0, 0))],
        out_shape=[jax.ShapeDtypeStruct((T, LANE), F32), jax.ShapeDtypeStruct((1, LANE), F32)],
        scratch_shapes=[pltpu.VMEM((1, LANE), F32)],
        compiler_params=_params(("arbitrary",)),
    )(dc, proj, bf)


_NT = (((1,), (1,)), ((), ()))
_TN = (((0,), (0,)), ((), ()))


def _dot(a, b, dims=None):
    if dims is None:
        return jnp.dot(a, b, preferred_element_type=F32)
    return lax.dot_general(a, b, dims, preferred_element_type=F32)


LOG2E = 1.0 / math.log(2.0)
LN2 = math.log(2.0)
Q_SCALE = 0.125 * LOG2E
HALF = LANE // 2
L_LANE = (HALF, 0)


def _attn_prep(proj, *, name):
    T = proj.shape[0]
    bt = _pick(T, ATTN_BLOCK)

    def kern(q_ref, k_ref, v_ref, qa_ref, qb_ref, kk_ref, ka_ref, kb_ref, vv_ref, va_ref, vb_ref):
        lane = lax.broadcasted_iota(jnp.int32, (bt, LANE), 1)
        lo = lane < HALF
        q = q_ref[...] * Q_SCALE
        k = k_ref[...]
        v = v_ref[...]
        qa_ref[...] = jnp.where(lo, q, 0.0).astype(BF16)
        qb_ref[...] = jnp.where(lo, 0.0, q).astype(BF16)
        kk_ref[...] = k.astype(BF16)
        ka_ref[...] = jnp.where(lo, k, 0.0).astype(BF16)
        kb_ref[...] = jnp.where(lo, 0.0, k).astype(BF16)
        vv_ref[...] = v.astype(BF16)
        va_ref[...] = jnp.where(lo, v, jnp.where(lane == L_LANE[0], 1.0, 0.0)).astype(BF16)
        vb_ref[...] = jnp.where(lo, jnp.where(lane == L_LANE[1], 1.0, 0.0), v).astype(BF16)

    kcol, vcol = D_MODEL // LANE, 2 * D_MODEL // LANE
    out = pl.BlockSpec((bt, LANE), lambda i, hp: (i, hp))
    return pl.pallas_call(
        kern, name=name, grid=(T // bt, HEAD_PAIRS),
        in_specs=[out, pl.BlockSpec((bt, LANE), lambda i, hp: (i, kcol + hp)),
                  pl.BlockSpec((bt, LANE), lambda i, hp: (i, vcol + hp))],
        out_specs=[out] * 8, out_shape=[jax.ShapeDtypeStruct((T, D_MODEL), BF16)] * 8,
        compiler_params=_params(("parallel", "parallel")),
    )(proj, proj, proj)


def _attn_fwd(qa, qb, kk, va, vb, cT, *, name):
    T = qa.shape[0]
    tb = _pick(T, ATTN_BLOCK)
    nq = T // tb
    rep = tb // LANE

    def kern(qa_ref, qb_ref, k_ref, va_ref, vb_ref, c_ref, o_ref, ob_ref, lsea_ref, lseb_ref, m_ref, acc_ref):
        qi = pl.program_id(1)
        ki = pl.program_id(2)

        @pl.when(ki == 0)
        def _():
            m_ref[...] = jnp.full_like(m_ref, NEG)
            acc_ref[...] = jnp.zeros_like(acc_ref)

        def step(diag):
            k = k_ref[...]
            for h, (q_ref, v_ref) in enumerate(((qa_ref, va_ref), (qb_ref, vb_ref))):
                s = _dot(q_ref[...], k, _NT) - c_ref[h:h + 1, :]
                if diag:
                    r = lax.broadcasted_iota(jnp.int32, (tb, tb), 0)
                    c = lax.broadcasted_iota(jnp.int32, (tb, tb), 1)
                    s = jnp.where(c <= r, s, NEG)
                m_prev = m_ref[h]
                m_new = jnp.maximum(m_prev, jnp.max(s, axis=1, keepdims=True))
                p = jnp.exp2(s - jnp.tile(m_new, (1, rep)))
                acc_ref[h] = acc_ref[h] * jnp.exp2(m_prev - m_new) + _dot(p.astype(BF16), v_ref[...])
                m_ref[h] = m_new

        @pl.when(ki < qi)
        def _():
            step(False)

        @pl.when(ki == qi)
        def _():
            step(True)
            lo = lax.broadcasted_iota(jnp.int32, (tb, LANE), 1) < HALF
            a0, a1 = acc_ref[0], acc_ref[1]
            l0 = a0[:, L_LANE[0]:L_LANE[0] + 1]
            l1 = a1[:, L_LANE[1]:L_LANE[1] + 1]
            o = jnp.where(lo, a0 / l0, a1 / l1)
            o_ref[...] = o
            ob_ref[...] = o.astype(BF16)
            lsea_ref[...] = m_ref[0] + jnp.log(l0) * LOG2E
            lseb_ref[...] = m_ref[1] + jnp.log(l1) * LOG2E

    qspec = pl.BlockSpec((tb, LANE), lambda hp, qi, ki: (qi, hp))
    kspec = pl.BlockSpec((tb, LANE), lambda hp, qi, ki: (jnp.minimum(ki, qi), hp))
    return pl.pallas_call(
        kern, name=name, grid=(HEAD_PAIRS, nq, nq),
        in_specs=[qspec, qspec, kspec, kspec, kspec,
                  pl.BlockSpec((None, 2, tb), lambda hp, qi, ki: (hp, 0, jnp.minimum(ki, qi)))],
        out_specs=[qspec, qspec, qspec, qspec],
        out_shape=[jax.ShapeDtypeStruct((T, D_MODEL), F32), jax.ShapeDtypeStruct((T, D_MODEL), BF16),
                   jax.ShapeDtypeStruct((T, D_MODEL), F32), jax.ShapeDtypeStruct((T, D_MODEL), F32)],
        scratch_shapes=[pltpu.VMEM((2, tb, LANE), F32), pltpu.VMEM((2, tb, LANE), F32)],
        compiler_params=_params(("parallel", "parallel", "arbitrary")),
    )(qa, qb, kk, va, vb, cT)


def _attn_bwd_prep(do, o, *, name):
    T, D = do.shape
    bt = _pick(T, ATTN_BLOCK)

    def kern(do_ref, o_ref, doa_ref, dob_ref, dlta_ref, dltb_ref):
        lo = lax.broadcasted_iota(jnp.int32, (bt, LANE), 1) < HALF
        dov = do_ref[...]
        prod = dov * o_ref[...]
        doa_ref[...] = jnp.where(lo, dov, 0.0).astype(BF16)
        dob_ref[...] = jnp.where(lo, 0.0, dov).astype(BF16)
        dlta_ref[...] = jnp.broadcast_to(jnp.sum(jnp.where(lo, prod, 0.0), axis=1, keepdims=True), (bt, LANE))
        dltb_ref[...] = jnp.broadcast_to(jnp.sum(jnp.where(lo, 0.0, prod), axis=1, keepdims=True), (bt, LANE))

    blk = pl.BlockSpec((bt, LANE), lambda i, hp: (i, hp))
    return pl.pallas_call(
        kern, name=name, grid=(T // bt, HEAD_PAIRS), in_specs=[blk, blk], out_specs=[blk] * 4,
        out_shape=[jax.ShapeDtypeStruct((T, D), BF16)] * 2 + [jax.ShapeDtypeStruct((T, D), F32)] * 2,
        compiler_params=_params(("parallel", "parallel")),
    )(do, o)


def _attn_bwd(qa, qb, kk, ka, kb, vv, doa, dob, lsea, lseb, dlta, dltb, cT, *, name):
    T = qa.shape[0]
    tb = _pick(T, ATTN_BLOCK)
    nq = T // tb
    rep = tb // LANE

    def kern(qa_ref, qb_ref, k_ref, ka_ref, kb_ref, v_ref, doa_ref, dob_ref, lsea_ref, lseb_ref, dlta_ref, dltb_ref,
             c_ref, dq_ref, dk_ref, dv_ref, dc_ref, dcq_ref):
        ki = pl.program_id(1)
        qi = pl.program_id(2)

        @pl.when(jnp.logical_and(ki == 0, qi == 0))
        def _():
            dq_ref[...] = jnp.zeros_like(dq_ref)
            dcq_ref[...] = jnp.zeros_like(dcq_ref)

        @pl.when(qi == 0)
        def _():
            dk_ref[...] = jnp.zeros_like(dk_ref)
            dv_ref[...] = jnp.zeros_like(dv_ref)
            dc_ref[...] = jnp.zeros_like(dc_ref)

        def step(diag):
            k = k_ref[...]
            v = v_ref[...]
            dq = None
            dk = None
            dv = None
            row_sums = []
            heads = ((qa_ref, ka_ref, doa_ref, lsea_ref, dlta_ref), (qb_ref, kb_ref, dob_ref, lseb_ref, dltb_ref))
            for h, (q_ref, km_ref, do_ref, lse_ref, dlt_ref) in enumerate(heads):
                q = q_ref[...]
                dom = do_ref[...]
                s = _dot(q, k, _NT) - c_ref[h:h + 1, :]
                if diag:
                    r = lax.broadcasted_iota(jnp.int32, (tb, tb), 0)
                    c = lax.broadcasted_iota(jnp.int32, (tb, tb), 1)
                    s = jnp.where(c <= r, s, NEG)
                p = jnp.exp2(s - jnp.tile(lse_ref[...], (1, rep)))
                ds = p * (_dot(dom, v, _NT) - jnp.tile(dlt_ref[...], (1, rep)))
                dc_ref[h:h + 1, :] -= jnp.sum(ds, axis=0, keepdims=True)
                row_sums.append(jnp.sum(ds, axis=1, keepdims=True))
                dsb = ds.astype(BF16)
                tv = _dot(p.astype(BF16), dom, _TN)
                tk = _dot(dsb, q, _TN)
                tq = _dot(dsb, km_ref[...])
                dv = tv if dv is None else dv + tv
                dk = tk if dk is None else dk + tk
                dq = tq if dq is None else dq + tq
            dv_ref[...] += dv
            dk_ref[...] += dk * LN2
            rows = pl.ds(pl.multiple_of(qi * tb, tb), tb)
            dq_ref[rows, :] += dq * 0.125
            lo = lax.broadcasted_iota(jnp.int32, (tb, LANE), 1) < HALF
            dcq_ref[rows, :] += jnp.where(lo, row_sums[0], row_sums[1])

        @pl.when(qi > ki)
        def _():
            step(False)

        @pl.when(qi == ki)
        def _():
            step(True)

    qspec = pl.BlockSpec((tb, LANE), lambda hp, ki, qi: (jnp.maximum(qi, ki), hp))
    kspec = pl.BlockSpec((tb, LANE), lambda hp, ki, qi: (ki, hp))
    cspec = pl.BlockSpec((None, 2, tb), lambda hp, ki, qi: (hp, 0, ki))
    qacc = pl.BlockSpec((T, LANE), lambda hp, ki, qi: (0, hp), pipeline_mode=pl.Buffered(1))
    return pl.pallas_call(
        kern, name=name, grid=(HEAD_PAIRS, nq, nq),
        in_specs=[qspec, qspec, kspec, kspec, kspec, kspec, qspec, qspec, qspec, qspec, qspec, qspec, cspec],
        out_specs=[qacc, kspec, kspec, cspec, qacc],
        out_shape=[jax.ShapeDtypeStruct((T, D_MODEL), F32)] * 3 + [jax.ShapeDtypeStruct((HEAD_PAIRS, 2, T), F32),
                                                                   jax.ShapeDtypeStruct((T, D_MODEL), F32)],
        compiler_params=_params(("parallel", "arbitrary", "arbitrary"), vmem_mb=56),
    )(qa, qb, kk, ka, kb, vv, doa, dob, lsea, lseb, dlta, dltb, cT)


def _ssd_dt_fwd(proj, dt_bias, a_log, *, name):
    T = proj.shape[0]
    Q = SSM_CHUNK
    col = (2 * SSM_D_INNER + 2 * SSM_GROUPS * SSM_STATE) // LANE

    def kern(raw_ref, b_ref, al_ref, dt_ref, ac_ref):
        dt = _softplus(raw_ref[...] + b_ref[...])
        dt_ref[...] = dt
        ac_ref[...] = _tri_matmul(_tri(Q, True), dt * (-jnp.exp(al_ref[...])))

    vec = pl.BlockSpec((1, LANE), lambda i: (0, 0))
    blk = pl.BlockSpec((Q, LANE), lambda i: (i, 0))
    return pl.pallas_call(
        kern, name=name, grid=(T // Q,),
        in_specs=[pl.BlockSpec((Q, LANE), lambda i: (i, col)), vec, vec], out_specs=[blk, blk],
        out_shape=[jax.ShapeDtypeStruct((T, LANE), F32)] * 2,
        compiler_params=_params(("parallel",)),
    )(proj, dt_bias, a_log)


def _ssd_dt_bwd(da_a, da_b, ddt_a, ddt_b, dt, proj, dt_bias, a_log, into, *, name):
    T = proj.shape[0]
    Q = SSM_CHUNK
    col = (2 * SSM_D_INNER + 2 * SSM_GROUPS * SSM_STATE) // LANE

    def kern(daa_ref, dab_ref, dda_ref, ddb_ref, dt_ref, raw_ref, b_ref, al_ref, into_ref, draw_ref, dal_ref, db_ref,
             acc_ref):
        i = pl.program_id(0)

        @pl.when(i == 0)
        def _():
            acc_ref[...] = jnp.zeros_like(acc_ref)
            db_ref[...] = jnp.zeros_like(db_ref)

        A = -jnp.exp(al_ref[...])
        ddA = _tri_matmul(_tri(Q, False), daa_ref[...] + dab_ref[...])
        ddt = dda_ref[...] + ddb_ref[...] + ddA * A
        acc_ref[...] += jnp.sum(ddA * dt_ref[...], axis=0, keepdims=True)
        lane = lax.broadcasted_iota(jnp.int32, (Q, LANE), 1)
        draw = jnp.where(lane < SSM_HEADS, ddt * _sigmoid(raw_ref[...] + b_ref[...]), 0.0)
        draw_ref[...] = draw.astype(BF16)
        db_ref[...] += jnp.sum(draw, axis=0, keepdims=True)
        dal_ref[...] = acc_ref[...] * A

    vec = pl.BlockSpec((1, LANE), lambda i: (0, 0))
    blk = pl.BlockSpec((Q, LANE), lambda i: (i, 0))
    return pl.pallas_call(
        kern, name=name, grid=(T // Q,),
        in_specs=[blk, blk, blk, blk, blk, pl.BlockSpec((Q, LANE), lambda i: (i, col)), vec, vec, _ANY],
        out_specs=[pl.BlockSpec((Q, LANE), lambda i: (i, col)), vec, vec],
        out_shape=[jax.ShapeDtypeStruct(into.shape, into.dtype), jax.ShapeDtypeStruct((1, LANE), F32),
                   jax.ShapeDtypeStruct((1, LANE), F32)],
        input_output_aliases={8: 0},
        scratch_shapes=[pltpu.VMEM((1, LANE), F32)],
        compiler_params=_params(("arbitrary",)),
    )(da_a, da_b, ddt_a, ddt_b, dt, proj, dt_bias, a_log, into)


def _conv_silu_fwd(proj, cw, cb, *, name):
    T = proj.shape[0]
    C = SSM_XBC
    bt = _pick(T, ROW_BLOCK, SUBLANE)
    bc = 1024
    off = SSM_D_INNER // bc

    def kern(x_ref, halo_ref, cw_ref, cb_ref, o_ref, ext_ref):
        i = pl.program_id(0)
        _fill_ext_past(ext_ref, halo_ref, x_ref[...], i, bt)
        pre = cb_ref[...] + _conv_past(ext_ref, cw_ref, SSM_CONV, bt)
        o_ref[...] = pre * _sigmoid(pre)

    return pl.pallas_call(
        kern, name=name, grid=(T // bt, C // bc),
        in_specs=[pl.BlockSpec((bt, bc), lambda i, j: (i, j + off)), _halo_prev(bt, bc, off),
                  pl.BlockSpec((SSM_CONV, bc), lambda i, j: (0, j)), pl.BlockSpec((1, bc), lambda i, j: (0, j))],
        out_specs=pl.BlockSpec((bt, bc), lambda i, j: (i, j)),
        out_shape=jax.ShapeDtypeStruct((T, C), F32),
        scratch_shapes=[pltpu.VMEM((bt + HALO, bc), F32)],
        compiler_params=_params(("parallel", "parallel")),
    )(proj, proj, cw, cb)


def _conv_silu_bwd(dxbc, proj, cw, cb, *, name):
    T = proj.shape[0]
    C = SSM_XBC
    K = SSM_CONV
    bt = _pick(T, ROW_BLOCK, SUBLANE)
    bc = 1024
    off = SSM_D_INNER // bc

    def kern(d_ref, x_ref, halo_ref, cw_ref, cb_ref, dpre_ref, dcb_ref, dcw_ref, ext_ref):
        i = pl.program_id(1)
        _fill_ext_past(ext_ref, halo_ref, x_ref[...], i, bt)
        pre = cb_ref[...] + _conv_past(ext_ref, cw_ref, K, bt)
        sg = _sigmoid(pre)
        dpre = d_ref[...] * sg * (1.0 + pre * (1.0 - sg))
        dpre_ref[...] = dpre

        @pl.when(i == 0)
        def _():
            dcb_ref[...] = jnp.zeros_like(dcb_ref)
            dcw_ref[...] = jnp.zeros_like(dcw_ref)

        dcb_ref[...] += jnp.sum(dpre, axis=0, keepdims=True)
        for k in range(K):
            dcw_ref[k:k + 1, :] += jnp.sum(dpre * ext_ref[pl.ds(HALO - (K - 1) + k, bt), :], axis=0, keepdims=True)

    blk = pl.BlockSpec((bt, bc), lambda j, i: (i, j))
    return pl.pallas_call(
        kern, name=name, grid=(C // bc, T // bt),
        in_specs=[blk, pl.BlockSpec((bt, bc), lambda j, i: (i, j + off)),
                  pl.BlockSpec((HALO, bc), lambda j, i: (jnp.maximum(i * (bt // HALO) - 1, 0), j + off)),
                  pl.BlockSpec((K, bc), lambda j, i: (0, j)), pl.BlockSpec((1, bc), lambda j, i: (0, j))],
        out_specs=[blk, pl.BlockSpec((1, bc), lambda j, i: (0, j)), pl.BlockSpec((K, bc), lambda j, i: (0, j))],
        out_shape=[jax.ShapeDtypeStruct((T, C), F32), jax.ShapeDtypeStruct((1, C), F32),
                   jax.ShapeDtypeStruct((K, C), F32)],
        scratch_shapes=[pltpu.VMEM((bt + HALO, bc), F32)],
        compiler_params=_params(("parallel", "arbitrary")),
    )(dxbc, proj, proj, cw, cb)


_GP = SSM_D_INNER // SSM_GROUPS
_HPG = SSM_HEADS // SSM_GROUPS
_PH = SSM_D_INNER // SSM_HEADS


def _head_masks(rows):
    lane = lax.broadcasted_iota(jnp.int32, (rows, _GP), 1)
    return [jnp.logical_and(lane >= r * _PH, lane < (r + 1) * _PH) for r in range(_HPG)]


def _ssd_cols(g):
    x0 = g * _GP
    b0 = SSM_D_INNER + g * SSM_STATE
    c0 = SSM_D_INNER + (SSM_GROUPS + g) * SSM_STATE
    return slice(x0, x0 + _GP), slice(b0, b0 + SSM_STATE), slice(c0, c0 + SSM_STATE)


def _ssd_specs(idx):
    Q, N = SSM_CHUNK, SSM_STATE
    return dict(
        xbc=pl.BlockSpec((Q, SSM_XBC), lambda j: (idx(j), 0)),
        x=pl.BlockSpec((Q, SSM_D_INNER), lambda j: (idx(j), 0)),
        col=pl.BlockSpec((Q, LANE), lambda j: (idx(j), 0)),
        row=pl.BlockSpec((SSM_HEADS, Q), lambda j: (0, idx(j))),
        st=pl.BlockSpec((N, SSM_D_INNER), lambda j: (idx(j), 0)),
    )


def _ssd_scan_fwd(xbc, dtc, acc_, dtr, acr, *, name):
    T = xbc.shape[0]
    Q, N = SSM_CHUNK, SSM_STATE
    nc = T // Q
    sp = _ssd_specs(lambda j: j)

    def kern(xbc_ref, dtc_ref, ac_ref, dtr_ref, ar_ref, ys_ref, st_ref, state_ref):
        @pl.when(pl.program_id(0) == 0)
        def _():
            state_ref[...] = jnp.zeros_like(state_ref)

        r_i = lax.broadcasted_iota(jnp.int32, (Q, Q), 0)
        c_i = lax.broadcasted_iota(jnp.int32, (Q, Q), 1)
        tri = c_i <= r_i
        masks = _head_masks(Q)
        masks1 = _head_masks(1)
        for g in range(SSM_GROUPS):
            xs, bs, cs = _ssd_cols(g)
            S = state_ref[g]
            st_ref[:, xs] = S
            x = xbc_ref[:, xs]
            xb = x.astype(BF16)
            Bb = xbc_ref[:, bs].astype(BF16)
            Cb = xbc_ref[:, cs].astype(BF16)
            CB = _dot(Cb, Bb, _NT)
            y = jnp.zeros((Q, _GP), F32)
            El = jnp.zeros((Q, _GP), F32)
            Wl = jnp.zeros((Q, _GP), F32)
            decl = jnp.zeros((1, _GP), F32)
            for r in range(_HPG):
                h = g * _HPG + r
                a_c = ac_ref[:, h:h + 1]
                a_r = ar_ref[h:h + 1, :]
                dt_c = dtc_ref[:, h:h + 1]
                dt_r = dtr_ref[h:h + 1, :]
                L = jnp.exp(jnp.where(tri, a_c - a_r, NEG))
                W = CB * L * dt_r
                y = jnp.where(masks[r], _dot(W.astype(BF16), xb), y)
                a_q = a_c[Q - 1:Q, :]
                El = jnp.where(masks[r], jnp.exp(a_c), El)
                Wl = jnp.where(masks[r], jnp.exp(a_q - a_c) * dt_c, Wl)
                decl = jnp.where(masks1[r], jnp.exp(a_q), decl)
            ys_ref[:, xs] = y + _dot(Cb, S.astype(BF16)) * El
            state_ref[g] = S * decl + _dot(Bb, (x * Wl).astype(BF16), _TN)

    return pl.pallas_call(
        kern, name=name, grid=(nc,),
        in_specs=[sp["xbc"], sp["col"], sp["col"], sp["row"], sp["row"]],
        out_specs=[sp["x"], sp["st"]],
        out_shape=[jax.ShapeDtypeStruct((T, SSM_D_INNER), F32), jax.ShapeDtypeStruct((nc * N, SSM_D_INNER), F32)],
        scratch_shapes=[pltpu.VMEM((SSM_GROUPS, N, _GP), F32)],
        compiler_params=_params(("arbitrary",)),
    )(xbc, dtc, acc_, dtr, acr)


def _ssd_scan_bwd(xbc, dys, dskip, st, dtc, acc_, dtr, acr, *, name):
    T = xbc.shape[0]
    Q, N = SSM_CHUNK, SSM_STATE
    nc = T // Q
    sp = _ssd_specs(lambda j: nc - 1 - j)

    def kern(xbc_ref, dy_ref, dsk_ref, st_ref, dtc_ref, ac_ref, dtr_ref, ar_ref,
             dxbc_ref, dac_ref, dar_ref, ddc_ref, ddr_ref, dstate_ref):
        @pl.when(pl.program_id(0) == 0)
        def _():
            dstate_ref[...] = jnp.zeros_like(dstate_ref)

        r_i = lax.broadcasted_iota(jnp.int32, (Q, Q), 0)
        c_i = lax.broadcasted_iota(jnp.int32, (Q, Q), 1)
        tri = c_i <= r_i
        last_row = lax.broadcasted_iota(jnp.int32, (Q, 1), 0) == Q - 1
        lane128 = lax.broadcasted_iota(jnp.int32, (Q, LANE), 1)
        masks = _head_masks(Q)
        masksN = _head_masks(N)
        masks1 = _head_masks(1)
        zeros = jnp.zeros((Q, _GP), F32)
        dacol = jnp.zeros((Q, LANE), F32)
        ddcol = jnp.zeros((Q, LANE), F32)
        for g in range(SSM_GROUPS):
            xs, bs, cs = _ssd_cols(g)
            dS = dstate_ref[g]
            dSb = dS.astype(BF16)
            S = st_ref[:, xs]
            Sb = S.astype(BF16)
            x = xbc_ref[:, xs]
            xb = x.astype(BF16)
            Bb = xbc_ref[:, bs].astype(BF16)
            Cb = xbc_ref[:, cs].astype(BF16)
            dy = dy_ref[:, xs]
            CB = _dot(Cb, Bb, _NT)
            BdS = _dot(Bb, dSb)
            hx = BdS * x
            yd = _dot(Cb, Sb) * dy
            dSS = dS * S
            dxi, El, Wl = zeros, zeros, zeros
            decl = jnp.zeros((1, _GP), F32)
            dBacc = jnp.zeros((Q, N), F32)
            dCacc = jnp.zeros((Q, N), F32)
            for r in range(_HPG):
                h = g * _HPG + r
                hm = masks[r]
                a_c = ac_ref[:, h:h + 1]
                a_r = ar_ref[h:h + 1, :]
                dt_c = dtc_ref[:, h:h + 1]
                dt_r = dtr_ref[h:h + 1, :]
                L = jnp.exp(jnp.where(tri, a_c - a_r, NEG))
                GL = CB * L
                W = GL * dt_r
                dym = jnp.where(hm, dy, 0.0).astype(BF16)
                dW = _dot(dym, xb, _NT)
                E = dW * W
                da_c = jnp.sum(E, axis=1, keepdims=True)
                dar_ref[h:h + 1, :] = -jnp.sum(E, axis=0, keepdims=True)
                ddr_ref[h:h + 1, :] = jnp.sum(dW * GL, axis=0, keepdims=True)
                dGb = (dW * L * dt_r).astype(BF16)
                dCacc = dCacc + _dot(dGb, Bb)
                dBacc = dBacc + _dot(dGb, Cb, _TN)
                dxi = dxi + _dot(W.astype(BF16), dym, _TN)
                a_q = a_c[Q - 1:Q, :]
                e_c = jnp.exp(a_c)
                eq_c = jnp.exp(a_q - a_c)
                w_c = eq_c * dt_c
                ydr = jnp.sum(jnp.where(hm, yd, 0.0), axis=1, keepdims=True) * e_c
                h_c = jnp.sum(jnp.where(hm, hx, 0.0), axis=1, keepdims=True)
                hw = h_c * w_c
                dss = jnp.sum(jnp.sum(jnp.where(masksN[r], dSS, 0.0), axis=1, keepdims=True), axis=0, keepdims=True)
                s_q = jnp.sum(hw, axis=0, keepdims=True) + jnp.exp(a_q) * dss
                da_c = da_c + ydr - hw + jnp.where(last_row, s_q, 0.0)
                dacol = jnp.where(lane128 == h, da_c, dacol)
                ddcol = jnp.where(lane128 == h, h_c * eq_c, ddcol)
                El = jnp.where(hm, e_c, El)
                Wl = jnp.where(hm, w_c, Wl)
                decl = jnp.where(masks1[r], jnp.exp(a_q), decl)
            dxbc_ref[:, xs] = dxi + BdS * Wl + dsk_ref[:, xs]
            dxbc_ref[:, bs] = dBacc + _dot((x * Wl).astype(BF16), dSb, _NT)
            dyE = (dy * El).astype(BF16)
            dxbc_ref[:, cs] = dCacc + _dot(dyE, Sb, _NT)
            dstate_ref[g] = dS * decl + _dot(Cb, dyE, _TN)
        dac_ref[...] = dacol
        ddc_ref[...] = ddcol

    return pl.pallas_call(
        kern, name=name, grid=(nc,),
        in_specs=[sp["xbc"], sp["x"], sp["x"], sp["st"], sp["col"], sp["col"], sp["row"], sp["row"]],
        out_specs=[sp["xbc"], sp["col"], sp["row"], sp["col"], sp["row"]],
        out_shape=[jax.ShapeDtypeStruct((T, SSM_XBC), F32),
                   jax.ShapeDtypeStruct((T, LANE), F32), jax.ShapeDtypeStruct((SSM_HEADS, T), F32),
                   jax.ShapeDtypeStruct((T, LANE), F32), jax.ShapeDtypeStruct((SSM_HEADS, T), F32)],
        scratch_shapes=[pltpu.VMEM((SSM_GROUPS, N, _GP), F32)],
        compiler_params=_params(("arbitrary",)),
    )(xbc, dys, dskip, st, dtc, acc_, dtr, acr)


def _gate_norm_fwd(ys, xbc, proj, d_exp, norm_w, *, name):
    T = ys.shape[0]
    bt = _pick(T, NARROW_ROW_BLOCK, SUBLANE)

    def kern(ys_ref, x_ref, z_ref, d_ref, w_ref, o_ref):
        z = z_ref[...]
        yz = (ys_ref[...] + d_ref[...] * x_ref[...]) * (z * _sigmoid(z))
        rstd = lax.rsqrt(jnp.mean(yz * yz, axis=-1, keepdims=True) + RMS_EPS)
        o_ref[...] = (yz * rstd * w_ref[...]).astype(BF16)

    blk = pl.BlockSpec((bt, _GP), lambda i, g: (i, g))
    vec = pl.BlockSpec((1, _GP), lambda i, g: (0, g))
    return pl.pallas_call(
        kern, name=name, grid=(T // bt, SSM_GROUPS), in_specs=[blk, blk, blk, vec, vec], out_specs=blk,
        out_shape=jax.ShapeDtypeStruct((T, SSM_D_INNER), BF16), compiler_params=_params(("parallel", "parallel")),
    )(ys, xbc, proj, d_exp, norm_w)


def _gate_norm_bwd(dyn, ys, xbc, proj, d_exp, norm_w, *, name):
    T = ys.shape[0]
    bt = _pick(T, NARROW_ROW_BLOCK, SUBLANE)

    def kern(dyn_ref, ys_ref, x_ref, z_ref, d_ref, w_ref, dz_ref, dys_ref, dsk_ref, dw_ref, dd_ref):
        i = pl.program_id(1)
        z = z_ref[...]
        x = x_ref[...]
        sg = _sigmoid(z)
        sz = z * sg
        y = ys_ref[...] + d_ref[...] * x
        yz = y * sz
        rstd = lax.rsqrt(jnp.mean(yz * yz, axis=-1, keepdims=True) + RMS_EPS)
        yhat = yz * rstd
        dynv = dyn_ref[...]
        gg = dynv * w_ref[...]
        dyz = rstd * (gg - yhat * jnp.mean(gg * yhat, axis=-1, keepdims=True))
        dy = dyz * sz
        dz_ref[...] = (dyz * y * sg * (1.0 + z * (1.0 - sg))).astype(BF16)
        dys_ref[...] = dy
        dsk_ref[...] = dy * d_ref[...]

        @pl.when(i == 0)
        def _():
            dw_ref[...] = jnp.zeros_like(dw_ref)
            dd_ref[...] = jnp.zeros_like(dd_ref)

        dw_ref[...] += jnp.sum(dynv * yhat, axis=0, keepdims=True)
        dd_ref[...] += jnp.sum(dy * x, axis=0, keepdims=True)

    blk = pl.BlockSpec((bt, _GP), lambda g, i: (i, g))
    vec = pl.BlockSpec((1, _GP), lambda g, i: (0, g))
    act = jax.ShapeDtypeStruct((T, SSM_D_INNER), F32)
    par = jax.ShapeDtypeStruct((1, SSM_D_INNER), F32)
    return pl.pallas_call(
        kern, name=name, grid=(SSM_GROUPS, T // bt), in_specs=[blk, blk, blk, blk, vec, vec],
        out_specs=[blk, blk, blk, vec, vec],
        out_shape=[jax.ShapeDtypeStruct((T, SSM_IN_PAD), BF16), act, act, par, par],
        compiler_params=_params(("parallel", "arbitrary")),
    )(dyn, ys, xbc, proj, d_exp, norm_w)


def _loss_head(y, target, *, name):
    T, D = y.shape
    bt = _pick(T, ROW_BLOCK, SUBLANE)

    def kern(y_ref, t_ref, l_ref, dy_ref):
        i = pl.program_id(0)
        err = y_ref[...] - t_ref[...]
        dy_ref[...] = err * (1.0 / D)

        @pl.when(i == 0)
        def _():
            l_ref[...] = jnp.zeros_like(l_ref)

        l_ref[...] += jnp.sum(err * err, axis=0, keepdims=True) * (0.5 / D)

    row = pl.BlockSpec((bt, D), lambda i: (i, 0))
    vec = pl.BlockSpec((1, D), lambda i: (0, 0))
    return pl.pallas_call(
        kern, name=name, grid=(T // bt,), in_specs=[row, row], out_specs=[vec, row],
        out_shape=[jax.ShapeDtypeStruct((1, D), F32), jax.ShapeDtypeStruct((T, D), F32)],
        compiler_params=_params(("arbitrary",)),
    )(y, target)


def _adamw(w, g, m, v, *, name):
    R, C = w.shape
    br = _pick(R, 512, SUBLANE)

    def kern(w_ref, g_ref, m_ref, v_ref, d_ref, nm_ref, nv_ref):
        gv = g_ref[...]
        nm = ADAM_B1 * m_ref[...] + (1.0 - ADAM_B1) * gv
        nv = ADAM_B2 * v_ref[...] + (1.0 - ADAM_B2) * (gv * gv)
        m_hat = nm / (1.0 - ADAM_B1 ** ADAM_STEP)
        v_hat = nv / (1.0 - ADAM_B2 ** ADAM_STEP)
        d_ref[...] = -ADAM_LR * (m_hat / (jnp.sqrt(v_hat) + ADAM_EPS) + ADAM_WD * w_ref[...])
        nm_ref[...] = nm
        nv_ref[...] = nv

    blk = pl.BlockSpec((br, C), lambda i: (i, 0))
    return pl.pallas_call(
        kern, name=name, grid=(R // br,), in_specs=[blk] * 4, out_specs=[blk] * 3,
        out_shape=[jax.ShapeDtypeStruct((R, C), F32)] * 3, compiler_params=_params(("parallel",)),
    )(w, g, m, v)


def _add2(a, b, *, name):
    shape = a.shape
    a2, b2 = a.reshape(-1, shape[-1]), b.reshape(-1, shape[-1])
    R, C = a2.shape
    br = _pick(R, 512, SUBLANE)

    def kern(a_ref, b_ref, o_ref):
        o_ref[...] = a_ref[...] + b_ref[...]

    blk = pl.BlockSpec((br, C), lambda i: (i, 0))
    return pl.pallas_call(
        kern, name=name, grid=(R // br,), in_specs=[blk, blk], out_specs=blk,
        out_shape=jax.ShapeDtypeStruct((R, C), F32), compiler_params=_params(("parallel",)),
    )(a2, b2).reshape(shape)


def _sum4(buf, *, name):
    _, R, C = buf.shape
    br = _pick(R, 512, SUBLANE)

    def kern(b_ref, o_ref):
        o_ref[...] = ((b_ref[0] + b_ref[1]) + b_ref[2]) + b_ref[3]

    return pl.pallas_call(
        kern, name=name, grid=(R // br,), in_specs=[pl.BlockSpec((4, br, C), lambda i: (0, i, 0))],
        out_specs=pl.BlockSpec((br, C), lambda i: (i, 0)),
        out_shape=jax.ShapeDtypeStruct((R, C), F32), compiler_params=_params(("parallel",)),
    )(buf)


def _place():
    x, y, c = lax.axis_index("x"), lax.axis_index("y"), lax.axis_index("c")
    other_chips = [(1 - x, y), (x, 1 - y), (1 - x, 1 - y)]
    return x, y, c, other_chips


def _gather_chips(w, *, name):
    R, C = w.shape
    H = R // 2

    def body(w_ref, out_ref, send_sems, recv_sems):
        x, y, c, chips = _place()
        me_chip = 2 * x + y
        sib = (x, y, 1 - c)

        def rows(chip, hc):
            return out_ref.at[chip, pl.ds(hc * H, H), :]

        def copy(k, blk, to, src=None):
            return pltpu.make_async_remote_copy(
                src_ref=blk if src is None else src, dst_ref=blk, send_sem=send_sems.at[k], recv_sem=recv_sems.at[k],
                device_id=to, device_id_type=MESH)

        first = [copy(j, rows(me_chip, c), (cx, cy, c), src=w_ref.at[pl.ds(c * H, H), :])
                 for j, (cx, cy) in enumerate(chips)]
        for cp in first:
            cp.start()
        passed = []
        for j, (cx, cy) in enumerate(chips):
            blk = rows(2 * cx + cy, c)
            copy(j, blk, (cx, cy, c)).wait_recv()
            fw = copy(3 + j, blk, sib)
            fw.start()
            passed.append(fw)
        for j, (cx, cy) in enumerate(chips):
            copy(3 + j, rows(2 * cx + cy, 1 - c), sib).wait_recv()
        for cp in first + passed:
            cp.wait_send()

    return pl.pallas_call(
        body, name=name, in_specs=[_ANY], out_specs=_ANY,
        out_shape=jax.ShapeDtypeStruct((4, R, C), w.dtype),
        scratch_shapes=[pltpu.SemaphoreType.DMA((6,)), pltpu.SemaphoreType.DMA((6,))],
    )(w)


def _pair_swap(v, *, name):
    def body(v_ref, out_ref, send_sem, recv_sem):
        x, y, c, _ = _place()
        cp = pltpu.make_async_remote_copy(src_ref=v_ref, dst_ref=out_ref, send_sem=send_sem, recv_sem=recv_sem,
                                          device_id=(x, y, 1 - c), device_id_type=MESH)
        cp.start()
        cp.wait()

    return pl.pallas_call(
        body, name=name, in_specs=[_ANY], out_specs=_ANY, out_shape=jax.ShapeDtypeStruct(v.shape, v.dtype),
        scratch_shapes=[pltpu.SemaphoreType.DMA, pltpu.SemaphoreType.DMA],
    )(v)


def _chip_exchange(pv, *, name):
    def body(p_ref, out_ref, send_sems, recv_sems):
        x, y, c, chips = _place()
        me_chip = 2 * x + y
        sends = []
        for j, (cx, cy) in enumerate(chips):
            cp = pltpu.make_async_remote_copy(
                src_ref=p_ref.at[2 * cx + cy], dst_ref=out_ref.at[me_chip], send_sem=send_sems.at[j],
                recv_sem=recv_sems.at[j], device_id=(cx, cy, c), device_id_type=MESH)
            cp.start()
            sends.append(cp)
        for j, (cx, cy) in enumerate(chips):
            blk = out_ref.at[2 * cx + cy]
            pltpu.make_async_remote_copy(src_ref=blk, dst_ref=blk, send_sem=send_sems.at[j], recv_sem=recv_sems.at[j],
                                         device_id=(cx, cy, c), device_id_type=MESH).wait_recv()
        for cp in sends:
            cp.wait_send()

    return pl.pallas_call(
        body, name=name, in_specs=[_ANY], out_specs=_ANY, out_shape=jax.ShapeDtypeStruct(pv.shape, pv.dtype),
        scratch_shapes=[pltpu.SemaphoreType.DMA((3,)), pltpu.SemaphoreType.DMA((3,))],
    )(pv)


WEIGHTS = [
    ("attn_w_in", 2), ("attn_b_f", None), ("attn_w_out", 1), ("ssm_w_in", 2), ("ssm_conv_w", 2), ("ssm_conv_b", 1),
    ("ssm_dt_bias", None), ("ssm_A_log", None), ("ssm_D", None), ("ssm_norm_w", 1), ("ssm_w_out", 1),
    ("ln_mix_g", None), ("ln_mix_b", None), ("ffn_w_up", 2), ("ffn_conv_w", 2), ("ffn_conv_b", None),
    ("ffn_w_down", 1), ("ln_ffn_g", None), ("ln_ffn_b", None), ("ple_w_proj", 2), ("ple_w_gate", 1),
    ("ple_b_gate", None),
]
N_CHIPS = 4
MATMUL_WEIGHTS = ("attn_w_in", "attn_w_out", "ssm_w_in", "ssm_w_out", "ffn_w_up", "ffn_w_down", "ple_w_proj",
                  "ple_w_gate")


def _pack(arrays):
    parts = []
    total = 0
    for a in arrays:
        n = a.size
        pad = (-n) % PACK_COLS
        flat = a.reshape(-1)
        parts.append(jnp.pad(flat, (0, pad)) if pad else flat)
        total += n + pad
    rows = total // PACK_COLS
    rpad = (-rows) % PACK_ROW_ALIGN
    if rpad:
        parts.append(jnp.zeros((rpad * PACK_COLS,), arrays[0].dtype))
    return jnp.concatenate(parts).reshape(rows + rpad, PACK_COLS)


def _unpack(buf, shapes):
    flat = buf.reshape(-1)
    out = []
    off = 0
    for s in shapes:
        n = math.prod(s)
        out.append(flat[off:off + n].reshape(s))
        off += n + ((-n) % PACK_COLS)
    return out


def _from_row_layout(a):
    return jnp.pad(a.T, ((0, 0), (0, LANE - SSM_HEADS)))


def _pad_lanes(v, n=LANE):
    return jnp.pad(v, (0, n - v.shape[0])).reshape(1, n)


def _local_step(x, p, target, W):
    T = x.shape[0]
    row = lambda v: v.reshape(1, -1)
    attn_in = jnp.pad(W["attn_w_in"][0], ((0, 0), (0, ATTN_IN_PAD - W["attn_w_in"].shape[2])))
    ssm_in = jnp.pad(W["ssm_w_in"][0], ((0, 0), (0, SSM_IN_PAD - W["ssm_w_in"].shape[2])))
    bf = _pad_lanes(W["attn_b_f"][0])
    dt_bias = _pad_lanes(W["ssm_dt_bias"][0])
    a_log = _pad_lanes(W["ssm_A_log"][0])
    d_exp = jnp.repeat(W["ssm_D"][0], _PH).reshape(1, SSM_D_INNER)
    norm_w = row(W["ssm_norm_w"][0])
    G = {}

    def ffn_ple_fwd(i, xin, mix, tag):
        s = {}
        s["z1"], s["h1"], s["h1b"] = _ln_fwd(xin, mix, row(W["ln_mix_g"][i]), row(W["ln_mix_b"][i]),
                                             name=f"ln_mix_fwd{tag}")
        s["up"] = _mm(s["h1b"], W["ffn_w_up"][i], name=f"ffn_up{tag}")
        s["a"] = _ffn_act_fwd(s["up"], W["ffn_conv_w"][i], row(W["ffn_conv_b"][i]), name=f"ffn_act_fwd{tag}")
        ffn = _mm(s["a"], W["ffn_w_down"][i], name=f"ffn_down{tag}")
        s["z2"], s["h2"], s["h2b"] = _ln_fwd(s["h1"], ffn, row(W["ln_ffn_g"][i]), row(W["ln_ffn_b"][i]),
                                             name=f"ln_ffn_fwd{tag}")
        s["G"] = _mm(s["h2b"], W["ple_w_gate"][i], name=f"ple_gate_mm{tag}")
        s["pp"] = _mm(pb[i], W["ple_w_proj"][i], name=f"ple_proj_mm{tag}")
        out, outb = _ple_fwd(s["h2"], s["G"], row(W["ple_b_gate"][i]), s["pp"], name=f"ple_fwd{tag}")
        return out, outb, s

    def ffn_ple_bwd(i, dx, s, tag):
        g = {}
        dG, dpp, g["ple_b_gate"] = _ple_bwd(dx, s["G"], row(W["ple_b_gate"][i]), s["pp"], name=f"ple_bwd{tag}")
        g["ple_w_gate"] = _mm(s["h2b"], dG, ta=True, name=f"ple_gate_dw{tag}")
        g["ple_w_proj"] = _mm(pb[i], dpp, ta=True, name=f"ple_proj_dw{tag}")
        dh2 = _mm(dG, W["ple_w_gate"][i], tb=True, add=dx, name=f"ple_gate_dx{tag}")
        dz2, dz2b, g["ln_ffn_g"], g["ln_ffn_b"] = _ln_bwd(dh2, s["z2"], row(W["ln_ffn_g"][i]), name=f"ln_ffn_bwd{tag}")
        da = _mm(dz2b, W["ffn_w_down"][i], tb=True, out_dtype=BF16, name=f"ffn_down_dx{tag}")
        g["ffn_w_down"] = _mm(s["a"], dz2b, ta=True, name=f"ffn_down_dw{tag}")
        dup, dgc, g["ffn_conv_b"], g["ffn_conv_w"] = _ffn_act_bwd(
            da, s["up"], W["ffn_conv_w"][i], row(W["ffn_conv_b"][i]), name=f"ffn_act_bwd{tag}")
        dup = _dwconv_bwd_data(dgc, W["ffn_conv_w"][i], FFN_CONV, dup, FFN_DIM, name=f"ffn_conv_bwd{tag}")
        g["ffn_w_up"] = _mm(s["h1b"], dup, ta=True, name=f"ffn_up_dw{tag}")
        dh1 = _mm(dup, W["ffn_w_up"][i], tb=True, add=dz2, add_scale=DEEPNORM_ALPHA, name=f"ffn_up_dx{tag}")
        dz1, dz1b, g["ln_mix_g"], g["ln_mix_b"] = _ln_bwd(dh1, s["z1"], row(W["ln_mix_g"][i]), name=f"ln_mix_bwd{tag}")
        return dz1, dz1b, g

    xb = x.astype(BF16)
    pb = p.astype(BF16)
    proj0 = _mm(xb, attn_in, name="attn_in")
    c_col = _fox_gate_fwd(proj0, bf, name="fox_gate_fwd")
    cT = (c_col[:, :ATTN_HEADS] * LOG2E).T.reshape(HEAD_PAIRS, 2, T)
    qa, qb, kk, ka, kb, vv, va, vb = _attn_prep(proj0, name="attn_prep")
    o, ob, lsea, lseb = _attn_fwd(qa, qb, kk, va, vb, cT, name="attn_fwd")
    mix0 = _mm(ob, W["attn_w_out"][0], name="attn_out")
    x1, x1b, s0 = ffn_ple_fwd(0, x, mix0, "0")

    proj1 = _mm(x1b, ssm_in, name="ssm_in")
    dt, acum = _ssd_dt_fwd(proj1, dt_bias, a_log, name="ssd_dt_fwd")
    xbc = _conv_silu_fwd(proj1, W["ssm_conv_w"][0], row(W["ssm_conv_b"][0]), name="ssd_conv_fwd")
    dtr, acr = dt[:, :SSM_HEADS].T, acum[:, :SSM_HEADS].T
    ys, states = _ssd_scan_fwd(xbc, dt, acum, dtr, acr, name="ssd_scan_fwd")
    yn = _gate_norm_fwd(ys, xbc, proj1, d_exp, norm_w, name="ssd_gate_norm_fwd")
    mix1 = _mm(yn, W["ssm_w_out"][0], name="ssm_out")
    x2, _, s1 = ffn_ple_fwd(1, x1, mix1, "1")

    lpart, dy = _loss_head(x2, target, name="loss_head")
    loss = jnp.sum(lpart)

    dz1, dz1b, g1 = ffn_ple_bwd(1, dy, s1, "1")
    G["ssm_w_out"] = _mm(yn, dz1b, ta=True, name="ssm_out_dw")[None]
    dyn = _mm(dz1b, W["ssm_w_out"][0], tb=True, name="ssm_out_dx")
    dproj1, dys, dskip, dnw, dde = _gate_norm_bwd(dyn, ys, xbc, proj1, d_exp, norm_w, name="ssd_gate_norm_bwd")
    G["ssm_norm_w"] = dnw
    G["ssm_D"] = dde.reshape(SSM_HEADS, _PH).sum(axis=1)[None]
    dxbc, dac, dar, ddc, ddr = _ssd_scan_bwd(xbc, dys, dskip, states, dt, acum, dtr, acr, name="ssd_scan_bwd")
    dproj1, dal, ddb = _ssd_dt_bwd(dac, _from_row_layout(dar), ddc, _from_row_layout(ddr), dt, proj1, dt_bias, a_log,
                                   dproj1, name="ssd_dt_bwd")
    G["ssm_A_log"] = dal[:, :SSM_HEADS]
    G["ssm_dt_bias"] = ddb[:, :SSM_HEADS]
    dpre, G["ssm_conv_b"], dcw = _conv_silu_bwd(dxbc, proj1, W["ssm_conv_w"][0], row(W["ssm_conv_b"][0]),
                                                name="ssd_conv_bwd")
    G["ssm_conv_w"] = dcw[None]
    dproj1 = _dwconv_bwd_data(dpre, W["ssm_conv_w"][0], SSM_CONV, dproj1, SSM_D_INNER, name="ssd_conv_bwd_data")
    G["ssm_w_in"] = _mm(x1b, dproj1, ta=True, name="ssm_in_dw")[None, :, :W["ssm_w_in"].shape[2]]
    dx1 = _mm(dproj1, ssm_in, tb=True, add=dz1, add_scale=DEEPNORM_ALPHA, name="ssm_in_dx")

    dz0, dz0b, g0 = ffn_ple_bwd(0, dx1, s0, "0")
    G["attn_w_out"] = _mm(ob, dz0b, ta=True, name="attn_out_dw")[None]
    do = _mm(dz0b, W["attn_w_out"][0], tb=True, name="attn_out_dx")
    doa, dob, dlta, dltb = _attn_bwd_prep(do, o, name="attn_bwd_prep")
    dq, dk, dv, dcT, dcq = _attn_bwd(qa, qb, kk, ka, kb, vv, doa, dob, lsea, lseb, dlta, dltb, cT, name="attn_bwd")
    dcq = dcq.reshape(T, HEAD_PAIRS, 2, HALF)[:, :, :, 0].reshape(T, ATTN_HEADS)
    dc_col = jnp.pad(dcT.reshape(ATTN_HEADS, T).T + dcq, ((0, 0), (0, LANE - ATTN_HEADS)))
    dfl, dbf = _fox_gate_bwd(dc_col, proj0, bf, name="fox_gate_bwd")
    G["attn_b_f"] = dbf[:, :ATTN_HEADS]
    dproj0 = jnp.concatenate([dq.astype(BF16), dk.astype(BF16), dv.astype(BF16), dfl.astype(BF16)], axis=1)
    G["attn_w_in"] = _mm(xb, dproj0, ta=True, name="attn_in_dw")[None, :, :W["attn_w_in"].shape[2]]
    grad_x = _mm(dproj0, attn_in, tb=True, add=dz0, add_scale=DEEPNORM_ALPHA, name="attn_in_dx")

    for k in g0:
        G[k] = jnp.stack([g0[k].reshape(W[k].shape[1:]), g1[k].reshape(W[k].shape[1:])])
    return loss, grad_x, G


def kernel(x, p, attn_w_in, attn_b_f, attn_w_out, ssm_w_in, ssm_conv_w, ssm_conv_b, ssm_dt_bias, ssm_A_log, ssm_D, ssm_norm_w, ssm_w_out, ln_mix_g, ln_mix_b, ffn_w_up, ffn_conv_w, ffn_conv_b, ffn_w_down, ln_ffn_g, ln_ffn_b, ple_w_proj, ple_w_gate, ple_b_gate, loss_target, m_attn_w_in, m_attn_b_f, m_attn_w_out, m_ssm_w_in, m_ssm_conv_w, m_ssm_conv_b, m_ssm_dt_bias, m_ssm_A_log, m_ssm_D, m_ssm_norm_w, m_ssm_w_out, m_ln_mix_g, m_ln_mix_b, m_ffn_w_up, m_ffn_conv_w, m_ffn_conv_b, m_ffn_w_down, m_ln_ffn_g, m_ln_ffn_b, m_ple_w_proj, m_ple_w_gate, m_ple_b_gate, v_attn_w_in, v_attn_b_f, v_attn_w_out, v_ssm_w_in, v_ssm_conv_w, v_ssm_conv_b, v_ssm_dt_bias, v_ssm_A_log, v_ssm_D, v_ssm_norm_w, v_ssm_w_out, v_ln_mix_g, v_ln_mix_b, v_ffn_w_up, v_ffn_conv_w, v_ffn_conv_b, v_ffn_w_down, v_ln_ffn_g, v_ln_ffn_b, v_ple_w_proj, v_ple_w_gate, v_ple_b_gate):
    names = [n for n, _ in WEIGHTS]
    axes = dict(WEIGHTS)
    w_loc = dict(zip(names, [attn_w_in, attn_b_f, attn_w_out, ssm_w_in, ssm_conv_w, ssm_conv_b, ssm_dt_bias, ssm_A_log, ssm_D, ssm_norm_w, ssm_w_out, ln_mix_g, ln_mix_b, ffn_w_up, ffn_conv_w, ffn_conv_b, ffn_w_down, ln_ffn_g, ln_ffn_b, ple_w_proj, ple_w_gate, ple_b_gate]))
    m_loc = dict(zip(names, [m_attn_w_in, m_attn_b_f, m_attn_w_out, m_ssm_w_in, m_ssm_conv_w, m_ssm_conv_b, m_ssm_dt_bias, m_ssm_A_log, m_ssm_D, m_ssm_norm_w, m_ssm_w_out, m_ln_mix_g, m_ln_mix_b, m_ffn_w_up, m_ffn_conv_w, m_ffn_conv_b, m_ffn_w_down, m_ln_ffn_g, m_ln_ffn_b, m_ple_w_proj, m_ple_w_gate, m_ple_b_gate]))
    v_loc = dict(zip(names, [v_attn_w_in, v_attn_b_f, v_attn_w_out, v_ssm_w_in, v_ssm_conv_w, v_ssm_conv_b, v_ssm_dt_bias, v_ssm_A_log, v_ssm_D, v_ssm_norm_w, v_ssm_w_out, v_ln_mix_g, v_ln_mix_b, v_ffn_w_up, v_ffn_conv_w, v_ffn_conv_b, v_ffn_w_down, v_ln_ffn_g, v_ln_ffn_b, v_ple_w_proj, v_ple_w_gate, v_ple_b_gate]))
    sharded = [n for n in names if axes[n] is not None]
    matrices = [n for n in sharded if n in MATMUL_WEIGHTS]

    def wire(n):
        if n in matrices:
            return w_loc[n].astype(BF16)
        return lax.bitcast_convert_type(w_loc[n], BF16)

    wired = [wire(n) for n in sharded]
    me_chip = 2 * lax.axis_index("x") + lax.axis_index("y")
    packed = _pack(wired)
    gathered = lax.dynamic_update_index_in_dim(_gather_chips(packed, name="gather_weights"), packed, me_chip, 0)
    W = dict(w_loc)
    per_chip = [_unpack(gathered[k], [w.shape for w in wired]) for k in range(N_CHIPS)]
    for i, n in enumerate(sharded):
        pieces = [per_chip[k][i] for k in range(N_CHIPS)]
        if n not in matrices:
            pieces = [lax.bitcast_convert_type(q, F32) for q in pieces]
        W[n] = jnp.concatenate(pieces, axis=axes[n])

    loss, grad_x, G = _local_step(x[0], p[:, 0], loss_target[0], W)
    loss = lax.psum(loss, ("x", "y", "c"))

    def slot(k):
        parts = []
        for n in names:
            g = G[n].reshape(W[n].shape)
            if axes[n] is not None:
                size = w_loc[n].shape[axes[n]]
                g = lax.slice_in_dim(g, k * size, (k + 1) * size, axis=axes[n])
            parts.append(g)
        return _pack(parts)

    contrib = jnp.stack([slot(k) for k in range(N_CHIPS)])
    R = contrib.shape[1]
    H = R // 2
    c = lax.axis_index("c")
    keep = lax.dynamic_slice_in_dim(contrib, c * H, H, axis=1)
    give = lax.dynamic_slice_in_dim(contrib, (1 - c) * H, H, axis=1)
    pair = _add2(keep, _pair_swap(give, name="grad_pair_swap"), name="grad_pair_sum")
    from_chips = lax.dynamic_update_index_in_dim(
        _chip_exchange(pair, name="grad_chip_exchange"), lax.dynamic_index_in_dim(pair, me_chip, 0, keepdims=False),
        me_chip, 0)
    half = _sum4(from_chips, name="grad_chip_sum")
    other = _pair_swap(half, name="grad_half_swap")
    gflat = jnp.concatenate([jnp.where(c == 0, half, other), jnp.where(c == 0, other, half)])

    shapes = [w_loc[n].shape for n in names]
    delta, new_m, new_v = _adamw(_pack([w_loc[n] for n in names]), gflat, _pack([m_loc[n] for n in names]),
                                 _pack([v_loc[n] for n in names]), name="adamw")
    return (loss, grad_x[None], *_unpack(gflat, shapes), *_unpack(delta, shapes), *_unpack(new_m, shapes),
            *_unpack(new_v, shapes))
```

```python
import functools
import math

import jax
import jax.numpy as jnp
from jax import lax
from jax.experimental import pallas as pl
from jax.experimental.pallas import tpu as pltpu

F32 = jnp.float32
BF16 = jnp.bfloat16
MESH = pl.DeviceIdType.MESH

D_MODEL = 1024
ATTN_HEADS = 16
HEAD_PAIRS = ATTN_HEADS // 2
SSM_D_INNER = 2048
SSM_HEADS = 32
SSM_GROUPS = 8
SSM_STATE = 128
SSM_CONV = 4
SSM_CHUNK = 128
SSM_XBC = SSM_D_INNER + 2 * SSM_GROUPS * SSM_STATE
FFN_DIM = 2816
FFN_CONV = 3
DEPTH = 2
LN_EPS = 1e-5
RMS_EPS = 1e-5
DEEPNORM_ALPHA = (2 * DEPTH) ** 0.25
ADAM_LR = 0.001
ADAM_B1 = 0.9
ADAM_B2 = 0.999
ADAM_EPS = 1e-08
ADAM_WD = 0.01
ADAM_STEP = 10

LANE = 128
SUBLANE = 8
HALO = SUBLANE
NEG = -1e30
ATTN_IN_PAD = 3 * D_MODEL + LANE
SSM_IN_PAD = 2 * SSM_D_INNER + 2 * SSM_GROUPS * SSM_STATE + LANE
PACK_COLS = 1024
PACK_ROW_ALIGN = 512

ATTN_BLOCK = 1024
ROW_BLOCK = 512
NARROW_ROW_BLOCK = 1024
CUM_BLOCK = 256


def _params(sem, vmem_mb=48):
    return pltpu.CompilerParams(dimension_semantics=sem, vmem_limit_bytes=vmem_mb * 2 ** 20)


def _pick(n, target, mult=LANE):
    best = None
    d = mult
    while d <= min(n, target):
        if n % d == 0:
            best = d
        d += mult
    return n if best is None else best


def _sigmoid(x):
    return 1.0 / (1.0 + jnp.exp(-x))


def _log1p(u):
    w = 1.0 + u
    return jnp.where(w == 1.0, u, jnp.log(w) * (u / (w - 1.0)))


def _softplus(x):
    return jnp.maximum(x, 0.0) + _log1p(jnp.exp(-jnp.abs(x)))


def _split3(x):
    hi = x.astype(BF16)
    r1 = x - hi.astype(F32)
    mid = r1.astype(BF16)
    lo = (r1 - mid.astype(F32)).astype(BF16)
    return hi, mid, lo


def _tri_matmul(tri, x):
    out = None
    for part in _split3(x):
        t = jnp.dot(tri, part, preferred_element_type=F32)
        out = t if out is None else out + t
    return out


def _tri(n, lower):
    r = lax.broadcasted_iota(jnp.int32, (n, n), 0)
    c = lax.broadcasted_iota(jnp.int32, (n, n), 1)
    return jnp.where((c <= r) if lower else (c >= r), 1.0, 0.0).astype(BF16)


_ANY = pl.BlockSpec(memory_space=pl.ANY)
MM_OUT_BLOCK_BYTES = 13 * 2 ** 20
MM_IN_BLOCK_BYTES = 6 * 2 ** 20
MM_WIDE_K = 3200


def _mm(a, b, *, name, ta=False, tb=False, add=None, add_scale=1.0, out_dtype=F32):
    if ta:
        K, M = a.shape
    else:
        M, K = a.shape
    if tb:
        N, Kb = b.shape
    else:
        Kb, N = b.shape
    assert K == Kb, (a.shape, b.shape, ta, tb)
    if ta:
        assert add is None and out_dtype == F32
        bm = _pick(M, 2816)
        bn = _pick(N, MM_OUT_BLOCK_BYTES // (4 * bm))
        bk = _pick(K, max(512, MM_IN_BLOCK_BYTES // (2 * max(bm, bn))))
    else:
        bm = _pick(M, 1024)
        bn = _pick(N, 1536 if K <= MM_WIDE_K else 512)
        bk = K
    nk = K // bk
    a_spec = pl.BlockSpec((bk, bm), lambda i, j, k: (k, i)) if ta else pl.BlockSpec((bm, bk), lambda i, j, k: (i, k))
    b_spec = pl.BlockSpec((bn, bk), lambda i, j, k: (j, k)) if tb else pl.BlockSpec((bk, bn), lambda i, j, k: (k, j))
    o_spec = pl.BlockSpec((bm, bn), lambda i, j, k: (i, j))
    dims = (((0 if ta else 1,), (1 if tb else 0,)), ((), ()))
    has_add = add is not None

    def kern(*refs):
        a_ref, b_ref = refs[0], refs[1]
        add_ref = refs[2] if has_add else None
        o_ref = refs[3] if has_add else refs[2]
        k = pl.program_id(2)
        part = lax.dot_general(a_ref[...].astype(BF16), b_ref[...].astype(BF16), dims, preferred_element_type=F32)
        if nk == 1:
            o_ref[...] = (part + add_scale * add_ref[...] if has_add else part).astype(out_dtype)
        else:
            @pl.when(k == 0)
            def _():
                o_ref[...] = part

            @pl.when(k > 0)
            def _():
                o_ref[...] += part

    ins = [a, b] + ([add] if has_add else [])
    in_specs = [a_spec, b_spec] + ([o_spec] if has_add else [])
    return pl.pallas_call(
        kern, name=name, grid=(M // bm, N // bn, nk),
        in_specs=in_specs, out_specs=o_spec,
        out_shape=jax.ShapeDtypeStruct((M, N), out_dtype),
        compiler_params=_params(("parallel", "parallel", "arbitrary"), vmem_mb=56),
    )(*ins)


def _ln_stats(z):
    mu = jnp.mean(z, axis=-1, keepdims=True)
    zc = z - mu
    var = jnp.mean(zc * zc, axis=-1, keepdims=True)
    return zc, lax.rsqrt(var + LN_EPS)


def _ln_fwd(x, r, g, b, *, name):
    T, D = x.shape
    bt = _pick(T, ROW_BLOCK, SUBLANE)

    def kern(x_ref, r_ref, g_ref, b_ref, z_ref, h_ref, hb_ref):
        z = DEEPNORM_ALPHA * x_ref[...] + r_ref[...]
        zc, rstd = _ln_stats(z)
        h = zc * rstd * g_ref[...] + b_ref[...]
        z_ref[...] = z
        h_ref[...] = h
        hb_ref[...] = h.astype(BF16)

    row = pl.BlockSpec((bt, D), lambda i: (i, 0))
    vec = pl.BlockSpec((1, D), lambda i: (0, 0))
    return pl.pallas_call(
        kern, name=name, grid=(T // bt,), in_specs=[row, row, vec, vec], out_specs=[row, row, row],
        out_shape=[jax.ShapeDtypeStruct((T, D), F32)] * 2 + [jax.ShapeDtypeStruct((T, D), BF16)],
        compiler_params=_params(("parallel",)),
    )(x, r, g, b)


def _ln_bwd(dy, z, g, *, name):
    T, D = z.shape
    bt = _pick(T, ROW_BLOCK, SUBLANE)

    def kern(dy_ref, z_ref, g_ref, dz_ref, dzb_ref, dg_ref, db_ref):
        i = pl.program_id(0)
        zc, rstd = _ln_stats(z_ref[...])
        xhat = zc * rstd
        dyv = dy_ref[...]
        dxh = dyv * g_ref[...]
        m1 = jnp.mean(dxh, axis=-1, keepdims=True)
        m2 = jnp.mean(dxh * xhat, axis=-1, keepdims=True)
        dz = rstd * (dxh - m1 - xhat * m2)
        dz_ref[...] = dz
        dzb_ref[...] = dz.astype(BF16)

        @pl.when(i == 0)
        def _():
            dg_ref[...] = jnp.zeros_like(dg_ref)
            db_ref[...] = jnp.zeros_like(db_ref)

        dg_ref[...] += jnp.sum(dyv * xhat, axis=0, keepdims=True)
        db_ref[...] += jnp.sum(dyv, axis=0, keepdims=True)

    row = pl.BlockSpec((bt, D), lambda i: (i, 0))
    vec = pl.BlockSpec((1, D), lambda i: (0, 0))
    return pl.pallas_call(
        kern, name=name, grid=(T // bt,), in_specs=[row, row, vec], out_specs=[row, row, vec, vec],
        out_shape=[jax.ShapeDtypeStruct((T, D), F32), jax.ShapeDtypeStruct((T, D), BF16),
                   jax.ShapeDtypeStruct((1, D), F32), jax.ShapeDtypeStruct((1, D), F32)],
        compiler_params=_params(("arbitrary",)),
    )(dy, z, g)


def _conv_past(ext_ref, cw_ref, K, bt):
    out = None
    for k in range(K):
        term = cw_ref[k:k + 1, :] * ext_ref[pl.ds(HALO - (K - 1) + k, bt), :]
        out = term if out is None else out + term
    return out


def _fill_ext_past(ext_ref, halo_ref, cur, i, bt):
    ext_ref[pl.ds(0, HALO), :] = jnp.where(i > 0, halo_ref[...], 0.0)
    ext_ref[pl.ds(HALO, bt), :] = cur


def _halo_prev(bt, bc, off):
    return pl.BlockSpec((HALO, bc), lambda i, j: (jnp.maximum(i * (bt // HALO) - 1, 0), j + off))


def _normal_cdf(x):
    return 0.5 * (1.0 + lax.erf(x * (1.0 / math.sqrt(2.0))))


def _gelu(x):
    return x * _normal_cdf(x)


def _gelu_and_grad(x):
    cdf = _normal_cdf(x)
    return x * cdf, cdf + x * jnp.exp(-0.5 * x * x) * (1.0 / math.sqrt(2.0 * math.pi))


def _ffn_act_fwd(up, cw, cb, *, name):
    T, F2 = up.shape
    F = F2 // 2
    bt = _pick(T, ROW_BLOCK, SUBLANE)
    bc = _pick(F, 1408)
    nb = F // bc

    def kern(u_ref, g_ref, halo_ref, cw_ref, cb_ref, a_ref, ext_ref):
        i = pl.program_id(0)
        _fill_ext_past(ext_ref, halo_ref, g_ref[...], i, bt)
        gc = cb_ref[...] + _conv_past(ext_ref, cw_ref, FFN_CONV, bt)
        a_ref[...] = (_gelu(gc) * u_ref[...]).astype(BF16)

    return pl.pallas_call(
        kern, name=name, grid=(T // bt, nb),
        in_specs=[pl.BlockSpec((bt, bc), lambda i, j: (i, j)),
                  pl.BlockSpec((bt, bc), lambda i, j: (i, j + nb)),
                  _halo_prev(bt, bc, nb),
                  pl.BlockSpec((FFN_CONV, bc), lambda i, j: (0, j)),
                  pl.BlockSpec((1, bc), lambda i, j: (0, j))],
        out_specs=pl.BlockSpec((bt, bc), lambda i, j: (i, j)),
        out_shape=jax.ShapeDtypeStruct((T, F), BF16),
        scratch_shapes=[pltpu.VMEM((bt + HALO, bc), F32)],
        compiler_params=_params(("parallel", "parallel")),
    )(up, up, up, cw, cb)


def _ffn_act_bwd(da, up, cw, cb, *, name):
    T, F2 = up.shape
    F = F2 // 2
    bt = _pick(T, ROW_BLOCK, SUBLANE)
    bc = _pick(F, 1408)
    nb = F // bc
    K = FFN_CONV

    def kern(da_ref, u_ref, g_ref, halo_ref, cw_ref, cb_ref, du_ref, dgc_ref, dcb_ref, dcw_ref, ext_ref):
        i = pl.program_id(1)
        _fill_ext_past(ext_ref, halo_ref, g_ref[...], i, bt)
        gc = cb_ref[...] + _conv_past(ext_ref, cw_ref, K, bt)
        dav = da_ref[...]
        act, act_grad = _gelu_and_grad(gc)
        du_ref[...] = (dav * act).astype(BF16)
        dgc = dav * u_ref[...] * act_grad
        dgc_ref[...] = dgc

        @pl.when(i == 0)
        def _():
            dcb_ref[...] = jnp.zeros_like(dcb_ref)
            dcw_ref[...] = jnp.zeros_like(dcw_ref)

        dcb_ref[...] += jnp.sum(dgc, axis=0, keepdims=True)
        for k in range(K):
            dcw_ref[k:k + 1, :] += jnp.sum(dgc * ext_ref[pl.ds(HALO - (K - 1) + k, bt), :], axis=0, keepdims=True)

    blk = pl.BlockSpec((bt, bc), lambda j, i: (i, j))
    return pl.pallas_call(
        kern, name=name, grid=(nb, T // bt),
        in_specs=[blk, blk,
                  pl.BlockSpec((bt, bc), lambda j, i: (i, j + nb)),
                  pl.BlockSpec((HALO, bc), lambda j, i: (jnp.maximum(i * (bt // HALO) - 1, 0), j + nb)),
                  pl.BlockSpec((K, bc), lambda j, i: (0, j)),
                  pl.BlockSpec((1, bc), lambda j, i: (0, j))],
        out_specs=[blk, blk, pl.BlockSpec((1, bc), lambda j, i: (0, j)), pl.BlockSpec((K, bc), lambda j, i: (0, j))],
        out_shape=[jax.ShapeDtypeStruct((T, F2), BF16), jax.ShapeDtypeStruct((T, F), F32),
                   jax.ShapeDtypeStruct((1, F), F32), jax.ShapeDtypeStruct((K, F), F32)],
        scratch_shapes=[pltpu.VMEM((bt + HALO, bc), F32)],
        compiler_params=_params(("parallel", "arbitrary")),
    )(da, up, up, up, cw, cb)


def _dwconv_bwd_data(dgc, cw, K, into, col, *, name):
    T, C = dgc.shape
    bt = _pick(T, ROW_BLOCK, SUBLANE)
    bc = _pick(C, 1408)
    nt = T // bt
    last_halo = T // HALO - 1
    off = col // bc
    assert off * bc == col

    def kern(d_ref, halo_ref, cw_ref, into_ref, o_ref, ext_ref):
        i = pl.program_id(0)
        ext_ref[pl.ds(0, bt), :] = d_ref[...]
        ext_ref[pl.ds(bt, HALO), :] = jnp.where(i < nt - 1, halo_ref[...], 0.0)
        out = None
        for k in range(K):
            term = cw_ref[k:k + 1, :] * ext_ref[pl.ds(K - 1 - k, bt), :]
            out = term if out is None else out + term
        o_ref[...] = out.astype(o_ref.dtype)

    return pl.pallas_call(
        kern, name=name, grid=(nt, C // bc),
        in_specs=[pl.BlockSpec((bt, bc), lambda i, j: (i, j)),
                  pl.BlockSpec((HALO, bc), lambda i, j: (jnp.minimum((i + 1) * (bt // HALO), last_halo), j)),
                  pl.BlockSpec((K, bc), lambda i, j: (0, j)), _ANY],
        out_specs=pl.BlockSpec((bt, bc), lambda i, j: (i, j + off)),
        out_shape=jax.ShapeDtypeStruct(into.shape, into.dtype), input_output_aliases={3: 0},
        scratch_shapes=[pltpu.VMEM((bt + HALO, bc), F32)],
        compiler_params=_params(("parallel", "parallel")),
    )(dgc, dgc, cw, into)


def _ple_fwd(h, G, bg, pp, *, name):
    T, D = h.shape
    bt = _pick(T, ROW_BLOCK, SUBLANE)

    def kern(h_ref, G_ref, bg_ref, pp_ref, o_ref, ob_ref):
        out = h_ref[...] + _sigmoid(G_ref[...] + bg_ref[...]) * pp_ref[...]
        o_ref[...] = out
        ob_ref[...] = out.astype(BF16)

    row = pl.BlockSpec((bt, D), lambda i: (i, 0))
    vec = pl.BlockSpec((1, D), lambda i: (0, 0))
    return pl.pallas_call(
        kern, name=name, grid=(T // bt,), in_specs=[row, row, vec, row], out_specs=[row, row],
        out_shape=[jax.ShapeDtypeStruct((T, D), F32), jax.ShapeDtypeStruct((T, D), BF16)],
        compiler_params=_params(("parallel",)),
    )(h, G, bg, pp)


def _ple_bwd(dx, G, bg, pp, *, name):
    T, D = dx.shape
    bt = _pick(T, ROW_BLOCK, SUBLANE)

    def kern(dx_ref, G_ref, bg_ref, pp_ref, dG_ref, dpp_ref, dbg_ref):
        i = pl.program_id(0)
        gate = _sigmoid(G_ref[...] + bg_ref[...])
        dxv = dx_ref[...]
        dG = dxv * pp_ref[...] * gate * (1.0 - gate)
        dG_ref[...] = dG.astype(BF16)
        dpp_ref[...] = (dxv * gate).astype(BF16)

        @pl.when(i == 0)
        def _():
            dbg_ref[...] = jnp.zeros_like(dbg_ref)

        dbg_ref[...] += jnp.sum(dG, axis=0, keepdims=True)

    row = pl.BlockSpec((bt, D), lambda i: (i, 0))
    vec = pl.BlockSpec((1, D), lambda i: (0, 0))
    return pl.pallas_call(
        kern, name=name, grid=(T // bt,), in_specs=[row, row, vec, row], out_specs=[row, row, vec],
        out_shape=[jax.ShapeDtypeStruct((T, D), BF16), jax.ShapeDtypeStruct((T, D), BF16),
                   jax.ShapeDtypeStruct((1, D), F32)],
        compiler_params=_params(("arbitrary",)),
    )(dx, G, bg, pp)


def _fox_gate_fwd(proj, bf, *, name):
    T = proj.shape[0]
    bt = _pick(T, CUM_BLOCK, SUBLANE)
    fcol = 3 * D_MODEL // LANE

    def kern(f_ref, bf_ref, c_ref, carry_ref):
        i = pl.program_id(0)

        @pl.when(i == 0)
        def _():
            carry_ref[...] = jnp.zeros_like(carry_ref)

        x = f_ref[...] + bf_ref[...]
        lf = jnp.minimum(x, 0.0) - _log1p(jnp.exp(-jnp.abs(x)))
        cs = _tri_matmul(_tri(bt, True), lf) + carry_ref[...]
        c_ref[...] = cs
        carry_ref[...] = cs[bt - 1:bt, :]

    return pl.pallas_call(
        kern, name=name, grid=(T // bt,),
        in_specs=[pl.BlockSpec((bt, LANE), lambda i: (i, fcol)), pl.BlockSpec((1, LANE), lambda i: (0, 0))],
        out_specs=pl.BlockSpec((bt, LANE), lambda i: (i, 0)),
        out_shape=jax.ShapeDtypeStruct((T, LANE), F32),
        scratch_shapes=[pltpu.VMEM((1, LANE), F32)],
        compiler_params=_params(("arbitrary",)),
    )(proj, bf)


def _fox_gate_bwd(dc, proj, bf, *, name):
    T = proj.shape[0]
    bt = _pick(T, CUM_BLOCK, SUBLANE)
    nb = T // bt
    fcol = 3 * D_MODEL // LANE

    def kern(dc_ref, f_ref, bf_ref, df_ref, dbf_ref, carry_ref):
        i = pl.program_id(0)

        @pl.when(i == 0)
        def _():
            carry_ref[...] = jnp.zeros_like(carry_ref)
            dbf_ref[...] = jnp.zeros_like(dbf_ref)

        dlf = _tri_matmul(_tri(bt, False), dc_ref[...]) + carry_ref[...]
        carry_ref[...] = dlf[0:1, :]
        x = f_ref[...] + bf_ref[...]
        lane = lax.broadcasted_iota(jnp.int32, (bt, LANE), 1)
        df = jnp.where(lane < ATTN_HEADS, dlf / (1.0 + jnp.exp(x)), 0.0)
        df_ref[...] = df
        dbf_ref[...] += jnp.sum(df, axis=0, keepdims=True)

    return pl.pallas_call(
        kern, name=name, grid=(nb,),
        in_specs=[pl.BlockSpec((bt, LANE), lambda i: (nb - 1 - i, 0)),
                  pl.BlockSpec((bt, LANE), lambda i: (nb - 1 - i, fcol)),
                  pl.BlockSpec((1, LANE), lambda i: (0, 0))],
        out_specs=[pl.BlockSpec((bt, LANE), lambda i: (nb - 1 - i, 0)), pl.BlockSpec((1, LANE), lambda i: (0, 0))],
        out_shape=[jax.ShapeDtypeStruct((T, LANE), F32), jax.ShapeDtypeStruct((1, LANE), F32)],
        scratch_shapes=[pltpu.VMEM((1, LANE), F32)],
        compiler_params=_params(("arbitrary",)),
    )(dc, proj, bf)


_NT = (((1,), (1,)), ((), ()))
_TN = (((0,), (0,)), ((), ()))


def _dot(a, b, dims=None):
    if dims is None:
        return jnp.dot(a, b, preferred_element_type=F32)
    return lax.dot_general(a, b, dims, preferred_element_type=F32)


LOG2E = 1.0 / math.log(2.0)
LN2 = math.log(2.0)
Q_SCALE = 0.125 * LOG2E
HALF = LANE // 2
L_LANE = (HALF, 0)


def _attn_prep(proj, *, name):
    T = proj.shape[0]
    bt = _pick(T, ATTN_BLOCK)

    def kern(q_ref, k_ref, v_ref, qa_ref, qb_ref, kk_ref, ka_ref, kb_ref, vv_ref, va_ref, vb_ref):
        lane = lax.broadcasted_iota(jnp.int32, (bt, LANE), 1)
        lo = lane < HALF
        q = q_ref[...] * Q_SCALE
        k = k_ref[...]
        v = v_ref[...]
        qa_ref[...] = jnp.where(lo, q, 0.0).astype(BF16)
        qb_ref[...] = jnp.where(lo, 0.0, q).astype(BF16)
        kk_ref[...] = k.astype(BF16)
        ka_ref[...] = jnp.where(lo, k, 0.0).astype(BF16)
        kb_ref[...] = jnp.where(lo, 0.0, k).astype(BF16)
        vv_ref[...] = v.astype(BF16)
        va_ref[...] = jnp.where(lo, v, jnp.where(lane == L_LANE[0], 1.0, 0.0)).astype(BF16)
        vb_ref[...] = jnp.where(lo, jnp.where(lane == L_LANE[1], 1.0, 0.0), v).astype(BF16)

    kcol, vcol = D_MODEL // LANE, 2 * D_MODEL // LANE
    out = pl.BlockSpec((bt, LANE), lambda i, hp: (i, hp))
    return pl.pallas_call(
        kern, name=name, grid=(T // bt, HEAD_PAIRS),
        in_specs=[out, pl.BlockSpec((bt, LANE), lambda i, hp: (i, kcol + hp)),
                  pl.BlockSpec((bt, LANE), lambda i, hp: (i, vcol + hp))],
        out_specs=[out] * 8, out_shape=[jax.ShapeDtypeStruct((T, D_MODEL), BF16)] * 8,
        compiler_params=_params(("parallel", "parallel")),
    )(proj, proj, proj)


def _attn_fwd(qa, qb, kk, va, vb, cT, *, name):
    T = qa.shape[0]
    tb = _pick(T, ATTN_BLOCK)
    nq = T // tb
    rep = tb // LANE

    def kern(qa_ref, qb_ref, k_ref, va_ref, vb_ref, c_ref, o_ref, ob_ref, lsea_ref, lseb_ref, m_ref, acc_ref):
        qi = pl.program_id(1)
        ki = pl.program_id(2)

        @pl.when(ki == 0)
        def _():
            m_ref[...] = jnp.full_like(m_ref, NEG)
            acc_ref[...] = jnp.zeros_like(acc_ref)

        def step(diag):
            k = k_ref[...]
            for h, (q_ref, v_ref) in enumerate(((qa_ref, va_ref), (qb_ref, vb_ref))):
                s = _dot(q_ref[...], k, _NT) - c_ref[h:h + 1, :]
                if diag:
                    r = lax.broadcasted_iota(jnp.int32, (tb, tb), 0)
                    c = lax.broadcasted_iota(jnp.int32, (tb, tb), 1)
                    s = jnp.where(c <= r, s, NEG)
                m_prev = m_ref[h]
                m_new = jnp.maximum(m_prev, jnp.max(s, axis=1, keepdims=True))
                p = jnp.exp2(s - jnp.tile(m_new, (1, rep)))
                acc_ref[h] = acc_ref[h] * jnp.exp2(m_prev - m_new) + _dot(p.astype(BF16), v_ref[...])
                m_ref[h] = m_new

        @pl.when(ki < qi)
        def _():
            step(False)

        @pl.when(ki == qi)
        def _():
            step(True)
            lo = lax.broadcasted_iota(jnp.int32, (tb, LANE), 1) < HALF
            a0, a1 = acc_ref[0], acc_ref[1]
            l0 = a0[:, L_LANE[0]:L_LANE[0] + 1]
            l1 = a1[:, L_LANE[1]:L_LANE[1] + 1]
            o = jnp.where(lo, a0 / l0, a1 / l1)
            o_ref[...] = o
            ob_ref[...] = o.astype(BF16)
            lsea_ref[...] = m_ref[0] + jnp.log(l0) * LOG2E
            lseb_ref[...] = m_ref[1] + jnp.log(l1) * LOG2E

    qspec = pl.BlockSpec((tb, LANE), lambda hp, qi, ki: (qi, hp))
    kspec = pl.BlockSpec((tb, LANE), lambda hp, qi, ki: (jnp.minimum(ki, qi), hp))
    return pl.pallas_call(
        kern, name=name, grid=(HEAD_PAIRS, nq, nq),
        in_specs=[qspec, qspec, kspec, kspec, kspec,
                  pl.BlockSpec((None, 2, tb), lambda hp, qi, ki: (hp, 0, jnp.minimum(ki, qi)))],
        out_specs=[qspec, qspec, qspec, qspec],
        out_shape=[jax.ShapeDtypeStruct((T, D_MODEL), F32), jax.ShapeDtypeStruct((T, D_MODEL), BF16),
                   jax.ShapeDtypeStruct((T, D_MODEL), F32), jax.ShapeDtypeStruct((T, D_MODEL), F32)],
        scratch_shapes=[pltpu.VMEM((2, tb, LANE), F32), pltpu.VMEM((2, tb, LANE), F32)],
        compiler_params=_params(("parallel", "parallel", "arbitrary")),
    )(qa, qb, kk, va, vb, cT)


def _attn_bwd_prep(do, o, *, name):
    T, D = do.shape
    bt = _pick(T, ATTN_BLOCK)

    def kern(do_ref, o_ref, doa_ref, dob_ref, dlta_ref, dltb_ref):
        lo = lax.broadcasted_iota(jnp.int32, (bt, LANE), 1) < HALF
        dov = do_ref[...]
        prod = dov * o_ref[...]
        doa_ref[...] = jnp.where(lo, dov, 0.0).astype(BF16)
        dob_ref[...] = jnp.where(lo, 0.0, dov).astype(BF16)
        dlta_ref[...] = jnp.broadcast_to(jnp.sum(jnp.where(lo, prod, 0.0), axis=1, keepdims=True), (bt, LANE))
        dltb_ref[...] = jnp.broadcast_to(jnp.sum(jnp.where(lo, 0.0, prod), axis=1, keepdims=True), (bt, LANE))

    blk = pl.BlockSpec((bt, LANE), lambda i, hp: (i, hp))
    return pl.pallas_call(
        kern, name=name, grid=(T // bt, HEAD_PAIRS), in_specs=[blk, blk], out_specs=[blk] * 4,
        out_shape=[jax.ShapeDtypeStruct((T, D), BF16)] * 2 + [jax.ShapeDtypeStruct((T, D), F32)] * 2,
        compiler_params=_params(("parallel", "parallel")),
    )(do, o)


def _attn_bwd(qa, qb, kk, ka, kb, vv, doa, dob, lsea, lseb, dlta, dltb, cT, *, name):
    T = qa.shape[0]
    tb = _pick(T, ATTN_BLOCK)
    nq = T // tb
    rep = tb // LANE

    def kern(qa_ref, qb_ref, k_ref, ka_ref, kb_ref, v_ref, doa_ref, dob_ref, lsea_ref, lseb_ref, dlta_ref, dltb_ref,
             c_ref, dq_ref, dk_ref, dv_ref, dc_ref, dcq_ref):
        ki = pl.program_id(1)
        qi = pl.program_id(2)

        @pl.when(jnp.logical_and(ki == 0, qi == 0))
        def _():
            dq_ref[...] = jnp.zeros_like(dq_ref)
            dcq_ref[...] = jnp.zeros_like(dcq_ref)

        @pl.when(qi == 0)
        def _():
            dk_ref[...] = jnp.zeros_like(dk_ref)
            dv_ref[...] = jnp.zeros_like(dv_ref)
            dc_ref[...] = jnp.zeros_like(dc_ref)

        def step(diag):
            k = k_ref[...]
            v = v_ref[...]
            dq = None
            dk = None
            dv = None
            row_sums = []
            heads = ((qa_ref, ka_ref, doa_ref, lsea_ref, dlta_ref), (qb_ref, kb_ref, dob_ref, lseb_ref, dltb_ref))
            for h, (q_ref, km_ref, do_ref, lse_ref, dlt_ref) in enumerate(heads):
                q = q_ref[...]
                dom = do_ref[...]
                s = _dot(q, k, _NT) - c_ref[h:h + 1, :]
                if diag:
                    r = lax.broadcasted_iota(jnp.int32, (tb, tb), 0)
                    c = lax.broadcasted_iota(jnp.int32, (tb, tb), 1)
                    s = jnp.where(c <= r, s, NEG)
                p = jnp.exp2(s - jnp.tile(lse_ref[...], (1, rep)))
                ds = p * (_dot(dom, v, _NT) - jnp.tile(dlt_ref[...], (1, rep)))
                dc_ref[h:h + 1, :] -= jnp.sum(ds, axis=0, keepdims=True)
                row_sums.append(jnp.sum(ds, axis=1, keepdims=True))
                dsb = ds.astype(BF16)
                tv = _dot(p.astype(BF16), dom, _TN)
                tk = _dot(dsb, q, _TN)
                tq = _dot(dsb, km_ref[...])
                dv = tv if dv is None else dv + tv
                dk = tk if dk is None else dk + tk
                dq = tq if dq is None else dq + tq
            dv_ref[...] += dv
            dk_ref[...] += dk * LN2
            rows = pl.ds(pl.multiple_of(qi * tb, tb), tb)
            dq_ref[rows, :] += dq * 0.125
            lo = lax.broadcasted_iota(jnp.int32, (tb, LANE), 1) < HALF
            dcq_ref[rows, :] += jnp.where(lo, row_sums[0], row_sums[1])

        @pl.when(qi > ki)
        def _():
            step(False)

        @pl.when(qi == ki)
        def _():
            step(True)

    qspec = pl.BlockSpec((tb, LANE), lambda hp, ki, qi: (jnp.maximum(qi, ki), hp))
    kspec = pl.BlockSpec((tb, LANE), lambda hp, ki, qi: (ki, hp))
    cspec = pl.BlockSpec((None, 2, tb), lambda hp, ki, qi: (hp, 0, ki))
    qacc = pl.BlockSpec((T, LANE), lambda hp, ki, qi: (0, hp), pipeline_mode=pl.Buffered(1))
    return pl.pallas_call(
        kern, name=name, grid=(HEAD_PAIRS, nq, nq),
        in_specs=[qspec, qspec, kspec, kspec, kspec, kspec, qspec, qspec, qspec, qspec, qspec, qspec, cspec],
        out_specs=[qacc, kspec, kspec, cspec, qacc],
        out_shape=[jax.ShapeDtypeStruct((T, D_MODEL), F32)] * 3 + [jax.ShapeDtypeStruct((HEAD_PAIRS, 2, T), F32),
                                                                   jax.ShapeDtypeStruct((T, D_MODEL), F32)],
        compiler_params=_params(("parallel", "arbitrary", "arbitrary"), vmem_mb=56),
    )(qa, qb, kk, ka, kb, vv, doa, dob, lsea, lseb, dlta, dltb, cT)


def _ssd_dt_fwd(proj, dt_bias, a_log, *, name):
    T = proj.shape[0]
    Q = SSM_CHUNK
    col = (2 * SSM_D_INNER + 2 * SSM_GROUPS * SSM_STATE) // LANE

    def kern(raw_ref, b_ref, al_ref, dt_ref, ac_ref):
        dt = _softplus(raw_ref[...] + b_ref[...])
        dt_ref[...] = dt
        ac_ref[...] = _tri_matmul(_tri(Q, True), dt * (-jnp.exp(al_ref[...])))

    vec = pl.BlockSpec((1, LANE), lambda i: (0, 0))
    blk = pl.BlockSpec((Q, LANE), lambda i: (i, 0))
    return pl.pallas_call(
        kern, name=name, grid=(T // Q,),
        in_specs=[pl.BlockSpec((Q, LANE), lambda i: (i, col)), vec, vec], out_specs=[blk, blk],
        out_shape=[jax.ShapeDtypeStruct((T, LANE), F32)] * 2,
        compiler_params=_params(("parallel",)),
    )(proj, dt_bias, a_log)


def _ssd_dt_bwd(da_a, da_b, ddt_a, ddt_b, dt, proj, dt_bias, a_log, into, *, name):
    T = proj.shape[0]
    Q = SSM_CHUNK
    col = (2 * SSM_D_INNER + 2 * SSM_GROUPS * SSM_STATE) // LANE

    def kern(daa_ref, dab_ref, dda_ref, ddb_ref, dt_ref, raw_ref, b_ref, al_ref, into_ref, draw_ref, dal_ref, db_ref,
             acc_ref):
        i = pl.program_id(0)

        @pl.when(i == 0)
        def _():
            acc_ref[...] = jnp.zeros_like(acc_ref)
            db_ref[...] = jnp.zeros_like(db_ref)

        A = -jnp.exp(al_ref[...])
        ddA = _tri_matmul(_tri(Q, False), daa_ref[...] + dab_ref[...])
        ddt = dda_ref[...] + ddb_ref[...] + ddA * A
        acc_ref[...] += jnp.sum(ddA * dt_ref[...], axis=0, keepdims=True)
        lane = lax.broadcasted_iota(jnp.int32, (Q, LANE), 1)
        draw = jnp.where(lane < SSM_HEADS, ddt * _sigmoid(raw_ref[...] + b_ref[...]), 0.0)
        draw_ref[...] = draw.astype(BF16)
        db_ref[...] += jnp.sum(draw, axis=0, keepdims=True)
        dal_ref[...] = acc_ref[...] * A

    vec = pl.BlockSpec((1, LANE), lambda i: (0, 0))
    blk = pl.BlockSpec((Q, LANE), lambda i: (i, 0))
    return pl.pallas_call(
        kern, name=name, grid=(T // Q,),
        in_specs=[blk, blk, blk, blk, blk, pl.BlockSpec((Q, LANE), lambda i: (i, col)), vec, vec, _ANY],
        out_specs=[pl.BlockSpec((Q, LANE), lambda i: (i, col)), vec, vec],
        out_shape=[jax.ShapeDtypeStruct(into.shape, into.dtype), jax.ShapeDtypeStruct((1, LANE), F32),
                   jax.ShapeDtypeStruct((1, LANE), F32)],
        input_output_aliases={8: 0},
        scratch_shapes=[pltpu.VMEM((1, LANE), F32)],
        compiler_params=_params(("arbitrary",)),
    )(da_a, da_b, ddt_a, ddt_b, dt, proj, dt_bias, a_log, into)


def _conv_silu_fwd(proj, cw, cb, *, name):
    T = proj.shape[0]
    C = SSM_XBC
    bt = _pick(T, ROW_BLOCK, SUBLANE)
    bc = 1024
    off = SSM_D_INNER // bc

    def kern(x_ref, halo_ref, cw_ref, cb_ref, o_ref, ext_ref):
        i = pl.program_id(0)
        _fill_ext_past(ext_ref, halo_ref, x_ref[...], i, bt)
        pre = cb_ref[...] + _conv_past(ext_ref, cw_ref, SSM_CONV, bt)
        o_ref[...] = pre * _sigmoid(pre)

    return pl.pallas_call(
        kern, name=name, grid=(T // bt, C // bc),
        in_specs=[pl.BlockSpec((bt, bc), lambda i, j: (i, j + off)), _halo_prev(bt, bc, off),
                  pl.BlockSpec((SSM_CONV, bc), lambda i, j: (0, j)), pl.BlockSpec((1, bc), lambda i, j: (0, j))],
        out_specs=pl.BlockSpec((bt, bc), lambda i, j: (i, j)),
        out_shape=jax.ShapeDtypeStruct((T, C), F32),
        scratch_shapes=[pltpu.VMEM((bt + HALO, bc), F32)],
        compiler_params=_params(("parallel", "parallel")),
    )(proj, proj, cw, cb)


def _conv_silu_bwd(dxbc, proj, cw, cb, *, name):
    T = proj.shape[0]
    C = SSM_XBC
    K = SSM_CONV
    bt = _pick(T, ROW_BLOCK, SUBLANE)
    bc = 1024
    off = SSM_D_INNER // bc

    def kern(d_ref, x_ref, halo_ref, cw_ref, cb_ref, dpre_ref, dcb_ref, dcw_ref, ext_ref):
        i = pl.program_id(1)
        _fill_ext_past(ext_ref, halo_ref, x_ref[...], i, bt)
        pre = cb_ref[...] + _conv_past(ext_ref, cw_ref, K, bt)
        sg = _sigmoid(pre)
        dpre = d_ref[...] * sg * (1.0 + pre * (1.0 - sg))
        dpre_ref[...] = dpre

        @pl.when(i == 0)
        def _():
            dcb_ref[...] = jnp.zeros_like(dcb_ref)
            dcw_ref[...] = jnp.zeros_like(dcw_ref)

        dcb_ref[...] += jnp.sum(dpre, axis=0, keepdims=True)
        for k in range(K):
            dcw_ref[k:k + 1, :] += jnp.sum(dpre * ext_ref[pl.ds(HALO - (K - 1) + k, bt), :], axis=0, keepdims=True)

    blk = pl.BlockSpec((bt, bc), lambda j, i: (i, j))
    return pl.pallas_call(
        kern, name=name, grid=(C // bc, T // bt),
        in_specs=[blk, pl.BlockSpec((bt, bc), lambda j, i: (i, j + off)),
                  pl.BlockSpec((HALO, bc), lambda j, i: (jnp.maximum(i * (bt // HALO) - 1, 0), j + off)),
                  pl.BlockSpec((K, bc), lambda j, i: (0, j)), pl.BlockSpec((1, bc), lambda j, i: (0, j))],
        out_specs=[blk, pl.BlockSpec((1, bc), lambda j, i: (0, j)), pl.BlockSpec((K, bc), lambda j, i: (0, j))],
        out_shape=[jax.ShapeDtypeStruct((T, C), F32), jax.ShapeDtypeStruct((1, C), F32),
                   jax.ShapeDtypeStruct((K, C), F32)],
        scratch_shapes=[pltpu.VMEM((bt + HALO, bc), F32)],
        compiler_params=_params(("parallel", "arbitrary")),
    )(dxbc, proj, proj, cw, cb)


_GP = SSM_D_INNER // SSM_GROUPS
_HPG = SSM_HEADS // SSM_GROUPS
_PH = SSM_D_INNER // SSM_HEADS


def _head_masks(rows):
    lane = lax.broadcasted_iota(jnp.int32, (rows, _GP), 1)
    return [jnp.logical_and(lane >= r * _PH, lane < (r + 1) * _PH) for r in range(_HPG)]


def _ssd_cols(g):
    x0 = g * _GP
    b0 = SSM_D_INNER + g * SSM_STATE
    c0 = SSM_D_INNER + (SSM_GROUPS + g) * SSM_STATE
    return slice(x0, x0 + _GP), slice(b0, b0 + SSM_STATE), slice(c0, c0 + SSM_STATE)


def _ssd_specs(idx):
    Q, N = SSM_CHUNK, SSM_STATE
    return dict(
        xbc=pl.BlockSpec((Q, SSM_XBC), lambda j: (idx(j), 0)),
        x=pl.BlockSpec((Q, SSM_D_INNER), lambda j: (idx(j), 0)),
        col=pl.BlockSpec((Q, LANE), lambda j: (idx(j), 0)),
        row=pl.BlockSpec((SSM_HEADS, Q), lambda j: (0, idx(j))),
        st=pl.BlockSpec((N, SSM_D_INNER), lambda j: (idx(j), 0)),
    )


def _ssd_scan_fwd(xbc, dtc, acc_, dtr, acr, *, name):
    T = xbc.shape[0]
    Q, N = SSM_CHUNK, SSM_STATE
    nc = T // Q
    sp = _ssd_specs(lambda j: j)

    def kern(xbc_ref, dtc_ref, ac_ref, dtr_ref, ar_ref, ys_ref, st_ref, state_ref):
        @pl.when(pl.program_id(0) == 0)
        def _():
            state_ref[...] = jnp.zeros_like(state_ref)

        r_i = lax.broadcasted_iota(jnp.int32, (Q, Q), 0)
        c_i = lax.broadcasted_iota(jnp.int32, (Q, Q), 1)
        tri = c_i <= r_i
        masks = _head_masks(Q)
        masks1 = _head_masks(1)
        for g in range(SSM_GROUPS):
            xs, bs, cs = _ssd_cols(g)
            S = state_ref[g]
            st_ref[:, xs] = S
            x = xbc_ref[:, xs]
            xb = x.astype(BF16)
            Bb = xbc_ref[:, bs].astype(BF16)
            Cb = xbc_ref[:, cs].astype(BF16)
            CB = _dot(Cb, Bb, _NT)
            y = jnp.zeros((Q, _GP), F32)
            El = jnp.zeros((Q, _GP), F32)
            Wl = jnp.zeros((Q, _GP), F32)
            decl = jnp.zeros((1, _GP), F32)
            for r in range(_HPG):
                h = g * _HPG + r
                a_c = ac_ref[:, h:h + 1]
                a_r = ar_ref[h:h + 1, :]
                dt_c = dtc_ref[:, h:h + 1]
                dt_r = dtr_ref[h:h + 1, :]
                L = jnp.exp(jnp.where(tri, a_c - a_r, NEG))
                W = CB * L * dt_r
                y = jnp.where(masks[r], _dot(W.astype(BF16), xb), y)
                a_q = a_c[Q - 1:Q, :]
                El = jnp.where(masks[r], jnp.exp(a_c), El)
                Wl = jnp.where(masks[r], jnp.exp(a_q - a_c) * dt_c, Wl)
                decl = jnp.where(masks1[r], jnp.exp(a_q), decl)
            ys_ref[:, xs] = y + _dot(Cb, S.astype(BF16)) * El
            state_ref[g] = S * decl + _dot(Bb, (x * Wl).astype(BF16), _TN)

    return pl.pallas_call(
        kern, name=name, grid=(nc,),
        in_specs=[sp["xbc"], sp["col"], sp["col"], sp["row"], sp["row"]],
        out_specs=[sp["x"], sp["st"]],
        out_shape=[jax.ShapeDtypeStruct((T, SSM_D_INNER), F32), jax.ShapeDtypeStruct((nc * N, SSM_D_INNER), F32)],
        scratch_shapes=[pltpu.VMEM((SSM_GROUPS, N, _GP), F32)],
        compiler_params=_params(("arbitrary",)),
    )(xbc, dtc, acc_, dtr, acr)


def _ssd_scan_bwd(xbc, dys, dskip, st, dtc, acc_, dtr, acr, *, name):
    T = xbc.shape[0]
    Q, N = SSM_CHUNK, SSM_STATE
    nc = T // Q
    sp = _ssd_specs(lambda j: nc - 1 - j)

    def kern(xbc_ref, dy_ref, dsk_ref, st_ref, dtc_ref, ac_ref, dtr_ref, ar_ref,
             dxbc_ref, dac_ref, dar_ref, ddc_ref, ddr_ref, dstate_ref):
        @pl.when(pl.program_id(0) == 0)
        def _():
            dstate_ref[...] = jnp.zeros_like(dstate_ref)

        r_i = lax.broadcasted_iota(jnp.int32, (Q, Q), 0)
        c_i = lax.broadcasted_iota(jnp.int32, (Q, Q), 1)
        tri = c_i <= r_i
        last_row = lax.broadcasted_iota(jnp.int32, (Q, 1), 0) == Q - 1
        lane128 = lax.broadcasted_iota(jnp.int32, (Q, LANE), 1)
        masks = _head_masks(Q)
        masksN = _head_masks(N)
        masks1 = _head_masks(1)
        zeros = jnp.zeros((Q, _GP), F32)
        dacol = jnp.zeros((Q, LANE), F32)
        ddcol = jnp.zeros((Q, LANE), F32)
        for g in range(SSM_GROUPS):
            xs, bs, cs = _ssd_cols(g)
            dS = dstate_ref[g]
            dSb = dS.astype(BF16)
            S = st_ref[:, xs]
            Sb = S.astype(BF16)
            x = xbc_ref[:, xs]
            xb = x.astype(BF16)
            Bb = xbc_ref[:, bs].astype(BF16)
            Cb = xbc_ref[:, cs].astype(BF16)
            dy = dy_ref[:, xs]
            CB = _dot(Cb, Bb, _NT)
            BdS = _dot(Bb, dSb)
            hx = BdS * x
            yd = _dot(Cb, Sb) * dy
            dSS = dS * S
            dxi, El, Wl = zeros, zeros, zeros
            decl = jnp.zeros((1, _GP), F32)
            dBacc = jnp.zeros((Q, N), F32)
            dCacc = jnp.zeros((Q, N), F32)
            for r in range(_HPG):
                h = g * _HPG + r
                hm = masks[r]
                a_c = ac_ref[:, h:h + 1]
                a_r = ar_ref[h:h + 1, :]
                dt_c = dtc_ref[:, h:h + 1]
                dt_r = dtr_ref[h:h + 1, :]
                L = jnp.exp(jnp.where(tri, a_c - a_r, NEG))
                GL = CB * L
                W = GL * dt_r
                dym = jnp.where(hm, dy, 0.0).astype(BF16)
                dW = _dot(dym, xb, _NT)
                E = dW * W
                da_c = jnp.sum(E, axis=1, keepdims=True)
                dar_ref[h:h + 1, :] = -jnp.sum(E, axis=0, keepdims=True)
                ddr_ref[h:h + 1, :] = jnp.sum(dW * GL, axis=0, keepdims=True)
                dGb = (dW * L * dt_r).astype(BF16)
                dCacc = dCacc + _dot(dGb, Bb)
                dBacc = dBacc + _dot(dGb, Cb, _TN)
                dxi = dxi + _dot(W.astype(BF16), dym, _TN)
                a_q = a_c[Q - 1:Q, :]
                e_c = jnp.exp(a_c)
                eq_c = jnp.exp(a_q - a_c)
                w_c = eq_c * dt_c
                ydr = jnp.sum(jnp.where(hm, yd, 0.0), axis=1, keepdims=True) * e_c
                h_c = jnp.sum(jnp.where(hm, hx, 0.0), axis=1, keepdims=True)
                hw = h_c * w_c
                dss = jnp.sum(jnp.sum(jnp.where(masksN[r], dSS, 0.0), axis=1, keepdims=True), axis=0, keepdims=True)
                s_q = jnp.sum(hw, axis=0, keepdims=True) + jnp.exp(a_q) * dss
                da_c = da_c + ydr - hw + jnp.where(last_row, s_q, 0.0)
                dacol = jnp.where(lane128 == h, da_c, dacol)
                ddcol = jnp.where(lane128 == h, h_c * eq_c, ddcol)
                El = jnp.where(hm, e_c, El)
                Wl = jnp.where(hm, w_c, Wl)
                decl = jnp.where(masks1[r], jnp.exp(a_q), decl)
            dxbc_ref[:, xs] = dxi + BdS * Wl + dsk_ref[:, xs]
            dxbc_ref[:, bs] = dBacc + _dot((x * Wl).astype(BF16), dSb, _NT)
            dyE = (dy * El).astype(BF16)
            dxbc_ref[:, cs] = dCacc + _dot(dyE, Sb, _NT)
            dstate_ref[g] = dS * decl + _dot(Cb, dyE, _TN)
        dac_ref[...] = dacol
        ddc_ref[...] = ddcol

    return pl.pallas_call(
        kern, name=name, grid=(nc,),
        in_specs=[sp["xbc"], sp["x"], sp["x"], sp["st"], sp["col"], sp["col"], sp["row"], sp["row"]],
        out_specs=[sp["xbc"], sp["col"], sp["row"], sp["col"], sp["row"]],
        out_shape=[jax.ShapeDtypeStruct((T, SSM_XBC), F32),
                   jax.ShapeDtypeStruct((T, LANE), F32), jax.ShapeDtypeStruct((SSM_HEADS, T), F32),
                   jax.ShapeDtypeStruct((T, LANE), F32), jax.ShapeDtypeStruct((SSM_HEADS, T), F32)],
        scratch_shapes=[pltpu.VMEM((SSM_GROUPS, N, _GP), F32)],
        compiler_params=_params(("arbitrary",)),
    )(xbc, dys, dskip, st, dtc, acc_, dtr, acr)


def _gate_norm_fwd(ys, xbc, proj, d_exp, norm_w, *, name):
    T = ys.shape[0]
    bt = _pick(T, NARROW_ROW_BLOCK, SUBLANE)

    def kern(ys_ref, x_ref, z_ref, d_ref, w_ref, o_ref):
        z = z_ref[...]
        yz = (ys_ref[...] + d_ref[...] * x_ref[...]) * (z * _sigmoid(z))
        rstd = lax.rsqrt(jnp.mean(yz * yz, axis=-1, keepdims=True) + RMS_EPS)
        o_ref[...] = (yz * rstd * w_ref[...]).astype(BF16)

    blk = pl.BlockSpec((bt, _GP), lambda i, g: (i, g))
    vec = pl.BlockSpec((1, _GP), lambda i, g: (0, g))
    return pl.pallas_call(
        kern, name=name, grid=(T // bt, SSM_GROUPS), in_specs=[blk, blk, blk, vec, vec], out_specs=blk,
        out_shape=jax.ShapeDtypeStruct((T, SSM_D_INNER), BF16), compiler_params=_params(("parallel", "parallel")),
    )(ys, xbc, proj, d_exp, norm_w)


def _gate_norm_bwd(dyn, ys, xbc, proj, d_exp, norm_w, *, name):
    T = ys.shape[0]
    bt = _pick(T, NARROW_ROW_BLOCK, SUBLANE)

    def kern(dyn_ref, ys_ref, x_ref, z_ref, d_ref, w_ref, dz_ref, dys_ref, dsk_ref, dw_ref, dd_ref):
        i = pl.program_id(1)
        z = z_ref[...]
        x = x_ref[...]
        sg = _sigmoid(z)
        sz = z * sg
        y = ys_ref[...] + d_ref[...] * x
        yz = y * sz
        rstd = lax.rsqrt(jnp.mean(yz * yz, axis=-1, keepdims=True) + RMS_EPS)
        yhat = yz * rstd
        dynv = dyn_ref[...]
        gg = dynv * w_ref[...]
        dyz = rstd * (gg - yhat * jnp.mean(gg * yhat, axis=-1, keepdims=True))
        dy = dyz * sz
        dz_ref[...] = (dyz * y * sg * (1.0 + z * (1.0 - sg))).astype(BF16)
        dys_ref[...] = dy
        dsk_ref[...] = dy * d_ref[...]

        @pl.when(i == 0)
        def _():
            dw_ref[...] = jnp.zeros_like(dw_ref)
            dd_ref[...] = jnp.zeros_like(dd_ref)

        dw_ref[...] += jnp.sum(dynv * yhat, axis=0, keepdims=True)
        dd_ref[...] += jnp.sum(dy * x, axis=0, keepdims=True)

    blk = pl.BlockSpec((bt, _GP), lambda g, i: (i, g))
    vec = pl.BlockSpec((1, _GP), lambda g, i: (0, g))
    act = jax.ShapeDtypeStruct((T, SSM_D_INNER), F32)
    par = jax.ShapeDtypeStruct((1, SSM_D_INNER), F32)
    return pl.pallas_call(
        kern, name=name, grid=(SSM_GROUPS, T // bt), in_specs=[blk, blk, blk, blk, vec, vec],
        out_specs=[blk, blk, blk, vec, vec],
        out_shape=[jax.ShapeDtypeStruct((T, SSM_IN_PAD), BF16), act, act, par, par],
        compiler_params=_params(("parallel", "arbitrary")),
    )(dyn, ys, xbc, proj, d_exp, norm_w)


def _loss_head(y, target, *, name):
    T, D = y.shape
    bt = _pick(T, ROW_BLOCK, SUBLANE)

    def kern(y_ref, t_ref, l_ref, dy_ref):
        i = pl.program_id(0)
        err = y_ref[...] - t_ref[...]
        dy_ref[...] = err * (1.0 / D)

        @pl.when(i == 0)
        def _():
            l_ref[...] = jnp.zeros_like(l_ref)

        l_ref[...] += jnp.sum(err * err, axis=0, keepdims=True) * (0.5 / D)

    row = pl.BlockSpec((bt, D), lambda i: (i, 0))
    vec = pl.BlockSpec((1, D), lambda i: (0, 0))
    return pl.pallas_call(
        kern, name=name, grid=(T // bt,), in_specs=[row, row], out_specs=[vec, row],
        out_shape=[jax.ShapeDtypeStruct((1, D), F32), jax.ShapeDtypeStruct((T, D), F32)],
        compiler_params=_params(("arbitrary",)),
    )(y, target)


def _adamw(w, g, m, v, *, name):
    R, C = w.shape
    br = _pick(R, 512, SUBLANE)

    def kern(w_ref, g_ref, m_ref, v_ref, d_ref, nm_ref, nv_ref):
        gv = g_ref[...]
        nm = ADAM_B1 * m_ref[...] + (1.0 - ADAM_B1) * gv
        nv = ADAM_B2 * v_ref[...] + (1.0 - ADAM_B2) * (gv * gv)
        m_hat = nm / (1.0 - ADAM_B1 ** ADAM_STEP)
        v_hat = nv / (1.0 - ADAM_B2 ** ADAM_STEP)
        d_ref[...] = -ADAM_LR * (m_hat / (jnp.sqrt(v_hat) + ADAM_EPS) + ADAM_WD * w_ref[...])
        nm_ref[...] = nm
        nv_ref[...] = nv

    blk = pl.BlockSpec((br, C), lambda i: (i, 0))
    return pl.pallas_call(
        kern, name=name, grid=(R // br,), in_specs=[blk] * 4, out_specs=[blk] * 3,
        out_shape=[jax.ShapeDtypeStruct((R, C), F32)] * 3, compiler_params=_params(("parallel",)),
    )(w, g, m, v)


def _add2(a, b, out_dtype, *, name):
    shape = a.shape
    a2, b2 = a.reshape(-1, shape[-1]), b.reshape(-1, shape[-1])
    R, C = a2.shape
    br = _pick(R, 512, SUBLANE)

    def kern(a_ref, b_ref, o_ref):
        o_ref[...] = (a_ref[...] + b_ref[...]).astype(out_dtype)

    blk = pl.BlockSpec((br, C), lambda i: (i, 0))
    return pl.pallas_call(
        kern, name=name, grid=(R // br,), in_specs=[blk, blk], out_specs=blk,
        out_shape=jax.ShapeDtypeStruct((R, C), out_dtype), compiler_params=_params(("parallel",)),
    )(a2, b2).reshape(shape)


def _sum4(buf, *, name):
    _, R, C = buf.shape
    br = _pick(R, 512, SUBLANE)

    def kern(b_ref, o_ref):
        b = [b_ref[k].astype(F32) for k in range(4)]
        o_ref[...] = ((b[0] + b[1]) + b[2]) + b[3]

    return pl.pallas_call(
        kern, name=name, grid=(R // br,), in_specs=[pl.BlockSpec((4, br, C), lambda i: (0, i, 0))],
        out_specs=pl.BlockSpec((br, C), lambda i: (i, 0)),
        out_shape=jax.ShapeDtypeStruct((R, C), F32), compiler_params=_params(("parallel",)),
    )(buf)


def _place():
    x, y, c = lax.axis_index("x"), lax.axis_index("y"), lax.axis_index("c")
    other_chips = [(1 - x, y), (x, 1 - y), (1 - x, 1 - y)]
    return x, y, c, other_chips


def _gather_chips(w, *, name):
    R, C = w.shape
    H = R // 2

    def body(w_ref, out_ref, send_sems, recv_sems):
        x, y, c, chips = _place()
        me_chip = 2 * x + y
        sib = (x, y, 1 - c)

        def rows(chip, hc):
            return out_ref.at[chip, pl.ds(hc * H, H), :]

        def copy(k, blk, to, src=None):
            return pltpu.make_async_remote_copy(
                src_ref=blk if src is None else src, dst_ref=blk, send_sem=send_sems.at[k], recv_sem=recv_sems.at[k],
                device_id=to, device_id_type=MESH)

        first = [copy(j, rows(me_chip, c), (cx, cy, c), src=w_ref.at[pl.ds(c * H, H), :])
                 for j, (cx, cy) in enumerate(chips)]
        for cp in first:
            cp.start()
        passed = []
        for j, (cx, cy) in enumerate(chips):
            blk = rows(2 * cx + cy, c)
            copy(j, blk, (cx, cy, c)).wait_recv()
            fw = copy(3 + j, blk, sib)
            fw.start()
            passed.append(fw)
        for j, (cx, cy) in enumerate(chips):
            copy(3 + j, rows(2 * cx + cy, 1 - c), sib).wait_recv()
        for cp in first + passed:
            cp.wait_send()

    return pl.pallas_call(
        body, name=name, in_specs=[_ANY], out_specs=_ANY,
        out_shape=jax.ShapeDtypeStruct((4, R, C), w.dtype),
        scratch_shapes=[pltpu.SemaphoreType.DMA((6,)), pltpu.SemaphoreType.DMA((6,))],
    )(w)


def _pair_swap(v, *, name, other_half=False):
    shape = (v.shape[0], v.shape[1] // 2, v.shape[2]) if other_half else v.shape

    def body(v_ref, out_ref, send_sem, recv_sem):
        x, y, c, _ = _place()
        src = v_ref.at[:, pl.ds((1 - c) * shape[1], shape[1]), :] if other_half else v_ref
        cp = pltpu.make_async_remote_copy(src_ref=src, dst_ref=out_ref, send_sem=send_sem, recv_sem=recv_sem,
                                          device_id=(x, y, 1 - c), device_id_type=MESH)
        cp.start()
        cp.wait()

    return pl.pallas_call(
        body, name=name, in_specs=[_ANY], out_specs=_ANY, out_shape=jax.ShapeDtypeStruct(shape, v.dtype),
        scratch_shapes=[pltpu.SemaphoreType.DMA, pltpu.SemaphoreType.DMA],
    )(v)


def _chip_exchange(pv, *, name):
    def body(p_ref, out_ref, send_sems, recv_sems):
        x, y, c, chips = _place()
        me_chip = 2 * x + y
        sends = []
        for j, (cx, cy) in enumerate(chips):
            cp = pltpu.make_async_remote_copy(
                src_ref=p_ref.at[2 * cx + cy], dst_ref=out_ref.at[me_chip], send_sem=send_sems.at[j],
                recv_sem=recv_sems.at[j], device_id=(cx, cy, c), device_id_type=MESH)
            cp.start()
            sends.append(cp)
        for j, (cx, cy) in enumerate(chips):
            blk = out_ref.at[2 * cx + cy]
            pltpu.make_async_remote_copy(src_ref=blk, dst_ref=blk, send_sem=send_sems.at[j], recv_sem=recv_sems.at[j],
                                         device_id=(cx, cy, c), device_id_type=MESH).wait_recv()
        for cp in sends:
            cp.wait_send()

    return pl.pallas_call(
        body, name=name, in_specs=[_ANY], out_specs=_ANY, out_shape=jax.ShapeDtypeStruct(pv.shape, pv.dtype),
        scratch_shapes=[pltpu.SemaphoreType.DMA((3,)), pltpu.SemaphoreType.DMA((3,))],
    )(pv)


WEIGHTS = [
    ("attn_w_in", 2), ("attn_b_f", None), ("attn_w_out", 1), ("ssm_w_in", 2), ("ssm_conv_w", 2), ("ssm_conv_b", 1),
    ("ssm_dt_bias", None), ("ssm_A_log", None), ("ssm_D", None), ("ssm_norm_w", 1), ("ssm_w_out", 1),
    ("ln_mix_g", None), ("ln_mix_b", None), ("ffn_w_up", 2), ("ffn_conv_w", 2), ("ffn_conv_b", None),
    ("ffn_w_down", 1), ("ln_ffn_g", None), ("ln_ffn_b", None), ("ple_w_proj", 2), ("ple_w_gate", 1),
    ("ple_b_gate", None),
]
N_CHIPS = 4
MATMUL_WEIGHTS = ("attn_w_in", "attn_w_out", "ssm_w_in", "ssm_w_out", "ffn_w_up", "ffn_w_down", "ple_w_proj",
                  "ple_w_gate")


def _pack(arrays):
    parts = []
    total = 0
    for a in arrays:
        n = a.size
        pad = (-n) % PACK_COLS
        flat = a.reshape(-1)
        parts.append(jnp.pad(flat, (0, pad)) if pad else flat)
        total += n + pad
    rows = total // PACK_COLS
    rpad = (-rows) % PACK_ROW_ALIGN
    if rpad:
        parts.append(jnp.zeros((rpad * PACK_COLS,), arrays[0].dtype))
    return jnp.concatenate(parts).reshape(rows + rpad, PACK_COLS)


def _unpack(buf, shapes):
    flat = buf.reshape(-1)
    out = []
    off = 0
    for s in shapes:
        n = math.prod(s)
        out.append(flat[off:off + n].reshape(s))
        off += n + ((-n) % PACK_COLS)
    return out


def _from_row_layout(a):
    return jnp.pad(a.T, ((0, 0), (0, LANE - SSM_HEADS)))


def _pad_lanes(v, n=LANE):
    return jnp.pad(v, (0, n - v.shape[0])).reshape(1, n)


def _local_step(x, p, target, W):
    T = x.shape[0]
    row = lambda v: v.reshape(1, -1)
    attn_in = jnp.pad(W["attn_w_in"][0], ((0, 0), (0, ATTN_IN_PAD - W["attn_w_in"].shape[2])))
    ssm_in = jnp.pad(W["ssm_w_in"][0], ((0, 0), (0, SSM_IN_PAD - W["ssm_w_in"].shape[2])))
    bf = _pad_lanes(W["attn_b_f"][0])
    dt_bias = _pad_lanes(W["ssm_dt_bias"][0])
    a_log = _pad_lanes(W["ssm_A_log"][0])
    d_exp = jnp.repeat(W["ssm_D"][0], _PH).reshape(1, SSM_D_INNER)
    norm_w = row(W["ssm_norm_w"][0])
    G = {}

    def ffn_ple_fwd(i, xin, mix, tag):
        s = {}
        s["z1"], s["h1"], s["h1b"] = _ln_fwd(xin, mix, row(W["ln_mix_g"][i]), row(W["ln_mix_b"][i]),
                                             name=f"ln_mix_fwd{tag}")
        s["up"] = _mm(s["h1b"], W["ffn_w_up"][i], name=f"ffn_up{tag}")
        s["a"] = _ffn_act_fwd(s["up"], W["ffn_conv_w"][i], row(W["ffn_conv_b"][i]), name=f"ffn_act_fwd{tag}")
        ffn = _mm(s["a"], W["ffn_w_down"][i], name=f"ffn_down{tag}")
        s["z2"], s["h2"], s["h2b"] = _ln_fwd(s["h1"], ffn, row(W["ln_ffn_g"][i]), row(W["ln_ffn_b"][i]),
                                             name=f"ln_ffn_fwd{tag}")
        s["G"] = _mm(s["h2b"], W["ple_w_gate"][i], name=f"ple_gate_mm{tag}")
        s["pp"] = _mm(pb[i], W["ple_w_proj"][i], name=f"ple_proj_mm{tag}")
        out, outb = _ple_fwd(s["h2"], s["G"], row(W["ple_b_gate"][i]), s["pp"], name=f"ple_fwd{tag}")
        return out, outb, s

    def ffn_ple_bwd(i, dx, s, tag):
        g = {}
        dG, dpp, g["ple_b_gate"] = _ple_bwd(dx, s["G"], row(W["ple_b_gate"][i]), s["pp"], name=f"ple_bwd{tag}")
        g["ple_w_gate"] = _mm(s["h2b"], dG, ta=True, name=f"ple_gate_dw{tag}")
        g["ple_w_proj"] = _mm(pb[i], dpp, ta=True, name=f"ple_proj_dw{tag}")
        dh2 = _mm(dG, W["ple_w_gate"][i], tb=True, add=dx, name=f"ple_gate_dx{tag}")
        dz2, dz2b, g["ln_ffn_g"], g["ln_ffn_b"] = _ln_bwd(dh2, s["z2"], row(W["ln_ffn_g"][i]), name=f"ln_ffn_bwd{tag}")
        da = _mm(dz2b, W["ffn_w_down"][i], tb=True, out_dtype=BF16, name=f"ffn_down_dx{tag}")
        g["ffn_w_down"] = _mm(s["a"], dz2b, ta=True, name=f"ffn_down_dw{tag}")
        dup, dgc, g["ffn_conv_b"], g["ffn_conv_w"] = _ffn_act_bwd(
            da, s["up"], W["ffn_conv_w"][i], row(W["ffn_conv_b"][i]), name=f"ffn_act_bwd{tag}")
        dup = _dwconv_bwd_data(dgc, W["ffn_conv_w"][i], FFN_CONV, dup, FFN_DIM, name=f"ffn_conv_bwd{tag}")
        g["ffn_w_up"] = _mm(s["h1b"], dup, ta=True, name=f"ffn_up_dw{tag}")
        dh1 = _mm(dup, W["ffn_w_up"][i], tb=True, add=dz2, add_scale=DEEPNORM_ALPHA, name=f"ffn_up_dx{tag}")
        dz1, dz1b, g["ln_mix_g"], g["ln_mix_b"] = _ln_bwd(dh1, s["z1"], row(W["ln_mix_g"][i]), name=f"ln_mix_bwd{tag}")
        return dz1, dz1b, g

    xb = x.astype(BF16)
    pb = p.astype(BF16)
    proj0 = _mm(xb, attn_in, name="attn_in")
    c_col = _fox_gate_fwd(proj0, bf, name="fox_gate_fwd")
    cT = (c_col[:, :ATTN_HEADS] * LOG2E).T.reshape(HEAD_PAIRS, 2, T)
    qa, qb, kk, ka, kb, vv, va, vb = _attn_prep(proj0, name="attn_prep")
    o, ob, lsea, lseb = _attn_fwd(qa, qb, kk, va, vb, cT, name="attn_fwd")
    mix0 = _mm(ob, W["attn_w_out"][0], name="attn_out")
    x1, x1b, s0 = ffn_ple_fwd(0, x, mix0, "0")

    proj1 = _mm(x1b, ssm_in, name="ssm_in")
    dt, acum = _ssd_dt_fwd(proj1, dt_bias, a_log, name="ssd_dt_fwd")
    xbc = _conv_silu_fwd(proj1, W["ssm_conv_w"][0], row(W["ssm_conv_b"][0]), name="ssd_conv_fwd")
    dtr, acr = dt[:, :SSM_HEADS].T, acum[:, :SSM_HEADS].T
    ys, states = _ssd_scan_fwd(xbc, dt, acum, dtr, acr, name="ssd_scan_fwd")
    yn = _gate_norm_fwd(ys, xbc, proj1, d_exp, norm_w, name="ssd_gate_norm_fwd")
    mix1 = _mm(yn, W["ssm_w_out"][0], name="ssm_out")
    x2, _, s1 = ffn_ple_fwd(1, x1, mix1, "1")

    lpart, dy = _loss_head(x2, target, name="loss_head")
    loss = jnp.sum(lpart)

    dz1, dz1b, g1 = ffn_ple_bwd(1, dy, s1, "1")
    G["ssm_w_out"] = _mm(yn, dz1b, ta=True, name="ssm_out_dw")[None]
    dyn = _mm(dz1b, W["ssm_w_out"][0], tb=True, name="ssm_out_dx")
    dproj1, dys, dskip, dnw, dde = _gate_norm_bwd(dyn, ys, xbc, proj1, d_exp, norm_w, name="ssd_gate_norm_bwd")
    G["ssm_norm_w"] = dnw
    G["ssm_D"] = dde.reshape(SSM_HEADS, _PH).sum(axis=1)[None]
    dxbc, dac, dar, ddc, ddr = _ssd_scan_bwd(xbc, dys, dskip, states, dt, acum, dtr, acr, name="ssd_scan_bwd")
    dproj1, dal, ddb = _ssd_dt_bwd(dac, _from_row_layout(dar), ddc, _from_row_layout(ddr), dt, proj1, dt_bias, a_log,
                                   dproj1, name="ssd_dt_bwd")
    G["ssm_A_log"] = dal[:, :SSM_HEADS]
    G["ssm_dt_bias"] = ddb[:, :SSM_HEADS]
    dpre, G["ssm_conv_b"], dcw = _conv_silu_bwd(dxbc, proj1, W["ssm_conv_w"][0], row(W["ssm_conv_b"][0]),
                                                name="ssd_conv_bwd")
    G["ssm_conv_w"] = dcw[None]
    dproj1 = _dwconv_bwd_data(dpre, W["ssm_conv_w"][0], SSM_CONV, dproj1, SSM_D_INNER, name="ssd_conv_bwd_data")
    G["ssm_w_in"] = _mm(x1b, dproj1, ta=True, name="ssm_in_dw")[None, :, :W["ssm_w_in"].shape[2]]
    dx1 = _mm(dproj1, ssm_in, tb=True, add=dz1, add_scale=DEEPNORM_ALPHA, name="ssm_in_dx")

    dz0, dz0b, g0 = ffn_ple_bwd(0, dx1, s0, "0")
    G["attn_w_out"] = _mm(ob, dz0b, ta=True, name="attn_out_dw")[None]
    do = _mm(dz0b, W["attn_w_out"][0], tb=True, name="attn_out_dx")
    doa, dob, dlta, dltb = _attn_bwd_prep(do, o, name="attn_bwd_prep")
    dq, dk, dv, dcT, dcq = _attn_bwd(qa, qb, kk, ka, kb, vv, doa, dob, lsea, lseb, dlta, dltb, cT, name="attn_bwd")
    dcq = dcq.reshape(T, HEAD_PAIRS, 2, HALF)[:, :, :, 0].reshape(T, ATTN_HEADS)
    dc_col = jnp.pad(dcT.reshape(ATTN_HEADS, T).T + dcq, ((0, 0), (0, LANE - ATTN_HEADS)))
    dfl, dbf = _fox_gate_bwd(dc_col, proj0, bf, name="fox_gate_bwd")
    G["attn_b_f"] = dbf[:, :ATTN_HEADS]
    dproj0 = jnp.concatenate([dq.astype(BF16), dk.astype(BF16), dv.astype(BF16), dfl.astype(BF16)], axis=1)
    G["attn_w_in"] = _mm(xb, dproj0, ta=True, name="attn_in_dw")[None, :, :W["attn_w_in"].shape[2]]
    grad_x = _mm(dproj0, attn_in, tb=True, add=dz0, add_scale=DEEPNORM_ALPHA, name="attn_in_dx")

    for k in g0:
        G[k] = jnp.stack([g0[k].reshape(W[k].shape[1:]), g1[k].reshape(W[k].shape[1:])])
    return loss, grad_x, G


def kernel(x, p, attn_w_in, attn_b_f, attn_w_out, ssm_w_in, ssm_conv_w, ssm_conv_b, ssm_dt_bias, ssm_A_log, ssm_D, ssm_norm_w, ssm_w_out, ln_mix_g, ln_mix_b, ffn_w_up, ffn_conv_w, ffn_conv_b, ffn_w_down, ln_ffn_g, ln_ffn_b, ple_w_proj, ple_w_gate, ple_b_gate, loss_target, m_attn_w_in, m_attn_b_f, m_attn_w_out, m_ssm_w_in, m_ssm_conv_w, m_ssm_conv_b, m_ssm_dt_bias, m_ssm_A_log, m_ssm_D, m_ssm_norm_w, m_ssm_w_out, m_ln_mix_g, m_ln_mix_b, m_ffn_w_up, m_ffn_conv_w, m_ffn_conv_b, m_ffn_w_down, m_ln_ffn_g, m_ln_ffn_b, m_ple_w_proj, m_ple_w_gate, m_ple_b_gate, v_attn_w_in, v_attn_b_f, v_attn_w_out, v_ssm_w_in, v_ssm_conv_w, v_ssm_conv_b, v_ssm_dt_bias, v_ssm_A_log, v_ssm_D, v_ssm_norm_w, v_ssm_w_out, v_ln_mix_g, v_ln_mix_b, v_ffn_w_up, v_ffn_conv_w, v_ffn_conv_b, v_ffn_w_down, v_ln_ffn_g, v_ln_ffn_b, v_ple_w_proj, v_ple_w_gate, v_ple_b_gate):
    names = [n for n, _ in WEIGHTS]
    axes = dict(WEIGHTS)
    w_loc = dict(zip(names, [attn_w_in, attn_b_f, attn_w_out, ssm_w_in, ssm_conv_w, ssm_conv_b, ssm_dt_bias, ssm_A_log, ssm_D, ssm_norm_w, ssm_w_out, ln_mix_g, ln_mix_b, ffn_w_up, ffn_conv_w, ffn_conv_b, ffn_w_down, ln_ffn_g, ln_ffn_b, ple_w_proj, ple_w_gate, ple_b_gate]))
    m_loc = dict(zip(names, [m_attn_w_in, m_attn_b_f, m_attn_w_out, m_ssm_w_in, m_ssm_conv_w, m_ssm_conv_b, m_ssm_dt_bias, m_ssm_A_log, m_ssm_D, m_ssm_norm_w, m_ssm_w_out, m_ln_mix_g, m_ln_mix_b, m_ffn_w_up, m_ffn_conv_w, m_ffn_conv_b, m_ffn_w_down, m_ln_ffn_g, m_ln_ffn_b, m_ple_w_proj, m_ple_w_gate, m_ple_b_gate]))
    v_loc = dict(zip(names, [v_attn_w_in, v_attn_b_f, v_attn_w_out, v_ssm_w_in, v_ssm_conv_w, v_ssm_conv_b, v_ssm_dt_bias, v_ssm_A_log, v_ssm_D, v_ssm_norm_w, v_ssm_w_out, v_ln_mix_g, v_ln_mix_b, v_ffn_w_up, v_ffn_conv_w, v_ffn_conv_b, v_ffn_w_down, v_ln_ffn_g, v_ln_ffn_b, v_ple_w_proj, v_ple_w_gate, v_ple_b_gate]))
    sharded = [n for n in names if axes[n] is not None]
    matrices = [n for n in sharded if n in MATMUL_WEIGHTS]

    def wire(n):
        if n in matrices:
            return w_loc[n].astype(BF16)
        return lax.bitcast_convert_type(w_loc[n], BF16)

    wired = [wire(n) for n in sharded]
    me_chip = 2 * lax.axis_index("x") + lax.axis_index("y")
    packed = _pack(wired)
    gathered = lax.dynamic_update_index_in_dim(_gather_chips(packed, name="gather_weights"), packed, me_chip, 0)
    W = dict(w_loc)
    per_chip = [_unpack(gathered[k], [w.shape for w in wired]) for k in range(N_CHIPS)]
    for i, n in enumerate(sharded):
        pieces = [per_chip[k][i] for k in range(N_CHIPS)]
        if n not in matrices:
            pieces = [lax.bitcast_convert_type(q, F32) for q in pieces]
        W[n] = jnp.concatenate(pieces, axis=axes[n])

    loss, grad_x, G = _local_step(x[0], p[:, 0], loss_target[0], W)
    loss = lax.psum(loss, ("x", "y", "c"))

    def slot(k):
        parts = []
        for n in names:
            g = G[n].reshape(W[n].shape)
            if axes[n] is not None:
                size = w_loc[n].shape[axes[n]]
                g = lax.slice_in_dim(g, k * size, (k + 1) * size, axis=axes[n])
            parts.append(g)
        return _pack(parts)

    contrib = jnp.stack([slot(k) for k in range(N_CHIPS)])
    R = contrib.shape[1]
    H = R // 2
    c = lax.axis_index("c")
    keep = lax.dynamic_slice_in_dim(contrib, c * H, H, axis=1)
    pair = _add2(keep, _pair_swap(contrib, other_half=True, name="grad_pair_swap"), BF16, name="grad_pair_sum")
    from_chips = lax.dynamic_update_index_in_dim(
        _chip_exchange(pair, name="grad_chip_exchange"), lax.dynamic_index_in_dim(pair, me_chip, 0, keepdims=False),
        me_chip, 0)
    half = _sum4(from_chips, name="grad_chip_sum")
    other = _pair_swap(half, name="grad_half_swap")
    gflat = jnp.concatenate([jnp.where(c == 0, half, other), jnp.where(c == 0, other, half)])

    shapes = [w_loc[n].shape for n in names]
    delta, new_m, new_v = _adamw(_pack([w_loc[n] for n in names]), gflat, _pack([m_loc[n] for n in names]),
                                 _pack([v_loc[n] for n in names]), name="adamw")
    return (loss, grad_x[None], *_unpack(gflat, shapes), *_unpack(delta, shapes), *_unpack(new_m, shapes),
            *_unpack(new_v, shapes))
```

```python
import functools
import math

import jax
import jax.numpy as jnp
from jax import lax
from jax.experimental import pallas as pl
from jax.experimental.pallas import tpu as pltpu

F32 = jnp.float32
BF16 = jnp.bfloat16
MESH = pl.DeviceIdType.MESH

D_MODEL = 1024
ATTN_HEADS = 16
HEAD_PAIRS = ATTN_HEADS // 2
SSM_D_INNER = 2048
SSM_HEADS = 32
SSM_GROUPS = 8
SSM_STATE = 128
SSM_CONV = 4
SSM_CHUNK = 128
SSM_XBC = SSM_D_INNER + 2 * SSM_GROUPS * SSM_STATE
FFN_DIM = 2816
FFN_CONV = 3
DEPTH = 2
LN_EPS = 1e-5
RMS_EPS = 1e-5
DEEPNORM_ALPHA = (2 * DEPTH) ** 0.25
ADAM_LR = 0.001
ADAM_B1 = 0.9
ADAM_B2 = 0.999
ADAM_EPS = 1e-08
ADAM_WD = 0.01
ADAM_STEP = 10

LANE = 128
SUBLANE = 8
HALO = SUBLANE
NEG = -1e30
ATTN_IN_PAD = 3 * D_MODEL + LANE
SSM_IN_PAD = 2 * SSM_D_INNER + 2 * SSM_GROUPS * SSM_STATE + LANE
PACK_COLS = 1024
PACK_ROW_ALIGN = 512

ATTN_BLOCK = 1024
ROW_BLOCK = 512
NARROW_ROW_BLOCK = 1024
CUM_BLOCK = 256


def _params(sem, vmem_mb=48):
    return pltpu.CompilerParams(dimension_semantics=sem, vmem_limit_bytes=vmem_mb * 2 ** 20)


def _pick(n, target, mult=LANE):
    best = None
    d = mult
    while d <= min(n, target):
        if n % d == 0:
            best = d
        d += mult
    return n if best is None else best


def _sigmoid(x):
    return 1.0 / (1.0 + jnp.exp(-x))


def _log1p(u):
    w = 1.0 + u
    return jnp.where(w == 1.0, u, jnp.log(w) * (u / (w - 1.0)))


def _softplus(x):
    return jnp.maximum(x, 0.0) + _log1p(jnp.exp(-jnp.abs(x)))


def _split3(x):
    hi = x.astype(BF16)
    r1 = x - hi.astype(F32)
    mid = r1.astype(BF16)
    lo = (r1 - mid.astype(F32)).astype(BF16)
    return hi, mid, lo


def _tri_matmul(tri, x):
    out = None
    for part in _split3(x):
        t = jnp.dot(tri, part, preferred_element_type=F32)
        out = t if out is None else out + t
    return out


def _tri(n, lower):
    r = lax.broadcasted_iota(jnp.int32, (n, n), 0)
    c = lax.broadcasted_iota(jnp.int32, (n, n), 1)
    return jnp.where((c <= r) if lower else (c >= r), 1.0, 0.0).astype(BF16)


_ANY = pl.BlockSpec(memory_space=pl.ANY)
MM_OUT_BLOCK_BYTES = 13 * 2 ** 20
MM_IN_BLOCK_BYTES = 6 * 2 ** 20
MM_WIDE_K = 3200


def _mm(a, b, *, name, ta=False, tb=False, add=None, add_scale=1.0, out_dtype=F32):
    if ta:
        K, M = a.shape
    else:
        M, K = a.shape
    if tb:
        N, Kb = b.shape
    else:
        Kb, N = b.shape
    assert K == Kb, (a.shape, b.shape, ta, tb)
    if ta:
        assert add is None and out_dtype == F32
        bm = _pick(M, 2816)
        bn = _pick(N, MM_OUT_BLOCK_BYTES // (4 * bm))
        bk = _pick(K, max(512, MM_IN_BLOCK_BYTES // (2 * max(bm, bn))))
    else:
        bm = _pick(M, 1024)
        bn = _pick(N, 1536 if K <= MM_WIDE_K else 512)
        bk = K
    nk = K // bk
    a_spec = pl.BlockSpec((bk, bm), lambda i, j, k: (k, i)) if ta else pl.BlockSpec((bm, bk), lambda i, j, k: (i, k))
    b_spec = pl.BlockSpec((bn, bk), lambda i, j, k: (j, k)) if tb else pl.BlockSpec((bk, bn), lambda i, j, k: (k, j))
    o_spec = pl.BlockSpec((bm, bn), lambda i, j, k: (i, j))
    dims = (((0 if ta else 1,), (1 if tb else 0,)), ((), ()))
    has_add = add is not None

    def kern(*refs):
        a_ref, b_ref = refs[0], refs[1]
        add_ref = refs[2] if has_add else None
        o_ref = refs[3] if has_add else refs[2]
        k = pl.program_id(2)
        part = lax.dot_general(a_ref[...].astype(BF16), b_ref[...].astype(BF16), dims, preferred_element_type=F32)
        if nk == 1:
            o_ref[...] = (part + add_scale * add_ref[...] if has_add else part).astype(out_dtype)
        else:
            @pl.when(k == 0)
            def _():
                o_ref[...] = part

            @pl.when(k > 0)
            def _():
                o_ref[...] += part

    ins = [a, b] + ([add] if has_add else [])
    in_specs = [a_spec, b_spec] + ([o_spec] if has_add else [])
    return pl.pallas_call(
        kern, name=name, grid=(M // bm, N // bn, nk),
        in_specs=in_specs, out_specs=o_spec,
        out_shape=jax.ShapeDtypeStruct((M, N), out_dtype),
        compiler_params=_params(("parallel", "parallel", "arbitrary"), vmem_mb=56),
    )(*ins)


def _ln_stats(z):
    mu = jnp.mean(z, axis=-1, keepdims=True)
    zc = z - mu
    var = jnp.mean(zc * zc, axis=-1, keepdims=True)
    return zc, lax.rsqrt(var + LN_EPS)


def _ln_fwd(x, r, g, b, *, name):
    T, D = x.shape
    bt = _pick(T, ROW_BLOCK, SUBLANE)

    def kern(x_ref, r_ref, g_ref, b_ref, z_ref, h_ref, hb_ref):
        z = DEEPNORM_ALPHA * x_ref[...] + r_ref[...]
        zc, rstd = _ln_stats(z)
        h = zc * rstd * g_ref[...] + b_ref[...]
        z_ref[...] = z
        h_ref[...] = h
        hb_ref[...] = h.astype(BF16)

    row = pl.BlockSpec((bt, D), lambda i: (i, 0))
    vec = pl.BlockSpec((1, D), lambda i: (0, 0))
    return pl.pallas_call(
        kern, name=name, grid=(T // bt,), in_specs=[row, row, vec, vec], out_specs=[row, row, row],
        out_shape=[jax.ShapeDtypeStruct((T, D), F32)] * 2 + [jax.ShapeDtypeStruct((T, D), BF16)],
        compiler_params=_params(("parallel",)),
    )(x, r, g, b)


def _ln_bwd(dy, z, g, *, name):
    T, D = z.shape
    bt = _pick(T, ROW_BLOCK, SUBLANE)

    def kern(dy_ref, z_ref, g_ref, dz_ref, dzb_ref, dg_ref, db_ref):
        i = pl.program_id(0)
        zc, rstd = _ln_stats(z_ref[...])
        xhat = zc * rstd
        dyv = dy_ref[...]
        dxh = dyv * g_ref[...]
        m1 = jnp.mean(dxh, axis=-1, keepdims=True)
        m2 = jnp.mean(dxh * xhat, axis=-1, keepdims=True)
        dz = rstd * (dxh - m1 - xhat * m2)
        dz_ref[...] = dz
        dzb_ref[...] = dz.astype(BF16)

        @pl.when(i == 0)
        def _():
            dg_ref[...] = jnp.zeros_like(dg_ref)
            db_ref[...] = jnp.zeros_like(db_ref)

        dg_ref[...] += jnp.sum(dyv * xhat, axis=0, keepdims=True)
        db_ref[...] += jnp.sum(dyv, axis=0, keepdims=True)

    row = pl.BlockSpec((bt, D), lambda i: (i, 0))
    vec = pl.BlockSpec((1, D), lambda i: (0, 0))
    return pl.pallas_call(
        kern, name=name, grid=(T // bt,), in_specs=[row, row, vec], out_specs=[row, row, vec, vec],
        out_shape=[jax.ShapeDtypeStruct((T, D), F32), jax.ShapeDtypeStruct((T, D), BF16),
                   jax.ShapeDtypeStruct((1, D), F32), jax.ShapeDtypeStruct((1, D), F32)],
        compiler_params=_params(("arbitrary",)),
    )(dy, z, g)


def _past_taps(ext_ref, K, bt):
    ext = ext_ref[...]
    return [(ext if k == K - 1 else pltpu.roll(ext, K - 1 - k, 0))[HALO:HALO + bt] for k in range(K)]


def _conv_past(taps, cw_ref):
    out = None
    for k, tap in enumerate(taps):
        term = cw_ref[k:k + 1, :] * tap
        out = term if out is None else out + term
    return out


def _fill_ext_past(ext_ref, halo_ref, cur, i, bt):
    ext_ref[pl.ds(0, HALO), :] = jnp.where(i > 0, halo_ref[...], 0.0)
    ext_ref[pl.ds(HALO, bt), :] = cur


def _halo_prev(bt, bc, off):
    return pl.BlockSpec((HALO, bc), lambda i, j: (jnp.maximum(i * (bt // HALO) - 1, 0), j + off))


def _normal_cdf(x):
    return 0.5 * (1.0 + lax.erf(x * (1.0 / math.sqrt(2.0))))


def _gelu(x):
    return x * _normal_cdf(x)


def _gelu_and_grad(x):
    cdf = _normal_cdf(x)
    return x * cdf, cdf + x * jnp.exp(-0.5 * x * x) * (1.0 / math.sqrt(2.0 * math.pi))


def _ffn_act_fwd(up, cw, cb, *, name):
    T, F2 = up.shape
    F = F2 // 2
    bt = _pick(T, ROW_BLOCK, SUBLANE)
    bc = _pick(F, 1408)
    nb = F // bc

    def kern(u_ref, g_ref, halo_ref, cw_ref, cb_ref, a_ref, ext_ref):
        i = pl.program_id(0)
        _fill_ext_past(ext_ref, halo_ref, g_ref[...], i, bt)
        gc = cb_ref[...] + _conv_past(_past_taps(ext_ref, FFN_CONV, bt), cw_ref)
        a_ref[...] = (_gelu(gc) * u_ref[...]).astype(BF16)

    return pl.pallas_call(
        kern, name=name, grid=(T // bt, nb),
        in_specs=[pl.BlockSpec((bt, bc), lambda i, j: (i, j)),
                  pl.BlockSpec((bt, bc), lambda i, j: (i, j + nb)),
                  _halo_prev(bt, bc, nb),
                  pl.BlockSpec((FFN_CONV, bc), lambda i, j: (0, j)),
                  pl.BlockSpec((1, bc), lambda i, j: (0, j))],
        out_specs=pl.BlockSpec((bt, bc), lambda i, j: (i, j)),
        out_shape=jax.ShapeDtypeStruct((T, F), BF16),
        scratch_shapes=[pltpu.VMEM((bt + HALO, bc), F32)],
        compiler_params=_params(("parallel", "parallel")),
    )(up, up, up, cw, cb)


def _ffn_act_bwd(da, up, cw, cb, *, name):
    T, F2 = up.shape
    F = F2 // 2
    bt = _pick(T, ROW_BLOCK, SUBLANE)
    bc = _pick(F, 1408)
    nb = F // bc
    K = FFN_CONV

    def kern(da_ref, u_ref, g_ref, halo_ref, cw_ref, cb_ref, du_ref, dgc_ref, dcb_ref, dcw_ref, ext_ref):
        i = pl.program_id(1)
        _fill_ext_past(ext_ref, halo_ref, g_ref[...], i, bt)
        taps = _past_taps(ext_ref, K, bt)
        gc = cb_ref[...] + _conv_past(taps, cw_ref)
        dav = da_ref[...]
        act, act_grad = _gelu_and_grad(gc)
        du_ref[...] = (dav * act).astype(BF16)
        dgc = dav * u_ref[...] * act_grad
        dgc_ref[...] = dgc

        @pl.when(i == 0)
        def _():
            dcb_ref[...] = jnp.zeros_like(dcb_ref)
            dcw_ref[...] = jnp.zeros_like(dcw_ref)

        dcb_ref[...] += jnp.sum(dgc, axis=0, keepdims=True)
        for k in range(K):
            dcw_ref[k:k + 1, :] += jnp.sum(dgc * taps[k], axis=0, keepdims=True)

    blk = pl.BlockSpec((bt, bc), lambda j, i: (i, j))
    return pl.pallas_call(
        kern, name=name, grid=(nb, T // bt),
        in_specs=[blk, blk,
                  pl.BlockSpec((bt, bc), lambda j, i: (i, j + nb)),
                  pl.BlockSpec((HALO, bc), lambda j, i: (jnp.maximum(i * (bt // HALO) - 1, 0), j + nb)),
                  pl.BlockSpec((K, bc), lambda j, i: (0, j)),
                  pl.BlockSpec((1, bc), lambda j, i: (0, j))],
        out_specs=[blk, blk, pl.BlockSpec((1, bc), lambda j, i: (0, j)), pl.BlockSpec((K, bc), lambda j, i: (0, j))],
        out_shape=[jax.ShapeDtypeStruct((T, F2), BF16), jax.ShapeDtypeStruct((T, F), F32),
                   jax.ShapeDtypeStruct((1, F), F32), jax.ShapeDtypeStruct((K, F), F32)],
        scratch_shapes=[pltpu.VMEM((bt + HALO, bc), F32)],
        compiler_params=_params(("parallel", "arbitrary")),
    )(da, up, up, up, cw, cb)


def _dwconv_bwd_data(dgc, cw, K, into, col, *, name):
    T, C = dgc.shape
    bt = _pick(T, ROW_BLOCK, SUBLANE)
    bc = _pick(C, 1408)
    nt = T // bt
    last_halo = T // HALO - 1
    off = col // bc
    assert off * bc == col

    def kern(d_ref, halo_ref, cw_ref, into_ref, o_ref, ext_ref):
        i = pl.program_id(0)
        ext_ref[pl.ds(0, bt), :] = d_ref[...]
        ext_ref[pl.ds(bt, HALO), :] = jnp.where(i < nt - 1, halo_ref[...], 0.0)
        ext = ext_ref[...]
        out = None
        for k in range(K):
            ahead = K - 1 - k
            tap = (ext if ahead == 0 else pltpu.roll(ext, bt + HALO - ahead, 0))[0:bt]
            term = cw_ref[k:k + 1, :] * tap
            out = term if out is None else out + term
        o_ref[...] = out.astype(o_ref.dtype)

    return pl.pallas_call(
        kern, name=name, grid=(nt, C // bc),
        in_specs=[pl.BlockSpec((bt, bc), lambda i, j: (i, j)),
                  pl.BlockSpec((HALO, bc), lambda i, j: (jnp.minimum((i + 1) * (bt // HALO), last_halo), j)),
                  pl.BlockSpec((K, bc), lambda i, j: (0, j)), _ANY],
        out_specs=pl.BlockSpec((bt, bc), lambda i, j: (i, j + off)),
        out_shape=jax.ShapeDtypeStruct(into.shape, into.dtype), input_output_aliases={3: 0},
        scratch_shapes=[pltpu.VMEM((bt + HALO, bc), F32)],
        compiler_params=_params(("parallel", "parallel")),
    )(dgc, dgc, cw, into)


def _ple_fwd(h, G, bg, pp, *, name):
    T, D = h.shape
    bt = _pick(T, ROW_BLOCK, SUBLANE)

    def kern(h_ref, G_ref, bg_ref, pp_ref, o_ref, ob_ref):
        out = h_ref[...] + _sigmoid(G_ref[...] + bg_ref[...]) * pp_ref[...]
        o_ref[...] = out
        ob_ref[...] = out.astype(BF16)

    row = pl.BlockSpec((bt, D), lambda i: (i, 0))
    vec = pl.BlockSpec((1, D), lambda i: (0, 0))
    return pl.pallas_call(
        kern, name=name, grid=(T // bt,), in_specs=[row, row, vec, row], out_specs=[row, row],
        out_shape=[jax.ShapeDtypeStruct((T, D), F32), jax.ShapeDtypeStruct((T, D), BF16)],
        compiler_params=_params(("parallel",)),
    )(h, G, bg, pp)


def _ple_bwd(dx, G, bg, pp, *, name):
    T, D = dx.shape
    bt = _pick(T, ROW_BLOCK, SUBLANE)

    def kern(dx_ref, G_ref, bg_ref, pp_ref, dG_ref, dpp_ref, dbg_ref):
        i = pl.program_id(0)
        gate = _sigmoid(G_ref[...] + bg_ref[...])
        dxv = dx_ref[...]
        dG = dxv * pp_ref[...] * gate * (1.0 - gate)
        dG_ref[...] = dG.astype(BF16)
        dpp_ref[...] = (dxv * gate).astype(BF16)

        @pl.when(i == 0)
        def _():
            dbg_ref[...] = jnp.zeros_like(dbg_ref)

        dbg_ref[...] += jnp.sum(dG, axis=0, keepdims=True)

    row = pl.BlockSpec((bt, D), lambda i: (i, 0))
    vec = pl.BlockSpec((1, D), lambda i: (0, 0))
    return pl.pallas_call(
        kern, name=name, grid=(T // bt,), in_specs=[row, row, vec, row], out_specs=[row, row, vec],
        out_shape=[jax.ShapeDtypeStruct((T, D), BF16), jax.ShapeDtypeStruct((T, D), BF16),
                   jax.ShapeDtypeStruct((1, D), F32)],
        compiler_params=_params(("arbitrary",)),
    )(dx, G, bg, pp)


def _fox_gate_fwd(proj, bf, *, name):
    T = proj.shape[0]
    bt = _pick(T, CUM_BLOCK, SUBLANE)
    fcol = 3 * D_MODEL // LANE

    def kern(f_ref, bf_ref, c_ref, carry_ref):
        i = pl.program_id(0)

        @pl.when(i == 0)
        def _():
            carry_ref[...] = jnp.zeros_like(carry_ref)

        x = f_ref[...] + bf_ref[...]
        lf = jnp.minimum(x, 0.0) - _log1p(jnp.exp(-jnp.abs(x)))
        cs = _tri_matmul(_tri(bt, True), lf) + carry_ref[...]
        c_ref[...] = cs
        carry_ref[...] = cs[bt - 1:bt, :]

    return pl.pallas_call(
        kern, name=name, grid=(T // bt,),
        in_specs=[pl.BlockSpec((bt, LANE), lambda i: (i, fcol)), pl.BlockSpec((1, LANE), lambda i: (0, 0))],
        out_specs=pl.BlockSpec((bt, LANE), lambda i: (i, 0)),
        out_shape=jax.ShapeDtypeStruct((T, LANE), F32),
        scratch_shapes=[pltpu.VMEM((1, LANE), F32)],
        compiler_params=_params(("arbitrary",)),
    )(proj, bf)


def _fox_gate_bwd(dc, proj, bf, *, name):
    T = proj.shape[0]
    bt = _pick(T, CUM_BLOCK, SUBLANE)
    nb = T // bt
    fcol = 3 * D_MODEL // LANE

    def kern(dc_ref, f_ref, bf_ref, df_ref, dbf_ref, carry_ref):
        i = pl.program_id(0)

        @pl.when(i == 0)
        def _():
            carry_ref[...] = jnp.zeros_like(carry_ref)
            dbf_ref[...] = jnp.zeros_like(dbf_ref)

        dlf = _tri_matmul(_tri(bt, False), dc_ref[...]) + carry_ref[...]
        carry_ref[...] = dlf[0:1, :]
        x = f_ref[...] + bf_ref[...]
        lane = lax.broadcasted_iota(jnp.int32, (bt, LANE), 1)
        df = jnp.where(lane < ATTN_HEADS, dlf / (1.0 + jnp.exp(x)), 0.0)
        df_ref[...] = df
        dbf_ref[...] += jnp.sum(df, axis=0, keepdims=True)

    return pl.pallas_call(
        kern, name=name, grid=(nb,),
        in_specs=[pl.BlockSpec((bt, LANE), lambda i: (nb - 1 - i, 0)),
                  pl.BlockSpec((bt, LANE), lambda i: (nb - 1 - i, fcol)),
                  pl.BlockSpec((1, LANE), lambda i: (0, 0))],
        out_specs=[pl.BlockSpec((bt, LANE), lambda i: (nb - 1 - i, 0)), pl.BlockSpec((1, LANE), lambda i: (0, 0))],
        out_shape=[jax.ShapeDtypeStruct((T, LANE), F32), jax.ShapeDtypeStruct((1, LANE), F32)],
        scratch_shapes=[pltpu.VMEM((1, LANE), F32)],
        compiler_params=_params(("arbitrary",)),
    )(dc, proj, bf)


_NT = (((1,), (1,)), ((), ()))
_TN = (((0,), (0,)), ((), ()))


def _dot(a, b, dims=None):
    if dims is None:
        return jnp.dot(a, b, preferred_element_type=F32)
    return lax.dot_general(a, b, dims, preferred_element_type=F32)


LOG2E = 1.0 / math.log(2.0)
LN2 = math.log(2.0)
Q_SCALE = 0.125 * LOG2E
HALF = LANE // 2
L_LANE = (HALF, 0)


def _attn_prep(proj, *, name):
    T = proj.shape[0]
    bt = _pick(T, ATTN_BLOCK)

    def kern(q_ref, k_ref, v_ref, qa_ref, qb_ref, kk_ref, ka_ref, kb_ref, vv_ref, va_ref, vb_ref):
        lane = lax.broadcasted_iota(jnp.int32, (bt, LANE), 1)
        lo = lane < HALF
        q = q_ref[...] * Q_SCALE
        k = k_ref[...]
        v = v_ref[...]
        qa_ref[...] = jnp.where(lo, q, 0.0).astype(BF16)
        qb_ref[...] = jnp.where(lo, 0.0, q).astype(BF16)
        kk_ref[...] = k.astype(BF16)
        ka_ref[...] = jnp.where(lo, k, 0.0).astype(BF16)
        kb_ref[...] = jnp.where(lo, 0.0, k).astype(BF16)
        vv_ref[...] = v.astype(BF16)
        va_ref[...] = jnp.where(lo, v, jnp.where(lane == L_LANE[0], 1.0, 0.0)).astype(BF16)
        vb_ref[...] = jnp.where(lo, jnp.where(lane == L_LANE[1], 1.0, 0.0), v).astype(BF16)

    kcol, vcol = D_MODEL // LANE, 2 * D_MODEL // LANE
    out = pl.BlockSpec((bt, LANE), lambda i, hp: (i, hp))
    return pl.pallas_call(
        kern, name=name, grid=(T // bt, HEAD_PAIRS),
        in_specs=[out, pl.BlockSpec((bt, LANE), lambda i, hp: (i, kcol + hp)),
                  pl.BlockSpec((bt, LANE), lambda i, hp: (i, vcol + hp))],
        out_specs=[out] * 8, out_shape=[jax.ShapeDtypeStruct((T, D_MODEL), BF16)] * 8,
        compiler_params=_params(("parallel", "parallel")),
    )(proj, proj, proj)


def _attn_fwd(qa, qb, kk, va, vb, cT, *, name):
    T = qa.shape[0]
    tb = _pick(T, ATTN_BLOCK)
    nq = T // tb
    rep = tb // LANE

    def kern(qa_ref, qb_ref, k_ref, va_ref, vb_ref, c_ref, o_ref, ob_ref, lsea_ref, lseb_ref, m_ref, acc_ref):
        qi = pl.program_id(1)
        ki = pl.program_id(2)

        @pl.when(ki == 0)
        def _():
            m_ref[...] = jnp.full_like(m_ref, NEG)
            acc_ref[...] = jnp.zeros_like(acc_ref)

        def step(diag):
            k = k_ref[...]
            for h, (q_ref, v_ref) in enumerate(((qa_ref, va_ref), (qb_ref, vb_ref))):
                s = _dot(q_ref[...], k, _NT) - c_ref[h:h + 1, :]
                if diag:
                    r = lax.broadcasted_iota(jnp.int32, (tb, tb), 0)
                    c = lax.broadcasted_iota(jnp.int32, (tb, tb), 1)
                    s = jnp.where(c <= r, s, NEG)
                m_prev = m_ref[h]
                m_new = jnp.maximum(m_prev, jnp.max(s, axis=1, keepdims=True))
                p = jnp.exp2(s - jnp.tile(m_new, (1, rep)))
                acc_ref[h] = acc_ref[h] * jnp.exp2(m_prev - m_new) + _dot(p.astype(BF16), v_ref[...])
                m_ref[h] = m_new

        @pl.when(ki < qi)
        def _():
            step(False)

        @pl.when(ki == qi)
        def _():
            step(True)
            lo = lax.broadcasted_iota(jnp.int32, (tb, LANE), 1) < HALF
            a0, a1 = acc_ref[0], acc_ref[1]
            l0 = a0[:, L_LANE[0]:L_LANE[0] + 1]
            l1 = a1[:, L_LANE[1]:L_LANE[1] + 1]
            o = jnp.where(lo, a0 / l0, a1 / l1)
            o_ref[...] = o
            ob_ref[...] = o.astype(BF16)
            lsea_ref[...] = m_ref[0] + jnp.log(l0) * LOG2E
            lseb_ref[...] = m_ref[1] + jnp.log(l1) * LOG2E

    qspec = pl.BlockSpec((tb, LANE), lambda hp, qi, ki: (qi, hp))
    kspec = pl.BlockSpec((tb, LANE), lambda hp, qi, ki: (jnp.minimum(ki, qi), hp))
    return pl.pallas_call(
        kern, name=name, grid=(HEAD_PAIRS, nq, nq),
        in_specs=[qspec, qspec, kspec, kspec, kspec,
                  pl.BlockSpec((None, 2, tb), lambda hp, qi, ki: (hp, 0, jnp.minimum(ki, qi)))],
        out_specs=[qspec, qspec, qspec, qspec],
        out_shape=[jax.ShapeDtypeStruct((T, D_MODEL), F32), jax.ShapeDtypeStruct((T, D_MODEL), BF16),
                   jax.ShapeDtypeStruct((T, D_MODEL), F32), jax.ShapeDtypeStruct((T, D_MODEL), F32)],
        scratch_shapes=[pltpu.VMEM((2, tb, LANE), F32), pltpu.VMEM((2, tb, LANE), F32)],
        compiler_params=_params(("parallel", "parallel", "arbitrary")),
    )(qa, qb, kk, va, vb, cT)


def _attn_bwd_prep(do, o, *, name):
    T, D = do.shape
    bt = _pick(T, ATTN_BLOCK)

    def kern(do_ref, o_ref, doa_ref, dob_ref, dlta_ref, dltb_ref):
        lo = lax.broadcasted_iota(jnp.int32, (bt, LANE), 1) < HALF
        dov = do_ref[...]
        prod = dov * o_ref[...]
        doa_ref[...] = jnp.where(lo, dov, 0.0).astype(BF16)
        dob_ref[...] = jnp.where(lo, 0.0, dov).astype(BF16)
        dlta_ref[...] = jnp.broadcast_to(jnp.sum(jnp.where(lo, prod, 0.0), axis=1, keepdims=True), (bt, LANE))
        dltb_ref[...] = jnp.broadcast_to(jnp.sum(jnp.where(lo, 0.0, prod), axis=1, keepdims=True), (bt, LANE))

    blk = pl.BlockSpec((bt, LANE), lambda i, hp: (i, hp))
    return pl.pallas_call(
        kern, name=name, grid=(T // bt, HEAD_PAIRS), in_specs=[blk, blk], out_specs=[blk] * 4,
        out_shape=[jax.ShapeDtypeStruct((T, D), BF16)] * 2 + [jax.ShapeDtypeStruct((T, D), F32)] * 2,
        compiler_params=_params(("parallel", "parallel")),
    )(do, o)


def _attn_bwd(qa, qb, kk, ka, kb, vv, doa, dob, lsea, lseb, dlta, dltb, cT, *, name):
    T = qa.shape[0]
    tb = _pick(T, ATTN_BLOCK)
    nq = T // tb
    rep = tb // LANE

    def kern(qa_ref, qb_ref, k_ref, ka_ref, kb_ref, v_ref, doa_ref, dob_ref, lsea_ref, lseb_ref, dlta_ref, dltb_ref,
             c_ref, dq_ref, dk_ref, dv_ref, dc_ref, dcq_ref):
        ki = pl.program_id(1)
        qi = pl.program_id(2)

        @pl.when(jnp.logical_and(ki == 0, qi == 0))
        def _():
            dq_ref[...] = jnp.zeros_like(dq_ref)
            dcq_ref[...] = jnp.zeros_like(dcq_ref)

        @pl.when(qi == 0)
        def _():
            dk_ref[...] = jnp.zeros_like(dk_ref)
            dv_ref[...] = jnp.zeros_like(dv_ref)
            dc_ref[...] = jnp.zeros_like(dc_ref)

        def step(diag):
            k = k_ref[...]
            v = v_ref[...]
            dq = None
            dk = None
            dv = None
            row_sums = []
            heads = ((qa_ref, ka_ref, doa_ref, lsea_ref, dlta_ref), (qb_ref, kb_ref, dob_ref, lseb_ref, dltb_ref))
            for h, (q_ref, km_ref, do_ref, lse_ref, dlt_ref) in enumerate(heads):
                q = q_ref[...]
                dom = do_ref[...]
                s = _dot(q, k, _NT) - c_ref[h:h + 1, :]
                if diag:
                    r = lax.broadcasted_iota(jnp.int32, (tb, tb), 0)
                    c = lax.broadcasted_iota(jnp.int32, (tb, tb), 1)
                    s = jnp.where(c <= r, s, NEG)
                p = jnp.exp2(s - jnp.tile(lse_ref[...], (1, rep)))
                ds = p * (_dot(dom, v, _NT) - jnp.tile(dlt_ref[...], (1, rep)))
                dc_ref[h:h + 1, :] -= jnp.sum(ds, axis=0, keepdims=True)
                row_sums.append(jnp.sum(ds, axis=1, keepdims=True))
                dsb = ds.astype(BF16)
                tv = _dot(p.astype(BF16), dom, _TN)
                tk = _dot(dsb, q, _TN)
                tq = _dot(dsb, km_ref[...])
                dv = tv if dv is None else dv + tv
                dk = tk if dk is None else dk + tk
                dq = tq if dq is None else dq + tq
            dv_ref[...] += dv
            dk_ref[...] += dk * LN2
            rows = pl.ds(pl.multiple_of(qi * tb, tb), tb)
            dq_ref[rows, :] += dq * 0.125
            lo = lax.broadcasted_iota(jnp.int32, (tb, LANE), 1) < HALF
            dcq_ref[rows, :] += jnp.where(lo, row_sums[0], row_sums[1])

        @pl.when(qi > ki)
        def _():
            step(False)

        @pl.when(qi == ki)
        def _():
            step(True)

    qspec = pl.BlockSpec((tb, LANE), lambda hp, ki, qi: (jnp.maximum(qi, ki), hp))
    kspec = pl.BlockSpec((tb, LANE), lambda hp, ki, qi: (ki, hp))
    cspec = pl.BlockSpec((None, 2, tb), lambda hp, ki, qi: (hp, 0, ki))
    qacc = pl.BlockSpec((T, LANE), lambda hp, ki, qi: (0, hp), pipeline_mode=pl.Buffered(1))
    return pl.pallas_call(
        kern, name=name, grid=(HEAD_PAIRS, nq, nq),
        in_specs=[qspec, qspec, kspec, kspec, kspec, kspec, qspec, qspec, qspec, qspec, qspec, qspec, cspec],
        out_specs=[qacc, kspec, kspec, cspec, qacc],
        out_shape=[jax.ShapeDtypeStruct((T, D_MODEL), F32)] * 3 + [jax.ShapeDtypeStruct((HEAD_PAIRS, 2, T), F32),
                                                                   jax.ShapeDtypeStruct((T, D_MODEL), F32)],
        compiler_params=_params(("parallel", "arbitrary", "arbitrary"), vmem_mb=56),
    )(qa, qb, kk, ka, kb, vv, doa, dob, lsea, lseb, dlta, dltb, cT)


def _ssd_dt_fwd(proj, dt_bias, a_log, *, name):
    T = proj.shape[0]
    Q = SSM_CHUNK
    col = (2 * SSM_D_INNER + 2 * SSM_GROUPS * SSM_STATE) // LANE

    def kern(raw_ref, b_ref, al_ref, dt_ref, ac_ref):
        dt = _softplus(raw_ref[...] + b_ref[...])
        dt_ref[...] = dt
        ac_ref[...] = _tri_matmul(_tri(Q, True), dt * (-jnp.exp(al_ref[...])))

    vec = pl.BlockSpec((1, LANE), lambda i: (0, 0))
    blk = pl.BlockSpec((Q, LANE), lambda i: (i, 0))
    return pl.pallas_call(
        kern, name=name, grid=(T // Q,),
        in_specs=[pl.BlockSpec((Q, LANE), lambda i: (i, col)), vec, vec], out_specs=[blk, blk],
        out_shape=[jax.ShapeDtypeStruct((T, LANE), F32)] * 2,
        compiler_params=_params(("parallel",)),
    )(proj, dt_bias, a_log)


def _ssd_dt_bwd(da_a, da_b, ddt_a, ddt_b, dt, proj, dt_bias, a_log, into, *, name):
    T = proj.shape[0]
    Q = SSM_CHUNK
    col = (2 * SSM_D_INNER + 2 * SSM_GROUPS * SSM_STATE) // LANE

    def kern(daa_ref, dab_ref, dda_ref, ddb_ref, dt_ref, raw_ref, b_ref, al_ref, into_ref, draw_ref, dal_ref, db_ref,
             acc_ref):
        i = pl.program_id(0)

        @pl.when(i == 0)
        def _():
            acc_ref[...] = jnp.zeros_like(acc_ref)
            db_ref[...] = jnp.zeros_like(db_ref)

        A = -jnp.exp(al_ref[...])
        ddA = _tri_matmul(_tri(Q, False), daa_ref[...] + dab_ref[...])
        ddt = dda_ref[...] + ddb_ref[...] + ddA * A
        acc_ref[...] += jnp.sum(ddA * dt_ref[...], axis=0, keepdims=True)
        lane = lax.broadcasted_iota(jnp.int32, (Q, LANE), 1)
        draw = jnp.where(lane < SSM_HEADS, ddt * _sigmoid(raw_ref[...] + b_ref[...]), 0.0)
        draw_ref[...] = draw.astype(BF16)
        db_ref[...] += jnp.sum(draw, axis=0, keepdims=True)
        dal_ref[...] = acc_ref[...] * A

    vec = pl.BlockSpec((1, LANE), lambda i: (0, 0))
    blk = pl.BlockSpec((Q, LANE), lambda i: (i, 0))
    return pl.pallas_call(
        kern, name=name, grid=(T // Q,),
        in_specs=[blk, blk, blk, blk, blk, pl.BlockSpec((Q, LANE), lambda i: (i, col)), vec, vec, _ANY],
        out_specs=[pl.BlockSpec((Q, LANE), lambda i: (i, col)), vec, vec],
        out_shape=[jax.ShapeDtypeStruct(into.shape, into.dtype), jax.ShapeDtypeStruct((1, LANE), F32),
                   jax.ShapeDtypeStruct((1, LANE), F32)],
        input_output_aliases={8: 0},
        scratch_shapes=[pltpu.VMEM((1, LANE), F32)],
        compiler_params=_params(("arbitrary",)),
    )(da_a, da_b, ddt_a, ddt_b, dt, proj, dt_bias, a_log, into)


def _conv_silu_fwd(proj, cw, cb, *, name):
    T = proj.shape[0]
    C = SSM_XBC
    bt = _pick(T, ROW_BLOCK, SUBLANE)
    bc = 1024
    off = SSM_D_INNER // bc

    def kern(x_ref, halo_ref, cw_ref, cb_ref, o_ref, ext_ref):
        i = pl.program_id(0)
        _fill_ext_past(ext_ref, halo_ref, x_ref[...], i, bt)
        pre = cb_ref[...] + _conv_past(_past_taps(ext_ref, SSM_CONV, bt), cw_ref)
        o_ref[...] = pre * _sigmoid(pre)

    return pl.pallas_call(
        kern, name=name, grid=(T // bt, C // bc),
        in_specs=[pl.BlockSpec((bt, bc), lambda i, j: (i, j + off)), _halo_prev(bt, bc, off),
                  pl.BlockSpec((SSM_CONV, bc), lambda i, j: (0, j)), pl.BlockSpec((1, bc), lambda i, j: (0, j))],
        out_specs=pl.BlockSpec((bt, bc), lambda i, j: (i, j)),
        out_shape=jax.ShapeDtypeStruct((T, C), F32),
        scratch_shapes=[pltpu.VMEM((bt + HALO, bc), F32)],
        compiler_params=_params(("parallel", "parallel")),
    )(proj, proj, cw, cb)


def _conv_silu_bwd(dxbc, proj, cw, cb, *, name):
    T = proj.shape[0]
    C = SSM_XBC
    K = SSM_CONV
    bt = _pick(T, ROW_BLOCK, SUBLANE)
    bc = 1024
    off = SSM_D_INNER // bc

    def kern(d_ref, x_ref, halo_ref, cw_ref, cb_ref, dpre_ref, dcb_ref, dcw_ref, ext_ref):
        i = pl.program_id(1)
        _fill_ext_past(ext_ref, halo_ref, x_ref[...], i, bt)
        taps = _past_taps(ext_ref, K, bt)
        pre = cb_ref[...] + _conv_past(taps, cw_ref)
        sg = _sigmoid(pre)
        dpre = d_ref[...] * sg * (1.0 + pre * (1.0 - sg))
        dpre_ref[...] = dpre

        @pl.when(i == 0)
        def _():
            dcb_ref[...] = jnp.zeros_like(dcb_ref)
            dcw_ref[...] = jnp.zeros_like(dcw_ref)

        dcb_ref[...] += jnp.sum(dpre, axis=0, keepdims=True)
        for k in range(K):
            dcw_ref[k:k + 1, :] += jnp.sum(dpre * taps[k], axis=0, keepdims=True)

    blk = pl.BlockSpec((bt, bc), lambda j, i: (i, j))
    return pl.pallas_call(
        kern, name=name, grid=(C // bc, T // bt),
        in_specs=[blk, pl.BlockSpec((bt, bc), lambda j, i: (i, j + off)),
                  pl.BlockSpec((HALO, bc), lambda j, i: (jnp.maximum(i * (bt // HALO) - 1, 0), j + off)),
                  pl.BlockSpec((K, bc), lambda j, i: (0, j)), pl.BlockSpec((1, bc), lambda j, i: (0, j))],
        out_specs=[blk, pl.BlockSpec((1, bc), lambda j, i: (0, j)), pl.BlockSpec((K, bc), lambda j, i: (0, j))],
        out_shape=[jax.ShapeDtypeStruct((T, C), F32), jax.ShapeDtypeStruct((1, C), F32),
                   jax.ShapeDtypeStruct((K, C), F32)],
        scratch_shapes=[pltpu.VMEM((bt + HALO, bc), F32)],
        compiler_params=_params(("parallel", "arbitrary")),
    )(dxbc, proj, proj, cw, cb)


_GP = SSM_D_INNER // SSM_GROUPS
_HPG = SSM_HEADS // SSM_GROUPS
_PH = SSM_D_INNER // SSM_HEADS


def _head_masks(rows):
    lane = lax.broadcasted_iota(jnp.int32, (rows, _GP), 1)
    return [jnp.logical_and(lane >= r * _PH, lane < (r + 1) * _PH) for r in range(_HPG)]


def _ssd_cols(g):
    x0 = g * _GP
    b0 = SSM_D_INNER + g * SSM_STATE
    c0 = SSM_D_INNER + (SSM_GROUPS + g) * SSM_STATE
    return slice(x0, x0 + _GP), slice(b0, b0 + SSM_STATE), slice(c0, c0 + SSM_STATE)


def _ssd_specs(idx):
    Q, N = SSM_CHUNK, SSM_STATE
    return dict(
        xbc=pl.BlockSpec((Q, SSM_XBC), lambda j: (idx(j), 0)),
        x=pl.BlockSpec((Q, SSM_D_INNER), lambda j: (idx(j), 0)),
        col=pl.BlockSpec((Q, LANE), lambda j: (idx(j), 0)),
        row=pl.BlockSpec((SSM_HEADS, Q), lambda j: (0, idx(j))),
        st=pl.BlockSpec((N, SSM_D_INNER), lambda j: (idx(j), 0)),
    )


def _ssd_scan_fwd(xbc, dtc, acc_, dtr, acr, *, name):
    T = xbc.shape[0]
    Q, N = SSM_CHUNK, SSM_STATE
    nc = T // Q
    sp = _ssd_specs(lambda j: j)

    def kern(xbc_ref, dtc_ref, ac_ref, dtr_ref, ar_ref, ys_ref, st_ref, state_ref):
        @pl.when(pl.program_id(0) == 0)
        def _():
            state_ref[...] = jnp.zeros_like(state_ref)

        r_i = lax.broadcasted_iota(jnp.int32, (Q, Q), 0)
        c_i = lax.broadcasted_iota(jnp.int32, (Q, Q), 1)
        tri = c_i <= r_i
        masks = _head_masks(Q)
        masks1 = _head_masks(1)
        for g in range(SSM_GROUPS):
            xs, bs, cs = _ssd_cols(g)
            S = state_ref[g]
            st_ref[:, xs] = S
            x = xbc_ref[:, xs]
            xb = x.astype(BF16)
            Bb = xbc_ref[:, bs].astype(BF16)
            Cb = xbc_ref[:, cs].astype(BF16)
            CB = _dot(Cb, Bb, _NT)
            y = jnp.zeros((Q, _GP), F32)
            El = jnp.zeros((Q, _GP), F32)
            Wl = jnp.zeros((Q, _GP), F32)
            decl = jnp.zeros((1, _GP), F32)
            for r in range(_HPG):
                h = g * _HPG + r
                a_c = ac_ref[:, h:h + 1]
                a_r = ar_ref[h:h + 1, :]
                dt_c = dtc_ref[:, h:h + 1]
                dt_r = dtr_ref[h:h + 1, :]
                L = jnp.exp(jnp.where(tri, a_c - a_r, NEG))
                W = CB * L * dt_r
                y = jnp.where(masks[r], _dot(W.astype(BF16), xb), y)
                a_q = a_c[Q - 1:Q, :]
                El = jnp.where(masks[r], jnp.exp(a_c), El)
                Wl = jnp.where(masks[r], jnp.exp(a_q - a_c) * dt_c, Wl)
                decl = jnp.where(masks1[r], jnp.exp(a_q), decl)
            ys_ref[:, xs] = y + _dot(Cb, S.astype(BF16)) * El
            state_ref[g] = S * decl + _dot(Bb, (x * Wl).astype(BF16), _TN)

    return pl.pallas_call(
        kern, name=name, grid=(nc,),
        in_specs=[sp["xbc"], sp["col"], sp["col"], sp["row"], sp["row"]],
        out_specs=[sp["x"], sp["st"]],
        out_shape=[jax.ShapeDtypeStruct((T, SSM_D_INNER), F32), jax.ShapeDtypeStruct((nc * N, SSM_D_INNER), F32)],
        scratch_shapes=[pltpu.VMEM((SSM_GROUPS, N, _GP), F32)],
        compiler_params=_params(("arbitrary",)),
    )(xbc, dtc, acc_, dtr, acr)


def _ssd_scan_bwd(xbc, dys, dskip, st, dtc, acc_, dtr, acr, *, name):
    T = xbc.shape[0]
    Q, N = SSM_CHUNK, SSM_STATE
    nc = T // Q
    sp = _ssd_specs(lambda j: nc - 1 - j)

    def kern(xbc_ref, dy_ref, dsk_ref, st_ref, dtc_ref, ac_ref, dtr_ref, ar_ref,
             dxbc_ref, dac_ref, dar_ref, ddc_ref, ddr_ref, dstate_ref):
        @pl.when(pl.program_id(0) == 0)
        def _():
            dstate_ref[...] = jnp.zeros_like(dstate_ref)

        r_i = lax.broadcasted_iota(jnp.int32, (Q, Q), 0)
        c_i = lax.broadcasted_iota(jnp.int32, (Q, Q), 1)
        tri = c_i <= r_i
        last_row = lax.broadcasted_iota(jnp.int32, (Q, 1), 0) == Q - 1
        lane128 = lax.broadcasted_iota(jnp.int32, (Q, LANE), 1)
        masks = _head_masks(Q)
        masksN = _head_masks(N)
        masks1 = _head_masks(1)
        zeros = jnp.zeros((Q, _GP), F32)
        dacol = jnp.zeros((Q, LANE), F32)
        ddcol = jnp.zeros((Q, LANE), F32)
        for g in range(SSM_GROUPS):
            xs, bs, cs = _ssd_cols(g)
            dS = dstate_ref[g]
            dSb = dS.astype(BF16)
            S = st_ref[:, xs]
            Sb = S.astype(BF16)
            x = xbc_ref[:, xs]
            xb = x.astype(BF16)
            Bb = xbc_ref[:, bs].astype(BF16)
            Cb = xbc_ref[:, cs].astype(BF16)
            dy = dy_ref[:, xs]
            CB = _dot(Cb, Bb, _NT)
            BdS = _dot(Bb, dSb)
            hx = BdS * x
            yd = _dot(Cb, Sb) * dy
            dSS = dS * S
            dxi, El, Wl = zeros, zeros, zeros
            decl = jnp.zeros((1, _GP), F32)
            dBacc = jnp.zeros((Q, N), F32)
            dCacc = jnp.zeros((Q, N), F32)
            for r in range(_HPG):
                h = g * _HPG + r
                hm = masks[r]
                a_c = ac_ref[:, h:h + 1]
                a_r = ar_ref[h:h + 1, :]
                dt_c = dtc_ref[:, h:h + 1]
                dt_r = dtr_ref[h:h + 1, :]
                L = jnp.exp(jnp.where(tri, a_c - a_r, NEG))
                GL = CB * L
                W = GL * dt_r
                dym = jnp.where(hm, dy, 0.0).astype(BF16)
                dW = _dot(dym, xb, _NT)
                E = dW * W
                da_c = jnp.sum(E, axis=1, keepdims=True)
                dar_ref[h:h + 1, :] = -jnp.sum(E, axis=0, keepdims=True)
                ddr_ref[h:h + 1, :] = jnp.sum(dW * GL, axis=0, keepdims=True)
                dGb = (dW * L * dt_r).astype(BF16)
                dCacc = dCacc + _dot(dGb, Bb)
                dBacc = dBacc + _dot(dGb, Cb, _TN)
                dxi = dxi + _dot(W.astype(BF16), dym, _TN)
                a_q = a_c[Q - 1:Q, :]
                e_c = jnp.exp(a_c)
                eq_c = jnp.exp(a_q - a_c)
                w_c = eq_c * dt_c
                ydr = jnp.sum(jnp.where(hm, yd, 0.0), axis=1, keepdims=True) * e_c
                h_c = jnp.sum(jnp.where(hm, hx, 0.0), axis=1, keepdims=True)
                hw = h_c * w_c
                dss = jnp.sum(jnp.sum(jnp.where(masksN[r], dSS, 0.0), axis=1, keepdims=True), axis=0, keepdims=True)
                s_q = jnp.sum(hw, axis=0, keepdims=True) + jnp.exp(a_q) * dss
                da_c = da_c + ydr - hw + jnp.where(last_row, s_q, 0.0)
                dacol = jnp.where(lane128 == h, da_c, dacol)
                ddcol = jnp.where(lane128 == h, h_c * eq_c, ddcol)
                El = jnp.where(hm, e_c, El)
                Wl = jnp.where(hm, w_c, Wl)
                decl = jnp.where(masks1[r], jnp.exp(a_q), decl)
            dxbc_ref[:, xs] = dxi + BdS * Wl + dsk_ref[:, xs]
            dxbc_ref[:, bs] = dBacc + _dot((x * Wl).astype(BF16), dSb, _NT)
            dyE = (dy * El).astype(BF16)
            dxbc_ref[:, cs] = dCacc + _dot(dyE, Sb, _NT)
            dstate_ref[g] = dS * decl + _dot(Cb, dyE, _TN)
        dac_ref[...] = dacol
        ddc_ref[...] = ddcol

    return pl.pallas_call(
        kern, name=name, grid=(nc,),
        in_specs=[sp["xbc"], sp["x"], sp["x"], sp["st"], sp["col"], sp["col"], sp["row"], sp["row"]],
        out_specs=[sp["xbc"], sp["col"], sp["row"], sp["col"], sp["row"]],
        out_shape=[jax.ShapeDtypeStruct((T, SSM_XBC), F32),
                   jax.ShapeDtypeStruct((T, LANE), F32), jax.ShapeDtypeStruct((SSM_HEADS, T), F32),
                   jax.ShapeDtypeStruct((T, LANE), F32), jax.ShapeDtypeStruct((SSM_HEADS, T), F32)],
        scratch_shapes=[pltpu.VMEM((SSM_GROUPS, N, _GP), F32)],
        compiler_params=_params(("arbitrary",)),
    )(xbc, dys, dskip, st, dtc, acc_, dtr, acr)


def _gate_norm_fwd(ys, xbc, proj, d_exp, norm_w, *, name):
    T = ys.shape[0]
    bt = _pick(T, NARROW_ROW_BLOCK, SUBLANE)

    def kern(ys_ref, x_ref, z_ref, d_ref, w_ref, o_ref):
        z = z_ref[...]
        yz = (ys_ref[...] + d_ref[...] * x_ref[...]) * (z * _sigmoid(z))
        rstd = lax.rsqrt(jnp.mean(yz * yz, axis=-1, keepdims=True) + RMS_EPS)
        o_ref[...] = (yz * rstd * w_ref[...]).astype(BF16)

    blk = pl.BlockSpec((bt, _GP), lambda i, g: (i, g))
    vec = pl.BlockSpec((1, _GP), lambda i, g: (0, g))
    return pl.pallas_call(
        kern, name=name, grid=(T // bt, SSM_GROUPS), in_specs=[blk, blk, blk, vec, vec], out_specs=blk,
        out_shape=jax.ShapeDtypeStruct((T, SSM_D_INNER), BF16), compiler_params=_params(("parallel", "parallel")),
    )(ys, xbc, proj, d_exp, norm_w)


def _gate_norm_bwd(dyn, ys, xbc, proj, d_exp, norm_w, *, name):
    T = ys.shape[0]
    bt = _pick(T, NARROW_ROW_BLOCK, SUBLANE)

    def kern(dyn_ref, ys_ref, x_ref, z_ref, d_ref, w_ref, dz_ref, dys_ref, dsk_ref, dw_ref, dd_ref):
        i = pl.program_id(1)
        z = z_ref[...]
        x = x_ref[...]
        sg = _sigmoid(z)
        sz = z * sg
        y = ys_ref[...] + d_ref[...] * x
        yz = y * sz
        rstd = lax.rsqrt(jnp.mean(yz * yz, axis=-1, keepdims=True) + RMS_EPS)
        yhat = yz * rstd
        dynv = dyn_ref[...]
        gg = dynv * w_ref[...]
        dyz = rstd * (gg - yhat * jnp.mean(gg * yhat, axis=-1, keepdims=True))
        dy = dyz * sz
        dz_ref[...] = (dyz * y * sg * (1.0 + z * (1.0 - sg))).astype(BF16)
        dys_ref[...] = dy
        dsk_ref[...] = dy * d_ref[...]

        @pl.when(i == 0)
        def _():
            dw_ref[...] = jnp.zeros_like(dw_ref)
            dd_ref[...] = jnp.zeros_like(dd_ref)

        dw_ref[...] += jnp.sum(dynv * yhat, axis=0, keepdims=True)
        dd_ref[...] += jnp.sum(dy * x, axis=0, keepdims=True)

    blk = pl.BlockSpec((bt, _GP), lambda g, i: (i, g))
    vec = pl.BlockSpec((1, _GP), lambda g, i: (0, g))
    act = jax.ShapeDtypeStruct((T, SSM_D_INNER), F32)
    par = jax.ShapeDtypeStruct((1, SSM_D_INNER), F32)
    return pl.pallas_call(
        kern, name=name, grid=(SSM_GROUPS, T // bt), in_specs=[blk, blk, blk, blk, vec, vec],
        out_specs=[blk, blk, blk, vec, vec],
        out_shape=[jax.ShapeDtypeStruct((T, SSM_IN_PAD), BF16), act, act, par, par],
        compiler_params=_params(("parallel", "arbitrary")),
    )(dyn, ys, xbc, proj, d_exp, norm_w)


def _loss_head(y, target, *, name):
    T, D = y.shape
    bt = _pick(T, ROW_BLOCK, SUBLANE)

    def kern(y_ref, t_ref, l_ref, dy_ref):
        i = pl.program_id(0)
        err = y_ref[...] - t_ref[...]
        dy_ref[...] = err * (1.0 / D)

        @pl.when(i == 0)
        def _():
            l_ref[...] = jnp.zeros_like(l_ref)

        l_ref[...] += jnp.sum(err * err, axis=0, keepdims=True) * (0.5 / D)

    row = pl.BlockSpec((bt, D), lambda i: (i, 0))
    vec = pl.BlockSpec((1, D), lambda i: (0, 0))
    return pl.pallas_call(
        kern, name=name, grid=(T // bt,), in_specs=[row, row], out_specs=[vec, row],
        out_shape=[jax.ShapeDtypeStruct((1, D), F32), jax.ShapeDtypeStruct((T, D), F32)],
        compiler_params=_params(("arbitrary",)),
    )(y, target)


def _adamw(w, g, m, v, *, name):
    shape = w.shape
    w, g, m, v = (t.reshape(-1, shape[-1]) for t in (w, g, m, v))
    R, C = w.shape
    br = _pick(R, 256, SUBLANE)

    def kern(w_ref, g_ref, m_ref, v_ref, d_ref, nm_ref, nv_ref):
        gv = g_ref[...]
        nm = ADAM_B1 * m_ref[...] + (1.0 - ADAM_B1) * gv
        nv = ADAM_B2 * v_ref[...] + (1.0 - ADAM_B2) * (gv * gv)
        m_hat = nm / (1.0 - ADAM_B1 ** ADAM_STEP)
        v_hat = nv / (1.0 - ADAM_B2 ** ADAM_STEP)
        d_ref[...] = -ADAM_LR * (m_hat / (jnp.sqrt(v_hat) + ADAM_EPS) + ADAM_WD * w_ref[...])
        nm_ref[...] = nm
        nv_ref[...] = nv

    blk = pl.BlockSpec((br, C), lambda i: (i, 0))
    outs = pl.pallas_call(
        kern, name=name, grid=(R // br,), in_specs=[blk] * 4, out_specs=[blk] * 3,
        out_shape=[jax.ShapeDtypeStruct((R, C), F32)] * 3, compiler_params=_params(("parallel",)),
    )(w, g, m, v)
    return [o.reshape(shape) for o in outs]


def _add2(a, b, out_dtype, *, name):
    shape = a.shape
    a2, b2 = a.reshape(-1, shape[-1]), b.reshape(-1, shape[-1])
    R, C = a2.shape
    br = _pick(R, 512, SUBLANE)

    def kern(a_ref, b_ref, o_ref):
        o_ref[...] = (a_ref[...] + b_ref[...]).astype(out_dtype)

    blk = pl.BlockSpec((br, C), lambda i: (i, 0))
    return pl.pallas_call(
        kern, name=name, grid=(R // br,), in_specs=[blk, blk], out_specs=blk,
        out_shape=jax.ShapeDtypeStruct((R, C), out_dtype), compiler_params=_params(("parallel",)),
    )(a2, b2).reshape(shape)


def _sum4(buf, *, name):
    _, R, C = buf.shape
    br = _pick(R, 512, SUBLANE)

    def kern(b_ref, o_ref):
        b = [b_ref[k].astype(F32) for k in range(4)]
        o_ref[...] = ((b[0] + b[1]) + b[2]) + b[3]

    return pl.pallas_call(
        kern, name=name, grid=(R // br,), in_specs=[pl.BlockSpec((4, br, C), lambda i: (0, i, 0))],
        out_specs=pl.BlockSpec((br, C), lambda i: (i, 0)),
        out_shape=jax.ShapeDtypeStruct((R, C), F32), compiler_params=_params(("parallel",)),
    )(buf)


def _place():
    x, y, c = lax.axis_index("x"), lax.axis_index("y"), lax.axis_index("c")
    other_chips = [(1 - x, y), (x, 1 - y), (1 - x, 1 - y)]
    return x, y, c, other_chips


def _gather_chips(w, *, name):
    R, C = w.shape
    H = R // 2

    def body(w_ref, out_ref, send_sems, recv_sems):
        x, y, c, chips = _place()
        me_chip = 2 * x + y
        sib = (x, y, 1 - c)

        def rows(chip, hc):
            return out_ref.at[chip, pl.ds(hc * H, H), :]

        def copy(k, blk, to, src=None):
            return pltpu.make_async_remote_copy(
                src_ref=blk if src is None else src, dst_ref=blk, send_sem=send_sems.at[k], recv_sem=recv_sems.at[k],
                device_id=to, device_id_type=MESH)

        first = [copy(j, rows(me_chip, c), (cx, cy, c), src=w_ref.at[pl.ds(c * H, H), :])
                 for j, (cx, cy) in enumerate(chips)]
        for cp in first:
            cp.start()
        passed = []
        for j, (cx, cy) in enumerate(chips):
            blk = rows(2 * cx + cy, c)
            copy(j, blk, (cx, cy, c)).wait_recv()
            fw = copy(3 + j, blk, sib)
            fw.start()
            passed.append(fw)
        for j, (cx, cy) in enumerate(chips):
            copy(3 + j, rows(2 * cx + cy, 1 - c), sib).wait_recv()
        for cp in first + passed:
            cp.wait_send()

    return pl.pallas_call(
        body, name=name, in_specs=[_ANY], out_specs=_ANY,
        out_shape=jax.ShapeDtypeStruct((4, R, C), w.dtype),
        scratch_shapes=[pltpu.SemaphoreType.DMA((6,)), pltpu.SemaphoreType.DMA((6,))],
    )(w)


def _pair_swap(v, *, name, other_half=False):
    shape = (v.shape[0], v.shape[1] // 2, v.shape[2]) if other_half else v.shape

    def body(v_ref, out_ref, send_sem, recv_sem):
        x, y, c, _ = _place()
        src = v_ref.at[:, pl.ds((1 - c) * shape[1], shape[1]), :] if other_half else v_ref
        cp = pltpu.make_async_remote_copy(src_ref=src, dst_ref=out_ref, send_sem=send_sem, recv_sem=recv_sem,
                                          device_id=(x, y, 1 - c), device_id_type=MESH)
        cp.start()
        cp.wait()

    return pl.pallas_call(
        body, name=name, in_specs=[_ANY], out_specs=_ANY, out_shape=jax.ShapeDtypeStruct(shape, v.dtype),
        scratch_shapes=[pltpu.SemaphoreType.DMA, pltpu.SemaphoreType.DMA],
    )(v)


def _chip_exchange(pv, *, name):
    def body(p_ref, out_ref, send_sems, recv_sems):
        x, y, c, chips = _place()
        me_chip = 2 * x + y
        sends = []
        for j, (cx, cy) in enumerate(chips):
            cp = pltpu.make_async_remote_copy(
                src_ref=p_ref.at[2 * cx + cy], dst_ref=out_ref.at[me_chip], send_sem=send_sems.at[j],
                recv_sem=recv_sems.at[j], device_id=(cx, cy, c), device_id_type=MESH)
            cp.start()
            sends.append(cp)
        for j, (cx, cy) in enumerate(chips):
            blk = out_ref.at[2 * cx + cy]
            pltpu.make_async_remote_copy(src_ref=blk, dst_ref=blk, send_sem=send_sems.at[j], recv_sem=recv_sems.at[j],
                                         device_id=(cx, cy, c), device_id_type=MESH).wait_recv()
        for cp in sends:
            cp.wait_send()

    return pl.pallas_call(
        body, name=name, in_specs=[_ANY], out_specs=_ANY, out_shape=jax.ShapeDtypeStruct(pv.shape, pv.dtype),
        scratch_shapes=[pltpu.SemaphoreType.DMA((3,)), pltpu.SemaphoreType.DMA((3,))],
    )(pv)


WEIGHTS = [
    ("attn_w_in", 2), ("attn_b_f", None), ("attn_w_out", 1), ("ssm_w_in", 2), ("ssm_conv_w", 2), ("ssm_conv_b", 1),
    ("ssm_dt_bias", None), ("ssm_A_log", None), ("ssm_D", None), ("ssm_norm_w", 1), ("ssm_w_out", 1),
    ("ln_mix_g", None), ("ln_mix_b", None), ("ffn_w_up", 2), ("ffn_conv_w", 2), ("ffn_conv_b", None),
    ("ffn_w_down", 1), ("ln_ffn_g", None), ("ln_ffn_b", None), ("ple_w_proj", 2), ("ple_w_gate", 1),
    ("ple_b_gate", None),
]
N_CHIPS = 4
MATMUL_WEIGHTS = ("attn_w_in", "attn_w_out", "ssm_w_in", "ssm_w_out", "ffn_w_up", "ffn_w_down", "ple_w_proj",
                  "ple_w_gate")


def _pack(arrays):
    parts = []
    total = 0
    for a in arrays:
        n = a.size
        pad = (-n) % PACK_COLS
        flat = a.reshape(-1)
        parts.append(jnp.pad(flat, (0, pad)) if pad else flat)
        total += n + pad
    rows = total // PACK_COLS
    rpad = (-rows) % PACK_ROW_ALIGN
    if rpad:
        parts.append(jnp.zeros((rpad * PACK_COLS,), arrays[0].dtype))
    return jnp.concatenate(parts).reshape(rows + rpad, PACK_COLS)


def _unpack(buf, shapes):
    flat = buf.reshape(-1)
    out = []
    off = 0
    for s in shapes:
        n = math.prod(s)
        out.append(flat[off:off + n].reshape(s))
        off += n + ((-n) % PACK_COLS)
    return out


def _from_row_layout(a):
    return jnp.pad(a.T, ((0, 0), (0, LANE - SSM_HEADS)))


def _pad_lanes(v, n=LANE):
    return jnp.pad(v, (0, n - v.shape[0])).reshape(1, n)


def _local_step(x, p, target, W):
    T = x.shape[0]
    row = lambda v: v.reshape(1, -1)
    attn_in = jnp.pad(W["attn_w_in"][0], ((0, 0), (0, ATTN_IN_PAD - W["attn_w_in"].shape[2])))
    ssm_in = jnp.pad(W["ssm_w_in"][0], ((0, 0), (0, SSM_IN_PAD - W["ssm_w_in"].shape[2])))
    bf = _pad_lanes(W["attn_b_f"][0])
    dt_bias = _pad_lanes(W["ssm_dt_bias"][0])
    a_log = _pad_lanes(W["ssm_A_log"][0])
    d_exp = jnp.repeat(W["ssm_D"][0], _PH).reshape(1, SSM_D_INNER)
    norm_w = row(W["ssm_norm_w"][0])
    G = {}

    def ffn_ple_fwd(i, xin, mix, tag):
        s = {}
        s["z1"], s["h1"], s["h1b"] = _ln_fwd(xin, mix, row(W["ln_mix_g"][i]), row(W["ln_mix_b"][i]),
                                             name=f"ln_mix_fwd{tag}")
        s["up"] = _mm(s["h1b"], W["ffn_w_up"][i], name=f"ffn_up{tag}")
        s["a"] = _ffn_act_fwd(s["up"], W["ffn_conv_w"][i], row(W["ffn_conv_b"][i]), name=f"ffn_act_fwd{tag}")
        ffn = _mm(s["a"], W["ffn_w_down"][i], name=f"ffn_down{tag}")
        s["z2"], s["h2"], s["h2b"] = _ln_fwd(s["h1"], ffn, row(W["ln_ffn_g"][i]), row(W["ln_ffn_b"][i]),
                                             name=f"ln_ffn_fwd{tag}")
        s["G"] = _mm(s["h2b"], W["ple_w_gate"][i], name=f"ple_gate_mm{tag}")
        s["pp"] = _mm(pb[i], W["ple_w_proj"][i], name=f"ple_proj_mm{tag}")
        out, outb = _ple_fwd(s["h2"], s["G"], row(W["ple_b_gate"][i]), s["pp"], name=f"ple_fwd{tag}")
        return out, outb, s

    def ffn_ple_bwd(i, dx, s, tag):
        g = {}
        dG, dpp, g["ple_b_gate"] = _ple_bwd(dx, s["G"], row(W["ple_b_gate"][i]), s["pp"], name=f"ple_bwd{tag}")
        g["ple_w_gate"] = _mm(s["h2b"], dG, ta=True, name=f"ple_gate_dw{tag}")
        g["ple_w_proj"] = _mm(pb[i], dpp, ta=True, name=f"ple_proj_dw{tag}")
        dh2 = _mm(dG, W["ple_w_gate"][i], tb=True, add=dx, name=f"ple_gate_dx{tag}")
        dz2, dz2b, g["ln_ffn_g"], g["ln_ffn_b"] = _ln_bwd(dh2, s["z2"], row(W["ln_ffn_g"][i]), name=f"ln_ffn_bwd{tag}")
        da = _mm(dz2b, W["ffn_w_down"][i], tb=True, out_dtype=BF16, name=f"ffn_down_dx{tag}")
        g["ffn_w_down"] = _mm(s["a"], dz2b, ta=True, name=f"ffn_down_dw{tag}")
        dup, dgc, g["ffn_conv_b"], g["ffn_conv_w"] = _ffn_act_bwd(
            da, s["up"], W["ffn_conv_w"][i], row(W["ffn_conv_b"][i]), name=f"ffn_act_bwd{tag}")
        dup = _dwconv_bwd_data(dgc, W["ffn_conv_w"][i], FFN_CONV, dup, FFN_DIM, name=f"ffn_conv_bwd{tag}")
        g["ffn_w_up"] = _mm(s["h1b"], dup, ta=True, name=f"ffn_up_dw{tag}")
        dh1 = _mm(dup, W["ffn_w_up"][i], tb=True, add=dz2, add_scale=DEEPNORM_ALPHA, name=f"ffn_up_dx{tag}")
        dz1, dz1b, g["ln_mix_g"], g["ln_mix_b"] = _ln_bwd(dh1, s["z1"], row(W["ln_mix_g"][i]), name=f"ln_mix_bwd{tag}")
        return dz1, dz1b, g

    xb = x.astype(BF16)
    pb = p.astype(BF16)
    proj0 = _mm(xb, attn_in, name="attn_in")
    c_col = _fox_gate_fwd(proj0, bf, name="fox_gate_fwd")
    cT = (c_col[:, :ATTN_HEADS] * LOG2E).T.reshape(HEAD_PAIRS, 2, T)
    qa, qb, kk, ka, kb, vv, va, vb = _attn_prep(proj0, name="attn_prep")
    o, ob, lsea, lseb = _attn_fwd(qa, qb, kk, va, vb, cT, name="attn_fwd")
    mix0 = _mm(ob, W["attn_w_out"][0], name="attn_out")
    x1, x1b, s0 = ffn_ple_fwd(0, x, mix0, "0")

    proj1 = _mm(x1b, ssm_in, name="ssm_in")
    dt, acum = _ssd_dt_fwd(proj1, dt_bias, a_log, name="ssd_dt_fwd")
    xbc = _conv_silu_fwd(proj1, W["ssm_conv_w"][0], row(W["ssm_conv_b"][0]), name="ssd_conv_fwd")
    dtr, acr = dt[:, :SSM_HEADS].T, acum[:, :SSM_HEADS].T
    ys, states = _ssd_scan_fwd(xbc, dt, acum, dtr, acr, name="ssd_scan_fwd")
    yn = _gate_norm_fwd(ys, xbc, proj1, d_exp, norm_w, name="ssd_gate_norm_fwd")
    mix1 = _mm(yn, W["ssm_w_out"][0], name="ssm_out")
    x2, _, s1 = ffn_ple_fwd(1, x1, mix1, "1")

    lpart, dy = _loss_head(x2, target, name="loss_head")
    loss = jnp.sum(lpart)

    dz1, dz1b, g1 = ffn_ple_bwd(1, dy, s1, "1")
    G["ssm_w_out"] = _mm(yn, dz1b, ta=True, name="ssm_out_dw")[None]
    dyn = _mm(dz1b, W["ssm_w_out"][0], tb=True, name="ssm_out_dx")
    dproj1, dys, dskip, dnw, dde = _gate_norm_bwd(dyn, ys, xbc, proj1, d_exp, norm_w, name="ssd_gate_norm_bwd")
    G["ssm_norm_w"] = dnw
    G["ssm_D"] = dde.reshape(SSM_HEADS, _PH).sum(axis=1)[None]
    dxbc, dac, dar, ddc, ddr = _ssd_scan_bwd(xbc, dys, dskip, states, dt, acum, dtr, acr, name="ssd_scan_bwd")
    dproj1, dal, ddb = _ssd_dt_bwd(dac, _from_row_layout(dar), ddc, _from_row_layout(ddr), dt, proj1, dt_bias, a_log,
                                   dproj1, name="ssd_dt_bwd")
    G["ssm_A_log"] = dal[:, :SSM_HEADS]
    G["ssm_dt_bias"] = ddb[:, :SSM_HEADS]
    dpre, G["ssm_conv_b"], dcw = _conv_silu_bwd(dxbc, proj1, W["ssm_conv_w"][0], row(W["ssm_conv_b"][0]),
                                                name="ssd_conv_bwd")
    G["ssm_conv_w"] = dcw[None]
    dproj1 = _dwconv_bwd_data(dpre, W["ssm_conv_w"][0], SSM_CONV, dproj1, SSM_D_INNER, name="ssd_conv_bwd_data")
    G["ssm_w_in"] = _mm(x1b, dproj1, ta=True, name="ssm_in_dw")[None, :, :W["ssm_w_in"].shape[2]]
    dx1 = _mm(dproj1, ssm_in, tb=True, add=dz1, add_scale=DEEPNORM_ALPHA, name="ssm_in_dx")

    dz0, dz0b, g0 = ffn_ple_bwd(0, dx1, s0, "0")
    G["attn_w_out"] = _mm(ob, dz0b, ta=True, name="attn_out_dw")[None]
    do = _mm(dz0b, W["attn_w_out"][0], tb=True, name="attn_out_dx")
    doa, dob, dlta, dltb = _attn_bwd_prep(do, o, name="attn_bwd_prep")
    dq, dk, dv, dcT, dcq = _attn_bwd(qa, qb, kk, ka, kb, vv, doa, dob, lsea, lseb, dlta, dltb, cT, name="attn_bwd")
    dcq = dcq.reshape(T, HEAD_PAIRS, 2, HALF)[:, :, :, 0].reshape(T, ATTN_HEADS)
    dc_col = jnp.pad(dcT.reshape(ATTN_HEADS, T).T + dcq, ((0, 0), (0, LANE - ATTN_HEADS)))
    dfl, dbf = _fox_gate_bwd(dc_col, proj0, bf, name="fox_gate_bwd")
    G["attn_b_f"] = dbf[:, :ATTN_HEADS]
    dproj0 = jnp.concatenate([dq.astype(BF16), dk.astype(BF16), dv.astype(BF16), dfl.astype(BF16)], axis=1)
    G["attn_w_in"] = _mm(xb, dproj0, ta=True, name="attn_in_dw")[None, :, :W["attn_w_in"].shape[2]]
    grad_x = _mm(dproj0, attn_in, tb=True, add=dz0, add_scale=DEEPNORM_ALPHA, name="attn_in_dx")

    for k in g0:
        G[k] = jnp.stack([g0[k].reshape(W[k].shape[1:]), g1[k].reshape(W[k].shape[1:])])
    return loss, grad_x, G


def kernel(x, p, attn_w_in, attn_b_f, attn_w_out, ssm_w_in, ssm_conv_w, ssm_conv_b, ssm_dt_bias, ssm_A_log, ssm_D, ssm_norm_w, ssm_w_out, ln_mix_g, ln_mix_b, ffn_w_up, ffn_conv_w, ffn_conv_b, ffn_w_down, ln_ffn_g, ln_ffn_b, ple_w_proj, ple_w_gate, ple_b_gate, loss_target, m_attn_w_in, m_attn_b_f, m_attn_w_out, m_ssm_w_in, m_ssm_conv_w, m_ssm_conv_b, m_ssm_dt_bias, m_ssm_A_log, m_ssm_D, m_ssm_norm_w, m_ssm_w_out, m_ln_mix_g, m_ln_mix_b, m_ffn_w_up, m_ffn_conv_w, m_ffn_conv_b, m_ffn_w_down, m_ln_ffn_g, m_ln_ffn_b, m_ple_w_proj, m_ple_w_gate, m_ple_b_gate, v_attn_w_in, v_attn_b_f, v_attn_w_out, v_ssm_w_in, v_ssm_conv_w, v_ssm_conv_b, v_ssm_dt_bias, v_ssm_A_log, v_ssm_D, v_ssm_norm_w, v_ssm_w_out, v_ln_mix_g, v_ln_mix_b, v_ffn_w_up, v_ffn_conv_w, v_ffn_conv_b, v_ffn_w_down, v_ln_ffn_g, v_ln_ffn_b, v_ple_w_proj, v_ple_w_gate, v_ple_b_gate):
    names = [n for n, _ in WEIGHTS]
    axes = dict(WEIGHTS)
    w_loc = dict(zip(names, [attn_w_in, attn_b_f, attn_w_out, ssm_w_in, ssm_conv_w, ssm_conv_b, ssm_dt_bias, ssm_A_log, ssm_D, ssm_norm_w, ssm_w_out, ln_mix_g, ln_mix_b, ffn_w_up, ffn_conv_w, ffn_conv_b, ffn_w_down, ln_ffn_g, ln_ffn_b, ple_w_proj, ple_w_gate, ple_b_gate]))
    m_loc = dict(zip(names, [m_attn_w_in, m_attn_b_f, m_attn_w_out, m_ssm_w_in, m_ssm_conv_w, m_ssm_conv_b, m_ssm_dt_bias, m_ssm_A_log, m_ssm_D, m_ssm_norm_w, m_ssm_w_out, m_ln_mix_g, m_ln_mix_b, m_ffn_w_up, m_ffn_conv_w, m_ffn_conv_b, m_ffn_w_down, m_ln_ffn_g, m_ln_ffn_b, m_ple_w_proj, m_ple_w_gate, m_ple_b_gate]))
    v_loc = dict(zip(names, [v_attn_w_in, v_attn_b_f, v_attn_w_out, v_ssm_w_in, v_ssm_conv_w, v_ssm_conv_b, v_ssm_dt_bias, v_ssm_A_log, v_ssm_D, v_ssm_norm_w, v_ssm_w_out, v_ln_mix_g, v_ln_mix_b, v_ffn_w_up, v_ffn_conv_w, v_ffn_conv_b, v_ffn_w_down, v_ln_ffn_g, v_ln_ffn_b, v_ple_w_proj, v_ple_w_gate, v_ple_b_gate]))
    sharded = [n for n in names if axes[n] is not None]
    matrices = [n for n in sharded if n in MATMUL_WEIGHTS]

    def wire(n):
        if n in matrices:
            return w_loc[n].astype(BF16)
        return lax.bitcast_convert_type(w_loc[n], BF16)

    wired = [wire(n) for n in sharded]
    me_chip = 2 * lax.axis_index("x") + lax.axis_index("y")
    packed = _pack(wired)
    gathered = lax.dynamic_update_index_in_dim(_gather_chips(packed, name="gather_weights"), packed, me_chip, 0)
    W = dict(w_loc)
    per_chip = [_unpack(gathered[k], [w.shape for w in wired]) for k in range(N_CHIPS)]
    for i, n in enumerate(sharded):
        pieces = [per_chip[k][i] for k in range(N_CHIPS)]
        if n not in matrices:
            pieces = [lax.bitcast_convert_type(q, F32) for q in pieces]
        W[n] = jnp.concatenate(pieces, axis=axes[n])

    loss, grad_x, G = _local_step(x[0], p[:, 0], loss_target[0], W)
    loss = lax.psum(loss, ("x", "y", "c"))

    def slot(k):
        parts = []
        for n in names:
            g = G[n].reshape(W[n].shape)
            if axes[n] is not None:
                size = w_loc[n].shape[axes[n]]
                g = lax.slice_in_dim(g, k * size, (k + 1) * size, axis=axes[n])
            parts.append(g)
        return _pack(parts)

    contrib = jnp.stack([slot(k) for k in range(N_CHIPS)])
    R = contrib.shape[1]
    H = R // 2
    c = lax.axis_index("c")
    keep = lax.dynamic_slice_in_dim(contrib, c * H, H, axis=1)
    pair = _add2(keep, _pair_swap(contrib, other_half=True, name="grad_pair_swap"), BF16, name="grad_pair_sum")
    from_chips = lax.dynamic_update_index_in_dim(
        _chip_exchange(pair, name="grad_chip_exchange"), lax.dynamic_index_in_dim(pair, me_chip, 0, keepdims=False),
        me_chip, 0)
    half = _sum4(from_chips, name="grad_chip_sum")
    other = _pair_swap(half, name="grad_half_swap")
    gflat = jnp.concatenate([jnp.where(c == 0, half, other), jnp.where(c == 0, other, half)])

    grads = _unpack(gflat, [w_loc[n].shape for n in names])
    steps = [_adamw(w_loc[n], g, m_loc[n], v_loc[n], name=f"adamw_{n}") for n, g in zip(names, grads)]
    return (loss, grad_x[None], *grads, *[s[0] for s in steps], *[s[1] for s in steps], *[s[2] for s in steps])
```

```python
import functools
import math

import jax
import jax.numpy as jnp
from jax import lax
from jax.experimental import pallas as pl
from jax.experimental.pallas import tpu as pltpu

F32 = jnp.float32
BF16 = jnp.bfloat16
MESH = pl.DeviceIdType.MESH

D_MODEL = 1024
ATTN_HEADS = 16
HEAD_PAIRS = ATTN_HEADS // 2
SSM_D_INNER = 2048
SSM_HEADS = 32
SSM_GROUPS = 8
SSM_STATE = 128
SSM_CONV = 4
SSM_CHUNK = 128
SSM_XBC = SSM_D_INNER + 2 * SSM_GROUPS * SSM_STATE
FFN_DIM = 2816
FFN_CONV = 3
DEPTH = 2
LN_EPS = 1e-5
RMS_EPS = 1e-5
DEEPNORM_ALPHA = (2 * DEPTH) ** 0.25
ADAM_LR = 0.001
ADAM_B1 = 0.9
ADAM_B2 = 0.999
ADAM_EPS = 1e-08
ADAM_WD = 0.01
ADAM_STEP = 10

LANE = 128
SUBLANE = 8
HALO = SUBLANE
NEG = -1e30
ATTN_IN_PAD = 3 * D_MODEL + LANE
SSM_IN_PAD = 2 * SSM_D_INNER + 2 * SSM_GROUPS * SSM_STATE + LANE
PACK_COLS = 1024
PACK_ROW_ALIGN = 512

ATTN_BLOCK = 1024
ROW_BLOCK = 512
NARROW_ROW_BLOCK = 1024
CUM_BLOCK = 256


def _params(sem, vmem_mb=48):
    return pltpu.CompilerParams(dimension_semantics=sem, vmem_limit_bytes=vmem_mb * 2 ** 20)


def _pick(n, target, mult=LANE):
    best = None
    d = mult
    while d <= min(n, target):
        if n % d == 0:
            best = d
        d += mult
    return n if best is None else best


def _sigmoid(x):
    return 1.0 / (1.0 + jnp.exp(-x))


def _log1p(u):
    w = 1.0 + u
    return jnp.where(w == 1.0, u, jnp.log(w) * (u / (w - 1.0)))


def _softplus(x):
    return jnp.maximum(x, 0.0) + _log1p(jnp.exp(-jnp.abs(x)))


def _split3(x):
    hi = x.astype(BF16)
    r1 = x - hi.astype(F32)
    mid = r1.astype(BF16)
    lo = (r1 - mid.astype(F32)).astype(BF16)
    return hi, mid, lo


def _tri_matmul(tri, x):
    out = None
    for part in _split3(x):
        t = jnp.dot(tri, part, preferred_element_type=F32)
        out = t if out is None else out + t
    return out


def _tri(n, lower):
    r = lax.broadcasted_iota(jnp.int32, (n, n), 0)
    c = lax.broadcasted_iota(jnp.int32, (n, n), 1)
    return jnp.where((c <= r) if lower else (c >= r), 1.0, 0.0).astype(BF16)


_ANY = pl.BlockSpec(memory_space=pl.ANY)
MM_OUT_BLOCK_BYTES = 13 * 2 ** 20
MM_IN_BLOCK_BYTES = 6 * 2 ** 20
MM_WIDE_K = 3200


def _mm(a, b, *, name, ta=False, tb=False, add=None, add_scale=1.0, out_dtype=F32):
    if ta:
        K, M = a.shape
    else:
        M, K = a.shape
    if tb:
        N, Kb = b.shape
    else:
        Kb, N = b.shape
    assert K == Kb, (a.shape, b.shape, ta, tb)
    if ta:
        assert add is None and out_dtype == F32
        bm = _pick(M, 2816)
        bn = _pick(N, MM_OUT_BLOCK_BYTES // (4 * bm))
        bk = _pick(K, max(512, MM_IN_BLOCK_BYTES // (2 * max(bm, bn))))
    else:
        bm = _pick(M, 1024)
        bn = _pick(N, 1536 if K <= MM_WIDE_K else 512)
        bk = K
    nk = K // bk
    a_spec = pl.BlockSpec((bk, bm), lambda i, j, k: (k, i)) if ta else pl.BlockSpec((bm, bk), lambda i, j, k: (i, k))
    b_spec = pl.BlockSpec((bn, bk), lambda i, j, k: (j, k)) if tb else pl.BlockSpec((bk, bn), lambda i, j, k: (k, j))
    o_spec = pl.BlockSpec((bm, bn), lambda i, j, k: (i, j))
    dims = (((0 if ta else 1,), (1 if tb else 0,)), ((), ()))
    has_add = add is not None

    def kern(*refs):
        a_ref, b_ref = refs[0], refs[1]
        add_ref = refs[2] if has_add else None
        o_ref = refs[3] if has_add else refs[2]
        k = pl.program_id(2)
        part = lax.dot_general(a_ref[...].astype(BF16), b_ref[...].astype(BF16), dims, preferred_element_type=F32)
        if nk == 1:
            o_ref[...] = (part + add_scale * add_ref[...] if has_add else part).astype(out_dtype)
        else:
            @pl.when(k == 0)
            def _():
                o_ref[...] = part

            @pl.when(k > 0)
            def _():
                o_ref[...] += part

    ins = [a, b] + ([add] if has_add else [])
    in_specs = [a_spec, b_spec] + ([o_spec] if has_add else [])
    return pl.pallas_call(
        kern, name=name, grid=(M // bm, N // bn, nk),
        in_specs=in_specs, out_specs=o_spec,
        out_shape=jax.ShapeDtypeStruct((M, N), out_dtype),
        compiler_params=_params(("parallel", "parallel", "arbitrary"), vmem_mb=56),
    )(*ins)


def _ln_stats(z):
    mu = jnp.mean(z, axis=-1, keepdims=True)
    zc = z - mu
    var = jnp.mean(zc * zc, axis=-1, keepdims=True)
    return zc, lax.rsqrt(var + LN_EPS)


def _ln_fwd(x, r, g, b, *, name):
    T, D = x.shape
    bt = _pick(T, ROW_BLOCK, SUBLANE)

    def kern(x_ref, r_ref, g_ref, b_ref, z_ref, h_ref, hb_ref):
        z = DEEPNORM_ALPHA * x_ref[...] + r_ref[...]
        zc, rstd = _ln_stats(z)
        h = zc * rstd * g_ref[...] + b_ref[...]
        z_ref[...] = z
        h_ref[...] = h
        hb_ref[...] = h.astype(BF16)

    row = pl.BlockSpec((bt, D), lambda i: (i, 0))
    vec = pl.BlockSpec((1, D), lambda i: (0, 0))
    return pl.pallas_call(
        kern, name=name, grid=(T // bt,), in_specs=[row, row, vec, vec], out_specs=[row, row, row],
        out_shape=[jax.ShapeDtypeStruct((T, D), F32)] * 2 + [jax.ShapeDtypeStruct((T, D), BF16)],
        compiler_params=_params(("parallel",)),
    )(x, r, g, b)


def _ln_bwd(dy, z, g, *, name):
    T, D = z.shape
    bt = _pick(T, ROW_BLOCK, SUBLANE)

    def kern(dy_ref, z_ref, g_ref, dz_ref, dzb_ref, dg_ref, db_ref):
        i = pl.program_id(0)
        zc, rstd = _ln_stats(z_ref[...])
        xhat = zc * rstd
        dyv = dy_ref[...]
        dxh = dyv * g_ref[...]
        m1 = jnp.mean(dxh, axis=-1, keepdims=True)
        m2 = jnp.mean(dxh * xhat, axis=-1, keepdims=True)
        dz = rstd * (dxh - m1 - xhat * m2)
        dz_ref[...] = dz
        dzb_ref[...] = dz.astype(BF16)

        @pl.when(i == 0)
        def _():
            dg_ref[...] = jnp.zeros_like(dg_ref)
            db_ref[...] = jnp.zeros_like(db_ref)

        dg_ref[...] += jnp.sum(dyv * xhat, axis=0, keepdims=True)
        db_ref[...] += jnp.sum(dyv, axis=0, keepdims=True)

    row = pl.BlockSpec((bt, D), lambda i: (i, 0))
    vec = pl.BlockSpec((1, D), lambda i: (0, 0))
    return pl.pallas_call(
        kern, name=name, grid=(T // bt,), in_specs=[row, row, vec], out_specs=[row, row, vec, vec],
        out_shape=[jax.ShapeDtypeStruct((T, D), F32), jax.ShapeDtypeStruct((T, D), BF16),
                   jax.ShapeDtypeStruct((1, D), F32), jax.ShapeDtypeStruct((1, D), F32)],
        compiler_params=_params(("arbitrary",)),
    )(dy, z, g)


def _past_taps(ext_ref, K, bt):
    ext = ext_ref[...]
    return [(ext if k == K - 1 else pltpu.roll(ext, K - 1 - k, 0))[HALO:HALO + bt] for k in range(K)]


def _conv_past(taps, cw_ref):
    out = None
    for k, tap in enumerate(taps):
        term = cw_ref[k:k + 1, :] * tap
        out = term if out is None else out + term
    return out


def _fill_ext_past(ext_ref, halo_ref, cur, i, bt):
    ext_ref[pl.ds(0, HALO), :] = jnp.where(i > 0, halo_ref[...], 0.0)
    ext_ref[pl.ds(HALO, bt), :] = cur


def _halo_prev(bt, bc, off):
    return pl.BlockSpec((HALO, bc), lambda i, j: (jnp.maximum(i * (bt // HALO) - 1, 0), j + off))


def _normal_cdf(x):
    return 0.5 * (1.0 + lax.erf(x * (1.0 / math.sqrt(2.0))))


def _gelu(x):
    return x * _normal_cdf(x)


def _gelu_and_grad(x):
    cdf = _normal_cdf(x)
    return x * cdf, cdf + x * jnp.exp(-0.5 * x * x) * (1.0 / math.sqrt(2.0 * math.pi))


def _ffn_act_fwd(up, cw, cb, *, name):
    T, F2 = up.shape
    F = F2 // 2
    bt = _pick(T, ROW_BLOCK, SUBLANE)
    bc = _pick(F, 1408)
    nb = F // bc

    def kern(u_ref, g_ref, halo_ref, cw_ref, cb_ref, a_ref, ext_ref):
        i = pl.program_id(0)
        _fill_ext_past(ext_ref, halo_ref, g_ref[...], i, bt)
        gc = cb_ref[...] + _conv_past(_past_taps(ext_ref, FFN_CONV, bt), cw_ref)
        a_ref[...] = (_gelu(gc) * u_ref[...]).astype(BF16)

    return pl.pallas_call(
        kern, name=name, grid=(T // bt, nb),
        in_specs=[pl.BlockSpec((bt, bc), lambda i, j: (i, j)),
                  pl.BlockSpec((bt, bc), lambda i, j: (i, j + nb)),
                  _halo_prev(bt, bc, nb),
                  pl.BlockSpec((FFN_CONV, bc), lambda i, j: (0, j)),
                  pl.BlockSpec((1, bc), lambda i, j: (0, j))],
        out_specs=pl.BlockSpec((bt, bc), lambda i, j: (i, j)),
        out_shape=jax.ShapeDtypeStruct((T, F), BF16),
        scratch_shapes=[pltpu.VMEM((bt + HALO, bc), F32)],
        compiler_params=_params(("parallel", "parallel")),
    )(up, up, up, cw, cb)


def _ffn_act_bwd(da, up, cw, cb, *, name):
    T, F2 = up.shape
    F = F2 // 2
    bt = _pick(T, ROW_BLOCK, SUBLANE)
    bc = _pick(F, 1408)
    nb = F // bc
    K = FFN_CONV

    def kern(da_ref, u_ref, g_ref, halo_ref, cw_ref, cb_ref, du_ref, dgc_ref, dcb_ref, dcw_ref, ext_ref):
        i = pl.program_id(1)
        _fill_ext_past(ext_ref, halo_ref, g_ref[...], i, bt)
        taps = _past_taps(ext_ref, K, bt)
        gc = cb_ref[...] + _conv_past(taps, cw_ref)
        dav = da_ref[...]
        act, act_grad = _gelu_and_grad(gc)
        du_ref[...] = (dav * act).astype(BF16)
        dgc = dav * u_ref[...] * act_grad
        dgc_ref[...] = dgc

        @pl.when(i == 0)
        def _():
            dcb_ref[...] = jnp.zeros_like(dcb_ref)
            dcw_ref[...] = jnp.zeros_like(dcw_ref)

        dcb_ref[...] += jnp.sum(dgc, axis=0, keepdims=True)
        for k in range(K):
            dcw_ref[k:k + 1, :] += jnp.sum(dgc * taps[k], axis=0, keepdims=True)

    blk = pl.BlockSpec((bt, bc), lambda j, i: (i, j))
    return pl.pallas_call(
        kern, name=name, grid=(nb, T // bt),
        in_specs=[blk, blk,
                  pl.BlockSpec((bt, bc), lambda j, i: (i, j + nb)),
                  pl.BlockSpec((HALO, bc), lambda j, i: (jnp.maximum(i * (bt // HALO) - 1, 0), j + nb)),
                  pl.BlockSpec((K, bc), lambda j, i: (0, j)),
                  pl.BlockSpec((1, bc), lambda j, i: (0, j))],
        out_specs=[blk, blk, pl.BlockSpec((1, bc), lambda j, i: (0, j)), pl.BlockSpec((K, bc), lambda j, i: (0, j))],
        out_shape=[jax.ShapeDtypeStruct((T, F2), BF16), jax.ShapeDtypeStruct((T, F), F32),
                   jax.ShapeDtypeStruct((1, F), F32), jax.ShapeDtypeStruct((K, F), F32)],
        scratch_shapes=[pltpu.VMEM((bt + HALO, bc), F32)],
        compiler_params=_params(("parallel", "arbitrary")),
    )(da, up, up, up, cw, cb)


def _dwconv_bwd_data(dgc, cw, K, into, col, *, name):
    T, C = dgc.shape
    bt = _pick(T, ROW_BLOCK, SUBLANE)
    bc = _pick(C, 1408)
    nt = T // bt
    last_halo = T // HALO - 1
    off = col // bc
    assert off * bc == col

    def kern(d_ref, halo_ref, cw_ref, into_ref, o_ref, ext_ref):
        i = pl.program_id(0)
        ext_ref[pl.ds(0, bt), :] = d_ref[...]
        ext_ref[pl.ds(bt, HALO), :] = jnp.where(i < nt - 1, halo_ref[...], 0.0)
        ext = ext_ref[...]
        out = None
        for k in range(K):
            ahead = K - 1 - k
            tap = (ext if ahead == 0 else pltpu.roll(ext, bt + HALO - ahead, 0))[0:bt]
            term = cw_ref[k:k + 1, :] * tap
            out = term if out is None else out + term
        o_ref[...] = out.astype(o_ref.dtype)

    return pl.pallas_call(
        kern, name=name, grid=(nt, C // bc),
        in_specs=[pl.BlockSpec((bt, bc), lambda i, j: (i, j)),
                  pl.BlockSpec((HALO, bc), lambda i, j: (jnp.minimum((i + 1) * (bt // HALO), last_halo), j)),
                  pl.BlockSpec((K, bc), lambda i, j: (0, j)), _ANY],
        out_specs=pl.BlockSpec((bt, bc), lambda i, j: (i, j + off)),
        out_shape=jax.ShapeDtypeStruct(into.shape, into.dtype), input_output_aliases={3: 0},
        scratch_shapes=[pltpu.VMEM((bt + HALO, bc), F32)],
        compiler_params=_params(("parallel", "parallel")),
    )(dgc, dgc, cw, into)


def _ple_fwd(h, G, bg, pp, *, name):
    T, D = h.shape
    bt = _pick(T, ROW_BLOCK, SUBLANE)

    def kern(h_ref, G_ref, bg_ref, pp_ref, o_ref, ob_ref):
        out = h_ref[...] + _sigmoid(G_ref[...] + bg_ref[...]) * pp_ref[...]
        o_ref[...] = out
        ob_ref[...] = out.astype(BF16)

    row = pl.BlockSpec((bt, D), lambda i: (i, 0))
    vec = pl.BlockSpec((1, D), lambda i: (0, 0))
    return pl.pallas_call(
        kern, name=name, grid=(T // bt,), in_specs=[row, row, vec, row], out_specs=[row, row],
        out_shape=[jax.ShapeDtypeStruct((T, D), F32), jax.ShapeDtypeStruct((T, D), BF16)],
        compiler_params=_params(("parallel",)),
    )(h, G, bg, pp)


def _ple_bwd(dx, G, bg, pp, *, name):
    T, D = dx.shape
    bt = _pick(T, ROW_BLOCK, SUBLANE)

    def kern(dx_ref, G_ref, bg_ref, pp_ref, dG_ref, dpp_ref, dbg_ref):
        i = pl.program_id(0)
        gate = _sigmoid(G_ref[...] + bg_ref[...])
        dxv = dx_ref[...]
        dG = dxv * pp_ref[...] * gate * (1.0 - gate)
        dG_ref[...] = dG.astype(BF16)
        dpp_ref[...] = (dxv * gate).astype(BF16)

        @pl.when(i == 0)
        def _():
            dbg_ref[...] = jnp.zeros_like(dbg_ref)

        dbg_ref[...] += jnp.sum(dG, axis=0, keepdims=True)

    row = pl.BlockSpec((bt, D), lambda i: (i, 0))
    vec = pl.BlockSpec((1, D), lambda i: (0, 0))
    return pl.pallas_call(
        kern, name=name, grid=(T // bt,), in_specs=[row, row, vec, row], out_specs=[row, row, vec],
        out_shape=[jax.ShapeDtypeStruct((T, D), BF16), jax.ShapeDtypeStruct((T, D), BF16),
                   jax.ShapeDtypeStruct((1, D), F32)],
        compiler_params=_params(("arbitrary",)),
    )(dx, G, bg, pp)


def _fox_gate_fwd(proj, bf, *, name):
    T = proj.shape[0]
    bt = _pick(T, CUM_BLOCK, SUBLANE)
    fcol = 3 * D_MODEL // LANE

    def kern(f_ref, bf_ref, c_ref, carry_ref):
        i = pl.program_id(0)

        @pl.when(i == 0)
        def _():
            carry_ref[...] = jnp.zeros_like(carry_ref)

        x = f_ref[...] + bf_ref[...]
        lf = jnp.minimum(x, 0.0) - _log1p(jnp.exp(-jnp.abs(x)))
        cs = _tri_matmul(_tri(bt, True), lf) + carry_ref[...]
        c_ref[...] = cs
        carry_ref[...] = cs[bt - 1:bt, :]

    return pl.pallas_call(
        kern, name=name, grid=(T // bt,),
        in_specs=[pl.BlockSpec((bt, LANE), lambda i: (i, fcol)), pl.BlockSpec((1, LANE), lambda i: (0, 0))],
        out_specs=pl.BlockSpec((bt, LANE), lambda i: (i, 0)),
        out_shape=jax.ShapeDtypeStruct((T, LANE), F32),
        scratch_shapes=[pltpu.VMEM((1, LANE), F32)],
        compiler_params=_params(("arbitrary",)),
    )(proj, bf)


def _fox_gate_bwd(dc, proj, bf, *, name):
    T = proj.shape[0]
    bt = _pick(T, CUM_BLOCK, SUBLANE)
    nb = T // bt
    fcol = 3 * D_MODEL // LANE

    def kern(dc_ref, f_ref, bf_ref, df_ref, dbf_ref, carry_ref):
        i = pl.program_id(0)

        @pl.when(i == 0)
        def _():
            carry_ref[...] = jnp.zeros_like(carry_ref)
            dbf_ref[...] = jnp.zeros_like(dbf_ref)

        dlf = _tri_matmul(_tri(bt, False), dc_ref[...]) + carry_ref[...]
        carry_ref[...] = dlf[0:1, :]
        x = f_ref[...] + bf_ref[...]
        lane = lax.broadcasted_iota(jnp.int32, (bt, LANE), 1)
        df = jnp.where(lane < ATTN_HEADS, dlf / (1.0 + jnp.exp(x)), 0.0)
        df_ref[...] = df
        dbf_ref[...] += jnp.sum(df, axis=0, keepdims=True)

    return pl.pallas_call(
        kern, name=name, grid=(nb,),
        in_specs=[pl.BlockSpec((bt, LANE), lambda i: (nb - 1 - i, 0)),
                  pl.BlockSpec((bt, LANE), lambda i: (nb - 1 - i, fcol)),
                  pl.BlockSpec((1, LANE), lambda i: (0, 0))],
        out_specs=[pl.BlockSpec((bt, LANE), lambda i: (nb - 1 - i, 0)), pl.BlockSpec((1, LANE), lambda i: (0, 0))],
        out_shape=[jax.ShapeDtypeStruct((T, LANE), F32), jax.ShapeDtypeStruct((1, LANE), F32)],
        scratch_shapes=[pltpu.VMEM((1, LANE), F32)],
        compiler_params=_params(("arbitrary",)),
    )(dc, proj, bf)


_NT = (((1,), (1,)), ((), ()))
_TN = (((0,), (0,)), ((), ()))


def _dot(a, b, dims=None):
    if dims is None:
        return jnp.dot(a, b, preferred_element_type=F32)
    return lax.dot_general(a, b, dims, preferred_element_type=F32)


LOG2E = 1.0 / math.log(2.0)
LN2 = math.log(2.0)
Q_SCALE = 0.125 * LOG2E
HALF = LANE // 2
L_LANE = (HALF, 0)
FWD_PAIRS = 2
BWD_PAIRS = 1


def _attn_prep(proj, *, name):
    T = proj.shape[0]
    bt = _pick(T, ATTN_BLOCK)

    def kern(q_ref, k_ref, v_ref, qa_ref, qb_ref, kk_ref, ka_ref, kb_ref, vv_ref, va_ref, vb_ref):
        lane = lax.broadcasted_iota(jnp.int32, (bt, LANE), 1)
        lo = lane < HALF
        q = q_ref[...] * Q_SCALE
        k = k_ref[...]
        v = v_ref[...]
        qa_ref[...] = jnp.where(lo, q, 0.0).astype(BF16)
        qb_ref[...] = jnp.where(lo, 0.0, q).astype(BF16)
        kk_ref[...] = k.astype(BF16)
        ka_ref[...] = jnp.where(lo, k, 0.0).astype(BF16)
        kb_ref[...] = jnp.where(lo, 0.0, k).astype(BF16)
        vv_ref[...] = v.astype(BF16)
        va_ref[...] = jnp.where(lo, v, jnp.where(lane == L_LANE[0], 1.0, 0.0)).astype(BF16)
        vb_ref[...] = jnp.where(lo, jnp.where(lane == L_LANE[1], 1.0, 0.0), v).astype(BF16)

    kcol, vcol = D_MODEL // LANE, 2 * D_MODEL // LANE
    out = pl.BlockSpec((bt, LANE), lambda i, hp: (i, hp))
    return pl.pallas_call(
        kern, name=name, grid=(T // bt, HEAD_PAIRS),
        in_specs=[out, pl.BlockSpec((bt, LANE), lambda i, hp: (i, kcol + hp)),
                  pl.BlockSpec((bt, LANE), lambda i, hp: (i, vcol + hp))],
        out_specs=[out] * 8, out_shape=[jax.ShapeDtypeStruct((T, D_MODEL), BF16)] * 8,
        compiler_params=_params(("parallel", "parallel")),
    )(proj, proj, proj)


def _attn_fwd(qa, qb, kk, va, vb, cT, *, name):
    T = qa.shape[0]
    tb = _pick(T, ATTN_BLOCK)
    nq = T // tb
    rep = tb // LANE
    width = FWD_PAIRS * LANE

    def kern(qa_ref, qb_ref, k_ref, va_ref, vb_ref, c_ref, o_ref, ob_ref, lsea_ref, lseb_ref, m_ref, acc_ref):
        qi = pl.program_id(1)
        ki = pl.program_id(2)

        @pl.when(ki == 0)
        def _():
            m_ref[...] = jnp.full_like(m_ref, NEG)
            acc_ref[...] = jnp.zeros_like(acc_ref)

        def step(diag):
            for pp in range(FWD_PAIRS):
                cols = slice(pp * LANE, (pp + 1) * LANE)
                k = k_ref[:, cols]
                for h, (q_ref, v_ref) in enumerate(((qa_ref, va_ref), (qb_ref, vb_ref))):
                    i = 2 * pp + h
                    s = _dot(q_ref[:, cols], k, _NT) - c_ref[pp, h:h + 1, :]
                    if diag:
                        r = lax.broadcasted_iota(jnp.int32, (tb, tb), 0)
                        c = lax.broadcasted_iota(jnp.int32, (tb, tb), 1)
                        s = jnp.where(c <= r, s, NEG)
                    m_prev = m_ref[i]
                    m_new = jnp.maximum(m_prev, jnp.max(s, axis=1, keepdims=True))
                    p = jnp.exp2(s - jnp.tile(m_new, (1, rep)))
                    acc_ref[i] = acc_ref[i] * jnp.exp2(m_prev - m_new) + _dot(p.astype(BF16), v_ref[:, cols])
                    m_ref[i] = m_new

        @pl.when(ki < qi)
        def _():
            step(False)

        @pl.when(ki == qi)
        def _():
            step(True)
            lo = lax.broadcasted_iota(jnp.int32, (tb, LANE), 1) < HALF
            for pp in range(FWD_PAIRS):
                cols = slice(pp * LANE, (pp + 1) * LANE)
                a0, a1 = acc_ref[2 * pp], acc_ref[2 * pp + 1]
                l0 = a0[:, L_LANE[0]:L_LANE[0] + 1]
                l1 = a1[:, L_LANE[1]:L_LANE[1] + 1]
                o = jnp.where(lo, a0 / l0, a1 / l1)
                o_ref[:, cols] = o
                ob_ref[:, cols] = o.astype(BF16)
                lsea_ref[:, cols] = m_ref[2 * pp] + jnp.log(l0) * LOG2E
                lseb_ref[:, cols] = m_ref[2 * pp + 1] + jnp.log(l1) * LOG2E

    qspec = pl.BlockSpec((tb, width), lambda g, qi, ki: (qi, g))
    kspec = pl.BlockSpec((tb, width), lambda g, qi, ki: (jnp.minimum(ki, qi), g))
    return pl.pallas_call(
        kern, name=name, grid=(HEAD_PAIRS // FWD_PAIRS, nq, nq),
        in_specs=[qspec, qspec, kspec, kspec, kspec,
                  pl.BlockSpec((FWD_PAIRS, 2, tb), lambda g, qi, ki: (g, 0, jnp.minimum(ki, qi)))],
        out_specs=[qspec, qspec, qspec, qspec],
        out_shape=[jax.ShapeDtypeStruct((T, D_MODEL), F32), jax.ShapeDtypeStruct((T, D_MODEL), BF16),
                   jax.ShapeDtypeStruct((T, D_MODEL), F32), jax.ShapeDtypeStruct((T, D_MODEL), F32)],
        scratch_shapes=[pltpu.VMEM((2 * FWD_PAIRS, tb, LANE), F32), pltpu.VMEM((2 * FWD_PAIRS, tb, LANE), F32)],
        compiler_params=_params(("parallel", "parallel", "arbitrary")),
    )(qa, qb, kk, va, vb, cT)


def _attn_bwd_prep(do, o, *, name):
    T, D = do.shape
    bt = _pick(T, ATTN_BLOCK)

    def kern(do_ref, o_ref, doa_ref, dob_ref, dlta_ref, dltb_ref):
        lo = lax.broadcasted_iota(jnp.int32, (bt, LANE), 1) < HALF
        dov = do_ref[...]
        prod = dov * o_ref[...]
        doa_ref[...] = jnp.where(lo, dov, 0.0).astype(BF16)
        dob_ref[...] = jnp.where(lo, 0.0, dov).astype(BF16)
        dlta_ref[...] = jnp.broadcast_to(jnp.sum(jnp.where(lo, prod, 0.0), axis=1, keepdims=True), (bt, LANE))
        dltb_ref[...] = jnp.broadcast_to(jnp.sum(jnp.where(lo, 0.0, prod), axis=1, keepdims=True), (bt, LANE))

    blk = pl.BlockSpec((bt, LANE), lambda i, hp: (i, hp))
    return pl.pallas_call(
        kern, name=name, grid=(T // bt, HEAD_PAIRS), in_specs=[blk, blk], out_specs=[blk] * 4,
        out_shape=[jax.ShapeDtypeStruct((T, D), BF16)] * 2 + [jax.ShapeDtypeStruct((T, D), F32)] * 2,
        compiler_params=_params(("parallel", "parallel")),
    )(do, o)


def _attn_bwd(qa, qb, kk, ka, kb, vv, doa, dob, lsea, lseb, dlta, dltb, cT, *, name):
    T = qa.shape[0]
    tb = _pick(T, ATTN_BLOCK)
    nq = T // tb
    rep = tb // LANE
    width = BWD_PAIRS * LANE

    def kern(qa_ref, qb_ref, k_ref, ka_ref, kb_ref, v_ref, doa_ref, dob_ref, lsea_ref, lseb_ref, dlta_ref, dltb_ref,
             c_ref, dq_ref, dk_ref, dv_ref, dc_ref, dcq_ref):
        g = pl.program_id(0)
        ki = pl.program_id(1)
        qi = pl.program_id(2)
        first = jnp.logical_and(ki == 0, qi == 0)

        @pl.when(first)
        def _():
            dq_ref[...] = jnp.zeros_like(dq_ref)

        @pl.when(jnp.logical_and(first, g == 0))
        def _():
            dcq_ref[...] = jnp.zeros_like(dcq_ref)

        @pl.when(qi == 0)
        def _():
            dk_ref[...] = jnp.zeros_like(dk_ref)
            dv_ref[...] = jnp.zeros_like(dv_ref)
            dc_ref[...] = jnp.zeros_like(dc_ref)

        def step(diag):
            rows = pl.ds(pl.multiple_of(qi * tb, tb), tb)
            lane = lax.broadcasted_iota(jnp.int32, (tb, LANE), 1)
            row_sums = jnp.zeros((tb, LANE), F32)
            for pp in range(BWD_PAIRS):
                cols = slice(pp * LANE, (pp + 1) * LANE)
                k = k_ref[:, cols]
                v = v_ref[:, cols]
                dq = None
                dk = None
                dv = None
                heads = ((qa_ref, ka_ref, doa_ref, lsea_ref, dlta_ref), (qb_ref, kb_ref, dob_ref, lseb_ref, dltb_ref))
                for h, (q_ref, km_ref, do_ref, lse_ref, dlt_ref) in enumerate(heads):
                    q = q_ref[:, cols]
                    dom = do_ref[:, cols]
                    s = _dot(q, k, _NT) - c_ref[pp, h:h + 1, :]
                    if diag:
                        r = lax.broadcasted_iota(jnp.int32, (tb, tb), 0)
                        c = lax.broadcasted_iota(jnp.int32, (tb, tb), 1)
                        s = jnp.where(c <= r, s, NEG)
                    p = jnp.exp2(s - jnp.tile(lse_ref[:, cols], (1, rep)))
                    ds = p * (_dot(dom, v, _NT) - jnp.tile(dlt_ref[:, cols], (1, rep)))
                    dc_ref[pp, h:h + 1, :] -= jnp.sum(ds, axis=0, keepdims=True)
                    head = 2 * (BWD_PAIRS * g + pp) + h
                    row_sums = jnp.where(lane == head, jnp.sum(ds, axis=1, keepdims=True), row_sums)
                    dsb = ds.astype(BF16)
                    tv = _dot(p.astype(BF16), dom, _TN)
                    tk = _dot(dsb, q, _TN)
                    tq = _dot(dsb, km_ref[:, cols])
                    dv = tv if dv is None else dv + tv
                    dk = tk if dk is None else dk + tk
                    dq = tq if dq is None else dq + tq
                dv_ref[:, cols] += dv
                dk_ref[:, cols] += dk * LN2
                dq_ref[rows, cols] += dq * 0.125
            dcq_ref[rows, :] += row_sums

        @pl.when(qi > ki)
        def _():
            step(False)

        @pl.when(qi == ki)
        def _():
            step(True)

    qspec = pl.BlockSpec((tb, width), lambda g, ki, qi: (jnp.maximum(qi, ki), g))
    kspec = pl.BlockSpec((tb, width), lambda g, ki, qi: (ki, g))
    cspec = pl.BlockSpec((BWD_PAIRS, 2, tb), lambda g, ki, qi: (g, 0, ki))
    qacc = pl.BlockSpec((T, width), lambda g, ki, qi: (0, g), pipeline_mode=pl.Buffered(1))
    cqacc = pl.BlockSpec((T, LANE), lambda g, ki, qi: (0, 0), pipeline_mode=pl.Buffered(1))
    return pl.pallas_call(
        kern, name=name, grid=(HEAD_PAIRS // BWD_PAIRS, nq, nq),
        in_specs=[qspec, qspec, kspec, kspec, kspec, kspec, qspec, qspec, qspec, qspec, qspec, qspec, cspec],
        out_specs=[qacc, kspec, kspec, cspec, cqacc],
        out_shape=[jax.ShapeDtypeStruct((T, D_MODEL), F32)] * 3 + [jax.ShapeDtypeStruct((HEAD_PAIRS, 2, T), F32),
                                                                   jax.ShapeDtypeStruct((T, LANE), F32)],
        compiler_params=_params(("arbitrary", "arbitrary", "arbitrary"), vmem_mb=56),
    )(qa, qb, kk, ka, kb, vv, doa, dob, lsea, lseb, dlta, dltb, cT)


def _ssd_dt_fwd(proj, dt_bias, a_log, *, name):
    T = proj.shape[0]
    Q = SSM_CHUNK
    col = (2 * SSM_D_INNER + 2 * SSM_GROUPS * SSM_STATE) // LANE

    def kern(raw_ref, b_ref, al_ref, dt_ref, ac_ref):
        dt = _softplus(raw_ref[...] + b_ref[...])
        dt_ref[...] = dt
        ac_ref[...] = _tri_matmul(_tri(Q, True), dt * (-jnp.exp(al_ref[...])))

    vec = pl.BlockSpec((1, LANE), lambda i: (0, 0))
    blk = pl.BlockSpec((Q, LANE), lambda i: (i, 0))
    return pl.pallas_call(
        kern, name=name, grid=(T // Q,),
        in_specs=[pl.BlockSpec((Q, LANE), lambda i: (i, col)), vec, vec], out_specs=[blk, blk],
        out_shape=[jax.ShapeDtypeStruct((T, LANE), F32)] * 2,
        compiler_params=_params(("parallel",)),
    )(proj, dt_bias, a_log)


def _ssd_dt_bwd(da_a, da_b, ddt_a, ddt_b, dt, proj, dt_bias, a_log, into, *, name):
    T = proj.shape[0]
    Q = SSM_CHUNK
    col = (2 * SSM_D_INNER + 2 * SSM_GROUPS * SSM_STATE) // LANE

    def kern(daa_ref, dab_ref, dda_ref, ddb_ref, dt_ref, raw_ref, b_ref, al_ref, into_ref, draw_ref, dal_ref, db_ref,
             acc_ref):
        i = pl.program_id(0)

        @pl.when(i == 0)
        def _():
            acc_ref[...] = jnp.zeros_like(acc_ref)
            db_ref[...] = jnp.zeros_like(db_ref)

        A = -jnp.exp(al_ref[...])
        ddA = _tri_matmul(_tri(Q, False), daa_ref[...] + dab_ref[...])
        ddt = dda_ref[...] + ddb_ref[...] + ddA * A
        acc_ref[...] += jnp.sum(ddA * dt_ref[...], axis=0, keepdims=True)
        lane = lax.broadcasted_iota(jnp.int32, (Q, LANE), 1)
        draw = jnp.where(lane < SSM_HEADS, ddt * _sigmoid(raw_ref[...] + b_ref[...]), 0.0)
        draw_ref[...] = draw.astype(BF16)
        db_ref[...] += jnp.sum(draw, axis=0, keepdims=True)
        dal_ref[...] = acc_ref[...] * A

    vec = pl.BlockSpec((1, LANE), lambda i: (0, 0))
    blk = pl.BlockSpec((Q, LANE), lambda i: (i, 0))
    return pl.pallas_call(
        kern, name=name, grid=(T // Q,),
        in_specs=[blk, blk, blk, blk, blk, pl.BlockSpec((Q, LANE), lambda i: (i, col)), vec, vec, _ANY],
        out_specs=[pl.BlockSpec((Q, LANE), lambda i: (i, col)), vec, vec],
        out_shape=[jax.ShapeDtypeStruct(into.shape, into.dtype), jax.ShapeDtypeStruct((1, LANE), F32),
                   jax.ShapeDtypeStruct((1, LANE), F32)],
        input_output_aliases={8: 0},
        scratch_shapes=[pltpu.VMEM((1, LANE), F32)],
        compiler_params=_params(("arbitrary",)),
    )(da_a, da_b, ddt_a, ddt_b, dt, proj, dt_bias, a_log, into)


def _conv_silu_fwd(proj, cw, cb, *, name):
    T = proj.shape[0]
    C = SSM_XBC
    bt = _pick(T, ROW_BLOCK, SUBLANE)
    bc = 1024
    off = SSM_D_INNER // bc

    def kern(x_ref, halo_ref, cw_ref, cb_ref, o_ref, ext_ref):
        i = pl.program_id(0)
        _fill_ext_past(ext_ref, halo_ref, x_ref[...], i, bt)
        pre = cb_ref[...] + _conv_past(_past_taps(ext_ref, SSM_CONV, bt), cw_ref)
        o_ref[...] = pre * _sigmoid(pre)

    return pl.pallas_call(
        kern, name=name, grid=(T // bt, C // bc),
        in_specs=[pl.BlockSpec((bt, bc), lambda i, j: (i, j + off)), _halo_prev(bt, bc, off),
                  pl.BlockSpec((SSM_CONV, bc), lambda i, j: (0, j)), pl.BlockSpec((1, bc), lambda i, j: (0, j))],
        out_specs=pl.BlockSpec((bt, bc), lambda i, j: (i, j)),
        out_shape=jax.ShapeDtypeStruct((T, C), F32),
        scratch_shapes=[pltpu.VMEM((bt + HALO, bc), F32)],
        compiler_params=_params(("parallel", "parallel")),
    )(proj, proj, cw, cb)


def _conv_silu_bwd(dxbc, proj, cw, cb, *, name):
    T = proj.shape[0]
    C = SSM_XBC
    K = SSM_CONV
    bt = _pick(T, ROW_BLOCK, SUBLANE)
    bc = 1024
    off = SSM_D_INNER // bc

    def kern(d_ref, x_ref, halo_ref, cw_ref, cb_ref, dpre_ref, dcb_ref, dcw_ref, ext_ref):
        i = pl.program_id(1)
        _fill_ext_past(ext_ref, halo_ref, x_ref[...], i, bt)
        taps = _past_taps(ext_ref, K, bt)
        pre = cb_ref[...] + _conv_past(taps, cw_ref)
        sg = _sigmoid(pre)
        dpre = d_ref[...] * sg * (1.0 + pre * (1.0 - sg))
        dpre_ref[...] = dpre

        @pl.when(i == 0)
        def _():
            dcb_ref[...] = jnp.zeros_like(dcb_ref)
            dcw_ref[...] = jnp.zeros_like(dcw_ref)

        dcb_ref[...] += jnp.sum(dpre, axis=0, keepdims=True)
        for k in range(K):
            dcw_ref[k:k + 1, :] += jnp.sum(dpre * taps[k], axis=0, keepdims=True)

    blk = pl.BlockSpec((bt, bc), lambda j, i: (i, j))
    return pl.pallas_call(
        kern, name=name, grid=(C // bc, T // bt),
        in_specs=[blk, pl.BlockSpec((bt, bc), lambda j, i: (i, j + off)),
                  pl.BlockSpec((HALO, bc), lambda j, i: (jnp.maximum(i * (bt // HALO) - 1, 0), j + off)),
                  pl.BlockSpec((K, bc), lambda j, i: (0, j)), pl.BlockSpec((1, bc), lambda j, i: (0, j))],
        out_specs=[blk, pl.BlockSpec((1, bc), lambda j, i: (0, j)), pl.BlockSpec((K, bc), lambda j, i: (0, j))],
        out_shape=[jax.ShapeDtypeStruct((T, C), F32), jax.ShapeDtypeStruct((1, C), F32),
                   jax.ShapeDtypeStruct((K, C), F32)],
        scratch_shapes=[pltpu.VMEM((bt + HALO, bc), F32)],
        compiler_params=_params(("parallel", "arbitrary")),
    )(dxbc, proj, proj, cw, cb)


_GP = SSM_D_INNER // SSM_GROUPS
_HPG = SSM_HEADS // SSM_GROUPS
_PH = SSM_D_INNER // SSM_HEADS


def _head_masks(rows):
    lane = lax.broadcasted_iota(jnp.int32, (rows, _GP), 1)
    return [jnp.logical_and(lane >= r * _PH, lane < (r + 1) * _PH) for r in range(_HPG)]


def _ssd_cols(g):
    x0 = g * _GP
    b0 = SSM_D_INNER + g * SSM_STATE
    c0 = SSM_D_INNER + (SSM_GROUPS + g) * SSM_STATE
    return slice(x0, x0 + _GP), slice(b0, b0 + SSM_STATE), slice(c0, c0 + SSM_STATE)


def _ssd_specs(idx):
    Q, N = SSM_CHUNK, SSM_STATE
    return dict(
        xbc=pl.BlockSpec((Q, SSM_XBC), lambda j: (idx(j), 0)),
        x=pl.BlockSpec((Q, SSM_D_INNER), lambda j: (idx(j), 0)),
        col=pl.BlockSpec((Q, LANE), lambda j: (idx(j), 0)),
        row=pl.BlockSpec((SSM_HEADS, Q), lambda j: (0, idx(j))),
        st=pl.BlockSpec((N, SSM_D_INNER), lambda j: (idx(j), 0)),
    )


def _ssd_scan_fwd(xbc, dtc, acc_, dtr, acr, *, name):
    T = xbc.shape[0]
    Q, N = SSM_CHUNK, SSM_STATE
    nc = T // Q
    sp = _ssd_specs(lambda j: j)

    def kern(xbc_ref, dtc_ref, ac_ref, dtr_ref, ar_ref, ys_ref, st_ref, state_ref):
        @pl.when(pl.program_id(0) == 0)
        def _():
            state_ref[...] = jnp.zeros_like(state_ref)

        r_i = lax.broadcasted_iota(jnp.int32, (Q, Q), 0)
        c_i = lax.broadcasted_iota(jnp.int32, (Q, Q), 1)
        tri = c_i <= r_i
        masks = _head_masks(Q)
        masks1 = _head_masks(1)
        for g in range(SSM_GROUPS):
            xs, bs, cs = _ssd_cols(g)
            S = state_ref[g]
            st_ref[:, xs] = S
            x = xbc_ref[:, xs]
            xb = x.astype(BF16)
            Bb = xbc_ref[:, bs].astype(BF16)
            Cb = xbc_ref[:, cs].astype(BF16)
            CB = _dot(Cb, Bb, _NT)
            y = jnp.zeros((Q, _GP), F32)
            El = jnp.zeros((Q, _GP), F32)
            Wl = jnp.zeros((Q, _GP), F32)
            decl = jnp.zeros((1, _GP), F32)
            for r in range(_HPG):
                h = g * _HPG + r
                a_c = ac_ref[:, h:h + 1]
                a_r = ar_ref[h:h + 1, :]
                dt_c = dtc_ref[:, h:h + 1]
                dt_r = dtr_ref[h:h + 1, :]
                L = jnp.exp(jnp.where(tri, a_c - a_r, NEG))
                W = CB * L * dt_r
                y = jnp.where(masks[r], _dot(W.astype(BF16), xb), y)
                a_q = a_c[Q - 1:Q, :]
                El = jnp.where(masks[r], jnp.exp(a_c), El)
                Wl = jnp.where(masks[r], jnp.exp(a_q - a_c) * dt_c, Wl)
                decl = jnp.where(masks1[r], jnp.exp(a_q), decl)
            ys_ref[:, xs] = y + _dot(Cb, S.astype(BF16)) * El
            state_ref[g] = S * decl + _dot(Bb, (x * Wl).astype(BF16), _TN)

    return pl.pallas_call(
        kern, name=name, grid=(nc,),
        in_specs=[sp["xbc"], sp["col"], sp["col"], sp["row"], sp["row"]],
        out_specs=[sp["x"], sp["st"]],
        out_shape=[jax.ShapeDtypeStruct((T, SSM_D_INNER), F32), jax.ShapeDtypeStruct((nc * N, SSM_D_INNER), F32)],
        scratch_shapes=[pltpu.VMEM((SSM_GROUPS, N, _GP), F32)],
        compiler_params=_params(("arbitrary",)),
    )(xbc, dtc, acc_, dtr, acr)


def _ssd_scan_bwd(xbc, dys, dskip, st, dtc, acc_, dtr, acr, *, name):
    T = xbc.shape[0]
    Q, N = SSM_CHUNK, SSM_STATE
    nc = T // Q
    sp = _ssd_specs(lambda j: nc - 1 - j)

    def kern(xbc_ref, dy_ref, dsk_ref, st_ref, dtc_ref, ac_ref, dtr_ref, ar_ref,
             dxbc_ref, dac_ref, dar_ref, ddc_ref, ddr_ref, dstate_ref):
        @pl.when(pl.program_id(0) == 0)
        def _():
            dstate_ref[...] = jnp.zeros_like(dstate_ref)

        r_i = lax.broadcasted_iota(jnp.int32, (Q, Q), 0)
        c_i = lax.broadcasted_iota(jnp.int32, (Q, Q), 1)
        tri = c_i <= r_i
        last_row = lax.broadcasted_iota(jnp.int32, (Q, 1), 0) == Q - 1
        lane128 = lax.broadcasted_iota(jnp.int32, (Q, LANE), 1)
        masks = _head_masks(Q)
        masksN = _head_masks(N)
        masks1 = _head_masks(1)
        zeros = jnp.zeros((Q, _GP), F32)
        dacol = jnp.zeros((Q, LANE), F32)
        ddcol = jnp.zeros((Q, LANE), F32)
        for g in range(SSM_GROUPS):
            xs, bs, cs = _ssd_cols(g)
            dS = dstate_ref[g]
            dSb = dS.astype(BF16)
            S = st_ref[:, xs]
            Sb = S.astype(BF16)
            x = xbc_ref[:, xs]
            xb = x.astype(BF16)
            Bb = xbc_ref[:, bs].astype(BF16)
            Cb = xbc_ref[:, cs].astype(BF16)
            dy = dy_ref[:, xs]
            CB = _dot(Cb, Bb, _NT)
            BdS = _dot(Bb, dSb)
            hx = BdS * x
            yd = _dot(Cb, Sb) * dy
            dSS = dS * S
            dxi, El, Wl = zeros, zeros, zeros
            decl = jnp.zeros((1, _GP), F32)
            dBacc = jnp.zeros((Q, N), F32)
            dCacc = jnp.zeros((Q, N), F32)
            for r in range(_HPG):
                h = g * _HPG + r
                hm = masks[r]
                a_c = ac_ref[:, h:h + 1]
                a_r = ar_ref[h:h + 1, :]
                dt_c = dtc_ref[:, h:h + 1]
                dt_r = dtr_ref[h:h + 1, :]
                L = jnp.exp(jnp.where(tri, a_c - a_r, NEG))
                GL = CB * L
                W = GL * dt_r
                dym = jnp.where(hm, dy, 0.0).astype(BF16)
                dW = _dot(dym, xb, _NT)
                E = dW * W
                da_c = jnp.sum(E, axis=1, keepdims=True)
                dar_ref[h:h + 1, :] = -jnp.sum(E, axis=0, keepdims=True)
                ddr_ref[h:h + 1, :] = jnp.sum(dW * GL, axis=0, keepdims=True)
                dGb = (dW * L * dt_r).astype(BF16)
                dCacc = dCacc + _dot(dGb, Bb)
                dBacc = dBacc + _dot(dGb, Cb, _TN)
                dxi = dxi + _dot(W.astype(BF16), dym, _TN)
                a_q = a_c[Q - 1:Q, :]
                e_c = jnp.exp(a_c)
                eq_c = jnp.exp(a_q - a_c)
                w_c = eq_c * dt_c
                ydr = jnp.sum(jnp.where(hm, yd, 0.0), axis=1, keepdims=True) * e_c
                h_c = jnp.sum(jnp.where(hm, hx, 0.0), axis=1, keepdims=True)
                hw = h_c * w_c
                dss = jnp.sum(jnp.sum(jnp.where(masksN[r], dSS, 0.0), axis=1, keepdims=True), axis=0, keepdims=True)
                s_q = jnp.sum(hw, axis=0, keepdims=True) + jnp.exp(a_q) * dss
                da_c = da_c + ydr - hw + jnp.where(last_row, s_q, 0.0)
                dacol = jnp.where(lane128 == h, da_c, dacol)
                ddcol = jnp.where(lane128 == h, h_c * eq_c, ddcol)
                El = jnp.where(hm, e_c, El)
                Wl = jnp.where(hm, w_c, Wl)
                decl = jnp.where(masks1[r], jnp.exp(a_q), decl)
            dxbc_ref[:, xs] = dxi + BdS * Wl + dsk_ref[:, xs]
            dxbc_ref[:, bs] = dBacc + _dot((x * Wl).astype(BF16), dSb, _NT)
            dyE = (dy * El).astype(BF16)
            dxbc_ref[:, cs] = dCacc + _dot(dyE, Sb, _NT)
            dstate_ref[g] = dS * decl + _dot(Cb, dyE, _TN)
        dac_ref[...] = dacol
        ddc_ref[...] = ddcol

    return pl.pallas_call(
        kern, name=name, grid=(nc,),
        in_specs=[sp["xbc"], sp["x"], sp["x"], sp["st"], sp["col"], sp["col"], sp["row"], sp["row"]],
        out_specs=[sp["xbc"], sp["col"], sp["row"], sp["col"], sp["row"]],
        out_shape=[jax.ShapeDtypeStruct((T, SSM_XBC), F32),
                   jax.ShapeDtypeStruct((T, LANE), F32), jax.ShapeDtypeStruct((SSM_HEADS, T), F32),
                   jax.ShapeDtypeStruct((T, LANE), F32), jax.ShapeDtypeStruct((SSM_HEADS, T), F32)],
        scratch_shapes=[pltpu.VMEM((SSM_GROUPS, N, _GP), F32)],
        compiler_params=_params(("arbitrary",)),
    )(xbc, dys, dskip, st, dtc, acc_, dtr, acr)


def _gate_norm_fwd(ys, xbc, proj, d_exp, norm_w, *, name):
    T = ys.shape[0]
    bt = _pick(T, NARROW_ROW_BLOCK, SUBLANE)

    def kern(ys_ref, x_ref, z_ref, d_ref, w_ref, o_ref):
        z = z_ref[...]
        yz = (ys_ref[...] + d_ref[...] * x_ref[...]) * (z * _sigmoid(z))
        rstd = lax.rsqrt(jnp.mean(yz * yz, axis=-1, keepdims=True) + RMS_EPS)
        o_ref[...] = (yz * rstd * w_ref[...]).astype(BF16)

    blk = pl.BlockSpec((bt, _GP), lambda i, g: (i, g))
    vec = pl.BlockSpec((1, _GP), lambda i, g: (0, g))
    return pl.pallas_call(
        kern, name=name, grid=(T // bt, SSM_GROUPS), in_specs=[blk, blk, blk, vec, vec], out_specs=blk,
        out_shape=jax.ShapeDtypeStruct((T, SSM_D_INNER), BF16), compiler_params=_params(("parallel", "parallel")),
    )(ys, xbc, proj, d_exp, norm_w)


def _gate_norm_bwd(dyn, ys, xbc, proj, d_exp, norm_w, *, name):
    T = ys.shape[0]
    bt = _pick(T, NARROW_ROW_BLOCK, SUBLANE)

    def kern(dyn_ref, ys_ref, x_ref, z_ref, d_ref, w_ref, dz_ref, dys_ref, dsk_ref, dw_ref, dd_ref):
        i = pl.program_id(1)
        z = z_ref[...]
        x = x_ref[...]
        sg = _sigmoid(z)
        sz = z * sg
        y = ys_ref[...] + d_ref[...] * x
        yz = y * sz
        rstd = lax.rsqrt(jnp.mean(yz * yz, axis=-1, keepdims=True) + RMS_EPS)
        yhat = yz * rstd
        dynv = dyn_ref[...]
        gg = dynv * w_ref[...]
        dyz = rstd * (gg - yhat * jnp.mean(gg * yhat, axis=-1, keepdims=True))
        dy = dyz * sz
        dz_ref[...] = (dyz * y * sg * (1.0 + z * (1.0 - sg))).astype(BF16)
        dys_ref[...] = dy
        dsk_ref[...] = dy * d_ref[...]

        @pl.when(i == 0)
        def _():
            dw_ref[...] = jnp.zeros_like(dw_ref)
            dd_ref[...] = jnp.zeros_like(dd_ref)

        dw_ref[...] += jnp.sum(dynv * yhat, axis=0, keepdims=True)
        dd_ref[...] += jnp.sum(dy * x, axis=0, keepdims=True)

    blk = pl.BlockSpec((bt, _GP), lambda g, i: (i, g))
    vec = pl.BlockSpec((1, _GP), lambda g, i: (0, g))
    act = jax.ShapeDtypeStruct((T, SSM_D_INNER), F32)
    par = jax.ShapeDtypeStruct((1, SSM_D_INNER), F32)
    return pl.pallas_call(
        kern, name=name, grid=(SSM_GROUPS, T // bt), in_specs=[blk, blk, blk, blk, vec, vec],
        out_specs=[blk, blk, blk, vec, vec],
        out_shape=[jax.ShapeDtypeStruct((T, SSM_IN_PAD), BF16), act, act, par, par],
        compiler_params=_params(("parallel", "arbitrary")),
    )(dyn, ys, xbc, proj, d_exp, norm_w)


def _loss_head(y, target, *, name):
    T, D = y.shape
    bt = _pick(T, ROW_BLOCK, SUBLANE)

    def kern(y_ref, t_ref, l_ref, dy_ref):
        i = pl.program_id(0)
        err = y_ref[...] - t_ref[...]
        dy_ref[...] = err * (1.0 / D)

        @pl.when(i == 0)
        def _():
            l_ref[...] = jnp.zeros_like(l_ref)

        l_ref[...] += jnp.sum(err * err, axis=0, keepdims=True) * (0.5 / D)

    row = pl.BlockSpec((bt, D), lambda i: (i, 0))
    vec = pl.BlockSpec((1, D), lambda i: (0, 0))
    return pl.pallas_call(
        kern, name=name, grid=(T // bt,), in_specs=[row, row], out_specs=[vec, row],
        out_shape=[jax.ShapeDtypeStruct((1, D), F32), jax.ShapeDtypeStruct((T, D), F32)],
        compiler_params=_params(("arbitrary",)),
    )(y, target)


def _adamw(w, g, m, v, *, name):
    shape = w.shape
    w, g, m, v = (t.reshape(-1, shape[-1]) for t in (w, g, m, v))
    R, C = w.shape
    br = _pick(R, 256, SUBLANE)

    def kern(w_ref, g_ref, m_ref, v_ref, d_ref, nm_ref, nv_ref):
        gv = g_ref[...]
        nm = ADAM_B1 * m_ref[...] + (1.0 - ADAM_B1) * gv
        nv = ADAM_B2 * v_ref[...] + (1.0 - ADAM_B2) * (gv * gv)
        m_hat = nm / (1.0 - ADAM_B1 ** ADAM_STEP)
        v_hat = nv / (1.0 - ADAM_B2 ** ADAM_STEP)
        d_ref[...] = -ADAM_LR * (m_hat / (jnp.sqrt(v_hat) + ADAM_EPS) + ADAM_WD * w_ref[...])
        nm_ref[...] = nm
        nv_ref[...] = nv

    blk = pl.BlockSpec((br, C), lambda i: (i, 0))
    outs = pl.pallas_call(
        kern, name=name, grid=(R // br,), in_specs=[blk] * 4, out_specs=[blk] * 3,
        out_shape=[jax.ShapeDtypeStruct((R, C), F32)] * 3, compiler_params=_params(("parallel",)),
    )(w, g, m, v)
    return [o.reshape(shape) for o in outs]


def _add2(a, b, out_dtype, *, name):
    shape = a.shape
    a2, b2 = a.reshape(-1, shape[-1]), b.reshape(-1, shape[-1])
    R, C = a2.shape
    br = _pick(R, 512, SUBLANE)

    def kern(a_ref, b_ref, o_ref):
        o_ref[...] = (a_ref[...] + b_ref[...]).astype(out_dtype)

    blk = pl.BlockSpec((br, C), lambda i: (i, 0))
    return pl.pallas_call(
        kern, name=name, grid=(R // br,), in_specs=[blk, blk], out_specs=blk,
        out_shape=jax.ShapeDtypeStruct((R, C), out_dtype), compiler_params=_params(("parallel",)),
    )(a2, b2).reshape(shape)


def _sum4(buf, *, name):
    _, R, C = buf.shape
    br = _pick(R, 512, SUBLANE)

    def kern(b_ref, o_ref):
        b = [b_ref[k].astype(F32) for k in range(4)]
        o_ref[...] = ((b[0] + b[1]) + b[2]) + b[3]

    return pl.pallas_call(
        kern, name=name, grid=(R // br,), in_specs=[pl.BlockSpec((4, br, C), lambda i: (0, i, 0))],
        out_specs=pl.BlockSpec((br, C), lambda i: (i, 0)),
        out_shape=jax.ShapeDtypeStruct((R, C), F32), compiler_params=_params(("parallel",)),
    )(buf)


def _place():
    x, y, c = lax.axis_index("x"), lax.axis_index("y"), lax.axis_index("c")
    other_chips = [(1 - x, y), (x, 1 - y), (1 - x, 1 - y)]
    return x, y, c, other_chips


def _gather_chips(w, *, name):
    R, C = w.shape
    H = R // 2

    def body(w_ref, out_ref, send_sems, recv_sems):
        x, y, c, chips = _place()
        me_chip = 2 * x + y
        sib = (x, y, 1 - c)

        def rows(chip, hc):
            return out_ref.at[chip, pl.ds(hc * H, H), :]

        def copy(k, blk, to, src=None):
            return pltpu.make_async_remote_copy(
                src_ref=blk if src is None else src, dst_ref=blk, send_sem=send_sems.at[k], recv_sem=recv_sems.at[k],
                device_id=to, device_id_type=MESH)

        first = [copy(j, rows(me_chip, c), (cx, cy, c), src=w_ref.at[pl.ds(c * H, H), :])
                 for j, (cx, cy) in enumerate(chips)]
        for cp in first:
            cp.start()
        passed = []
        for j, (cx, cy) in enumerate(chips):
            blk = rows(2 * cx + cy, c)
            copy(j, blk, (cx, cy, c)).wait_recv()
            fw = copy(3 + j, blk, sib)
            fw.start()
            passed.append(fw)
        for j, (cx, cy) in enumerate(chips):
            copy(3 + j, rows(2 * cx + cy, 1 - c), sib).wait_recv()
        for cp in first + passed:
            cp.wait_send()

    return pl.pallas_call(
        body, name=name, in_specs=[_ANY], out_specs=_ANY,
        out_shape=jax.ShapeDtypeStruct((4, R, C), w.dtype),
        scratch_shapes=[pltpu.SemaphoreType.DMA((6,)), pltpu.SemaphoreType.DMA((6,))],
    )(w)


def _pair_swap(v, *, name, other_half=False):
    shape = (v.shape[0], v.shape[1] // 2, v.shape[2]) if other_half else v.shape

    def body(v_ref, out_ref, send_sem, recv_sem):
        x, y, c, _ = _place()
        src = v_ref.at[:, pl.ds((1 - c) * shape[1], shape[1]), :] if other_half else v_ref
        cp = pltpu.make_async_remote_copy(src_ref=src, dst_ref=out_ref, send_sem=send_sem, recv_sem=recv_sem,
                                          device_id=(x, y, 1 - c), device_id_type=MESH)
        cp.start()
        cp.wait()

    return pl.pallas_call(
        body, name=name, in_specs=[_ANY], out_specs=_ANY, out_shape=jax.ShapeDtypeStruct(shape, v.dtype),
        scratch_shapes=[pltpu.SemaphoreType.DMA, pltpu.SemaphoreType.DMA],
    )(v)


def _chip_exchange(pv, *, name):
    def body(p_ref, out_ref, send_sems, recv_sems):
        x, y, c, chips = _place()
        me_chip = 2 * x + y
        sends = []
        for j, (cx, cy) in enumerate(chips):
            cp = pltpu.make_async_remote_copy(
                src_ref=p_ref.at[2 * cx + cy], dst_ref=out_ref.at[me_chip], send_sem=send_sems.at[j],
                recv_sem=recv_sems.at[j], device_id=(cx, cy, c), device_id_type=MESH)
            cp.start()
            sends.append(cp)
        for j, (cx, cy) in enumerate(chips):
            blk = out_ref.at[2 * cx + cy]
            pltpu.make_async_remote_copy(src_ref=blk, dst_ref=blk, send_sem=send_sems.at[j], recv_sem=recv_sems.at[j],
                                         device_id=(cx, cy, c), device_id_type=MESH).wait_recv()
        for cp in sends:
            cp.wait_send()

    return pl.pallas_call(
        body, name=name, in_specs=[_ANY], out_specs=_ANY, out_shape=jax.ShapeDtypeStruct(pv.shape, pv.dtype),
        scratch_shapes=[pltpu.SemaphoreType.DMA((3,)), pltpu.SemaphoreType.DMA((3,))],
    )(pv)


WEIGHTS = [
    ("attn_w_in", 2), ("attn_b_f", None), ("attn_w_out", 1), ("ssm_w_in", 2), ("ssm_conv_w", 2), ("ssm_conv_b", 1),
    ("ssm_dt_bias", None), ("ssm_A_log", None), ("ssm_D", None), ("ssm_norm_w", 1), ("ssm_w_out", 1),
    ("ln_mix_g", None), ("ln_mix_b", None), ("ffn_w_up", 2), ("ffn_conv_w", 2), ("ffn_conv_b", None),
    ("ffn_w_down", 1), ("ln_ffn_g", None), ("ln_ffn_b", None), ("ple_w_proj", 2), ("ple_w_gate", 1),
    ("ple_b_gate", None),
]
N_CHIPS = 4
MATMUL_WEIGHTS = ("attn_w_in", "attn_w_out", "ssm_w_in", "ssm_w_out", "ffn_w_up", "ffn_w_down", "ple_w_proj",
                  "ple_w_gate")


def _pack(arrays):
    parts = []
    total = 0
    for a in arrays:
        n = a.size
        pad = (-n) % PACK_COLS
        flat = a.reshape(-1)
        parts.append(jnp.pad(flat, (0, pad)) if pad else flat)
        total += n + pad
    rows = total // PACK_COLS
    rpad = (-rows) % PACK_ROW_ALIGN
    if rpad:
        parts.append(jnp.zeros((rpad * PACK_COLS,), arrays[0].dtype))
    return jnp.concatenate(parts).reshape(rows + rpad, PACK_COLS)


def _unpack(buf, shapes):
    flat = buf.reshape(-1)
    out = []
    off = 0
    for s in shapes:
        n = math.prod(s)
        out.append(flat[off:off + n].reshape(s))
        off += n + ((-n) % PACK_COLS)
    return out


def _from_row_layout(a):
    return jnp.pad(a.T, ((0, 0), (0, LANE - SSM_HEADS)))


def _pad_lanes(v, n=LANE):
    return jnp.pad(v, (0, n - v.shape[0])).reshape(1, n)


def _local_step(x, p, target, W):
    T = x.shape[0]
    row = lambda v: v.reshape(1, -1)
    attn_in = jnp.pad(W["attn_w_in"][0], ((0, 0), (0, ATTN_IN_PAD - W["attn_w_in"].shape[2])))
    ssm_in = jnp.pad(W["ssm_w_in"][0], ((0, 0), (0, SSM_IN_PAD - W["ssm_w_in"].shape[2])))
    bf = _pad_lanes(W["attn_b_f"][0])
    dt_bias = _pad_lanes(W["ssm_dt_bias"][0])
    a_log = _pad_lanes(W["ssm_A_log"][0])
    d_exp = jnp.repeat(W["ssm_D"][0], _PH).reshape(1, SSM_D_INNER)
    norm_w = row(W["ssm_norm_w"][0])
    G = {}

    def ffn_ple_fwd(i, xin, mix, tag):
        s = {}
        s["z1"], s["h1"], s["h1b"] = _ln_fwd(xin, mix, row(W["ln_mix_g"][i]), row(W["ln_mix_b"][i]),
                                             name=f"ln_mix_fwd{tag}")
        s["up"] = _mm(s["h1b"], W["ffn_w_up"][i], name=f"ffn_up{tag}")
        s["a"] = _ffn_act_fwd(s["up"], W["ffn_conv_w"][i], row(W["ffn_conv_b"][i]), name=f"ffn_act_fwd{tag}")
        ffn = _mm(s["a"], W["ffn_w_down"][i], name=f"ffn_down{tag}")
        s["z2"], s["h2"], s["h2b"] = _ln_fwd(s["h1"], ffn, row(W["ln_ffn_g"][i]), row(W["ln_ffn_b"][i]),
                                             name=f"ln_ffn_fwd{tag}")
        s["G"] = _mm(s["h2b"], W["ple_w_gate"][i], name=f"ple_gate_mm{tag}")
        s["pp"] = _mm(pb[i], W["ple_w_proj"][i], name=f"ple_proj_mm{tag}")
        out, outb = _ple_fwd(s["h2"], s["G"], row(W["ple_b_gate"][i]), s["pp"], name=f"ple_fwd{tag}")
        return out, outb, s

    def ffn_ple_bwd(i, dx, s, tag):
        g = {}
        dG, dpp, g["ple_b_gate"] = _ple_bwd(dx, s["G"], row(W["ple_b_gate"][i]), s["pp"], name=f"ple_bwd{tag}")
        g["ple_w_gate"] = _mm(s["h2b"], dG, ta=True, name=f"ple_gate_dw{tag}")
        g["ple_w_proj"] = _mm(pb[i], dpp, ta=True, name=f"ple_proj_dw{tag}")
        dh2 = _mm(dG, W["ple_w_gate"][i], tb=True, add=dx, name=f"ple_gate_dx{tag}")
        dz2, dz2b, g["ln_ffn_g"], g["ln_ffn_b"] = _ln_bwd(dh2, s["z2"], row(W["ln_ffn_g"][i]), name=f"ln_ffn_bwd{tag}")
        da = _mm(dz2b, W["ffn_w_down"][i], tb=True, out_dtype=BF16, name=f"ffn_down_dx{tag}")
        g["ffn_w_down"] = _mm(s["a"], dz2b, ta=True, name=f"ffn_down_dw{tag}")
        dup, dgc, g["ffn_conv_b"], g["ffn_conv_w"] = _ffn_act_bwd(
            da, s["up"], W["ffn_conv_w"][i], row(W["ffn_conv_b"][i]), name=f"ffn_act_bwd{tag}")
        dup = _dwconv_bwd_data(dgc, W["ffn_conv_w"][i], FFN_CONV, dup, FFN_DIM, name=f"ffn_conv_bwd{tag}")
        g["ffn_w_up"] = _mm(s["h1b"], dup, ta=True, name=f"ffn_up_dw{tag}")
        dh1 = _mm(dup, W["ffn_w_up"][i], tb=True, add=dz2, add_scale=DEEPNORM_ALPHA, name=f"ffn_up_dx{tag}")
        dz1, dz1b, g["ln_mix_g"], g["ln_mix_b"] = _ln_bwd(dh1, s["z1"], row(W["ln_mix_g"][i]), name=f"ln_mix_bwd{tag}")
        return dz1, dz1b, g

    xb = x.astype(BF16)
    pb = p.astype(BF16)
    proj0 = _mm(xb, attn_in, name="attn_in")
    c_col = _fox_gate_fwd(proj0, bf, name="fox_gate_fwd")
    cT = (c_col[:, :ATTN_HEADS] * LOG2E).T.reshape(HEAD_PAIRS, 2, T)
    qa, qb, kk, ka, kb, vv, va, vb = _attn_prep(proj0, name="attn_prep")
    o, ob, lsea, lseb = _attn_fwd(qa, qb, kk, va, vb, cT, name="attn_fwd")
    mix0 = _mm(ob, W["attn_w_out"][0], name="attn_out")
    x1, x1b, s0 = ffn_ple_fwd(0, x, mix0, "0")

    proj1 = _mm(x1b, ssm_in, name="ssm_in")
    dt, acum = _ssd_dt_fwd(proj1, dt_bias, a_log, name="ssd_dt_fwd")
    xbc = _conv_silu_fwd(proj1, W["ssm_conv_w"][0], row(W["ssm_conv_b"][0]), name="ssd_conv_fwd")
    dtr, acr = dt[:, :SSM_HEADS].T, acum[:, :SSM_HEADS].T
    ys, states = _ssd_scan_fwd(xbc, dt, acum, dtr, acr, name="ssd_scan_fwd")
    yn = _gate_norm_fwd(ys, xbc, proj1, d_exp, norm_w, name="ssd_gate_norm_fwd")
    mix1 = _mm(yn, W["ssm_w_out"][0], name="ssm_out")
    x2, _, s1 = ffn_ple_fwd(1, x1, mix1, "1")

    lpart, dy = _loss_head(x2, target, name="loss_head")
    loss = jnp.sum(lpart)

    dz1, dz1b, g1 = ffn_ple_bwd(1, dy, s1, "1")
    G["ssm_w_out"] = _mm(yn, dz1b, ta=True, name="ssm_out_dw")[None]
    dyn = _mm(dz1b, W["ssm_w_out"][0], tb=True, name="ssm_out_dx")
    dproj1, dys, dskip, dnw, dde = _gate_norm_bwd(dyn, ys, xbc, proj1, d_exp, norm_w, name="ssd_gate_norm_bwd")
    G["ssm_norm_w"] = dnw
    G["ssm_D"] = dde.reshape(SSM_HEADS, _PH).sum(axis=1)[None]
    dxbc, dac, dar, ddc, ddr = _ssd_scan_bwd(xbc, dys, dskip, states, dt, acum, dtr, acr, name="ssd_scan_bwd")
    dproj1, dal, ddb = _ssd_dt_bwd(dac, _from_row_layout(dar), ddc, _from_row_layout(ddr), dt, proj1, dt_bias, a_log,
                                   dproj1, name="ssd_dt_bwd")
    G["ssm_A_log"] = dal[:, :SSM_HEADS]
    G["ssm_dt_bias"] = ddb[:, :SSM_HEADS]
    dpre, G["ssm_conv_b"], dcw = _conv_silu_bwd(dxbc, proj1, W["ssm_conv_w"][0], row(W["ssm_conv_b"][0]),
                                                name="ssd_conv_bwd")
    G["ssm_conv_w"] = dcw[None]
    dproj1 = _dwconv_bwd_data(dpre, W["ssm_conv_w"][0], SSM_CONV, dproj1, SSM_D_INNER, name="ssd_conv_bwd_data")
    G["ssm_w_in"] = _mm(x1b, dproj1, ta=True, name="ssm_in_dw")[None, :, :W["ssm_w_in"].shape[2]]
    dx1 = _mm(dproj1, ssm_in, tb=True, add=dz1, add_scale=DEEPNORM_ALPHA, name="ssm_in_dx")

    dz0, dz0b, g0 = ffn_ple_bwd(0, dx1, s0, "0")
    G["attn_w_out"] = _mm(ob, dz0b, ta=True, name="attn_out_dw")[None]
    do = _mm(dz0b, W["attn_w_out"][0], tb=True, name="attn_out_dx")
    doa, dob, dlta, dltb = _attn_bwd_prep(do, o, name="attn_bwd_prep")
    dq, dk, dv, dcT, dcq = _attn_bwd(qa, qb, kk, ka, kb, vv, doa, dob, lsea, lseb, dlta, dltb, cT, name="attn_bwd")
    dc_col = jnp.pad(dcT.reshape(ATTN_HEADS, T).T + dcq[:, :ATTN_HEADS], ((0, 0), (0, LANE - ATTN_HEADS)))
    dfl, dbf = _fox_gate_bwd(dc_col, proj0, bf, name="fox_gate_bwd")
    G["attn_b_f"] = dbf[:, :ATTN_HEADS]
    dproj0 = jnp.concatenate([dq.astype(BF16), dk.astype(BF16), dv.astype(BF16), dfl.astype(BF16)], axis=1)
    G["attn_w_in"] = _mm(xb, dproj0, ta=True, name="attn_in_dw")[None, :, :W["attn_w_in"].shape[2]]
    grad_x = _mm(dproj0, attn_in, tb=True, add=dz0, add_scale=DEEPNORM_ALPHA, name="attn_in_dx")

    for k in g0:
        G[k] = jnp.stack([g0[k].reshape(W[k].shape[1:]), g1[k].reshape(W[k].shape[1:])])
    return loss, grad_x, G


def kernel(x, p, attn_w_in, attn_b_f, attn_w_out, ssm_w_in, ssm_conv_w, ssm_conv_b, ssm_dt_bias, ssm_A_log, ssm_D, ssm_norm_w, ssm_w_out, ln_mix_g, ln_mix_b, ffn_w_up, ffn_conv_w, ffn_conv_b, ffn_w_down, ln_ffn_g, ln_ffn_b, ple_w_proj, ple_w_gate, ple_b_gate, loss_target, m_attn_w_in, m_attn_b_f, m_attn_w_out, m_ssm_w_in, m_ssm_conv_w, m_ssm_conv_b, m_ssm_dt_bias, m_ssm_A_log, m_ssm_D, m_ssm_norm_w, m_ssm_w_out, m_ln_mix_g, m_ln_mix_b, m_ffn_w_up, m_ffn_conv_w, m_ffn_conv_b, m_ffn_w_down, m_ln_ffn_g, m_ln_ffn_b, m_ple_w_proj, m_ple_w_gate, m_ple_b_gate, v_attn_w_in, v_attn_b_f, v_attn_w_out, v_ssm_w_in, v_ssm_conv_w, v_ssm_conv_b, v_ssm_dt_bias, v_ssm_A_log, v_ssm_D, v_ssm_norm_w, v_ssm_w_out, v_ln_mix_g, v_ln_mix_b, v_ffn_w_up, v_ffn_conv_w, v_ffn_conv_b, v_ffn_w_down, v_ln_ffn_g, v_ln_ffn_b, v_ple_w_proj, v_ple_w_gate, v_ple_b_gate):
    names = [n for n, _ in WEIGHTS]
    axes = dict(WEIGHTS)
    w_loc = dict(zip(names, [attn_w_in, attn_b_f, attn_w_out, ssm_w_in, ssm_conv_w, ssm_conv_b, ssm_dt_bias, ssm_A_log, ssm_D, ssm_norm_w, ssm_w_out, ln_mix_g, ln_mix_b, ffn_w_up, ffn_conv_w, ffn_conv_b, ffn_w_down, ln_ffn_g, ln_ffn_b, ple_w_proj, ple_w_gate, ple_b_gate]))
    m_loc = dict(zip(names, [m_attn_w_in, m_attn_b_f, m_attn_w_out, m_ssm_w_in, m_ssm_conv_w, m_ssm_conv_b, m_ssm_dt_bias, m_ssm_A_log, m_ssm_D, m_ssm_norm_w, m_ssm_w_out, m_ln_mix_g, m_ln_mix_b, m_ffn_w_up, m_ffn_conv_w, m_ffn_conv_b, m_ffn_w_down, m_ln_ffn_g, m_ln_ffn_b, m_ple_w_proj, m_ple_w_gate, m_ple_b_gate]))
    v_loc = dict(zip(names, [v_attn_w_in, v_attn_b_f, v_attn_w_out, v_ssm_w_in, v_ssm_conv_w, v_ssm_conv_b, v_ssm_dt_bias, v_ssm_A_log, v_ssm_D, v_ssm_norm_w, v_ssm_w_out, v_ln_mix_g, v_ln_mix_b, v_ffn_w_up, v_ffn_conv_w, v_ffn_conv_b, v_ffn_w_down, v_ln_ffn_g, v_ln_ffn_b, v_ple_w_proj, v_ple_w_gate, v_ple_b_gate]))
    sharded = [n for n in names if axes[n] is not None]
    matrices = [n for n in sharded if n in MATMUL_WEIGHTS]

    def wire(n):
        if n in matrices:
            return w_loc[n].astype(BF16)
        return lax.bitcast_convert_type(w_loc[n], BF16)

    wired = [wire(n) for n in sharded]
    me_chip = 2 * lax.axis_index("x") + lax.axis_index("y")
    packed = _pack(wired)
    gathered = lax.dynamic_update_index_in_dim(_gather_chips(packed, name="gather_weights"), packed, me_chip, 0)
    W = dict(w_loc)
    per_chip = [_unpack(gathered[k], [w.shape for w in wired]) for k in range(N_CHIPS)]
    for i, n in enumerate(sharded):
        pieces = [per_chip[k][i] for k in range(N_CHIPS)]
        if n not in matrices:
            pieces = [lax.bitcast_convert_type(q, F32) for q in pieces]
        W[n] = jnp.concatenate(pieces, axis=axes[n])

    loss, grad_x, G = _local_step(x[0], p[:, 0], loss_target[0], W)
    loss = lax.psum(loss, ("x", "y", "c"))

    def slot(k):
        parts = []
        for n in names:
            g = G[n].reshape(W[n].shape)
            if axes[n] is not None:
                size = w_loc[n].shape[axes[n]]
                g = lax.slice_in_dim(g, k * size, (k + 1) * size, axis=axes[n])
            parts.append(g)
        return _pack(parts)

    contrib = jnp.stack([slot(k) for k in range(N_CHIPS)])
    R = contrib.shape[1]
    H = R // 2
    c = lax.axis_index("c")
    keep = lax.dynamic_slice_in_dim(contrib, c * H, H, axis=1)
    pair = _add2(keep, _pair_swap(contrib, other_half=True, name="grad_pair_swap"), BF16, name="grad_pair_sum")
    from_chips = lax.dynamic_update_index_in_dim(
        _chip_exchange(pair, name="grad_chip_exchange"), lax.dynamic_index_in_dim(pair, me_chip, 0, keepdims=False),
        me_chip, 0)
    half = _sum4(from_chips, name="grad_chip_sum")
    other = _pair_swap(half, name="grad_half_swap")
    gflat = jnp.concatenate([jnp.where(c == 0, half, other), jnp.where(c == 0, other, half)])

    grads = _unpack(gflat, [w_loc[n].shape for n in names])
    steps = [_adamw(w_loc[n], g, m_loc[n], v_loc[n], name=f"adamw_{n}") for n, g in zip(names, grads)]
    return (loss, grad_x[None], *grads, *[s[0] for s in steps], *[s[1] for s in steps], *[s[2] for s in steps])
```

```python
import functools
import math

import jax
import jax.numpy as jnp
from jax import lax
from jax.experimental import pallas as pl
from jax.experimental.pallas import tpu as pltpu

F32 = jnp.float32
BF16 = jnp.bfloat16
MESH = pl.DeviceIdType.MESH

D_MODEL = 1024
ATTN_HEADS = 16
HEAD_PAIRS = ATTN_HEADS // 2
SSM_D_INNER = 2048
SSM_HEADS = 32
SSM_GROUPS = 8
SSM_STATE = 128
SSM_CONV = 4
SSM_CHUNK = 128
SSM_XBC = SSM_D_INNER + 2 * SSM_GROUPS * SSM_STATE
FFN_DIM = 2816
FFN_CONV = 3
DEPTH = 2
LN_EPS = 1e-5
RMS_EPS = 1e-5
DEEPNORM_ALPHA = (2 * DEPTH) ** 0.25
ADAM_LR = 0.001
ADAM_B1 = 0.9
ADAM_B2 = 0.999
ADAM_EPS = 1e-08
ADAM_WD = 0.01
ADAM_STEP = 10

LANE = 128
SUBLANE = 8
HALO = SUBLANE
NEG = -1e30
ATTN_IN_PAD = 3 * D_MODEL + LANE
SSM_IN_PAD = 2 * SSM_D_INNER + 2 * SSM_GROUPS * SSM_STATE + LANE
PACK_COLS = 1024
PACK_ROW_ALIGN = 512

ATTN_BLOCK = 1024
ROW_BLOCK = 512
NARROW_ROW_BLOCK = 1024
CUM_BLOCK = 256


def _params(sem, vmem_mb=48):
    return pltpu.CompilerParams(dimension_semantics=sem, vmem_limit_bytes=vmem_mb * 2 ** 20)


def _pick(n, target, mult=LANE):
    best = None
    d = mult
    while d <= min(n, target):
        if n % d == 0:
            best = d
        d += mult
    return n if best is None else best


def _sigmoid(x):
    return 1.0 / (1.0 + jnp.exp(-x))


def _log1p(u):
    w = 1.0 + u
    return jnp.where(w == 1.0, u, jnp.log(w) * (u / (w - 1.0)))


def _softplus(x):
    return jnp.maximum(x, 0.0) + _log1p(jnp.exp(-jnp.abs(x)))


def _split3(x):
    hi = x.astype(BF16)
    r1 = x - hi.astype(F32)
    mid = r1.astype(BF16)
    lo = (r1 - mid.astype(F32)).astype(BF16)
    return hi, mid, lo


def _tri_matmul(tri, x):
    out = None
    for part in _split3(x):
        t = jnp.dot(tri, part, preferred_element_type=F32)
        out = t if out is None else out + t
    return out


def _tri(n, lower):
    r = lax.broadcasted_iota(jnp.int32, (n, n), 0)
    c = lax.broadcasted_iota(jnp.int32, (n, n), 1)
    return jnp.where((c <= r) if lower else (c >= r), 1.0, 0.0).astype(BF16)


_ANY = pl.BlockSpec(memory_space=pl.ANY)
MM_OUT_BLOCK_BYTES = 13 * 2 ** 20
MM_IN_BLOCK_BYTES = 6 * 2 ** 20
MM_VMEM_BUDGET = 44 * 2 ** 20
MM_WIDE_K = 3200


def _mm(a, b, *, name, ta=False, tb=False, add=None, add_scale=1.0, out_dtype=F32):
    if ta:
        K, M = a.shape
    else:
        M, K = a.shape
    if tb:
        N, Kb = b.shape
    else:
        Kb, N = b.shape
    assert K == Kb, (a.shape, b.shape, ta, tb)
    if ta:
        assert add is None and out_dtype == F32
        bm = _pick(M, 2816)
        bn = _pick(N, MM_OUT_BLOCK_BYTES // (4 * bm))
        bk = _pick(K, max(512, MM_IN_BLOCK_BYTES // (2 * max(bm, bn))))
    else:
        bn = _pick(N, 1536 if K <= MM_WIDE_K else 512)
        bk = K
        bm = _pick(M, 2048)
        if 4 * bm * K + 4 * K * bn + (8 if add is None else 16) * bm * bn > MM_VMEM_BUDGET:
            bm = _pick(M, 1024)
    nk = K // bk
    a_spec = pl.BlockSpec((bk, bm), lambda i, j, k: (k, i)) if ta else pl.BlockSpec((bm, bk), lambda i, j, k: (i, k))
    b_spec = pl.BlockSpec((bn, bk), lambda i, j, k: (j, k)) if tb else pl.BlockSpec((bk, bn), lambda i, j, k: (k, j))
    o_spec = pl.BlockSpec((bm, bn), lambda i, j, k: (i, j))
    dims = (((0 if ta else 1,), (1 if tb else 0,)), ((), ()))
    has_add = add is not None

    def kern(*refs):
        a_ref, b_ref = refs[0], refs[1]
        add_ref = refs[2] if has_add else None
        o_ref = refs[3] if has_add else refs[2]
        k = pl.program_id(2)
        part = lax.dot_general(a_ref[...].astype(BF16), b_ref[...].astype(BF16), dims, preferred_element_type=F32)
        if nk == 1:
            o_ref[...] = (part + add_scale * add_ref[...] if has_add else part).astype(out_dtype)
        else:
            @pl.when(k == 0)
            def _():
                o_ref[...] = part

            @pl.when(k > 0)
            def _():
                o_ref[...] += part

    ins = [a, b] + ([add] if has_add else [])
    in_specs = [a_spec, b_spec] + ([o_spec] if has_add else [])
    return pl.pallas_call(
        kern, name=name, grid=(M // bm, N // bn, nk),
        in_specs=in_specs, out_specs=o_spec,
        out_shape=jax.ShapeDtypeStruct((M, N), out_dtype),
        compiler_params=_params(("parallel", "parallel", "arbitrary"), vmem_mb=56),
    )(*ins)


def _ln_stats(z):
    mu = jnp.mean(z, axis=-1, keepdims=True)
    zc = z - mu
    var = jnp.mean(zc * zc, axis=-1, keepdims=True)
    return zc, lax.rsqrt(var + LN_EPS)


def _ln_fwd(x, r, g, b, *, name):
    T, D = x.shape
    bt = _pick(T, ROW_BLOCK, SUBLANE)

    def kern(x_ref, r_ref, g_ref, b_ref, z_ref, h_ref, hb_ref):
        z = DEEPNORM_ALPHA * x_ref[...] + r_ref[...]
        zc, rstd = _ln_stats(z)
        h = zc * rstd * g_ref[...] + b_ref[...]
        z_ref[...] = z
        h_ref[...] = h
        hb_ref[...] = h.astype(BF16)

    row = pl.BlockSpec((bt, D), lambda i: (i, 0))
    vec = pl.BlockSpec((1, D), lambda i: (0, 0))
    return pl.pallas_call(
        kern, name=name, grid=(T // bt,), in_specs=[row, row, vec, vec], out_specs=[row, row, row],
        out_shape=[jax.ShapeDtypeStruct((T, D), F32)] * 2 + [jax.ShapeDtypeStruct((T, D), BF16)],
        compiler_params=_params(("parallel",)),
    )(x, r, g, b)


def _ln_bwd(dy, z, g, *, name):
    T, D = z.shape
    bt = _pick(T, ROW_BLOCK, SUBLANE)

    def kern(dy_ref, z_ref, g_ref, dz_ref, dzb_ref, dg_ref, db_ref):
        i = pl.program_id(0)
        zc, rstd = _ln_stats(z_ref[...])
        xhat = zc * rstd
        dyv = dy_ref[...]
        dxh = dyv * g_ref[...]
        m1 = jnp.mean(dxh, axis=-1, keepdims=True)
        m2 = jnp.mean(dxh * xhat, axis=-1, keepdims=True)
        dz = rstd * (dxh - m1 - xhat * m2)
        dz_ref[...] = dz
        dzb_ref[...] = dz.astype(BF16)

        @pl.when(i == 0)
        def _():
            dg_ref[...] = jnp.zeros_like(dg_ref)
            db_ref[...] = jnp.zeros_like(db_ref)

        dg_ref[...] += jnp.sum(dyv * xhat, axis=0, keepdims=True)
        db_ref[...] += jnp.sum(dyv, axis=0, keepdims=True)

    row = pl.BlockSpec((bt, D), lambda i: (i, 0))
    vec = pl.BlockSpec((1, D), lambda i: (0, 0))
    return pl.pallas_call(
        kern, name=name, grid=(T // bt,), in_specs=[row, row, vec], out_specs=[row, row, vec, vec],
        out_shape=[jax.ShapeDtypeStruct((T, D), F32), jax.ShapeDtypeStruct((T, D), BF16),
                   jax.ShapeDtypeStruct((1, D), F32), jax.ShapeDtypeStruct((1, D), F32)],
        compiler_params=_params(("arbitrary",)),
    )(dy, z, g)


def _past_taps(ext_ref, K, bt):
    ext = ext_ref[...]
    return [(ext if k == K - 1 else pltpu.roll(ext, K - 1 - k, 0))[HALO:HALO + bt] for k in range(K)]


def _conv_past(taps, cw_ref):
    out = None
    for k, tap in enumerate(taps):
        term = cw_ref[k:k + 1, :] * tap
        out = term if out is None else out + term
    return out


def _fill_ext_past(ext_ref, halo_ref, cur, i, bt):
    ext_ref[pl.ds(0, HALO), :] = jnp.where(i > 0, halo_ref[...], 0.0)
    ext_ref[pl.ds(HALO, bt), :] = cur


def _halo_prev(bt, bc, off):
    return pl.BlockSpec((HALO, bc), lambda i, j: (jnp.maximum(i * (bt // HALO) - 1, 0), j + off))


def _normal_cdf(x):
    return 0.5 * (1.0 + lax.erf(x * (1.0 / math.sqrt(2.0))))


def _gelu(x):
    return x * _normal_cdf(x)


def _gelu_and_grad(x):
    cdf = _normal_cdf(x)
    return x * cdf, cdf + x * jnp.exp(-0.5 * x * x) * (1.0 / math.sqrt(2.0 * math.pi))


def _ffn_act_fwd(up, cw, cb, *, name):
    T, F2 = up.shape
    F = F2 // 2
    bt = _pick(T, ROW_BLOCK, SUBLANE)
    bc = _pick(F, 1408)
    nb = F // bc

    def kern(u_ref, g_ref, halo_ref, cw_ref, cb_ref, a_ref, ext_ref):
        i = pl.program_id(0)
        _fill_ext_past(ext_ref, halo_ref, g_ref[...], i, bt)
        gc = cb_ref[...] + _conv_past(_past_taps(ext_ref, FFN_CONV, bt), cw_ref)
        a_ref[...] = (_gelu(gc) * u_ref[...]).astype(BF16)

    return pl.pallas_call(
        kern, name=name, grid=(T // bt, nb),
        in_specs=[pl.BlockSpec((bt, bc), lambda i, j: (i, j)),
                  pl.BlockSpec((bt, bc), lambda i, j: (i, j + nb)),
                  _halo_prev(bt, bc, nb),
                  pl.BlockSpec((FFN_CONV, bc), lambda i, j: (0, j)),
                  pl.BlockSpec((1, bc), lambda i, j: (0, j))],
        out_specs=pl.BlockSpec((bt, bc), lambda i, j: (i, j)),
        out_shape=jax.ShapeDtypeStruct((T, F), BF16),
        scratch_shapes=[pltpu.VMEM((bt + HALO, bc), F32)],
        compiler_params=_params(("parallel", "parallel")),
    )(up, up, up, cw, cb)


def _ffn_act_bwd(da, up, cw, cb, *, name):
    T, F2 = up.shape
    F = F2 // 2
    bt = _pick(T, ROW_BLOCK, SUBLANE)
    bc = _pick(F, 1408)
    nb = F // bc
    K = FFN_CONV

    def kern(da_ref, u_ref, g_ref, halo_ref, cw_ref, cb_ref, du_ref, dgc_ref, dcb_ref, dcw_ref, ext_ref):
        i = pl.program_id(1)
        _fill_ext_past(ext_ref, halo_ref, g_ref[...], i, bt)
        taps = _past_taps(ext_ref, K, bt)
        gc = cb_ref[...] + _conv_past(taps, cw_ref)
        dav = da_ref[...]
        act, act_grad = _gelu_and_grad(gc)
        du_ref[...] = (dav * act).astype(BF16)
        dgc = dav * u_ref[...] * act_grad
        dgc_ref[...] = dgc

        @pl.when(i == 0)
        def _():
            dcb_ref[...] = jnp.zeros_like(dcb_ref)
            dcw_ref[...] = jnp.zeros_like(dcw_ref)

        dcb_ref[...] += jnp.sum(dgc, axis=0, keepdims=True)
        for k in range(K):
            dcw_ref[k:k + 1, :] += jnp.sum(dgc * taps[k], axis=0, keepdims=True)

    blk = pl.BlockSpec((bt, bc), lambda j, i: (i, j))
    return pl.pallas_call(
        kern, name=name, grid=(nb, T // bt),
        in_specs=[blk, blk,
                  pl.BlockSpec((bt, bc), lambda j, i: (i, j + nb)),
                  pl.BlockSpec((HALO, bc), lambda j, i: (jnp.maximum(i * (bt // HALO) - 1, 0), j + nb)),
                  pl.BlockSpec((K, bc), lambda j, i: (0, j)),
                  pl.BlockSpec((1, bc), lambda j, i: (0, j))],
        out_specs=[blk, blk, pl.BlockSpec((1, bc), lambda j, i: (0, j)), pl.BlockSpec((K, bc), lambda j, i: (0, j))],
        out_shape=[jax.ShapeDtypeStruct((T, F2), BF16), jax.ShapeDtypeStruct((T, F), F32),
                   jax.ShapeDtypeStruct((1, F), F32), jax.ShapeDtypeStruct((K, F), F32)],
        scratch_shapes=[pltpu.VMEM((bt + HALO, bc), F32)],
        compiler_params=_params(("parallel", "arbitrary")),
    )(da, up, up, up, cw, cb)


def _dwconv_bwd_data(dgc, cw, K, into, col, *, name):
    T, C = dgc.shape
    bt = _pick(T, ROW_BLOCK, SUBLANE)
    bc = _pick(C, 1408)
    nt = T // bt
    last_halo = T // HALO - 1
    off = col // bc
    assert off * bc == col

    def kern(d_ref, halo_ref, cw_ref, into_ref, o_ref, ext_ref):
        i = pl.program_id(0)
        ext_ref[pl.ds(0, bt), :] = d_ref[...]
        ext_ref[pl.ds(bt, HALO), :] = jnp.where(i < nt - 1, halo_ref[...], 0.0)
        ext = ext_ref[...]
        out = None
        for k in range(K):
            ahead = K - 1 - k
            tap = (ext if ahead == 0 else pltpu.roll(ext, bt + HALO - ahead, 0))[0:bt]
            term = cw_ref[k:k + 1, :] * tap
            out = term if out is None else out + term
        o_ref[...] = out.astype(o_ref.dtype)

    return pl.pallas_call(
        kern, name=name, grid=(nt, C // bc),
        in_specs=[pl.BlockSpec((bt, bc), lambda i, j: (i, j)),
                  pl.BlockSpec((HALO, bc), lambda i, j: (jnp.minimum((i + 1) * (bt // HALO), last_halo), j)),
                  pl.BlockSpec((K, bc), lambda i, j: (0, j)), _ANY],
        out_specs=pl.BlockSpec((bt, bc), lambda i, j: (i, j + off)),
        out_shape=jax.ShapeDtypeStruct(into.shape, into.dtype), input_output_aliases={3: 0},
        scratch_shapes=[pltpu.VMEM((bt + HALO, bc), F32)],
        compiler_params=_params(("parallel", "parallel")),
    )(dgc, dgc, cw, into)


def _ple_fwd(h, G, bg, pp, *, name):
    T, D = h.shape
    bt = _pick(T, ROW_BLOCK, SUBLANE)

    def kern(h_ref, G_ref, bg_ref, pp_ref, o_ref, ob_ref):
        out = h_ref[...] + _sigmoid(G_ref[...] + bg_ref[...]) * pp_ref[...]
        o_ref[...] = out
        ob_ref[...] = out.astype(BF16)

    row = pl.BlockSpec((bt, D), lambda i: (i, 0))
    vec = pl.BlockSpec((1, D), lambda i: (0, 0))
    return pl.pallas_call(
        kern, name=name, grid=(T // bt,), in_specs=[row, row, vec, row], out_specs=[row, row],
        out_shape=[jax.ShapeDtypeStruct((T, D), F32), jax.ShapeDtypeStruct((T, D), BF16)],
        compiler_params=_params(("parallel",)),
    )(h, G, bg, pp)


def _ple_bwd(dx, G, bg, pp, *, name):
    T, D = dx.shape
    bt = _pick(T, ROW_BLOCK, SUBLANE)

    def kern(dx_ref, G_ref, bg_ref, pp_ref, dG_ref, dpp_ref, dbg_ref):
        i = pl.program_id(0)
        gate = _sigmoid(G_ref[...] + bg_ref[...])
        dxv = dx_ref[...]
        dG = dxv * pp_ref[...] * gate * (1.0 - gate)
        dG_ref[...] = dG.astype(BF16)
        dpp_ref[...] = (dxv * gate).astype(BF16)

        @pl.when(i == 0)
        def _():
            dbg_ref[...] = jnp.zeros_like(dbg_ref)

        dbg_ref[...] += jnp.sum(dG, axis=0, keepdims=True)

    row = pl.BlockSpec((bt, D), lambda i: (i, 0))
    vec = pl.BlockSpec((1, D), lambda i: (0, 0))
    return pl.pallas_call(
        kern, name=name, grid=(T // bt,), in_specs=[row, row, vec, row], out_specs=[row, row, vec],
        out_shape=[jax.ShapeDtypeStruct((T, D), BF16), jax.ShapeDtypeStruct((T, D), BF16),
                   jax.ShapeDtypeStruct((1, D), F32)],
        compiler_params=_params(("arbitrary",)),
    )(dx, G, bg, pp)


def _fox_gate_fwd(proj, bf, *, name):
    T = proj.shape[0]
    bt = _pick(T, CUM_BLOCK, SUBLANE)
    fcol = 3 * D_MODEL // LANE

    def kern(f_ref, bf_ref, c_ref, carry_ref):
        i = pl.program_id(0)

        @pl.when(i == 0)
        def _():
            carry_ref[...] = jnp.zeros_like(carry_ref)

        x = f_ref[...] + bf_ref[...]
        lf = jnp.minimum(x, 0.0) - _log1p(jnp.exp(-jnp.abs(x)))
        cs = _tri_matmul(_tri(bt, True), lf) + carry_ref[...]
        c_ref[...] = cs
        carry_ref[...] = cs[bt - 1:bt, :]

    return pl.pallas_call(
        kern, name=name, grid=(T // bt,),
        in_specs=[pl.BlockSpec((bt, LANE), lambda i: (i, fcol)), pl.BlockSpec((1, LANE), lambda i: (0, 0))],
        out_specs=pl.BlockSpec((bt, LANE), lambda i: (i, 0)),
        out_shape=jax.ShapeDtypeStruct((T, LANE), F32),
        scratch_shapes=[pltpu.VMEM((1, LANE), F32)],
        compiler_params=_params(("arbitrary",)),
    )(proj, bf)


def _fox_gate_bwd(dc, proj, bf, *, name):
    T = proj.shape[0]
    bt = _pick(T, CUM_BLOCK, SUBLANE)
    nb = T // bt
    fcol = 3 * D_MODEL // LANE

    def kern(dc_ref, f_ref, bf_ref, df_ref, dbf_ref, carry_ref):
        i = pl.program_id(0)

        @pl.when(i == 0)
        def _():
            carry_ref[...] = jnp.zeros_like(carry_ref)
            dbf_ref[...] = jnp.zeros_like(dbf_ref)

        dlf = _tri_matmul(_tri(bt, False), dc_ref[...]) + carry_ref[...]
        carry_ref[...] = dlf[0:1, :]
        x = f_ref[...] + bf_ref[...]
        lane = lax.broadcasted_iota(jnp.int32, (bt, LANE), 1)
        df = jnp.where(lane < ATTN_HEADS, dlf / (1.0 + jnp.exp(x)), 0.0)
        df_ref[...] = df
        dbf_ref[...] += jnp.sum(df, axis=0, keepdims=True)

    return pl.pallas_call(
        kern, name=name, grid=(nb,),
        in_specs=[pl.BlockSpec((bt, LANE), lambda i: (nb - 1 - i, 0)),
                  pl.BlockSpec((bt, LANE), lambda i: (nb - 1 - i, fcol)),
                  pl.BlockSpec((1, LANE), lambda i: (0, 0))],
        out_specs=[pl.BlockSpec((bt, LANE), lambda i: (nb - 1 - i, 0)), pl.BlockSpec((1, LANE), lambda i: (0, 0))],
        out_shape=[jax.ShapeDtypeStruct((T, LANE), F32), jax.ShapeDtypeStruct((1, LANE), F32)],
        scratch_shapes=[pltpu.VMEM((1, LANE), F32)],
        compiler_params=_params(("arbitrary",)),
    )(dc, proj, bf)


_NT = (((1,), (1,)), ((), ()))
_TN = (((0,), (0,)), ((), ()))


def _dot(a, b, dims=None):
    if dims is None:
        return jnp.dot(a, b, preferred_element_type=F32)
    return lax.dot_general(a, b, dims, preferred_element_type=F32)


LOG2E = 1.0 / math.log(2.0)
LN2 = math.log(2.0)
Q_SCALE = 0.125 * LOG2E
HALF = LANE // 2
L_LANE = (HALF, 0)
FWD_PAIRS = 4
BWD_PAIRS = 1


def _attn_prep(proj, *, name):
    T = proj.shape[0]
    bt = _pick(T, ATTN_BLOCK)

    def kern(q_ref, k_ref, v_ref, qa_ref, qb_ref, kk_ref, ka_ref, kb_ref, vv_ref, va_ref, vb_ref):
        lane = lax.broadcasted_iota(jnp.int32, (bt, LANE), 1)
        lo = lane < HALF
        q = q_ref[...] * Q_SCALE
        k = k_ref[...]
        v = v_ref[...]
        qa_ref[...] = jnp.where(lo, q, 0.0).astype(BF16)
        qb_ref[...] = jnp.where(lo, 0.0, q).astype(BF16)
        kk_ref[...] = k.astype(BF16)
        ka_ref[...] = jnp.where(lo, k, 0.0).astype(BF16)
        kb_ref[...] = jnp.where(lo, 0.0, k).astype(BF16)
        vv_ref[...] = v.astype(BF16)
        va_ref[...] = jnp.where(lo, v, jnp.where(lane == L_LANE[0], 1.0, 0.0)).astype(BF16)
        vb_ref[...] = jnp.where(lo, jnp.where(lane == L_LANE[1], 1.0, 0.0), v).astype(BF16)

    kcol, vcol = D_MODEL // LANE, 2 * D_MODEL // LANE
    out = pl.BlockSpec((bt, LANE), lambda i, hp: (i, hp))
    return pl.pallas_call(
        kern, name=name, grid=(T // bt, HEAD_PAIRS),
        in_specs=[out, pl.BlockSpec((bt, LANE), lambda i, hp: (i, kcol + hp)),
                  pl.BlockSpec((bt, LANE), lambda i, hp: (i, vcol + hp))],
        out_specs=[out] * 8, out_shape=[jax.ShapeDtypeStruct((T, D_MODEL), BF16)] * 8,
        compiler_params=_params(("parallel", "parallel")),
    )(proj, proj, proj)


def _attn_fwd(qa, qb, kk, va, vb, cT, *, name):
    T = qa.shape[0]
    tb = _pick(T, ATTN_BLOCK)
    nq = T // tb
    rep = tb // LANE
    width = FWD_PAIRS * LANE

    def kern(qa_ref, qb_ref, k_ref, va_ref, vb_ref, c_ref, o_ref, ob_ref, lsea_ref, lseb_ref, m_ref, acc_ref):
        qi = pl.program_id(1)
        ki = pl.program_id(2)

        @pl.when(ki == 0)
        def _():
            m_ref[...] = jnp.full_like(m_ref, NEG)
            acc_ref[...] = jnp.zeros_like(acc_ref)

        def step(diag):
            for pp in range(FWD_PAIRS):
                cols = slice(pp * LANE, (pp + 1) * LANE)
                k = k_ref[:, cols]
                for h, (q_ref, v_ref) in enumerate(((qa_ref, va_ref), (qb_ref, vb_ref))):
                    i = 2 * pp + h
                    s = _dot(q_ref[:, cols], k, _NT) - c_ref[pp, h:h + 1, :]
                    if diag:
                        r = lax.broadcasted_iota(jnp.int32, (tb, tb), 0)
                        c = lax.broadcasted_iota(jnp.int32, (tb, tb), 1)
                        s = jnp.where(c <= r, s, NEG)
                    m_prev = m_ref[i]
                    m_new = jnp.maximum(m_prev, jnp.max(s, axis=1, keepdims=True))
                    p = jnp.exp2(s - jnp.tile(m_new, (1, rep)))
                    acc_ref[i] = acc_ref[i] * jnp.exp2(m_prev - m_new) + _dot(p.astype(BF16), v_ref[:, cols])
                    m_ref[i] = m_new

        @pl.when(ki < qi)
        def _():
            step(False)

        @pl.when(ki == qi)
        def _():
            step(True)
            lo = lax.broadcasted_iota(jnp.int32, (tb, LANE), 1) < HALF
            for pp in range(FWD_PAIRS):
                cols = slice(pp * LANE, (pp + 1) * LANE)
                a0, a1 = acc_ref[2 * pp], acc_ref[2 * pp + 1]
                l0 = a0[:, L_LANE[0]:L_LANE[0] + 1]
                l1 = a1[:, L_LANE[1]:L_LANE[1] + 1]
                o = jnp.where(lo, a0 / l0, a1 / l1)
                o_ref[:, cols] = o
                ob_ref[:, cols] = o.astype(BF16)
                lsea_ref[:, cols] = m_ref[2 * pp] + jnp.log(l0) * LOG2E
                lseb_ref[:, cols] = m_ref[2 * pp + 1] + jnp.log(l1) * LOG2E

    qspec = pl.BlockSpec((tb, width), lambda g, qi, ki: (qi, g))
    kspec = pl.BlockSpec((tb, width), lambda g, qi, ki: (jnp.minimum(ki, qi), g))
    return pl.pallas_call(
        kern, name=name, grid=(HEAD_PAIRS // FWD_PAIRS, nq, nq),
        in_specs=[qspec, qspec, kspec, kspec, kspec,
                  pl.BlockSpec((FWD_PAIRS, 2, tb), lambda g, qi, ki: (g, 0, jnp.minimum(ki, qi)))],
        out_specs=[qspec, qspec, qspec, qspec],
        out_shape=[jax.ShapeDtypeStruct((T, D_MODEL), F32), jax.ShapeDtypeStruct((T, D_MODEL), BF16),
                   jax.ShapeDtypeStruct((T, D_MODEL), F32), jax.ShapeDtypeStruct((T, D_MODEL), F32)],
        scratch_shapes=[pltpu.VMEM((2 * FWD_PAIRS, tb, LANE), F32), pltpu.VMEM((2 * FWD_PAIRS, tb, LANE), F32)],
        compiler_params=_params(("parallel", "parallel", "arbitrary")),
    )(qa, qb, kk, va, vb, cT)


def _attn_bwd_prep(do, o, *, name):
    T, D = do.shape
    bt = _pick(T, ATTN_BLOCK)

    def kern(do_ref, o_ref, doa_ref, dob_ref, dlta_ref, dltb_ref):
        lo = lax.broadcasted_iota(jnp.int32, (bt, LANE), 1) < HALF
        dov = do_ref[...]
        prod = dov * o_ref[...]
        doa_ref[...] = jnp.where(lo, dov, 0.0).astype(BF16)
        dob_ref[...] = jnp.where(lo, 0.0, dov).astype(BF16)
        dlta_ref[...] = jnp.broadcast_to(jnp.sum(jnp.where(lo, prod, 0.0), axis=1, keepdims=True), (bt, LANE))
        dltb_ref[...] = jnp.broadcast_to(jnp.sum(jnp.where(lo, 0.0, prod), axis=1, keepdims=True), (bt, LANE))

    blk = pl.BlockSpec((bt, LANE), lambda i, hp: (i, hp))
    return pl.pallas_call(
        kern, name=name, grid=(T // bt, HEAD_PAIRS), in_specs=[blk, blk], out_specs=[blk] * 4,
        out_shape=[jax.ShapeDtypeStruct((T, D), BF16)] * 2 + [jax.ShapeDtypeStruct((T, D), F32)] * 2,
        compiler_params=_params(("parallel", "parallel")),
    )(do, o)


def _attn_bwd(qa, qb, kk, ka, kb, vv, doa, dob, lsea, lseb, dlta, dltb, cT, *, name):
    T = qa.shape[0]
    tb = _pick(T, ATTN_BLOCK)
    nq = T // tb
    rep = tb // LANE
    width = BWD_PAIRS * LANE

    def kern(qa_ref, qb_ref, k_ref, ka_ref, kb_ref, v_ref, doa_ref, dob_ref, lsea_ref, lseb_ref, dlta_ref, dltb_ref,
             c_ref, dq_ref, dk_ref, dv_ref, dc_ref, dcq_ref):
        g = pl.program_id(0)
        ki = pl.program_id(1)
        qi = pl.program_id(2)
        first = jnp.logical_and(ki == 0, qi == 0)

        @pl.when(first)
        def _():
            dq_ref[...] = jnp.zeros_like(dq_ref)

        @pl.when(jnp.logical_and(first, g == 0))
        def _():
            dcq_ref[...] = jnp.zeros_like(dcq_ref)

        @pl.when(qi == 0)
        def _():
            dk_ref[...] = jnp.zeros_like(dk_ref)
            dv_ref[...] = jnp.zeros_like(dv_ref)
            dc_ref[...] = jnp.zeros_like(dc_ref)

        def step(diag):
            rows = pl.ds(pl.multiple_of(qi * tb, tb), tb)
            lane = lax.broadcasted_iota(jnp.int32, (tb, LANE), 1)
            row_sums = jnp.zeros((tb, LANE), F32)
            for pp in range(BWD_PAIRS):
                cols = slice(pp * LANE, (pp + 1) * LANE)
                k = k_ref[:, cols]
                v = v_ref[:, cols]
                dq = None
                dk = None
                dv = None
                heads = ((qa_ref, ka_ref, doa_ref, lsea_ref, dlta_ref), (qb_ref, kb_ref, dob_ref, lseb_ref, dltb_ref))
                for h, (q_ref, km_ref, do_ref, lse_ref, dlt_ref) in enumerate(heads):
                    q = q_ref[:, cols]
                    dom = do_ref[:, cols]
                    s = _dot(q, k, _NT) - c_ref[pp, h:h + 1, :]
                    if diag:
                        r = lax.broadcasted_iota(jnp.int32, (tb, tb), 0)
                        c = lax.broadcasted_iota(jnp.int32, (tb, tb), 1)
                        s = jnp.where(c <= r, s, NEG)
                    p = jnp.exp2(s - jnp.tile(lse_ref[:, cols], (1, rep)))
                    ds = p * (_dot(dom, v, _NT) - jnp.tile(dlt_ref[:, cols], (1, rep)))
                    dc_ref[pp, h:h + 1, :] -= jnp.sum(ds, axis=0, keepdims=True)
                    head = 2 * (BWD_PAIRS * g + pp) + h
                    row_sums = jnp.where(lane == head, jnp.sum(ds, axis=1, keepdims=True), row_sums)
                    dsb = ds.astype(BF16)
                    tv = _dot(p.astype(BF16), dom, _TN)
                    tk = _dot(dsb, q, _TN)
                    tq = _dot(dsb, km_ref[:, cols])
                    dv = tv if dv is None else dv + tv
                    dk = tk if dk is None else dk + tk
                    dq = tq if dq is None else dq + tq
                dv_ref[:, cols] += dv
                dk_ref[:, cols] += dk * LN2
                dq_ref[rows, cols] += dq * 0.125
            dcq_ref[rows, :] += row_sums

        @pl.when(qi > ki)
        def _():
            step(False)

        @pl.when(qi == ki)
        def _():
            step(True)

    qspec = pl.BlockSpec((tb, width), lambda g, ki, qi: (jnp.maximum(qi, ki), g))
    kspec = pl.BlockSpec((tb, width), lambda g, ki, qi: (ki, g))
    cspec = pl.BlockSpec((BWD_PAIRS, 2, tb), lambda g, ki, qi: (g, 0, ki))
    qacc = pl.BlockSpec((T, width), lambda g, ki, qi: (0, g), pipeline_mode=pl.Buffered(1))
    cqacc = pl.BlockSpec((T, LANE), lambda g, ki, qi: (0, 0), pipeline_mode=pl.Buffered(1))
    return pl.pallas_call(
        kern, name=name, grid=(HEAD_PAIRS // BWD_PAIRS, nq, nq),
        in_specs=[qspec, qspec, kspec, kspec, kspec, kspec, qspec, qspec, qspec, qspec, qspec, qspec, cspec],
        out_specs=[qacc, kspec, kspec, cspec, cqacc],
        out_shape=[jax.ShapeDtypeStruct((T, D_MODEL), F32)] * 3 + [jax.ShapeDtypeStruct((HEAD_PAIRS, 2, T), F32),
                                                                   jax.ShapeDtypeStruct((T, LANE), F32)],
        compiler_params=_params(("arbitrary", "arbitrary", "arbitrary"), vmem_mb=56),
    )(qa, qb, kk, ka, kb, vv, doa, dob, lsea, lseb, dlta, dltb, cT)


def _ssd_dt_fwd(proj, dt_bias, a_log, *, name):
    T = proj.shape[0]
    Q = SSM_CHUNK
    col = (2 * SSM_D_INNER + 2 * SSM_GROUPS * SSM_STATE) // LANE

    def kern(raw_ref, b_ref, al_ref, dt_ref, ac_ref):
        dt = _softplus(raw_ref[...] + b_ref[...])
        dt_ref[...] = dt
        ac_ref[...] = _tri_matmul(_tri(Q, True), dt * (-jnp.exp(al_ref[...])))

    vec = pl.BlockSpec((1, LANE), lambda i: (0, 0))
    blk = pl.BlockSpec((Q, LANE), lambda i: (i, 0))
    return pl.pallas_call(
        kern, name=name, grid=(T // Q,),
        in_specs=[pl.BlockSpec((Q, LANE), lambda i: (i, col)), vec, vec], out_specs=[blk, blk],
        out_shape=[jax.ShapeDtypeStruct((T, LANE), F32)] * 2,
        compiler_params=_params(("parallel",)),
    )(proj, dt_bias, a_log)


def _ssd_dt_bwd(da_a, da_b, ddt_a, ddt_b, dt, proj, dt_bias, a_log, into, *, name):
    T = proj.shape[0]
    Q = SSM_CHUNK
    col = (2 * SSM_D_INNER + 2 * SSM_GROUPS * SSM_STATE) // LANE

    def kern(daa_ref, dab_ref, dda_ref, ddb_ref, dt_ref, raw_ref, b_ref, al_ref, into_ref, draw_ref, dal_ref, db_ref,
             acc_ref):
        i = pl.program_id(0)

        @pl.when(i == 0)
        def _():
            acc_ref[...] = jnp.zeros_like(acc_ref)
            db_ref[...] = jnp.zeros_like(db_ref)

        A = -jnp.exp(al_ref[...])
        ddA = _tri_matmul(_tri(Q, False), daa_ref[...] + dab_ref[...])
        ddt = dda_ref[...] + ddb_ref[...] + ddA * A
        acc_ref[...] += jnp.sum(ddA * dt_ref[...], axis=0, keepdims=True)
        lane = lax.broadcasted_iota(jnp.int32, (Q, LANE), 1)
        draw = jnp.where(lane < SSM_HEADS, ddt * _sigmoid(raw_ref[...] + b_ref[...]), 0.0)
        draw_ref[...] = draw.astype(BF16)
        db_ref[...] += jnp.sum(draw, axis=0, keepdims=True)
        dal_ref[...] = acc_ref[...] * A

    vec = pl.BlockSpec((1, LANE), lambda i: (0, 0))
    blk = pl.BlockSpec((Q, LANE), lambda i: (i, 0))
    return pl.pallas_call(
        kern, name=name, grid=(T // Q,),
        in_specs=[blk, blk, blk, blk, blk, pl.BlockSpec((Q, LANE), lambda i: (i, col)), vec, vec, _ANY],
        out_specs=[pl.BlockSpec((Q, LANE), lambda i: (i, col)), vec, vec],
        out_shape=[jax.ShapeDtypeStruct(into.shape, into.dtype), jax.ShapeDtypeStruct((1, LANE), F32),
                   jax.ShapeDtypeStruct((1, LANE), F32)],
        input_output_aliases={8: 0},
        scratch_shapes=[pltpu.VMEM((1, LANE), F32)],
        compiler_params=_params(("arbitrary",)),
    )(da_a, da_b, ddt_a, ddt_b, dt, proj, dt_bias, a_log, into)


def _conv_silu_fwd(proj, cw, cb, *, name):
    T = proj.shape[0]
    C = SSM_XBC
    bt = _pick(T, ROW_BLOCK, SUBLANE)
    bc = 1024
    off = SSM_D_INNER // bc

    def kern(x_ref, halo_ref, cw_ref, cb_ref, o_ref, ext_ref):
        i = pl.program_id(0)
        _fill_ext_past(ext_ref, halo_ref, x_ref[...], i, bt)
        pre = cb_ref[...] + _conv_past(_past_taps(ext_ref, SSM_CONV, bt), cw_ref)
        o_ref[...] = pre * _sigmoid(pre)

    return pl.pallas_call(
        kern, name=name, grid=(T // bt, C // bc),
        in_specs=[pl.BlockSpec((bt, bc), lambda i, j: (i, j + off)), _halo_prev(bt, bc, off),
                  pl.BlockSpec((SSM_CONV, bc), lambda i, j: (0, j)), pl.BlockSpec((1, bc), lambda i, j: (0, j))],
        out_specs=pl.BlockSpec((bt, bc), lambda i, j: (i, j)),
        out_shape=jax.ShapeDtypeStruct((T, C), F32),
        scratch_shapes=[pltpu.VMEM((bt + HALO, bc), F32)],
        compiler_params=_params(("parallel", "parallel")),
    )(proj, proj, cw, cb)


def _conv_silu_bwd(dxbc, proj, cw, cb, *, name):
    T = proj.shape[0]
    C = SSM_XBC
    K = SSM_CONV
    bt = _pick(T, ROW_BLOCK, SUBLANE)
    bc = 1024
    off = SSM_D_INNER // bc

    def kern(d_ref, x_ref, halo_ref, cw_ref, cb_ref, dpre_ref, dcb_ref, dcw_ref, ext_ref):
        i = pl.program_id(1)
        _fill_ext_past(ext_ref, halo_ref, x_ref[...], i, bt)
        taps = _past_taps(ext_ref, K, bt)
        pre = cb_ref[...] + _conv_past(taps, cw_ref)
        sg = _sigmoid(pre)
        dpre = d_ref[...] * sg * (1.0 + pre * (1.0 - sg))
        dpre_ref[...] = dpre

        @pl.when(i == 0)
        def _():
            dcb_ref[...] = jnp.zeros_like(dcb_ref)
            dcw_ref[...] = jnp.zeros_like(dcw_ref)

        dcb_ref[...] += jnp.sum(dpre, axis=0, keepdims=True)
        for k in range(K):
            dcw_ref[k:k + 1, :] += jnp.sum(dpre * taps[k], axis=0, keepdims=True)

    blk = pl.BlockSpec((bt, bc), lambda j, i: (i, j))
    return pl.pallas_call(
        kern, name=name, grid=(C // bc, T // bt),
        in_specs=[blk, pl.BlockSpec((bt, bc), lambda j, i: (i, j + off)),
                  pl.BlockSpec((HALO, bc), lambda j, i: (jnp.maximum(i * (bt // HALO) - 1, 0), j + off)),
                  pl.BlockSpec((K, bc), lambda j, i: (0, j)), pl.BlockSpec((1, bc), lambda j, i: (0, j))],
        out_specs=[blk, pl.BlockSpec((1, bc), lambda j, i: (0, j)), pl.BlockSpec((K, bc), lambda j, i: (0, j))],
        out_shape=[jax.ShapeDtypeStruct((T, C), F32), jax.ShapeDtypeStruct((1, C), F32),
                   jax.ShapeDtypeStruct((K, C), F32)],
        scratch_shapes=[pltpu.VMEM((bt + HALO, bc), F32)],
        compiler_params=_params(("parallel", "arbitrary")),
    )(dxbc, proj, proj, cw, cb)


_GP = SSM_D_INNER // SSM_GROUPS
_HPG = SSM_HEADS // SSM_GROUPS
_PH = SSM_D_INNER // SSM_HEADS


def _head_masks(rows):
    lane = lax.broadcasted_iota(jnp.int32, (rows, _GP), 1)
    return [jnp.logical_and(lane >= r * _PH, lane < (r + 1) * _PH) for r in range(_HPG)]


def _ssd_cols(g):
    x0 = g * _GP
    b0 = SSM_D_INNER + g * SSM_STATE
    c0 = SSM_D_INNER + (SSM_GROUPS + g) * SSM_STATE
    return slice(x0, x0 + _GP), slice(b0, b0 + SSM_STATE), slice(c0, c0 + SSM_STATE)


def _ssd_specs(idx):
    Q, N = SSM_CHUNK, SSM_STATE
    return dict(
        xbc=pl.BlockSpec((Q, SSM_XBC), lambda j: (idx(j), 0)),
        x=pl.BlockSpec((Q, SSM_D_INNER), lambda j: (idx(j), 0)),
        col=pl.BlockSpec((Q, LANE), lambda j: (idx(j), 0)),
        row=pl.BlockSpec((SSM_HEADS, Q), lambda j: (0, idx(j))),
        st=pl.BlockSpec((N, SSM_D_INNER), lambda j: (idx(j), 0)),
    )


def _ssd_scan_fwd(xbc, dtc, acc_, dtr, acr, *, name):
    T = xbc.shape[0]
    Q, N = SSM_CHUNK, SSM_STATE
    nc = T // Q
    sp = _ssd_specs(lambda j: j)

    def kern(xbc_ref, dtc_ref, ac_ref, dtr_ref, ar_ref, ys_ref, st_ref, state_ref):
        @pl.when(pl.program_id(0) == 0)
        def _():
            state_ref[...] = jnp.zeros_like(state_ref)

        r_i = lax.broadcasted_iota(jnp.int32, (Q, Q), 0)
        c_i = lax.broadcasted_iota(jnp.int32, (Q, Q), 1)
        tri = c_i <= r_i
        masks = _head_masks(Q)
        masks1 = _head_masks(1)
        for g in range(SSM_GROUPS):
            xs, bs, cs = _ssd_cols(g)
            S = state_ref[g]
            st_ref[:, xs] = S
            x = xbc_ref[:, xs]
            xb = x.astype(BF16)
            Bb = xbc_ref[:, bs].astype(BF16)
            Cb = xbc_ref[:, cs].astype(BF16)
            CB = _dot(Cb, Bb, _NT)
            y = jnp.zeros((Q, _GP), F32)
            El = jnp.zeros((Q, _GP), F32)
            Wl = jnp.zeros((Q, _GP), F32)
            decl = jnp.zeros((1, _GP), F32)
            for r in range(_HPG):
                h = g * _HPG + r
                a_c = ac_ref[:, h:h + 1]
                a_r = ar_ref[h:h + 1, :]
                dt_c = dtc_ref[:, h:h + 1]
                dt_r = dtr_ref[h:h + 1, :]
                L = jnp.exp(jnp.where(tri, a_c - a_r, NEG))
                W = CB * L * dt_r
                y = jnp.where(masks[r], _dot(W.astype(BF16), xb), y)
                a_q = a_c[Q - 1:Q, :]
                El = jnp.where(masks[r], jnp.exp(a_c), El)
                Wl = jnp.where(masks[r], jnp.exp(a_q - a_c) * dt_c, Wl)
                decl = jnp.where(masks1[r], jnp.exp(a_q), decl)
            ys_ref[:, xs] = y + _dot(Cb, S.astype(BF16)) * El
            state_ref[g] = S * decl + _dot(Bb, (x * Wl).astype(BF16), _TN)

    return pl.pallas_call(
        kern, name=name, grid=(nc,),
        in_specs=[sp["xbc"], sp["col"], sp["col"], sp["row"], sp["row"]],
        out_specs=[sp["x"], sp["st"]],
        out_shape=[jax.ShapeDtypeStruct((T, SSM_D_INNER), F32), jax.ShapeDtypeStruct((nc * N, SSM_D_INNER), F32)],
        scratch_shapes=[pltpu.VMEM((SSM_GROUPS, N, _GP), F32)],
        compiler_params=_params(("arbitrary",)),
    )(xbc, dtc, acc_, dtr, acr)


def _ssd_scan_bwd(xbc, dys, dskip, st, dtc, acc_, dtr, acr, *, name):
    T = xbc.shape[0]
    Q, N = SSM_CHUNK, SSM_STATE
    nc = T // Q
    sp = _ssd_specs(lambda j: nc - 1 - j)

    def kern(xbc_ref, dy_ref, dsk_ref, st_ref, dtc_ref, ac_ref, dtr_ref, ar_ref,
             dxbc_ref, dac_ref, dar_ref, ddc_ref, ddr_ref, dstate_ref):
        @pl.when(pl.program_id(0) == 0)
        def _():
            dstate_ref[...] = jnp.zeros_like(dstate_ref)

        r_i = lax.broadcasted_iota(jnp.int32, (Q, Q), 0)
        c_i = lax.broadcasted_iota(jnp.int32, (Q, Q), 1)
        tri = c_i <= r_i
        last_row = lax.broadcasted_iota(jnp.int32, (Q, 1), 0) == Q - 1
        lane128 = lax.broadcasted_iota(jnp.int32, (Q, LANE), 1)
        masks = _head_masks(Q)
        masksN = _head_masks(N)
        masks1 = _head_masks(1)
        zeros = jnp.zeros((Q, _GP), F32)
        dacol = jnp.zeros((Q, LANE), F32)
        ddcol = jnp.zeros((Q, LANE), F32)
        for g in range(SSM_GROUPS):
            xs, bs, cs = _ssd_cols(g)
            dS = dstate_ref[g]
            dSb = dS.astype(BF16)
            S = st_ref[:, xs]
            Sb = S.astype(BF16)
            x = xbc_ref[:, xs]
            xb = x.astype(BF16)
            Bb = xbc_ref[:, bs].astype(BF16)
            Cb = xbc_ref[:, cs].astype(BF16)
            dy = dy_ref[:, xs]
            CB = _dot(Cb, Bb, _NT)
            BdS = _dot(Bb, dSb)
            hx = BdS * x
            yd = _dot(Cb, Sb) * dy
            dSS = dS * S
            dxi, El, Wl = zeros, zeros, zeros
            decl = jnp.zeros((1, _GP), F32)
            dBacc = jnp.zeros((Q, N), F32)
            dCacc = jnp.zeros((Q, N), F32)
            for r in range(_HPG):
                h = g * _HPG + r
                hm = masks[r]
                a_c = ac_ref[:, h:h + 1]
                a_r = ar_ref[h:h + 1, :]
                dt_c = dtc_ref[:, h:h + 1]
                dt_r = dtr_ref[h:h + 1, :]
                L = jnp.exp(jnp.where(tri, a_c - a_r, NEG))
                GL = CB * L
                W = GL * dt_r
                dym = jnp.where(hm, dy, 0.0).astype(BF16)
                dW = _dot(dym, xb, _NT)
                E = dW * W
                da_c = jnp.sum(E, axis=1, keepdims=True)
                dar_ref[h:h + 1, :] = -jnp.sum(E, axis=0, keepdims=True)
                ddr_ref[h:h + 1, :] = jnp.sum(dW * GL, axis=0, keepdims=True)
                dGb = (dW * L * dt_r).astype(BF16)
                dCacc = dCacc + _dot(dGb, Bb)
                dBacc = dBacc + _dot(dGb, Cb, _TN)
                dxi = dxi + _dot(W.astype(BF16), dym, _TN)
                a_q = a_c[Q - 1:Q, :]
                e_c = jnp.exp(a_c)
                eq_c = jnp.exp(a_q - a_c)
                w_c = eq_c * dt_c
                ydr = jnp.sum(jnp.where(hm, yd, 0.0), axis=1, keepdims=True) * e_c
                h_c = jnp.sum(jnp.where(hm, hx, 0.0), axis=1, keepdims=True)
                hw = h_c * w_c
                dss = jnp.sum(jnp.sum(jnp.where(masksN[r], dSS, 0.0), axis=1, keepdims=True), axis=0, keepdims=True)
                s_q = jnp.sum(hw, axis=0, keepdims=True) + jnp.exp(a_q) * dss
                da_c = da_c + ydr - hw + jnp.where(last_row, s_q, 0.0)
                dacol = jnp.where(lane128 == h, da_c, dacol)
                ddcol = jnp.where(lane128 == h, h_c * eq_c, ddcol)
                El = jnp.where(hm, e_c, El)
                Wl = jnp.where(hm, w_c, Wl)
                decl = jnp.where(masks1[r], jnp.exp(a_q), decl)
            dxbc_ref[:, xs] = dxi + BdS * Wl + dsk_ref[:, xs]
            dxbc_ref[:, bs] = dBacc + _dot((x * Wl).astype(BF16), dSb, _NT)
            dyE = (dy * El).astype(BF16)
            dxbc_ref[:, cs] = dCacc + _dot(dyE, Sb, _NT)
            dstate_ref[g] = dS * decl + _dot(Cb, dyE, _TN)
        dac_ref[...] = dacol
        ddc_ref[...] = ddcol

    return pl.pallas_call(
        kern, name=name, grid=(nc,),
        in_specs=[sp["xbc"], sp["x"], sp["x"], sp["st"], sp["col"], sp["col"], sp["row"], sp["row"]],
        out_specs=[sp["xbc"], sp["col"], sp["row"], sp["col"], sp["row"]],
        out_shape=[jax.ShapeDtypeStruct((T, SSM_XBC), F32),
                   jax.ShapeDtypeStruct((T, LANE), F32), jax.ShapeDtypeStruct((SSM_HEADS, T), F32),
                   jax.ShapeDtypeStruct((T, LANE), F32), jax.ShapeDtypeStruct((SSM_HEADS, T), F32)],
        scratch_shapes=[pltpu.VMEM((SSM_GROUPS, N, _GP), F32)],
        compiler_params=_params(("arbitrary",)),
    )(xbc, dys, dskip, st, dtc, acc_, dtr, acr)


def _gate_norm_fwd(ys, xbc, proj, d_exp, norm_w, *, name):
    T = ys.shape[0]
    bt = _pick(T, NARROW_ROW_BLOCK, SUBLANE)

    def kern(ys_ref, x_ref, z_ref, d_ref, w_ref, o_ref):
        z = z_ref[...]
        yz = (ys_ref[...] + d_ref[...] * x_ref[...]) * (z * _sigmoid(z))
        rstd = lax.rsqrt(jnp.mean(yz * yz, axis=-1, keepdims=True) + RMS_EPS)
        o_ref[...] = (yz * rstd * w_ref[...]).astype(BF16)

    blk = pl.BlockSpec((bt, _GP), lambda i, g: (i, g))
    vec = pl.BlockSpec((1, _GP), lambda i, g: (0, g))
    return pl.pallas_call(
        kern, name=name, grid=(T // bt, SSM_GROUPS), in_specs=[blk, blk, blk, vec, vec], out_specs=blk,
        out_shape=jax.ShapeDtypeStruct((T, SSM_D_INNER), BF16), compiler_params=_params(("parallel", "parallel")),
    )(ys, xbc, proj, d_exp, norm_w)


def _gate_norm_bwd(dyn, ys, xbc, proj, d_exp, norm_w, *, name):
    T = ys.shape[0]
    bt = _pick(T, NARROW_ROW_BLOCK, SUBLANE)

    def kern(dyn_ref, ys_ref, x_ref, z_ref, d_ref, w_ref, dz_ref, dys_ref, dsk_ref, dw_ref, dd_ref):
        i = pl.program_id(1)
        z = z_ref[...]
        x = x_ref[...]
        sg = _sigmoid(z)
        sz = z * sg
        y = ys_ref[...] + d_ref[...] * x
        yz = y * sz
        rstd = lax.rsqrt(jnp.mean(yz * yz, axis=-1, keepdims=True) + RMS_EPS)
        yhat = yz * rstd
        dynv = dyn_ref[...]
        gg = dynv * w_ref[...]
        dyz = rstd * (gg - yhat * jnp.mean(gg * yhat, axis=-1, keepdims=True))
        dy = dyz * sz
        dz_ref[...] = (dyz * y * sg * (1.0 + z * (1.0 - sg))).astype(BF16)
        dys_ref[...] = dy
        dsk_ref[...] = dy * d_ref[...]

        @pl.when(i == 0)
        def _():
            dw_ref[...] = jnp.zeros_like(dw_ref)
            dd_ref[...] = jnp.zeros_like(dd_ref)

        dw_ref[...] += jnp.sum(dynv * yhat, axis=0, keepdims=True)
        dd_ref[...] += jnp.sum(dy * x, axis=0, keepdims=True)

    blk = pl.BlockSpec((bt, _GP), lambda g, i: (i, g))
    vec = pl.BlockSpec((1, _GP), lambda g, i: (0, g))
    act = jax.ShapeDtypeStruct((T, SSM_D_INNER), F32)
    par = jax.ShapeDtypeStruct((1, SSM_D_INNER), F32)
    return pl.pallas_call(
        kern, name=name, grid=(SSM_GROUPS, T // bt), in_specs=[blk, blk, blk, blk, vec, vec],
        out_specs=[blk, blk, blk, vec, vec],
        out_shape=[jax.ShapeDtypeStruct((T, SSM_IN_PAD), BF16), act, act, par, par],
        compiler_params=_params(("parallel", "arbitrary")),
    )(dyn, ys, xbc, proj, d_exp, norm_w)


def _loss_head(y, target, *, name):
    T, D = y.shape
    bt = _pick(T, ROW_BLOCK, SUBLANE)

    def kern(y_ref, t_ref, l_ref, dy_ref):
        i = pl.program_id(0)
        err = y_ref[...] - t_ref[...]
        dy_ref[...] = err * (1.0 / D)

        @pl.when(i == 0)
        def _():
            l_ref[...] = jnp.zeros_like(l_ref)

        l_ref[...] += jnp.sum(err * err, axis=0, keepdims=True) * (0.5 / D)

    row = pl.BlockSpec((bt, D), lambda i: (i, 0))
    vec = pl.BlockSpec((1, D), lambda i: (0, 0))
    return pl.pallas_call(
        kern, name=name, grid=(T // bt,), in_specs=[row, row], out_specs=[vec, row],
        out_shape=[jax.ShapeDtypeStruct((1, D), F32), jax.ShapeDtypeStruct((T, D), F32)],
        compiler_params=_params(("arbitrary",)),
    )(y, target)


def _adamw(w, g, m, v, *, name):
    shape = w.shape
    w, g, m, v = (t.reshape(-1, shape[-1]) for t in (w, g, m, v))
    R, C = w.shape
    br = _pick(R, 256, SUBLANE)

    def kern(w_ref, g_ref, m_ref, v_ref, d_ref, nm_ref, nv_ref):
        gv = g_ref[...]
        nm = ADAM_B1 * m_ref[...] + (1.0 - ADAM_B1) * gv
        nv = ADAM_B2 * v_ref[...] + (1.0 - ADAM_B2) * (gv * gv)
        m_hat = nm / (1.0 - ADAM_B1 ** ADAM_STEP)
        v_hat = nv / (1.0 - ADAM_B2 ** ADAM_STEP)
        d_ref[...] = -ADAM_LR * (m_hat / (jnp.sqrt(v_hat) + ADAM_EPS) + ADAM_WD * w_ref[...])
        nm_ref[...] = nm
        nv_ref[...] = nv

    blk = pl.BlockSpec((br, C), lambda i: (i, 0))
    outs = pl.pallas_call(
        kern, name=name, grid=(R // br,), in_specs=[blk] * 4, out_specs=[blk] * 3,
        out_shape=[jax.ShapeDtypeStruct((R, C), F32)] * 3, compiler_params=_params(("parallel",)),
    )(w, g, m, v)
    return [o.reshape(shape) for o in outs]


def _add2(a, b, out_dtype, *, name):
    shape = a.shape
    a2, b2 = a.reshape(-1, shape[-1]), b.reshape(-1, shape[-1])
    R, C = a2.shape
    br = _pick(R, 512, SUBLANE)

    def kern(a_ref, b_ref, o_ref):
        o_ref[...] = (a_ref[...].astype(F32) + b_ref[...].astype(F32)).astype(out_dtype)

    blk = pl.BlockSpec((br, C), lambda i: (i, 0))
    return pl.pallas_call(
        kern, name=name, grid=(R // br,), in_specs=[blk, blk], out_specs=blk,
        out_shape=jax.ShapeDtypeStruct((R, C), out_dtype), compiler_params=_params(("parallel",)),
    )(a2, b2).reshape(shape)


def _sum4(buf, *, name):
    _, R, C = buf.shape
    br = _pick(R, 512, SUBLANE)

    def kern(b_ref, o_ref):
        b = [b_ref[k].astype(F32) for k in range(4)]
        o_ref[...] = ((b[0] + b[1]) + b[2]) + b[3]

    return pl.pallas_call(
        kern, name=name, grid=(R // br,), in_specs=[pl.BlockSpec((4, br, C), lambda i: (0, i, 0))],
        out_specs=pl.BlockSpec((br, C), lambda i: (i, 0)),
        out_shape=jax.ShapeDtypeStruct((R, C), F32), compiler_params=_params(("parallel",)),
    )(buf)


def _place():
    x, y, c = lax.axis_index("x"), lax.axis_index("y"), lax.axis_index("c")
    other_chips = [(1 - x, y), (x, 1 - y), (1 - x, 1 - y)]
    return x, y, c, other_chips


def _gather_chips(w, *, name):
    R, C = w.shape
    H = R // 2

    def body(w_ref, out_ref, send_sems, recv_sems):
        x, y, c, chips = _place()
        me_chip = 2 * x + y
        sib = (x, y, 1 - c)

        def rows(chip, hc):
            return out_ref.at[chip, pl.ds(hc * H, H), :]

        def copy(k, blk, to, src=None):
            return pltpu.make_async_remote_copy(
                src_ref=blk if src is None else src, dst_ref=blk, send_sem=send_sems.at[k], recv_sem=recv_sems.at[k],
                device_id=to, device_id_type=MESH)

        first = [copy(j, rows(me_chip, c), (cx, cy, c), src=w_ref.at[pl.ds(c * H, H), :])
                 for j, (cx, cy) in enumerate(chips)]
        for cp in first:
            cp.start()
        passed = []
        for j, (cx, cy) in enumerate(chips):
            blk = rows(2 * cx + cy, c)
            copy(j, blk, (cx, cy, c)).wait_recv()
            fw = copy(3 + j, blk, sib)
            fw.start()
            passed.append(fw)
        for j, (cx, cy) in enumerate(chips):
            copy(3 + j, rows(2 * cx + cy, 1 - c), sib).wait_recv()
        for cp in first + passed:
            cp.wait_send()

    return pl.pallas_call(
        body, name=name, in_specs=[_ANY], out_specs=_ANY,
        out_shape=jax.ShapeDtypeStruct((4, R, C), w.dtype),
        scratch_shapes=[pltpu.SemaphoreType.DMA((6,)), pltpu.SemaphoreType.DMA((6,))],
    )(w)


def _pair_swap(v, *, name, other_half=False):
    shape = (v.shape[0], v.shape[1] // 2, v.shape[2]) if other_half else v.shape

    def body(v_ref, out_ref, send_sem, recv_sem):
        x, y, c, _ = _place()
        src = v_ref.at[:, pl.ds((1 - c) * shape[1], shape[1]), :] if other_half else v_ref
        cp = pltpu.make_async_remote_copy(src_ref=src, dst_ref=out_ref, send_sem=send_sem, recv_sem=recv_sem,
                                          device_id=(x, y, 1 - c), device_id_type=MESH)
        cp.start()
        cp.wait()

    return pl.pallas_call(
        body, name=name, in_specs=[_ANY], out_specs=_ANY, out_shape=jax.ShapeDtypeStruct(shape, v.dtype),
        scratch_shapes=[pltpu.SemaphoreType.DMA, pltpu.SemaphoreType.DMA],
    )(v)


def _chip_exchange(pv, *, name):
    def body(p_ref, out_ref, send_sems, recv_sems):
        x, y, c, chips = _place()
        me_chip = 2 * x + y
        sends = []
        for j, (cx, cy) in enumerate(chips):
            cp = pltpu.make_async_remote_copy(
                src_ref=p_ref.at[2 * cx + cy], dst_ref=out_ref.at[me_chip], send_sem=send_sems.at[j],
                recv_sem=recv_sems.at[j], device_id=(cx, cy, c), device_id_type=MESH)
            cp.start()
            sends.append(cp)
        for j, (cx, cy) in enumerate(chips):
            blk = out_ref.at[2 * cx + cy]
            pltpu.make_async_remote_copy(src_ref=blk, dst_ref=blk, send_sem=send_sems.at[j], recv_sem=recv_sems.at[j],
                                         device_id=(cx, cy, c), device_id_type=MESH).wait_recv()
        for cp in sends:
            cp.wait_send()

    return pl.pallas_call(
        body, name=name, in_specs=[_ANY], out_specs=_ANY, out_shape=jax.ShapeDtypeStruct(pv.shape, pv.dtype),
        scratch_shapes=[pltpu.SemaphoreType.DMA((3,)), pltpu.SemaphoreType.DMA((3,))],
    )(pv)


WEIGHTS = [
    ("attn_w_in", 2), ("attn_b_f", None), ("attn_w_out", 1), ("ssm_w_in", 2), ("ssm_conv_w", 2), ("ssm_conv_b", 1),
    ("ssm_dt_bias", None), ("ssm_A_log", None), ("ssm_D", None), ("ssm_norm_w", 1), ("ssm_w_out", 1),
    ("ln_mix_g", None), ("ln_mix_b", None), ("ffn_w_up", 2), ("ffn_conv_w", 2), ("ffn_conv_b", None),
    ("ffn_w_down", 1), ("ln_ffn_g", None), ("ln_ffn_b", None), ("ple_w_proj", 2), ("ple_w_gate", 1),
    ("ple_b_gate", None),
]
N_CHIPS = 4
MATMUL_WEIGHTS = ("attn_w_in", "attn_w_out", "ssm_w_in", "ssm_w_out", "ffn_w_up", "ffn_w_down", "ple_w_proj",
                  "ple_w_gate")


def _pack(arrays):
    parts = []
    total = 0
    for a in arrays:
        n = a.size
        pad = (-n) % PACK_COLS
        flat = a.reshape(-1)
        parts.append(jnp.pad(flat, (0, pad)) if pad else flat)
        total += n + pad
    rows = total // PACK_COLS
    rpad = (-rows) % PACK_ROW_ALIGN
    if rpad:
        parts.append(jnp.zeros((rpad * PACK_COLS,), arrays[0].dtype))
    return jnp.concatenate(parts).reshape(rows + rpad, PACK_COLS)


def _unpack(buf, shapes):
    flat = buf.reshape(-1)
    out = []
    off = 0
    for s in shapes:
        n = math.prod(s)
        out.append(flat[off:off + n].reshape(s))
        off += n + ((-n) % PACK_COLS)
    return out


def _from_row_layout(a):
    return jnp.pad(a.T, ((0, 0), (0, LANE - SSM_HEADS)))


def _pad_lanes(v, n=LANE):
    return jnp.pad(v, (0, n - v.shape[0])).reshape(1, n)


def _local_step(x, p, target, W):
    T = x.shape[0]
    row = lambda v: v.reshape(1, -1)
    attn_in = jnp.pad(W["attn_w_in"][0], ((0, 0), (0, ATTN_IN_PAD - W["attn_w_in"].shape[2])))
    ssm_in = jnp.pad(W["ssm_w_in"][0], ((0, 0), (0, SSM_IN_PAD - W["ssm_w_in"].shape[2])))
    bf = _pad_lanes(W["attn_b_f"][0])
    dt_bias = _pad_lanes(W["ssm_dt_bias"][0])
    a_log = _pad_lanes(W["ssm_A_log"][0])
    d_exp = jnp.repeat(W["ssm_D"][0], _PH).reshape(1, SSM_D_INNER)
    norm_w = row(W["ssm_norm_w"][0])
    G = {}

    def ffn_ple_fwd(i, xin, mix, tag):
        s = {}
        s["z1"], s["h1"], s["h1b"] = _ln_fwd(xin, mix, row(W["ln_mix_g"][i]), row(W["ln_mix_b"][i]),
                                             name=f"ln_mix_fwd{tag}")
        s["up"] = _mm(s["h1b"], W["ffn_w_up"][i], name=f"ffn_up{tag}")
        s["a"] = _ffn_act_fwd(s["up"], W["ffn_conv_w"][i], row(W["ffn_conv_b"][i]), name=f"ffn_act_fwd{tag}")
        ffn = _mm(s["a"], W["ffn_w_down"][i], name=f"ffn_down{tag}")
        s["z2"], s["h2"], s["h2b"] = _ln_fwd(s["h1"], ffn, row(W["ln_ffn_g"][i]), row(W["ln_ffn_b"][i]),
                                             name=f"ln_ffn_fwd{tag}")
        s["G"] = _mm(s["h2b"], W["ple_w_gate"][i], name=f"ple_gate_mm{tag}")
        s["pp"] = _mm(pb[i], W["ple_w_proj"][i], name=f"ple_proj_mm{tag}")
        out, outb = _ple_fwd(s["h2"], s["G"], row(W["ple_b_gate"][i]), s["pp"], name=f"ple_fwd{tag}")
        return out, outb, s

    def ffn_ple_bwd(i, dx, s, tag):
        g = {}
        dG, dpp, g["ple_b_gate"] = _ple_bwd(dx, s["G"], row(W["ple_b_gate"][i]), s["pp"], name=f"ple_bwd{tag}")
        g["ple_w_gate"] = _mm(s["h2b"], dG, ta=True, name=f"ple_gate_dw{tag}")
        g["ple_w_proj"] = _mm(pb[i], dpp, ta=True, name=f"ple_proj_dw{tag}")
        dh2 = _mm(dG, W["ple_w_gate"][i], tb=True, add=dx, name=f"ple_gate_dx{tag}")
        dz2, dz2b, g["ln_ffn_g"], g["ln_ffn_b"] = _ln_bwd(dh2, s["z2"], row(W["ln_ffn_g"][i]), name=f"ln_ffn_bwd{tag}")
        da = _mm(dz2b, W["ffn_w_down"][i], tb=True, out_dtype=BF16, name=f"ffn_down_dx{tag}")
        g["ffn_w_down"] = _mm(s["a"], dz2b, ta=True, name=f"ffn_down_dw{tag}")
        dup, dgc, g["ffn_conv_b"], g["ffn_conv_w"] = _ffn_act_bwd(
            da, s["up"], W["ffn_conv_w"][i], row(W["ffn_conv_b"][i]), name=f"ffn_act_bwd{tag}")
        dup = _dwconv_bwd_data(dgc, W["ffn_conv_w"][i], FFN_CONV, dup, FFN_DIM, name=f"ffn_conv_bwd{tag}")
        g["ffn_w_up"] = _mm(s["h1b"], dup, ta=True, name=f"ffn_up_dw{tag}")
        dh1 = _mm(dup, W["ffn_w_up"][i], tb=True, add=dz2, add_scale=DEEPNORM_ALPHA, name=f"ffn_up_dx{tag}")
        dz1, dz1b, g["ln_mix_g"], g["ln_mix_b"] = _ln_bwd(dh1, s["z1"], row(W["ln_mix_g"][i]), name=f"ln_mix_bwd{tag}")
        return dz1, dz1b, g

    xb = x.astype(BF16)
    pb = p.astype(BF16)
    proj0 = _mm(xb, attn_in, name="attn_in")
    c_col = _fox_gate_fwd(proj0, bf, name="fox_gate_fwd")
    cT = (c_col[:, :ATTN_HEADS] * LOG2E).T.reshape(HEAD_PAIRS, 2, T)
    qa, qb, kk, ka, kb, vv, va, vb = _attn_prep(proj0, name="attn_prep")
    o, ob, lsea, lseb = _attn_fwd(qa, qb, kk, va, vb, cT, name="attn_fwd")
    mix0 = _mm(ob, W["attn_w_out"][0], name="attn_out")
    x1, x1b, s0 = ffn_ple_fwd(0, x, mix0, "0")

    proj1 = _mm(x1b, ssm_in, name="ssm_in")
    dt, acum = _ssd_dt_fwd(proj1, dt_bias, a_log, name="ssd_dt_fwd")
    xbc = _conv_silu_fwd(proj1, W["ssm_conv_w"][0], row(W["ssm_conv_b"][0]), name="ssd_conv_fwd")
    dtr, acr = dt[:, :SSM_HEADS].T, acum[:, :SSM_HEADS].T
    ys, states = _ssd_scan_fwd(xbc, dt, acum, dtr, acr, name="ssd_scan_fwd")
    yn = _gate_norm_fwd(ys, xbc, proj1, d_exp, norm_w, name="ssd_gate_norm_fwd")
    mix1 = _mm(yn, W["ssm_w_out"][0], name="ssm_out")
    x2, _, s1 = ffn_ple_fwd(1, x1, mix1, "1")

    lpart, dy = _loss_head(x2, target, name="loss_head")
    loss = jnp.sum(lpart)

    dz1, dz1b, g1 = ffn_ple_bwd(1, dy, s1, "1")
    G["ssm_w_out"] = _mm(yn, dz1b, ta=True, name="ssm_out_dw")[None]
    dyn = _mm(dz1b, W["ssm_w_out"][0], tb=True, name="ssm_out_dx")
    dproj1, dys, dskip, dnw, dde = _gate_norm_bwd(dyn, ys, xbc, proj1, d_exp, norm_w, name="ssd_gate_norm_bwd")
    G["ssm_norm_w"] = dnw
    G["ssm_D"] = dde.reshape(SSM_HEADS, _PH).sum(axis=1)[None]
    dxbc, dac, dar, ddc, ddr = _ssd_scan_bwd(xbc, dys, dskip, states, dt, acum, dtr, acr, name="ssd_scan_bwd")
    dproj1, dal, ddb = _ssd_dt_bwd(dac, _from_row_layout(dar), ddc, _from_row_layout(ddr), dt, proj1, dt_bias, a_log,
                                   dproj1, name="ssd_dt_bwd")
    G["ssm_A_log"] = dal[:, :SSM_HEADS]
    G["ssm_dt_bias"] = ddb[:, :SSM_HEADS]
    dpre, G["ssm_conv_b"], dcw = _conv_silu_bwd(dxbc, proj1, W["ssm_conv_w"][0], row(W["ssm_conv_b"][0]),
                                                name="ssd_conv_bwd")
    G["ssm_conv_w"] = dcw[None]
    dproj1 = _dwconv_bwd_data(dpre, W["ssm_conv_w"][0], SSM_CONV, dproj1, SSM_D_INNER, name="ssd_conv_bwd_data")
    G["ssm_w_in"] = _mm(x1b, dproj1, ta=True, name="ssm_in_dw")[None, :, :W["ssm_w_in"].shape[2]]
    dx1 = _mm(dproj1, ssm_in, tb=True, add=dz1, add_scale=DEEPNORM_ALPHA, name="ssm_in_dx")

    dz0, dz0b, g0 = ffn_ple_bwd(0, dx1, s0, "0")
    G["attn_w_out"] = _mm(ob, dz0b, ta=True, name="attn_out_dw")[None]
    do = _mm(dz0b, W["attn_w_out"][0], tb=True, name="attn_out_dx")
    doa, dob, dlta, dltb = _attn_bwd_prep(do, o, name="attn_bwd_prep")
    dq, dk, dv, dcT, dcq = _attn_bwd(qa, qb, kk, ka, kb, vv, doa, dob, lsea, lseb, dlta, dltb, cT, name="attn_bwd")
    dc_col = jnp.pad(dcT.reshape(ATTN_HEADS, T).T + dcq[:, :ATTN_HEADS], ((0, 0), (0, LANE - ATTN_HEADS)))
    dfl, dbf = _fox_gate_bwd(dc_col, proj0, bf, name="fox_gate_bwd")
    G["attn_b_f"] = dbf[:, :ATTN_HEADS]
    dproj0 = jnp.concatenate([dq.astype(BF16), dk.astype(BF16), dv.astype(BF16), dfl.astype(BF16)], axis=1)
    G["attn_w_in"] = _mm(xb, dproj0, ta=True, name="attn_in_dw")[None, :, :W["attn_w_in"].shape[2]]
    grad_x = _mm(dproj0, attn_in, tb=True, add=dz0, add_scale=DEEPNORM_ALPHA, name="attn_in_dx")

    for k in g0:
        G[k] = jnp.stack([g0[k].reshape(W[k].shape[1:]), g1[k].reshape(W[k].shape[1:])])
    return loss, grad_x, G


def kernel(x, p, attn_w_in, attn_b_f, attn_w_out, ssm_w_in, ssm_conv_w, ssm_conv_b, ssm_dt_bias, ssm_A_log, ssm_D, ssm_norm_w, ssm_w_out, ln_mix_g, ln_mix_b, ffn_w_up, ffn_conv_w, ffn_conv_b, ffn_w_down, ln_ffn_g, ln_ffn_b, ple_w_proj, ple_w_gate, ple_b_gate, loss_target, m_attn_w_in, m_attn_b_f, m_attn_w_out, m_ssm_w_in, m_ssm_conv_w, m_ssm_conv_b, m_ssm_dt_bias, m_ssm_A_log, m_ssm_D, m_ssm_norm_w, m_ssm_w_out, m_ln_mix_g, m_ln_mix_b, m_ffn_w_up, m_ffn_conv_w, m_ffn_conv_b, m_ffn_w_down, m_ln_ffn_g, m_ln_ffn_b, m_ple_w_proj, m_ple_w_gate, m_ple_b_gate, v_attn_w_in, v_attn_b_f, v_attn_w_out, v_ssm_w_in, v_ssm_conv_w, v_ssm_conv_b, v_ssm_dt_bias, v_ssm_A_log, v_ssm_D, v_ssm_norm_w, v_ssm_w_out, v_ln_mix_g, v_ln_mix_b, v_ffn_w_up, v_ffn_conv_w, v_ffn_conv_b, v_ffn_w_down, v_ln_ffn_g, v_ln_ffn_b, v_ple_w_proj, v_ple_w_gate, v_ple_b_gate):
    names = [n for n, _ in WEIGHTS]
    axes = dict(WEIGHTS)
    w_loc = dict(zip(names, [attn_w_in, attn_b_f, attn_w_out, ssm_w_in, ssm_conv_w, ssm_conv_b, ssm_dt_bias, ssm_A_log, ssm_D, ssm_norm_w, ssm_w_out, ln_mix_g, ln_mix_b, ffn_w_up, ffn_conv_w, ffn_conv_b, ffn_w_down, ln_ffn_g, ln_ffn_b, ple_w_proj, ple_w_gate, ple_b_gate]))
    m_loc = dict(zip(names, [m_attn_w_in, m_attn_b_f, m_attn_w_out, m_ssm_w_in, m_ssm_conv_w, m_ssm_conv_b, m_ssm_dt_bias, m_ssm_A_log, m_ssm_D, m_ssm_norm_w, m_ssm_w_out, m_ln_mix_g, m_ln_mix_b, m_ffn_w_up, m_ffn_conv_w, m_ffn_conv_b, m_ffn_w_down, m_ln_ffn_g, m_ln_ffn_b, m_ple_w_proj, m_ple_w_gate, m_ple_b_gate]))
    v_loc = dict(zip(names, [v_attn_w_in, v_attn_b_f, v_attn_w_out, v_ssm_w_in, v_ssm_conv_w, v_ssm_conv_b, v_ssm_dt_bias, v_ssm_A_log, v_ssm_D, v_ssm_norm_w, v_ssm_w_out, v_ln_mix_g, v_ln_mix_b, v_ffn_w_up, v_ffn_conv_w, v_ffn_conv_b, v_ffn_w_down, v_ln_ffn_g, v_ln_ffn_b, v_ple_w_proj, v_ple_w_gate, v_ple_b_gate]))
    sharded = [n for n in names if axes[n] is not None]
    matrices = [n for n in sharded if n in MATMUL_WEIGHTS]

    def wire(n):
        if n in matrices:
            return w_loc[n].astype(BF16)
        return lax.bitcast_convert_type(w_loc[n], BF16)

    wired = [wire(n) for n in sharded]
    me_chip = 2 * lax.axis_index("x") + lax.axis_index("y")
    packed = _pack(wired)
    gathered = lax.dynamic_update_index_in_dim(_gather_chips(packed, name="gather_weights"), packed, me_chip, 0)
    W = dict(w_loc)
    per_chip = [_unpack(gathered[k], [w.shape for w in wired]) for k in range(N_CHIPS)]
    for i, n in enumerate(sharded):
        pieces = [per_chip[k][i] for k in range(N_CHIPS)]
        if n not in matrices:
            pieces = [lax.bitcast_convert_type(q, F32) for q in pieces]
        W[n] = jnp.concatenate(pieces, axis=axes[n])

    loss, grad_x, G = _local_step(x[0], p[:, 0], loss_target[0], W)
    loss = lax.psum(loss, ("x", "y", "c"))

    def slot(k):
        parts = []
        for n in names:
            g = G[n].reshape(W[n].shape)
            if axes[n] is not None:
                size = w_loc[n].shape[axes[n]]
                g = lax.slice_in_dim(g, k * size, (k + 1) * size, axis=axes[n])
            parts.append(g.astype(BF16))
        return _pack(parts)

    contrib = jnp.stack([slot(k) for k in range(N_CHIPS)])
    R = contrib.shape[1]
    H = R // 2
    c = lax.axis_index("c")
    keep = lax.dynamic_slice_in_dim(contrib, c * H, H, axis=1)
    pair = _add2(keep, _pair_swap(contrib, other_half=True, name="grad_pair_swap"), BF16, name="grad_pair_sum")
    from_chips = lax.dynamic_update_index_in_dim(
        _chip_exchange(pair, name="grad_chip_exchange"), lax.dynamic_index_in_dim(pair, me_chip, 0, keepdims=False),
        me_chip, 0)
    half = _sum4(from_chips, name="grad_chip_sum")
    other = _pair_swap(half, name="grad_half_swap")
    gflat = jnp.concatenate([jnp.where(c == 0, half, other), jnp.where(c == 0, other, half)])

    grads = _unpack(gflat, [w_loc[n].shape for n in names])
    steps = [_adamw(w_loc[n], g, m_loc[n], v_loc[n], name=f"adamw_{n}") for n, g in zip(names, grads)]
    return (loss, grad_x[None], *grads, *[s[0] for s in steps], *[s[1] for s in steps], *[s[2] for s in steps])
```

```python
import functools
import math

import jax
import jax.numpy as jnp
from jax import lax
from jax.experimental import pallas as pl
from jax.experimental.pallas import tpu as pltpu

F32 = jnp.float32
BF16 = jnp.bfloat16
MESH = pl.DeviceIdType.MESH

D_MODEL = 1024
ATTN_HEADS = 16
HEAD_PAIRS = ATTN_HEADS // 2
SSM_D_INNER = 2048
SSM_HEADS = 32
SSM_GROUPS = 8
SSM_STATE = 128
SSM_CONV = 4
SSM_CHUNK = 128
SSM_XBC = SSM_D_INNER + 2 * SSM_GROUPS * SSM_STATE
FFN_DIM = 2816
FFN_CONV = 3
DEPTH = 2
LN_EPS = 1e-5
RMS_EPS = 1e-5
DEEPNORM_ALPHA = (2 * DEPTH) ** 0.25
ADAM_LR = 0.001
ADAM_B1 = 0.9
ADAM_B2 = 0.999
ADAM_EPS = 1e-08
ADAM_WD = 0.01
ADAM_STEP = 10

LANE = 128
SUBLANE = 8
HALO = SUBLANE
HALO_BF16 = 2 * SUBLANE
NEG = -1e30
ATTN_IN_PAD = 3 * D_MODEL + LANE
SSM_IN_PAD = 2 * SSM_D_INNER + 2 * SSM_GROUPS * SSM_STATE + LANE
PACK_COLS = 1024
PACK_ROW_ALIGN = 512

ATTN_BLOCK = 1024
ROW_BLOCK = 512
NARROW_ROW_BLOCK = 1024
CUM_BLOCK = 256


def _params(sem, vmem_mb=48):
    return pltpu.CompilerParams(dimension_semantics=sem, vmem_limit_bytes=vmem_mb * 2 ** 20)


def _pick(n, target, mult=LANE):
    best = None
    d = mult
    while d <= min(n, target):
        if n % d == 0:
            best = d
        d += mult
    return n if best is None else best


def _sigmoid(x):
    return 1.0 / (1.0 + jnp.exp(-x))


def _log1p(u):
    w = 1.0 + u
    return jnp.where(w == 1.0, u, jnp.log(w) * (u / (w - 1.0)))


def _softplus(x):
    return jnp.maximum(x, 0.0) + _log1p(jnp.exp(-jnp.abs(x)))


def _split3(x):
    hi = x.astype(BF16)
    r1 = x - hi.astype(F32)
    mid = r1.astype(BF16)
    lo = (r1 - mid.astype(F32)).astype(BF16)
    return hi, mid, lo


def _tri_matmul(tri, x):
    out = None
    for part in _split3(x):
        t = jnp.dot(tri, part, preferred_element_type=F32)
        out = t if out is None else out + t
    return out


def _tri(n, lower):
    r = lax.broadcasted_iota(jnp.int32, (n, n), 0)
    c = lax.broadcasted_iota(jnp.int32, (n, n), 1)
    return jnp.where((c <= r) if lower else (c >= r), 1.0, 0.0).astype(BF16)


_ANY = pl.BlockSpec(memory_space=pl.ANY)
MM_OUT_BLOCK_BYTES = 13 * 2 ** 20
MM_IN_BLOCK_BYTES = 6 * 2 ** 20
MM_VMEM_BUDGET = 44 * 2 ** 20
MM_WIDE_K = 3200


def _mm(a, b, *, name, ta=False, tb=False, add=None, add_scale=1.0, out_dtype=F32):
    if ta:
        K, M = a.shape
    else:
        M, K = a.shape
    if tb:
        N, Kb = b.shape
    else:
        Kb, N = b.shape
    assert K == Kb, (a.shape, b.shape, ta, tb)
    if ta:
        assert add is None and out_dtype == F32
        bm = _pick(M, 2816)
        bn = _pick(N, MM_OUT_BLOCK_BYTES // (4 * bm))
        bk = _pick(K, max(512, MM_IN_BLOCK_BYTES // (2 * max(bm, bn))))
    else:
        bn = _pick(N, 1536 if K <= MM_WIDE_K else 512)
        bk = K
        bm = _pick(M, 2048)
        if 4 * bm * K + 4 * K * bn + (8 if add is None else 16) * bm * bn > MM_VMEM_BUDGET:
            bm = _pick(M, 1024)
    nk = K // bk
    a_spec = pl.BlockSpec((bk, bm), lambda i, j, k: (k, i)) if ta else pl.BlockSpec((bm, bk), lambda i, j, k: (i, k))
    b_spec = pl.BlockSpec((bn, bk), lambda i, j, k: (j, k)) if tb else pl.BlockSpec((bk, bn), lambda i, j, k: (k, j))
    o_spec = pl.BlockSpec((bm, bn), lambda i, j, k: (i, j))
    dims = (((0 if ta else 1,), (1 if tb else 0,)), ((), ()))
    has_add = add is not None

    def kern(*refs):
        a_ref, b_ref = refs[0], refs[1]
        add_ref = refs[2] if has_add else None
        o_ref = refs[3] if has_add else refs[2]
        k = pl.program_id(2)
        part = lax.dot_general(a_ref[...].astype(BF16), b_ref[...].astype(BF16), dims, preferred_element_type=F32)
        if nk == 1:
            o_ref[...] = (part + add_scale * add_ref[...] if has_add else part).astype(out_dtype)
        else:
            @pl.when(k == 0)
            def _():
                o_ref[...] = part

            @pl.when(k > 0)
            def _():
                o_ref[...] += part

    ins = [a, b] + ([add] if has_add else [])
    in_specs = [a_spec, b_spec] + ([o_spec] if has_add else [])
    return pl.pallas_call(
        kern, name=name, grid=(M // bm, N // bn, nk),
        in_specs=in_specs, out_specs=o_spec,
        out_shape=jax.ShapeDtypeStruct((M, N), out_dtype),
        compiler_params=_params(("parallel", "parallel", "arbitrary"), vmem_mb=56),
    )(*ins)


def _ln_stats(z):
    mu = jnp.mean(z, axis=-1, keepdims=True)
    zc = z - mu
    var = jnp.mean(zc * zc, axis=-1, keepdims=True)
    return zc, lax.rsqrt(var + LN_EPS)


def _ln_fwd(x, r, g, b, *, name):
    T, D = x.shape
    bt = _pick(T, ROW_BLOCK, SUBLANE)

    def kern(x_ref, r_ref, g_ref, b_ref, z_ref, h_ref, hb_ref):
        z = DEEPNORM_ALPHA * x_ref[...] + r_ref[...]
        zc, rstd = _ln_stats(z)
        h = zc * rstd * g_ref[...] + b_ref[...]
        z_ref[...] = z
        h_ref[...] = h
        hb_ref[...] = h.astype(BF16)

    row = pl.BlockSpec((bt, D), lambda i: (i, 0))
    vec = pl.BlockSpec((1, D), lambda i: (0, 0))
    return pl.pallas_call(
        kern, name=name, grid=(T // bt,), in_specs=[row, row, vec, vec], out_specs=[row, row, row],
        out_shape=[jax.ShapeDtypeStruct((T, D), F32)] * 2 + [jax.ShapeDtypeStruct((T, D), BF16)],
        compiler_params=_params(("parallel",)),
    )(x, r, g, b)


def _ln_bwd(dy, z, g, *, name):
    T, D = z.shape
    bt = _pick(T, ROW_BLOCK, SUBLANE)

    def kern(dy_ref, z_ref, g_ref, dz_ref, dzb_ref, dg_ref, db_ref):
        i = pl.program_id(0)
        zc, rstd = _ln_stats(z_ref[...])
        xhat = zc * rstd
        dyv = dy_ref[...]
        dxh = dyv * g_ref[...]
        m1 = jnp.mean(dxh, axis=-1, keepdims=True)
        m2 = jnp.mean(dxh * xhat, axis=-1, keepdims=True)
        dz = rstd * (dxh - m1 - xhat * m2)
        dz_ref[...] = dz
        dzb_ref[...] = dz.astype(BF16)

        @pl.when(i == 0)
        def _():
            dg_ref[...] = jnp.zeros_like(dg_ref)
            db_ref[...] = jnp.zeros_like(db_ref)

        dg_ref[...] += jnp.sum(dyv * xhat, axis=0, keepdims=True)
        db_ref[...] += jnp.sum(dyv, axis=0, keepdims=True)

    row = pl.BlockSpec((bt, D), lambda i: (i, 0))
    vec = pl.BlockSpec((1, D), lambda i: (0, 0))
    return pl.pallas_call(
        kern, name=name, grid=(T // bt,), in_specs=[row, row, vec], out_specs=[row, row, vec, vec],
        out_shape=[jax.ShapeDtypeStruct((T, D), F32), jax.ShapeDtypeStruct((T, D), BF16),
                   jax.ShapeDtypeStruct((1, D), F32), jax.ShapeDtypeStruct((1, D), F32)],
        compiler_params=_params(("arbitrary",)),
    )(dy, z, g)


def _past_taps(ext_ref, K, bt):
    ext = ext_ref[...]
    return [(ext if k == K - 1 else pltpu.roll(ext, K - 1 - k, 0))[HALO:HALO + bt] for k in range(K)]


def _conv_past(taps, cw_ref):
    out = None
    for k, tap in enumerate(taps):
        term = cw_ref[k:k + 1, :] * tap
        out = term if out is None else out + term
    return out


def _fill_ext_past(ext_ref, halo_ref, cur, i, bt):
    ext_ref[pl.ds(0, HALO), :] = jnp.where(i > 0, halo_ref[...], 0.0)
    ext_ref[pl.ds(HALO, bt), :] = cur


def _halo_prev(bt, bc, off):
    return pl.BlockSpec((HALO, bc), lambda i, j: (jnp.maximum(i * (bt // HALO) - 1, 0), j + off))


def _normal_cdf(x):
    return 0.5 * (1.0 + lax.erf(x * (1.0 / math.sqrt(2.0))))


def _gelu(x):
    return x * _normal_cdf(x)


def _gelu_and_grad(x):
    cdf = _normal_cdf(x)
    return x * cdf, cdf + x * jnp.exp(-0.5 * x * x) * (1.0 / math.sqrt(2.0 * math.pi))


def _ffn_act_fwd(up, cw, cb, *, name):
    T, F2 = up.shape
    F = F2 // 2
    bt = _pick(T, ROW_BLOCK, SUBLANE)
    bc = _pick(F, 1408)
    nb = F // bc

    def kern(u_ref, g_ref, halo_ref, cw_ref, cb_ref, a_ref, ext_ref):
        i = pl.program_id(0)
        _fill_ext_past(ext_ref, halo_ref, g_ref[...], i, bt)
        gc = cb_ref[...] + _conv_past(_past_taps(ext_ref, FFN_CONV, bt), cw_ref)
        a_ref[...] = (_gelu(gc) * u_ref[...]).astype(BF16)

    return pl.pallas_call(
        kern, name=name, grid=(T // bt, nb),
        in_specs=[pl.BlockSpec((bt, bc), lambda i, j: (i, j)),
                  pl.BlockSpec((bt, bc), lambda i, j: (i, j + nb)),
                  _halo_prev(bt, bc, nb),
                  pl.BlockSpec((FFN_CONV, bc), lambda i, j: (0, j)),
                  pl.BlockSpec((1, bc), lambda i, j: (0, j))],
        out_specs=pl.BlockSpec((bt, bc), lambda i, j: (i, j)),
        out_shape=jax.ShapeDtypeStruct((T, F), BF16),
        scratch_shapes=[pltpu.VMEM((bt + HALO, bc), F32)],
        compiler_params=_params(("parallel", "parallel")),
    )(up, up, up, cw, cb)


def _ffn_act_bwd(da, up, cw, cb, *, name):
    T, F2 = up.shape
    F = F2 // 2
    bt = _pick(T, ROW_BLOCK, SUBLANE)
    bc = _pick(F, 1408)
    nb = F // bc
    K = FFN_CONV

    def kern(da_ref, u_ref, g_ref, halo_ref, cw_ref, cb_ref, du_ref, dgc_ref, dcb_ref, dcw_ref, ext_ref):
        i = pl.program_id(1)
        _fill_ext_past(ext_ref, halo_ref, g_ref[...], i, bt)
        taps = _past_taps(ext_ref, K, bt)
        gc = cb_ref[...] + _conv_past(taps, cw_ref)
        dav = da_ref[...]
        act, act_grad = _gelu_and_grad(gc)
        du_ref[...] = (dav * act).astype(BF16)
        dgc = dav * u_ref[...] * act_grad
        dgc_ref[...] = dgc.astype(BF16)

        @pl.when(i == 0)
        def _():
            dcb_ref[...] = jnp.zeros_like(dcb_ref)
            dcw_ref[...] = jnp.zeros_like(dcw_ref)

        dcb_ref[...] += jnp.sum(dgc, axis=0, keepdims=True)
        for k in range(K):
            dcw_ref[k:k + 1, :] += jnp.sum(dgc * taps[k], axis=0, keepdims=True)

    blk = pl.BlockSpec((bt, bc), lambda j, i: (i, j))
    return pl.pallas_call(
        kern, name=name, grid=(nb, T // bt),
        in_specs=[blk, blk,
                  pl.BlockSpec((bt, bc), lambda j, i: (i, j + nb)),
                  pl.BlockSpec((HALO, bc), lambda j, i: (jnp.maximum(i * (bt // HALO) - 1, 0), j + nb)),
                  pl.BlockSpec((K, bc), lambda j, i: (0, j)),
                  pl.BlockSpec((1, bc), lambda j, i: (0, j))],
        out_specs=[blk, blk, pl.BlockSpec((1, bc), lambda j, i: (0, j)), pl.BlockSpec((K, bc), lambda j, i: (0, j))],
        out_shape=[jax.ShapeDtypeStruct((T, F2), BF16), jax.ShapeDtypeStruct((T, F), BF16),
                   jax.ShapeDtypeStruct((1, F), F32), jax.ShapeDtypeStruct((K, F), F32)],
        scratch_shapes=[pltpu.VMEM((bt + HALO, bc), F32)],
        compiler_params=_params(("parallel", "arbitrary")),
    )(da, up, up, up, cw, cb)


def _dwconv_bwd_data(dgc, cw, K, into, col, *, name):
    T, C = dgc.shape
    bt = _pick(T, ROW_BLOCK, SUBLANE)
    bc = _pick(C, 1408)
    nt = T // bt
    halo = HALO_BF16 if dgc.dtype == BF16 else HALO
    last_halo = T // halo - 1
    off = col // bc
    assert off * bc == col

    def kern(d_ref, halo_ref, cw_ref, into_ref, o_ref, ext_ref):
        i = pl.program_id(0)
        ext_ref[pl.ds(0, bt), :] = d_ref[...].astype(F32)
        ext_ref[pl.ds(bt, halo), :] = jnp.where(i < nt - 1, halo_ref[...].astype(F32), 0.0)
        ext = ext_ref[...]
        out = None
        for k in range(K):
            ahead = K - 1 - k
            tap = (ext if ahead == 0 else pltpu.roll(ext, bt + halo - ahead, 0))[0:bt]
            term = cw_ref[k:k + 1, :] * tap
            out = term if out is None else out + term
        o_ref[...] = out.astype(o_ref.dtype)

    return pl.pallas_call(
        kern, name=name, grid=(nt, C // bc),
        in_specs=[pl.BlockSpec((bt, bc), lambda i, j: (i, j)),
                  pl.BlockSpec((halo, bc), lambda i, j: (jnp.minimum((i + 1) * (bt // halo), last_halo), j)),
                  pl.BlockSpec((K, bc), lambda i, j: (0, j)), _ANY],
        out_specs=pl.BlockSpec((bt, bc), lambda i, j: (i, j + off)),
        out_shape=jax.ShapeDtypeStruct(into.shape, into.dtype), input_output_aliases={3: 0},
        scratch_shapes=[pltpu.VMEM((bt + halo, bc), F32)],
        compiler_params=_params(("parallel", "parallel")),
    )(dgc, dgc, cw, into)


def _ple_fwd(h, G, bg, pp, *, name):
    T, D = h.shape
    bt = _pick(T, ROW_BLOCK, SUBLANE)

    def kern(h_ref, G_ref, bg_ref, pp_ref, o_ref, ob_ref):
        out = h_ref[...] + _sigmoid(G_ref[...] + bg_ref[...]) * pp_ref[...]
        o_ref[...] = out
        ob_ref[...] = out.astype(BF16)

    row = pl.BlockSpec((bt, D), lambda i: (i, 0))
    vec = pl.BlockSpec((1, D), lambda i: (0, 0))
    return pl.pallas_call(
        kern, name=name, grid=(T // bt,), in_specs=[row, row, vec, row], out_specs=[row, row],
        out_shape=[jax.ShapeDtypeStruct((T, D), F32), jax.ShapeDtypeStruct((T, D), BF16)],
        compiler_params=_params(("parallel",)),
    )(h, G, bg, pp)


def _ple_bwd(dx, G, bg, pp, *, name):
    T, D = dx.shape
    bt = _pick(T, ROW_BLOCK, SUBLANE)

    def kern(dx_ref, G_ref, bg_ref, pp_ref, dG_ref, dpp_ref, dbg_ref):
        i = pl.program_id(0)
        gate = _sigmoid(G_ref[...] + bg_ref[...])
        dxv = dx_ref[...]
        dG = dxv * pp_ref[...] * gate * (1.0 - gate)
        dG_ref[...] = dG.astype(BF16)
        dpp_ref[...] = (dxv * gate).astype(BF16)

        @pl.when(i == 0)
        def _():
            dbg_ref[...] = jnp.zeros_like(dbg_ref)

        dbg_ref[...] += jnp.sum(dG, axis=0, keepdims=True)

    row = pl.BlockSpec((bt, D), lambda i: (i, 0))
    vec = pl.BlockSpec((1, D), lambda i: (0, 0))
    return pl.pallas_call(
        kern, name=name, grid=(T // bt,), in_specs=[row, row, vec, row], out_specs=[row, row, vec],
        out_shape=[jax.ShapeDtypeStruct((T, D), BF16), jax.ShapeDtypeStruct((T, D), BF16),
                   jax.ShapeDtypeStruct((1, D), F32)],
        compiler_params=_params(("arbitrary",)),
    )(dx, G, bg, pp)


def _fox_gate_fwd(proj, bf, *, name):
    T = proj.shape[0]
    bt = _pick(T, CUM_BLOCK, SUBLANE)
    fcol = 3 * D_MODEL // LANE

    def kern(f_ref, bf_ref, c_ref, carry_ref):
        i = pl.program_id(0)

        @pl.when(i == 0)
        def _():
            carry_ref[...] = jnp.zeros_like(carry_ref)

        x = f_ref[...] + bf_ref[...]
        lf = jnp.minimum(x, 0.0) - _log1p(jnp.exp(-jnp.abs(x)))
        cs = _tri_matmul(_tri(bt, True), lf) + carry_ref[...]
        c_ref[...] = cs
        carry_ref[...] = cs[bt - 1:bt, :]

    return pl.pallas_call(
        kern, name=name, grid=(T // bt,),
        in_specs=[pl.BlockSpec((bt, LANE), lambda i: (i, fcol)), pl.BlockSpec((1, LANE), lambda i: (0, 0))],
        out_specs=pl.BlockSpec((bt, LANE), lambda i: (i, 0)),
        out_shape=jax.ShapeDtypeStruct((T, LANE), F32),
        scratch_shapes=[pltpu.VMEM((1, LANE), F32)],
        compiler_params=_params(("arbitrary",)),
    )(proj, bf)


def _fox_gate_bwd(dc, proj, bf, dq, dk, dv, *, name):
    T = proj.shape[0]
    bt = _pick(T, CUM_BLOCK, SUBLANE)
    nb = T // bt
    fcol = 3 * D_MODEL // LANE

    def kern(dc_ref, f_ref, bf_ref, dq_ref, dk_ref, dv_ref, dproj_ref, dbf_ref, carry_ref):
        i = pl.program_id(0)

        @pl.when(i == 0)
        def _():
            carry_ref[...] = jnp.zeros_like(carry_ref)
            dbf_ref[...] = jnp.zeros_like(dbf_ref)

        dlf = _tri_matmul(_tri(bt, False), dc_ref[...]) + carry_ref[...]
        carry_ref[...] = dlf[0:1, :]
        x = f_ref[...] + bf_ref[...]
        lane = lax.broadcasted_iota(jnp.int32, (bt, LANE), 1)
        df = jnp.where(lane < ATTN_HEADS, dlf / (1.0 + jnp.exp(x)), 0.0)
        dbf_ref[...] += jnp.sum(df, axis=0, keepdims=True)
        for n, part_ref in enumerate((dq_ref, dk_ref, dv_ref)):
            dproj_ref[:, n * D_MODEL:(n + 1) * D_MODEL] = part_ref[...].astype(BF16)
        dproj_ref[:, 3 * D_MODEL:] = df.astype(BF16)

    rows = lambda i: (nb - 1 - i, 0)
    wide = pl.BlockSpec((bt, D_MODEL), rows)
    return pl.pallas_call(
        kern, name=name, grid=(nb,),
        in_specs=[pl.BlockSpec((bt, LANE), rows), pl.BlockSpec((bt, LANE), lambda i: (nb - 1 - i, fcol)),
                  pl.BlockSpec((1, LANE), lambda i: (0, 0)), wide, wide, wide],
        out_specs=[pl.BlockSpec((bt, ATTN_IN_PAD), rows), pl.BlockSpec((1, LANE), lambda i: (0, 0))],
        out_shape=[jax.ShapeDtypeStruct((T, ATTN_IN_PAD), BF16), jax.ShapeDtypeStruct((1, LANE), F32)],
        scratch_shapes=[pltpu.VMEM((1, LANE), F32)],
        compiler_params=_params(("arbitrary",)),
    )(dc, proj, bf, dq, dk, dv)


_NT = (((1,), (1,)), ((), ()))
_TN = (((0,), (0,)), ((), ()))


def _dot(a, b, dims=None):
    if dims is None:
        return jnp.dot(a, b, preferred_element_type=F32)
    return lax.dot_general(a, b, dims, preferred_element_type=F32)


LOG2E = 1.0 / math.log(2.0)
LN2 = math.log(2.0)
Q_SCALE = 0.125 * LOG2E
HALF = LANE // 2
L_LANE = (HALF, 0)
FWD_PAIRS = 4
BWD_PAIRS = 1


def _attn_prep(proj, *, name):
    T = proj.shape[0]
    bt = _pick(T, ATTN_BLOCK)

    def kern(q_ref, k_ref, v_ref, qa_ref, qb_ref, kk_ref, ka_ref, kb_ref, vv_ref, va_ref, vb_ref):
        lane = lax.broadcasted_iota(jnp.int32, (bt, LANE), 1)
        lo = lane < HALF
        q = q_ref[...] * Q_SCALE
        k = k_ref[...]
        v = v_ref[...]
        qa_ref[...] = jnp.where(lo, q, 0.0).astype(BF16)
        qb_ref[...] = jnp.where(lo, 0.0, q).astype(BF16)
        kk_ref[...] = k.astype(BF16)
        ka_ref[...] = jnp.where(lo, k, 0.0).astype(BF16)
        kb_ref[...] = jnp.where(lo, 0.0, k).astype(BF16)
        vv_ref[...] = v.astype(BF16)
        va_ref[...] = jnp.where(lo, v, jnp.where(lane == L_LANE[0], 1.0, 0.0)).astype(BF16)
        vb_ref[...] = jnp.where(lo, jnp.where(lane == L_LANE[1], 1.0, 0.0), v).astype(BF16)

    kcol, vcol = D_MODEL // LANE, 2 * D_MODEL // LANE
    out = pl.BlockSpec((bt, LANE), lambda i, hp: (i, hp))
    return pl.pallas_call(
        kern, name=name, grid=(T // bt, HEAD_PAIRS),
        in_specs=[out, pl.BlockSpec((bt, LANE), lambda i, hp: (i, kcol + hp)),
                  pl.BlockSpec((bt, LANE), lambda i, hp: (i, vcol + hp))],
        out_specs=[out] * 8, out_shape=[jax.ShapeDtypeStruct((T, D_MODEL), BF16)] * 8,
        compiler_params=_params(("parallel", "parallel")),
    )(proj, proj, proj)


def _attn_fwd(qa, qb, kk, va, vb, cT, *, name):
    T = qa.shape[0]
    tb = _pick(T, ATTN_BLOCK)
    nq = T // tb
    rep = tb // LANE
    width = FWD_PAIRS * LANE

    def kern(qa_ref, qb_ref, k_ref, va_ref, vb_ref, c_ref, o_ref, ob_ref, lsea_ref, lseb_ref, m_ref, acc_ref):
        qi = pl.program_id(1)
        ki = pl.program_id(2)

        @pl.when(ki == 0)
        def _():
            m_ref[...] = jnp.full_like(m_ref, NEG)
            acc_ref[...] = jnp.zeros_like(acc_ref)

        def step(diag):
            for pp in range(FWD_PAIRS):
                cols = slice(pp * LANE, (pp + 1) * LANE)
                k = k_ref[:, cols]
                for h, (q_ref, v_ref) in enumerate(((qa_ref, va_ref), (qb_ref, vb_ref))):
                    i = 2 * pp + h
                    s = _dot(q_ref[:, cols], k, _NT) - c_ref[pp, h:h + 1, :]
                    if diag:
                        r = lax.broadcasted_iota(jnp.int32, (tb, tb), 0)
                        c = lax.broadcasted_iota(jnp.int32, (tb, tb), 1)
                        s = jnp.where(c <= r, s, NEG)
                    m_prev = m_ref[i]
                    m_new = jnp.maximum(m_prev, jnp.max(s, axis=1, keepdims=True))
                    p = jnp.exp2(s - jnp.tile(m_new, (1, rep)))
                    acc_ref[i] = acc_ref[i] * jnp.exp2(m_prev - m_new) + _dot(p.astype(BF16), v_ref[:, cols])
                    m_ref[i] = m_new

        @pl.when(ki < qi)
        def _():
            step(False)

        @pl.when(ki == qi)
        def _():
            step(True)
            lo = lax.broadcasted_iota(jnp.int32, (tb, LANE), 1) < HALF
            for pp in range(FWD_PAIRS):
                cols = slice(pp * LANE, (pp + 1) * LANE)
                a0, a1 = acc_ref[2 * pp], acc_ref[2 * pp + 1]
                l0 = a0[:, L_LANE[0]:L_LANE[0] + 1]
                l1 = a1[:, L_LANE[1]:L_LANE[1] + 1]
                o = jnp.where(lo, a0 / l0, a1 / l1)
                o_ref[:, cols] = o
                ob_ref[:, cols] = o.astype(BF16)
                lsea_ref[:, cols] = m_ref[2 * pp] + jnp.log(l0) * LOG2E
                lseb_ref[:, cols] = m_ref[2 * pp + 1] + jnp.log(l1) * LOG2E

    qspec = pl.BlockSpec((tb, width), lambda g, qi, ki: (qi, g))
    kspec = pl.BlockSpec((tb, width), lambda g, qi, ki: (jnp.minimum(ki, qi), g))
    return pl.pallas_call(
        kern, name=name, grid=(HEAD_PAIRS // FWD_PAIRS, nq, nq),
        in_specs=[qspec, qspec, kspec, kspec, kspec,
                  pl.BlockSpec((FWD_PAIRS, 2, tb), lambda g, qi, ki: (g, 0, jnp.minimum(ki, qi)))],
        out_specs=[qspec, qspec, qspec, qspec],
        out_shape=[jax.ShapeDtypeStruct((T, D_MODEL), F32), jax.ShapeDtypeStruct((T, D_MODEL), BF16),
                   jax.ShapeDtypeStruct((T, D_MODEL), F32), jax.ShapeDtypeStruct((T, D_MODEL), F32)],
        scratch_shapes=[pltpu.VMEM((2 * FWD_PAIRS, tb, LANE), F32), pltpu.VMEM((2 * FWD_PAIRS, tb, LANE), F32)],
        compiler_params=_params(("parallel", "parallel", "arbitrary")),
    )(qa, qb, kk, va, vb, cT)


def _attn_bwd_prep(do, o, *, name):
    T, D = do.shape
    bt = _pick(T, ATTN_BLOCK)

    def kern(do_ref, o_ref, doa_ref, dob_ref, dlta_ref, dltb_ref):
        lo = lax.broadcasted_iota(jnp.int32, (bt, LANE), 1) < HALF
        dov = do_ref[...]
        prod = dov * o_ref[...]
        doa_ref[...] = jnp.where(lo, dov, 0.0).astype(BF16)
        dob_ref[...] = jnp.where(lo, 0.0, dov).astype(BF16)
        dlta_ref[...] = jnp.broadcast_to(jnp.sum(jnp.where(lo, prod, 0.0), axis=1, keepdims=True), (bt, LANE))
        dltb_ref[...] = jnp.broadcast_to(jnp.sum(jnp.where(lo, 0.0, prod), axis=1, keepdims=True), (bt, LANE))

    blk = pl.BlockSpec((bt, LANE), lambda i, hp: (i, hp))
    return pl.pallas_call(
        kern, name=name, grid=(T // bt, HEAD_PAIRS), in_specs=[blk, blk], out_specs=[blk] * 4,
        out_shape=[jax.ShapeDtypeStruct((T, D), BF16)] * 2 + [jax.ShapeDtypeStruct((T, D), F32)] * 2,
        compiler_params=_params(("parallel", "parallel")),
    )(do, o)


def _attn_bwd(qa, qb, kk, ka, kb, vv, doa, dob, lsea, lseb, dlta, dltb, cT, *, name):
    T = qa.shape[0]
    tb = _pick(T, ATTN_BLOCK)
    nq = T // tb
    rep = tb // LANE
    width = BWD_PAIRS * LANE

    def kern(qa_ref, qb_ref, k_ref, ka_ref, kb_ref, v_ref, doa_ref, dob_ref, lsea_ref, lseb_ref, dlta_ref, dltb_ref,
             c_ref, dq_ref, dk_ref, dv_ref, dc_ref, dcq_ref):
        g = pl.program_id(0)
        ki = pl.program_id(1)
        qi = pl.program_id(2)
        first = jnp.logical_and(ki == 0, qi == 0)

        @pl.when(first)
        def _():
            dq_ref[...] = jnp.zeros_like(dq_ref)

        @pl.when(jnp.logical_and(first, g == 0))
        def _():
            dcq_ref[...] = jnp.zeros_like(dcq_ref)

        @pl.when(qi == 0)
        def _():
            dk_ref[...] = jnp.zeros_like(dk_ref)
            dv_ref[...] = jnp.zeros_like(dv_ref)
            dc_ref[...] = jnp.zeros_like(dc_ref)

        def step(diag):
            rows = pl.ds(pl.multiple_of(qi * tb, tb), tb)
            lane = lax.broadcasted_iota(jnp.int32, (tb, LANE), 1)
            row_sums = jnp.zeros((tb, LANE), F32)
            for pp in range(BWD_PAIRS):
                cols = slice(pp * LANE, (pp + 1) * LANE)
                k = k_ref[:, cols]
                v = v_ref[:, cols]
                dq = None
                dk = None
                dv = None
                heads = ((qa_ref, ka_ref, doa_ref, lsea_ref, dlta_ref), (qb_ref, kb_ref, dob_ref, lseb_ref, dltb_ref))
                for h, (q_ref, km_ref, do_ref, lse_ref, dlt_ref) in enumerate(heads):
                    q = q_ref[:, cols]
                    dom = do_ref[:, cols]
                    s = _dot(q, k, _NT) - c_ref[pp, h:h + 1, :]
                    if diag:
                        r = lax.broadcasted_iota(jnp.int32, (tb, tb), 0)
                        c = lax.broadcasted_iota(jnp.int32, (tb, tb), 1)
                        s = jnp.where(c <= r, s, NEG)
                    p = jnp.exp2(s - jnp.tile(lse_ref[:, cols], (1, rep)))
                    ds = p * (_dot(dom, v, _NT) - jnp.tile(dlt_ref[:, cols], (1, rep)))
                    dc_ref[pp, h:h + 1, :] -= jnp.sum(ds, axis=0, keepdims=True)
                    head = 2 * (BWD_PAIRS * g + pp) + h
                    row_sums = jnp.where(lane == head, jnp.sum(ds, axis=1, keepdims=True), row_sums)
                    dsb = ds.astype(BF16)
                    tv = _dot(p.astype(BF16), dom, _TN)
                    tk = _dot(dsb, q, _TN)
                    tq = _dot(dsb, km_ref[:, cols])
                    dv = tv if dv is None else dv + tv
                    dk = tk if dk is None else dk + tk
                    dq = tq if dq is None else dq + tq
                dv_ref[:, cols] += dv
                dk_ref[:, cols] += dk * LN2
                dq_ref[rows, cols] += dq * 0.125
            dcq_ref[rows, :] += row_sums

        @pl.when(qi > ki)
        def _():
            step(False)

        @pl.when(qi == ki)
        def _():
            step(True)

    qspec = pl.BlockSpec((tb, width), lambda g, ki, qi: (jnp.maximum(qi, ki), g))
    kspec = pl.BlockSpec((tb, width), lambda g, ki, qi: (ki, g))
    cspec = pl.BlockSpec((BWD_PAIRS, 2, tb), lambda g, ki, qi: (g, 0, ki))
    qacc = pl.BlockSpec((T, width), lambda g, ki, qi: (0, g), pipeline_mode=pl.Buffered(1))
    cqacc = pl.BlockSpec((T, LANE), lambda g, ki, qi: (0, 0), pipeline_mode=pl.Buffered(1))
    return pl.pallas_call(
        kern, name=name, grid=(HEAD_PAIRS // BWD_PAIRS, nq, nq),
        in_specs=[qspec, qspec, kspec, kspec, kspec, kspec, qspec, qspec, qspec, qspec, qspec, qspec, cspec],
        out_specs=[qacc, kspec, kspec, cspec, cqacc],
        out_shape=[jax.ShapeDtypeStruct((T, D_MODEL), F32)] * 3 + [jax.ShapeDtypeStruct((HEAD_PAIRS, 2, T), F32),
                                                                   jax.ShapeDtypeStruct((T, LANE), F32)],
        compiler_params=_params(("arbitrary", "arbitrary", "arbitrary"), vmem_mb=56),
    )(qa, qb, kk, ka, kb, vv, doa, dob, lsea, lseb, dlta, dltb, cT)


def _ssd_dt_fwd(proj, dt_bias, a_log, *, name):
    T = proj.shape[0]
    Q = SSM_CHUNK
    col = (2 * SSM_D_INNER + 2 * SSM_GROUPS * SSM_STATE) // LANE

    def kern(raw_ref, b_ref, al_ref, dt_ref, ac_ref):
        dt = _softplus(raw_ref[...] + b_ref[...])
        dt_ref[...] = dt
        ac_ref[...] = _tri_matmul(_tri(Q, True), dt * (-jnp.exp(al_ref[...])))

    vec = pl.BlockSpec((1, LANE), lambda i: (0, 0))
    blk = pl.BlockSpec((Q, LANE), lambda i: (i, 0))
    return pl.pallas_call(
        kern, name=name, grid=(T // Q,),
        in_specs=[pl.BlockSpec((Q, LANE), lambda i: (i, col)), vec, vec], out_specs=[blk, blk],
        out_shape=[jax.ShapeDtypeStruct((T, LANE), F32)] * 2,
        compiler_params=_params(("parallel",)),
    )(proj, dt_bias, a_log)


def _ssd_dt_bwd(da_a, da_b, ddt_a, ddt_b, dt, proj, dt_bias, a_log, into, *, name):
    T = proj.shape[0]
    Q = SSM_CHUNK
    col = (2 * SSM_D_INNER + 2 * SSM_GROUPS * SSM_STATE) // LANE

    def kern(daa_ref, dab_ref, dda_ref, ddb_ref, dt_ref, raw_ref, b_ref, al_ref, into_ref, draw_ref, dal_ref, db_ref,
             acc_ref):
        i = pl.program_id(0)

        @pl.when(i == 0)
        def _():
            acc_ref[...] = jnp.zeros_like(acc_ref)
            db_ref[...] = jnp.zeros_like(db_ref)

        A = -jnp.exp(al_ref[...])
        ddA = _tri_matmul(_tri(Q, False), daa_ref[...] + dab_ref[...])
        ddt = dda_ref[...] + ddb_ref[...] + ddA * A
        acc_ref[...] += jnp.sum(ddA * dt_ref[...], axis=0, keepdims=True)
        lane = lax.broadcasted_iota(jnp.int32, (Q, LANE), 1)
        draw = jnp.where(lane < SSM_HEADS, ddt * _sigmoid(raw_ref[...] + b_ref[...]), 0.0)
        draw_ref[...] = draw.astype(BF16)
        db_ref[...] += jnp.sum(draw, axis=0, keepdims=True)
        dal_ref[...] = acc_ref[...] * A

    vec = pl.BlockSpec((1, LANE), lambda i: (0, 0))
    blk = pl.BlockSpec((Q, LANE), lambda i: (i, 0))
    return pl.pallas_call(
        kern, name=name, grid=(T // Q,),
        in_specs=[blk, blk, blk, blk, blk, pl.BlockSpec((Q, LANE), lambda i: (i, col)), vec, vec, _ANY],
        out_specs=[pl.BlockSpec((Q, LANE), lambda i: (i, col)), vec, vec],
        out_shape=[jax.ShapeDtypeStruct(into.shape, into.dtype), jax.ShapeDtypeStruct((1, LANE), F32),
                   jax.ShapeDtypeStruct((1, LANE), F32)],
        input_output_aliases={8: 0},
        scratch_shapes=[pltpu.VMEM((1, LANE), F32)],
        compiler_params=_params(("arbitrary",)),
    )(da_a, da_b, ddt_a, ddt_b, dt, proj, dt_bias, a_log, into)


def _conv_silu_fwd(proj, cw, cb, *, name):
    T = proj.shape[0]
    C = SSM_XBC
    bt = _pick(T, ROW_BLOCK, SUBLANE)
    bc = 1024
    off = SSM_D_INNER // bc

    def kern(x_ref, halo_ref, cw_ref, cb_ref, o_ref, ext_ref):
        i = pl.program_id(0)
        _fill_ext_past(ext_ref, halo_ref, x_ref[...], i, bt)
        pre = cb_ref[...] + _conv_past(_past_taps(ext_ref, SSM_CONV, bt), cw_ref)
        o_ref[...] = pre * _sigmoid(pre)

    return pl.pallas_call(
        kern, name=name, grid=(T // bt, C // bc),
        in_specs=[pl.BlockSpec((bt, bc), lambda i, j: (i, j + off)), _halo_prev(bt, bc, off),
                  pl.BlockSpec((SSM_CONV, bc), lambda i, j: (0, j)), pl.BlockSpec((1, bc), lambda i, j: (0, j))],
        out_specs=pl.BlockSpec((bt, bc), lambda i, j: (i, j)),
        out_shape=jax.ShapeDtypeStruct((T, C), F32),
        scratch_shapes=[pltpu.VMEM((bt + HALO, bc), F32)],
        compiler_params=_params(("parallel", "parallel")),
    )(proj, proj, cw, cb)


def _conv_silu_bwd(dxbc, proj, cw, cb, *, name):
    T = proj.shape[0]
    C = SSM_XBC
    K = SSM_CONV
    bt = _pick(T, ROW_BLOCK, SUBLANE)
    bc = 1024
    off = SSM_D_INNER // bc

    def kern(d_ref, x_ref, halo_ref, cw_ref, cb_ref, dpre_ref, dcb_ref, dcw_ref, ext_ref):
        i = pl.program_id(1)
        _fill_ext_past(ext_ref, halo_ref, x_ref[...], i, bt)
        taps = _past_taps(ext_ref, K, bt)
        pre = cb_ref[...] + _conv_past(taps, cw_ref)
        sg = _sigmoid(pre)
        dpre = d_ref[...] * sg * (1.0 + pre * (1.0 - sg))
        dpre_ref[...] = dpre.astype(BF16)

        @pl.when(i == 0)
        def _():
            dcb_ref[...] = jnp.zeros_like(dcb_ref)
            dcw_ref[...] = jnp.zeros_like(dcw_ref)

        dcb_ref[...] += jnp.sum(dpre, axis=0, keepdims=True)
        for k in range(K):
            dcw_ref[k:k + 1, :] += jnp.sum(dpre * taps[k], axis=0, keepdims=True)

    blk = pl.BlockSpec((bt, bc), lambda j, i: (i, j))
    return pl.pallas_call(
        kern, name=name, grid=(C // bc, T // bt),
        in_specs=[blk, pl.BlockSpec((bt, bc), lambda j, i: (i, j + off)),
                  pl.BlockSpec((HALO, bc), lambda j, i: (jnp.maximum(i * (bt // HALO) - 1, 0), j + off)),
                  pl.BlockSpec((K, bc), lambda j, i: (0, j)), pl.BlockSpec((1, bc), lambda j, i: (0, j))],
        out_specs=[blk, pl.BlockSpec((1, bc), lambda j, i: (0, j)), pl.BlockSpec((K, bc), lambda j, i: (0, j))],
        out_shape=[jax.ShapeDtypeStruct((T, C), BF16), jax.ShapeDtypeStruct((1, C), F32),
                   jax.ShapeDtypeStruct((K, C), F32)],
        scratch_shapes=[pltpu.VMEM((bt + HALO, bc), F32)],
        compiler_params=_params(("parallel", "arbitrary")),
    )(dxbc, proj, proj, cw, cb)


_GP = SSM_D_INNER // SSM_GROUPS
_HPG = SSM_HEADS // SSM_GROUPS
_PH = SSM_D_INNER // SSM_HEADS


def _head_masks(rows):
    lane = lax.broadcasted_iota(jnp.int32, (rows, _GP), 1)
    return [jnp.logical_and(lane >= r * _PH, lane < (r + 1) * _PH) for r in range(_HPG)]


def _ssd_cols(g):
    x0 = g * _GP
    b0 = SSM_D_INNER + g * SSM_STATE
    c0 = SSM_D_INNER + (SSM_GROUPS + g) * SSM_STATE
    return slice(x0, x0 + _GP), slice(b0, b0 + SSM_STATE), slice(c0, c0 + SSM_STATE)


def _ssd_specs(idx):
    Q, N = SSM_CHUNK, SSM_STATE
    return dict(
        xbc=pl.BlockSpec((Q, SSM_XBC), lambda j: (idx(j), 0)),
        x=pl.BlockSpec((Q, SSM_D_INNER), lambda j: (idx(j), 0)),
        col=pl.BlockSpec((Q, LANE), lambda j: (idx(j), 0)),
        row=pl.BlockSpec((SSM_HEADS, Q), lambda j: (0, idx(j))),
        st=pl.BlockSpec((N, SSM_D_INNER), lambda j: (idx(j), 0)),
    )


def _ssd_scan_fwd(xbc, dtc, acc_, dtr, acr, *, name):
    T = xbc.shape[0]
    Q, N = SSM_CHUNK, SSM_STATE
    nc = T // Q
    sp = _ssd_specs(lambda j: j)

    def kern(xbc_ref, dtc_ref, ac_ref, dtr_ref, ar_ref, ys_ref, st_ref, state_ref):
        @pl.when(pl.program_id(0) == 0)
        def _():
            state_ref[...] = jnp.zeros_like(state_ref)

        r_i = lax.broadcasted_iota(jnp.int32, (Q, Q), 0)
        c_i = lax.broadcasted_iota(jnp.int32, (Q, Q), 1)
        tri = c_i <= r_i
        masks = _head_masks(Q)
        masks1 = _head_masks(1)
        for g in range(SSM_GROUPS):
            xs, bs, cs = _ssd_cols(g)
            S = state_ref[g]
            st_ref[:, xs] = S
            x = xbc_ref[:, xs]
            xb = x.astype(BF16)
            Bb = xbc_ref[:, bs].astype(BF16)
            Cb = xbc_ref[:, cs].astype(BF16)
            CB = _dot(Cb, Bb, _NT)
            y = jnp.zeros((Q, _GP), F32)
            El = jnp.zeros((Q, _GP), F32)
            Wl = jnp.zeros((Q, _GP), F32)
            decl = jnp.zeros((1, _GP), F32)
            for r in range(_HPG):
                h = g * _HPG + r
                a_c = ac_ref[:, h:h + 1]
                a_r = ar_ref[h:h + 1, :]
                dt_c = dtc_ref[:, h:h + 1]
                dt_r = dtr_ref[h:h + 1, :]
                L = jnp.exp(jnp.where(tri, a_c - a_r, NEG))
                W = CB * L * dt_r
                y = jnp.where(masks[r], _dot(W.astype(BF16), xb), y)
                a_q = a_c[Q - 1:Q, :]
                El = jnp.where(masks[r], jnp.exp(a_c), El)
                Wl = jnp.where(masks[r], jnp.exp(a_q - a_c) * dt_c, Wl)
                decl = jnp.where(masks1[r], jnp.exp(a_q), decl)
            ys_ref[:, xs] = y + _dot(Cb, S.astype(BF16)) * El
            state_ref[g] = S * decl + _dot(Bb, (x * Wl).astype(BF16), _TN)

    return pl.pallas_call(
        kern, name=name, grid=(nc,),
        in_specs=[sp["xbc"], sp["col"], sp["col"], sp["row"], sp["row"]],
        out_specs=[sp["x"], sp["st"]],
        out_shape=[jax.ShapeDtypeStruct((T, SSM_D_INNER), F32), jax.ShapeDtypeStruct((nc * N, SSM_D_INNER), F32)],
        scratch_shapes=[pltpu.VMEM((SSM_GROUPS, N, _GP), F32)],
        compiler_params=_params(("arbitrary",)),
    )(xbc, dtc, acc_, dtr, acr)


def _ssd_scan_bwd(xbc, dys, dskip, st, dtc, acc_, dtr, acr, *, name):
    T = xbc.shape[0]
    Q, N = SSM_CHUNK, SSM_STATE
    nc = T // Q
    sp = _ssd_specs(lambda j: nc - 1 - j)

    def kern(xbc_ref, dy_ref, dsk_ref, st_ref, dtc_ref, ac_ref, dtr_ref, ar_ref,
             dxbc_ref, dac_ref, dar_ref, ddc_ref, ddr_ref, dstate_ref):
        @pl.when(pl.program_id(0) == 0)
        def _():
            dstate_ref[...] = jnp.zeros_like(dstate_ref)

        r_i = lax.broadcasted_iota(jnp.int32, (Q, Q), 0)
        c_i = lax.broadcasted_iota(jnp.int32, (Q, Q), 1)
        tri = c_i <= r_i
        last_row = lax.broadcasted_iota(jnp.int32, (Q, 1), 0) == Q - 1
        lane128 = lax.broadcasted_iota(jnp.int32, (Q, LANE), 1)
        masks = _head_masks(Q)
        masksN = _head_masks(N)
        masks1 = _head_masks(1)
        zeros = jnp.zeros((Q, _GP), F32)
        dacol = jnp.zeros((Q, LANE), F32)
        ddcol = jnp.zeros((Q, LANE), F32)
        for g in range(SSM_GROUPS):
            xs, bs, cs = _ssd_cols(g)
            dS = dstate_ref[g]
            dSb = dS.astype(BF16)
            S = st_ref[:, xs]
            Sb = S.astype(BF16)
            x = xbc_ref[:, xs]
            xb = x.astype(BF16)
            Bb = xbc_ref[:, bs].astype(BF16)
            Cb = xbc_ref[:, cs].astype(BF16)
            dy = dy_ref[:, xs]
            CB = _dot(Cb, Bb, _NT)
            BdS = _dot(Bb, dSb)
            hx = BdS * x
            yd = _dot(Cb, Sb) * dy
            dSS = dS * S
            dxi, El, Wl = zeros, zeros, zeros
            decl = jnp.zeros((1, _GP), F32)
            dBacc = jnp.zeros((Q, N), F32)
            dCacc = jnp.zeros((Q, N), F32)
            for r in range(_HPG):
                h = g * _HPG + r
                hm = masks[r]
                a_c = ac_ref[:, h:h + 1]
                a_r = ar_ref[h:h + 1, :]
                dt_c = dtc_ref[:, h:h + 1]
                dt_r = dtr_ref[h:h + 1, :]
                L = jnp.exp(jnp.where(tri, a_c - a_r, NEG))
                GL = CB * L
                W = GL * dt_r
                dym = jnp.where(hm, dy, 0.0).astype(BF16)
                dW = _dot(dym, xb, _NT)
                E = dW * W
                da_c = jnp.sum(E, axis=1, keepdims=True)
                dar_ref[h:h + 1, :] = -jnp.sum(E, axis=0, keepdims=True)
                ddr_ref[h:h + 1, :] = jnp.sum(dW * GL, axis=0, keepdims=True)
                dGb = (dW * L * dt_r).astype(BF16)
                dCacc = dCacc + _dot(dGb, Bb)
                dBacc = dBacc + _dot(dGb, Cb, _TN)
                dxi = dxi + _dot(W.astype(BF16), dym, _TN)
                a_q = a_c[Q - 1:Q, :]
                e_c = jnp.exp(a_c)
                eq_c = jnp.exp(a_q - a_c)
                w_c = eq_c * dt_c
                ydr = jnp.sum(jnp.where(hm, yd, 0.0), axis=1, keepdims=True) * e_c
                h_c = jnp.sum(jnp.where(hm, hx, 0.0), axis=1, keepdims=True)
                hw = h_c * w_c
                dss = jnp.sum(jnp.sum(jnp.where(masksN[r], dSS, 0.0), axis=1, keepdims=True), axis=0, keepdims=True)
                s_q = jnp.sum(hw, axis=0, keepdims=True) + jnp.exp(a_q) * dss
                da_c = da_c + ydr - hw + jnp.where(last_row, s_q, 0.0)
                dacol = jnp.where(lane128 == h, da_c, dacol)
                ddcol = jnp.where(lane128 == h, h_c * eq_c, ddcol)
                El = jnp.where(hm, e_c, El)
                Wl = jnp.where(hm, w_c, Wl)
                decl = jnp.where(masks1[r], jnp.exp(a_q), decl)
            dxbc_ref[:, xs] = dxi + BdS * Wl + dsk_ref[:, xs]
            dxbc_ref[:, bs] = dBacc + _dot((x * Wl).astype(BF16), dSb, _NT)
            dyE = (dy * El).astype(BF16)
            dxbc_ref[:, cs] = dCacc + _dot(dyE, Sb, _NT)
            dstate_ref[g] = dS * decl + _dot(Cb, dyE, _TN)
        dac_ref[...] = dacol
        ddc_ref[...] = ddcol

    return pl.pallas_call(
        kern, name=name, grid=(nc,),
        in_specs=[sp["xbc"], sp["x"], sp["x"], sp["st"], sp["col"], sp["col"], sp["row"], sp["row"]],
        out_specs=[sp["xbc"], sp["col"], sp["row"], sp["col"], sp["row"]],
        out_shape=[jax.ShapeDtypeStruct((T, SSM_XBC), F32),
                   jax.ShapeDtypeStruct((T, LANE), F32), jax.ShapeDtypeStruct((SSM_HEADS, T), F32),
                   jax.ShapeDtypeStruct((T, LANE), F32), jax.ShapeDtypeStruct((SSM_HEADS, T), F32)],
        scratch_shapes=[pltpu.VMEM((SSM_GROUPS, N, _GP), F32)],
        compiler_params=_params(("arbitrary",)),
    )(xbc, dys, dskip, st, dtc, acc_, dtr, acr)


def _gate_norm_fwd(ys, xbc, proj, d_exp, norm_w, *, name):
    T = ys.shape[0]
    bt = _pick(T, NARROW_ROW_BLOCK, SUBLANE)

    def kern(ys_ref, x_ref, z_ref, d_ref, w_ref, o_ref):
        z = z_ref[...]
        yz = (ys_ref[...] + d_ref[...] * x_ref[...]) * (z * _sigmoid(z))
        rstd = lax.rsqrt(jnp.mean(yz * yz, axis=-1, keepdims=True) + RMS_EPS)
        o_ref[...] = (yz * rstd * w_ref[...]).astype(BF16)

    blk = pl.BlockSpec((bt, _GP), lambda i, g: (i, g))
    vec = pl.BlockSpec((1, _GP), lambda i, g: (0, g))
    return pl.pallas_call(
        kern, name=name, grid=(T // bt, SSM_GROUPS), in_specs=[blk, blk, blk, vec, vec], out_specs=blk,
        out_shape=jax.ShapeDtypeStruct((T, SSM_D_INNER), BF16), compiler_params=_params(("parallel", "parallel")),
    )(ys, xbc, proj, d_exp, norm_w)


def _gate_norm_bwd(dyn, ys, xbc, proj, d_exp, norm_w, *, name):
    T = ys.shape[0]
    bt = _pick(T, NARROW_ROW_BLOCK, SUBLANE)

    def kern(dyn_ref, ys_ref, x_ref, z_ref, d_ref, w_ref, dz_ref, dys_ref, dsk_ref, dw_ref, dd_ref):
        i = pl.program_id(1)
        z = z_ref[...]
        x = x_ref[...]
        sg = _sigmoid(z)
        sz = z * sg
        y = ys_ref[...] + d_ref[...] * x
        yz = y * sz
        rstd = lax.rsqrt(jnp.mean(yz * yz, axis=-1, keepdims=True) + RMS_EPS)
        yhat = yz * rstd
        dynv = dyn_ref[...]
        gg = dynv * w_ref[...]
        dyz = rstd * (gg - yhat * jnp.mean(gg * yhat, axis=-1, keepdims=True))
        dy = dyz * sz
        dz_ref[...] = (dyz * y * sg * (1.0 + z * (1.0 - sg))).astype(BF16)
        dys_ref[...] = dy
        dsk_ref[...] = dy * d_ref[...]

        @pl.when(i == 0)
        def _():
            dw_ref[...] = jnp.zeros_like(dw_ref)
            dd_ref[...] = jnp.zeros_like(dd_ref)

        dw_ref[...] += jnp.sum(dynv * yhat, axis=0, keepdims=True)
        dd_ref[...] += jnp.sum(dy * x, axis=0, keepdims=True)

    blk = pl.BlockSpec((bt, _GP), lambda g, i: (i, g))
    vec = pl.BlockSpec((1, _GP), lambda g, i: (0, g))
    act = jax.ShapeDtypeStruct((T, SSM_D_INNER), F32)
    par = jax.ShapeDtypeStruct((1, SSM_D_INNER), F32)
    return pl.pallas_call(
        kern, name=name, grid=(SSM_GROUPS, T // bt), in_specs=[blk, blk, blk, blk, vec, vec],
        out_specs=[blk, blk, blk, vec, vec],
        out_shape=[jax.ShapeDtypeStruct((T, SSM_IN_PAD), BF16), act, act, par, par],
        compiler_params=_params(("parallel", "arbitrary")),
    )(dyn, ys, xbc, proj, d_exp, norm_w)


def _loss_head(y, target, *, name):
    T, D = y.shape
    bt = _pick(T, ROW_BLOCK, SUBLANE)

    def kern(y_ref, t_ref, l_ref, dy_ref):
        i = pl.program_id(0)
        err = y_ref[...] - t_ref[...]
        dy_ref[...] = err * (1.0 / D)

        @pl.when(i == 0)
        def _():
            l_ref[...] = jnp.zeros_like(l_ref)

        l_ref[...] += jnp.sum(err * err, axis=0, keepdims=True) * (0.5 / D)

    row = pl.BlockSpec((bt, D), lambda i: (i, 0))
    vec = pl.BlockSpec((1, D), lambda i: (0, 0))
    return pl.pallas_call(
        kern, name=name, grid=(T // bt,), in_specs=[row, row], out_specs=[vec, row],
        out_shape=[jax.ShapeDtypeStruct((1, D), F32), jax.ShapeDtypeStruct((T, D), F32)],
        compiler_params=_params(("arbitrary",)),
    )(y, target)


def _adamw(w, g, m, v, *, name):
    shape = w.shape
    w, g, m, v = (t.reshape(-1, shape[-1]) for t in (w, g, m, v))
    R, C = w.shape
    br = _pick(R, 256, SUBLANE)

    def kern(w_ref, g_ref, m_ref, v_ref, d_ref, nm_ref, nv_ref):
        gv = g_ref[...]
        nm = ADAM_B1 * m_ref[...] + (1.0 - ADAM_B1) * gv
        nv = ADAM_B2 * v_ref[...] + (1.0 - ADAM_B2) * (gv * gv)
        m_hat = nm / (1.0 - ADAM_B1 ** ADAM_STEP)
        v_hat = nv / (1.0 - ADAM_B2 ** ADAM_STEP)
        d_ref[...] = -ADAM_LR * (m_hat / (jnp.sqrt(v_hat) + ADAM_EPS) + ADAM_WD * w_ref[...])
        nm_ref[...] = nm
        nv_ref[...] = nv

    blk = pl.BlockSpec((br, C), lambda i: (i, 0))
    outs = pl.pallas_call(
        kern, name=name, grid=(R // br,), in_specs=[blk] * 4, out_specs=[blk] * 3,
        out_shape=[jax.ShapeDtypeStruct((R, C), F32)] * 3, compiler_params=_params(("parallel",)),
    )(w, g, m, v)
    return [o.reshape(shape) for o in outs]


def _add2(a, b, out_dtype, *, name):
    shape = a.shape
    a2, b2 = a.reshape(-1, shape[-1]), b.reshape(-1, shape[-1])
    R, C = a2.shape
    br = _pick(R, 512, SUBLANE)

    def kern(a_ref, b_ref, o_ref):
        o_ref[...] = (a_ref[...].astype(F32) + b_ref[...].astype(F32)).astype(out_dtype)

    blk = pl.BlockSpec((br, C), lambda i: (i, 0))
    return pl.pallas_call(
        kern, name=name, grid=(R // br,), in_specs=[blk, blk], out_specs=blk,
        out_shape=jax.ShapeDtypeStruct((R, C), out_dtype), compiler_params=_params(("parallel",)),
    )(a2, b2).reshape(shape)


def _sum4(buf, *, name):
    _, R, C = buf.shape
    br = _pick(R, 512, SUBLANE)

    def kern(b_ref, o_ref):
        b = [b_ref[k].astype(F32) for k in range(4)]
        o_ref[...] = ((b[0] + b[1]) + b[2]) + b[3]

    return pl.pallas_call(
        kern, name=name, grid=(R // br,), in_specs=[pl.BlockSpec((4, br, C), lambda i: (0, i, 0))],
        out_specs=pl.BlockSpec((br, C), lambda i: (i, 0)),
        out_shape=jax.ShapeDtypeStruct((R, C), F32), compiler_params=_params(("parallel",)),
    )(buf)


def _place():
    x, y, c = lax.axis_index("x"), lax.axis_index("y"), lax.axis_index("c")
    other_chips = [(1 - x, y), (x, 1 - y), (1 - x, 1 - y)]
    return x, y, c, other_chips


def _gather_chips(w, *, name):
    R, C = w.shape
    H = R // 2

    def body(w_ref, out_ref, send_sems, recv_sems):
        x, y, c, chips = _place()
        me_chip = 2 * x + y
        sib = (x, y, 1 - c)

        def rows(chip, hc):
            return out_ref.at[chip, pl.ds(hc * H, H), :]

        def copy(k, blk, to, src=None):
            return pltpu.make_async_remote_copy(
                src_ref=blk if src is None else src, dst_ref=blk, send_sem=send_sems.at[k], recv_sem=recv_sems.at[k],
                device_id=to, device_id_type=MESH)

        first = [copy(j, rows(me_chip, c), (cx, cy, c), src=w_ref.at[pl.ds(c * H, H), :])
                 for j, (cx, cy) in enumerate(chips)]
        for cp in first:
            cp.start()
        passed = []
        for j, (cx, cy) in enumerate(chips):
            blk = rows(2 * cx + cy, c)
            copy(j, blk, (cx, cy, c)).wait_recv()
            fw = copy(3 + j, blk, sib)
            fw.start()
            passed.append(fw)
        for j, (cx, cy) in enumerate(chips):
            copy(3 + j, rows(2 * cx + cy, 1 - c), sib).wait_recv()
        for cp in first + passed:
            cp.wait_send()

    return pl.pallas_call(
        body, name=name, in_specs=[_ANY], out_specs=_ANY,
        out_shape=jax.ShapeDtypeStruct((4, R, C), w.dtype),
        scratch_shapes=[pltpu.SemaphoreType.DMA((6,)), pltpu.SemaphoreType.DMA((6,))],
    )(w)


def _pair_swap(v, *, name, other_half=False):
    shape = (v.shape[0], v.shape[1] // 2, v.shape[2]) if other_half else v.shape

    def body(v_ref, out_ref, send_sem, recv_sem):
        x, y, c, _ = _place()
        src = v_ref.at[:, pl.ds((1 - c) * shape[1], shape[1]), :] if other_half else v_ref
        cp = pltpu.make_async_remote_copy(src_ref=src, dst_ref=out_ref, send_sem=send_sem, recv_sem=recv_sem,
                                          device_id=(x, y, 1 - c), device_id_type=MESH)
        cp.start()
        cp.wait()

    return pl.pallas_call(
        body, name=name, in_specs=[_ANY], out_specs=_ANY, out_shape=jax.ShapeDtypeStruct(shape, v.dtype),
        scratch_shapes=[pltpu.SemaphoreType.DMA, pltpu.SemaphoreType.DMA],
    )(v)


def _chip_exchange(pv, *, name):
    def body(p_ref, out_ref, send_sems, recv_sems):
        x, y, c, chips = _place()
        me_chip = 2 * x + y
        sends = []
        for j, (cx, cy) in enumerate(chips):
            cp = pltpu.make_async_remote_copy(
                src_ref=p_ref.at[2 * cx + cy], dst_ref=out_ref.at[me_chip], send_sem=send_sems.at[j],
                recv_sem=recv_sems.at[j], device_id=(cx, cy, c), device_id_type=MESH)
            cp.start()
            sends.append(cp)
        for j, (cx, cy) in enumerate(chips):
            blk = out_ref.at[2 * cx + cy]
            pltpu.make_async_remote_copy(src_ref=blk, dst_ref=blk, send_sem=send_sems.at[j], recv_sem=recv_sems.at[j],
                                         device_id=(cx, cy, c), device_id_type=MESH).wait_recv()
        for cp in sends:
            cp.wait_send()

    return pl.pallas_call(
        body, name=name, in_specs=[_ANY], out_specs=_ANY, out_shape=jax.ShapeDtypeStruct(pv.shape, pv.dtype),
        scratch_shapes=[pltpu.SemaphoreType.DMA((3,)), pltpu.SemaphoreType.DMA((3,))],
    )(pv)


WEIGHTS = [
    ("attn_w_in", 2), ("attn_b_f", None), ("attn_w_out", 1), ("ssm_w_in", 2), ("ssm_conv_w", 2), ("ssm_conv_b", 1),
    ("ssm_dt_bias", None), ("ssm_A_log", None), ("ssm_D", None), ("ssm_norm_w", 1), ("ssm_w_out", 1),
    ("ln_mix_g", None), ("ln_mix_b", None), ("ffn_w_up", 2), ("ffn_conv_w", 2), ("ffn_conv_b", None),
    ("ffn_w_down", 1), ("ln_ffn_g", None), ("ln_ffn_b", None), ("ple_w_proj", 2), ("ple_w_gate", 1),
    ("ple_b_gate", None),
]
N_CHIPS = 4
MATMUL_WEIGHTS = ("attn_w_in", "attn_w_out", "ssm_w_in", "ssm_w_out", "ffn_w_up", "ffn_w_down", "ple_w_proj",
                  "ple_w_gate")


def _pack(arrays):
    parts = []
    total = 0
    for a in arrays:
        n = a.size
        pad = (-n) % PACK_COLS
        flat = a.reshape(-1)
        parts.append(jnp.pad(flat, (0, pad)) if pad else flat)
        total += n + pad
    rows = total // PACK_COLS
    rpad = (-rows) % PACK_ROW_ALIGN
    if rpad:
        parts.append(jnp.zeros((rpad * PACK_COLS,), arrays[0].dtype))
    return jnp.concatenate(parts).reshape(rows + rpad, PACK_COLS)


def _unpack(buf, shapes):
    flat = buf.reshape(-1)
    out = []
    off = 0
    for s in shapes:
        n = math.prod(s)
        out.append(flat[off:off + n].reshape(s))
        off += n + ((-n) % PACK_COLS)
    return out


def _from_row_layout(a):
    return jnp.pad(a.T, ((0, 0), (0, LANE - SSM_HEADS)))


def _pad_lanes(v, n=LANE):
    return jnp.pad(v, (0, n - v.shape[0])).reshape(1, n)


def _local_step(x, p, target, W):
    T = x.shape[0]
    row = lambda v: v.reshape(1, -1)
    attn_in = jnp.pad(W["attn_w_in"][0], ((0, 0), (0, ATTN_IN_PAD - W["attn_w_in"].shape[2])))
    ssm_in = jnp.pad(W["ssm_w_in"][0], ((0, 0), (0, SSM_IN_PAD - W["ssm_w_in"].shape[2])))
    bf = _pad_lanes(W["attn_b_f"][0])
    dt_bias = _pad_lanes(W["ssm_dt_bias"][0])
    a_log = _pad_lanes(W["ssm_A_log"][0])
    d_exp = jnp.repeat(W["ssm_D"][0], _PH).reshape(1, SSM_D_INNER)
    norm_w = row(W["ssm_norm_w"][0])
    G = {}

    def ffn_ple_fwd(i, xin, mix, tag):
        s = {}
        s["z1"], s["h1"], s["h1b"] = _ln_fwd(xin, mix, row(W["ln_mix_g"][i]), row(W["ln_mix_b"][i]),
                                             name=f"ln_mix_fwd{tag}")
        s["up"] = _mm(s["h1b"], W["ffn_w_up"][i], name=f"ffn_up{tag}")
        s["a"] = _ffn_act_fwd(s["up"], W["ffn_conv_w"][i], row(W["ffn_conv_b"][i]), name=f"ffn_act_fwd{tag}")
        ffn = _mm(s["a"], W["ffn_w_down"][i], name=f"ffn_down{tag}")
        s["z2"], s["h2"], s["h2b"] = _ln_fwd(s["h1"], ffn, row(W["ln_ffn_g"][i]), row(W["ln_ffn_b"][i]),
                                             name=f"ln_ffn_fwd{tag}")
        s["G"] = _mm(s["h2b"], W["ple_w_gate"][i], name=f"ple_gate_mm{tag}")
        s["pp"] = _mm(pb[i], W["ple_w_proj"][i], name=f"ple_proj_mm{tag}")
        out, outb = _ple_fwd(s["h2"], s["G"], row(W["ple_b_gate"][i]), s["pp"], name=f"ple_fwd{tag}")
        return out, outb, s

    def ffn_ple_bwd(i, dx, s, tag):
        g = {}
        dG, dpp, g["ple_b_gate"] = _ple_bwd(dx, s["G"], row(W["ple_b_gate"][i]), s["pp"], name=f"ple_bwd{tag}")
        g["ple_w_gate"] = _mm(s["h2b"], dG, ta=True, name=f"ple_gate_dw{tag}")
        g["ple_w_proj"] = _mm(pb[i], dpp, ta=True, name=f"ple_proj_dw{tag}")
        dh2 = _mm(dG, W["ple_w_gate"][i], tb=True, add=dx, name=f"ple_gate_dx{tag}")
        dz2, dz2b, g["ln_ffn_g"], g["ln_ffn_b"] = _ln_bwd(dh2, s["z2"], row(W["ln_ffn_g"][i]), name=f"ln_ffn_bwd{tag}")
        da = _mm(dz2b, W["ffn_w_down"][i], tb=True, out_dtype=BF16, name=f"ffn_down_dx{tag}")
        g["ffn_w_down"] = _mm(s["a"], dz2b, ta=True, name=f"ffn_down_dw{tag}")
        dup, dgc, g["ffn_conv_b"], g["ffn_conv_w"] = _ffn_act_bwd(
            da, s["up"], W["ffn_conv_w"][i], row(W["ffn_conv_b"][i]), name=f"ffn_act_bwd{tag}")
        dup = _dwconv_bwd_data(dgc, W["ffn_conv_w"][i], FFN_CONV, dup, FFN_DIM, name=f"ffn_conv_bwd{tag}")
        g["ffn_w_up"] = _mm(s["h1b"], dup, ta=True, name=f"ffn_up_dw{tag}")
        dh1 = _mm(dup, W["ffn_w_up"][i], tb=True, add=dz2, add_scale=DEEPNORM_ALPHA, name=f"ffn_up_dx{tag}")
        dz1, dz1b, g["ln_mix_g"], g["ln_mix_b"] = _ln_bwd(dh1, s["z1"], row(W["ln_mix_g"][i]), name=f"ln_mix_bwd{tag}")
        return dz1, dz1b, g

    xb = x.astype(BF16)
    pb = p.astype(BF16)
    proj0 = _mm(xb, attn_in, name="attn_in")
    c_col = _fox_gate_fwd(proj0, bf, name="fox_gate_fwd")
    cT = (c_col[:, :ATTN_HEADS] * LOG2E).T.reshape(HEAD_PAIRS, 2, T)
    qa, qb, kk, ka, kb, vv, va, vb = _attn_prep(proj0, name="attn_prep")
    o, ob, lsea, lseb = _attn_fwd(qa, qb, kk, va, vb, cT, name="attn_fwd")
    mix0 = _mm(ob, W["attn_w_out"][0], name="attn_out")
    x1, x1b, s0 = ffn_ple_fwd(0, x, mix0, "0")

    proj1 = _mm(x1b, ssm_in, name="ssm_in")
    dt, acum = _ssd_dt_fwd(proj1, dt_bias, a_log, name="ssd_dt_fwd")
    xbc = _conv_silu_fwd(proj1, W["ssm_conv_w"][0], row(W["ssm_conv_b"][0]), name="ssd_conv_fwd")
    dtr, acr = dt[:, :SSM_HEADS].T, acum[:, :SSM_HEADS].T
    ys, states = _ssd_scan_fwd(xbc, dt, acum, dtr, acr, name="ssd_scan_fwd")
    yn = _gate_norm_fwd(ys, xbc, proj1, d_exp, norm_w, name="ssd_gate_norm_fwd")
    mix1 = _mm(yn, W["ssm_w_out"][0], name="ssm_out")
    x2, _, s1 = ffn_ple_fwd(1, x1, mix1, "1")

    lpart, dy = _loss_head(x2, target, name="loss_head")
    loss = jnp.sum(lpart)

    dz1, dz1b, g1 = ffn_ple_bwd(1, dy, s1, "1")
    G["ssm_w_out"] = _mm(yn, dz1b, ta=True, name="ssm_out_dw")[None]
    dyn = _mm(dz1b, W["ssm_w_out"][0], tb=True, name="ssm_out_dx")
    dproj1, dys, dskip, dnw, dde = _gate_norm_bwd(dyn, ys, xbc, proj1, d_exp, norm_w, name="ssd_gate_norm_bwd")
    G["ssm_norm_w"] = dnw
    G["ssm_D"] = dde.reshape(SSM_HEADS, _PH).sum(axis=1)[None]
    dxbc, dac, dar, ddc, ddr = _ssd_scan_bwd(xbc, dys, dskip, states, dt, acum, dtr, acr, name="ssd_scan_bwd")
    dproj1, dal, ddb = _ssd_dt_bwd(dac, _from_row_layout(dar), ddc, _from_row_layout(ddr), dt, proj1, dt_bias, a_log,
                                   dproj1, name="ssd_dt_bwd")
    G["ssm_A_log"] = dal[:, :SSM_HEADS]
    G["ssm_dt_bias"] = ddb[:, :SSM_HEADS]
    dpre, G["ssm_conv_b"], dcw = _conv_silu_bwd(dxbc, proj1, W["ssm_conv_w"][0], row(W["ssm_conv_b"][0]),
                                                name="ssd_conv_bwd")
    G["ssm_conv_w"] = dcw[None]
    dproj1 = _dwconv_bwd_data(dpre, W["ssm_conv_w"][0], SSM_CONV, dproj1, SSM_D_INNER, name="ssd_conv_bwd_data")
    G["ssm_w_in"] = _mm(x1b, dproj1, ta=True, name="ssm_in_dw")[None, :, :W["ssm_w_in"].shape[2]]
    dx1 = _mm(dproj1, ssm_in, tb=True, add=dz1, add_scale=DEEPNORM_ALPHA, name="ssm_in_dx")

    dz0, dz0b, g0 = ffn_ple_bwd(0, dx1, s0, "0")
    G["attn_w_out"] = _mm(ob, dz0b, ta=True, name="attn_out_dw")[None]
    do = _mm(dz0b, W["attn_w_out"][0], tb=True, name="attn_out_dx")
    doa, dob, dlta, dltb = _attn_bwd_prep(do, o, name="attn_bwd_prep")
    dq, dk, dv, dcT, dcq = _attn_bwd(qa, qb, kk, ka, kb, vv, doa, dob, lsea, lseb, dlta, dltb, cT, name="attn_bwd")
    dc_col = jnp.pad(dcT.reshape(ATTN_HEADS, T).T + dcq[:, :ATTN_HEADS], ((0, 0), (0, LANE - ATTN_HEADS)))
    dproj0, dbf = _fox_gate_bwd(dc_col, proj0, bf, dq, dk, dv, name="fox_gate_bwd")
    G["attn_b_f"] = dbf[:, :ATTN_HEADS]
    G["attn_w_in"] = _mm(xb, dproj0, ta=True, name="attn_in_dw")[None, :, :W["attn_w_in"].shape[2]]
    grad_x = _mm(dproj0, attn_in, tb=True, add=dz0, add_scale=DEEPNORM_ALPHA, name="attn_in_dx")

    for k in g0:
        G[k] = jnp.stack([g0[k].reshape(W[k].shape[1:]), g1[k].reshape(W[k].shape[1:])])
    return loss, grad_x, G


def kernel(x, p, attn_w_in, attn_b_f, attn_w_out, ssm_w_in, ssm_conv_w, ssm_conv_b, ssm_dt_bias, ssm_A_log, ssm_D, ssm_norm_w, ssm_w_out, ln_mix_g, ln_mix_b, ffn_w_up, ffn_conv_w, ffn_conv_b, ffn_w_down, ln_ffn_g, ln_ffn_b, ple_w_proj, ple_w_gate, ple_b_gate, loss_target, m_attn_w_in, m_attn_b_f, m_attn_w_out, m_ssm_w_in, m_ssm_conv_w, m_ssm_conv_b, m_ssm_dt_bias, m_ssm_A_log, m_ssm_D, m_ssm_norm_w, m_ssm_w_out, m_ln_mix_g, m_ln_mix_b, m_ffn_w_up, m_ffn_conv_w, m_ffn_conv_b, m_ffn_w_down, m_ln_ffn_g, m_ln_ffn_b, m_ple_w_proj, m_ple_w_gate, m_ple_b_gate, v_attn_w_in, v_attn_b_f, v_attn_w_out, v_ssm_w_in, v_ssm_conv_w, v_ssm_conv_b, v_ssm_dt_bias, v_ssm_A_log, v_ssm_D, v_ssm_norm_w, v_ssm_w_out, v_ln_mix_g, v_ln_mix_b, v_ffn_w_up, v_ffn_conv_w, v_ffn_conv_b, v_ffn_w_down, v_ln_ffn_g, v_ln_ffn_b, v_ple_w_proj, v_ple_w_gate, v_ple_b_gate):
    names = [n for n, _ in WEIGHTS]
    axes = dict(WEIGHTS)
    w_loc = dict(zip(names, [attn_w_in, attn_b_f, attn_w_out, ssm_w_in, ssm_conv_w, ssm_conv_b, ssm_dt_bias, ssm_A_log, ssm_D, ssm_norm_w, ssm_w_out, ln_mix_g, ln_mix_b, ffn_w_up, ffn_conv_w, ffn_conv_b, ffn_w_down, ln_ffn_g, ln_ffn_b, ple_w_proj, ple_w_gate, ple_b_gate]))
    m_loc = dict(zip(names, [m_attn_w_in, m_attn_b_f, m_attn_w_out, m_ssm_w_in, m_ssm_conv_w, m_ssm_conv_b, m_ssm_dt_bias, m_ssm_A_log, m_ssm_D, m_ssm_norm_w, m_ssm_w_out, m_ln_mix_g, m_ln_mix_b, m_ffn_w_up, m_ffn_conv_w, m_ffn_conv_b, m_ffn_w_down, m_ln_ffn_g, m_ln_ffn_b, m_ple_w_proj, m_ple_w_gate, m_ple_b_gate]))
    v_loc = dict(zip(names, [v_attn_w_in, v_attn_b_f, v_attn_w_out, v_ssm_w_in, v_ssm_conv_w, v_ssm_conv_b, v_ssm_dt_bias, v_ssm_A_log, v_ssm_D, v_ssm_norm_w, v_ssm_w_out, v_ln_mix_g, v_ln_mix_b, v_ffn_w_up, v_ffn_conv_w, v_ffn_conv_b, v_ffn_w_down, v_ln_ffn_g, v_ln_ffn_b, v_ple_w_proj, v_ple_w_gate, v_ple_b_gate]))
    sharded = [n for n in names if axes[n] is not None]
    matrices = [n for n in sharded if n in MATMUL_WEIGHTS]

    def wire(n):
        if n in matrices:
            return w_loc[n].astype(BF16)
        return lax.bitcast_convert_type(w_loc[n], BF16)

    wired = [wire(n) for n in sharded]
    me_chip = 2 * lax.axis_index("x") + lax.axis_index("y")
    packed = _pack(wired)
    gathered = lax.dynamic_update_index_in_dim(_gather_chips(packed, name="gather_weights"), packed, me_chip, 0)
    W = dict(w_loc)
    per_chip = [_unpack(gathered[k], [w.shape for w in wired]) for k in range(N_CHIPS)]
    for i, n in enumerate(sharded):
        pieces = [per_chip[k][i] for k in range(N_CHIPS)]
        if n not in matrices:
            pieces = [lax.bitcast_convert_type(q, F32) for q in pieces]
        W[n] = jnp.concatenate(pieces, axis=axes[n])

    loss, grad_x, G = _local_step(x[0], p[:, 0], loss_target[0], W)
    loss = lax.psum(loss, ("x", "y", "c"))

    def slot(k):
        parts = []
        for n in names:
            g = G[n].reshape(W[n].shape)
            if axes[n] is not None:
                size = w_loc[n].shape[axes[n]]
                g = lax.slice_in_dim(g, k * size, (k + 1) * size, axis=axes[n])
            parts.append(g.astype(BF16))
        return _pack(parts)

    contrib = jnp.stack([slot(k) for k in range(N_CHIPS)])
    R = contrib.shape[1]
    H = R // 2
    c = lax.axis_index("c")
    keep = lax.dynamic_slice_in_dim(contrib, c * H, H, axis=1)
    pair = _add2(keep, _pair_swap(contrib, other_half=True, name="grad_pair_swap"), BF16, name="grad_pair_sum")
    from_chips = lax.dynamic_update_index_in_dim(
        _chip_exchange(pair, name="grad_chip_exchange"), lax.dynamic_index_in_dim(pair, me_chip, 0, keepdims=False),
        me_chip, 0)
    half = _sum4(from_chips, name="grad_chip_sum")
    other = _pair_swap(half, name="grad_half_swap")
    gflat = jnp.concatenate([jnp.where(c == 0, half, other), jnp.where(c == 0, other, half)])

    grads = _unpack(gflat, [w_loc[n].shape for n in names])
    steps = [_adamw(w_loc[n], g, m_loc[n], v_loc[n], name=f"adamw_{n}") for n, g in zip(names, grads)]
    return (loss, grad_x[None], *grads, *[s[0] for s in steps], *[s[1] for s in steps], *[s[2] for s in steps])
```

```python
import math

import jax
import jax.numpy as jnp
from jax import lax
from jax.experimental import pallas as pl
from jax.experimental.pallas import tpu as pltpu

F32 = jnp.float32
BF16 = jnp.bfloat16
MESH = pl.DeviceIdType.MESH

D_MODEL = 1024
ATTN_HEADS = 16
HEAD_PAIRS = ATTN_HEADS // 2
SSM_D_INNER = 2048
SSM_HEADS = 32
SSM_GROUPS = 8
SSM_STATE = 128
SSM_CONV = 4
SSM_CHUNK = 128
SSM_XBC = SSM_D_INNER + 2 * SSM_GROUPS * SSM_STATE
FFN_DIM = 2816
FFN_CONV = 3
DEPTH = 2
LN_EPS = 1e-5
RMS_EPS = 1e-5
DEEPNORM_ALPHA = (2 * DEPTH) ** 0.25
ADAM_LR = 0.001
ADAM_B1 = 0.9
ADAM_B2 = 0.999
ADAM_EPS = 1e-08
ADAM_WD = 0.01
ADAM_STEP = 10

LANE = 128
SUBLANE = 8
HALO = SUBLANE
HALO_BF16 = 2 * SUBLANE
NEG = -1e30
ATTN_IN_PAD = 3 * D_MODEL + LANE
SSM_IN_PAD = 2 * SSM_D_INNER + 2 * SSM_GROUPS * SSM_STATE + LANE
PACK_COLS = 1024
PACK_ROW_ALIGN = 512

ATTN_BLOCK = 1024
ROW_BLOCK = 512
NARROW_ROW_BLOCK = 2048
CUM_BLOCK = 512


def _params(sem, vmem_mb=48):
    return pltpu.CompilerParams(dimension_semantics=sem, vmem_limit_bytes=vmem_mb * 2 ** 20)


def _pick(n, target, mult=LANE):
    best = None
    d = mult
    while d <= min(n, target):
        if n % d == 0:
            best = d
        d += mult
    return n if best is None else best


def _sigmoid(x):
    return 1.0 / (1.0 + jnp.exp(-x))


def _log1p(u):
    w = 1.0 + u
    return jnp.where(w == 1.0, u, jnp.log(w) * (u / (w - 1.0)))


def _softplus(x):
    return jnp.maximum(x, 0.0) + _log1p(jnp.exp(-jnp.abs(x)))


def _split3(x):
    hi = x.astype(BF16)
    r1 = x - hi.astype(F32)
    mid = r1.astype(BF16)
    lo = (r1 - mid.astype(F32)).astype(BF16)
    return hi, mid, lo


def _tri_matmul(tri, x):
    out = None
    for part in _split3(x):
        t = jnp.dot(tri, part, preferred_element_type=F32)
        out = t if out is None else out + t
    return out


def _tri(n, lower):
    r = lax.broadcasted_iota(jnp.int32, (n, n), 0)
    c = lax.broadcasted_iota(jnp.int32, (n, n), 1)
    return jnp.where((c <= r) if lower else (c >= r), 1.0, 0.0).astype(BF16)


_ANY = pl.BlockSpec(memory_space=pl.ANY)
MM_OUT_BLOCK_BYTES = 13 * 2 ** 20
MM_IN_BLOCK_BYTES = 6 * 2 ** 20
MM_VMEM_LIMIT_MB = 56
MM_VMEM_BUDGET = 44 * 2 ** 20
MM_WIDE_K = 3200


def _mm(a, b, *, name, ta=False, tb=False, add=None, add_scale=1.0, out_dtype=F32):
    if ta:
        K, M = a.shape
    else:
        M, K = a.shape
    if tb:
        N, Kb = b.shape
    else:
        Kb, N = b.shape
    assert K == Kb, (a.shape, b.shape, ta, tb)
    if ta:
        assert add is None and out_dtype == F32
        bm = _pick(M, 2816)
        bn = _pick(N, MM_OUT_BLOCK_BYTES // (4 * bm))
        bk = _pick(K, max(512, MM_IN_BLOCK_BYTES // (2 * max(bm, bn))))
    else:
        bn = _pick(N, 1536 if K <= MM_WIDE_K else 512)
        bk = K
        bm = _pick(M, 2048)
        if 4 * bm * K + 4 * K * bn + (8 if add is None else 16) * bm * bn > MM_VMEM_BUDGET:
            bm = _pick(M, 1024)
    nk = K // bk
    a_spec = pl.BlockSpec((bk, bm), lambda i, j, k: (k, i)) if ta else pl.BlockSpec((bm, bk), lambda i, j, k: (i, k))
    b_spec = pl.BlockSpec((bn, bk), lambda i, j, k: (j, k)) if tb else pl.BlockSpec((bk, bn), lambda i, j, k: (k, j))
    o_spec = pl.BlockSpec((bm, bn), lambda i, j, k: (i, j))
    dims = (((0 if ta else 1,), (1 if tb else 0,)), ((), ()))
    has_add = add is not None

    def kern(*refs):
        a_ref, b_ref = refs[0], refs[1]
        add_ref = refs[2] if has_add else None
        o_ref = refs[3] if has_add else refs[2]
        k = pl.program_id(2)
        part = lax.dot_general(a_ref[...].astype(BF16), b_ref[...].astype(BF16), dims, preferred_element_type=F32)
        if nk == 1:
            o_ref[...] = (part + add_scale * add_ref[...] if has_add else part).astype(out_dtype)
        else:
            @pl.when(k == 0)
            def _():
                o_ref[...] = part

            @pl.when(k > 0)
            def _():
                o_ref[...] += part

    ins = [a, b] + ([add] if has_add else [])
    in_specs = [a_spec, b_spec] + ([o_spec] if has_add else [])
    return pl.pallas_call(
        kern, name=name, grid=(M // bm, N // bn, nk),
        in_specs=in_specs, out_specs=o_spec,
        out_shape=jax.ShapeDtypeStruct((M, N), out_dtype),
        compiler_params=_params(("parallel", "parallel", "arbitrary"), vmem_mb=MM_VMEM_LIMIT_MB),
    )(*ins)


def _ln_stats(z):
    mu = jnp.mean(z, axis=-1, keepdims=True)
    zc = z - mu
    var = jnp.mean(zc * zc, axis=-1, keepdims=True)
    return zc, lax.rsqrt(var + LN_EPS)


def _ln_fwd(x, r, g, b, *, name):
    T, D = x.shape
    bt = _pick(T, ROW_BLOCK, SUBLANE)

    def kern(x_ref, r_ref, g_ref, b_ref, z_ref, h_ref, hb_ref):
        z = DEEPNORM_ALPHA * x_ref[...] + r_ref[...]
        zc, rstd = _ln_stats(z)
        h = zc * rstd * g_ref[...] + b_ref[...]
        z_ref[...] = z
        h_ref[...] = h
        hb_ref[...] = h.astype(BF16)

    row = pl.BlockSpec((bt, D), lambda i: (i, 0))
    vec = pl.BlockSpec((1, D), lambda i: (0, 0))
    return pl.pallas_call(
        kern, name=name, grid=(T // bt,), in_specs=[row, row, vec, vec], out_specs=[row, row, row],
        out_shape=[jax.ShapeDtypeStruct((T, D), F32)] * 2 + [jax.ShapeDtypeStruct((T, D), BF16)],
        compiler_params=_params(("parallel",)),
    )(x, r, g, b)


def _ln_bwd(dy, z, g, *, name):
    T, D = z.shape
    bt = _pick(T, ROW_BLOCK, SUBLANE)

    def kern(dy_ref, z_ref, g_ref, dz_ref, dzb_ref, dg_ref, db_ref):
        i = pl.program_id(0)
        zc, rstd = _ln_stats(z_ref[...])
        xhat = zc * rstd
        dyv = dy_ref[...]
        dxh = dyv * g_ref[...]
        m1 = jnp.mean(dxh, axis=-1, keepdims=True)
        m2 = jnp.mean(dxh * xhat, axis=-1, keepdims=True)
        dz = rstd * (dxh - m1 - xhat * m2)
        dz_ref[...] = dz
        dzb_ref[...] = dz.astype(BF16)

        @pl.when(i == 0)
        def _():
            dg_ref[...] = jnp.zeros_like(dg_ref)
            db_ref[...] = jnp.zeros_like(db_ref)

        dg_ref[...] += jnp.sum(dyv * xhat, axis=0, keepdims=True)
        db_ref[...] += jnp.sum(dyv, axis=0, keepdims=True)

    row = pl.BlockSpec((bt, D), lambda i: (i, 0))
    vec = pl.BlockSpec((1, D), lambda i: (0, 0))
    return pl.pallas_call(
        kern, name=name, grid=(T // bt,), in_specs=[row, row, vec], out_specs=[row, row, vec, vec],
        out_shape=[jax.ShapeDtypeStruct((T, D), F32), jax.ShapeDtypeStruct((T, D), BF16),
                   jax.ShapeDtypeStruct((1, D), F32), jax.ShapeDtypeStruct((1, D), F32)],
        compiler_params=_params(("arbitrary",)),
    )(dy, z, g)


def _past_taps(ext_ref, K, bt):
    ext = ext_ref[...]
    return [(ext if k == K - 1 else pltpu.roll(ext, K - 1 - k, 0))[HALO:HALO + bt] for k in range(K)]


def _conv_past(taps, cw_ref):
    out = None
    for k, tap in enumerate(taps):
        term = cw_ref[k:k + 1, :] * tap
        out = term if out is None else out + term
    return out


def _fill_ext_past(ext_ref, halo_ref, cur, i, bt):
    ext_ref[pl.ds(0, HALO), :] = jnp.where(i > 0, halo_ref[...], 0.0)
    ext_ref[pl.ds(HALO, bt), :] = cur


def _halo_prev(bt, bc, off):
    return pl.BlockSpec((HALO, bc), lambda i, j: (jnp.maximum(i * (bt // HALO) - 1, 0), j + off))


def _normal_cdf(x):
    return 0.5 * (1.0 + lax.erf(x * (1.0 / math.sqrt(2.0))))


def _gelu(x):
    return x * _normal_cdf(x)


def _gelu_and_grad(x):
    cdf = _normal_cdf(x)
    return x * cdf, cdf + x * jnp.exp(-0.5 * x * x) * (1.0 / math.sqrt(2.0 * math.pi))


def _ffn_act_fwd(up, cw, cb, *, name):
    T, F2 = up.shape
    F = F2 // 2
    bt = _pick(T, ROW_BLOCK, SUBLANE)
    bc = _pick(F, 1408)
    nb = F // bc

    def kern(u_ref, g_ref, halo_ref, cw_ref, cb_ref, a_ref, ext_ref):
        i = pl.program_id(0)
        _fill_ext_past(ext_ref, halo_ref, g_ref[...], i, bt)
        gc = cb_ref[...] + _conv_past(_past_taps(ext_ref, FFN_CONV, bt), cw_ref)
        a_ref[...] = (_gelu(gc) * u_ref[...]).astype(BF16)

    return pl.pallas_call(
        kern, name=name, grid=(T // bt, nb),
        in_specs=[pl.BlockSpec((bt, bc), lambda i, j: (i, j)),
                  pl.BlockSpec((bt, bc), lambda i, j: (i, j + nb)),
                  _halo_prev(bt, bc, nb),
                  pl.BlockSpec((FFN_CONV, bc), lambda i, j: (0, j)),
                  pl.BlockSpec((1, bc), lambda i, j: (0, j))],
        out_specs=pl.BlockSpec((bt, bc), lambda i, j: (i, j)),
        out_shape=jax.ShapeDtypeStruct((T, F), BF16),
        scratch_shapes=[pltpu.VMEM((bt + HALO, bc), F32)],
        compiler_params=_params(("parallel", "parallel")),
    )(up, up, up, cw, cb)


def _ffn_act_bwd(da, up, cw, cb, *, name):
    T, F2 = up.shape
    F = F2 // 2
    bt = _pick(T, ROW_BLOCK, SUBLANE)
    bc = _pick(F, 1408)
    nb = F // bc
    K = FFN_CONV

    def kern(da_ref, u_ref, g_ref, halo_ref, cw_ref, cb_ref, du_ref, dgc_ref, dcb_ref, dcw_ref, ext_ref):
        i = pl.program_id(1)
        _fill_ext_past(ext_ref, halo_ref, g_ref[...], i, bt)
        taps = _past_taps(ext_ref, K, bt)
        gc = cb_ref[...] + _conv_past(taps, cw_ref)
        dav = da_ref[...]
        act, act_grad = _gelu_and_grad(gc)
        du_ref[...] = (dav * act).astype(BF16)
        dgc = dav * u_ref[...] * act_grad
        dgc_ref[...] = dgc.astype(BF16)

        @pl.when(i == 0)
        def _():
            dcb_ref[...] = jnp.zeros_like(dcb_ref)
            dcw_ref[...] = jnp.zeros_like(dcw_ref)

        dcb_ref[...] += jnp.sum(dgc, axis=0, keepdims=True)
        for k in range(K):
            dcw_ref[k:k + 1, :] += jnp.sum(dgc * taps[k], axis=0, keepdims=True)

    blk = pl.BlockSpec((bt, bc), lambda j, i: (i, j))
    return pl.pallas_call(
        kern, name=name, grid=(nb, T // bt),
        in_specs=[blk, blk,
                  pl.BlockSpec((bt, bc), lambda j, i: (i, j + nb)),
                  pl.BlockSpec((HALO, bc), lambda j, i: (jnp.maximum(i * (bt // HALO) - 1, 0), j + nb)),
                  pl.BlockSpec((K, bc), lambda j, i: (0, j)),
                  pl.BlockSpec((1, bc), lambda j, i: (0, j))],
        out_specs=[blk, blk, pl.BlockSpec((1, bc), lambda j, i: (0, j)), pl.BlockSpec((K, bc), lambda j, i: (0, j))],
        out_shape=[jax.ShapeDtypeStruct((T, F2), BF16), jax.ShapeDtypeStruct((T, F), BF16),
                   jax.ShapeDtypeStruct((1, F), F32), jax.ShapeDtypeStruct((K, F), F32)],
        scratch_shapes=[pltpu.VMEM((bt + HALO, bc), F32)],
        compiler_params=_params(("parallel", "arbitrary")),
    )(da, up, up, up, cw, cb)


def _dwconv_bwd_data(dgc, cw, K, into, col, *, name):
    T, C = dgc.shape
    bt = _pick(T, ROW_BLOCK, SUBLANE)
    bc = _pick(C, 1408)
    nt = T // bt
    halo = HALO_BF16 if dgc.dtype == BF16 else HALO
    last_halo = T // halo - 1
    off = col // bc
    assert off * bc == col

    def kern(d_ref, halo_ref, cw_ref, into_ref, o_ref, ext_ref):
        i = pl.program_id(0)
        ext_ref[pl.ds(0, bt), :] = d_ref[...].astype(F32)
        ext_ref[pl.ds(bt, halo), :] = jnp.where(i < nt - 1, halo_ref[...].astype(F32), 0.0)
        ext = ext_ref[...]
        out = None
        for k in range(K):
            ahead = K - 1 - k
            tap = (ext if ahead == 0 else pltpu.roll(ext, bt + halo - ahead, 0))[0:bt]
            term = cw_ref[k:k + 1, :] * tap
            out = term if out is None else out + term
        o_ref[...] = out.astype(o_ref.dtype)

    return pl.pallas_call(
        kern, name=name, grid=(nt, C // bc),
        in_specs=[pl.BlockSpec((bt, bc), lambda i, j: (i, j)),
                  pl.BlockSpec((halo, bc), lambda i, j: (jnp.minimum((i + 1) * (bt // halo), last_halo), j)),
                  pl.BlockSpec((K, bc), lambda i, j: (0, j)), _ANY],
        out_specs=pl.BlockSpec((bt, bc), lambda i, j: (i, j + off)),
        out_shape=jax.ShapeDtypeStruct(into.shape, into.dtype), input_output_aliases={3: 0},
        scratch_shapes=[pltpu.VMEM((bt + halo, bc), F32)],
        compiler_params=_params(("parallel", "parallel")),
    )(dgc, dgc, cw, into)


def _ple_fwd(h, G, bg, pp, *, name):
    T, D = h.shape
    bt = _pick(T, ROW_BLOCK, SUBLANE)

    def kern(h_ref, G_ref, bg_ref, pp_ref, o_ref, ob_ref):
        out = h_ref[...] + _sigmoid(G_ref[...] + bg_ref[...]) * pp_ref[...]
        o_ref[...] = out
        ob_ref[...] = out.astype(BF16)

    row = pl.BlockSpec((bt, D), lambda i: (i, 0))
    vec = pl.BlockSpec((1, D), lambda i: (0, 0))
    return pl.pallas_call(
        kern, name=name, grid=(T // bt,), in_specs=[row, row, vec, row], out_specs=[row, row],
        out_shape=[jax.ShapeDtypeStruct((T, D), F32), jax.ShapeDtypeStruct((T, D), BF16)],
        compiler_params=_params(("parallel",)),
    )(h, G, bg, pp)


def _ple_bwd(dx, G, bg, pp, *, name):
    T, D = dx.shape
    bt = _pick(T, ROW_BLOCK, SUBLANE)

    def kern(dx_ref, G_ref, bg_ref, pp_ref, dG_ref, dpp_ref, dbg_ref):
        i = pl.program_id(0)
        gate = _sigmoid(G_ref[...] + bg_ref[...])
        dxv = dx_ref[...]
        dG = dxv * pp_ref[...] * gate * (1.0 - gate)
        dG_ref[...] = dG.astype(BF16)
        dpp_ref[...] = (dxv * gate).astype(BF16)

        @pl.when(i == 0)
        def _():
            dbg_ref[...] = jnp.zeros_like(dbg_ref)

        dbg_ref[...] += jnp.sum(dG, axis=0, keepdims=True)

    row = pl.BlockSpec((bt, D), lambda i: (i, 0))
    vec = pl.BlockSpec((1, D), lambda i: (0, 0))
    return pl.pallas_call(
        kern, name=name, grid=(T // bt,), in_specs=[row, row, vec, row], out_specs=[row, row, vec],
        out_shape=[jax.ShapeDtypeStruct((T, D), BF16), jax.ShapeDtypeStruct((T, D), BF16),
                   jax.ShapeDtypeStruct((1, D), F32)],
        compiler_params=_params(("arbitrary",)),
    )(dx, G, bg, pp)


def _fox_gate_fwd(proj, bf, *, name):
    T = proj.shape[0]
    bt = _pick(T, CUM_BLOCK, SUBLANE)
    fcol = 3 * D_MODEL // LANE

    def kern(f_ref, bf_ref, c_ref, carry_ref):
        i = pl.program_id(0)

        @pl.when(i == 0)
        def _():
            carry_ref[...] = jnp.zeros_like(carry_ref)

        x = f_ref[...] + bf_ref[...]
        lf = jnp.minimum(x, 0.0) - _log1p(jnp.exp(-jnp.abs(x)))
        cs = _tri_matmul(_tri(bt, True), lf) + carry_ref[...]
        c_ref[...] = cs
        carry_ref[...] = cs[bt - 1:bt, :]

    return pl.pallas_call(
        kern, name=name, grid=(T // bt,),
        in_specs=[pl.BlockSpec((bt, LANE), lambda i: (i, fcol)), pl.BlockSpec((1, LANE), lambda i: (0, 0))],
        out_specs=pl.BlockSpec((bt, LANE), lambda i: (i, 0)),
        out_shape=jax.ShapeDtypeStruct((T, LANE), F32),
        scratch_shapes=[pltpu.VMEM((1, LANE), F32)],
        compiler_params=_params(("arbitrary",)),
    )(proj, bf)


def _fox_gate_bwd(dc, proj, bf, dq, dk, dv, *, name):
    T = proj.shape[0]
    bt = _pick(T, CUM_BLOCK, SUBLANE)
    nb = T // bt
    fcol = 3 * D_MODEL // LANE

    def kern(dc_ref, f_ref, bf_ref, dq_ref, dk_ref, dv_ref, dproj_ref, dbf_ref, carry_ref):
        i = pl.program_id(0)

        @pl.when(i == 0)
        def _():
            carry_ref[...] = jnp.zeros_like(carry_ref)
            dbf_ref[...] = jnp.zeros_like(dbf_ref)

        dlf = _tri_matmul(_tri(bt, False), dc_ref[...]) + carry_ref[...]
        carry_ref[...] = dlf[0:1, :]
        x = f_ref[...] + bf_ref[...]
        lane = lax.broadcasted_iota(jnp.int32, (bt, LANE), 1)
        df = jnp.where(lane < ATTN_HEADS, dlf / (1.0 + jnp.exp(x)), 0.0)
        dbf_ref[...] += jnp.sum(df, axis=0, keepdims=True)
        for n, part_ref in enumerate((dq_ref, dk_ref, dv_ref)):
            dproj_ref[:, n * D_MODEL:(n + 1) * D_MODEL] = part_ref[...].astype(BF16)
        dproj_ref[:, 3 * D_MODEL:] = df.astype(BF16)

    rows = lambda i: (nb - 1 - i, 0)
    wide = pl.BlockSpec((bt, D_MODEL), rows)
    return pl.pallas_call(
        kern, name=name, grid=(nb,),
        in_specs=[pl.BlockSpec((bt, LANE), rows), pl.BlockSpec((bt, LANE), lambda i: (nb - 1 - i, fcol)),
                  pl.BlockSpec((1, LANE), lambda i: (0, 0)), wide, wide, wide],
        out_specs=[pl.BlockSpec((bt, ATTN_IN_PAD), rows), pl.BlockSpec((1, LANE), lambda i: (0, 0))],
        out_shape=[jax.ShapeDtypeStruct((T, ATTN_IN_PAD), BF16), jax.ShapeDtypeStruct((1, LANE), F32)],
        scratch_shapes=[pltpu.VMEM((1, LANE), F32)],
        compiler_params=_params(("arbitrary",)),
    )(dc, proj, bf, dq, dk, dv)


_NT = (((1,), (1,)), ((), ()))
_TN = (((0,), (0,)), ((), ()))


def _dot(a, b, dims=None):
    if dims is None:
        return jnp.dot(a, b, preferred_element_type=F32)
    return lax.dot_general(a, b, dims, preferred_element_type=F32)


LOG2E = 1.0 / math.log(2.0)
LN2 = math.log(2.0)
Q_SCALE = 0.125 * LOG2E
HALF = LANE // 2
L_LANE = (HALF, 0)
FWD_PAIRS = 4
BWD_PAIRS = 1
ATTN_BWD_VMEM_MB = 56


def _attn_prep(proj, *, name):
    T = proj.shape[0]
    bt = _pick(T, ATTN_BLOCK)

    def kern(q_ref, k_ref, v_ref, qa_ref, qb_ref, kk_ref, ka_ref, kb_ref, vv_ref, va_ref, vb_ref):
        lane = lax.broadcasted_iota(jnp.int32, (bt, LANE), 1)
        lo = lane < HALF
        q = q_ref[...] * Q_SCALE
        k = k_ref[...]
        v = v_ref[...]
        qa_ref[...] = jnp.where(lo, q, 0.0).astype(BF16)
        qb_ref[...] = jnp.where(lo, 0.0, q).astype(BF16)
        kk_ref[...] = k.astype(BF16)
        ka_ref[...] = jnp.where(lo, k, 0.0).astype(BF16)
        kb_ref[...] = jnp.where(lo, 0.0, k).astype(BF16)
        vv_ref[...] = v.astype(BF16)
        va_ref[...] = jnp.where(lo, v, jnp.where(lane == L_LANE[0], 1.0, 0.0)).astype(BF16)
        vb_ref[...] = jnp.where(lo, jnp.where(lane == L_LANE[1], 1.0, 0.0), v).astype(BF16)

    kcol, vcol = D_MODEL // LANE, 2 * D_MODEL // LANE
    out = pl.BlockSpec((bt, LANE), lambda i, hp: (i, hp))
    return pl.pallas_call(
        kern, name=name, grid=(T // bt, HEAD_PAIRS),
        in_specs=[out, pl.BlockSpec((bt, LANE), lambda i, hp: (i, kcol + hp)),
                  pl.BlockSpec((bt, LANE), lambda i, hp: (i, vcol + hp))],
        out_specs=[out] * 8, out_shape=[jax.ShapeDtypeStruct((T, D_MODEL), BF16)] * 8,
        compiler_params=_params(("parallel", "parallel")),
    )(proj, proj, proj)


def _attn_fwd(qa, qb, kk, va, vb, cT, *, name):
    T = qa.shape[0]
    tb = _pick(T, ATTN_BLOCK)
    nq = T // tb
    rep = tb // LANE
    width = FWD_PAIRS * LANE

    def kern(qa_ref, qb_ref, k_ref, va_ref, vb_ref, c_ref, o_ref, ob_ref, lsea_ref, lseb_ref, m_ref, acc_ref):
        qi = pl.program_id(1)
        ki = pl.program_id(2)

        @pl.when(ki == 0)
        def _():
            m_ref[...] = jnp.full_like(m_ref, NEG)
            acc_ref[...] = jnp.zeros_like(acc_ref)

        def step(diag):
            for pp in range(FWD_PAIRS):
                cols = slice(pp * LANE, (pp + 1) * LANE)
                k = k_ref[:, cols]
                for h, (q_ref, v_ref) in enumerate(((qa_ref, va_ref), (qb_ref, vb_ref))):
                    i = 2 * pp + h
                    s = _dot(q_ref[:, cols], k, _NT) - c_ref[pp, h:h + 1, :]
                    if diag:
                        r = lax.broadcasted_iota(jnp.int32, (tb, tb), 0)
                        c = lax.broadcasted_iota(jnp.int32, (tb, tb), 1)
                        s = jnp.where(c <= r, s, NEG)
                    m_prev = m_ref[i]
                    m_new = jnp.maximum(m_prev, jnp.max(s, axis=1, keepdims=True))
                    p = jnp.exp2(s - jnp.tile(m_new, (1, rep)))
                    acc_ref[i] = acc_ref[i] * jnp.exp2(m_prev - m_new) + _dot(p.astype(BF16), v_ref[:, cols])
                    m_ref[i] = m_new

        @pl.when(ki < qi)
        def _():
            step(False)

        @pl.when(ki == qi)
        def _():
            step(True)
            lo = lax.broadcasted_iota(jnp.int32, (tb, LANE), 1) < HALF
            for pp in range(FWD_PAIRS):
                cols = slice(pp * LANE, (pp + 1) * LANE)
                a0, a1 = acc_ref[2 * pp], acc_ref[2 * pp + 1]
                l0 = a0[:, L_LANE[0]:L_LANE[0] + 1]
                l1 = a1[:, L_LANE[1]:L_LANE[1] + 1]
                o = jnp.where(lo, a0 / l0, a1 / l1)
                o_ref[:, cols] = o
                ob_ref[:, cols] = o.astype(BF16)
                lsea_ref[:, cols] = m_ref[2 * pp] + jnp.log(l0) * LOG2E
                lseb_ref[:, cols] = m_ref[2 * pp + 1] + jnp.log(l1) * LOG2E

    qspec = pl.BlockSpec((tb, width), lambda g, qi, ki: (qi, g))
    kspec = pl.BlockSpec((tb, width), lambda g, qi, ki: (jnp.minimum(ki, qi), g))
    return pl.pallas_call(
        kern, name=name, grid=(HEAD_PAIRS // FWD_PAIRS, nq, nq),
        in_specs=[qspec, qspec, kspec, kspec, kspec,
                  pl.BlockSpec((FWD_PAIRS, 2, tb), lambda g, qi, ki: (g, 0, jnp.minimum(ki, qi)))],
        out_specs=[qspec, qspec, qspec, qspec],
        out_shape=[jax.ShapeDtypeStruct((T, D_MODEL), F32), jax.ShapeDtypeStruct((T, D_MODEL), BF16),
                   jax.ShapeDtypeStruct((T, D_MODEL), F32), jax.ShapeDtypeStruct((T, D_MODEL), F32)],
        scratch_shapes=[pltpu.VMEM((2 * FWD_PAIRS, tb, LANE), F32), pltpu.VMEM((2 * FWD_PAIRS, tb, LANE), F32)],
        compiler_params=_params(("parallel", "parallel", "arbitrary")),
    )(qa, qb, kk, va, vb, cT)


def _attn_bwd_prep(do, o, *, name):
    T, D = do.shape
    bt = _pick(T, ATTN_BLOCK)

    def kern(do_ref, o_ref, doa_ref, dob_ref, dlta_ref, dltb_ref):
        lo = lax.broadcasted_iota(jnp.int32, (bt, LANE), 1) < HALF
        dov = do_ref[...]
        prod = dov * o_ref[...]
        doa_ref[...] = jnp.where(lo, dov, 0.0).astype(BF16)
        dob_ref[...] = jnp.where(lo, 0.0, dov).astype(BF16)
        dlta_ref[...] = jnp.broadcast_to(jnp.sum(jnp.where(lo, prod, 0.0), axis=1, keepdims=True), (bt, LANE))
        dltb_ref[...] = jnp.broadcast_to(jnp.sum(jnp.where(lo, 0.0, prod), axis=1, keepdims=True), (bt, LANE))

    blk = pl.BlockSpec((bt, LANE), lambda i, hp: (i, hp))
    return pl.pallas_call(
        kern, name=name, grid=(T // bt, HEAD_PAIRS), in_specs=[blk, blk], out_specs=[blk] * 4,
        out_shape=[jax.ShapeDtypeStruct((T, D), BF16)] * 2 + [jax.ShapeDtypeStruct((T, D), F32)] * 2,
        compiler_params=_params(("parallel", "parallel")),
    )(do, o)


def _attn_bwd(qa, qb, kk, ka, kb, vv, doa, dob, lsea, lseb, dlta, dltb, cT, *, name):
    T = qa.shape[0]
    tb = _pick(T, ATTN_BLOCK)
    nq = T // tb
    rep = tb // LANE
    width = BWD_PAIRS * LANE

    def kern(qa_ref, qb_ref, k_ref, ka_ref, kb_ref, v_ref, doa_ref, dob_ref, lsea_ref, lseb_ref, dlta_ref, dltb_ref,
             c_ref, dq_ref, dk_ref, dv_ref, dc_ref, dcq_ref):
        g = pl.program_id(0)
        ki = pl.program_id(1)
        qi = pl.program_id(2)
        first = jnp.logical_and(ki == 0, qi == 0)

        @pl.when(first)
        def _():
            dq_ref[...] = jnp.zeros_like(dq_ref)

        @pl.when(jnp.logical_and(first, g == 0))
        def _():
            dcq_ref[...] = jnp.zeros_like(dcq_ref)

        @pl.when(qi == 0)
        def _():
            dk_ref[...] = jnp.zeros_like(dk_ref)
            dv_ref[...] = jnp.zeros_like(dv_ref)
            dc_ref[...] = jnp.zeros_like(dc_ref)

        def step(diag):
            rows = pl.ds(pl.multiple_of(qi * tb, tb), tb)
            lane = lax.broadcasted_iota(jnp.int32, (tb, LANE), 1)
            row_sums = jnp.zeros((tb, LANE), F32)
            for pp in range(BWD_PAIRS):
                cols = slice(pp * LANE, (pp + 1) * LANE)
                k = k_ref[:, cols]
                v = v_ref[:, cols]
                dq = None
                dk = None
                dv = None
                heads = ((qa_ref, ka_ref, doa_ref, lsea_ref, dlta_ref), (qb_ref, kb_ref, dob_ref, lseb_ref, dltb_ref))
                for h, (q_ref, km_ref, do_ref, lse_ref, dlt_ref) in enumerate(heads):
                    q = q_ref[:, cols]
                    dom = do_ref[:, cols]
                    s = _dot(q, k, _NT) - c_ref[pp, h:h + 1, :]
                    if diag:
                        r = lax.broadcasted_iota(jnp.int32, (tb, tb), 0)
                        c = lax.broadcasted_iota(jnp.int32, (tb, tb), 1)
                        s = jnp.where(c <= r, s, NEG)
                    p = jnp.exp2(s - jnp.tile(lse_ref[:, cols], (1, rep)))
                    ds = p * (_dot(dom, v, _NT) - jnp.tile(dlt_ref[:, cols], (1, rep)))
                    dc_ref[pp, h:h + 1, :] -= jnp.sum(ds, axis=0, keepdims=True)
                    head = 2 * (BWD_PAIRS * g + pp) + h
                    row_sums = jnp.where(lane == head, jnp.sum(ds, axis=1, keepdims=True), row_sums)
                    dsb = ds.astype(BF16)
                    tv = _dot(p.astype(BF16), dom, _TN)
                    tk = _dot(dsb, q, _TN)
                    tq = _dot(dsb, km_ref[:, cols])
                    dv = tv if dv is None else dv + tv
                    dk = tk if dk is None else dk + tk
                    dq = tq if dq is None else dq + tq
                dv_ref[:, cols] += dv
                dk_ref[:, cols] += dk * LN2
                dq_ref[rows, cols] += dq * 0.125
            dcq_ref[rows, :] += row_sums

        @pl.when(qi > ki)
        def _():
            step(False)

        @pl.when(qi == ki)
        def _():
            step(True)

    qspec = pl.BlockSpec((tb, width), lambda g, ki, qi: (jnp.maximum(qi, ki), g))
    kspec = pl.BlockSpec((tb, width), lambda g, ki, qi: (ki, g))
    cspec = pl.BlockSpec((BWD_PAIRS, 2, tb), lambda g, ki, qi: (g, 0, ki))
    qacc = pl.BlockSpec((T, width), lambda g, ki, qi: (0, g), pipeline_mode=pl.Buffered(1))
    cqacc = pl.BlockSpec((T, LANE), lambda g, ki, qi: (0, 0), pipeline_mode=pl.Buffered(1))
    return pl.pallas_call(
        kern, name=name, grid=(HEAD_PAIRS // BWD_PAIRS, nq, nq),
        in_specs=[qspec, qspec, kspec, kspec, kspec, kspec, qspec, qspec, qspec, qspec, qspec, qspec, cspec],
        out_specs=[qacc, kspec, kspec, cspec, cqacc],
        out_shape=[jax.ShapeDtypeStruct((T, D_MODEL), F32)] * 3 + [jax.ShapeDtypeStruct((HEAD_PAIRS, 2, T), F32),
                                                                   jax.ShapeDtypeStruct((T, LANE), F32)],
        compiler_params=_params(("arbitrary", "arbitrary", "arbitrary"), vmem_mb=ATTN_BWD_VMEM_MB),
    )(qa, qb, kk, ka, kb, vv, doa, dob, lsea, lseb, dlta, dltb, cT)


def _ssd_dt_fwd(proj, dt_bias, a_log, *, name):
    T = proj.shape[0]
    Q = SSM_CHUNK
    col = (2 * SSM_D_INNER + 2 * SSM_GROUPS * SSM_STATE) // LANE

    def kern(raw_ref, b_ref, al_ref, dt_ref, ac_ref):
        dt = _softplus(raw_ref[...] + b_ref[...])
        dt_ref[...] = dt
        ac_ref[...] = _tri_matmul(_tri(Q, True), dt * (-jnp.exp(al_ref[...])))

    vec = pl.BlockSpec((1, LANE), lambda i: (0, 0))
    blk = pl.BlockSpec((Q, LANE), lambda i: (i, 0))
    return pl.pallas_call(
        kern, name=name, grid=(T // Q,),
        in_specs=[pl.BlockSpec((Q, LANE), lambda i: (i, col)), vec, vec], out_specs=[blk, blk],
        out_shape=[jax.ShapeDtypeStruct((T, LANE), F32)] * 2,
        compiler_params=_params(("parallel",)),
    )(proj, dt_bias, a_log)


def _ssd_dt_bwd(da_a, da_b, ddt_a, ddt_b, dt, proj, dt_bias, a_log, into, *, name):
    T = proj.shape[0]
    Q = SSM_CHUNK
    col = (2 * SSM_D_INNER + 2 * SSM_GROUPS * SSM_STATE) // LANE

    def kern(daa_ref, dab_ref, dda_ref, ddb_ref, dt_ref, raw_ref, b_ref, al_ref, into_ref, draw_ref, dal_ref, db_ref,
             acc_ref):
        i = pl.program_id(0)

        @pl.when(i == 0)
        def _():
            acc_ref[...] = jnp.zeros_like(acc_ref)
            db_ref[...] = jnp.zeros_like(db_ref)

        A = -jnp.exp(al_ref[...])
        ddA = _tri_matmul(_tri(Q, False), daa_ref[...] + dab_ref[...])
        ddt = dda_ref[...] + ddb_ref[...] + ddA * A
        acc_ref[...] += jnp.sum(ddA * dt_ref[...], axis=0, keepdims=True)
        lane = lax.broadcasted_iota(jnp.int32, (Q, LANE), 1)
        draw = jnp.where(lane < SSM_HEADS, ddt * _sigmoid(raw_ref[...] + b_ref[...]), 0.0)
        draw_ref[...] = draw.astype(BF16)
        db_ref[...] += jnp.sum(draw, axis=0, keepdims=True)
        dal_ref[...] = acc_ref[...] * A

    vec = pl.BlockSpec((1, LANE), lambda i: (0, 0))
    blk = pl.BlockSpec((Q, LANE), lambda i: (i, 0))
    return pl.pallas_call(
        kern, name=name, grid=(T // Q,),
        in_specs=[blk, blk, blk, blk, blk, pl.BlockSpec((Q, LANE), lambda i: (i, col)), vec, vec, _ANY],
        out_specs=[pl.BlockSpec((Q, LANE), lambda i: (i, col)), vec, vec],
        out_shape=[jax.ShapeDtypeStruct(into.shape, into.dtype), jax.ShapeDtypeStruct((1, LANE), F32),
                   jax.ShapeDtypeStruct((1, LANE), F32)],
        input_output_aliases={8: 0},
        scratch_shapes=[pltpu.VMEM((1, LANE), F32)],
        compiler_params=_params(("arbitrary",)),
    )(da_a, da_b, ddt_a, ddt_b, dt, proj, dt_bias, a_log, into)


def _conv_silu_fwd(proj, cw, cb, *, name):
    T = proj.shape[0]
    C = SSM_XBC
    bt = _pick(T, ROW_BLOCK, SUBLANE)
    bc = 1024
    off = SSM_D_INNER // bc

    def kern(x_ref, halo_ref, cw_ref, cb_ref, o_ref, ext_ref):
        i = pl.program_id(0)
        _fill_ext_past(ext_ref, halo_ref, x_ref[...], i, bt)
        pre = cb_ref[...] + _conv_past(_past_taps(ext_ref, SSM_CONV, bt), cw_ref)
        o_ref[...] = pre * _sigmoid(pre)

    return pl.pallas_call(
        kern, name=name, grid=(T // bt, C // bc),
        in_specs=[pl.BlockSpec((bt, bc), lambda i, j: (i, j + off)), _halo_prev(bt, bc, off),
                  pl.BlockSpec((SSM_CONV, bc), lambda i, j: (0, j)), pl.BlockSpec((1, bc), lambda i, j: (0, j))],
        out_specs=pl.BlockSpec((bt, bc), lambda i, j: (i, j)),
        out_shape=jax.ShapeDtypeStruct((T, C), F32),
        scratch_shapes=[pltpu.VMEM((bt + HALO, bc), F32)],
        compiler_params=_params(("parallel", "parallel")),
    )(proj, proj, cw, cb)


def _conv_silu_bwd(dxbc, proj, cw, cb, *, name):
    T = proj.shape[0]
    C = SSM_XBC
    K = SSM_CONV
    bt = _pick(T, ROW_BLOCK, SUBLANE)
    bc = 1024
    off = SSM_D_INNER // bc

    def kern(d_ref, x_ref, halo_ref, cw_ref, cb_ref, dpre_ref, dcb_ref, dcw_ref, ext_ref):
        i = pl.program_id(1)
        _fill_ext_past(ext_ref, halo_ref, x_ref[...], i, bt)
        taps = _past_taps(ext_ref, K, bt)
        pre = cb_ref[...] + _conv_past(taps, cw_ref)
        sg = _sigmoid(pre)
        dpre = d_ref[...] * sg * (1.0 + pre * (1.0 - sg))
        dpre_ref[...] = dpre.astype(BF16)

        @pl.when(i == 0)
        def _():
            dcb_ref[...] = jnp.zeros_like(dcb_ref)
            dcw_ref[...] = jnp.zeros_like(dcw_ref)

        dcb_ref[...] += jnp.sum(dpre, axis=0, keepdims=True)
        for k in range(K):
            dcw_ref[k:k + 1, :] += jnp.sum(dpre * taps[k], axis=0, keepdims=True)

    blk = pl.BlockSpec((bt, bc), lambda j, i: (i, j))
    return pl.pallas_call(
        kern, name=name, grid=(C // bc, T // bt),
        in_specs=[blk, pl.BlockSpec((bt, bc), lambda j, i: (i, j + off)),
                  pl.BlockSpec((HALO, bc), lambda j, i: (jnp.maximum(i * (bt // HALO) - 1, 0), j + off)),
                  pl.BlockSpec((K, bc), lambda j, i: (0, j)), pl.BlockSpec((1, bc), lambda j, i: (0, j))],
        out_specs=[blk, pl.BlockSpec((1, bc), lambda j, i: (0, j)), pl.BlockSpec((K, bc), lambda j, i: (0, j))],
        out_shape=[jax.ShapeDtypeStruct((T, C), BF16), jax.ShapeDtypeStruct((1, C), F32),
                   jax.ShapeDtypeStruct((K, C), F32)],
        scratch_shapes=[pltpu.VMEM((bt + HALO, bc), F32)],
        compiler_params=_params(("parallel", "arbitrary")),
    )(dxbc, proj, proj, cw, cb)


_GP = SSM_D_INNER // SSM_GROUPS
_HPG = SSM_HEADS // SSM_GROUPS
_PH = SSM_D_INNER // SSM_HEADS


def _head_masks(rows):
    lane = lax.broadcasted_iota(jnp.int32, (rows, _GP), 1)
    return [jnp.logical_and(lane >= r * _PH, lane < (r + 1) * _PH) for r in range(_HPG)]


def _ssd_cols(g):
    x0 = g * _GP
    b0 = SSM_D_INNER + g * SSM_STATE
    c0 = SSM_D_INNER + (SSM_GROUPS + g) * SSM_STATE
    return slice(x0, x0 + _GP), slice(b0, b0 + SSM_STATE), slice(c0, c0 + SSM_STATE)


def _ssd_specs(idx):
    Q, N = SSM_CHUNK, SSM_STATE
    return dict(
        xbc=pl.BlockSpec((Q, SSM_XBC), lambda j: (idx(j), 0)),
        x=pl.BlockSpec((Q, SSM_D_INNER), lambda j: (idx(j), 0)),
        col=pl.BlockSpec((Q, LANE), lambda j: (idx(j), 0)),
        row=pl.BlockSpec((SSM_HEADS, Q), lambda j: (0, idx(j))),
        st=pl.BlockSpec((N, SSM_D_INNER), lambda j: (idx(j), 0)),
    )


def _ssd_scan_fwd(xbc, dtc, acc_, dtr, acr, *, name):
    T = xbc.shape[0]
    Q, N = SSM_CHUNK, SSM_STATE
    nc = T // Q
    sp = _ssd_specs(lambda j: j)

    def kern(xbc_ref, dtc_ref, ac_ref, dtr_ref, ar_ref, ys_ref, st_ref, state_ref):
        @pl.when(pl.program_id(0) == 0)
        def _():
            state_ref[...] = jnp.zeros_like(state_ref)

        r_i = lax.broadcasted_iota(jnp.int32, (Q, Q), 0)
        c_i = lax.broadcasted_iota(jnp.int32, (Q, Q), 1)
        tri = c_i <= r_i
        masks = _head_masks(Q)
        masks1 = _head_masks(1)
        for g in range(SSM_GROUPS):
            xs, bs, cs = _ssd_cols(g)
            S = state_ref[g]
            st_ref[:, xs] = S
            x = xbc_ref[:, xs]
            xb = x.astype(BF16)
            Bb = xbc_ref[:, bs].astype(BF16)
            Cb = xbc_ref[:, cs].astype(BF16)
            CB = _dot(Cb, Bb, _NT)
            y = jnp.zeros((Q, _GP), F32)
            El = jnp.zeros((Q, _GP), F32)
            Wl = jnp.zeros((Q, _GP), F32)
            decl = jnp.zeros((1, _GP), F32)
            for r in range(_HPG):
                h = g * _HPG + r
                a_c = ac_ref[:, h:h + 1]
                a_r = ar_ref[h:h + 1, :]
                dt_c = dtc_ref[:, h:h + 1]
                dt_r = dtr_ref[h:h + 1, :]
                L = jnp.exp(jnp.where(tri, a_c - a_r, NEG))
                W = CB * L * dt_r
                y = jnp.where(masks[r], _dot(W.astype(BF16), xb), y)
                a_q = a_c[Q - 1:Q, :]
                El = jnp.where(masks[r], jnp.exp(a_c), El)
                Wl = jnp.where(masks[r], jnp.exp(a_q - a_c) * dt_c, Wl)
                decl = jnp.where(masks1[r], jnp.exp(a_q), decl)
            ys_ref[:, xs] = y + _dot(Cb, S.astype(BF16)) * El
            state_ref[g] = S * decl + _dot(Bb, (x * Wl).astype(BF16), _TN)

    return pl.pallas_call(
        kern, name=name, grid=(nc,),
        in_specs=[sp["xbc"], sp["col"], sp["col"], sp["row"], sp["row"]],
        out_specs=[sp["x"], sp["st"]],
        out_shape=[jax.ShapeDtypeStruct((T, SSM_D_INNER), F32), jax.ShapeDtypeStruct((nc * N, SSM_D_INNER), F32)],
        scratch_shapes=[pltpu.VMEM((SSM_GROUPS, N, _GP), F32)],
        compiler_params=_params(("arbitrary",)),
    )(xbc, dtc, acc_, dtr, acr)


def _ssd_scan_bwd(xbc, dys, dskip, st, dtc, acc_, dtr, acr, *, name):
    T = xbc.shape[0]
    Q, N = SSM_CHUNK, SSM_STATE
    nc = T // Q
    sp = _ssd_specs(lambda j: nc - 1 - j)

    def kern(xbc_ref, dy_ref, dsk_ref, st_ref, dtc_ref, ac_ref, dtr_ref, ar_ref,
             dxbc_ref, dac_ref, dar_ref, ddc_ref, ddr_ref, dstate_ref):
        @pl.when(pl.program_id(0) == 0)
        def _():
            dstate_ref[...] = jnp.zeros_like(dstate_ref)

        r_i = lax.broadcasted_iota(jnp.int32, (Q, Q), 0)
        c_i = lax.broadcasted_iota(jnp.int32, (Q, Q), 1)
        tri = c_i <= r_i
        last_row = lax.broadcasted_iota(jnp.int32, (Q, 1), 0) == Q - 1
        lane128 = lax.broadcasted_iota(jnp.int32, (Q, LANE), 1)
        masks = _head_masks(Q)
        masksN = _head_masks(N)
        masks1 = _head_masks(1)
        zeros = jnp.zeros((Q, _GP), F32)
        dacol = jnp.zeros((Q, LANE), F32)
        ddcol = jnp.zeros((Q, LANE), F32)
        for g in range(SSM_GROUPS):
            xs, bs, cs = _ssd_cols(g)
            dS = dstate_ref[g]
            dSb = dS.astype(BF16)
            S = st_ref[:, xs]
            Sb = S.astype(BF16)
            x = xbc_ref[:, xs]
            xb = x.astype(BF16)
            Bb = xbc_ref[:, bs].astype(BF16)
            Cb = xbc_ref[:, cs].astype(BF16)
            dy = dy_ref[:, xs]
            CB = _dot(Cb, Bb, _NT)
            BdS = _dot(Bb, dSb)
            hx = BdS * x
            yd = _dot(Cb, Sb) * dy
            dSS = dS * S
            dxi, El, Wl = zeros, zeros, zeros
            decl = jnp.zeros((1, _GP), F32)
            dBacc = jnp.zeros((Q, N), F32)
            dCacc = jnp.zeros((Q, N), F32)
            for r in range(_HPG):
                h = g * _HPG + r
                hm = masks[r]
                a_c = ac_ref[:, h:h + 1]
                a_r = ar_ref[h:h + 1, :]
                dt_c = dtc_ref[:, h:h + 1]
                dt_r = dtr_ref[h:h + 1, :]
                L = jnp.exp(jnp.where(tri, a_c - a_r, NEG))
                GL = CB * L
                W = GL * dt_r
                dym = jnp.where(hm, dy, 0.0).astype(BF16)
                dW = _dot(dym, xb, _NT)
                E = dW * W
                da_c = jnp.sum(E, axis=1, keepdims=True)
                dar_ref[h:h + 1, :] = -jnp.sum(E, axis=0, keepdims=True)
                ddr_ref[h:h + 1, :] = jnp.sum(dW * GL, axis=0, keepdims=True)
                dGb = (dW * L * dt_r).astype(BF16)
                dCacc = dCacc + _dot(dGb, Bb)
                dBacc = dBacc + _dot(dGb, Cb, _TN)
                dxi = dxi + _dot(W.astype(BF16), dym, _TN)
                a_q = a_c[Q - 1:Q, :]
                e_c = jnp.exp(a_c)
                eq_c = jnp.exp(a_q - a_c)
                w_c = eq_c * dt_c
                ydr = jnp.sum(jnp.where(hm, yd, 0.0), axis=1, keepdims=True) * e_c
                h_c = jnp.sum(jnp.where(hm, hx, 0.0), axis=1, keepdims=True)
                hw = h_c * w_c
                dss = jnp.sum(jnp.sum(jnp.where(masksN[r], dSS, 0.0), axis=1, keepdims=True), axis=0, keepdims=True)
                s_q = jnp.sum(hw, axis=0, keepdims=True) + jnp.exp(a_q) * dss
                da_c = da_c + ydr - hw + jnp.where(last_row, s_q, 0.0)
                dacol = jnp.where(lane128 == h, da_c, dacol)
                ddcol = jnp.where(lane128 == h, h_c * eq_c, ddcol)
                El = jnp.where(hm, e_c, El)
                Wl = jnp.where(hm, w_c, Wl)
                decl = jnp.where(masks1[r], jnp.exp(a_q), decl)
            dxbc_ref[:, xs] = dxi + BdS * Wl + dsk_ref[:, xs]
            dxbc_ref[:, bs] = dBacc + _dot((x * Wl).astype(BF16), dSb, _NT)
            dyE = (dy * El).astype(BF16)
            dxbc_ref[:, cs] = dCacc + _dot(dyE, Sb, _NT)
            dstate_ref[g] = dS * decl + _dot(Cb, dyE, _TN)
        dac_ref[...] = dacol
        ddc_ref[...] = ddcol

    return pl.pallas_call(
        kern, name=name, grid=(nc,),
        in_specs=[sp["xbc"], sp["x"], sp["x"], sp["st"], sp["col"], sp["col"], sp["row"], sp["row"]],
        out_specs=[sp["xbc"], sp["col"], sp["row"], sp["col"], sp["row"]],
        out_shape=[jax.ShapeDtypeStruct((T, SSM_XBC), F32),
                   jax.ShapeDtypeStruct((T, LANE), F32), jax.ShapeDtypeStruct((SSM_HEADS, T), F32),
                   jax.ShapeDtypeStruct((T, LANE), F32), jax.ShapeDtypeStruct((SSM_HEADS, T), F32)],
        scratch_shapes=[pltpu.VMEM((SSM_GROUPS, N, _GP), F32)],
        compiler_params=_params(("arbitrary",)),
    )(xbc, dys, dskip, st, dtc, acc_, dtr, acr)


def _gate_norm_fwd(ys, xbc, proj, d_exp, norm_w, *, name):
    T = ys.shape[0]
    bt = _pick(T, NARROW_ROW_BLOCK, SUBLANE)

    def kern(ys_ref, x_ref, z_ref, d_ref, w_ref, o_ref):
        z = z_ref[...]
        yz = (ys_ref[...] + d_ref[...] * x_ref[...]) * (z * _sigmoid(z))
        rstd = lax.rsqrt(jnp.mean(yz * yz, axis=-1, keepdims=True) + RMS_EPS)
        o_ref[...] = (yz * rstd * w_ref[...]).astype(BF16)

    blk = pl.BlockSpec((bt, _GP), lambda i, g: (i, g))
    vec = pl.BlockSpec((1, _GP), lambda i, g: (0, g))
    return pl.pallas_call(
        kern, name=name, grid=(T // bt, SSM_GROUPS), in_specs=[blk, blk, blk, vec, vec], out_specs=blk,
        out_shape=jax.ShapeDtypeStruct((T, SSM_D_INNER), BF16), compiler_params=_params(("parallel", "parallel")),
    )(ys, xbc, proj, d_exp, norm_w)


def _gate_norm_bwd(dyn, ys, xbc, proj, d_exp, norm_w, *, name):
    T = ys.shape[0]
    bt = _pick(T, NARROW_ROW_BLOCK, SUBLANE)

    def kern(dyn_ref, ys_ref, x_ref, z_ref, d_ref, w_ref, dz_ref, dys_ref, dsk_ref, dw_ref, dd_ref):
        i = pl.program_id(1)
        z = z_ref[...]
        x = x_ref[...]
        sg = _sigmoid(z)
        sz = z * sg
        y = ys_ref[...] + d_ref[...] * x
        yz = y * sz
        rstd = lax.rsqrt(jnp.mean(yz * yz, axis=-1, keepdims=True) + RMS_EPS)
        yhat = yz * rstd
        dynv = dyn_ref[...]
        gg = dynv * w_ref[...]
        dyz = rstd * (gg - yhat * jnp.mean(gg * yhat, axis=-1, keepdims=True))
        dy = dyz * sz
        dz_ref[...] = (dyz * y * sg * (1.0 + z * (1.0 - sg))).astype(BF16)
        dys_ref[...] = dy
        dsk_ref[...] = dy * d_ref[...]

        @pl.when(i == 0)
        def _():
            dw_ref[...] = jnp.zeros_like(dw_ref)
            dd_ref[...] = jnp.zeros_like(dd_ref)

        dw_ref[...] += jnp.sum(dynv * yhat, axis=0, keepdims=True)
        dd_ref[...] += jnp.sum(dy * x, axis=0, keepdims=True)

    blk = pl.BlockSpec((bt, _GP), lambda g, i: (i, g))
    vec = pl.BlockSpec((1, _GP), lambda g, i: (0, g))
    act = jax.ShapeDtypeStruct((T, SSM_D_INNER), F32)
    par = jax.ShapeDtypeStruct((1, SSM_D_INNER), F32)
    return pl.pallas_call(
        kern, name=name, grid=(SSM_GROUPS, T // bt), in_specs=[blk, blk, blk, blk, vec, vec],
        out_specs=[blk, blk, blk, vec, vec],
        out_shape=[jax.ShapeDtypeStruct((T, SSM_IN_PAD), BF16), act, act, par, par],
        compiler_params=_params(("parallel", "arbitrary")),
    )(dyn, ys, xbc, proj, d_exp, norm_w)


def _loss_head(y, target, *, name):
    T, D = y.shape
    bt = _pick(T, ROW_BLOCK, SUBLANE)

    def kern(y_ref, t_ref, l_ref, dy_ref):
        i = pl.program_id(0)
        err = y_ref[...] - t_ref[...]
        dy_ref[...] = err * (1.0 / D)

        @pl.when(i == 0)
        def _():
            l_ref[...] = jnp.zeros_like(l_ref)

        l_ref[...] += jnp.sum(err * err, axis=0, keepdims=True) * (0.5 / D)

    row = pl.BlockSpec((bt, D), lambda i: (i, 0))
    vec = pl.BlockSpec((1, D), lambda i: (0, 0))
    return pl.pallas_call(
        kern, name=name, grid=(T // bt,), in_specs=[row, row], out_specs=[vec, row],
        out_shape=[jax.ShapeDtypeStruct((1, D), F32), jax.ShapeDtypeStruct((T, D), F32)],
        compiler_params=_params(("arbitrary",)),
    )(y, target)


def _adamw(w, g, m, v, *, name):
    shape = w.shape
    w, g, m, v = (t.reshape(-1, shape[-1]) for t in (w, g, m, v))
    R, C = w.shape
    br = _pick(R, 256, SUBLANE)

    def kern(w_ref, g_ref, m_ref, v_ref, d_ref, nm_ref, nv_ref):
        gv = g_ref[...]
        nm = ADAM_B1 * m_ref[...] + (1.0 - ADAM_B1) * gv
        nv = ADAM_B2 * v_ref[...] + (1.0 - ADAM_B2) * (gv * gv)
        m_hat = nm / (1.0 - ADAM_B1 ** ADAM_STEP)
        v_hat = nv / (1.0 - ADAM_B2 ** ADAM_STEP)
        d_ref[...] = -ADAM_LR * (m_hat / (jnp.sqrt(v_hat) + ADAM_EPS) + ADAM_WD * w_ref[...])
        nm_ref[...] = nm
        nv_ref[...] = nv

    blk = pl.BlockSpec((br, C), lambda i: (i, 0))
    outs = pl.pallas_call(
        kern, name=name, grid=(R // br,), in_specs=[blk] * 4, out_specs=[blk] * 3,
        out_shape=[jax.ShapeDtypeStruct((R, C), F32)] * 3, compiler_params=_params(("parallel",)),
    )(w, g, m, v)
    return [o.reshape(shape) for o in outs]


def _add2(a, b, out_dtype, *, name):
    shape = a.shape
    a2, b2 = a.reshape(-1, shape[-1]), b.reshape(-1, shape[-1])
    R, C = a2.shape
    br = _pick(R, 512, SUBLANE)

    def kern(a_ref, b_ref, o_ref):
        o_ref[...] = (a_ref[...].astype(F32) + b_ref[...].astype(F32)).astype(out_dtype)

    blk = pl.BlockSpec((br, C), lambda i: (i, 0))
    return pl.pallas_call(
        kern, name=name, grid=(R // br,), in_specs=[blk, blk], out_specs=blk,
        out_shape=jax.ShapeDtypeStruct((R, C), out_dtype), compiler_params=_params(("parallel",)),
    )(a2, b2).reshape(shape)


def _sum4(buf, *, name):
    _, R, C = buf.shape
    br = _pick(R, 512, SUBLANE)

    def kern(b_ref, o_ref):
        b = [b_ref[k].astype(F32) for k in range(4)]
        o_ref[...] = ((b[0] + b[1]) + b[2]) + b[3]

    return pl.pallas_call(
        kern, name=name, grid=(R // br,), in_specs=[pl.BlockSpec((4, br, C), lambda i: (0, i, 0))],
        out_specs=pl.BlockSpec((br, C), lambda i: (i, 0)),
        out_shape=jax.ShapeDtypeStruct((R, C), F32), compiler_params=_params(("parallel",)),
    )(buf)


def _place():
    x, y, c = lax.axis_index("x"), lax.axis_index("y"), lax.axis_index("c")
    other_chips = [(1 - x, y), (x, 1 - y), (1 - x, 1 - y)]
    return x, y, c, other_chips


def _gather_chips(w, *, name):
    R, C = w.shape
    H = R // 2

    def body(w_ref, out_ref, send_sems, recv_sems):
        x, y, c, chips = _place()
        me_chip = 2 * x + y
        sib = (x, y, 1 - c)

        def rows(chip, hc):
            return out_ref.at[chip, pl.ds(hc * H, H), :]

        def copy(k, blk, to, src=None):
            return pltpu.make_async_remote_copy(
                src_ref=blk if src is None else src, dst_ref=blk, send_sem=send_sems.at[k], recv_sem=recv_sems.at[k],
                device_id=to, device_id_type=MESH)

        first = [copy(j, rows(me_chip, c), (cx, cy, c), src=w_ref.at[pl.ds(c * H, H), :])
                 for j, (cx, cy) in enumerate(chips)]
        for cp in first:
            cp.start()
        passed = []
        for j, (cx, cy) in enumerate(chips):
            blk = rows(2 * cx + cy, c)
            copy(j, blk, (cx, cy, c)).wait_recv()
            fw = copy(3 + j, blk, sib)
            fw.start()
            passed.append(fw)
        for j, (cx, cy) in enumerate(chips):
            copy(3 + j, rows(2 * cx + cy, 1 - c), sib).wait_recv()
        for cp in first + passed:
            cp.wait_send()

    return pl.pallas_call(
        body, name=name, in_specs=[_ANY], out_specs=_ANY,
        out_shape=jax.ShapeDtypeStruct((4, R, C), w.dtype),
        scratch_shapes=[pltpu.SemaphoreType.DMA((6,)), pltpu.SemaphoreType.DMA((6,))],
    )(w)


def _pair_swap(v, *, name, other_half=False):
    shape = (v.shape[0], v.shape[1] // 2, v.shape[2]) if other_half else v.shape

    def body(v_ref, out_ref, send_sem, recv_sem):
        x, y, c, _ = _place()
        src = v_ref.at[:, pl.ds((1 - c) * shape[1], shape[1]), :] if other_half else v_ref
        cp = pltpu.make_async_remote_copy(src_ref=src, dst_ref=out_ref, send_sem=send_sem, recv_sem=recv_sem,
                                          device_id=(x, y, 1 - c), device_id_type=MESH)
        cp.start()
        cp.wait()

    return pl.pallas_call(
        body, name=name, in_specs=[_ANY], out_specs=_ANY, out_shape=jax.ShapeDtypeStruct(shape, v.dtype),
        scratch_shapes=[pltpu.SemaphoreType.DMA, pltpu.SemaphoreType.DMA],
    )(v)


def _chip_exchange(pv, *, name):
    def body(p_ref, out_ref, send_sems, recv_sems):
        x, y, c, chips = _place()
        me_chip = 2 * x + y
        sends = []
        for j, (cx, cy) in enumerate(chips):
            cp = pltpu.make_async_remote_copy(
                src_ref=p_ref.at[2 * cx + cy], dst_ref=out_ref.at[me_chip], send_sem=send_sems.at[j],
                recv_sem=recv_sems.at[j], device_id=(cx, cy, c), device_id_type=MESH)
            cp.start()
            sends.append(cp)
        for j, (cx, cy) in enumerate(chips):
            blk = out_ref.at[2 * cx + cy]
            pltpu.make_async_remote_copy(src_ref=blk, dst_ref=blk, send_sem=send_sems.at[j], recv_sem=recv_sems.at[j],
                                         device_id=(cx, cy, c), device_id_type=MESH).wait_recv()
        for cp in sends:
            cp.wait_send()

    return pl.pallas_call(
        body, name=name, in_specs=[_ANY], out_specs=_ANY, out_shape=jax.ShapeDtypeStruct(pv.shape, pv.dtype),
        scratch_shapes=[pltpu.SemaphoreType.DMA((3,)), pltpu.SemaphoreType.DMA((3,))],
    )(pv)


WEIGHTS = [
    ("attn_w_in", 2), ("attn_b_f", None), ("attn_w_out", 1), ("ssm_w_in", 2), ("ssm_conv_w", 2), ("ssm_conv_b", 1),
    ("ssm_dt_bias", None), ("ssm_A_log", None), ("ssm_D", None), ("ssm_norm_w", 1), ("ssm_w_out", 1),
    ("ln_mix_g", None), ("ln_mix_b", None), ("ffn_w_up", 2), ("ffn_conv_w", 2), ("ffn_conv_b", None),
    ("ffn_w_down", 1), ("ln_ffn_g", None), ("ln_ffn_b", None), ("ple_w_proj", 2), ("ple_w_gate", 1),
    ("ple_b_gate", None),
]
N_CHIPS = 4
MATMUL_WEIGHTS = ("attn_w_in", "attn_w_out", "ssm_w_in", "ssm_w_out", "ffn_w_up", "ffn_w_down", "ple_w_proj",
                  "ple_w_gate")


def _pack(arrays):
    parts = []
    total = 0
    for a in arrays:
        n = a.size
        pad = (-n) % PACK_COLS
        flat = a.reshape(-1)
        parts.append(jnp.pad(flat, (0, pad)) if pad else flat)
        total += n + pad
    rows = total // PACK_COLS
    rpad = (-rows) % PACK_ROW_ALIGN
    if rpad:
        parts.append(jnp.zeros((rpad * PACK_COLS,), arrays[0].dtype))
    return jnp.concatenate(parts).reshape(rows + rpad, PACK_COLS)


def _unpack(buf, shapes):
    flat = buf.reshape(-1)
    out = []
    off = 0
    for s in shapes:
        n = math.prod(s)
        out.append(flat[off:off + n].reshape(s))
        off += n + ((-n) % PACK_COLS)
    return out


def _from_row_layout(a):
    return jnp.pad(a.T, ((0, 0), (0, LANE - SSM_HEADS)))


def _pad_lanes(v, n=LANE):
    return jnp.pad(v, (0, n - v.shape[0])).reshape(1, n)


def _local_step(x, p, target, W):
    T = x.shape[0]
    row = lambda v: v.reshape(1, -1)
    attn_in = jnp.pad(W["attn_w_in"][0], ((0, 0), (0, ATTN_IN_PAD - W["attn_w_in"].shape[2])))
    ssm_in = jnp.pad(W["ssm_w_in"][0], ((0, 0), (0, SSM_IN_PAD - W["ssm_w_in"].shape[2])))
    bf = _pad_lanes(W["attn_b_f"][0])
    dt_bias = _pad_lanes(W["ssm_dt_bias"][0])
    a_log = _pad_lanes(W["ssm_A_log"][0])
    d_exp = jnp.repeat(W["ssm_D"][0], _PH).reshape(1, SSM_D_INNER)
    norm_w = row(W["ssm_norm_w"][0])
    G = {}

    def ffn_ple_fwd(i, xin, mix, tag):
        s = {}
        s["z1"], s["h1"], s["h1b"] = _ln_fwd(xin, mix, row(W["ln_mix_g"][i]), row(W["ln_mix_b"][i]),
                                             name=f"ln_mix_fwd{tag}")
        s["up"] = _mm(s["h1b"], W["ffn_w_up"][i], name=f"ffn_up{tag}")
        s["a"] = _ffn_act_fwd(s["up"], W["ffn_conv_w"][i], row(W["ffn_conv_b"][i]), name=f"ffn_act_fwd{tag}")
        ffn = _mm(s["a"], W["ffn_w_down"][i], name=f"ffn_down{tag}")
        s["z2"], s["h2"], s["h2b"] = _ln_fwd(s["h1"], ffn, row(W["ln_ffn_g"][i]), row(W["ln_ffn_b"][i]),
                                             name=f"ln_ffn_fwd{tag}")
        s["G"] = _mm(s["h2b"], W["ple_w_gate"][i], name=f"ple_gate_mm{tag}")
        s["pp"] = _mm(pb[i], W["ple_w_proj"][i], name=f"ple_proj_mm{tag}")
        out, outb = _ple_fwd(s["h2"], s["G"], row(W["ple_b_gate"][i]), s["pp"], name=f"ple_fwd{tag}")
        return out, outb, s

    def ffn_ple_bwd(i, dx, s, tag):
        g = {}
        dG, dpp, g["ple_b_gate"] = _ple_bwd(dx, s["G"], row(W["ple_b_gate"][i]), s["pp"], name=f"ple_bwd{tag}")
        g["ple_w_gate"] = _mm(s["h2b"], dG, ta=True, name=f"ple_gate_dw{tag}")
        g["ple_w_proj"] = _mm(pb[i], dpp, ta=True, name=f"ple_proj_dw{tag}")
        dh2 = _mm(dG, W["ple_w_gate"][i], tb=True, add=dx, name=f"ple_gate_dx{tag}")
        dz2, dz2b, g["ln_ffn_g"], g["ln_ffn_b"] = _ln_bwd(dh2, s["z2"], row(W["ln_ffn_g"][i]), name=f"ln_ffn_bwd{tag}")
        da = _mm(dz2b, W["ffn_w_down"][i], tb=True, out_dtype=BF16, name=f"ffn_down_dx{tag}")
        g["ffn_w_down"] = _mm(s["a"], dz2b, ta=True, name=f"ffn_down_dw{tag}")
        dup, dgc, g["ffn_conv_b"], g["ffn_conv_w"] = _ffn_act_bwd(
            da, s["up"], W["ffn_conv_w"][i], row(W["ffn_conv_b"][i]), name=f"ffn_act_bwd{tag}")
        dup = _dwconv_bwd_data(dgc, W["ffn_conv_w"][i], FFN_CONV, dup, FFN_DIM, name=f"ffn_conv_bwd{tag}")
        g["ffn_w_up"] = _mm(s["h1b"], dup, ta=True, name=f"ffn_up_dw{tag}")
        dh1 = _mm(dup, W["ffn_w_up"][i], tb=True, add=dz2, add_scale=DEEPNORM_ALPHA, name=f"ffn_up_dx{tag}")
        dz1, dz1b, g["ln_mix_g"], g["ln_mix_b"] = _ln_bwd(dh1, s["z1"], row(W["ln_mix_g"][i]), name=f"ln_mix_bwd{tag}")
        return dz1, dz1b, g

    xb = x.astype(BF16)
    pb = p.astype(BF16)
    proj0 = _mm(xb, attn_in, name="attn_in")
    c_col = _fox_gate_fwd(proj0, bf, name="fox_gate_fwd")
    cT = (c_col[:, :ATTN_HEADS] * LOG2E).T.reshape(HEAD_PAIRS, 2, T)
    qa, qb, kk, ka, kb, vv, va, vb = _attn_prep(proj0, name="attn_prep")
    o, ob, lsea, lseb = _attn_fwd(qa, qb, kk, va, vb, cT, name="attn_fwd")
    mix0 = _mm(ob, W["attn_w_out"][0], name="attn_out")
    x1, x1b, s0 = ffn_ple_fwd(0, x, mix0, "0")

    proj1 = _mm(x1b, ssm_in, name="ssm_in")
    dt, acum = _ssd_dt_fwd(proj1, dt_bias, a_log, name="ssd_dt_fwd")
    xbc = _conv_silu_fwd(proj1, W["ssm_conv_w"][0], row(W["ssm_conv_b"][0]), name="ssd_conv_fwd")
    dtr, acr = dt[:, :SSM_HEADS].T, acum[:, :SSM_HEADS].T
    ys, states = _ssd_scan_fwd(xbc, dt, acum, dtr, acr, name="ssd_scan_fwd")
    yn = _gate_norm_fwd(ys, xbc, proj1, d_exp, norm_w, name="ssd_gate_norm_fwd")
    mix1 = _mm(yn, W["ssm_w_out"][0], name="ssm_out")
    x2, _, s1 = ffn_ple_fwd(1, x1, mix1, "1")

    lpart, dy = _loss_head(x2, target, name="loss_head")
    loss = jnp.sum(lpart)

    dz1, dz1b, g1 = ffn_ple_bwd(1, dy, s1, "1")
    G["ssm_w_out"] = _mm(yn, dz1b, ta=True, name="ssm_out_dw")[None]
    dyn = _mm(dz1b, W["ssm_w_out"][0], tb=True, name="ssm_out_dx")
    dproj1, dys, dskip, dnw, dde = _gate_norm_bwd(dyn, ys, xbc, proj1, d_exp, norm_w, name="ssd_gate_norm_bwd")
    G["ssm_norm_w"] = dnw
    G["ssm_D"] = dde.reshape(SSM_HEADS, _PH).sum(axis=1)[None]
    dxbc, dac, dar, ddc, ddr = _ssd_scan_bwd(xbc, dys, dskip, states, dt, acum, dtr, acr, name="ssd_scan_bwd")
    dproj1, dal, ddb = _ssd_dt_bwd(dac, _from_row_layout(dar), ddc, _from_row_layout(ddr), dt, proj1, dt_bias, a_log,
                                   dproj1, name="ssd_dt_bwd")
    G["ssm_A_log"] = dal[:, :SSM_HEADS]
    G["ssm_dt_bias"] = ddb[:, :SSM_HEADS]
    dpre, G["ssm_conv_b"], dcw = _conv_silu_bwd(dxbc, proj1, W["ssm_conv_w"][0], row(W["ssm_conv_b"][0]),
                                                name="ssd_conv_bwd")
    G["ssm_conv_w"] = dcw[None]
    dproj1 = _dwconv_bwd_data(dpre, W["ssm_conv_w"][0], SSM_CONV, dproj1, SSM_D_INNER, name="ssd_conv_bwd_data")
    G["ssm_w_in"] = _mm(x1b, dproj1, ta=True, name="ssm_in_dw")[None, :, :W["ssm_w_in"].shape[2]]
    dx1 = _mm(dproj1, ssm_in, tb=True, add=dz1, add_scale=DEEPNORM_ALPHA, name="ssm_in_dx")

    dz0, dz0b, g0 = ffn_ple_bwd(0, dx1, s0, "0")
    G["attn_w_out"] = _mm(ob, dz0b, ta=True, name="attn_out_dw")[None]
    do = _mm(dz0b, W["attn_w_out"][0], tb=True, name="attn_out_dx")
    doa, dob, dlta, dltb = _attn_bwd_prep(do, o, name="attn_bwd_prep")
    dq, dk, dv, dcT, dcq = _attn_bwd(qa, qb, kk, ka, kb, vv, doa, dob, lsea, lseb, dlta, dltb, cT, name="attn_bwd")
    dc_col = jnp.pad(dcT.reshape(ATTN_HEADS, T).T + dcq[:, :ATTN_HEADS], ((0, 0), (0, LANE - ATTN_HEADS)))
    dproj0, dbf = _fox_gate_bwd(dc_col, proj0, bf, dq, dk, dv, name="fox_gate_bwd")
    G["attn_b_f"] = dbf[:, :ATTN_HEADS]
    G["attn_w_in"] = _mm(xb, dproj0, ta=True, name="attn_in_dw")[None, :, :W["attn_w_in"].shape[2]]
    grad_x = _mm(dproj0, attn_in, tb=True, add=dz0, add_scale=DEEPNORM_ALPHA, name="attn_in_dx")

    for k in g0:
        G[k] = jnp.stack([g0[k].reshape(W[k].shape[1:]), g1[k].reshape(W[k].shape[1:])])
    return loss, grad_x, G


def kernel(x, p, attn_w_in, attn_b_f, attn_w_out, ssm_w_in, ssm_conv_w, ssm_conv_b, ssm_dt_bias, ssm_A_log, ssm_D, ssm_norm_w, ssm_w_out, ln_mix_g, ln_mix_b, ffn_w_up, ffn_conv_w, ffn_conv_b, ffn_w_down, ln_ffn_g, ln_ffn_b, ple_w_proj, ple_w_gate, ple_b_gate, loss_target, m_attn_w_in, m_attn_b_f, m_attn_w_out, m_ssm_w_in, m_ssm_conv_w, m_ssm_conv_b, m_ssm_dt_bias, m_ssm_A_log, m_ssm_D, m_ssm_norm_w, m_ssm_w_out, m_ln_mix_g, m_ln_mix_b, m_ffn_w_up, m_ffn_conv_w, m_ffn_conv_b, m_ffn_w_down, m_ln_ffn_g, m_ln_ffn_b, m_ple_w_proj, m_ple_w_gate, m_ple_b_gate, v_attn_w_in, v_attn_b_f, v_attn_w_out, v_ssm_w_in, v_ssm_conv_w, v_ssm_conv_b, v_ssm_dt_bias, v_ssm_A_log, v_ssm_D, v_ssm_norm_w, v_ssm_w_out, v_ln_mix_g, v_ln_mix_b, v_ffn_w_up, v_ffn_conv_w, v_ffn_conv_b, v_ffn_w_down, v_ln_ffn_g, v_ln_ffn_b, v_ple_w_proj, v_ple_w_gate, v_ple_b_gate):
    names = [n for n, _ in WEIGHTS]
    axes = dict(WEIGHTS)
    w_loc = dict(zip(names, [attn_w_in, attn_b_f, attn_w_out, ssm_w_in, ssm_conv_w, ssm_conv_b, ssm_dt_bias, ssm_A_log, ssm_D, ssm_norm_w, ssm_w_out, ln_mix_g, ln_mix_b, ffn_w_up, ffn_conv_w, ffn_conv_b, ffn_w_down, ln_ffn_g, ln_ffn_b, ple_w_proj, ple_w_gate, ple_b_gate]))
    m_loc = dict(zip(names, [m_attn_w_in, m_attn_b_f, m_attn_w_out, m_ssm_w_in, m_ssm_conv_w, m_ssm_conv_b, m_ssm_dt_bias, m_ssm_A_log, m_ssm_D, m_ssm_norm_w, m_ssm_w_out, m_ln_mix_g, m_ln_mix_b, m_ffn_w_up, m_ffn_conv_w, m_ffn_conv_b, m_ffn_w_down, m_ln_ffn_g, m_ln_ffn_b, m_ple_w_proj, m_ple_w_gate, m_ple_b_gate]))
    v_loc = dict(zip(names, [v_attn_w_in, v_attn_b_f, v_attn_w_out, v_ssm_w_in, v_ssm_conv_w, v_ssm_conv_b, v_ssm_dt_bias, v_ssm_A_log, v_ssm_D, v_ssm_norm_w, v_ssm_w_out, v_ln_mix_g, v_ln_mix_b, v_ffn_w_up, v_ffn_conv_w, v_ffn_conv_b, v_ffn_w_down, v_ln_ffn_g, v_ln_ffn_b, v_ple_w_proj, v_ple_w_gate, v_ple_b_gate]))
    sharded = [n for n in names if axes[n] is not None]
    matrices = [n for n in sharded if n in MATMUL_WEIGHTS]

    def wire(n):
        if n in matrices:
            return w_loc[n].astype(BF16)
        return lax.bitcast_convert_type(w_loc[n], BF16)

    wired = [wire(n) for n in sharded]
    me_chip = 2 * lax.axis_index("x") + lax.axis_index("y")
    packed = _pack(wired)
    gathered = lax.dynamic_update_index_in_dim(_gather_chips(packed, name="gather_weights"), packed, me_chip, 0)
    W = dict(w_loc)
    per_chip = [_unpack(gathered[k], [w.shape for w in wired]) for k in range(N_CHIPS)]
    for i, n in enumerate(sharded):
        pieces = [per_chip[k][i] for k in range(N_CHIPS)]
        if n not in matrices:
            pieces = [lax.bitcast_convert_type(q, F32) for q in pieces]
        W[n] = jnp.concatenate(pieces, axis=axes[n])

    loss, grad_x, G = _local_step(x[0], p[:, 0], loss_target[0], W)
    loss = lax.psum(loss, ("x", "y", "c"))

    def slot(k):
        parts = []
        for n in names:
            g = G[n].reshape(W[n].shape)
            if axes[n] is not None:
                size = w_loc[n].shape[axes[n]]
                g = lax.slice_in_dim(g, k * size, (k + 1) * size, axis=axes[n])
            parts.append(g.astype(BF16))
        return _pack(parts)

    contrib = jnp.stack([slot(k) for k in range(N_CHIPS)])
    R = contrib.shape[1]
    H = R // 2
    c = lax.axis_index("c")
    keep = lax.dynamic_slice_in_dim(contrib, c * H, H, axis=1)
    pair = _add2(keep, _pair_swap(contrib, other_half=True, name="grad_pair_swap"), BF16, name="grad_pair_sum")
    from_chips = lax.dynamic_update_index_in_dim(
        _chip_exchange(pair, name="grad_chip_exchange"), lax.dynamic_index_in_dim(pair, me_chip, 0, keepdims=False),
        me_chip, 0)
    half = _sum4(from_chips, name="grad_chip_sum")
    other = _pair_swap(half, name="grad_half_swap")
    gflat = jnp.concatenate([jnp.where(c == 0, half, other), jnp.where(c == 0, other, half)])

    grads = _unpack(gflat, [w_loc[n].shape for n in names])
    steps = [_adamw(w_loc[n], g, m_loc[n], v_loc[n], name=f"adamw_{n}") for n, g in zip(names, grads)]
    return (loss, grad_x[None], *grads, *[s[0] for s in steps], *[s[1] for s in steps], *[s[2] for s in steps])
```

```python
import math

import jax
import jax.numpy as jnp
from jax import lax
from jax.experimental import pallas as pl
from jax.experimental.pallas import tpu as pltpu

F32 = jnp.float32
BF16 = jnp.bfloat16
MESH = pl.DeviceIdType.MESH

D_MODEL = 1024
ATTN_HEADS = 16
HEAD_PAIRS = ATTN_HEADS // 2
SSM_D_INNER = 2048
SSM_HEADS = 32
SSM_GROUPS = 8
SSM_STATE = 128
SSM_CONV = 4
SSM_CHUNK = 128
SSM_XBC = SSM_D_INNER + 2 * SSM_GROUPS * SSM_STATE
FFN_DIM = 2816
FFN_CONV = 3
DEPTH = 2
LN_EPS = 1e-5
RMS_EPS = 1e-5
DEEPNORM_ALPHA = (2 * DEPTH) ** 0.25
ADAM_LR = 0.001
ADAM_B1 = 0.9
ADAM_B2 = 0.999
ADAM_EPS = 1e-08
ADAM_WD = 0.01
ADAM_STEP = 10

LANE = 128
SUBLANE = 8
HALO = SUBLANE
HALO_BF16 = 2 * SUBLANE
NEG = -1e30
ATTN_IN_PAD = 3 * D_MODEL + LANE
SSM_IN_PAD = 2 * SSM_D_INNER + 2 * SSM_GROUPS * SSM_STATE + LANE
PACK_COLS = 1024
PACK_ROW_ALIGN = 512

ATTN_BLOCK = 1024
ROW_BLOCK = 512
NARROW_ROW_BLOCK = 2048
CUM_BLOCK = 512


def _params(sem, vmem_mb=48):
    return pltpu.CompilerParams(dimension_semantics=sem, vmem_limit_bytes=vmem_mb * 2 ** 20)


def _pick(n, target, mult=LANE):
    best = None
    d = mult
    while d <= min(n, target):
        if n % d == 0:
            best = d
        d += mult
    return n if best is None else best


def _sigmoid(x):
    return 1.0 / (1.0 + jnp.exp(-x))


def _log1p(u):
    w = 1.0 + u
    return jnp.where(w == 1.0, u, jnp.log(w) * (u / (w - 1.0)))


def _softplus(x):
    return jnp.maximum(x, 0.0) + _log1p(jnp.exp(-jnp.abs(x)))


def _split3(x):
    hi = x.astype(BF16)
    r1 = x - hi.astype(F32)
    mid = r1.astype(BF16)
    lo = (r1 - mid.astype(F32)).astype(BF16)
    return hi, mid, lo


def _tri_matmul(tri, x):
    out = None
    for part in _split3(x):
        t = jnp.dot(tri, part, preferred_element_type=F32)
        out = t if out is None else out + t
    return out


def _tri(n, lower):
    r = lax.broadcasted_iota(jnp.int32, (n, n), 0)
    c = lax.broadcasted_iota(jnp.int32, (n, n), 1)
    return jnp.where((c <= r) if lower else (c >= r), 1.0, 0.0).astype(BF16)


_ANY = pl.BlockSpec(memory_space=pl.ANY)
MM_OUT_BLOCK_BYTES = 13 * 2 ** 20
MM_IN_BLOCK_BYTES = 6 * 2 ** 20
MM_VMEM_LIMIT_MB = 56
MM_VMEM_BUDGET = 44 * 2 ** 20
MM_WIDE_K = 3200


def _mm(a, b, *, name, ta=False, tb=False, add=None, add_scale=1.0, out_dtype=F32):
    if ta:
        K, M = a.shape
    else:
        M, K = a.shape
    if tb:
        N, Kb = b.shape
    else:
        Kb, N = b.shape
    assert K == Kb, (a.shape, b.shape, ta, tb)
    if ta:
        assert add is None and out_dtype == F32
        bm = _pick(M, 2816)
        bn = _pick(N, MM_OUT_BLOCK_BYTES // (4 * bm))
        bk = _pick(K, max(512, MM_IN_BLOCK_BYTES // (2 * max(bm, bn))))
    else:
        bn = _pick(N, 1536 if K <= MM_WIDE_K else 512)
        bk = K
        bm = _pick(M, 2048)
        if 4 * bm * K + 4 * K * bn + (8 if add is None else 16) * bm * bn > MM_VMEM_BUDGET:
            bm = _pick(M, 1024)
    nk = K // bk
    a_spec = pl.BlockSpec((bk, bm), lambda i, j, k: (k, i)) if ta else pl.BlockSpec((bm, bk), lambda i, j, k: (i, k))
    b_spec = pl.BlockSpec((bn, bk), lambda i, j, k: (j, k)) if tb else pl.BlockSpec((bk, bn), lambda i, j, k: (k, j))
    o_spec = pl.BlockSpec((bm, bn), lambda i, j, k: (i, j))
    dims = (((0 if ta else 1,), (1 if tb else 0,)), ((), ()))
    has_add = add is not None

    def kern(*refs):
        a_ref, b_ref = refs[0], refs[1]
        add_ref = refs[2] if has_add else None
        o_ref = refs[3] if has_add else refs[2]
        k = pl.program_id(2)
        part = lax.dot_general(a_ref[...].astype(BF16), b_ref[...].astype(BF16), dims, preferred_element_type=F32)
        if nk == 1:
            o_ref[...] = (part + add_scale * add_ref[...] if has_add else part).astype(out_dtype)
        else:
            @pl.when(k == 0)
            def _():
                o_ref[...] = part

            @pl.when(k > 0)
            def _():
                o_ref[...] += part

    ins = [a, b] + ([add] if has_add else [])
    in_specs = [a_spec, b_spec] + ([o_spec] if has_add else [])
    return pl.pallas_call(
        kern, name=name, grid=(M // bm, N // bn, nk),
        in_specs=in_specs, out_specs=o_spec,
        out_shape=jax.ShapeDtypeStruct((M, N), out_dtype),
        compiler_params=_params(("parallel", "parallel", "arbitrary"), vmem_mb=MM_VMEM_LIMIT_MB),
    )(*ins)


def _ln_stats(z):
    mu = jnp.mean(z, axis=-1, keepdims=True)
    zc = z - mu
    var = jnp.mean(zc * zc, axis=-1, keepdims=True)
    return zc, lax.rsqrt(var + LN_EPS)


def _ln_fwd(x, r, g, b, *, name):
    T, D = x.shape
    bt = _pick(T, ROW_BLOCK, SUBLANE)

    def kern(x_ref, r_ref, g_ref, b_ref, z_ref, h_ref, hb_ref):
        z = DEEPNORM_ALPHA * x_ref[...] + r_ref[...]
        zc, rstd = _ln_stats(z)
        h = zc * rstd * g_ref[...] + b_ref[...]
        z_ref[...] = z
        h_ref[...] = h
        hb_ref[...] = h.astype(BF16)

    row = pl.BlockSpec((bt, D), lambda i: (i, 0))
    vec = pl.BlockSpec((1, D), lambda i: (0, 0))
    return pl.pallas_call(
        kern, name=name, grid=(T // bt,), in_specs=[row, row, vec, vec], out_specs=[row, row, row],
        out_shape=[jax.ShapeDtypeStruct((T, D), F32)] * 2 + [jax.ShapeDtypeStruct((T, D), BF16)],
        compiler_params=_params(("parallel",)),
    )(x, r, g, b)


def _ln_bwd(dy, z, g, *, name):
    T, D = z.shape
    bt = _pick(T, ROW_BLOCK, SUBLANE)

    def kern(dy_ref, z_ref, g_ref, dz_ref, dzb_ref, dg_ref, db_ref):
        i = pl.program_id(0)
        zc, rstd = _ln_stats(z_ref[...])
        xhat = zc * rstd
        dyv = dy_ref[...]
        dxh = dyv * g_ref[...]
        m1 = jnp.mean(dxh, axis=-1, keepdims=True)
        m2 = jnp.mean(dxh * xhat, axis=-1, keepdims=True)
        dz = rstd * (dxh - m1 - xhat * m2)
        dz_ref[...] = dz
        dzb_ref[...] = dz.astype(BF16)

        @pl.when(i == 0)
        def _():
            dg_ref[...] = jnp.zeros_like(dg_ref)
            db_ref[...] = jnp.zeros_like(db_ref)

        dg_ref[...] += jnp.sum(dyv * xhat, axis=0, keepdims=True)
        db_ref[...] += jnp.sum(dyv, axis=0, keepdims=True)

    row = pl.BlockSpec((bt, D), lambda i: (i, 0))
    vec = pl.BlockSpec((1, D), lambda i: (0, 0))
    return pl.pallas_call(
        kern, name=name, grid=(T // bt,), in_specs=[row, row, vec], out_specs=[row, row, vec, vec],
        out_shape=[jax.ShapeDtypeStruct((T, D), F32), jax.ShapeDtypeStruct((T, D), BF16),
                   jax.ShapeDtypeStruct((1, D), F32), jax.ShapeDtypeStruct((1, D), F32)],
        compiler_params=_params(("arbitrary",)),
    )(dy, z, g)


def _past_taps(ext_ref, K, bt):
    ext = ext_ref[...]
    return [(ext if k == K - 1 else pltpu.roll(ext, K - 1 - k, 0))[HALO:HALO + bt] for k in range(K)]


def _conv_past(taps, cw_ref):
    out = None
    for k, tap in enumerate(taps):
        term = cw_ref[k:k + 1, :] * tap
        out = term if out is None else out + term
    return out


def _fill_ext_past(ext_ref, halo_ref, cur, i, bt):
    ext_ref[pl.ds(0, HALO), :] = jnp.where(i > 0, halo_ref[...], 0.0)
    ext_ref[pl.ds(HALO, bt), :] = cur


def _halo_prev(bt, bc, off):
    return pl.BlockSpec((HALO, bc), lambda i, j: (jnp.maximum(i * (bt // HALO) - 1, 0), j + off))


def _normal_cdf(x):
    return 0.5 * (1.0 + lax.erf(x * (1.0 / math.sqrt(2.0))))


def _gelu(x):
    return x * _normal_cdf(x)


def _gelu_and_grad(x):
    cdf = _normal_cdf(x)
    return x * cdf, cdf + x * jnp.exp(-0.5 * x * x) * (1.0 / math.sqrt(2.0 * math.pi))


def _ffn_act_fwd(up, cw, cb, *, name):
    T, F2 = up.shape
    F = F2 // 2
    bt = _pick(T, ROW_BLOCK, SUBLANE)
    bc = _pick(F, 1408)
    nb = F // bc

    def kern(u_ref, g_ref, halo_ref, cw_ref, cb_ref, a_ref, ext_ref):
        i = pl.program_id(0)
        _fill_ext_past(ext_ref, halo_ref, g_ref[...], i, bt)
        gc = cb_ref[...] + _conv_past(_past_taps(ext_ref, FFN_CONV, bt), cw_ref)
        a_ref[...] = (_gelu(gc) * u_ref[...]).astype(BF16)

    return pl.pallas_call(
        kern, name=name, grid=(T // bt, nb),
        in_specs=[pl.BlockSpec((bt, bc), lambda i, j: (i, j)),
                  pl.BlockSpec((bt, bc), lambda i, j: (i, j + nb)),
                  _halo_prev(bt, bc, nb),
                  pl.BlockSpec((FFN_CONV, bc), lambda i, j: (0, j)),
                  pl.BlockSpec((1, bc), lambda i, j: (0, j))],
        out_specs=pl.BlockSpec((bt, bc), lambda i, j: (i, j)),
        out_shape=jax.ShapeDtypeStruct((T, F), BF16),
        scratch_shapes=[pltpu.VMEM((bt + HALO, bc), F32)],
        compiler_params=_params(("parallel", "parallel")),
    )(up, up, up, cw, cb)


def _ffn_act_bwd(da, up, cw, cb, *, name):
    T, F2 = up.shape
    F = F2 // 2
    bt = _pick(T, ROW_BLOCK, SUBLANE)
    bc = _pick(F, 1408)
    nb = F // bc
    K = FFN_CONV

    def kern(da_ref, u_ref, g_ref, halo_ref, cw_ref, cb_ref, du_ref, dgc_ref, dcb_ref, dcw_ref, ext_ref):
        i = pl.program_id(1)
        _fill_ext_past(ext_ref, halo_ref, g_ref[...], i, bt)
        taps = _past_taps(ext_ref, K, bt)
        gc = cb_ref[...] + _conv_past(taps, cw_ref)
        dav = da_ref[...]
        act, act_grad = _gelu_and_grad(gc)
        du_ref[...] = (dav * act).astype(BF16)
        dgc = dav * u_ref[...] * act_grad
        dgc_ref[...] = dgc.astype(BF16)

        @pl.when(i == 0)
        def _():
            dcb_ref[...] = jnp.zeros_like(dcb_ref)
            dcw_ref[...] = jnp.zeros_like(dcw_ref)

        dcb_ref[...] += jnp.sum(dgc, axis=0, keepdims=True)
        for k in range(K):
            dcw_ref[k:k + 1, :] += jnp.sum(dgc * taps[k], axis=0, keepdims=True)

    blk = pl.BlockSpec((bt, bc), lambda j, i: (i, j))
    return pl.pallas_call(
        kern, name=name, grid=(nb, T // bt),
        in_specs=[blk, blk,
                  pl.BlockSpec((bt, bc), lambda j, i: (i, j + nb)),
                  pl.BlockSpec((HALO, bc), lambda j, i: (jnp.maximum(i * (bt // HALO) - 1, 0), j + nb)),
                  pl.BlockSpec((K, bc), lambda j, i: (0, j)),
                  pl.BlockSpec((1, bc), lambda j, i: (0, j))],
        out_specs=[blk, blk, pl.BlockSpec((1, bc), lambda j, i: (0, j)), pl.BlockSpec((K, bc), lambda j, i: (0, j))],
        out_shape=[jax.ShapeDtypeStruct((T, F2), BF16), jax.ShapeDtypeStruct((T, F), BF16),
                   jax.ShapeDtypeStruct((1, F), F32), jax.ShapeDtypeStruct((K, F), F32)],
        scratch_shapes=[pltpu.VMEM((bt + HALO, bc), F32)],
        compiler_params=_params(("parallel", "arbitrary")),
    )(da, up, up, up, cw, cb)


def _dwconv_bwd_data(dgc, cw, K, into, col, *, name):
    T, C = dgc.shape
    bt = _pick(T, ROW_BLOCK, SUBLANE)
    bc = _pick(C, 1408)
    nt = T // bt
    halo = HALO_BF16 if dgc.dtype == BF16 else HALO
    last_halo = T // halo - 1
    off = col // bc
    assert off * bc == col

    def kern(d_ref, halo_ref, cw_ref, into_ref, o_ref, ext_ref):
        i = pl.program_id(0)
        ext_ref[pl.ds(0, bt), :] = d_ref[...].astype(F32)
        ext_ref[pl.ds(bt, halo), :] = jnp.where(i < nt - 1, halo_ref[...].astype(F32), 0.0)
        ext = ext_ref[...]
        out = None
        for k in range(K):
            ahead = K - 1 - k
            tap = (ext if ahead == 0 else pltpu.roll(ext, bt + halo - ahead, 0))[0:bt]
            term = cw_ref[k:k + 1, :] * tap
            out = term if out is None else out + term
        o_ref[...] = out.astype(o_ref.dtype)

    return pl.pallas_call(
        kern, name=name, grid=(nt, C // bc),
        in_specs=[pl.BlockSpec((bt, bc), lambda i, j: (i, j)),
                  pl.BlockSpec((halo, bc), lambda i, j: (jnp.minimum((i + 1) * (bt // halo), last_halo), j)),
                  pl.BlockSpec((K, bc), lambda i, j: (0, j)), _ANY],
        out_specs=pl.BlockSpec((bt, bc), lambda i, j: (i, j + off)),
        out_shape=jax.ShapeDtypeStruct(into.shape, into.dtype), input_output_aliases={3: 0},
        scratch_shapes=[pltpu.VMEM((bt + halo, bc), F32)],
        compiler_params=_params(("parallel", "parallel")),
    )(dgc, dgc, cw, into)


def _ple_fwd(h, G, bg, pp, *, name):
    T, D = h.shape
    bt = _pick(T, ROW_BLOCK, SUBLANE)

    def kern(h_ref, G_ref, bg_ref, pp_ref, o_ref, ob_ref):
        out = h_ref[...] + _sigmoid(G_ref[...] + bg_ref[...]) * pp_ref[...]
        o_ref[...] = out
        ob_ref[...] = out.astype(BF16)

    row = pl.BlockSpec((bt, D), lambda i: (i, 0))
    vec = pl.BlockSpec((1, D), lambda i: (0, 0))
    return pl.pallas_call(
        kern, name=name, grid=(T // bt,), in_specs=[row, row, vec, row], out_specs=[row, row],
        out_shape=[jax.ShapeDtypeStruct((T, D), F32), jax.ShapeDtypeStruct((T, D), BF16)],
        compiler_params=_params(("parallel",)),
    )(h, G, bg, pp)


def _ple_bwd(dx, G, bg, pp, *, name):
    T, D = dx.shape
    bt = _pick(T, ROW_BLOCK, SUBLANE)

    def kern(dx_ref, G_ref, bg_ref, pp_ref, dG_ref, dpp_ref, dbg_ref):
        i = pl.program_id(0)
        gate = _sigmoid(G_ref[...] + bg_ref[...])
        dxv = dx_ref[...]
        dG = dxv * pp_ref[...] * gate * (1.0 - gate)
        dG_ref[...] = dG.astype(BF16)
        dpp_ref[...] = (dxv * gate).astype(BF16)

        @pl.when(i == 0)
        def _():
            dbg_ref[...] = jnp.zeros_like(dbg_ref)

        dbg_ref[...] += jnp.sum(dG, axis=0, keepdims=True)

    row = pl.BlockSpec((bt, D), lambda i: (i, 0))
    vec = pl.BlockSpec((1, D), lambda i: (0, 0))
    return pl.pallas_call(
        kern, name=name, grid=(T // bt,), in_specs=[row, row, vec, row], out_specs=[row, row, vec],
        out_shape=[jax.ShapeDtypeStruct((T, D), BF16), jax.ShapeDtypeStruct((T, D), BF16),
                   jax.ShapeDtypeStruct((1, D), F32)],
        compiler_params=_params(("arbitrary",)),
    )(dx, G, bg, pp)


def _fox_gate_fwd(proj, bf, *, name):
    T = proj.shape[0]
    bt = _pick(T, CUM_BLOCK, SUBLANE)
    fcol = 3 * D_MODEL // LANE

    def kern(f_ref, bf_ref, c_ref, carry_ref):
        i = pl.program_id(0)

        @pl.when(i == 0)
        def _():
            carry_ref[...] = jnp.zeros_like(carry_ref)

        x = f_ref[...] + bf_ref[...]
        lf = jnp.minimum(x, 0.0) - _log1p(jnp.exp(-jnp.abs(x)))
        cs = _tri_matmul(_tri(bt, True), lf) + carry_ref[...]
        c_ref[...] = cs
        carry_ref[...] = cs[bt - 1:bt, :]

    return pl.pallas_call(
        kern, name=name, grid=(T // bt,),
        in_specs=[pl.BlockSpec((bt, LANE), lambda i: (i, fcol)), pl.BlockSpec((1, LANE), lambda i: (0, 0))],
        out_specs=pl.BlockSpec((bt, LANE), lambda i: (i, 0)),
        out_shape=jax.ShapeDtypeStruct((T, LANE), F32),
        scratch_shapes=[pltpu.VMEM((1, LANE), F32)],
        compiler_params=_params(("arbitrary",)),
    )(proj, bf)


def _fox_gate_bwd(dc, proj, bf, dq, dk, dv, *, name):
    T = proj.shape[0]
    bt = _pick(T, CUM_BLOCK, SUBLANE)
    nb = T // bt
    fcol = 3 * D_MODEL // LANE

    def kern(dc_ref, f_ref, bf_ref, dq_ref, dk_ref, dv_ref, dproj_ref, dbf_ref, carry_ref):
        i = pl.program_id(0)

        @pl.when(i == 0)
        def _():
            carry_ref[...] = jnp.zeros_like(carry_ref)
            dbf_ref[...] = jnp.zeros_like(dbf_ref)

        dlf = _tri_matmul(_tri(bt, False), dc_ref[...]) + carry_ref[...]
        carry_ref[...] = dlf[0:1, :]
        x = f_ref[...] + bf_ref[...]
        lane = lax.broadcasted_iota(jnp.int32, (bt, LANE), 1)
        df = jnp.where(lane < ATTN_HEADS, dlf / (1.0 + jnp.exp(x)), 0.0)
        dbf_ref[...] += jnp.sum(df, axis=0, keepdims=True)
        for n, part_ref in enumerate((dq_ref, dk_ref, dv_ref)):
            dproj_ref[:, n * D_MODEL:(n + 1) * D_MODEL] = part_ref[...].astype(BF16)
        dproj_ref[:, 3 * D_MODEL:] = df.astype(BF16)

    rows = lambda i: (nb - 1 - i, 0)
    wide = pl.BlockSpec((bt, D_MODEL), rows)
    return pl.pallas_call(
        kern, name=name, grid=(nb,),
        in_specs=[pl.BlockSpec((bt, LANE), rows), pl.BlockSpec((bt, LANE), lambda i: (nb - 1 - i, fcol)),
                  pl.BlockSpec((1, LANE), lambda i: (0, 0)), wide, wide, wide],
        out_specs=[pl.BlockSpec((bt, ATTN_IN_PAD), rows), pl.BlockSpec((1, LANE), lambda i: (0, 0))],
        out_shape=[jax.ShapeDtypeStruct((T, ATTN_IN_PAD), BF16), jax.ShapeDtypeStruct((1, LANE), F32)],
        scratch_shapes=[pltpu.VMEM((1, LANE), F32)],
        compiler_params=_params(("arbitrary",)),
    )(dc, proj, bf, dq, dk, dv)


_NT = (((1,), (1,)), ((), ()))
_TN = (((0,), (0,)), ((), ()))


def _dot(a, b, dims=None):
    if dims is None:
        return jnp.dot(a, b, preferred_element_type=F32)
    return lax.dot_general(a, b, dims, preferred_element_type=F32)


LOG2E = 1.0 / math.log(2.0)
LN2 = math.log(2.0)
Q_SCALE = 0.125 * LOG2E
HALF = LANE // 2
L_LANE = (HALF, 0)
FWD_PAIRS = 4
BWD_PAIRS = 1
ATTN_BWD_VMEM_MB = 56


def _attn_prep(proj, *, name):
    T = proj.shape[0]
    bt = _pick(T, ATTN_BLOCK)

    def kern(q_ref, k_ref, v_ref, qa_ref, qb_ref, kk_ref, ka_ref, kb_ref, vv_ref, va_ref, vb_ref):
        lane = lax.broadcasted_iota(jnp.int32, (bt, LANE), 1)
        lo = lane < HALF
        q = q_ref[...] * Q_SCALE
        k = k_ref[...]
        v = v_ref[...]
        qa_ref[...] = jnp.where(lo, q, 0.0).astype(BF16)
        qb_ref[...] = jnp.where(lo, 0.0, q).astype(BF16)
        kk_ref[...] = k.astype(BF16)
        ka_ref[...] = jnp.where(lo, k, 0.0).astype(BF16)
        kb_ref[...] = jnp.where(lo, 0.0, k).astype(BF16)
        vv_ref[...] = v.astype(BF16)
        va_ref[...] = jnp.where(lo, v, jnp.where(lane == L_LANE[0], 1.0, 0.0)).astype(BF16)
        vb_ref[...] = jnp.where(lo, jnp.where(lane == L_LANE[1], 1.0, 0.0), v).astype(BF16)

    kcol, vcol = D_MODEL // LANE, 2 * D_MODEL // LANE
    out = pl.BlockSpec((bt, LANE), lambda i, hp: (i, hp))
    return pl.pallas_call(
        kern, name=name, grid=(T // bt, HEAD_PAIRS),
        in_specs=[out, pl.BlockSpec((bt, LANE), lambda i, hp: (i, kcol + hp)),
                  pl.BlockSpec((bt, LANE), lambda i, hp: (i, vcol + hp))],
        out_specs=[out] * 8, out_shape=[jax.ShapeDtypeStruct((T, D_MODEL), BF16)] * 8,
        compiler_params=_params(("parallel", "parallel")),
    )(proj, proj, proj)


def _attn_fwd(qa, qb, kk, va, vb, cT, *, name):
    T = qa.shape[0]
    tb = _pick(T, ATTN_BLOCK)
    nq = T // tb
    rep = tb // LANE
    width = FWD_PAIRS * LANE

    def kern(qa_ref, qb_ref, k_ref, va_ref, vb_ref, c_ref, o_ref, ob_ref, lsea_ref, lseb_ref, m_ref, acc_ref):
        qi = pl.program_id(1)
        ki = pl.program_id(2)

        @pl.when(ki == 0)
        def _():
            m_ref[...] = jnp.full_like(m_ref, NEG)
            acc_ref[...] = jnp.zeros_like(acc_ref)

        def step(diag):
            for pp in range(FWD_PAIRS):
                cols = slice(pp * LANE, (pp + 1) * LANE)
                k = k_ref[:, cols]
                for h, (q_ref, v_ref) in enumerate(((qa_ref, va_ref), (qb_ref, vb_ref))):
                    i = 2 * pp + h
                    s = _dot(q_ref[:, cols], k, _NT) - c_ref[pp, h:h + 1, :]
                    if diag:
                        r = lax.broadcasted_iota(jnp.int32, (tb, tb), 0)
                        c = lax.broadcasted_iota(jnp.int32, (tb, tb), 1)
                        s = jnp.where(c <= r, s, NEG)
                    m_prev = m_ref[i]
                    m_new = jnp.maximum(m_prev, jnp.max(s, axis=1, keepdims=True))
                    p = jnp.exp2(s - jnp.tile(m_new, (1, rep)))
                    acc_ref[i] = acc_ref[i] * jnp.exp2(m_prev - m_new) + _dot(p.astype(BF16), v_ref[:, cols])
                    m_ref[i] = m_new

        @pl.when(ki < qi)
        def _():
            step(False)

        @pl.when(ki == qi)
        def _():
            step(True)
            lo = lax.broadcasted_iota(jnp.int32, (tb, LANE), 1) < HALF
            for pp in range(FWD_PAIRS):
                cols = slice(pp * LANE, (pp + 1) * LANE)
                a0, a1 = acc_ref[2 * pp], acc_ref[2 * pp + 1]
                l0 = a0[:, L_LANE[0]:L_LANE[0] + 1]
                l1 = a1[:, L_LANE[1]:L_LANE[1] + 1]
                o = jnp.where(lo, a0 / l0, a1 / l1)
                o_ref[:, cols] = o
                ob_ref[:, cols] = o.astype(BF16)
                lsea_ref[:, cols] = m_ref[2 * pp] + jnp.log(l0) * LOG2E
                lseb_ref[:, cols] = m_ref[2 * pp + 1] + jnp.log(l1) * LOG2E

    qspec = pl.BlockSpec((tb, width), lambda g, qi, ki: (qi, g))
    kspec = pl.BlockSpec((tb, width), lambda g, qi, ki: (jnp.minimum(ki, qi), g))
    return pl.pallas_call(
        kern, name=name, grid=(HEAD_PAIRS // FWD_PAIRS, nq, nq),
        in_specs=[qspec, qspec, kspec, kspec, kspec,
                  pl.BlockSpec((FWD_PAIRS, 2, tb), lambda g, qi, ki: (g, 0, jnp.minimum(ki, qi)))],
        out_specs=[qspec, qspec, qspec, qspec],
        out_shape=[jax.ShapeDtypeStruct((T, D_MODEL), F32), jax.ShapeDtypeStruct((T, D_MODEL), BF16),
                   jax.ShapeDtypeStruct((T, D_MODEL), F32), jax.ShapeDtypeStruct((T, D_MODEL), F32)],
        scratch_shapes=[pltpu.VMEM((2 * FWD_PAIRS, tb, LANE), F32), pltpu.VMEM((2 * FWD_PAIRS, tb, LANE), F32)],
        compiler_params=_params(("parallel", "parallel", "arbitrary")),
    )(qa, qb, kk, va, vb, cT)


def _attn_bwd_prep(do, o, *, name):
    T, D = do.shape
    bt = _pick(T, ATTN_BLOCK)

    def kern(do_ref, o_ref, doa_ref, dob_ref, dlta_ref, dltb_ref):
        lo = lax.broadcasted_iota(jnp.int32, (bt, LANE), 1) < HALF
        dov = do_ref[...]
        prod = dov * o_ref[...]
        doa_ref[...] = jnp.where(lo, dov, 0.0).astype(BF16)
        dob_ref[...] = jnp.where(lo, 0.0, dov).astype(BF16)
        dlta_ref[...] = jnp.broadcast_to(jnp.sum(jnp.where(lo, prod, 0.0), axis=1, keepdims=True), (bt, LANE))
        dltb_ref[...] = jnp.broadcast_to(jnp.sum(jnp.where(lo, 0.0, prod), axis=1, keepdims=True), (bt, LANE))

    blk = pl.BlockSpec((bt, LANE), lambda i, hp: (i, hp))
    return pl.pallas_call(
        kern, name=name, grid=(T // bt, HEAD_PAIRS), in_specs=[blk, blk], out_specs=[blk] * 4,
        out_shape=[jax.ShapeDtypeStruct((T, D), BF16)] * 2 + [jax.ShapeDtypeStruct((T, D), F32)] * 2,
        compiler_params=_params(("parallel", "parallel")),
    )(do, o)


def _attn_bwd(qa, qb, kk, ka, kb, vv, doa, dob, lsea, lseb, dlta, dltb, cT, *, name):
    T = qa.shape[0]
    tb = _pick(T, ATTN_BLOCK)
    nq = T // tb
    rep = tb // LANE
    width = BWD_PAIRS * LANE

    def kern(qa_ref, qb_ref, k_ref, ka_ref, kb_ref, v_ref, doa_ref, dob_ref, lsea_ref, lseb_ref, dlta_ref, dltb_ref,
             c_ref, dq_ref, dk_ref, dv_ref, dc_ref, dcq_ref):
        g = pl.program_id(0)
        ki = pl.program_id(1)
        qi = pl.program_id(2)
        first = jnp.logical_and(ki == 0, qi == 0)

        @pl.when(first)
        def _():
            dq_ref[...] = jnp.zeros_like(dq_ref)

        @pl.when(jnp.logical_and(first, g == 0))
        def _():
            dcq_ref[...] = jnp.zeros_like(dcq_ref)

        @pl.when(qi == 0)
        def _():
            dk_ref[...] = jnp.zeros_like(dk_ref)
            dv_ref[...] = jnp.zeros_like(dv_ref)
            dc_ref[...] = jnp.zeros_like(dc_ref)

        def step(diag):
            rows = pl.ds(pl.multiple_of(qi * tb, tb), tb)
            lane = lax.broadcasted_iota(jnp.int32, (tb, LANE), 1)
            row_sums = jnp.zeros((tb, LANE), F32)
            for pp in range(BWD_PAIRS):
                cols = slice(pp * LANE, (pp + 1) * LANE)
                k = k_ref[:, cols]
                v = v_ref[:, cols]
                dq = None
                dk = None
                dv = None
                heads = ((qa_ref, ka_ref, doa_ref, lsea_ref, dlta_ref), (qb_ref, kb_ref, dob_ref, lseb_ref, dltb_ref))
                for h, (q_ref, km_ref, do_ref, lse_ref, dlt_ref) in enumerate(heads):
                    q = q_ref[:, cols]
                    dom = do_ref[:, cols]
                    s = _dot(q, k, _NT) - c_ref[pp, h:h + 1, :]
                    if diag:
                        r = lax.broadcasted_iota(jnp.int32, (tb, tb), 0)
                        c = lax.broadcasted_iota(jnp.int32, (tb, tb), 1)
                        s = jnp.where(c <= r, s, NEG)
                    p = jnp.exp2(s - jnp.tile(lse_ref[:, cols], (1, rep)))
                    ds = p * (_dot(dom, v, _NT) - jnp.tile(dlt_ref[:, cols], (1, rep)))
                    dc_ref[pp, h:h + 1, :] -= jnp.sum(ds, axis=0, keepdims=True)
                    head = 2 * (BWD_PAIRS * g + pp) + h
                    row_sums = jnp.where(lane == head, jnp.sum(ds, axis=1, keepdims=True), row_sums)
                    dsb = ds.astype(BF16)
                    tv = _dot(p.astype(BF16), dom, _TN)
                    tk = _dot(dsb, q, _TN)
                    tq = _dot(dsb, km_ref[:, cols])
                    dv = tv if dv is None else dv + tv
                    dk = tk if dk is None else dk + tk
                    dq = tq if dq is None else dq + tq
                dv_ref[:, cols] += dv
                dk_ref[:, cols] += dk * LN2
                dq_ref[rows, cols] += dq * 0.125
            dcq_ref[rows, :] += row_sums

        @pl.when(qi > ki)
        def _():
            step(False)

        @pl.when(qi == ki)
        def _():
            step(True)

    qspec = pl.BlockSpec((tb, width), lambda g, ki, qi: (jnp.maximum(qi, ki), g))
    kspec = pl.BlockSpec((tb, width), lambda g, ki, qi: (ki, g))
    cspec = pl.BlockSpec((BWD_PAIRS, 2, tb), lambda g, ki, qi: (g, 0, ki))
    qacc = pl.BlockSpec((T, width), lambda g, ki, qi: (0, g), pipeline_mode=pl.Buffered(1))
    cqacc = pl.BlockSpec((T, LANE), lambda g, ki, qi: (0, 0), pipeline_mode=pl.Buffered(1))
    return pl.pallas_call(
        kern, name=name, grid=(HEAD_PAIRS // BWD_PAIRS, nq, nq),
        in_specs=[qspec, qspec, kspec, kspec, kspec, kspec, qspec, qspec, qspec, qspec, qspec, qspec, cspec],
        out_specs=[qacc, kspec, kspec, cspec, cqacc],
        out_shape=[jax.ShapeDtypeStruct((T, D_MODEL), F32)] * 3 + [jax.ShapeDtypeStruct((HEAD_PAIRS, 2, T), F32),
                                                                   jax.ShapeDtypeStruct((T, LANE), F32)],
        compiler_params=_params(("arbitrary", "arbitrary", "arbitrary"), vmem_mb=ATTN_BWD_VMEM_MB),
    )(qa, qb, kk, ka, kb, vv, doa, dob, lsea, lseb, dlta, dltb, cT)


def _ssd_dt_fwd(proj, dt_bias, a_log, *, name):
    T = proj.shape[0]
    Q = SSM_CHUNK
    col = (2 * SSM_D_INNER + 2 * SSM_GROUPS * SSM_STATE) // LANE

    def kern(raw_ref, b_ref, al_ref, dt_ref, ac_ref):
        dt = _softplus(raw_ref[...] + b_ref[...])
        dt_ref[...] = dt
        ac_ref[...] = _tri_matmul(_tri(Q, True), dt * (-jnp.exp(al_ref[...])))

    vec = pl.BlockSpec((1, LANE), lambda i: (0, 0))
    blk = pl.BlockSpec((Q, LANE), lambda i: (i, 0))
    return pl.pallas_call(
        kern, name=name, grid=(T // Q,),
        in_specs=[pl.BlockSpec((Q, LANE), lambda i: (i, col)), vec, vec], out_specs=[blk, blk],
        out_shape=[jax.ShapeDtypeStruct((T, LANE), F32)] * 2,
        compiler_params=_params(("parallel",)),
    )(proj, dt_bias, a_log)


def _ssd_dt_bwd(da_a, da_b, ddt_a, ddt_b, dt, proj, dt_bias, a_log, into, *, name):
    T = proj.shape[0]
    Q = SSM_CHUNK
    col = (2 * SSM_D_INNER + 2 * SSM_GROUPS * SSM_STATE) // LANE

    def kern(daa_ref, dab_ref, dda_ref, ddb_ref, dt_ref, raw_ref, b_ref, al_ref, into_ref, draw_ref, dal_ref, db_ref,
             acc_ref):
        i = pl.program_id(0)

        @pl.when(i == 0)
        def _():
            acc_ref[...] = jnp.zeros_like(acc_ref)
            db_ref[...] = jnp.zeros_like(db_ref)

        A = -jnp.exp(al_ref[...])
        ddA = _tri_matmul(_tri(Q, False), daa_ref[...] + dab_ref[...])
        ddt = dda_ref[...] + ddb_ref[...] + ddA * A
        acc_ref[...] += jnp.sum(ddA * dt_ref[...], axis=0, keepdims=True)
        lane = lax.broadcasted_iota(jnp.int32, (Q, LANE), 1)
        draw = jnp.where(lane < SSM_HEADS, ddt * _sigmoid(raw_ref[...] + b_ref[...]), 0.0)
        draw_ref[...] = draw.astype(BF16)
        db_ref[...] += jnp.sum(draw, axis=0, keepdims=True)
        dal_ref[...] = acc_ref[...] * A

    vec = pl.BlockSpec((1, LANE), lambda i: (0, 0))
    blk = pl.BlockSpec((Q, LANE), lambda i: (i, 0))
    return pl.pallas_call(
        kern, name=name, grid=(T // Q,),
        in_specs=[blk, blk, blk, blk, blk, pl.BlockSpec((Q, LANE), lambda i: (i, col)), vec, vec, _ANY],
        out_specs=[pl.BlockSpec((Q, LANE), lambda i: (i, col)), vec, vec],
        out_shape=[jax.ShapeDtypeStruct(into.shape, into.dtype), jax.ShapeDtypeStruct((1, LANE), F32),
                   jax.ShapeDtypeStruct((1, LANE), F32)],
        input_output_aliases={8: 0},
        scratch_shapes=[pltpu.VMEM((1, LANE), F32)],
        compiler_params=_params(("arbitrary",)),
    )(da_a, da_b, ddt_a, ddt_b, dt, proj, dt_bias, a_log, into)


def _conv_silu_fwd(proj, cw, cb, *, name):
    T = proj.shape[0]
    C = SSM_XBC
    bt = _pick(T, ROW_BLOCK, SUBLANE)
    bc = 1024
    off = SSM_D_INNER // bc

    def kern(x_ref, halo_ref, cw_ref, cb_ref, o_ref, ext_ref):
        i = pl.program_id(0)
        _fill_ext_past(ext_ref, halo_ref, x_ref[...], i, bt)
        pre = cb_ref[...] + _conv_past(_past_taps(ext_ref, SSM_CONV, bt), cw_ref)
        o_ref[...] = pre * _sigmoid(pre)

    return pl.pallas_call(
        kern, name=name, grid=(T // bt, C // bc),
        in_specs=[pl.BlockSpec((bt, bc), lambda i, j: (i, j + off)), _halo_prev(bt, bc, off),
                  pl.BlockSpec((SSM_CONV, bc), lambda i, j: (0, j)), pl.BlockSpec((1, bc), lambda i, j: (0, j))],
        out_specs=pl.BlockSpec((bt, bc), lambda i, j: (i, j)),
        out_shape=jax.ShapeDtypeStruct((T, C), F32),
        scratch_shapes=[pltpu.VMEM((bt + HALO, bc), F32)],
        compiler_params=_params(("parallel", "parallel")),
    )(proj, proj, cw, cb)


def _conv_silu_bwd(dxbc, proj, cw, cb, *, name):
    T = proj.shape[0]
    C = SSM_XBC
    K = SSM_CONV
    bt = _pick(T, ROW_BLOCK, SUBLANE)
    bc = 1024
    off = SSM_D_INNER // bc

    def kern(d_ref, x_ref, halo_ref, cw_ref, cb_ref, dpre_ref, dcb_ref, dcw_ref, ext_ref):
        i = pl.program_id(1)
        _fill_ext_past(ext_ref, halo_ref, x_ref[...], i, bt)
        taps = _past_taps(ext_ref, K, bt)
        pre = cb_ref[...] + _conv_past(taps, cw_ref)
        sg = _sigmoid(pre)
        dpre = d_ref[...] * sg * (1.0 + pre * (1.0 - sg))
        dpre_ref[...] = dpre.astype(BF16)

        @pl.when(i == 0)
        def _():
            dcb_ref[...] = jnp.zeros_like(dcb_ref)
            dcw_ref[...] = jnp.zeros_like(dcw_ref)

        dcb_ref[...] += jnp.sum(dpre, axis=0, keepdims=True)
        for k in range(K):
            dcw_ref[k:k + 1, :] += jnp.sum(dpre * taps[k], axis=0, keepdims=True)

    blk = pl.BlockSpec((bt, bc), lambda j, i: (i, j))
    return pl.pallas_call(
        kern, name=name, grid=(C // bc, T // bt),
        in_specs=[blk, pl.BlockSpec((bt, bc), lambda j, i: (i, j + off)),
                  pl.BlockSpec((HALO, bc), lambda j, i: (jnp.maximum(i * (bt // HALO) - 1, 0), j + off)),
                  pl.BlockSpec((K, bc), lambda j, i: (0, j)), pl.BlockSpec((1, bc), lambda j, i: (0, j))],
        out_specs=[blk, pl.BlockSpec((1, bc), lambda j, i: (0, j)), pl.BlockSpec((K, bc), lambda j, i: (0, j))],
        out_shape=[jax.ShapeDtypeStruct((T, C), BF16), jax.ShapeDtypeStruct((1, C), F32),
                   jax.ShapeDtypeStruct((K, C), F32)],
        scratch_shapes=[pltpu.VMEM((bt + HALO, bc), F32)],
        compiler_params=_params(("parallel", "arbitrary")),
    )(dxbc, proj, proj, cw, cb)


_GP = SSM_D_INNER // SSM_GROUPS
_HPG = SSM_HEADS // SSM_GROUPS
_PH = SSM_D_INNER // SSM_HEADS


def _head_masks(rows):
    lane = lax.broadcasted_iota(jnp.int32, (rows, _GP), 1)
    return [jnp.logical_and(lane >= r * _PH, lane < (r + 1) * _PH) for r in range(_HPG)]


def _ssd_cols(g):
    x0 = g * _GP
    b0 = SSM_D_INNER + g * SSM_STATE
    c0 = SSM_D_INNER + (SSM_GROUPS + g) * SSM_STATE
    return slice(x0, x0 + _GP), slice(b0, b0 + SSM_STATE), slice(c0, c0 + SSM_STATE)


def _ssd_specs(idx):
    Q, N = SSM_CHUNK, SSM_STATE
    return dict(
        xbc=pl.BlockSpec((Q, SSM_XBC), lambda j: (idx(j), 0)),
        x=pl.BlockSpec((Q, SSM_D_INNER), lambda j: (idx(j), 0)),
        col=pl.BlockSpec((Q, LANE), lambda j: (idx(j), 0)),
        row=pl.BlockSpec((SSM_HEADS, Q), lambda j: (0, idx(j))),
        st=pl.BlockSpec((N, SSM_D_INNER), lambda j: (idx(j), 0)),
    )


def _ssd_scan_fwd(xbc, dtc, acc_, dtr, acr, *, name):
    T = xbc.shape[0]
    Q, N = SSM_CHUNK, SSM_STATE
    nc = T // Q
    sp = _ssd_specs(lambda j: j)

    def kern(xbc_ref, dtc_ref, ac_ref, dtr_ref, ar_ref, ys_ref, st_ref, state_ref):
        @pl.when(pl.program_id(0) == 0)
        def _():
            state_ref[...] = jnp.zeros_like(state_ref)

        r_i = lax.broadcasted_iota(jnp.int32, (Q, Q), 0)
        c_i = lax.broadcasted_iota(jnp.int32, (Q, Q), 1)
        tri = c_i <= r_i
        masks = _head_masks(Q)
        masks1 = _head_masks(1)
        for g in range(SSM_GROUPS):
            xs, bs, cs = _ssd_cols(g)
            S = state_ref[g]
            st_ref[:, xs] = S
            x = xbc_ref[:, xs]
            xb = x.astype(BF16)
            Bb = xbc_ref[:, bs].astype(BF16)
            Cb = xbc_ref[:, cs].astype(BF16)
            CB = _dot(Cb, Bb, _NT)
            y = jnp.zeros((Q, _GP), F32)
            El = jnp.zeros((Q, _GP), F32)
            Wl = jnp.zeros((Q, _GP), F32)
            decl = jnp.zeros((1, _GP), F32)
            for r in range(_HPG):
                h = g * _HPG + r
                a_c = ac_ref[:, h:h + 1]
                a_r = ar_ref[h:h + 1, :]
                dt_c = dtc_ref[:, h:h + 1]
                dt_r = dtr_ref[h:h + 1, :]
                L = jnp.exp(jnp.where(tri, a_c - a_r, NEG))
                W = CB * L * dt_r
                y = jnp.where(masks[r], _dot(W.astype(BF16), xb), y)
                a_q = a_c[Q - 1:Q, :]
                El = jnp.where(masks[r], jnp.exp(a_c), El)
                Wl = jnp.where(masks[r], jnp.exp(a_q - a_c) * dt_c, Wl)
                decl = jnp.where(masks1[r], jnp.exp(a_q), decl)
            ys_ref[:, xs] = y + _dot(Cb, S.astype(BF16)) * El
            state_ref[g] = S * decl + _dot(Bb, (x * Wl).astype(BF16), _TN)

    return pl.pallas_call(
        kern, name=name, grid=(nc,),
        in_specs=[sp["xbc"], sp["col"], sp["col"], sp["row"], sp["row"]],
        out_specs=[sp["x"], sp["st"]],
        out_shape=[jax.ShapeDtypeStruct((T, SSM_D_INNER), F32), jax.ShapeDtypeStruct((nc * N, SSM_D_INNER), F32)],
        scratch_shapes=[pltpu.VMEM((SSM_GROUPS, N, _GP), F32)],
        compiler_params=_params(("arbitrary",)),
    )(xbc, dtc, acc_, dtr, acr)


def _ssd_scan_bwd(xbc, dys, dskip, st, dtc, acc_, dtr, acr, *, name):
    T = xbc.shape[0]
    Q, N = SSM_CHUNK, SSM_STATE
    nc = T // Q
    sp = _ssd_specs(lambda j: nc - 1 - j)

    def kern(xbc_ref, dy_ref, dsk_ref, st_ref, dtc_ref, ac_ref, dtr_ref, ar_ref,
             dxbc_ref, dac_ref, dar_ref, ddc_ref, ddr_ref, dstate_ref):
        @pl.when(pl.program_id(0) == 0)
        def _():
            dstate_ref[...] = jnp.zeros_like(dstate_ref)

        r_i = lax.broadcasted_iota(jnp.int32, (Q, Q), 0)
        c_i = lax.broadcasted_iota(jnp.int32, (Q, Q), 1)
        tri = c_i <= r_i
        last_row = lax.broadcasted_iota(jnp.int32, (Q, 1), 0) == Q - 1
        lane128 = lax.broadcasted_iota(jnp.int32, (Q, LANE), 1)
        masks = _head_masks(Q)
        masksN = _head_masks(N)
        masks1 = _head_masks(1)
        zeros = jnp.zeros((Q, _GP), F32)
        dacol = jnp.zeros((Q, LANE), F32)
        ddcol = jnp.zeros((Q, LANE), F32)
        for g in range(SSM_GROUPS):
            xs, bs, cs = _ssd_cols(g)
            dS = dstate_ref[g]
            dSb = dS.astype(BF16)
            S = st_ref[:, xs]
            Sb = S.astype(BF16)
            x = xbc_ref[:, xs]
            xb = x.astype(BF16)
            Bb = xbc_ref[:, bs].astype(BF16)
            Cb = xbc_ref[:, cs].astype(BF16)
            dy = dy_ref[:, xs]
            CB = _dot(Cb, Bb, _NT)
            BdS = _dot(Bb, dSb)
            hx = BdS * x
            yd = _dot(Cb, Sb) * dy
            dSS = dS * S
            dxi, El, Wl = zeros, zeros, zeros
            decl = jnp.zeros((1, _GP), F32)
            dG = jnp.zeros((Q, Q), F32)
            for r in range(_HPG):
                h = g * _HPG + r
                hm = masks[r]
                a_c = ac_ref[:, h:h + 1]
                a_r = ar_ref[h:h + 1, :]
                dt_c = dtc_ref[:, h:h + 1]
                dt_r = dtr_ref[h:h + 1, :]
                L = jnp.exp(jnp.where(tri, a_c - a_r, NEG))
                GL = CB * L
                W = GL * dt_r
                dym = jnp.where(hm, dy, 0.0).astype(BF16)
                dW = _dot(dym, xb, _NT)
                X = dW * GL
                da_c = jnp.sum(X * dt_r, axis=1, keepdims=True)
                ddt_r = jnp.sum(X, axis=0, keepdims=True)
                ddr_ref[h:h + 1, :] = ddt_r
                dar_ref[h:h + 1, :] = -ddt_r * dt_r
                dG = dG + dW * L * dt_r
                dxi = dxi + _dot(W.astype(BF16), dym, _TN)
                a_q = a_c[Q - 1:Q, :]
                e_c = jnp.exp(a_c)
                eq_c = jnp.exp(a_q - a_c)
                w_c = eq_c * dt_c
                ydr = jnp.sum(jnp.where(hm, yd, 0.0), axis=1, keepdims=True) * e_c
                h_c = jnp.sum(jnp.where(hm, hx, 0.0), axis=1, keepdims=True)
                hw = h_c * w_c
                dss = jnp.sum(jnp.sum(jnp.where(masksN[r], dSS, 0.0), axis=1, keepdims=True), axis=0, keepdims=True)
                s_q = jnp.sum(hw, axis=0, keepdims=True) + jnp.exp(a_q) * dss
                da_c = da_c + ydr - hw + jnp.where(last_row, s_q, 0.0)
                dacol = jnp.where(lane128 == h, da_c, dacol)
                ddcol = jnp.where(lane128 == h, h_c * eq_c, ddcol)
                El = jnp.where(hm, e_c, El)
                Wl = jnp.where(hm, w_c, Wl)
                decl = jnp.where(masks1[r], jnp.exp(a_q), decl)
            dxbc_ref[:, xs] = dxi + BdS * Wl + dsk_ref[:, xs]
            dGb = dG.astype(BF16)
            dxbc_ref[:, bs] = _dot(dGb, Cb, _TN) + _dot((x * Wl).astype(BF16), dSb, _NT)
            dyE = (dy * El).astype(BF16)
            dxbc_ref[:, cs] = _dot(dGb, Bb) + _dot(dyE, Sb, _NT)
            dstate_ref[g] = dS * decl + _dot(Cb, dyE, _TN)
        dac_ref[...] = dacol
        ddc_ref[...] = ddcol

    return pl.pallas_call(
        kern, name=name, grid=(nc,),
        in_specs=[sp["xbc"], sp["x"], sp["x"], sp["st"], sp["col"], sp["col"], sp["row"], sp["row"]],
        out_specs=[sp["xbc"], sp["col"], sp["row"], sp["col"], sp["row"]],
        out_shape=[jax.ShapeDtypeStruct((T, SSM_XBC), F32),
                   jax.ShapeDtypeStruct((T, LANE), F32), jax.ShapeDtypeStruct((SSM_HEADS, T), F32),
                   jax.ShapeDtypeStruct((T, LANE), F32), jax.ShapeDtypeStruct((SSM_HEADS, T), F32)],
        scratch_shapes=[pltpu.VMEM((SSM_GROUPS, N, _GP), F32)],
        compiler_params=_params(("arbitrary",)),
    )(xbc, dys, dskip, st, dtc, acc_, dtr, acr)


def _gate_norm_fwd(ys, xbc, proj, d_exp, norm_w, *, name):
    T = ys.shape[0]
    bt = _pick(T, NARROW_ROW_BLOCK, SUBLANE)

    def kern(ys_ref, x_ref, z_ref, d_ref, w_ref, o_ref):
        z = z_ref[...]
        yz = (ys_ref[...] + d_ref[...] * x_ref[...]) * (z * _sigmoid(z))
        rstd = lax.rsqrt(jnp.mean(yz * yz, axis=-1, keepdims=True) + RMS_EPS)
        o_ref[...] = (yz * rstd * w_ref[...]).astype(BF16)

    blk = pl.BlockSpec((bt, _GP), lambda i, g: (i, g))
    vec = pl.BlockSpec((1, _GP), lambda i, g: (0, g))
    return pl.pallas_call(
        kern, name=name, grid=(T // bt, SSM_GROUPS), in_specs=[blk, blk, blk, vec, vec], out_specs=blk,
        out_shape=jax.ShapeDtypeStruct((T, SSM_D_INNER), BF16), compiler_params=_params(("parallel", "parallel")),
    )(ys, xbc, proj, d_exp, norm_w)


def _gate_norm_bwd(dyn, ys, xbc, proj, d_exp, norm_w, *, name):
    T = ys.shape[0]
    bt = _pick(T, NARROW_ROW_BLOCK, SUBLANE)

    def kern(dyn_ref, ys_ref, x_ref, z_ref, d_ref, w_ref, dz_ref, dys_ref, dsk_ref, dw_ref, dd_ref):
        i = pl.program_id(1)
        z = z_ref[...]
        x = x_ref[...]
        sg = _sigmoid(z)
        sz = z * sg
        y = ys_ref[...] + d_ref[...] * x
        yz = y * sz
        rstd = lax.rsqrt(jnp.mean(yz * yz, axis=-1, keepdims=True) + RMS_EPS)
        yhat = yz * rstd
        dynv = dyn_ref[...]
        gg = dynv * w_ref[...]
        dyz = rstd * (gg - yhat * jnp.mean(gg * yhat, axis=-1, keepdims=True))
        dy = dyz * sz
        dz_ref[...] = (dyz * y * sg * (1.0 + z * (1.0 - sg))).astype(BF16)
        dys_ref[...] = dy
        dsk_ref[...] = dy * d_ref[...]

        @pl.when(i == 0)
        def _():
            dw_ref[...] = jnp.zeros_like(dw_ref)
            dd_ref[...] = jnp.zeros_like(dd_ref)

        dw_ref[...] += jnp.sum(dynv * yhat, axis=0, keepdims=True)
        dd_ref[...] += jnp.sum(dy * x, axis=0, keepdims=True)

    blk = pl.BlockSpec((bt, _GP), lambda g, i: (i, g))
    vec = pl.BlockSpec((1, _GP), lambda g, i: (0, g))
    act = jax.ShapeDtypeStruct((T, SSM_D_INNER), F32)
    par = jax.ShapeDtypeStruct((1, SSM_D_INNER), F32)
    return pl.pallas_call(
        kern, name=name, grid=(SSM_GROUPS, T // bt), in_specs=[blk, blk, blk, blk, vec, vec],
        out_specs=[blk, blk, blk, vec, vec],
        out_shape=[jax.ShapeDtypeStruct((T, SSM_IN_PAD), BF16), act, act, par, par],
        compiler_params=_params(("parallel", "arbitrary")),
    )(dyn, ys, xbc, proj, d_exp, norm_w)


def _loss_head(y, target, *, name):
    T, D = y.shape
    bt = _pick(T, ROW_BLOCK, SUBLANE)

    def kern(y_ref, t_ref, l_ref, dy_ref):
        i = pl.program_id(0)
        err = y_ref[...] - t_ref[...]
        dy_ref[...] = err * (1.0 / D)

        @pl.when(i == 0)
        def _():
            l_ref[...] = jnp.zeros_like(l_ref)

        l_ref[...] += jnp.sum(err * err, axis=0, keepdims=True) * (0.5 / D)

    row = pl.BlockSpec((bt, D), lambda i: (i, 0))
    vec = pl.BlockSpec((1, D), lambda i: (0, 0))
    return pl.pallas_call(
        kern, name=name, grid=(T // bt,), in_specs=[row, row], out_specs=[vec, row],
        out_shape=[jax.ShapeDtypeStruct((1, D), F32), jax.ShapeDtypeStruct((T, D), F32)],
        compiler_params=_params(("arbitrary",)),
    )(y, target)


def _adamw(w, g, m, v, *, name):
    shape = w.shape
    w, g, m, v = (t.reshape(-1, shape[-1]) for t in (w, g, m, v))
    R, C = w.shape
    br = _pick(R, 256, SUBLANE)

    def kern(w_ref, g_ref, m_ref, v_ref, d_ref, nm_ref, nv_ref):
        gv = g_ref[...]
        nm = ADAM_B1 * m_ref[...] + (1.0 - ADAM_B1) * gv
        nv = ADAM_B2 * v_ref[...] + (1.0 - ADAM_B2) * (gv * gv)
        m_hat = nm / (1.0 - ADAM_B1 ** ADAM_STEP)
        v_hat = nv / (1.0 - ADAM_B2 ** ADAM_STEP)
        d_ref[...] = -ADAM_LR * (m_hat / (jnp.sqrt(v_hat) + ADAM_EPS) + ADAM_WD * w_ref[...])
        nm_ref[...] = nm
        nv_ref[...] = nv

    blk = pl.BlockSpec((br, C), lambda i: (i, 0))
    outs = pl.pallas_call(
        kern, name=name, grid=(R // br,), in_specs=[blk] * 4, out_specs=[blk] * 3,
        out_shape=[jax.ShapeDtypeStruct((R, C), F32)] * 3, compiler_params=_params(("parallel",)),
    )(w, g, m, v)
    return [o.reshape(shape) for o in outs]


def _add2(a, b, out_dtype, *, name):
    shape = a.shape
    a2, b2 = a.reshape(-1, shape[-1]), b.reshape(-1, shape[-1])
    R, C = a2.shape
    br = _pick(R, 512, SUBLANE)

    def kern(a_ref, b_ref, o_ref):
        o_ref[...] = (a_ref[...].astype(F32) + b_ref[...].astype(F32)).astype(out_dtype)

    blk = pl.BlockSpec((br, C), lambda i: (i, 0))
    return pl.pallas_call(
        kern, name=name, grid=(R // br,), in_specs=[blk, blk], out_specs=blk,
        out_shape=jax.ShapeDtypeStruct((R, C), out_dtype), compiler_params=_params(("parallel",)),
    )(a2, b2).reshape(shape)


def _sum4(buf, *, name):
    _, R, C = buf.shape
    br = _pick(R, 512, SUBLANE)

    def kern(b_ref, o_ref):
        b = [b_ref[k].astype(F32) for k in range(4)]
        o_ref[...] = ((b[0] + b[1]) + b[2]) + b[3]

    return pl.pallas_call(
        kern, name=name, grid=(R // br,), in_specs=[pl.BlockSpec((4, br, C), lambda i: (0, i, 0))],
        out_specs=pl.BlockSpec((br, C), lambda i: (i, 0)),
        out_shape=jax.ShapeDtypeStruct((R, C), F32), compiler_params=_params(("parallel",)),
    )(buf)


def _place():
    x, y, c = lax.axis_index("x"), lax.axis_index("y"), lax.axis_index("c")
    other_chips = [(1 - x, y), (x, 1 - y), (1 - x, 1 - y)]
    return x, y, c, other_chips


def _gather_chips(w, *, name):
    R, C = w.shape
    H = R // 2

    def body(w_ref, out_ref, send_sems, recv_sems):
        x, y, c, chips = _place()
        me_chip = 2 * x + y
        sib = (x, y, 1 - c)

        def rows(chip, hc):
            return out_ref.at[chip, pl.ds(hc * H, H), :]

        def copy(k, blk, to, src=None):
            return pltpu.make_async_remote_copy(
                src_ref=blk if src is None else src, dst_ref=blk, send_sem=send_sems.at[k], recv_sem=recv_sems.at[k],
                device_id=to, device_id_type=MESH)

        first = [copy(j, rows(me_chip, c), (cx, cy, c), src=w_ref.at[pl.ds(c * H, H), :])
                 for j, (cx, cy) in enumerate(chips)]
        for cp in first:
            cp.start()
        passed = []
        for j, (cx, cy) in enumerate(chips):
            blk = rows(2 * cx + cy, c)
            copy(j, blk, (cx, cy, c)).wait_recv()
            fw = copy(3 + j, blk, sib)
            fw.start()
            passed.append(fw)
        for j, (cx, cy) in enumerate(chips):
            copy(3 + j, rows(2 * cx + cy, 1 - c), sib).wait_recv()
        for cp in first + passed:
            cp.wait_send()

    return pl.pallas_call(
        body, name=name, in_specs=[_ANY], out_specs=_ANY,
        out_shape=jax.ShapeDtypeStruct((4, R, C), w.dtype),
        scratch_shapes=[pltpu.SemaphoreType.DMA((6,)), pltpu.SemaphoreType.DMA((6,))],
    )(w)


def _pair_swap(v, *, name, other_half=False):
    shape = (v.shape[0], v.shape[1] // 2, v.shape[2]) if other_half else v.shape

    def body(v_ref, out_ref, send_sem, recv_sem):
        x, y, c, _ = _place()
        src = v_ref.at[:, pl.ds((1 - c) * shape[1], shape[1]), :] if other_half else v_ref
        cp = pltpu.make_async_remote_copy(src_ref=src, dst_ref=out_ref, send_sem=send_sem, recv_sem=recv_sem,
                                          device_id=(x, y, 1 - c), device_id_type=MESH)
        cp.start()
        cp.wait()

    return pl.pallas_call(
        body, name=name, in_specs=[_ANY], out_specs=_ANY, out_shape=jax.ShapeDtypeStruct(shape, v.dtype),
        scratch_shapes=[pltpu.SemaphoreType.DMA, pltpu.SemaphoreType.DMA],
    )(v)


def _chip_exchange(pv, *, name):
    def body(p_ref, out_ref, send_sems, recv_sems):
        x, y, c, chips = _place()
        me_chip = 2 * x + y
        sends = []
        for j, (cx, cy) in enumerate(chips):
            cp = pltpu.make_async_remote_copy(
                src_ref=p_ref.at[2 * cx + cy], dst_ref=out_ref.at[me_chip], send_sem=send_sems.at[j],
                recv_sem=recv_sems.at[j], device_id=(cx, cy, c), device_id_type=MESH)
            cp.start()
            sends.append(cp)
        for j, (cx, cy) in enumerate(chips):
            blk = out_ref.at[2 * cx + cy]
            pltpu.make_async_remote_copy(src_ref=blk, dst_ref=blk, send_sem=send_sems.at[j], recv_sem=recv_sems.at[j],
                                         device_id=(cx, cy, c), device_id_type=MESH).wait_recv()
        for cp in sends:
            cp.wait_send()

    return pl.pallas_call(
        body, name=name, in_specs=[_ANY], out_specs=_ANY, out_shape=jax.ShapeDtypeStruct(pv.shape, pv.dtype),
        scratch_shapes=[pltpu.SemaphoreType.DMA((3,)), pltpu.SemaphoreType.DMA((3,))],
    )(pv)


WEIGHTS = [
    ("attn_w_in", 2), ("attn_b_f", None), ("attn_w_out", 1), ("ssm_w_in", 2), ("ssm_conv_w", 2), ("ssm_conv_b", 1),
    ("ssm_dt_bias", None), ("ssm_A_log", None), ("ssm_D", None), ("ssm_norm_w", 1), ("ssm_w_out", 1),
    ("ln_mix_g", None), ("ln_mix_b", None), ("ffn_w_up", 2), ("ffn_conv_w", 2), ("ffn_conv_b", None),
    ("ffn_w_down", 1), ("ln_ffn_g", None), ("ln_ffn_b", None), ("ple_w_proj", 2), ("ple_w_gate", 1),
    ("ple_b_gate", None),
]
N_CHIPS = 4
MATMUL_WEIGHTS = ("attn_w_in", "attn_w_out", "ssm_w_in", "ssm_w_out", "ffn_w_up", "ffn_w_down", "ple_w_proj",
                  "ple_w_gate")


def _pack(arrays):
    parts = []
    total = 0
    for a in arrays:
        n = a.size
        pad = (-n) % PACK_COLS
        flat = a.reshape(-1)
        parts.append(jnp.pad(flat, (0, pad)) if pad else flat)
        total += n + pad
    rows = total // PACK_COLS
    rpad = (-rows) % PACK_ROW_ALIGN
    if rpad:
        parts.append(jnp.zeros((rpad * PACK_COLS,), arrays[0].dtype))
    return jnp.concatenate(parts).reshape(rows + rpad, PACK_COLS)


def _unpack(buf, shapes):
    flat = buf.reshape(-1)
    out = []
    off = 0
    for s in shapes:
        n = math.prod(s)
        out.append(flat[off:off + n].reshape(s))
        off += n + ((-n) % PACK_COLS)
    return out


def _from_row_layout(a):
    return jnp.pad(a.T, ((0, 0), (0, LANE - SSM_HEADS)))


def _pad_lanes(v, n=LANE):
    return jnp.pad(v, (0, n - v.shape[0])).reshape(1, n)


def _local_step(x, p, target, W):
    T = x.shape[0]
    row = lambda v: v.reshape(1, -1)
    attn_in = jnp.pad(W["attn_w_in"][0], ((0, 0), (0, ATTN_IN_PAD - W["attn_w_in"].shape[2])))
    ssm_in = jnp.pad(W["ssm_w_in"][0], ((0, 0), (0, SSM_IN_PAD - W["ssm_w_in"].shape[2])))
    bf = _pad_lanes(W["attn_b_f"][0])
    dt_bias = _pad_lanes(W["ssm_dt_bias"][0])
    a_log = _pad_lanes(W["ssm_A_log"][0])
    d_exp = jnp.repeat(W["ssm_D"][0], _PH).reshape(1, SSM_D_INNER)
    norm_w = row(W["ssm_norm_w"][0])
    G = {}

    def ffn_ple_fwd(i, xin, mix, tag):
        s = {}
        s["z1"], s["h1"], s["h1b"] = _ln_fwd(xin, mix, row(W["ln_mix_g"][i]), row(W["ln_mix_b"][i]),
                                             name=f"ln_mix_fwd{tag}")
        s["up"] = _mm(s["h1b"], W["ffn_w_up"][i], name=f"ffn_up{tag}")
        s["a"] = _ffn_act_fwd(s["up"], W["ffn_conv_w"][i], row(W["ffn_conv_b"][i]), name=f"ffn_act_fwd{tag}")
        ffn = _mm(s["a"], W["ffn_w_down"][i], name=f"ffn_down{tag}")
        s["z2"], s["h2"], s["h2b"] = _ln_fwd(s["h1"], ffn, row(W["ln_ffn_g"][i]), row(W["ln_ffn_b"][i]),
                                             name=f"ln_ffn_fwd{tag}")
        s["G"] = _mm(s["h2b"], W["ple_w_gate"][i], name=f"ple_gate_mm{tag}")
        s["pp"] = _mm(pb[i], W["ple_w_proj"][i], name=f"ple_proj_mm{tag}")
        out, outb = _ple_fwd(s["h2"], s["G"], row(W["ple_b_gate"][i]), s["pp"], name=f"ple_fwd{tag}")
        return out, outb, s

    def ffn_ple_bwd(i, dx, s, tag):
        g = {}
        dG, dpp, g["ple_b_gate"] = _ple_bwd(dx, s["G"], row(W["ple_b_gate"][i]), s["pp"], name=f"ple_bwd{tag}")
        g["ple_w_gate"] = _mm(s["h2b"], dG, ta=True, name=f"ple_gate_dw{tag}")
        g["ple_w_proj"] = _mm(pb[i], dpp, ta=True, name=f"ple_proj_dw{tag}")
        dh2 = _mm(dG, W["ple_w_gate"][i], tb=True, add=dx, name=f"ple_gate_dx{tag}")
        dz2, dz2b, g["ln_ffn_g"], g["ln_ffn_b"] = _ln_bwd(dh2, s["z2"], row(W["ln_ffn_g"][i]), name=f"ln_ffn_bwd{tag}")
        da = _mm(dz2b, W["ffn_w_down"][i], tb=True, out_dtype=BF16, name=f"ffn_down_dx{tag}")
        g["ffn_w_down"] = _mm(s["a"], dz2b, ta=True, name=f"ffn_down_dw{tag}")
        dup, dgc, g["ffn_conv_b"], g["ffn_conv_w"] = _ffn_act_bwd(
            da, s["up"], W["ffn_conv_w"][i], row(W["ffn_conv_b"][i]), name=f"ffn_act_bwd{tag}")
        dup = _dwconv_bwd_data(dgc, W["ffn_conv_w"][i], FFN_CONV, dup, FFN_DIM, name=f"ffn_conv_bwd{tag}")
        g["ffn_w_up"] = _mm(s["h1b"], dup, ta=True, name=f"ffn_up_dw{tag}")
        dh1 = _mm(dup, W["ffn_w_up"][i], tb=True, add=dz2, add_scale=DEEPNORM_ALPHA, name=f"ffn_up_dx{tag}")
        dz1, dz1b, g["ln_mix_g"], g["ln_mix_b"] = _ln_bwd(dh1, s["z1"], row(W["ln_mix_g"][i]), name=f"ln_mix_bwd{tag}")
        return dz1, dz1b, g

    xb = x.astype(BF16)
    pb = p.astype(BF16)
    proj0 = _mm(xb, attn_in, name="attn_in")
    c_col = _fox_gate_fwd(proj0, bf, name="fox_gate_fwd")
    cT = (c_col[:, :ATTN_HEADS] * LOG2E).T.reshape(HEAD_PAIRS, 2, T)
    qa, qb, kk, ka, kb, vv, va, vb = _attn_prep(proj0, name="attn_prep")
    o, ob, lsea, lseb = _attn_fwd(qa, qb, kk, va, vb, cT, name="attn_fwd")
    mix0 = _mm(ob, W["attn_w_out"][0], name="attn_out")
    x1, x1b, s0 = ffn_ple_fwd(0, x, mix0, "0")

    proj1 = _mm(x1b, ssm_in, name="ssm_in")
    dt, acum = _ssd_dt_fwd(proj1, dt_bias, a_log, name="ssd_dt_fwd")
    xbc = _conv_silu_fwd(proj1, W["ssm_conv_w"][0], row(W["ssm_conv_b"][0]), name="ssd_conv_fwd")
    dtr, acr = dt[:, :SSM_HEADS].T, acum[:, :SSM_HEADS].T
    ys, states = _ssd_scan_fwd(xbc, dt, acum, dtr, acr, name="ssd_scan_fwd")
    yn = _gate_norm_fwd(ys, xbc, proj1, d_exp, norm_w, name="ssd_gate_norm_fwd")
    mix1 = _mm(yn, W["ssm_w_out"][0], name="ssm_out")
    x2, _, s1 = ffn_ple_fwd(1, x1, mix1, "1")

    lpart, dy = _loss_head(x2, target, name="loss_head")
    loss = jnp.sum(lpart)

    dz1, dz1b, g1 = ffn_ple_bwd(1, dy, s1, "1")
    G["ssm_w_out"] = _mm(yn, dz1b, ta=True, name="ssm_out_dw")[None]
    dyn = _mm(dz1b, W["ssm_w_out"][0], tb=True, name="ssm_out_dx")
    dproj1, dys, dskip, dnw, dde = _gate_norm_bwd(dyn, ys, xbc, proj1, d_exp, norm_w, name="ssd_gate_norm_bwd")
    G["ssm_norm_w"] = dnw
    G["ssm_D"] = dde.reshape(SSM_HEADS, _PH).sum(axis=1)[None]
    dxbc, dac, dar, ddc, ddr = _ssd_scan_bwd(xbc, dys, dskip, states, dt, acum, dtr, acr, name="ssd_scan_bwd")
    dproj1, dal, ddb = _ssd_dt_bwd(dac, _from_row_layout(dar), ddc, _from_row_layout(ddr), dt, proj1, dt_bias, a_log,
                                   dproj1, name="ssd_dt_bwd")
    G["ssm_A_log"] = dal[:, :SSM_HEADS]
    G["ssm_dt_bias"] = ddb[:, :SSM_HEADS]
    dpre, G["ssm_conv_b"], dcw = _conv_silu_bwd(dxbc, proj1, W["ssm_conv_w"][0], row(W["ssm_conv_b"][0]),
                                                name="ssd_conv_bwd")
    G["ssm_conv_w"] = dcw[None]
    dproj1 = _dwconv_bwd_data(dpre, W["ssm_conv_w"][0], SSM_CONV, dproj1, SSM_D_INNER, name="ssd_conv_bwd_data")
    G["ssm_w_in"] = _mm(x1b, dproj1, ta=True, name="ssm_in_dw")[None, :, :W["ssm_w_in"].shape[2]]
    dx1 = _mm(dproj1, ssm_in, tb=True, add=dz1, add_scale=DEEPNORM_ALPHA, name="ssm_in_dx")

    dz0, dz0b, g0 = ffn_ple_bwd(0, dx1, s0, "0")
    G["attn_w_out"] = _mm(ob, dz0b, ta=True, name="attn_out_dw")[None]
    do = _mm(dz0b, W["attn_w_out"][0], tb=True, name="attn_out_dx")
    doa, dob, dlta, dltb = _attn_bwd_prep(do, o, name="attn_bwd_prep")
    dq, dk, dv, dcT, dcq = _attn_bwd(qa, qb, kk, ka, kb, vv, doa, dob, lsea, lseb, dlta, dltb, cT, name="attn_bwd")
    dc_col = jnp.pad(dcT.reshape(ATTN_HEADS, T).T + dcq[:, :ATTN_HEADS], ((0, 0), (0, LANE - ATTN_HEADS)))
    dproj0, dbf = _fox_gate_bwd(dc_col, proj0, bf, dq, dk, dv, name="fox_gate_bwd")
    G["attn_b_f"] = dbf[:, :ATTN_HEADS]
    G["attn_w_in"] = _mm(xb, dproj0, ta=True, name="attn_in_dw")[None, :, :W["attn_w_in"].shape[2]]
    grad_x = _mm(dproj0, attn_in, tb=True, add=dz0, add_scale=DEEPNORM_ALPHA, name="attn_in_dx")

    for k in g0:
        G[k] = jnp.stack([g0[k].reshape(W[k].shape[1:]), g1[k].reshape(W[k].shape[1:])])
    return loss, grad_x, G


def kernel(x, p, attn_w_in, attn_b_f, attn_w_out, ssm_w_in, ssm_conv_w, ssm_conv_b, ssm_dt_bias, ssm_A_log, ssm_D, ssm_norm_w, ssm_w_out, ln_mix_g, ln_mix_b, ffn_w_up, ffn_conv_w, ffn_conv_b, ffn_w_down, ln_ffn_g, ln_ffn_b, ple_w_proj, ple_w_gate, ple_b_gate, loss_target, m_attn_w_in, m_attn_b_f, m_attn_w_out, m_ssm_w_in, m_ssm_conv_w, m_ssm_conv_b, m_ssm_dt_bias, m_ssm_A_log, m_ssm_D, m_ssm_norm_w, m_ssm_w_out, m_ln_mix_g, m_ln_mix_b, m_ffn_w_up, m_ffn_conv_w, m_ffn_conv_b, m_ffn_w_down, m_ln_ffn_g, m_ln_ffn_b, m_ple_w_proj, m_ple_w_gate, m_ple_b_gate, v_attn_w_in, v_attn_b_f, v_attn_w_out, v_ssm_w_in, v_ssm_conv_w, v_ssm_conv_b, v_ssm_dt_bias, v_ssm_A_log, v_ssm_D, v_ssm_norm_w, v_ssm_w_out, v_ln_mix_g, v_ln_mix_b, v_ffn_w_up, v_ffn_conv_w, v_ffn_conv_b, v_ffn_w_down, v_ln_ffn_g, v_ln_ffn_b, v_ple_w_proj, v_ple_w_gate, v_ple_b_gate):
    names = [n for n, _ in WEIGHTS]
    axes = dict(WEIGHTS)
    w_loc = dict(zip(names, [attn_w_in, attn_b_f, attn_w_out, ssm_w_in, ssm_conv_w, ssm_conv_b, ssm_dt_bias, ssm_A_log, ssm_D, ssm_norm_w, ssm_w_out, ln_mix_g, ln_mix_b, ffn_w_up, ffn_conv_w, ffn_conv_b, ffn_w_down, ln_ffn_g, ln_ffn_b, ple_w_proj, ple_w_gate, ple_b_gate]))
    m_loc = dict(zip(names, [m_attn_w_in, m_attn_b_f, m_attn_w_out, m_ssm_w_in, m_ssm_conv_w, m_ssm_conv_b, m_ssm_dt_bias, m_ssm_A_log, m_ssm_D, m_ssm_norm_w, m_ssm_w_out, m_ln_mix_g, m_ln_mix_b, m_ffn_w_up, m_ffn_conv_w, m_ffn_conv_b, m_ffn_w_down, m_ln_ffn_g, m_ln_ffn_b, m_ple_w_proj, m_ple_w_gate, m_ple_b_gate]))
    v_loc = dict(zip(names, [v_attn_w_in, v_attn_b_f, v_attn_w_out, v_ssm_w_in, v_ssm_conv_w, v_ssm_conv_b, v_ssm_dt_bias, v_ssm_A_log, v_ssm_D, v_ssm_norm_w, v_ssm_w_out, v_ln_mix_g, v_ln_mix_b, v_ffn_w_up, v_ffn_conv_w, v_ffn_conv_b, v_ffn_w_down, v_ln_ffn_g, v_ln_ffn_b, v_ple_w_proj, v_ple_w_gate, v_ple_b_gate]))
    sharded = [n for n in names if axes[n] is not None]
    matrices = [n for n in sharded if n in MATMUL_WEIGHTS]

    def wire(n):
        if n in matrices:
            return w_loc[n].astype(BF16)
        return lax.bitcast_convert_type(w_loc[n], BF16)

    wired = [wire(n) for n in sharded]
    me_chip = 2 * lax.axis_index("x") + lax.axis_index("y")
    packed = _pack(wired)
    gathered = lax.dynamic_update_index_in_dim(_gather_chips(packed, name="gather_weights"), packed, me_chip, 0)
    W = dict(w_loc)
    per_chip = [_unpack(gathered[k], [w.shape for w in wired]) for k in range(N_CHIPS)]
    for i, n in enumerate(sharded):
        pieces = [per_chip[k][i] for k in range(N_CHIPS)]
        if n not in matrices:
            pieces = [lax.bitcast_convert_type(q, F32) for q in pieces]
        W[n] = jnp.concatenate(pieces, axis=axes[n])

    loss, grad_x, G = _local_step(x[0], p[:, 0], loss_target[0], W)
    loss = lax.psum(loss, ("x", "y", "c"))

    def slot(k):
        parts = []
        for n in names:
            g = G[n].reshape(W[n].shape)
            if axes[n] is not None:
                size = w_loc[n].shape[axes[n]]
                g = lax.slice_in_dim(g, k * size, (k + 1) * size, axis=axes[n])
            parts.append(g.astype(BF16))
        return _pack(parts)

    contrib = jnp.stack([slot(k) for k in range(N_CHIPS)])
    R = contrib.shape[1]
    H = R // 2
    c = lax.axis_index("c")
    keep = lax.dynamic_slice_in_dim(contrib, c * H, H, axis=1)
    pair = _add2(keep, _pair_swap(contrib, other_half=True, name="grad_pair_swap"), BF16, name="grad_pair_sum")
    from_chips = lax.dynamic_update_index_in_dim(
        _chip_exchange(pair, name="grad_chip_exchange"), lax.dynamic_index_in_dim(pair, me_chip, 0, keepdims=False),
        me_chip, 0)
    half = _sum4(from_chips, name="grad_chip_sum")
    other = _pair_swap(half, name="grad_half_swap")
    gflat = jnp.concatenate([jnp.where(c == 0, half, other), jnp.where(c == 0, other, half)])

    grads = _unpack(gflat, [w_loc[n].shape for n in names])
    steps = [_adamw(w_loc[n], g, m_loc[n], v_loc[n], name=f"adamw_{n}") for n, g in zip(names, grads)]
    return (loss, grad_x[None], *grads, *[s[0] for s in steps], *[s[1] for s in steps], *[s[2] for s in steps])
```

```python
import math

import jax
import jax.numpy as jnp
from jax import lax
from jax.experimental import pallas as pl
from jax.experimental.pallas import tpu as pltpu

F32 = jnp.float32
BF16 = jnp.bfloat16
MESH = pl.DeviceIdType.MESH

D_MODEL = 1024
ATTN_HEADS = 16
HEAD_PAIRS = ATTN_HEADS // 2
SSM_D_INNER = 2048
SSM_HEADS = 32
SSM_GROUPS = 8
SSM_STATE = 128
SSM_CONV = 4
SSM_CHUNK = 128
SSM_XBC = SSM_D_INNER + 2 * SSM_GROUPS * SSM_STATE
FFN_DIM = 2816
FFN_CONV = 3
DEPTH = 2
LN_EPS = 1e-5
RMS_EPS = 1e-5
DEEPNORM_ALPHA = (2 * DEPTH) ** 0.25
ADAM_LR = 0.001
ADAM_B1 = 0.9
ADAM_B2 = 0.999
ADAM_EPS = 1e-08
ADAM_WD = 0.01
ADAM_STEP = 10

LANE = 128
SUBLANE = 8
HALO = SUBLANE
HALO_BF16 = 2 * SUBLANE
NEG = -1e30
ATTN_IN_PAD = 3 * D_MODEL + LANE
SSM_IN_PAD = 2 * SSM_D_INNER + 2 * SSM_GROUPS * SSM_STATE + LANE
PACK_COLS = 1024
PACK_ROW_ALIGN = 512

ATTN_BLOCK = 1024
ROW_BLOCK = 512
NARROW_ROW_BLOCK = 2048
CUM_BLOCK = 512


def _params(sem, vmem_mb=48):
    return pltpu.CompilerParams(dimension_semantics=sem, vmem_limit_bytes=vmem_mb * 2 ** 20)


def _pick(n, target, mult=LANE):
    best = None
    d = mult
    while d <= min(n, target):
        if n % d == 0:
            best = d
        d += mult
    return n if best is None else best


def _sigmoid(x):
    return 1.0 / (1.0 + jnp.exp(-x))


def _log1p(u):
    w = 1.0 + u
    return jnp.where(w == 1.0, u, jnp.log(w) * (u / (w - 1.0)))


def _softplus(x):
    return jnp.maximum(x, 0.0) + _log1p(jnp.exp(-jnp.abs(x)))


def _split3(x):
    hi = x.astype(BF16)
    r1 = x - hi.astype(F32)
    mid = r1.astype(BF16)
    lo = (r1 - mid.astype(F32)).astype(BF16)
    return hi, mid, lo


def _tri_matmul(tri, x):
    out = None
    for part in _split3(x):
        t = jnp.dot(tri, part, preferred_element_type=F32)
        out = t if out is None else out + t
    return out


def _tri(n, lower):
    r = lax.broadcasted_iota(jnp.int32, (n, n), 0)
    c = lax.broadcasted_iota(jnp.int32, (n, n), 1)
    return jnp.where((c <= r) if lower else (c >= r), 1.0, 0.0).astype(BF16)


_ANY = pl.BlockSpec(memory_space=pl.ANY)
MM_OUT_BLOCK_BYTES = 13 * 2 ** 20
MM_IN_BLOCK_BYTES = 6 * 2 ** 20
MM_VMEM_LIMIT_MB = 56
MM_VMEM_BUDGET = 44 * 2 ** 20
MM_WIDE_K = 3200


def _mm(a, b, *, name, ta=False, tb=False, add=None, add_scale=1.0, out_dtype=F32):
    if ta:
        K, M = a.shape
    else:
        M, K = a.shape
    if tb:
        N, Kb = b.shape
    else:
        Kb, N = b.shape
    assert K == Kb, (a.shape, b.shape, ta, tb)
    if ta:
        assert add is None and out_dtype == F32
        bm = _pick(M, 2816)
        bn = _pick(N, MM_OUT_BLOCK_BYTES // (4 * bm))
        bk = _pick(K, max(512, MM_IN_BLOCK_BYTES // (2 * max(bm, bn))))
    else:
        bn = _pick(N, 1536 if K <= MM_WIDE_K else 512)
        bk = K
        bm = _pick(M, 2048)
        if 4 * bm * K + 4 * K * bn + (8 if add is None else 16) * bm * bn > MM_VMEM_BUDGET:
            bm = _pick(M, 1024)
    nk = K // bk
    a_spec = pl.BlockSpec((bk, bm), lambda i, j, k: (k, i)) if ta else pl.BlockSpec((bm, bk), lambda i, j, k: (i, k))
    b_spec = pl.BlockSpec((bn, bk), lambda i, j, k: (j, k)) if tb else pl.BlockSpec((bk, bn), lambda i, j, k: (k, j))
    o_spec = pl.BlockSpec((bm, bn), lambda i, j, k: (i, j))
    dims = (((0 if ta else 1,), (1 if tb else 0,)), ((), ()))
    has_add = add is not None

    def kern(*refs):
        a_ref, b_ref = refs[0], refs[1]
        add_ref = refs[2] if has_add else None
        o_ref = refs[3] if has_add else refs[2]
        k = pl.program_id(2)
        part = lax.dot_general(a_ref[...].astype(BF16), b_ref[...].astype(BF16), dims, preferred_element_type=F32)
        if nk == 1:
            o_ref[...] = (part + add_scale * add_ref[...] if has_add else part).astype(out_dtype)
        else:
            @pl.when(k == 0)
            def _():
                o_ref[...] = part

            @pl.when(k > 0)
            def _():
                o_ref[...] += part

    ins = [a, b] + ([add] if has_add else [])
    in_specs = [a_spec, b_spec] + ([o_spec] if has_add else [])
    return pl.pallas_call(
        kern, name=name, grid=(M // bm, N // bn, nk),
        in_specs=in_specs, out_specs=o_spec,
        out_shape=jax.ShapeDtypeStruct((M, N), out_dtype),
        compiler_params=_params(("parallel", "parallel", "arbitrary"), vmem_mb=MM_VMEM_LIMIT_MB),
    )(*ins)


def _ln_stats(z):
    mu = jnp.mean(z, axis=-1, keepdims=True)
    zc = z - mu
    var = jnp.mean(zc * zc, axis=-1, keepdims=True)
    return zc, lax.rsqrt(var + LN_EPS)


def _ln_fwd(x, r, g, b, *, name):
    T, D = x.shape
    bt = _pick(T, ROW_BLOCK, SUBLANE)

    def kern(x_ref, r_ref, g_ref, b_ref, z_ref, h_ref, hb_ref):
        z = DEEPNORM_ALPHA * x_ref[...] + r_ref[...]
        zc, rstd = _ln_stats(z)
        h = zc * rstd * g_ref[...] + b_ref[...]
        z_ref[...] = z
        h_ref[...] = h
        hb_ref[...] = h.astype(BF16)

    row = pl.BlockSpec((bt, D), lambda i: (i, 0))
    vec = pl.BlockSpec((1, D), lambda i: (0, 0))
    return pl.pallas_call(
        kern, name=name, grid=(T // bt,), in_specs=[row, row, vec, vec], out_specs=[row, row, row],
        out_shape=[jax.ShapeDtypeStruct((T, D), F32)] * 2 + [jax.ShapeDtypeStruct((T, D), BF16)],
        compiler_params=_params(("parallel",)),
    )(x, r, g, b)


def _ln_bwd(dy, z, g, *, name):
    T, D = z.shape
    bt = _pick(T, ROW_BLOCK, SUBLANE)

    def kern(dy_ref, z_ref, g_ref, dz_ref, dzb_ref, dg_ref, db_ref):
        i = pl.program_id(0)
        zc, rstd = _ln_stats(z_ref[...])
        xhat = zc * rstd
        dyv = dy_ref[...]
        dxh = dyv * g_ref[...]
        m1 = jnp.mean(dxh, axis=-1, keepdims=True)
        m2 = jnp.mean(dxh * xhat, axis=-1, keepdims=True)
        dz = rstd * (dxh - m1 - xhat * m2)
        dz_ref[...] = dz
        dzb_ref[...] = dz.astype(BF16)

        @pl.when(i == 0)
        def _():
            dg_ref[...] = jnp.zeros_like(dg_ref)
            db_ref[...] = jnp.zeros_like(db_ref)

        dg_ref[...] += jnp.sum(dyv * xhat, axis=0, keepdims=True)
        db_ref[...] += jnp.sum(dyv, axis=0, keepdims=True)

    row = pl.BlockSpec((bt, D), lambda i: (i, 0))
    vec = pl.BlockSpec((1, D), lambda i: (0, 0))
    return pl.pallas_call(
        kern, name=name, grid=(T // bt,), in_specs=[row, row, vec], out_specs=[row, row, vec, vec],
        out_shape=[jax.ShapeDtypeStruct((T, D), F32), jax.ShapeDtypeStruct((T, D), BF16),
                   jax.ShapeDtypeStruct((1, D), F32), jax.ShapeDtypeStruct((1, D), F32)],
        compiler_params=_params(("arbitrary",)),
    )(dy, z, g)


def _past_taps(ext_ref, K, bt):
    ext = ext_ref[...]
    return [(ext if k == K - 1 else pltpu.roll(ext, K - 1 - k, 0))[HALO:HALO + bt] for k in range(K)]


def _conv_past(taps, cw_ref):
    out = None
    for k, tap in enumerate(taps):
        term = cw_ref[k:k + 1, :] * tap
        out = term if out is None else out + term
    return out


def _fill_ext_past(ext_ref, halo_ref, cur, i, bt):
    ext_ref[pl.ds(0, HALO), :] = jnp.where(i > 0, halo_ref[...], 0.0)
    ext_ref[pl.ds(HALO, bt), :] = cur


def _halo_prev(bt, bc, off):
    return pl.BlockSpec((HALO, bc), lambda i, j: (jnp.maximum(i * (bt // HALO) - 1, 0), j + off))


def _normal_cdf(x):
    return 0.5 * (1.0 + lax.erf(x * (1.0 / math.sqrt(2.0))))


def _gelu(x):
    return x * _normal_cdf(x)


def _gelu_and_grad(x):
    cdf = _normal_cdf(x)
    return x * cdf, cdf + x * jnp.exp(-0.5 * x * x) * (1.0 / math.sqrt(2.0 * math.pi))


def _ffn_act_fwd(up, cw, cb, *, name):
    T, F2 = up.shape
    F = F2 // 2
    bt = _pick(T, ROW_BLOCK, SUBLANE)
    bc = _pick(F, 1408)
    nb = F // bc

    def kern(u_ref, g_ref, halo_ref, cw_ref, cb_ref, a_ref, ext_ref):
        i = pl.program_id(0)
        _fill_ext_past(ext_ref, halo_ref, g_ref[...], i, bt)
        gc = cb_ref[...] + _conv_past(_past_taps(ext_ref, FFN_CONV, bt), cw_ref)
        a_ref[...] = (_gelu(gc) * u_ref[...]).astype(BF16)

    return pl.pallas_call(
        kern, name=name, grid=(T // bt, nb),
        in_specs=[pl.BlockSpec((bt, bc), lambda i, j: (i, j)),
                  pl.BlockSpec((bt, bc), lambda i, j: (i, j + nb)),
                  _halo_prev(bt, bc, nb),
                  pl.BlockSpec((FFN_CONV, bc), lambda i, j: (0, j)),
                  pl.BlockSpec((1, bc), lambda i, j: (0, j))],
        out_specs=pl.BlockSpec((bt, bc), lambda i, j: (i, j)),
        out_shape=jax.ShapeDtypeStruct((T, F), BF16),
        scratch_shapes=[pltpu.VMEM((bt + HALO, bc), F32)],
        compiler_params=_params(("parallel", "parallel")),
    )(up, up, up, cw, cb)


def _ffn_act_bwd(da, up, cw, cb, *, name):
    T, F2 = up.shape
    F = F2 // 2
    bt = _pick(T, ROW_BLOCK, SUBLANE)
    bc = _pick(F, 1408)
    nb = F // bc
    K = FFN_CONV

    def kern(da_ref, u_ref, g_ref, halo_ref, cw_ref, cb_ref, du_ref, dgc_ref, dcb_ref, dcw_ref, ext_ref):
        i = pl.program_id(1)
        _fill_ext_past(ext_ref, halo_ref, g_ref[...], i, bt)
        taps = _past_taps(ext_ref, K, bt)
        gc = cb_ref[...] + _conv_past(taps, cw_ref)
        dav = da_ref[...]
        act, act_grad = _gelu_and_grad(gc)
        du_ref[...] = (dav * act).astype(BF16)
        dgc = dav * u_ref[...] * act_grad
        dgc_ref[...] = dgc.astype(BF16)

        @pl.when(i == 0)
        def _():
            dcb_ref[...] = jnp.zeros_like(dcb_ref)
            dcw_ref[...] = jnp.zeros_like(dcw_ref)

        dcb_ref[...] += jnp.sum(dgc, axis=0, keepdims=True)
        for k in range(K):
            dcw_ref[k:k + 1, :] += jnp.sum(dgc * taps[k], axis=0, keepdims=True)

    blk = pl.BlockSpec((bt, bc), lambda j, i: (i, j))
    return pl.pallas_call(
        kern, name=name, grid=(nb, T // bt),
        in_specs=[blk, blk,
                  pl.BlockSpec((bt, bc), lambda j, i: (i, j + nb)),
                  pl.BlockSpec((HALO, bc), lambda j, i: (jnp.maximum(i * (bt // HALO) - 1, 0), j + nb)),
                  pl.BlockSpec((K, bc), lambda j, i: (0, j)),
                  pl.BlockSpec((1, bc), lambda j, i: (0, j))],
        out_specs=[blk, blk, pl.BlockSpec((1, bc), lambda j, i: (0, j)), pl.BlockSpec((K, bc), lambda j, i: (0, j))],
        out_shape=[jax.ShapeDtypeStruct((T, F2), BF16), jax.ShapeDtypeStruct((T, F), BF16),
                   jax.ShapeDtypeStruct((1, F), F32), jax.ShapeDtypeStruct((K, F), F32)],
        scratch_shapes=[pltpu.VMEM((bt + HALO, bc), F32)],
        compiler_params=_params(("parallel", "arbitrary")),
    )(da, up, up, up, cw, cb)


def _dwconv_bwd_data(dgc, cw, K, into, col, *, name):
    T, C = dgc.shape
    bt = _pick(T, ROW_BLOCK, SUBLANE)
    bc = _pick(C, 1408)
    nt = T // bt
    halo = HALO_BF16 if dgc.dtype == BF16 else HALO
    last_halo = T // halo - 1
    off = col // bc
    assert off * bc == col

    def kern(d_ref, halo_ref, cw_ref, into_ref, o_ref, ext_ref):
        i = pl.program_id(0)
        ext_ref[pl.ds(0, bt), :] = d_ref[...].astype(F32)
        ext_ref[pl.ds(bt, halo), :] = jnp.where(i < nt - 1, halo_ref[...].astype(F32), 0.0)
        ext = ext_ref[...]
        out = None
        for k in range(K):
            ahead = K - 1 - k
            tap = (ext if ahead == 0 else pltpu.roll(ext, bt + halo - ahead, 0))[0:bt]
            term = cw_ref[k:k + 1, :] * tap
            out = term if out is None else out + term
        o_ref[...] = out.astype(o_ref.dtype)

    return pl.pallas_call(
        kern, name=name, grid=(nt, C // bc),
        in_specs=[pl.BlockSpec((bt, bc), lambda i, j: (i, j)),
                  pl.BlockSpec((halo, bc), lambda i, j: (jnp.minimum((i + 1) * (bt // halo), last_halo), j)),
                  pl.BlockSpec((K, bc), lambda i, j: (0, j)), _ANY],
        out_specs=pl.BlockSpec((bt, bc), lambda i, j: (i, j + off)),
        out_shape=jax.ShapeDtypeStruct(into.shape, into.dtype), input_output_aliases={3: 0},
        scratch_shapes=[pltpu.VMEM((bt + halo, bc), F32)],
        compiler_params=_params(("parallel", "parallel")),
    )(dgc, dgc, cw, into)


def _ple_fwd(h, G, bg, pp, *, name):
    T, D = h.shape
    bt = _pick(T, ROW_BLOCK, SUBLANE)

    def kern(h_ref, G_ref, bg_ref, pp_ref, o_ref, ob_ref):
        out = h_ref[...] + _sigmoid(G_ref[...] + bg_ref[...]) * pp_ref[...]
        o_ref[...] = out
        ob_ref[...] = out.astype(BF16)

    row = pl.BlockSpec((bt, D), lambda i: (i, 0))
    vec = pl.BlockSpec((1, D), lambda i: (0, 0))
    return pl.pallas_call(
        kern, name=name, grid=(T // bt,), in_specs=[row, row, vec, row], out_specs=[row, row],
        out_shape=[jax.ShapeDtypeStruct((T, D), F32), jax.ShapeDtypeStruct((T, D), BF16)],
        compiler_params=_params(("parallel",)),
    )(h, G, bg, pp)


def _ple_bwd(dx, G, bg, pp, *, name):
    T, D = dx.shape
    bt = _pick(T, ROW_BLOCK, SUBLANE)

    def kern(dx_ref, G_ref, bg_ref, pp_ref, dG_ref, dpp_ref, dbg_ref):
        i = pl.program_id(0)
        gate = _sigmoid(G_ref[...] + bg_ref[...])
        dxv = dx_ref[...]
        dG = dxv * pp_ref[...] * gate * (1.0 - gate)
        dG_ref[...] = dG.astype(BF16)
        dpp_ref[...] = (dxv * gate).astype(BF16)

        @pl.when(i == 0)
        def _():
            dbg_ref[...] = jnp.zeros_like(dbg_ref)

        dbg_ref[...] += jnp.sum(dG, axis=0, keepdims=True)

    row = pl.BlockSpec((bt, D), lambda i: (i, 0))
    vec = pl.BlockSpec((1, D), lambda i: (0, 0))
    return pl.pallas_call(
        kern, name=name, grid=(T // bt,), in_specs=[row, row, vec, row], out_specs=[row, row, vec],
        out_shape=[jax.ShapeDtypeStruct((T, D), BF16), jax.ShapeDtypeStruct((T, D), BF16),
                   jax.ShapeDtypeStruct((1, D), F32)],
        compiler_params=_params(("arbitrary",)),
    )(dx, G, bg, pp)


def _fox_gate_fwd(proj, bf, *, name):
    T = proj.shape[0]
    bt = _pick(T, CUM_BLOCK, SUBLANE)
    fcol = 3 * D_MODEL // LANE

    def kern(f_ref, bf_ref, c_ref, carry_ref):
        i = pl.program_id(0)

        @pl.when(i == 0)
        def _():
            carry_ref[...] = jnp.zeros_like(carry_ref)

        x = f_ref[...] + bf_ref[...]
        lf = jnp.minimum(x, 0.0) - _log1p(jnp.exp(-jnp.abs(x)))
        cs = _tri_matmul(_tri(bt, True), lf) + carry_ref[...]
        c_ref[...] = cs
        carry_ref[...] = cs[bt - 1:bt, :]

    return pl.pallas_call(
        kern, name=name, grid=(T // bt,),
        in_specs=[pl.BlockSpec((bt, LANE), lambda i: (i, fcol)), pl.BlockSpec((1, LANE), lambda i: (0, 0))],
        out_specs=pl.BlockSpec((bt, LANE), lambda i: (i, 0)),
        out_shape=jax.ShapeDtypeStruct((T, LANE), F32),
        scratch_shapes=[pltpu.VMEM((1, LANE), F32)],
        compiler_params=_params(("arbitrary",)),
    )(proj, bf)


def _fox_gate_bwd(dc, proj, bf, dq, dk, dv, *, name):
    T = proj.shape[0]
    bt = _pick(T, CUM_BLOCK, SUBLANE)
    nb = T // bt
    fcol = 3 * D_MODEL // LANE

    def kern(dc_ref, f_ref, bf_ref, dq_ref, dk_ref, dv_ref, dproj_ref, dbf_ref, carry_ref):
        i = pl.program_id(0)

        @pl.when(i == 0)
        def _():
            carry_ref[...] = jnp.zeros_like(carry_ref)
            dbf_ref[...] = jnp.zeros_like(dbf_ref)

        dlf = _tri_matmul(_tri(bt, False), dc_ref[...]) + carry_ref[...]
        carry_ref[...] = dlf[0:1, :]
        x = f_ref[...] + bf_ref[...]
        lane = lax.broadcasted_iota(jnp.int32, (bt, LANE), 1)
        df = jnp.where(lane < ATTN_HEADS, dlf / (1.0 + jnp.exp(x)), 0.0)
        dbf_ref[...] += jnp.sum(df, axis=0, keepdims=True)
        for n, part_ref in enumerate((dq_ref, dk_ref, dv_ref)):
            dproj_ref[:, n * D_MODEL:(n + 1) * D_MODEL] = part_ref[...].astype(BF16)
        dproj_ref[:, 3 * D_MODEL:] = df.astype(BF16)

    rows = lambda i: (nb - 1 - i, 0)
    wide = pl.BlockSpec((bt, D_MODEL), rows)
    return pl.pallas_call(
        kern, name=name, grid=(nb,),
        in_specs=[pl.BlockSpec((bt, LANE), rows), pl.BlockSpec((bt, LANE), lambda i: (nb - 1 - i, fcol)),
                  pl.BlockSpec((1, LANE), lambda i: (0, 0)), wide, wide, wide],
        out_specs=[pl.BlockSpec((bt, ATTN_IN_PAD), rows), pl.BlockSpec((1, LANE), lambda i: (0, 0))],
        out_shape=[jax.ShapeDtypeStruct((T, ATTN_IN_PAD), BF16), jax.ShapeDtypeStruct((1, LANE), F32)],
        scratch_shapes=[pltpu.VMEM((1, LANE), F32)],
        compiler_params=_params(("arbitrary",)),
    )(dc, proj, bf, dq, dk, dv)


_NT = (((1,), (1,)), ((), ()))
_TN = (((0,), (0,)), ((), ()))


def _dot(a, b, dims=None):
    if dims is None:
        return jnp.dot(a, b, preferred_element_type=F32)
    return lax.dot_general(a, b, dims, preferred_element_type=F32)


LOG2E = 1.0 / math.log(2.0)
LN2 = math.log(2.0)
Q_SCALE = 0.125 * LOG2E
HALF = LANE // 2
L_LANE = (HALF, 0)
FWD_PAIRS = 4
BWD_PAIRS = 1
ATTN_BWD_VMEM_MB = 56


def _attn_prep(proj, *, name):
    T = proj.shape[0]
    bt = _pick(T, ATTN_BLOCK)

    def kern(q_ref, k_ref, v_ref, qa_ref, qb_ref, kk_ref, ka_ref, kb_ref, vv_ref, va_ref, vb_ref):
        lane = lax.broadcasted_iota(jnp.int32, (bt, LANE), 1)
        lo = lane < HALF
        q = q_ref[...] * Q_SCALE
        k = k_ref[...]
        v = v_ref[...]
        qa_ref[...] = jnp.where(lo, q, 0.0).astype(BF16)
        qb_ref[...] = jnp.where(lo, 0.0, q).astype(BF16)
        kk_ref[...] = k.astype(BF16)
        ka_ref[...] = jnp.where(lo, k, 0.0).astype(BF16)
        kb_ref[...] = jnp.where(lo, 0.0, k).astype(BF16)
        vv_ref[...] = v.astype(BF16)
        va_ref[...] = jnp.where(lo, v, jnp.where(lane == L_LANE[0], 1.0, 0.0)).astype(BF16)
        vb_ref[...] = jnp.where(lo, jnp.where(lane == L_LANE[1], 1.0, 0.0), v).astype(BF16)

    kcol, vcol = D_MODEL // LANE, 2 * D_MODEL // LANE
    out = pl.BlockSpec((bt, LANE), lambda i, hp: (i, hp))
    return pl.pallas_call(
        kern, name=name, grid=(T // bt, HEAD_PAIRS),
        in_specs=[out, pl.BlockSpec((bt, LANE), lambda i, hp: (i, kcol + hp)),
                  pl.BlockSpec((bt, LANE), lambda i, hp: (i, vcol + hp))],
        out_specs=[out] * 8, out_shape=[jax.ShapeDtypeStruct((T, D_MODEL), BF16)] * 8,
        compiler_params=_params(("parallel", "parallel")),
    )(proj, proj, proj)


def _attn_fwd(qa, qb, kk, va, vb, cT, *, name):
    T = qa.shape[0]
    tb = _pick(T, ATTN_BLOCK)
    nq = T // tb
    rep = tb // LANE
    width = FWD_PAIRS * LANE

    def kern(qa_ref, qb_ref, k_ref, va_ref, vb_ref, c_ref, o_ref, ob_ref, lsea_ref, lseb_ref, m_ref, acc_ref):
        qi = pl.program_id(1)
        ki = pl.program_id(2)

        @pl.when(ki == 0)
        def _():
            m_ref[...] = jnp.full_like(m_ref, NEG)
            acc_ref[...] = jnp.zeros_like(acc_ref)

        def step(diag):
            for pp in range(FWD_PAIRS):
                cols = slice(pp * LANE, (pp + 1) * LANE)
                k = k_ref[:, cols]
                for h, (q_ref, v_ref) in enumerate(((qa_ref, va_ref), (qb_ref, vb_ref))):
                    i = 2 * pp + h
                    s = _dot(q_ref[:, cols], k, _NT) - c_ref[pp, h:h + 1, :]
                    if diag:
                        r = lax.broadcasted_iota(jnp.int32, (tb, tb), 0)
                        c = lax.broadcasted_iota(jnp.int32, (tb, tb), 1)
                        s = jnp.where(c <= r, s, NEG)
                    m_prev = m_ref[i]
                    m_new = jnp.maximum(m_prev, jnp.max(s, axis=1, keepdims=True))
                    p = jnp.exp2(s - jnp.tile(m_new, (1, rep)))
                    acc_ref[i] = acc_ref[i] * jnp.exp2(m_prev - m_new) + _dot(p.astype(BF16), v_ref[:, cols])
                    m_ref[i] = m_new

        @pl.when(ki < qi)
        def _():
            step(False)

        @pl.when(ki == qi)
        def _():
            step(True)
            lo = lax.broadcasted_iota(jnp.int32, (tb, LANE), 1) < HALF
            for pp in range(FWD_PAIRS):
                cols = slice(pp * LANE, (pp + 1) * LANE)
                a0, a1 = acc_ref[2 * pp], acc_ref[2 * pp + 1]
                l0 = a0[:, L_LANE[0]:L_LANE[0] + 1]
                l1 = a1[:, L_LANE[1]:L_LANE[1] + 1]
                o = jnp.where(lo, a0 / l0, a1 / l1)
                o_ref[:, cols] = o
                ob_ref[:, cols] = o.astype(BF16)
                lsea_ref[:, cols] = m_ref[2 * pp] + jnp.log(l0) * LOG2E
                lseb_ref[:, cols] = m_ref[2 * pp + 1] + jnp.log(l1) * LOG2E

    qspec = pl.BlockSpec((tb, width), lambda g, qi, ki: (qi, g))
    kspec = pl.BlockSpec((tb, width), lambda g, qi, ki: (jnp.minimum(ki, qi), g))
    return pl.pallas_call(
        kern, name=name, grid=(HEAD_PAIRS // FWD_PAIRS, nq, nq),
        in_specs=[qspec, qspec, kspec, kspec, kspec,
                  pl.BlockSpec((FWD_PAIRS, 2, tb), lambda g, qi, ki: (g, 0, jnp.minimum(ki, qi)))],
        out_specs=[qspec, qspec, qspec, qspec],
        out_shape=[jax.ShapeDtypeStruct((T, D_MODEL), F32), jax.ShapeDtypeStruct((T, D_MODEL), BF16),
                   jax.ShapeDtypeStruct((T, D_MODEL), F32), jax.ShapeDtypeStruct((T, D_MODEL), F32)],
        scratch_shapes=[pltpu.VMEM((2 * FWD_PAIRS, tb, LANE), F32), pltpu.VMEM((2 * FWD_PAIRS, tb, LANE), F32)],
        compiler_params=_params(("parallel", "parallel", "arbitrary")),
    )(qa, qb, kk, va, vb, cT)


def _attn_bwd_prep(do, o, *, name):
    T, D = do.shape
    bt = _pick(T, ATTN_BLOCK)

    def kern(do_ref, o_ref, doa_ref, dob_ref, dlta_ref, dltb_ref):
        lo = lax.broadcasted_iota(jnp.int32, (bt, LANE), 1) < HALF
        dov = do_ref[...]
        prod = dov * o_ref[...]
        doa_ref[...] = jnp.where(lo, dov, 0.0).astype(BF16)
        dob_ref[...] = jnp.where(lo, 0.0, dov).astype(BF16)
        dlta_ref[...] = jnp.broadcast_to(jnp.sum(jnp.where(lo, prod, 0.0), axis=1, keepdims=True), (bt, LANE))
        dltb_ref[...] = jnp.broadcast_to(jnp.sum(jnp.where(lo, 0.0, prod), axis=1, keepdims=True), (bt, LANE))

    blk = pl.BlockSpec((bt, LANE), lambda i, hp: (i, hp))
    return pl.pallas_call(
        kern, name=name, grid=(T // bt, HEAD_PAIRS), in_specs=[blk, blk], out_specs=[blk] * 4,
        out_shape=[jax.ShapeDtypeStruct((T, D), BF16)] * 2 + [jax.ShapeDtypeStruct((T, D), F32)] * 2,
        compiler_params=_params(("parallel", "parallel")),
    )(do, o)


def _attn_bwd(qa, qb, kk, ka, kb, vv, doa, dob, lsea, lseb, dlta, dltb, cT, *, name):
    T = qa.shape[0]
    tb = _pick(T, ATTN_BLOCK)
    nq = T // tb
    rep = tb // LANE
    width = BWD_PAIRS * LANE

    def kern(qa_ref, qb_ref, k_ref, ka_ref, kb_ref, v_ref, doa_ref, dob_ref, lsea_ref, lseb_ref, dlta_ref, dltb_ref,
             c_ref, dq_ref, dk_ref, dv_ref, dc_ref, dcq_ref):
        g = pl.program_id(0)
        ki = pl.program_id(1)
        qi = pl.program_id(2)
        first = jnp.logical_and(ki == 0, qi == 0)

        @pl.when(first)
        def _():
            dq_ref[...] = jnp.zeros_like(dq_ref)

        @pl.when(jnp.logical_and(first, g == 0))
        def _():
            dcq_ref[...] = jnp.zeros_like(dcq_ref)

        @pl.when(qi == 0)
        def _():
            dk_ref[...] = jnp.zeros_like(dk_ref)
            dv_ref[...] = jnp.zeros_like(dv_ref)
            dc_ref[...] = jnp.zeros_like(dc_ref)

        def step(diag):
            rows = pl.ds(pl.multiple_of(qi * tb, tb), tb)
            lane = lax.broadcasted_iota(jnp.int32, (tb, LANE), 1)
            row_sums = jnp.zeros((tb, LANE), F32)
            for pp in range(BWD_PAIRS):
                cols = slice(pp * LANE, (pp + 1) * LANE)
                k = k_ref[:, cols]
                v = v_ref[:, cols]
                dq = None
                dk = None
                dv = None
                heads = ((qa_ref, ka_ref, doa_ref, lsea_ref, dlta_ref), (qb_ref, kb_ref, dob_ref, lseb_ref, dltb_ref))
                for h, (q_ref, km_ref, do_ref, lse_ref, dlt_ref) in enumerate(heads):
                    q = q_ref[:, cols]
                    dom = do_ref[:, cols]
                    s = _dot(q, k, _NT) - c_ref[pp, h:h + 1, :]
                    if diag:
                        r = lax.broadcasted_iota(jnp.int32, (tb, tb), 0)
                        c = lax.broadcasted_iota(jnp.int32, (tb, tb), 1)
                        s = jnp.where(c <= r, s, NEG)
                    p = jnp.exp2(s - jnp.tile(lse_ref[:, cols], (1, rep)))
                    ds = p * (_dot(dom, v, _NT) - jnp.tile(dlt_ref[:, cols], (1, rep)))
                    dc_ref[pp, h:h + 1, :] -= jnp.sum(ds, axis=0, keepdims=True)
                    head = 2 * (BWD_PAIRS * g + pp) + h
                    row_sums = jnp.where(lane == head, jnp.sum(ds, axis=1, keepdims=True), row_sums)
                    dsb = ds.astype(BF16)
                    tv = _dot(p.astype(BF16), dom, _TN)
                    tk = _dot(dsb, q, _TN)
                    tq = _dot(dsb, km_ref[:, cols])
                    dv = tv if dv is None else dv + tv
                    dk = tk if dk is None else dk + tk
                    dq = tq if dq is None else dq + tq
                dv_ref[:, cols] += dv
                dk_ref[:, cols] += dk * LN2
                dq_ref[rows, cols] += dq * 0.125
            dcq_ref[rows, :] += row_sums

        @pl.when(qi > ki)
        def _():
            step(False)

        @pl.when(qi == ki)
        def _():
            step(True)

    qspec = pl.BlockSpec((tb, width), lambda g, ki, qi: (jnp.maximum(qi, ki), g))
    kspec = pl.BlockSpec((tb, width), lambda g, ki, qi: (ki, g))
    cspec = pl.BlockSpec((BWD_PAIRS, 2, tb), lambda g, ki, qi: (g, 0, ki))
    qacc = pl.BlockSpec((T, width), lambda g, ki, qi: (0, g), pipeline_mode=pl.Buffered(1))
    cqacc = pl.BlockSpec((T, LANE), lambda g, ki, qi: (0, 0), pipeline_mode=pl.Buffered(1))
    return pl.pallas_call(
        kern, name=name, grid=(HEAD_PAIRS // BWD_PAIRS, nq, nq),
        in_specs=[qspec, qspec, kspec, kspec, kspec, kspec, qspec, qspec, qspec, qspec, qspec, qspec, cspec],
        out_specs=[qacc, kspec, kspec, cspec, cqacc],
        out_shape=[jax.ShapeDtypeStruct((T, D_MODEL), F32)] * 3 + [jax.ShapeDtypeStruct((HEAD_PAIRS, 2, T), F32),
                                                                   jax.ShapeDtypeStruct((T, LANE), F32)],
        compiler_params=_params(("arbitrary", "arbitrary", "arbitrary"), vmem_mb=ATTN_BWD_VMEM_MB),
    )(qa, qb, kk, ka, kb, vv, doa, dob, lsea, lseb, dlta, dltb, cT)


def _ssd_dt_fwd(proj, dt_bias, a_log, *, name):
    T = proj.shape[0]
    Q = SSM_CHUNK
    col = (2 * SSM_D_INNER + 2 * SSM_GROUPS * SSM_STATE) // LANE

    def kern(raw_ref, b_ref, al_ref, dt_ref, ac_ref):
        dt = _softplus(raw_ref[...] + b_ref[...])
        dt_ref[...] = dt
        ac_ref[...] = _tri_matmul(_tri(Q, True), dt * (-jnp.exp(al_ref[...])))

    vec = pl.BlockSpec((1, LANE), lambda i: (0, 0))
    blk = pl.BlockSpec((Q, LANE), lambda i: (i, 0))
    return pl.pallas_call(
        kern, name=name, grid=(T // Q,),
        in_specs=[pl.BlockSpec((Q, LANE), lambda i: (i, col)), vec, vec], out_specs=[blk, blk],
        out_shape=[jax.ShapeDtypeStruct((T, LANE), F32)] * 2,
        compiler_params=_params(("parallel",)),
    )(proj, dt_bias, a_log)


def _ssd_dt_bwd(da_a, da_b, ddt_a, ddt_b, dt, proj, dt_bias, a_log, into, *, name):
    T = proj.shape[0]
    Q = SSM_CHUNK
    col = (2 * SSM_D_INNER + 2 * SSM_GROUPS * SSM_STATE) // LANE

    def kern(daa_ref, dab_ref, dda_ref, ddb_ref, dt_ref, raw_ref, b_ref, al_ref, into_ref, draw_ref, dal_ref, db_ref,
             acc_ref):
        i = pl.program_id(0)

        @pl.when(i == 0)
        def _():
            acc_ref[...] = jnp.zeros_like(acc_ref)
            db_ref[...] = jnp.zeros_like(db_ref)

        A = -jnp.exp(al_ref[...])
        ddA = _tri_matmul(_tri(Q, False), daa_ref[...] + dab_ref[...])
        ddt = dda_ref[...] + ddb_ref[...] + ddA * A
        acc_ref[...] += jnp.sum(ddA * dt_ref[...], axis=0, keepdims=True)
        lane = lax.broadcasted_iota(jnp.int32, (Q, LANE), 1)
        draw = jnp.where(lane < SSM_HEADS, ddt * _sigmoid(raw_ref[...] + b_ref[...]), 0.0)
        draw_ref[...] = draw.astype(BF16)
        db_ref[...] += jnp.sum(draw, axis=0, keepdims=True)
        dal_ref[...] = acc_ref[...] * A

    vec = pl.BlockSpec((1, LANE), lambda i: (0, 0))
    blk = pl.BlockSpec((Q, LANE), lambda i: (i, 0))
    return pl.pallas_call(
        kern, name=name, grid=(T // Q,),
        in_specs=[blk, blk, blk, blk, blk, pl.BlockSpec((Q, LANE), lambda i: (i, col)), vec, vec, _ANY],
        out_specs=[pl.BlockSpec((Q, LANE), lambda i: (i, col)), vec, vec],
        out_shape=[jax.ShapeDtypeStruct(into.shape, into.dtype), jax.ShapeDtypeStruct((1, LANE), F32),
                   jax.ShapeDtypeStruct((1, LANE), F32)],
        input_output_aliases={8: 0},
        scratch_shapes=[pltpu.VMEM((1, LANE), F32)],
        compiler_params=_params(("arbitrary",)),
    )(da_a, da_b, ddt_a, ddt_b, dt, proj, dt_bias, a_log, into)


def _conv_silu_fwd(proj, cw, cb, *, name):
    T = proj.shape[0]
    C = SSM_XBC
    bt = _pick(T, ROW_BLOCK, SUBLANE)
    bc = 1024
    off = SSM_D_INNER // bc

    def kern(x_ref, halo_ref, cw_ref, cb_ref, o_ref, ext_ref):
        i = pl.program_id(0)
        _fill_ext_past(ext_ref, halo_ref, x_ref[...], i, bt)
        pre = cb_ref[...] + _conv_past(_past_taps(ext_ref, SSM_CONV, bt), cw_ref)
        o_ref[...] = pre * _sigmoid(pre)

    return pl.pallas_call(
        kern, name=name, grid=(T // bt, C // bc),
        in_specs=[pl.BlockSpec((bt, bc), lambda i, j: (i, j + off)), _halo_prev(bt, bc, off),
                  pl.BlockSpec((SSM_CONV, bc), lambda i, j: (0, j)), pl.BlockSpec((1, bc), lambda i, j: (0, j))],
        out_specs=pl.BlockSpec((bt, bc), lambda i, j: (i, j)),
        out_shape=jax.ShapeDtypeStruct((T, C), F32),
        scratch_shapes=[pltpu.VMEM((bt + HALO, bc), F32)],
        compiler_params=_params(("parallel", "parallel")),
    )(proj, proj, cw, cb)


def _conv_silu_bwd(dxbc, proj, cw, cb, *, name):
    T = proj.shape[0]
    C = SSM_XBC
    K = SSM_CONV
    bt = _pick(T, ROW_BLOCK, SUBLANE)
    bc = 1024
    off = SSM_D_INNER // bc

    def kern(d_ref, x_ref, halo_ref, cw_ref, cb_ref, dpre_ref, dcb_ref, dcw_ref, ext_ref):
        i = pl.program_id(1)
        _fill_ext_past(ext_ref, halo_ref, x_ref[...], i, bt)
        taps = _past_taps(ext_ref, K, bt)
        pre = cb_ref[...] + _conv_past(taps, cw_ref)
        sg = _sigmoid(pre)
        dpre = d_ref[...] * sg * (1.0 + pre * (1.0 - sg))
        dpre_ref[...] = dpre.astype(BF16)

        @pl.when(i == 0)
        def _():
            dcb_ref[...] = jnp.zeros_like(dcb_ref)
            dcw_ref[...] = jnp.zeros_like(dcw_ref)

        dcb_ref[...] += jnp.sum(dpre, axis=0, keepdims=True)
        for k in range(K):
            dcw_ref[k:k + 1, :] += jnp.sum(dpre * taps[k], axis=0, keepdims=True)

    blk = pl.BlockSpec((bt, bc), lambda j, i: (i, j))
    return pl.pallas_call(
        kern, name=name, grid=(C // bc, T // bt),
        in_specs=[blk, pl.BlockSpec((bt, bc), lambda j, i: (i, j + off)),
                  pl.BlockSpec((HALO, bc), lambda j, i: (jnp.maximum(i * (bt // HALO) - 1, 0), j + off)),
                  pl.BlockSpec((K, bc), lambda j, i: (0, j)), pl.BlockSpec((1, bc), lambda j, i: (0, j))],
        out_specs=[blk, pl.BlockSpec((1, bc), lambda j, i: (0, j)), pl.BlockSpec((K, bc), lambda j, i: (0, j))],
        out_shape=[jax.ShapeDtypeStruct((T, C), BF16), jax.ShapeDtypeStruct((1, C), F32),
                   jax.ShapeDtypeStruct((K, C), F32)],
        scratch_shapes=[pltpu.VMEM((bt + HALO, bc), F32)],
        compiler_params=_params(("parallel", "arbitrary")),
    )(dxbc, proj, proj, cw, cb)


_GP = SSM_D_INNER // SSM_GROUPS
_HPG = SSM_HEADS // SSM_GROUPS
_PH = SSM_D_INNER // SSM_HEADS


def _head_masks(rows):
    lane = lax.broadcasted_iota(jnp.int32, (rows, _GP), 1)
    return [jnp.logical_and(lane >= r * _PH, lane < (r + 1) * _PH) for r in range(_HPG)]


def _ssd_cols(g):
    x0 = g * _GP
    b0 = SSM_D_INNER + g * SSM_STATE
    c0 = SSM_D_INNER + (SSM_GROUPS + g) * SSM_STATE
    return slice(x0, x0 + _GP), slice(b0, b0 + SSM_STATE), slice(c0, c0 + SSM_STATE)


def _ssd_specs(idx):
    Q, N = SSM_CHUNK, SSM_STATE
    return dict(
        xbc=pl.BlockSpec((Q, SSM_XBC), lambda j: (idx(j), 0)),
        x=pl.BlockSpec((Q, SSM_D_INNER), lambda j: (idx(j), 0)),
        col=pl.BlockSpec((Q, LANE), lambda j: (idx(j), 0)),
        row=pl.BlockSpec((SSM_HEADS, Q), lambda j: (0, idx(j))),
        st=pl.BlockSpec((N, SSM_D_INNER), lambda j: (idx(j), 0)),
    )


def _ssd_scan_fwd(xbc, dtc, acc_, dtr, acr, *, name):
    T = xbc.shape[0]
    Q, N = SSM_CHUNK, SSM_STATE
    nc = T // Q
    sp = _ssd_specs(lambda j: j)

    def kern(xbc_ref, dtc_ref, ac_ref, dtr_ref, ar_ref, ys_ref, st_ref, state_ref):
        @pl.when(pl.program_id(0) == 0)
        def _():
            state_ref[...] = jnp.zeros_like(state_ref)

        r_i = lax.broadcasted_iota(jnp.int32, (Q, Q), 0)
        c_i = lax.broadcasted_iota(jnp.int32, (Q, Q), 1)
        tri = c_i <= r_i
        masks = _head_masks(Q)
        masks1 = _head_masks(1)
        for g in range(SSM_GROUPS):
            xs, bs, cs = _ssd_cols(g)
            S = state_ref[g]
            st_ref[:, xs] = S
            x = xbc_ref[:, xs]
            xb = x.astype(BF16)
            Bb = xbc_ref[:, bs].astype(BF16)
            Cb = xbc_ref[:, cs].astype(BF16)
            CB = _dot(Cb, Bb, _NT)
            Ws = []
            El = jnp.zeros((Q, _GP), F32)
            Wl = jnp.zeros((Q, _GP), F32)
            decl = jnp.zeros((1, _GP), F32)
            for r in range(_HPG):
                h = g * _HPG + r
                a_c = ac_ref[:, h:h + 1]
                a_r = ar_ref[h:h + 1, :]
                dt_c = dtc_ref[:, h:h + 1]
                dt_r = dtr_ref[h:h + 1, :]
                L = jnp.exp(jnp.where(tri, a_c - a_r, NEG))
                Ws.append((CB * L * dt_r).astype(BF16))
                a_q = a_c[Q - 1:Q, :]
                El = jnp.where(masks[r], jnp.exp(a_c), El)
                Wl = jnp.where(masks[r], jnp.exp(a_q - a_c) * dt_c, Wl)
                decl = jnp.where(masks1[r], jnp.exp(a_q), decl)
            xm = jnp.concatenate([jnp.where(m, x, 0.0).astype(BF16) for m in masks], axis=0)
            y = _dot(jnp.concatenate(Ws, axis=1), xm)
            ys_ref[:, xs] = y + _dot(Cb, S.astype(BF16)) * El
            state_ref[g] = S * decl + _dot(Bb, (x * Wl).astype(BF16), _TN)

    return pl.pallas_call(
        kern, name=name, grid=(nc,),
        in_specs=[sp["xbc"], sp["col"], sp["col"], sp["row"], sp["row"]],
        out_specs=[sp["x"], sp["st"]],
        out_shape=[jax.ShapeDtypeStruct((T, SSM_D_INNER), F32), jax.ShapeDtypeStruct((nc * N, SSM_D_INNER), F32)],
        scratch_shapes=[pltpu.VMEM((SSM_GROUPS, N, _GP), F32)],
        compiler_params=_params(("arbitrary",)),
    )(xbc, dtc, acc_, dtr, acr)


def _ssd_scan_bwd(xbc, dys, dskip, st, dtc, acc_, dtr, acr, *, name):
    T = xbc.shape[0]
    Q, N = SSM_CHUNK, SSM_STATE
    nc = T // Q
    sp = _ssd_specs(lambda j: nc - 1 - j)

    def kern(xbc_ref, dy_ref, dsk_ref, st_ref, dtc_ref, ac_ref, dtr_ref, ar_ref,
             dxbc_ref, dac_ref, dar_ref, ddc_ref, ddr_ref, dstate_ref):
        @pl.when(pl.program_id(0) == 0)
        def _():
            dstate_ref[...] = jnp.zeros_like(dstate_ref)

        r_i = lax.broadcasted_iota(jnp.int32, (Q, Q), 0)
        c_i = lax.broadcasted_iota(jnp.int32, (Q, Q), 1)
        tri = c_i <= r_i
        last_row = lax.broadcasted_iota(jnp.int32, (Q, 1), 0) == Q - 1
        lane128 = lax.broadcasted_iota(jnp.int32, (Q, LANE), 1)
        masks = _head_masks(Q)
        masksN = _head_masks(N)
        masks1 = _head_masks(1)
        zeros = jnp.zeros((Q, _GP), F32)
        dacol = jnp.zeros((Q, LANE), F32)
        ddcol = jnp.zeros((Q, LANE), F32)
        for g in range(SSM_GROUPS):
            xs, bs, cs = _ssd_cols(g)
            dS = dstate_ref[g]
            dSb = dS.astype(BF16)
            S = st_ref[:, xs]
            Sb = S.astype(BF16)
            x = xbc_ref[:, xs]
            xb = x.astype(BF16)
            Bb = xbc_ref[:, bs].astype(BF16)
            Cb = xbc_ref[:, cs].astype(BF16)
            dy = dy_ref[:, xs]
            CB = _dot(Cb, Bb, _NT)
            BdS = _dot(Bb, dSb)
            hx = BdS * x
            yd = _dot(Cb, Sb) * dy
            dSS = dS * S
            El, Wl = zeros, zeros
            decl = jnp.zeros((1, _GP), F32)
            dG = jnp.zeros((Q, Q), F32)
            dym = jnp.concatenate([jnp.where(m, dy, 0.0).astype(BF16) for m in masks], axis=0)
            dWs = _dot(dym, xb, _NT)
            Ws = []
            for r in range(_HPG):
                h = g * _HPG + r
                hm = masks[r]
                a_c = ac_ref[:, h:h + 1]
                a_r = ar_ref[h:h + 1, :]
                dt_c = dtc_ref[:, h:h + 1]
                dt_r = dtr_ref[h:h + 1, :]
                L = jnp.exp(jnp.where(tri, a_c - a_r, NEG))
                GL = CB * L
                Ws.append((GL * dt_r).astype(BF16))
                dW = dWs[r * Q:(r + 1) * Q]
                X = dW * GL
                da_c = jnp.sum(X * dt_r, axis=1, keepdims=True)
                ddt_r = jnp.sum(X, axis=0, keepdims=True)
                ddr_ref[h:h + 1, :] = ddt_r
                dar_ref[h:h + 1, :] = -ddt_r * dt_r
                dG = dG + dW * L * dt_r
                a_q = a_c[Q - 1:Q, :]
                e_c = jnp.exp(a_c)
                eq_c = jnp.exp(a_q - a_c)
                w_c = eq_c * dt_c
                ydr = jnp.sum(jnp.where(hm, yd, 0.0), axis=1, keepdims=True) * e_c
                h_c = jnp.sum(jnp.where(hm, hx, 0.0), axis=1, keepdims=True)
                hw = h_c * w_c
                dss = jnp.sum(jnp.sum(jnp.where(masksN[r], dSS, 0.0), axis=1, keepdims=True), axis=0, keepdims=True)
                s_q = jnp.sum(hw, axis=0, keepdims=True) + jnp.exp(a_q) * dss
                da_c = da_c + ydr - hw + jnp.where(last_row, s_q, 0.0)
                dacol = jnp.where(lane128 == h, da_c, dacol)
                ddcol = jnp.where(lane128 == h, h_c * eq_c, ddcol)
                El = jnp.where(hm, e_c, El)
                Wl = jnp.where(hm, w_c, Wl)
                decl = jnp.where(masks1[r], jnp.exp(a_q), decl)
            dxi = _dot(jnp.concatenate(Ws, axis=0), dym, _TN)
            dxbc_ref[:, xs] = dxi + BdS * Wl + dsk_ref[:, xs]
            dGb = dG.astype(BF16)
            dxbc_ref[:, bs] = _dot(dGb, Cb, _TN) + _dot((x * Wl).astype(BF16), dSb, _NT)
            dyE = (dy * El).astype(BF16)
            dxbc_ref[:, cs] = _dot(dGb, Bb) + _dot(dyE, Sb, _NT)
            dstate_ref[g] = dS * decl + _dot(Cb, dyE, _TN)
        dac_ref[...] = dacol
        ddc_ref[...] = ddcol

    return pl.pallas_call(
        kern, name=name, grid=(nc,),
        in_specs=[sp["xbc"], sp["x"], sp["x"], sp["st"], sp["col"], sp["col"], sp["row"], sp["row"]],
        out_specs=[sp["xbc"], sp["col"], sp["row"], sp["col"], sp["row"]],
        out_shape=[jax.ShapeDtypeStruct((T, SSM_XBC), F32),
                   jax.ShapeDtypeStruct((T, LANE), F32), jax.ShapeDtypeStruct((SSM_HEADS, T), F32),
                   jax.ShapeDtypeStruct((T, LANE), F32), jax.ShapeDtypeStruct((SSM_HEADS, T), F32)],
        scratch_shapes=[pltpu.VMEM((SSM_GROUPS, N, _GP), F32)],
        compiler_params=_params(("arbitrary",)),
    )(xbc, dys, dskip, st, dtc, acc_, dtr, acr)


def _gate_norm_fwd(ys, xbc, proj, d_exp, norm_w, *, name):
    T = ys.shape[0]
    bt = _pick(T, NARROW_ROW_BLOCK, SUBLANE)

    def kern(ys_ref, x_ref, z_ref, d_ref, w_ref, o_ref):
        z = z_ref[...]
        yz = (ys_ref[...] + d_ref[...] * x_ref[...]) * (z * _sigmoid(z))
        rstd = lax.rsqrt(jnp.mean(yz * yz, axis=-1, keepdims=True) + RMS_EPS)
        o_ref[...] = (yz * rstd * w_ref[...]).astype(BF16)

    blk = pl.BlockSpec((bt, _GP), lambda i, g: (i, g))
    vec = pl.BlockSpec((1, _GP), lambda i, g: (0, g))
    return pl.pallas_call(
        kern, name=name, grid=(T // bt, SSM_GROUPS), in_specs=[blk, blk, blk, vec, vec], out_specs=blk,
        out_shape=jax.ShapeDtypeStruct((T, SSM_D_INNER), BF16), compiler_params=_params(("parallel", "parallel")),
    )(ys, xbc, proj, d_exp, norm_w)


def _gate_norm_bwd(dyn, ys, xbc, proj, d_exp, norm_w, *, name):
    T = ys.shape[0]
    bt = _pick(T, NARROW_ROW_BLOCK, SUBLANE)

    def kern(dyn_ref, ys_ref, x_ref, z_ref, d_ref, w_ref, dz_ref, dys_ref, dsk_ref, dw_ref, dd_ref):
        i = pl.program_id(1)
        z = z_ref[...]
        x = x_ref[...]
        sg = _sigmoid(z)
        sz = z * sg
        y = ys_ref[...] + d_ref[...] * x
        yz = y * sz
        rstd = lax.rsqrt(jnp.mean(yz * yz, axis=-1, keepdims=True) + RMS_EPS)
        yhat = yz * rstd
        dynv = dyn_ref[...]
        gg = dynv * w_ref[...]
        dyz = rstd * (gg - yhat * jnp.mean(gg * yhat, axis=-1, keepdims=True))
        dy = dyz * sz
        dz_ref[...] = (dyz * y * sg * (1.0 + z * (1.0 - sg))).astype(BF16)
        dys_ref[...] = dy
        dsk_ref[...] = dy * d_ref[...]

        @pl.when(i == 0)
        def _():
            dw_ref[...] = jnp.zeros_like(dw_ref)
            dd_ref[...] = jnp.zeros_like(dd_ref)

        dw_ref[...] += jnp.sum(dynv * yhat, axis=0, keepdims=True)
        dd_ref[...] += jnp.sum(dy * x, axis=0, keepdims=True)

    blk = pl.BlockSpec((bt, _GP), lambda g, i: (i, g))
    vec = pl.BlockSpec((1, _GP), lambda g, i: (0, g))
    act = jax.ShapeDtypeStruct((T, SSM_D_INNER), F32)
    par = jax.ShapeDtypeStruct((1, SSM_D_INNER), F32)
    return pl.pallas_call(
        kern, name=name, grid=(SSM_GROUPS, T // bt), in_specs=[blk, blk, blk, blk, vec, vec],
        out_specs=[blk, blk, blk, vec, vec],
        out_shape=[jax.ShapeDtypeStruct((T, SSM_IN_PAD), BF16), act, act, par, par],
        compiler_params=_params(("parallel", "arbitrary")),
    )(dyn, ys, xbc, proj, d_exp, norm_w)


def _loss_head(y, target, *, name):
    T, D = y.shape
    bt = _pick(T, ROW_BLOCK, SUBLANE)

    def kern(y_ref, t_ref, l_ref, dy_ref):
        i = pl.program_id(0)
        err = y_ref[...] - t_ref[...]
        dy_ref[...] = err * (1.0 / D)

        @pl.when(i == 0)
        def _():
            l_ref[...] = jnp.zeros_like(l_ref)

        l_ref[...] += jnp.sum(err * err, axis=0, keepdims=True) * (0.5 / D)

    row = pl.BlockSpec((bt, D), lambda i: (i, 0))
    vec = pl.BlockSpec((1, D), lambda i: (0, 0))
    return pl.pallas_call(
        kern, name=name, grid=(T // bt,), in_specs=[row, row], out_specs=[vec, row],
        out_shape=[jax.ShapeDtypeStruct((1, D), F32), jax.ShapeDtypeStruct((T, D), F32)],
        compiler_params=_params(("arbitrary",)),
    )(y, target)


def _adamw(w, g, m, v, *, name):
    shape = w.shape
    w, g, m, v = (t.reshape(-1, shape[-1]) for t in (w, g, m, v))
    R, C = w.shape
    br = _pick(R, 256, SUBLANE)

    def kern(w_ref, g_ref, m_ref, v_ref, d_ref, nm_ref, nv_ref):
        gv = g_ref[...]
        nm = ADAM_B1 * m_ref[...] + (1.0 - ADAM_B1) * gv
        nv = ADAM_B2 * v_ref[...] + (1.0 - ADAM_B2) * (gv * gv)
        m_hat = nm / (1.0 - ADAM_B1 ** ADAM_STEP)
        v_hat = nv / (1.0 - ADAM_B2 ** ADAM_STEP)
        d_ref[...] = -ADAM_LR * (m_hat / (jnp.sqrt(v_hat) + ADAM_EPS) + ADAM_WD * w_ref[...])
        nm_ref[...] = nm
        nv_ref[...] = nv

    blk = pl.BlockSpec((br, C), lambda i: (i, 0))
    outs = pl.pallas_call(
        kern, name=name, grid=(R // br,), in_specs=[blk] * 4, out_specs=[blk] * 3,
        out_shape=[jax.ShapeDtypeStruct((R, C), F32)] * 3, compiler_params=_params(("parallel",)),
    )(w, g, m, v)
    return [o.reshape(shape) for o in outs]


def _add2(a, b, out_dtype, *, name):
    shape = a.shape
    a2, b2 = a.reshape(-1, shape[-1]), b.reshape(-1, shape[-1])
    R, C = a2.shape
    br = _pick(R, 512, SUBLANE)

    def kern(a_ref, b_ref, o_ref):
        o_ref[...] = (a_ref[...].astype(F32) + b_ref[...].astype(F32)).astype(out_dtype)

    blk = pl.BlockSpec((br, C), lambda i: (i, 0))
    return pl.pallas_call(
        kern, name=name, grid=(R // br,), in_specs=[blk, blk], out_specs=blk,
        out_shape=jax.ShapeDtypeStruct((R, C), out_dtype), compiler_params=_params(("parallel",)),
    )(a2, b2).reshape(shape)


def _sum4(buf, *, name):
    _, R, C = buf.shape
    br = _pick(R, 512, SUBLANE)

    def kern(b_ref, o_ref):
        b = [b_ref[k].astype(F32) for k in range(4)]
        o_ref[...] = ((b[0] + b[1]) + b[2]) + b[3]

    return pl.pallas_call(
        kern, name=name, grid=(R // br,), in_specs=[pl.BlockSpec((4, br, C), lambda i: (0, i, 0))],
        out_specs=pl.BlockSpec((br, C), lambda i: (i, 0)),
        out_shape=jax.ShapeDtypeStruct((R, C), F32), compiler_params=_params(("parallel",)),
    )(buf)


def _place():
    x, y, c = lax.axis_index("x"), lax.axis_index("y"), lax.axis_index("c")
    other_chips = [(1 - x, y), (x, 1 - y), (1 - x, 1 - y)]
    return x, y, c, other_chips


def _gather_chips(w, *, name):
    R, C = w.shape
    H = R // 2

    def body(w_ref, out_ref, send_sems, recv_sems):
        x, y, c, chips = _place()
        me_chip = 2 * x + y
        sib = (x, y, 1 - c)

        def rows(chip, hc):
            return out_ref.at[chip, pl.ds(hc * H, H), :]

        def copy(k, blk, to, src=None):
            return pltpu.make_async_remote_copy(
                src_ref=blk if src is None else src, dst_ref=blk, send_sem=send_sems.at[k], recv_sem=recv_sems.at[k],
                device_id=to, device_id_type=MESH)

        first = [copy(j, rows(me_chip, c), (cx, cy, c), src=w_ref.at[pl.ds(c * H, H), :])
                 for j, (cx, cy) in enumerate(chips)]
        for cp in first:
            cp.start()
        passed = []
        for j, (cx, cy) in enumerate(chips):
            blk = rows(2 * cx + cy, c)
            copy(j, blk, (cx, cy, c)).wait_recv()
            fw = copy(3 + j, blk, sib)
            fw.start()
            passed.append(fw)
        for j, (cx, cy) in enumerate(chips):
            copy(3 + j, rows(2 * cx + cy, 1 - c), sib).wait_recv()
        for cp in first + passed:
            cp.wait_send()

    return pl.pallas_call(
        body, name=name, in_specs=[_ANY], out_specs=_ANY,
        out_shape=jax.ShapeDtypeStruct((4, R, C), w.dtype),
        scratch_shapes=[pltpu.SemaphoreType.DMA((6,)), pltpu.SemaphoreType.DMA((6,))],
    )(w)


def _pair_swap(v, *, name, other_half=False):
    shape = (v.shape[0], v.shape[1] // 2, v.shape[2]) if other_half else v.shape

    def body(v_ref, out_ref, send_sem, recv_sem):
        x, y, c, _ = _place()
        src = v_ref.at[:, pl.ds((1 - c) * shape[1], shape[1]), :] if other_half else v_ref
        cp = pltpu.make_async_remote_copy(src_ref=src, dst_ref=out_ref, send_sem=send_sem, recv_sem=recv_sem,
                                          device_id=(x, y, 1 - c), device_id_type=MESH)
        cp.start()
        cp.wait()

    return pl.pallas_call(
        body, name=name, in_specs=[_ANY], out_specs=_ANY, out_shape=jax.ShapeDtypeStruct(shape, v.dtype),
        scratch_shapes=[pltpu.SemaphoreType.DMA, pltpu.SemaphoreType.DMA],
    )(v)


def _chip_exchange(pv, *, name):
    def body(p_ref, out_ref, send_sems, recv_sems):
        x, y, c, chips = _place()
        me_chip = 2 * x + y
        sends = []
        for j, (cx, cy) in enumerate(chips):
            cp = pltpu.make_async_remote_copy(
                src_ref=p_ref.at[2 * cx + cy], dst_ref=out_ref.at[me_chip], send_sem=send_sems.at[j],
                recv_sem=recv_sems.at[j], device_id=(cx, cy, c), device_id_type=MESH)
            cp.start()
            sends.append(cp)
        for j, (cx, cy) in enumerate(chips):
            blk = out_ref.at[2 * cx + cy]
            pltpu.make_async_remote_copy(src_ref=blk, dst_ref=blk, send_sem=send_sems.at[j], recv_sem=recv_sems.at[j],
                                         device_id=(cx, cy, c), device_id_type=MESH).wait_recv()
        for cp in sends:
            cp.wait_send()

    return pl.pallas_call(
        body, name=name, in_specs=[_ANY], out_specs=_ANY, out_shape=jax.ShapeDtypeStruct(pv.shape, pv.dtype),
        scratch_shapes=[pltpu.SemaphoreType.DMA((3,)), pltpu.SemaphoreType.DMA((3,))],
    )(pv)


WEIGHTS = [
    ("attn_w_in", 2), ("attn_b_f", None), ("attn_w_out", 1), ("ssm_w_in", 2), ("ssm_conv_w", 2), ("ssm_conv_b", 1),
    ("ssm_dt_bias", None), ("ssm_A_log", None), ("ssm_D", None), ("ssm_norm_w", 1), ("ssm_w_out", 1),
    ("ln_mix_g", None), ("ln_mix_b", None), ("ffn_w_up", 2), ("ffn_conv_w", 2), ("ffn_conv_b", None),
    ("ffn_w_down", 1), ("ln_ffn_g", None), ("ln_ffn_b", None), ("ple_w_proj", 2), ("ple_w_gate", 1),
    ("ple_b_gate", None),
]
N_CHIPS = 4
MATMUL_WEIGHTS = ("attn_w_in", "attn_w_out", "ssm_w_in", "ssm_w_out", "ffn_w_up", "ffn_w_down", "ple_w_proj",
                  "ple_w_gate")


def _pack(arrays):
    parts = []
    total = 0
    for a in arrays:
        n = a.size
        pad = (-n) % PACK_COLS
        flat = a.reshape(-1)
        parts.append(jnp.pad(flat, (0, pad)) if pad else flat)
        total += n + pad
    rows = total // PACK_COLS
    rpad = (-rows) % PACK_ROW_ALIGN
    if rpad:
        parts.append(jnp.zeros((rpad * PACK_COLS,), arrays[0].dtype))
    return jnp.concatenate(parts).reshape(rows + rpad, PACK_COLS)


def _unpack(buf, shapes):
    flat = buf.reshape(-1)
    out = []
    off = 0
    for s in shapes:
        n = math.prod(s)
        out.append(flat[off:off + n].reshape(s))
        off += n + ((-n) % PACK_COLS)
    return out


def _from_row_layout(a):
    return jnp.pad(a.T, ((0, 0), (0, LANE - SSM_HEADS)))


def _pad_lanes(v, n=LANE):
    return jnp.pad(v, (0, n - v.shape[0])).reshape(1, n)


def _local_step(x, p, target, W):
    T = x.shape[0]
    row = lambda v: v.reshape(1, -1)
    attn_in = jnp.pad(W["attn_w_in"][0], ((0, 0), (0, ATTN_IN_PAD - W["attn_w_in"].shape[2])))
    ssm_in = jnp.pad(W["ssm_w_in"][0], ((0, 0), (0, SSM_IN_PAD - W["ssm_w_in"].shape[2])))
    bf = _pad_lanes(W["attn_b_f"][0])
    dt_bias = _pad_lanes(W["ssm_dt_bias"][0])
    a_log = _pad_lanes(W["ssm_A_log"][0])
    d_exp = jnp.repeat(W["ssm_D"][0], _PH).reshape(1, SSM_D_INNER)
    norm_w = row(W["ssm_norm_w"][0])
    G = {}

    def ffn_ple_fwd(i, xin, mix, tag):
        s = {}
        s["z1"], s["h1"], s["h1b"] = _ln_fwd(xin, mix, row(W["ln_mix_g"][i]), row(W["ln_mix_b"][i]),
                                             name=f"ln_mix_fwd{tag}")
        s["up"] = _mm(s["h1b"], W["ffn_w_up"][i], name=f"ffn_up{tag}")
        s["a"] = _ffn_act_fwd(s["up"], W["ffn_conv_w"][i], row(W["ffn_conv_b"][i]), name=f"ffn_act_fwd{tag}")
        ffn = _mm(s["a"], W["ffn_w_down"][i], name=f"ffn_down{tag}")
        s["z2"], s["h2"], s["h2b"] = _ln_fwd(s["h1"], ffn, row(W["ln_ffn_g"][i]), row(W["ln_ffn_b"][i]),
                                             name=f"ln_ffn_fwd{tag}")
        s["G"] = _mm(s["h2b"], W["ple_w_gate"][i], name=f"ple_gate_mm{tag}")
        s["pp"] = _mm(pb[i], W["ple_w_proj"][i], name=f"ple_proj_mm{tag}")
        out, outb = _ple_fwd(s["h2"], s["G"], row(W["ple_b_gate"][i]), s["pp"], name=f"ple_fwd{tag}")
        return out, outb, s

    def ffn_ple_bwd(i, dx, s, tag):
        g = {}
        dG, dpp, g["ple_b_gate"] = _ple_bwd(dx, s["G"], row(W["ple_b_gate"][i]), s["pp"], name=f"ple_bwd{tag}")
        g["ple_w_gate"] = _mm(s["h2b"], dG, ta=True, name=f"ple_gate_dw{tag}")
        g["ple_w_proj"] = _mm(pb[i], dpp, ta=True, name=f"ple_proj_dw{tag}")
        dh2 = _mm(dG, W["ple_w_gate"][i], tb=True, add=dx, name=f"ple_gate_dx{tag}")
        dz2, dz2b, g["ln_ffn_g"], g["ln_ffn_b"] = _ln_bwd(dh2, s["z2"], row(W["ln_ffn_g"][i]), name=f"ln_ffn_bwd{tag}")
        da = _mm(dz2b, W["ffn_w_down"][i], tb=True, out_dtype=BF16, name=f"ffn_down_dx{tag}")
        g["ffn_w_down"] = _mm(s["a"], dz2b, ta=True, name=f"ffn_down_dw{tag}")
        dup, dgc, g["ffn_conv_b"], g["ffn_conv_w"] = _ffn_act_bwd(
            da, s["up"], W["ffn_conv_w"][i], row(W["ffn_conv_b"][i]), name=f"ffn_act_bwd{tag}")
        dup = _dwconv_bwd_data(dgc, W["ffn_conv_w"][i], FFN_CONV, dup, FFN_DIM, name=f"ffn_conv_bwd{tag}")
        g["ffn_w_up"] = _mm(s["h1b"], dup, ta=True, name=f"ffn_up_dw{tag}")
        dh1 = _mm(dup, W["ffn_w_up"][i], tb=True, add=dz2, add_scale=DEEPNORM_ALPHA, name=f"ffn_up_dx{tag}")
        dz1, dz1b, g["ln_mix_g"], g["ln_mix_b"] = _ln_bwd(dh1, s["z1"], row(W["ln_mix_g"][i]), name=f"ln_mix_bwd{tag}")
        return dz1, dz1b, g

    xb = x.astype(BF16)
    pb = p.astype(BF16)
    proj0 = _mm(xb, attn_in, name="attn_in")
    c_col = _fox_gate_fwd(proj0, bf, name="fox_gate_fwd")
    cT = (c_col[:, :ATTN_HEADS] * LOG2E).T.reshape(HEAD_PAIRS, 2, T)
    qa, qb, kk, ka, kb, vv, va, vb = _attn_prep(proj0, name="attn_prep")
    o, ob, lsea, lseb = _attn_fwd(qa, qb, kk, va, vb, cT, name="attn_fwd")
    mix0 = _mm(ob, W["attn_w_out"][0], name="attn_out")
    x1, x1b, s0 = ffn_ple_fwd(0, x, mix0, "0")

    proj1 = _mm(x1b, ssm_in, name="ssm_in")
    dt, acum = _ssd_dt_fwd(proj1, dt_bias, a_log, name="ssd_dt_fwd")
    xbc = _conv_silu_fwd(proj1, W["ssm_conv_w"][0], row(W["ssm_conv_b"][0]), name="ssd_conv_fwd")
    dtr, acr = dt[:, :SSM_HEADS].T, acum[:, :SSM_HEADS].T
    ys, states = _ssd_scan_fwd(xbc, dt, acum, dtr, acr, name="ssd_scan_fwd")
    yn = _gate_norm_fwd(ys, xbc, proj1, d_exp, norm_w, name="ssd_gate_norm_fwd")
    mix1 = _mm(yn, W["ssm_w_out"][0], name="ssm_out")
    x2, _, s1 = ffn_ple_fwd(1, x1, mix1, "1")

    lpart, dy = _loss_head(x2, target, name="loss_head")
    loss = jnp.sum(lpart)

    dz1, dz1b, g1 = ffn_ple_bwd(1, dy, s1, "1")
    G["ssm_w_out"] = _mm(yn, dz1b, ta=True, name="ssm_out_dw")[None]
    dyn = _mm(dz1b, W["ssm_w_out"][0], tb=True, name="ssm_out_dx")
    dproj1, dys, dskip, dnw, dde = _gate_norm_bwd(dyn, ys, xbc, proj1, d_exp, norm_w, name="ssd_gate_norm_bwd")
    G["ssm_norm_w"] = dnw
    G["ssm_D"] = dde.reshape(SSM_HEADS, _PH).sum(axis=1)[None]
    dxbc, dac, dar, ddc, ddr = _ssd_scan_bwd(xbc, dys, dskip, states, dt, acum, dtr, acr, name="ssd_scan_bwd")
    dproj1, dal, ddb = _ssd_dt_bwd(dac, _from_row_layout(dar), ddc, _from_row_layout(ddr), dt, proj1, dt_bias, a_log,
                                   dproj1, name="ssd_dt_bwd")
    G["ssm_A_log"] = dal[:, :SSM_HEADS]
    G["ssm_dt_bias"] = ddb[:, :SSM_HEADS]
    dpre, G["ssm_conv_b"], dcw = _conv_silu_bwd(dxbc, proj1, W["ssm_conv_w"][0], row(W["ssm_conv_b"][0]),
                                                name="ssd_conv_bwd")
    G["ssm_conv_w"] = dcw[None]
    dproj1 = _dwconv_bwd_data(dpre, W["ssm_conv_w"][0], SSM_CONV, dproj1, SSM_D_INNER, name="ssd_conv_bwd_data")
    G["ssm_w_in"] = _mm(x1b, dproj1, ta=True, name="ssm_in_dw")[None, :, :W["ssm_w_in"].shape[2]]
    dx1 = _mm(dproj1, ssm_in, tb=True, add=dz1, add_scale=DEEPNORM_ALPHA, name="ssm_in_dx")

    dz0, dz0b, g0 = ffn_ple_bwd(0, dx1, s0, "0")
    G["attn_w_out"] = _mm(ob, dz0b, ta=True, name="attn_out_dw")[None]
    do = _mm(dz0b, W["attn_w_out"][0], tb=True, name="attn_out_dx")
    doa, dob, dlta, dltb = _attn_bwd_prep(do, o, name="attn_bwd_prep")
    dq, dk, dv, dcT, dcq = _attn_bwd(qa, qb, kk, ka, kb, vv, doa, dob, lsea, lseb, dlta, dltb, cT, name="attn_bwd")
    dc_col = jnp.pad(dcT.reshape(ATTN_HEADS, T).T + dcq[:, :ATTN_HEADS], ((0, 0), (0, LANE - ATTN_HEADS)))
    dproj0, dbf = _fox_gate_bwd(dc_col, proj0, bf, dq, dk, dv, name="fox_gate_bwd")
    G["attn_b_f"] = dbf[:, :ATTN_HEADS]
    G["attn_w_in"] = _mm(xb, dproj0, ta=True, name="attn_in_dw")[None, :, :W["attn_w_in"].shape[2]]
    grad_x = _mm(dproj0, attn_in, tb=True, add=dz0, add_scale=DEEPNORM_ALPHA, name="attn_in_dx")

    for k in g0:
        G[k] = jnp.stack([g0[k].reshape(W[k].shape[1:]), g1[k].reshape(W[k].shape[1:])])
    return loss, grad_x, G


def kernel(x, p, attn_w_in, attn_b_f, attn_w_out, ssm_w_in, ssm_conv_w, ssm_conv_b, ssm_dt_bias, ssm_A_log, ssm_D, ssm_norm_w, ssm_w_out, ln_mix_g, ln_mix_b, ffn_w_up, ffn_conv_w, ffn_conv_b, ffn_w_down, ln_ffn_g, ln_ffn_b, ple_w_proj, ple_w_gate, ple_b_gate, loss_target, m_attn_w_in, m_attn_b_f, m_attn_w_out, m_ssm_w_in, m_ssm_conv_w, m_ssm_conv_b, m_ssm_dt_bias, m_ssm_A_log, m_ssm_D, m_ssm_norm_w, m_ssm_w_out, m_ln_mix_g, m_ln_mix_b, m_ffn_w_up, m_ffn_conv_w, m_ffn_conv_b, m_ffn_w_down, m_ln_ffn_g, m_ln_ffn_b, m_ple_w_proj, m_ple_w_gate, m_ple_b_gate, v_attn_w_in, v_attn_b_f, v_attn_w_out, v_ssm_w_in, v_ssm_conv_w, v_ssm_conv_b, v_ssm_dt_bias, v_ssm_A_log, v_ssm_D, v_ssm_norm_w, v_ssm_w_out, v_ln_mix_g, v_ln_mix_b, v_ffn_w_up, v_ffn_conv_w, v_ffn_conv_b, v_ffn_w_down, v_ln_ffn_g, v_ln_ffn_b, v_ple_w_proj, v_ple_w_gate, v_ple_b_gate):
    names = [n for n, _ in WEIGHTS]
    axes = dict(WEIGHTS)
    w_loc = dict(zip(names, [attn_w_in, attn_b_f, attn_w_out, ssm_w_in, ssm_conv_w, ssm_conv_b, ssm_dt_bias, ssm_A_log, ssm_D, ssm_norm_w, ssm_w_out, ln_mix_g, ln_mix_b, ffn_w_up, ffn_conv_w, ffn_conv_b, ffn_w_down, ln_ffn_g, ln_ffn_b, ple_w_proj, ple_w_gate, ple_b_gate]))
    m_loc = dict(zip(names, [m_attn_w_in, m_attn_b_f, m_attn_w_out, m_ssm_w_in, m_ssm_conv_w, m_ssm_conv_b, m_ssm_dt_bias, m_ssm_A_log, m_ssm_D, m_ssm_norm_w, m_ssm_w_out, m_ln_mix_g, m_ln_mix_b, m_ffn_w_up, m_ffn_conv_w, m_ffn_conv_b, m_ffn_w_down, m_ln_ffn_g, m_ln_ffn_b, m_ple_w_proj, m_ple_w_gate, m_ple_b_gate]))
    v_loc = dict(zip(names, [v_attn_w_in, v_attn_b_f, v_attn_w_out, v_ssm_w_in, v_ssm_conv_w, v_ssm_conv_b, v_ssm_dt_bias, v_ssm_A_log, v_ssm_D, v_ssm_norm_w, v_ssm_w_out, v_ln_mix_g, v_ln_mix_b, v_ffn_w_up, v_ffn_conv_w, v_ffn_conv_b, v_ffn_w_down, v_ln_ffn_g, v_ln_ffn_b, v_ple_w_proj, v_ple_w_gate, v_ple_b_gate]))
    sharded = [n for n in names if axes[n] is not None]
    matrices = [n for n in sharded if n in MATMUL_WEIGHTS]

    def wire(n):
        if n in matrices:
            return w_loc[n].astype(BF16)
        return lax.bitcast_convert_type(w_loc[n], BF16)

    wired = [wire(n) for n in sharded]
    me_chip = 2 * lax.axis_index("x") + lax.axis_index("y")
    packed = _pack(wired)
    gathered = lax.dynamic_update_index_in_dim(_gather_chips(packed, name="gather_weights"), packed, me_chip, 0)
    W = dict(w_loc)
    per_chip = [_unpack(gathered[k], [w.shape for w in wired]) for k in range(N_CHIPS)]
    for i, n in enumerate(sharded):
        pieces = [per_chip[k][i] for k in range(N_CHIPS)]
        if n not in matrices:
            pieces = [lax.bitcast_convert_type(q, F32) for q in pieces]
        W[n] = jnp.concatenate(pieces, axis=axes[n])

    loss, grad_x, G = _local_step(x[0], p[:, 0], loss_target[0], W)
    loss = lax.psum(loss, ("x", "y", "c"))

    def slot(k):
        parts = []
        for n in names:
            g = G[n].reshape(W[n].shape)
            if axes[n] is not None:
                size = w_loc[n].shape[axes[n]]
                g = lax.slice_in_dim(g, k * size, (k + 1) * size, axis=axes[n])
            parts.append(g.astype(BF16))
        return _pack(parts)

    contrib = jnp.stack([slot(k) for k in range(N_CHIPS)])
    R = contrib.shape[1]
    H = R // 2
    c = lax.axis_index("c")
    keep = lax.dynamic_slice_in_dim(contrib, c * H, H, axis=1)
    pair = _add2(keep, _pair_swap(contrib, other_half=True, name="grad_pair_swap"), BF16, name="grad_pair_sum")
    from_chips = lax.dynamic_update_index_in_dim(
        _chip_exchange(pair, name="grad_chip_exchange"), lax.dynamic_index_in_dim(pair, me_chip, 0, keepdims=False),
        me_chip, 0)
    half = _sum4(from_chips, name="grad_chip_sum")
    other = _pair_swap(half, name="grad_half_swap")
    gflat = jnp.concatenate([jnp.where(c == 0, half, other), jnp.where(c == 0, other, half)])

    grads = _unpack(gflat, [w_loc[n].shape for n in names])
    steps = [_adamw(w_loc[n], g, m_loc[n], v_loc[n], name=f"adamw_{n}") for n, g in zip(names, grads)]
    return (loss, grad_x[None], *grads, *[s[0] for s in steps], *[s[1] for s in steps], *[s[2] for s in steps])
```

```python
import math

import jax
import jax.numpy as jnp
from jax import lax
from jax.experimental import pallas as pl
from jax.experimental.pallas import tpu as pltpu

F32 = jnp.float32
BF16 = jnp.bfloat16
MESH = pl.DeviceIdType.MESH

D_MODEL = 1024
ATTN_HEADS = 16
HEAD_PAIRS = ATTN_HEADS // 2
SSM_D_INNER = 2048
SSM_HEADS = 32
SSM_GROUPS = 8
SSM_STATE = 128
SSM_CONV = 4
SSM_CHUNK = 128
SSM_XBC = SSM_D_INNER + 2 * SSM_GROUPS * SSM_STATE
FFN_DIM = 2816
FFN_CONV = 3
DEPTH = 2
LN_EPS = 1e-5
RMS_EPS = 1e-5
DEEPNORM_ALPHA = (2 * DEPTH) ** 0.25
ADAM_LR = 0.001
ADAM_B1 = 0.9
ADAM_B2 = 0.999
ADAM_EPS = 1e-08
ADAM_WD = 0.01
ADAM_STEP = 10

LANE = 128
SUBLANE = 8
HALO = SUBLANE
HALO_BF16 = 2 * SUBLANE
NEG = -1e30
ATTN_IN_PAD = 3 * D_MODEL + LANE
SSM_IN_PAD = 2 * SSM_D_INNER + 2 * SSM_GROUPS * SSM_STATE + LANE
PACK_COLS = 1024
PACK_ROW_ALIGN = 512

ATTN_BLOCK = 1024
ROW_BLOCK = 512
NARROW_ROW_BLOCK = 2048
CUM_BLOCK = 512


def _params(sem, vmem_mb=48):
    return pltpu.CompilerParams(dimension_semantics=sem, vmem_limit_bytes=vmem_mb * 2 ** 20)


def _pick(n, target, mult=LANE):
    best = None
    d = mult
    while d <= min(n, target):
        if n % d == 0:
            best = d
        d += mult
    return n if best is None else best


def _sigmoid(x):
    return 1.0 / (1.0 + jnp.exp(-x))


def _log1p(u):
    w = 1.0 + u
    return jnp.where(w == 1.0, u, jnp.log(w) * (u / (w - 1.0)))


def _softplus(x):
    return jnp.maximum(x, 0.0) + _log1p(jnp.exp(-jnp.abs(x)))


def _split3(x):
    hi = x.astype(BF16)
    r1 = x - hi.astype(F32)
    mid = r1.astype(BF16)
    lo = (r1 - mid.astype(F32)).astype(BF16)
    return hi, mid, lo


def _tri_matmul(tri, x):
    out = None
    for part in _split3(x):
        t = jnp.dot(tri, part, preferred_element_type=F32)
        out = t if out is None else out + t
    return out


def _tri(n, lower):
    r = lax.broadcasted_iota(jnp.int32, (n, n), 0)
    c = lax.broadcasted_iota(jnp.int32, (n, n), 1)
    return jnp.where((c <= r) if lower else (c >= r), 1.0, 0.0).astype(BF16)


_ANY = pl.BlockSpec(memory_space=pl.ANY)
MM_OUT_BLOCK_BYTES = 13 * 2 ** 20
MM_IN_BLOCK_BYTES = 6 * 2 ** 20
MM_VMEM_LIMIT_MB = 56
MM_VMEM_BUDGET = 44 * 2 ** 20
MM_WIDE_K = 3200


def _mm(a, b, *, name, ta=False, tb=False, add=None, add_scale=1.0, out_dtype=F32):
    if ta:
        K, M = a.shape
    else:
        M, K = a.shape
    if tb:
        N, Kb = b.shape
    else:
        Kb, N = b.shape
    assert K == Kb, (a.shape, b.shape, ta, tb)
    if ta:
        assert add is None and out_dtype == F32
        bm = _pick(M, 2816)
        bn = _pick(N, MM_OUT_BLOCK_BYTES // (4 * bm))
        bk = _pick(K, max(512, MM_IN_BLOCK_BYTES // (2 * max(bm, bn))))
    else:
        bn = _pick(N, 1536 if K <= MM_WIDE_K else 512)
        bk = K
        bm = _pick(M, 2048)
        if 4 * bm * K + 4 * K * bn + (8 if add is None else 16) * bm * bn > MM_VMEM_BUDGET:
            bm = _pick(M, 1024)
    nk = K // bk
    a_spec = pl.BlockSpec((bk, bm), lambda i, j, k: (k, i)) if ta else pl.BlockSpec((bm, bk), lambda i, j, k: (i, k))
    b_spec = pl.BlockSpec((bn, bk), lambda i, j, k: (j, k)) if tb else pl.BlockSpec((bk, bn), lambda i, j, k: (k, j))
    o_spec = pl.BlockSpec((bm, bn), lambda i, j, k: (i, j))
    dims = (((0 if ta else 1,), (1 if tb else 0,)), ((), ()))
    has_add = add is not None

    def kern(*refs):
        a_ref, b_ref = refs[0], refs[1]
        add_ref = refs[2] if has_add else None
        o_ref = refs[3] if has_add else refs[2]
        k = pl.program_id(2)
        part = lax.dot_general(a_ref[...].astype(BF16), b_ref[...].astype(BF16), dims, preferred_element_type=F32)
        if nk == 1:
            o_ref[...] = (part + add_scale * add_ref[...] if has_add else part).astype(out_dtype)
        else:
            @pl.when(k == 0)
            def _():
                o_ref[...] = part

            @pl.when(k > 0)
            def _():
                o_ref[...] += part

    ins = [a, b] + ([add] if has_add else [])
    in_specs = [a_spec, b_spec] + ([o_spec] if has_add else [])
    return pl.pallas_call(
        kern, name=name, grid=(M // bm, N // bn, nk),
        in_specs=in_specs, out_specs=o_spec,
        out_shape=jax.ShapeDtypeStruct((M, N), out_dtype),
        compiler_params=_params(("parallel", "parallel", "arbitrary"), vmem_mb=MM_VMEM_LIMIT_MB),
    )(*ins)


def _ln_stats(z):
    mu = jnp.mean(z, axis=-1, keepdims=True)
    zc = z - mu
    var = jnp.mean(zc * zc, axis=-1, keepdims=True)
    return zc, lax.rsqrt(var + LN_EPS)


def _ln_fwd(x, r, g, b, *, name):
    T, D = x.shape
    bt = _pick(T, ROW_BLOCK, SUBLANE)

    def kern(x_ref, r_ref, g_ref, b_ref, z_ref, h_ref, hb_ref):
        z = DEEPNORM_ALPHA * x_ref[...] + r_ref[...]
        zc, rstd = _ln_stats(z)
        h = zc * rstd * g_ref[...] + b_ref[...]
        z_ref[...] = z
        h_ref[...] = h
        hb_ref[...] = h.astype(BF16)

    row = pl.BlockSpec((bt, D), lambda i: (i, 0))
    vec = pl.BlockSpec((1, D), lambda i: (0, 0))
    return pl.pallas_call(
        kern, name=name, grid=(T // bt,), in_specs=[row, row, vec, vec], out_specs=[row, row, row],
        out_shape=[jax.ShapeDtypeStruct((T, D), F32)] * 2 + [jax.ShapeDtypeStruct((T, D), BF16)],
        compiler_params=_params(("parallel",)),
    )(x, r, g, b)


def _ln_bwd(dy, z, g, *, name):
    T, D = z.shape
    bt = _pick(T, ROW_BLOCK, SUBLANE)

    def kern(dy_ref, z_ref, g_ref, dz_ref, dzb_ref, dg_ref, db_ref):
        i = pl.program_id(0)
        zc, rstd = _ln_stats(z_ref[...])
        xhat = zc * rstd
        dyv = dy_ref[...]
        dxh = dyv * g_ref[...]
        m1 = jnp.mean(dxh, axis=-1, keepdims=True)
        m2 = jnp.mean(dxh * xhat, axis=-1, keepdims=True)
        dz = rstd * (dxh - m1 - xhat * m2)
        dz_ref[...] = dz
        dzb_ref[...] = dz.astype(BF16)

        @pl.when(i == 0)
        def _():
            dg_ref[...] = jnp.zeros_like(dg_ref)
            db_ref[...] = jnp.zeros_like(db_ref)

        dg_ref[...] += jnp.sum(dyv * xhat, axis=0, keepdims=True)
        db_ref[...] += jnp.sum(dyv, axis=0, keepdims=True)

    row = pl.BlockSpec((bt, D), lambda i: (i, 0))
    vec = pl.BlockSpec((1, D), lambda i: (0, 0))
    return pl.pallas_call(
        kern, name=name, grid=(T // bt,), in_specs=[row, row, vec], out_specs=[row, row, vec, vec],
        out_shape=[jax.ShapeDtypeStruct((T, D), F32), jax.ShapeDtypeStruct((T, D), BF16),
                   jax.ShapeDtypeStruct((1, D), F32), jax.ShapeDtypeStruct((1, D), F32)],
        compiler_params=_params(("arbitrary",)),
    )(dy, z, g)


def _past_taps(ext_ref, K, bt):
    ext = ext_ref[...]
    return [(ext if k == K - 1 else pltpu.roll(ext, K - 1 - k, 0))[HALO:HALO + bt] for k in range(K)]


def _conv_past(taps, cw_ref):
    out = None
    for k, tap in enumerate(taps):
        term = cw_ref[k:k + 1, :] * tap
        out = term if out is None else out + term
    return out


def _fill_ext_past(ext_ref, halo_ref, cur, i, bt):
    ext_ref[pl.ds(0, HALO), :] = jnp.where(i > 0, halo_ref[...], 0.0)
    ext_ref[pl.ds(HALO, bt), :] = cur


def _halo_prev(bt, bc, off):
    return pl.BlockSpec((HALO, bc), lambda i, j: (jnp.maximum(i * (bt // HALO) - 1, 0), j + off))


def _normal_cdf(x):
    return 0.5 * (1.0 + lax.erf(x * (1.0 / math.sqrt(2.0))))


def _gelu(x):
    return x * _normal_cdf(x)


def _gelu_and_grad(x):
    cdf = _normal_cdf(x)
    return x * cdf, cdf + x * jnp.exp(-0.5 * x * x) * (1.0 / math.sqrt(2.0 * math.pi))


def _ffn_act_fwd(up, cw, cb, *, name):
    T, F2 = up.shape
    F = F2 // 2
    bt = _pick(T, ROW_BLOCK, SUBLANE)
    bc = _pick(F, 1408)
    nb = F // bc

    def kern(u_ref, g_ref, halo_ref, cw_ref, cb_ref, a_ref, ext_ref):
        i = pl.program_id(0)
        _fill_ext_past(ext_ref, halo_ref, g_ref[...], i, bt)
        gc = cb_ref[...] + _conv_past(_past_taps(ext_ref, FFN_CONV, bt), cw_ref)
        a_ref[...] = (_gelu(gc) * u_ref[...]).astype(BF16)

    return pl.pallas_call(
        kern, name=name, grid=(T // bt, nb),
        in_specs=[pl.BlockSpec((bt, bc), lambda i, j: (i, j)),
                  pl.BlockSpec((bt, bc), lambda i, j: (i, j + nb)),
                  _halo_prev(bt, bc, nb),
                  pl.BlockSpec((FFN_CONV, bc), lambda i, j: (0, j)),
                  pl.BlockSpec((1, bc), lambda i, j: (0, j))],
        out_specs=pl.BlockSpec((bt, bc), lambda i, j: (i, j)),
        out_shape=jax.ShapeDtypeStruct((T, F), BF16),
        scratch_shapes=[pltpu.VMEM((bt + HALO, bc), F32)],
        compiler_params=_params(("parallel", "parallel")),
    )(up, up, up, cw, cb)


def _ffn_act_bwd(da, up, cw, cb, *, name):
    T, F2 = up.shape
    F = F2 // 2
    bt = _pick(T, ROW_BLOCK, SUBLANE)
    bc = _pick(F, 1408)
    nb = F // bc
    K = FFN_CONV

    def kern(da_ref, u_ref, g_ref, halo_ref, cw_ref, cb_ref, du_ref, dgc_ref, dcb_ref, dcw_ref, ext_ref):
        i = pl.program_id(1)
        _fill_ext_past(ext_ref, halo_ref, g_ref[...], i, bt)
        taps = _past_taps(ext_ref, K, bt)
        gc = cb_ref[...] + _conv_past(taps, cw_ref)
        dav = da_ref[...]
        act, act_grad = _gelu_and_grad(gc)
        du_ref[...] = (dav * act).astype(BF16)
        dgc = dav * u_ref[...] * act_grad
        dgc_ref[...] = dgc.astype(BF16)

        @pl.when(i == 0)
        def _():
            dcb_ref[...] = jnp.zeros_like(dcb_ref)
            dcw_ref[...] = jnp.zeros_like(dcw_ref)

        dcb_ref[...] += jnp.sum(dgc, axis=0, keepdims=True)
        for k in range(K):
            dcw_ref[k:k + 1, :] += jnp.sum(dgc * taps[k], axis=0, keepdims=True)

    blk = pl.BlockSpec((bt, bc), lambda j, i: (i, j))
    return pl.pallas_call(
        kern, name=name, grid=(nb, T // bt),
        in_specs=[blk, blk,
                  pl.BlockSpec((bt, bc), lambda j, i: (i, j + nb)),
                  pl.BlockSpec((HALO, bc), lambda j, i: (jnp.maximum(i * (bt // HALO) - 1, 0), j + nb)),
                  pl.BlockSpec((K, bc), lambda j, i: (0, j)),
                  pl.BlockSpec((1, bc), lambda j, i: (0, j))],
        out_specs=[blk, blk, pl.BlockSpec((1, bc), lambda j, i: (0, j)), pl.BlockSpec((K, bc), lambda j, i: (0, j))],
        out_shape=[jax.ShapeDtypeStruct((T, F2), BF16), jax.ShapeDtypeStruct((T, F), BF16),
                   jax.ShapeDtypeStruct((1, F), F32), jax.ShapeDtypeStruct((K, F), F32)],
        scratch_shapes=[pltpu.VMEM((bt + HALO, bc), F32)],
        compiler_params=_params(("parallel", "arbitrary")),
    )(da, up, up, up, cw, cb)


def _dwconv_bwd_data(dgc, cw, K, into, col, *, name):
    T, C = dgc.shape
    bt = _pick(T, ROW_BLOCK, SUBLANE)
    bc = _pick(C, 1408)
    nt = T // bt
    halo = HALO_BF16 if dgc.dtype == BF16 else HALO
    last_halo = T // halo - 1
    off = col // bc
    assert off * bc == col

    def kern(d_ref, halo_ref, cw_ref, into_ref, o_ref, ext_ref):
        i = pl.program_id(0)
        ext_ref[pl.ds(0, bt), :] = d_ref[...].astype(F32)
        ext_ref[pl.ds(bt, halo), :] = jnp.where(i < nt - 1, halo_ref[...].astype(F32), 0.0)
        ext = ext_ref[...]
        out = None
        for k in range(K):
            ahead = K - 1 - k
            tap = (ext if ahead == 0 else pltpu.roll(ext, bt + halo - ahead, 0))[0:bt]
            term = cw_ref[k:k + 1, :] * tap
            out = term if out is None else out + term
        o_ref[...] = out.astype(o_ref.dtype)

    return pl.pallas_call(
        kern, name=name, grid=(nt, C // bc),
        in_specs=[pl.BlockSpec((bt, bc), lambda i, j: (i, j)),
                  pl.BlockSpec((halo, bc), lambda i, j: (jnp.minimum((i + 1) * (bt // halo), last_halo), j)),
                  pl.BlockSpec((K, bc), lambda i, j: (0, j)), _ANY],
        out_specs=pl.BlockSpec((bt, bc), lambda i, j: (i, j + off)),
        out_shape=jax.ShapeDtypeStruct(into.shape, into.dtype), input_output_aliases={3: 0},
        scratch_shapes=[pltpu.VMEM((bt + halo, bc), F32)],
        compiler_params=_params(("parallel", "parallel")),
    )(dgc, dgc, cw, into)


def _ple_fwd(h, G, bg, pp, *, name):
    T, D = h.shape
    bt = _pick(T, ROW_BLOCK, SUBLANE)

    def kern(h_ref, G_ref, bg_ref, pp_ref, o_ref, ob_ref):
        out = h_ref[...] + _sigmoid(G_ref[...] + bg_ref[...]) * pp_ref[...]
        o_ref[...] = out
        ob_ref[...] = out.astype(BF16)

    row = pl.BlockSpec((bt, D), lambda i: (i, 0))
    vec = pl.BlockSpec((1, D), lambda i: (0, 0))
    return pl.pallas_call(
        kern, name=name, grid=(T // bt,), in_specs=[row, row, vec, row], out_specs=[row, row],
        out_shape=[jax.ShapeDtypeStruct((T, D), F32), jax.ShapeDtypeStruct((T, D), BF16)],
        compiler_params=_params(("parallel",)),
    )(h, G, bg, pp)


def _ple_bwd(dx, G, bg, pp, *, name):
    T, D = dx.shape
    bt = _pick(T, ROW_BLOCK, SUBLANE)

    def kern(dx_ref, G_ref, bg_ref, pp_ref, dG_ref, dpp_ref, dbg_ref):
        i = pl.program_id(0)
        gate = _sigmoid(G_ref[...] + bg_ref[...])
        dxv = dx_ref[...]
        dG = dxv * pp_ref[...] * gate * (1.0 - gate)
        dG_ref[...] = dG.astype(BF16)
        dpp_ref[...] = (dxv * gate).astype(BF16)

        @pl.when(i == 0)
        def _():
            dbg_ref[...] = jnp.zeros_like(dbg_ref)

        dbg_ref[...] += jnp.sum(dG, axis=0, keepdims=True)

    row = pl.BlockSpec((bt, D), lambda i: (i, 0))
    vec = pl.BlockSpec((1, D), lambda i: (0, 0))
    return pl.pallas_call(
        kern, name=name, grid=(T // bt,), in_specs=[row, row, vec, row], out_specs=[row, row, vec],
        out_shape=[jax.ShapeDtypeStruct((T, D), BF16), jax.ShapeDtypeStruct((T, D), BF16),
                   jax.ShapeDtypeStruct((1, D), F32)],
        compiler_params=_params(("arbitrary",)),
    )(dx, G, bg, pp)


def _fox_gate_fwd(proj, bf, *, name):
    T = proj.shape[0]
    bt = _pick(T, CUM_BLOCK, SUBLANE)
    fcol = 3 * D_MODEL // LANE

    def kern(f_ref, bf_ref, c_ref, carry_ref):
        i = pl.program_id(0)

        @pl.when(i == 0)
        def _():
            carry_ref[...] = jnp.zeros_like(carry_ref)

        x = f_ref[...] + bf_ref[...]
        lf = jnp.minimum(x, 0.0) - _log1p(jnp.exp(-jnp.abs(x)))
        cs = _tri_matmul(_tri(bt, True), lf) + carry_ref[...]
        c_ref[...] = cs
        carry_ref[...] = cs[bt - 1:bt, :]

    return pl.pallas_call(
        kern, name=name, grid=(T // bt,),
        in_specs=[pl.BlockSpec((bt, LANE), lambda i: (i, fcol)), pl.BlockSpec((1, LANE), lambda i: (0, 0))],
        out_specs=pl.BlockSpec((bt, LANE), lambda i: (i, 0)),
        out_shape=jax.ShapeDtypeStruct((T, LANE), F32),
        scratch_shapes=[pltpu.VMEM((1, LANE), F32)],
        compiler_params=_params(("arbitrary",)),
    )(proj, bf)


def _fox_gate_bwd(dc, proj, bf, dq, dk, dv, *, name):
    T = proj.shape[0]
    bt = _pick(T, CUM_BLOCK, SUBLANE)
    nb = T // bt
    fcol = 3 * D_MODEL // LANE

    def kern(dc_ref, f_ref, bf_ref, dq_ref, dk_ref, dv_ref, dproj_ref, dbf_ref, carry_ref):
        i = pl.program_id(0)

        @pl.when(i == 0)
        def _():
            carry_ref[...] = jnp.zeros_like(carry_ref)
            dbf_ref[...] = jnp.zeros_like(dbf_ref)

        dlf = _tri_matmul(_tri(bt, False), dc_ref[...]) + carry_ref[...]
        carry_ref[...] = dlf[0:1, :]
        x = f_ref[...] + bf_ref[...]
        lane = lax.broadcasted_iota(jnp.int32, (bt, LANE), 1)
        df = jnp.where(lane < ATTN_HEADS, dlf / (1.0 + jnp.exp(x)), 0.0)
        dbf_ref[...] += jnp.sum(df, axis=0, keepdims=True)
        for n, part_ref in enumerate((dq_ref, dk_ref, dv_ref)):
            dproj_ref[:, n * D_MODEL:(n + 1) * D_MODEL] = part_ref[...].astype(BF16)
        dproj_ref[:, 3 * D_MODEL:] = df.astype(BF16)

    rows = lambda i: (nb - 1 - i, 0)
    wide = pl.BlockSpec((bt, D_MODEL), rows)
    return pl.pallas_call(
        kern, name=name, grid=(nb,),
        in_specs=[pl.BlockSpec((bt, LANE), rows), pl.BlockSpec((bt, LANE), lambda i: (nb - 1 - i, fcol)),
                  pl.BlockSpec((1, LANE), lambda i: (0, 0)), wide, wide, wide],
        out_specs=[pl.BlockSpec((bt, ATTN_IN_PAD), rows), pl.BlockSpec((1, LANE), lambda i: (0, 0))],
        out_shape=[jax.ShapeDtypeStruct((T, ATTN_IN_PAD), BF16), jax.ShapeDtypeStruct((1, LANE), F32)],
        scratch_shapes=[pltpu.VMEM((1, LANE), F32)],
        compiler_params=_params(("arbitrary",)),
    )(dc, proj, bf, dq, dk, dv)


_NT = (((1,), (1,)), ((), ()))
_TN = (((0,), (0,)), ((), ()))


def _dot(a, b, dims=None):
    if dims is None:
        return jnp.dot(a, b, preferred_element_type=F32)
    return lax.dot_general(a, b, dims, preferred_element_type=F32)


LOG2E = 1.0 / math.log(2.0)
LN2 = math.log(2.0)
Q_SCALE = 0.125 * LOG2E
HALF = LANE // 2
L_LANE = (HALF, 0)
FWD_PAIRS = 4
BWD_PAIRS = 1
BWD_KEY_CHUNKS = 2
ATTN_BWD_VMEM_MB = 56


def _attn_prep(proj, *, name):
    T = proj.shape[0]
    bt = _pick(T, ATTN_BLOCK)

    def kern(q_ref, k_ref, v_ref, qa_ref, qb_ref, kk_ref, ka_ref, kb_ref, vv_ref, va_ref, vb_ref):
        lane = lax.broadcasted_iota(jnp.int32, (bt, LANE), 1)
        lo = lane < HALF
        q = q_ref[...] * Q_SCALE
        k = k_ref[...]
        v = v_ref[...]
        qa_ref[...] = jnp.where(lo, q, 0.0).astype(BF16)
        qb_ref[...] = jnp.where(lo, 0.0, q).astype(BF16)
        kk_ref[...] = k.astype(BF16)
        ka_ref[...] = jnp.where(lo, k, 0.0).astype(BF16)
        kb_ref[...] = jnp.where(lo, 0.0, k).astype(BF16)
        vv_ref[...] = v.astype(BF16)
        va_ref[...] = jnp.where(lo, v, jnp.where(lane == L_LANE[0], 1.0, 0.0)).astype(BF16)
        vb_ref[...] = jnp.where(lo, jnp.where(lane == L_LANE[1], 1.0, 0.0), v).astype(BF16)

    kcol, vcol = D_MODEL // LANE, 2 * D_MODEL // LANE
    out = pl.BlockSpec((bt, LANE), lambda i, hp: (i, hp))
    return pl.pallas_call(
        kern, name=name, grid=(T // bt, HEAD_PAIRS),
        in_specs=[out, pl.BlockSpec((bt, LANE), lambda i, hp: (i, kcol + hp)),
                  pl.BlockSpec((bt, LANE), lambda i, hp: (i, vcol + hp))],
        out_specs=[out] * 8, out_shape=[jax.ShapeDtypeStruct((T, D_MODEL), BF16)] * 8,
        compiler_params=_params(("parallel", "parallel")),
    )(proj, proj, proj)


def _attn_fwd(qa, qb, kk, va, vb, cT, *, name):
    T = qa.shape[0]
    tb = _pick(T, ATTN_BLOCK)
    nq = T // tb
    rep = tb // LANE
    width = FWD_PAIRS * LANE

    def kern(qa_ref, qb_ref, k_ref, va_ref, vb_ref, c_ref, o_ref, ob_ref, lsea_ref, lseb_ref, m_ref, acc_ref):
        qi = pl.program_id(1)
        ki = pl.program_id(2)

        @pl.when(ki == 0)
        def _():
            m_ref[...] = jnp.full_like(m_ref, NEG)
            acc_ref[...] = jnp.zeros_like(acc_ref)

        def step(diag):
            for pp in range(FWD_PAIRS):
                cols = slice(pp * LANE, (pp + 1) * LANE)
                k = k_ref[:, cols]
                for h, (q_ref, v_ref) in enumerate(((qa_ref, va_ref), (qb_ref, vb_ref))):
                    i = 2 * pp + h
                    s = _dot(q_ref[:, cols], k, _NT) - c_ref[pp, h:h + 1, :]
                    if diag:
                        r = lax.broadcasted_iota(jnp.int32, (tb, tb), 0)
                        c = lax.broadcasted_iota(jnp.int32, (tb, tb), 1)
                        s = jnp.where(c <= r, s, NEG)
                    m_prev = m_ref[i]
                    m_new = jnp.maximum(m_prev, jnp.max(s, axis=1, keepdims=True))
                    p = jnp.exp2(s - jnp.tile(m_new, (1, rep)))
                    acc_ref[i] = acc_ref[i] * jnp.exp2(m_prev - m_new) + _dot(p.astype(BF16), v_ref[:, cols])
                    m_ref[i] = m_new

        @pl.when(ki < qi)
        def _():
            step(False)

        @pl.when(ki == qi)
        def _():
            step(True)
            lo = lax.broadcasted_iota(jnp.int32, (tb, LANE), 1) < HALF
            for pp in range(FWD_PAIRS):
                cols = slice(pp * LANE, (pp + 1) * LANE)
                a0, a1 = acc_ref[2 * pp], acc_ref[2 * pp + 1]
                l0 = a0[:, L_LANE[0]:L_LANE[0] + 1]
                l1 = a1[:, L_LANE[1]:L_LANE[1] + 1]
                o = jnp.where(lo, a0 / l0, a1 / l1)
                o_ref[:, cols] = o
                ob_ref[:, cols] = o.astype(BF16)
                lsea_ref[:, cols] = m_ref[2 * pp] + jnp.log(l0) * LOG2E
                lseb_ref[:, cols] = m_ref[2 * pp + 1] + jnp.log(l1) * LOG2E

    qspec = pl.BlockSpec((tb, width), lambda g, qi, ki: (qi, g))
    kspec = pl.BlockSpec((tb, width), lambda g, qi, ki: (jnp.minimum(ki, qi), g))
    return pl.pallas_call(
        kern, name=name, grid=(HEAD_PAIRS // FWD_PAIRS, nq, nq),
        in_specs=[qspec, qspec, kspec, kspec, kspec,
                  pl.BlockSpec((FWD_PAIRS, 2, tb), lambda g, qi, ki: (g, 0, jnp.minimum(ki, qi)))],
        out_specs=[qspec, qspec, qspec, qspec],
        out_shape=[jax.ShapeDtypeStruct((T, D_MODEL), F32), jax.ShapeDtypeStruct((T, D_MODEL), BF16),
                   jax.ShapeDtypeStruct((T, D_MODEL), F32), jax.ShapeDtypeStruct((T, D_MODEL), F32)],
        scratch_shapes=[pltpu.VMEM((2 * FWD_PAIRS, tb, LANE), F32), pltpu.VMEM((2 * FWD_PAIRS, tb, LANE), F32)],
        compiler_params=_params(("parallel", "parallel", "arbitrary")),
    )(qa, qb, kk, va, vb, cT)


def _attn_bwd_prep(do, o, *, name):
    T, D = do.shape
    bt = _pick(T, ATTN_BLOCK)

    def kern(do_ref, o_ref, doa_ref, dob_ref, dlta_ref, dltb_ref):
        lo = lax.broadcasted_iota(jnp.int32, (bt, LANE), 1) < HALF
        dov = do_ref[...]
        prod = dov * o_ref[...]
        doa_ref[...] = jnp.where(lo, dov, 0.0).astype(BF16)
        dob_ref[...] = jnp.where(lo, 0.0, dov).astype(BF16)
        dlta_ref[...] = jnp.broadcast_to(jnp.sum(jnp.where(lo, prod, 0.0), axis=1, keepdims=True), (bt, LANE))
        dltb_ref[...] = jnp.broadcast_to(jnp.sum(jnp.where(lo, 0.0, prod), axis=1, keepdims=True), (bt, LANE))

    blk = pl.BlockSpec((bt, LANE), lambda i, hp: (i, hp))
    return pl.pallas_call(
        kern, name=name, grid=(T // bt, HEAD_PAIRS), in_specs=[blk, blk], out_specs=[blk] * 4,
        out_shape=[jax.ShapeDtypeStruct((T, D), BF16)] * 2 + [jax.ShapeDtypeStruct((T, D), F32)] * 2,
        compiler_params=_params(("parallel", "parallel")),
    )(do, o)


def _attn_bwd(qa, qb, kk, ka, kb, vv, doa, dob, lsea, lseb, dlta, dltb, cT, *, name):
    T = qa.shape[0]
    tb = _pick(T, ATTN_BLOCK)
    nq = T // tb
    tkc = tb // BWD_KEY_CHUNKS
    width = BWD_PAIRS * LANE

    def kern(qa_ref, qb_ref, k_ref, ka_ref, kb_ref, v_ref, doa_ref, dob_ref, lsea_ref, lseb_ref, dlta_ref, dltb_ref,
             c_ref, dq_ref, dk_ref, dv_ref, dc_ref, dcq_ref):
        g = pl.program_id(0)
        ki = pl.program_id(1)
        qi = pl.program_id(2)
        first = jnp.logical_and(ki == 0, qi == 0)

        @pl.when(first)
        def _():
            dq_ref[...] = jnp.zeros_like(dq_ref)

        @pl.when(jnp.logical_and(first, g == 0))
        def _():
            dcq_ref[...] = jnp.zeros_like(dcq_ref)

        @pl.when(qi == 0)
        def _():
            dk_ref[...] = jnp.zeros_like(dk_ref)
            dv_ref[...] = jnp.zeros_like(dv_ref)
            dc_ref[...] = jnp.zeros_like(dc_ref)

        def step(diag):
            rows = pl.ds(pl.multiple_of(qi * tb, tb), tb)
            lane = lax.broadcasted_iota(jnp.int32, (tb, LANE), 1)
            row_sums = jnp.zeros((tb, LANE), F32)
            for pp in range(BWD_PAIRS):
                cols = slice(pp * LANE, (pp + 1) * LANE)
                dq = None
                heads = ((qa_ref, ka_ref, doa_ref, lsea_ref, dlta_ref), (qb_ref, kb_ref, dob_ref, lseb_ref, dltb_ref))
                for h, (q_ref, km_ref, do_ref, lse_ref, dlt_ref) in enumerate(heads):
                    q = q_ref[:, cols]
                    dom = do_ref[:, cols]
                    head = 2 * (BWD_PAIRS * g + pp) + h
                    rs = None
                    for kc in range(BWD_KEY_CHUNKS):
                        keys = slice(kc * tkc, (kc + 1) * tkc)
                        s = _dot(q, k_ref[keys, cols], _NT) - c_ref[pp, h:h + 1, keys]
                        if diag:
                            r = lax.broadcasted_iota(jnp.int32, (tb, tkc), 0)
                            c = lax.broadcasted_iota(jnp.int32, (tb, tkc), 1) + kc * tkc
                            s = jnp.where(c <= r, s, NEG)
                        p = jnp.exp2(s - jnp.tile(lse_ref[:, cols], (1, tkc // LANE)))
                        ds = p * (_dot(dom, v_ref[keys, cols], _NT) - jnp.tile(dlt_ref[:, cols], (1, tkc // LANE)))
                        dc_ref[pp, h:h + 1, keys] -= jnp.sum(ds, axis=0, keepdims=True)
                        t = jnp.sum(ds, axis=1, keepdims=True)
                        rs = t if rs is None else rs + t
                        dsb = ds.astype(BF16)
                        dv_ref[keys, cols] += _dot(p.astype(BF16), dom, _TN)
                        dk_ref[keys, cols] += _dot(dsb, q, _TN) * LN2
                        tq = _dot(dsb, km_ref[keys, cols])
                        dq = tq if dq is None else dq + tq
                    row_sums = jnp.where(lane == head, rs, row_sums)
                dq_ref[rows, cols] += dq * 0.125
            dcq_ref[rows, :] += row_sums

        @pl.when(qi > ki)
        def _():
            step(False)

        @pl.when(qi == ki)
        def _():
            step(True)

    qspec = pl.BlockSpec((tb, width), lambda g, ki, qi: (jnp.maximum(qi, ki), g))
    kspec = pl.BlockSpec((tb, width), lambda g, ki, qi: (ki, g))
    cspec = pl.BlockSpec((BWD_PAIRS, 2, tb), lambda g, ki, qi: (g, 0, ki))
    qacc = pl.BlockSpec((T, width), lambda g, ki, qi: (0, g), pipeline_mode=pl.Buffered(1))
    cqacc = pl.BlockSpec((T, LANE), lambda g, ki, qi: (0, 0), pipeline_mode=pl.Buffered(1))
    return pl.pallas_call(
        kern, name=name, grid=(HEAD_PAIRS // BWD_PAIRS, nq, nq),
        in_specs=[qspec, qspec, kspec, kspec, kspec, kspec, qspec, qspec, qspec, qspec, qspec, qspec, cspec],
        out_specs=[qacc, kspec, kspec, cspec, cqacc],
        out_shape=[jax.ShapeDtypeStruct((T, D_MODEL), F32)] * 3 + [jax.ShapeDtypeStruct((HEAD_PAIRS, 2, T), F32),
                                                                   jax.ShapeDtypeStruct((T, LANE), F32)],
        compiler_params=_params(("arbitrary", "arbitrary", "arbitrary"), vmem_mb=ATTN_BWD_VMEM_MB),
    )(qa, qb, kk, ka, kb, vv, doa, dob, lsea, lseb, dlta, dltb, cT)


def _ssd_dt_fwd(proj, dt_bias, a_log, *, name):
    T = proj.shape[0]
    Q = SSM_CHUNK
    col = (2 * SSM_D_INNER + 2 * SSM_GROUPS * SSM_STATE) // LANE

    def kern(raw_ref, b_ref, al_ref, dt_ref, ac_ref):
        dt = _softplus(raw_ref[...] + b_ref[...])
        dt_ref[...] = dt
        ac_ref[...] = _tri_matmul(_tri(Q, True), dt * (-jnp.exp(al_ref[...])))

    vec = pl.BlockSpec((1, LANE), lambda i: (0, 0))
    blk = pl.BlockSpec((Q, LANE), lambda i: (i, 0))
    return pl.pallas_call(
        kern, name=name, grid=(T // Q,),
        in_specs=[pl.BlockSpec((Q, LANE), lambda i: (i, col)), vec, vec], out_specs=[blk, blk],
        out_shape=[jax.ShapeDtypeStruct((T, LANE), F32)] * 2,
        compiler_params=_params(("parallel",)),
    )(proj, dt_bias, a_log)


def _ssd_dt_bwd(da_a, da_b, ddt_a, ddt_b, dt, proj, dt_bias, a_log, into, *, name):
    T = proj.shape[0]
    Q = SSM_CHUNK
    col = (2 * SSM_D_INNER + 2 * SSM_GROUPS * SSM_STATE) // LANE

    def kern(daa_ref, dab_ref, dda_ref, ddb_ref, dt_ref, raw_ref, b_ref, al_ref, into_ref, draw_ref, dal_ref, db_ref,
             acc_ref):
        i = pl.program_id(0)

        @pl.when(i == 0)
        def _():
            acc_ref[...] = jnp.zeros_like(acc_ref)
            db_ref[...] = jnp.zeros_like(db_ref)

        A = -jnp.exp(al_ref[...])
        ddA = _tri_matmul(_tri(Q, False), daa_ref[...] + dab_ref[...])
        ddt = dda_ref[...] + ddb_ref[...] + ddA * A
        acc_ref[...] += jnp.sum(ddA * dt_ref[...], axis=0, keepdims=True)
        lane = lax.broadcasted_iota(jnp.int32, (Q, LANE), 1)
        draw = jnp.where(lane < SSM_HEADS, ddt * _sigmoid(raw_ref[...] + b_ref[...]), 0.0)
        draw_ref[...] = draw.astype(BF16)
        db_ref[...] += jnp.sum(draw, axis=0, keepdims=True)
        dal_ref[...] = acc_ref[...] * A

    vec = pl.BlockSpec((1, LANE), lambda i: (0, 0))
    blk = pl.BlockSpec((Q, LANE), lambda i: (i, 0))
    return pl.pallas_call(
        kern, name=name, grid=(T // Q,),
        in_specs=[blk, blk, blk, blk, blk, pl.BlockSpec((Q, LANE), lambda i: (i, col)), vec, vec, _ANY],
        out_specs=[pl.BlockSpec((Q, LANE), lambda i: (i, col)), vec, vec],
        out_shape=[jax.ShapeDtypeStruct(into.shape, into.dtype), jax.ShapeDtypeStruct((1, LANE), F32),
                   jax.ShapeDtypeStruct((1, LANE), F32)],
        input_output_aliases={8: 0},
        scratch_shapes=[pltpu.VMEM((1, LANE), F32)],
        compiler_params=_params(("arbitrary",)),
    )(da_a, da_b, ddt_a, ddt_b, dt, proj, dt_bias, a_log, into)


def _conv_silu_fwd(proj, cw, cb, *, name):
    T = proj.shape[0]
    C = SSM_XBC
    bt = _pick(T, ROW_BLOCK, SUBLANE)
    bc = 1024
    off = SSM_D_INNER // bc

    def kern(x_ref, halo_ref, cw_ref, cb_ref, o_ref, ext_ref):
        i = pl.program_id(0)
        _fill_ext_past(ext_ref, halo_ref, x_ref[...], i, bt)
        pre = cb_ref[...] + _conv_past(_past_taps(ext_ref, SSM_CONV, bt), cw_ref)
        o_ref[...] = pre * _sigmoid(pre)

    return pl.pallas_call(
        kern, name=name, grid=(T // bt, C // bc),
        in_specs=[pl.BlockSpec((bt, bc), lambda i, j: (i, j + off)), _halo_prev(bt, bc, off),
                  pl.BlockSpec((SSM_CONV, bc), lambda i, j: (0, j)), pl.BlockSpec((1, bc), lambda i, j: (0, j))],
        out_specs=pl.BlockSpec((bt, bc), lambda i, j: (i, j)),
        out_shape=jax.ShapeDtypeStruct((T, C), F32),
        scratch_shapes=[pltpu.VMEM((bt + HALO, bc), F32)],
        compiler_params=_params(("parallel", "parallel")),
    )(proj, proj, cw, cb)


def _conv_silu_bwd(dxbc, proj, cw, cb, *, name):
    T = proj.shape[0]
    C = SSM_XBC
    K = SSM_CONV
    bt = _pick(T, ROW_BLOCK, SUBLANE)
    bc = 1024
    off = SSM_D_INNER // bc

    def kern(d_ref, x_ref, halo_ref, cw_ref, cb_ref, dpre_ref, dcb_ref, dcw_ref, ext_ref):
        i = pl.program_id(1)
        _fill_ext_past(ext_ref, halo_ref, x_ref[...], i, bt)
        taps = _past_taps(ext_ref, K, bt)
        pre = cb_ref[...] + _conv_past(taps, cw_ref)
        sg = _sigmoid(pre)
        dpre = d_ref[...] * sg * (1.0 + pre * (1.0 - sg))
        dpre_ref[...] = dpre.astype(BF16)

        @pl.when(i == 0)
        def _():
            dcb_ref[...] = jnp.zeros_like(dcb_ref)
            dcw_ref[...] = jnp.zeros_like(dcw_ref)

        dcb_ref[...] += jnp.sum(dpre, axis=0, keepdims=True)
        for k in range(K):
            dcw_ref[k:k + 1, :] += jnp.sum(dpre * taps[k], axis=0, keepdims=True)

    blk = pl.BlockSpec((bt, bc), lambda j, i: (i, j))
    return pl.pallas_call(
        kern, name=name, grid=(C // bc, T // bt),
        in_specs=[blk, pl.BlockSpec((bt, bc), lambda j, i: (i, j + off)),
                  pl.BlockSpec((HALO, bc), lambda j, i: (jnp.maximum(i * (bt // HALO) - 1, 0), j + off)),
                  pl.BlockSpec((K, bc), lambda j, i: (0, j)), pl.BlockSpec((1, bc), lambda j, i: (0, j))],
        out_specs=[blk, pl.BlockSpec((1, bc), lambda j, i: (0, j)), pl.BlockSpec((K, bc), lambda j, i: (0, j))],
        out_shape=[jax.ShapeDtypeStruct((T, C), BF16), jax.ShapeDtypeStruct((1, C), F32),
                   jax.ShapeDtypeStruct((K, C), F32)],
        scratch_shapes=[pltpu.VMEM((bt + HALO, bc), F32)],
        compiler_params=_params(("parallel", "arbitrary")),
    )(dxbc, proj, proj, cw, cb)


_GP = SSM_D_INNER // SSM_GROUPS
_HPG = SSM_HEADS // SSM_GROUPS
_PH = SSM_D_INNER // SSM_HEADS


def _head_masks(rows):
    lane = lax.broadcasted_iota(jnp.int32, (rows, _GP), 1)
    return [jnp.logical_and(lane >= r * _PH, lane < (r + 1) * _PH) for r in range(_HPG)]


def _ssd_cols(g):
    x0 = g * _GP
    b0 = SSM_D_INNER + g * SSM_STATE
    c0 = SSM_D_INNER + (SSM_GROUPS + g) * SSM_STATE
    return slice(x0, x0 + _GP), slice(b0, b0 + SSM_STATE), slice(c0, c0 + SSM_STATE)


def _ssd_specs(idx):
    Q, N = SSM_CHUNK, SSM_STATE
    return dict(
        xbc=pl.BlockSpec((Q, SSM_XBC), lambda j: (idx(j), 0)),
        x=pl.BlockSpec((Q, SSM_D_INNER), lambda j: (idx(j), 0)),
        col=pl.BlockSpec((Q, LANE), lambda j: (idx(j), 0)),
        row=pl.BlockSpec((SSM_HEADS, Q), lambda j: (0, idx(j))),
        st=pl.BlockSpec((N, SSM_D_INNER), lambda j: (idx(j), 0)),
    )


def _ssd_scan_fwd(xbc, dtc, acc_, dtr, acr, *, name):
    T = xbc.shape[0]
    Q, N = SSM_CHUNK, SSM_STATE
    nc = T // Q
    sp = _ssd_specs(lambda j: j)

    def kern(xbc_ref, dtc_ref, ac_ref, dtr_ref, ar_ref, ys_ref, st_ref, state_ref):
        @pl.when(pl.program_id(0) == 0)
        def _():
            state_ref[...] = jnp.zeros_like(state_ref)

        r_i = lax.broadcasted_iota(jnp.int32, (Q, Q), 0)
        c_i = lax.broadcasted_iota(jnp.int32, (Q, Q), 1)
        tri = c_i <= r_i
        masks = _head_masks(Q)
        masks1 = _head_masks(1)
        for g in range(SSM_GROUPS):
            xs, bs, cs = _ssd_cols(g)
            S = state_ref[g]
            st_ref[:, xs] = S
            x = xbc_ref[:, xs]
            xb = x.astype(BF16)
            Bb = xbc_ref[:, bs].astype(BF16)
            Cb = xbc_ref[:, cs].astype(BF16)
            CB = _dot(Cb, Bb, _NT)
            y = jnp.zeros((Q, _GP), F32)
            El = jnp.zeros((Q, _GP), F32)
            Wl = jnp.zeros((Q, _GP), F32)
            decl = jnp.zeros((1, _GP), F32)
            for r in range(_HPG):
                h = g * _HPG + r
                a_c = ac_ref[:, h:h + 1]
                a_r = ar_ref[h:h + 1, :]
                dt_c = dtc_ref[:, h:h + 1]
                dt_r = dtr_ref[h:h + 1, :]
                L = jnp.exp(jnp.where(tri, a_c - a_r, NEG))
                W = CB * L * dt_r
                y = jnp.where(masks[r], _dot(W.astype(BF16), xb), y)
                a_q = a_c[Q - 1:Q, :]
                El = jnp.where(masks[r], jnp.exp(a_c), El)
                Wl = jnp.where(masks[r], jnp.exp(a_q - a_c) * dt_c, Wl)
                decl = jnp.where(masks1[r], jnp.exp(a_q), decl)
            ys_ref[:, xs] = y + _dot(Cb, S.astype(BF16)) * El
            state_ref[g] = S * decl + _dot(Bb, (x * Wl).astype(BF16), _TN)

    return pl.pallas_call(
        kern, name=name, grid=(nc,),
        in_specs=[sp["xbc"], sp["col"], sp["col"], sp["row"], sp["row"]],
        out_specs=[sp["x"], sp["st"]],
        out_shape=[jax.ShapeDtypeStruct((T, SSM_D_INNER), F32), jax.ShapeDtypeStruct((nc * N, SSM_D_INNER), F32)],
        scratch_shapes=[pltpu.VMEM((SSM_GROUPS, N, _GP), F32)],
        compiler_params=_params(("arbitrary",)),
    )(xbc, dtc, acc_, dtr, acr)


def _ssd_scan_bwd(xbc, dys, dskip, st, dtc, acc_, dtr, acr, *, name):
    T = xbc.shape[0]
    Q, N = SSM_CHUNK, SSM_STATE
    nc = T // Q
    sp = _ssd_specs(lambda j: nc - 1 - j)

    def kern(xbc_ref, dy_ref, dsk_ref, st_ref, dtc_ref, ac_ref, dtr_ref, ar_ref,
             dxbc_ref, dac_ref, dar_ref, ddc_ref, ddr_ref, dstate_ref):
        @pl.when(pl.program_id(0) == 0)
        def _():
            dstate_ref[...] = jnp.zeros_like(dstate_ref)

        r_i = lax.broadcasted_iota(jnp.int32, (Q, Q), 0)
        c_i = lax.broadcasted_iota(jnp.int32, (Q, Q), 1)
        tri = c_i <= r_i
        last_row = lax.broadcasted_iota(jnp.int32, (Q, 1), 0) == Q - 1
        lane128 = lax.broadcasted_iota(jnp.int32, (Q, LANE), 1)
        masks = _head_masks(Q)
        masksN = _head_masks(N)
        masks1 = _head_masks(1)
        zeros = jnp.zeros((Q, _GP), F32)
        dacol = jnp.zeros((Q, LANE), F32)
        ddcol = jnp.zeros((Q, LANE), F32)
        for g in range(SSM_GROUPS):
            xs, bs, cs = _ssd_cols(g)
            dS = dstate_ref[g]
            dSb = dS.astype(BF16)
            S = st_ref[:, xs]
            Sb = S.astype(BF16)
            x = xbc_ref[:, xs]
            xb = x.astype(BF16)
            Bb = xbc_ref[:, bs].astype(BF16)
            Cb = xbc_ref[:, cs].astype(BF16)
            dy = dy_ref[:, xs]
            CB = _dot(Cb, Bb, _NT)
            BdS = _dot(Bb, dSb)
            hx = BdS * x
            yd = _dot(Cb, Sb) * dy
            dSS = dS * S
            dxi, El, Wl = zeros, zeros, zeros
            decl = jnp.zeros((1, _GP), F32)
            dG = jnp.zeros((Q, Q), F32)
            for r in range(_HPG):
                h = g * _HPG + r
                hm = masks[r]
                a_c = ac_ref[:, h:h + 1]
                a_r = ar_ref[h:h + 1, :]
                dt_c = dtc_ref[:, h:h + 1]
                dt_r = dtr_ref[h:h + 1, :]
                L = jnp.exp(jnp.where(tri, a_c - a_r, NEG))
                GL = CB * L
                W = GL * dt_r
                dym = jnp.where(hm, dy, 0.0).astype(BF16)
                dW = _dot(dym, xb, _NT)
                X = dW * GL
                da_c = jnp.sum(X * dt_r, axis=1, keepdims=True)
                ddt_r = jnp.sum(X, axis=0, keepdims=True)
                ddr_ref[h:h + 1, :] = ddt_r
                dar_ref[h:h + 1, :] = -ddt_r * dt_r
                dG = dG + dW * L * dt_r
                dxi = dxi + _dot(W.astype(BF16), dym, _TN)
                a_q = a_c[Q - 1:Q, :]
                e_c = jnp.exp(a_c)
                eq_c = jnp.exp(a_q - a_c)
                w_c = eq_c * dt_c
                ydr = jnp.sum(jnp.where(hm, yd, 0.0), axis=1, keepdims=True) * e_c
                h_c = jnp.sum(jnp.where(hm, hx, 0.0), axis=1, keepdims=True)
                hw = h_c * w_c
                dss = jnp.sum(jnp.sum(jnp.where(masksN[r], dSS, 0.0), axis=1, keepdims=True), axis=0, keepdims=True)
                s_q = jnp.sum(hw, axis=0, keepdims=True) + jnp.exp(a_q) * dss
                da_c = da_c + ydr - hw + jnp.where(last_row, s_q, 0.0)
                dacol = jnp.where(lane128 == h, da_c, dacol)
                ddcol = jnp.where(lane128 == h, h_c * eq_c, ddcol)
                El = jnp.where(hm, e_c, El)
                Wl = jnp.where(hm, w_c, Wl)
                decl = jnp.where(masks1[r], jnp.exp(a_q), decl)
            dxbc_ref[:, xs] = dxi + BdS * Wl + dsk_ref[:, xs]
            dGb = dG.astype(BF16)
            dxbc_ref[:, bs] = _dot(dGb, Cb, _TN) + _dot((x * Wl).astype(BF16), dSb, _NT)
            dyE = (dy * El).astype(BF16)
            dxbc_ref[:, cs] = _dot(dGb, Bb) + _dot(dyE, Sb, _NT)
            dstate_ref[g] = dS * decl + _dot(Cb, dyE, _TN)
        dac_ref[...] = dacol
        ddc_ref[...] = ddcol

    return pl.pallas_call(
        kern, name=name, grid=(nc,),
        in_specs=[sp["xbc"], sp["x"], sp["x"], sp["st"], sp["col"], sp["col"], sp["row"], sp["row"]],
        out_specs=[sp["xbc"], sp["col"], sp["row"], sp["col"], sp["row"]],
        out_shape=[jax.ShapeDtypeStruct((T, SSM_XBC), F32),
                   jax.ShapeDtypeStruct((T, LANE), F32), jax.ShapeDtypeStruct((SSM_HEADS, T), F32),
                   jax.ShapeDtypeStruct((T, LANE), F32), jax.ShapeDtypeStruct((SSM_HEADS, T), F32)],
        scratch_shapes=[pltpu.VMEM((SSM_GROUPS, N, _GP), F32)],
        compiler_params=_params(("arbitrary",)),
    )(xbc, dys, dskip, st, dtc, acc_, dtr, acr)


def _gate_norm_fwd(ys, xbc, proj, d_exp, norm_w, *, name):
    T = ys.shape[0]
    bt = _pick(T, NARROW_ROW_BLOCK, SUBLANE)

    def kern(ys_ref, x_ref, z_ref, d_ref, w_ref, o_ref):
        z = z_ref[...]
        yz = (ys_ref[...] + d_ref[...] * x_ref[...]) * (z * _sigmoid(z))
        rstd = lax.rsqrt(jnp.mean(yz * yz, axis=-1, keepdims=True) + RMS_EPS)
        o_ref[...] = (yz * rstd * w_ref[...]).astype(BF16)

    blk = pl.BlockSpec((bt, _GP), lambda i, g: (i, g))
    vec = pl.BlockSpec((1, _GP), lambda i, g: (0, g))
    return pl.pallas_call(
        kern, name=name, grid=(T // bt, SSM_GROUPS), in_specs=[blk, blk, blk, vec, vec], out_specs=blk,
        out_shape=jax.ShapeDtypeStruct((T, SSM_D_INNER), BF16), compiler_params=_params(("parallel", "parallel")),
    )(ys, xbc, proj, d_exp, norm_w)


def _gate_norm_bwd(dyn, ys, xbc, proj, d_exp, norm_w, *, name):
    T = ys.shape[0]
    bt = _pick(T, NARROW_ROW_BLOCK, SUBLANE)

    def kern(dyn_ref, ys_ref, x_ref, z_ref, d_ref, w_ref, dz_ref, dys_ref, dsk_ref, dw_ref, dd_ref):
        i = pl.program_id(1)
        z = z_ref[...]
        x = x_ref[...]
        sg = _sigmoid(z)
        sz = z * sg
        y = ys_ref[...] + d_ref[...] * x
        yz = y * sz
        rstd = lax.rsqrt(jnp.mean(yz * yz, axis=-1, keepdims=True) + RMS_EPS)
        yhat = yz * rstd
        dynv = dyn_ref[...]
        gg = dynv * w_ref[...]
        dyz = rstd * (gg - yhat * jnp.mean(gg * yhat, axis=-1, keepdims=True))
        dy = dyz * sz
        dz_ref[...] = (dyz * y * sg * (1.0 + z * (1.0 - sg))).astype(BF16)
        dys_ref[...] = dy
        dsk_ref[...] = dy * d_ref[...]

        @pl.when(i == 0)
        def _():
            dw_ref[...] = jnp.zeros_like(dw_ref)
            dd_ref[...] = jnp.zeros_like(dd_ref)

        dw_ref[...] += jnp.sum(dynv * yhat, axis=0, keepdims=True)
        dd_ref[...] += jnp.sum(dy * x, axis=0, keepdims=True)

    blk = pl.BlockSpec((bt, _GP), lambda g, i: (i, g))
    vec = pl.BlockSpec((1, _GP), lambda g, i: (0, g))
    act = jax.ShapeDtypeStruct((T, SSM_D_INNER), F32)
    par = jax.ShapeDtypeStruct((1, SSM_D_INNER), F32)
    return pl.pallas_call(
        kern, name=name, grid=(SSM_GROUPS, T // bt), in_specs=[blk, blk, blk, blk, vec, vec],
        out_specs=[blk, blk, blk, vec, vec],
        out_shape=[jax.ShapeDtypeStruct((T, SSM_IN_PAD), BF16), act, act, par, par],
        compiler_params=_params(("parallel", "arbitrary")),
    )(dyn, ys, xbc, proj, d_exp, norm_w)


def _loss_head(y, target, *, name):
    T, D = y.shape
    bt = _pick(T, ROW_BLOCK, SUBLANE)

    def kern(y_ref, t_ref, l_ref, dy_ref):
        i = pl.program_id(0)
        err = y_ref[...] - t_ref[...]
        dy_ref[...] = err * (1.0 / D)

        @pl.when(i == 0)
        def _():
            l_ref[...] = jnp.zeros_like(l_ref)

        l_ref[...] += jnp.sum(err * err, axis=0, keepdims=True) * (0.5 / D)

    row = pl.BlockSpec((bt, D), lambda i: (i, 0))
    vec = pl.BlockSpec((1, D), lambda i: (0, 0))
    return pl.pallas_call(
        kern, name=name, grid=(T // bt,), in_specs=[row, row], out_specs=[vec, row],
        out_shape=[jax.ShapeDtypeStruct((1, D), F32), jax.ShapeDtypeStruct((T, D), F32)],
        compiler_params=_params(("arbitrary",)),
    )(y, target)


def _adamw(w, g, m, v, *, name):
    shape = w.shape
    w, g, m, v = (t.reshape(-1, shape[-1]) for t in (w, g, m, v))
    R, C = w.shape
    br = _pick(R, 256, SUBLANE)

    def kern(w_ref, g_ref, m_ref, v_ref, d_ref, nm_ref, nv_ref):
        gv = g_ref[...]
        nm = ADAM_B1 * m_ref[...] + (1.0 - ADAM_B1) * gv
        nv = ADAM_B2 * v_ref[...] + (1.0 - ADAM_B2) * (gv * gv)
        m_hat = nm / (1.0 - ADAM_B1 ** ADAM_STEP)
        v_hat = nv / (1.0 - ADAM_B2 ** ADAM_STEP)
        d_ref[...] = -ADAM_LR * (m_hat / (jnp.sqrt(v_hat) + ADAM_EPS) + ADAM_WD * w_ref[...])
        nm_ref[...] = nm
        nv_ref[...] = nv

    blk = pl.BlockSpec((br, C), lambda i: (i, 0))
    outs = pl.pallas_call(
        kern, name=name, grid=(R // br,), in_specs=[blk] * 4, out_specs=[blk] * 3,
        out_shape=[jax.ShapeDtypeStruct((R, C), F32)] * 3, compiler_params=_params(("parallel",)),
    )(w, g, m, v)
    return [o.reshape(shape) for o in outs]


def _add2(a, b, out_dtype, *, name):
    shape = a.shape
    a2, b2 = a.reshape(-1, shape[-1]), b.reshape(-1, shape[-1])
    R, C = a2.shape
    br = _pick(R, 512, SUBLANE)

    def kern(a_ref, b_ref, o_ref):
        o_ref[...] = (a_ref[...].astype(F32) + b_ref[...].astype(F32)).astype(out_dtype)

    blk = pl.BlockSpec((br, C), lambda i: (i, 0))
    return pl.pallas_call(
        kern, name=name, grid=(R // br,), in_specs=[blk, blk], out_specs=blk,
        out_shape=jax.ShapeDtypeStruct((R, C), out_dtype), compiler_params=_params(("parallel",)),
    )(a2, b2).reshape(shape)


def _sum4(buf, *, name):
    _, R, C = buf.shape
    br = _pick(R, 512, SUBLANE)

    def kern(b_ref, o_ref):
        b = [b_ref[k].astype(F32) for k in range(4)]
        o_ref[...] = ((b[0] + b[1]) + b[2]) + b[3]

    return pl.pallas_call(
        kern, name=name, grid=(R // br,), in_specs=[pl.BlockSpec((4, br, C), lambda i: (0, i, 0))],
        out_specs=pl.BlockSpec((br, C), lambda i: (i, 0)),
        out_shape=jax.ShapeDtypeStruct((R, C), F32), compiler_params=_params(("parallel",)),
    )(buf)


def _place():
    x, y, c = lax.axis_index("x"), lax.axis_index("y"), lax.axis_index("c")
    other_chips = [(1 - x, y), (x, 1 - y), (1 - x, 1 - y)]
    return x, y, c, other_chips


def _gather_chips(w, *, name):
    R, C = w.shape
    H = R // 2

    def body(w_ref, out_ref, send_sems, recv_sems):
        x, y, c, chips = _place()
        me_chip = 2 * x + y
        sib = (x, y, 1 - c)

        def rows(chip, hc):
            return out_ref.at[chip, pl.ds(hc * H, H), :]

        def copy(k, blk, to, src=None):
            return pltpu.make_async_remote_copy(
                src_ref=blk if src is None else src, dst_ref=blk, send_sem=send_sems.at[k], recv_sem=recv_sems.at[k],
                device_id=to, device_id_type=MESH)

        first = [copy(j, rows(me_chip, c), (cx, cy, c), src=w_ref.at[pl.ds(c * H, H), :])
                 for j, (cx, cy) in enumerate(chips)]
        for cp in first:
            cp.start()
        passed = []
        for j, (cx, cy) in enumerate(chips):
            blk = rows(2 * cx + cy, c)
            copy(j, blk, (cx, cy, c)).wait_recv()
            fw = copy(3 + j, blk, sib)
            fw.start()
            passed.append(fw)
        for j, (cx, cy) in enumerate(chips):
            copy(3 + j, rows(2 * cx + cy, 1 - c), sib).wait_recv()
        for cp in first + passed:
            cp.wait_send()

    return pl.pallas_call(
        body, name=name, in_specs=[_ANY], out_specs=_ANY,
        out_shape=jax.ShapeDtypeStruct((4, R, C), w.dtype),
        scratch_shapes=[pltpu.SemaphoreType.DMA((6,)), pltpu.SemaphoreType.DMA((6,))],
    )(w)


def _pair_swap(v, *, name, other_half=False):
    shape = (v.shape[0], v.shape[1] // 2, v.shape[2]) if other_half else v.shape

    def body(v_ref, out_ref, send_sem, recv_sem):
        x, y, c, _ = _place()
        src = v_ref.at[:, pl.ds((1 - c) * shape[1], shape[1]), :] if other_half else v_ref
        cp = pltpu.make_async_remote_copy(src_ref=src, dst_ref=out_ref, send_sem=send_sem, recv_sem=recv_sem,
                                          device_id=(x, y, 1 - c), device_id_type=MESH)
        cp.start()
        cp.wait()

    return pl.pallas_call(
        body, name=name, in_specs=[_ANY], out_specs=_ANY, out_shape=jax.ShapeDtypeStruct(shape, v.dtype),
        scratch_shapes=[pltpu.SemaphoreType.DMA, pltpu.SemaphoreType.DMA],
    )(v)


def _chip_exchange(pv, *, name):
    def body(p_ref, out_ref, send_sems, recv_sems):
        x, y, c, chips = _place()
        me_chip = 2 * x + y
        sends = []
        for j, (cx, cy) in enumerate(chips):
            cp = pltpu.make_async_remote_copy(
                src_ref=p_ref.at[2 * cx + cy], dst_ref=out_ref.at[me_chip], send_sem=send_sems.at[j],
                recv_sem=recv_sems.at[j], device_id=(cx, cy, c), device_id_type=MESH)
            cp.start()
            sends.append(cp)
        for j, (cx, cy) in enumerate(chips):
            blk = out_ref.at[2 * cx + cy]
            pltpu.make_async_remote_copy(src_ref=blk, dst_ref=blk, send_sem=send_sems.at[j], recv_sem=recv_sems.at[j],
                                         device_id=(cx, cy, c), device_id_type=MESH).wait_recv()
        for cp in sends:
            cp.wait_send()

    return pl.pallas_call(
        body, name=name, in_specs=[_ANY], out_specs=_ANY, out_shape=jax.ShapeDtypeStruct(pv.shape, pv.dtype),
        scratch_shapes=[pltpu.SemaphoreType.DMA((3,)), pltpu.SemaphoreType.DMA((3,))],
    )(pv)


WEIGHTS = [
    ("attn_w_in", 2), ("attn_b_f", None), ("attn_w_out", 1), ("ssm_w_in", 2), ("ssm_conv_w", 2), ("ssm_conv_b", 1),
    ("ssm_dt_bias", None), ("ssm_A_log", None), ("ssm_D", None), ("ssm_norm_w", 1), ("ssm_w_out", 1),
    ("ln_mix_g", None), ("ln_mix_b", None), ("ffn_w_up", 2), ("ffn_conv_w", 2), ("ffn_conv_b", None),
    ("ffn_w_down", 1), ("ln_ffn_g", None), ("ln_ffn_b", None), ("ple_w_proj", 2), ("ple_w_gate", 1),
    ("ple_b_gate", None),
]
N_CHIPS = 4
MATMUL_WEIGHTS = ("attn_w_in", "attn_w_out", "ssm_w_in", "ssm_w_out", "ffn_w_up", "ffn_w_down", "ple_w_proj",
                  "ple_w_gate")


def _pack(arrays):
    parts = []
    total = 0
    for a in arrays:
        n = a.size
        pad = (-n) % PACK_COLS
        flat = a.reshape(-1)
        parts.append(jnp.pad(flat, (0, pad)) if pad else flat)
        total += n + pad
    rows = total // PACK_COLS
    rpad = (-rows) % PACK_ROW_ALIGN
    if rpad:
        parts.append(jnp.zeros((rpad * PACK_COLS,), arrays[0].dtype))
    return jnp.concatenate(parts).reshape(rows + rpad, PACK_COLS)


def _unpack(buf, shapes):
    flat = buf.reshape(-1)
    out = []
    off = 0
    for s in shapes:
        n = math.prod(s)
        out.append(flat[off:off + n].reshape(s))
        off += n + ((-n) % PACK_COLS)
    return out


def _from_row_layout(a):
    return jnp.pad(a.T, ((0, 0), (0, LANE - SSM_HEADS)))


def _pad_lanes(v, n=LANE):
    return jnp.pad(v, (0, n - v.shape[0])).reshape(1, n)


def _local_step(x, p, target, W):
    T = x.shape[0]
    row = lambda v: v.reshape(1, -1)
    attn_in = jnp.pad(W["attn_w_in"][0], ((0, 0), (0, ATTN_IN_PAD - W["attn_w_in"].shape[2])))
    ssm_in = jnp.pad(W["ssm_w_in"][0], ((0, 0), (0, SSM_IN_PAD - W["ssm_w_in"].shape[2])))
    bf = _pad_lanes(W["attn_b_f"][0])
    dt_bias = _pad_lanes(W["ssm_dt_bias"][0])
    a_log = _pad_lanes(W["ssm_A_log"][0])
    d_exp = jnp.repeat(W["ssm_D"][0], _PH).reshape(1, SSM_D_INNER)
    norm_w = row(W["ssm_norm_w"][0])
    G = {}

    def ffn_ple_fwd(i, xin, mix, tag):
        s = {}
        s["z1"], s["h1"], s["h1b"] = _ln_fwd(xin, mix, row(W["ln_mix_g"][i]), row(W["ln_mix_b"][i]),
                                             name=f"ln_mix_fwd{tag}")
        s["up"] = _mm(s["h1b"], W["ffn_w_up"][i], name=f"ffn_up{tag}")
        s["a"] = _ffn_act_fwd(s["up"], W["ffn_conv_w"][i], row(W["ffn_conv_b"][i]), name=f"ffn_act_fwd{tag}")
        ffn = _mm(s["a"], W["ffn_w_down"][i], name=f"ffn_down{tag}")
        s["z2"], s["h2"], s["h2b"] = _ln_fwd(s["h1"], ffn, row(W["ln_ffn_g"][i]), row(W["ln_ffn_b"][i]),
                                             name=f"ln_ffn_fwd{tag}")
        s["G"] = _mm(s["h2b"], W["ple_w_gate"][i], name=f"ple_gate_mm{tag}")
        s["pp"] = _mm(pb[i], W["ple_w_proj"][i], name=f"ple_proj_mm{tag}")
        out, outb = _ple_fwd(s["h2"], s["G"], row(W["ple_b_gate"][i]), s["pp"], name=f"ple_fwd{tag}")
        return out, outb, s

    def ffn_ple_bwd(i, dx, s, tag):
        g = {}
        dG, dpp, g["ple_b_gate"] = _ple_bwd(dx, s["G"], row(W["ple_b_gate"][i]), s["pp"], name=f"ple_bwd{tag}")
        g["ple_w_gate"] = _mm(s["h2b"], dG, ta=True, name=f"ple_gate_dw{tag}")
        g["ple_w_proj"] = _mm(pb[i], dpp, ta=True, name=f"ple_proj_dw{tag}")
        dh2 = _mm(dG, W["ple_w_gate"][i], tb=True, add=dx, name=f"ple_gate_dx{tag}")
        dz2, dz2b, g["ln_ffn_g"], g["ln_ffn_b"] = _ln_bwd(dh2, s["z2"], row(W["ln_ffn_g"][i]), name=f"ln_ffn_bwd{tag}")
        da = _mm(dz2b, W["ffn_w_down"][i], tb=True, out_dtype=BF16, name=f"ffn_down_dx{tag}")
        g["ffn_w_down"] = _mm(s["a"], dz2b, ta=True, name=f"ffn_down_dw{tag}")
        dup, dgc, g["ffn_conv_b"], g["ffn_conv_w"] = _ffn_act_bwd(
            da, s["up"], W["ffn_conv_w"][i], row(W["ffn_conv_b"][i]), name=f"ffn_act_bwd{tag}")
        dup = _dwconv_bwd_data(dgc, W["ffn_conv_w"][i], FFN_CONV, dup, FFN_DIM, name=f"ffn_conv_bwd{tag}")
        g["ffn_w_up"] = _mm(s["h1b"], dup, ta=True, name=f"ffn_up_dw{tag}")
        dh1 = _mm(dup, W["ffn_w_up"][i], tb=True, add=dz2, add_scale=DEEPNORM_ALPHA, name=f"ffn_up_dx{tag}")
        dz1, dz1b, g["ln_mix_g"], g["ln_mix_b"] = _ln_bwd(dh1, s["z1"], row(W["ln_mix_g"][i]), name=f"ln_mix_bwd{tag}")
        return dz1, dz1b, g

    xb = x.astype(BF16)
    pb = p.astype(BF16)
    proj0 = _mm(xb, attn_in, name="attn_in")
    c_col = _fox_gate_fwd(proj0, bf, name="fox_gate_fwd")
    cT = (c_col[:, :ATTN_HEADS] * LOG2E).T.reshape(HEAD_PAIRS, 2, T)
    qa, qb, kk, ka, kb, vv, va, vb = _attn_prep(proj0, name="attn_prep")
    o, ob, lsea, lseb = _attn_fwd(qa, qb, kk, va, vb, cT, name="attn_fwd")
    mix0 = _mm(ob, W["attn_w_out"][0], name="attn_out")
    x1, x1b, s0 = ffn_ple_fwd(0, x, mix0, "0")

    proj1 = _mm(x1b, ssm_in, name="ssm_in")
    dt, acum = _ssd_dt_fwd(proj1, dt_bias, a_log, name="ssd_dt_fwd")
    xbc = _conv_silu_fwd(proj1, W["ssm_conv_w"][0], row(W["ssm_conv_b"][0]), name="ssd_conv_fwd")
    dtr, acr = dt[:, :SSM_HEADS].T, acum[:, :SSM_HEADS].T
    ys, states = _ssd_scan_fwd(xbc, dt, acum, dtr, acr, name="ssd_scan_fwd")
    yn = _gate_norm_fwd(ys, xbc, proj1, d_exp, norm_w, name="ssd_gate_norm_fwd")
    mix1 = _mm(yn, W["ssm_w_out"][0], name="ssm_out")
    x2, _, s1 = ffn_ple_fwd(1, x1, mix1, "1")

    lpart, dy = _loss_head(x2, target, name="loss_head")
    loss = jnp.sum(lpart)

    dz1, dz1b, g1 = ffn_ple_bwd(1, dy, s1, "1")
    G["ssm_w_out"] = _mm(yn, dz1b, ta=True, name="ssm_out_dw")[None]
    dyn = _mm(dz1b, W["ssm_w_out"][0], tb=True, name="ssm_out_dx")
    dproj1, dys, dskip, dnw, dde = _gate_norm_bwd(dyn, ys, xbc, proj1, d_exp, norm_w, name="ssd_gate_norm_bwd")
    G["ssm_norm_w"] = dnw
    G["ssm_D"] = dde.reshape(SSM_HEADS, _PH).sum(axis=1)[None]
    dxbc, dac, dar, ddc, ddr = _ssd_scan_bwd(xbc, dys, dskip, states, dt, acum, dtr, acr, name="ssd_scan_bwd")
    dproj1, dal, ddb = _ssd_dt_bwd(dac, _from_row_layout(dar), ddc, _from_row_layout(ddr), dt, proj1, dt_bias, a_log,
                                   dproj1, name="ssd_dt_bwd")
    G["ssm_A_log"] = dal[:, :SSM_HEADS]
    G["ssm_dt_bias"] = ddb[:, :SSM_HEADS]
    dpre, G["ssm_conv_b"], dcw = _conv_silu_bwd(dxbc, proj1, W["ssm_conv_w"][0], row(W["ssm_conv_b"][0]),
                                                name="ssd_conv_bwd")
    G["ssm_conv_w"] = dcw[None]
    dproj1 = _dwconv_bwd_data(dpre, W["ssm_conv_w"][0], SSM_CONV, dproj1, SSM_D_INNER, name="ssd_conv_bwd_data")
    G["ssm_w_in"] = _mm(x1b, dproj1, ta=True, name="ssm_in_dw")[None, :, :W["ssm_w_in"].shape[2]]
    dx1 = _mm(dproj1, ssm_in, tb=True, add=dz1, add_scale=DEEPNORM_ALPHA, name="ssm_in_dx")

    dz0, dz0b, g0 = ffn_ple_bwd(0, dx1, s0, "0")
    G["attn_w_out"] = _mm(ob, dz0b, ta=True, name="attn_out_dw")[None]
    do = _mm(dz0b, W["attn_w_out"][0], tb=True, name="attn_out_dx")
    doa, dob, dlta, dltb = _attn_bwd_prep(do, o, name="attn_bwd_prep")
    dq, dk, dv, dcT, dcq = _attn_bwd(qa, qb, kk, ka, kb, vv, doa, dob, lsea, lseb, dlta, dltb, cT, name="attn_bwd")
    dc_col = jnp.pad(dcT.reshape(ATTN_HEADS, T).T + dcq[:, :ATTN_HEADS], ((0, 0), (0, LANE - ATTN_HEADS)))
    dproj0, dbf = _fox_gate_bwd(dc_col, proj0, bf, dq, dk, dv, name="fox_gate_bwd")
    G["attn_b_f"] = dbf[:, :ATTN_HEADS]
    G["attn_w_in"] = _mm(xb, dproj0, ta=True, name="attn_in_dw")[None, :, :W["attn_w_in"].shape[2]]
    grad_x = _mm(dproj0, attn_in, tb=True, add=dz0, add_scale=DEEPNORM_ALPHA, name="attn_in_dx")

    for k in g0:
        G[k] = jnp.stack([g0[k].reshape(W[k].shape[1:]), g1[k].reshape(W[k].shape[1:])])
    return loss, grad_x, G


def kernel(x, p, attn_w_in, attn_b_f, attn_w_out, ssm_w_in, ssm_conv_w, ssm_conv_b, ssm_dt_bias, ssm_A_log, ssm_D, ssm_norm_w, ssm_w_out, ln_mix_g, ln_mix_b, ffn_w_up, ffn_conv_w, ffn_conv_b, ffn_w_down, ln_ffn_g, ln_ffn_b, ple_w_proj, ple_w_gate, ple_b_gate, loss_target, m_attn_w_in, m_attn_b_f, m_attn_w_out, m_ssm_w_in, m_ssm_conv_w, m_ssm_conv_b, m_ssm_dt_bias, m_ssm_A_log, m_ssm_D, m_ssm_norm_w, m_ssm_w_out, m_ln_mix_g, m_ln_mix_b, m_ffn_w_up, m_ffn_conv_w, m_ffn_conv_b, m_ffn_w_down, m_ln_ffn_g, m_ln_ffn_b, m_ple_w_proj, m_ple_w_gate, m_ple_b_gate, v_attn_w_in, v_attn_b_f, v_attn_w_out, v_ssm_w_in, v_ssm_conv_w, v_ssm_conv_b, v_ssm_dt_bias, v_ssm_A_log, v_ssm_D, v_ssm_norm_w, v_ssm_w_out, v_ln_mix_g, v_ln_mix_b, v_ffn_w_up, v_ffn_conv_w, v_ffn_conv_b, v_ffn_w_down, v_ln_ffn_g, v_ln_ffn_b, v_ple_w_proj, v_ple_w_gate, v_ple_b_gate):
    names = [n for n, _ in WEIGHTS]
    axes = dict(WEIGHTS)
    w_loc = dict(zip(names, [attn_w_in, attn_b_f, attn_w_out, ssm_w_in, ssm_conv_w, ssm_conv_b, ssm_dt_bias, ssm_A_log, ssm_D, ssm_norm_w, ssm_w_out, ln_mix_g, ln_mix_b, ffn_w_up, ffn_conv_w, ffn_conv_b, ffn_w_down, ln_ffn_g, ln_ffn_b, ple_w_proj, ple_w_gate, ple_b_gate]))
    m_loc = dict(zip(names, [m_attn_w_in, m_attn_b_f, m_attn_w_out, m_ssm_w_in, m_ssm_conv_w, m_ssm_conv_b, m_ssm_dt_bias, m_ssm_A_log, m_ssm_D, m_ssm_norm_w, m_ssm_w_out, m_ln_mix_g, m_ln_mix_b, m_ffn_w_up, m_ffn_conv_w, m_ffn_conv_b, m_ffn_w_down, m_ln_ffn_g, m_ln_ffn_b, m_ple_w_proj, m_ple_w_gate, m_ple_b_gate]))
    v_loc = dict(zip(names, [v_attn_w_in, v_attn_b_f, v_attn_w_out, v_ssm_w_in, v_ssm_conv_w, v_ssm_conv_b, v_ssm_dt_bias, v_ssm_A_log, v_ssm_D, v_ssm_norm_w, v_ssm_w_out, v_ln_mix_g, v_ln_mix_b, v_ffn_w_up, v_ffn_conv_w, v_ffn_conv_b, v_ffn_w_down, v_ln_ffn_g, v_ln_ffn_b, v_ple_w_proj, v_ple_w_gate, v_ple_b_gate]))
    sharded = [n for n in names if axes[n] is not None]
    matrices = [n for n in sharded if n in MATMUL_WEIGHTS]

    def wire(n):
        if n in matrices:
            return w_loc[n].astype(BF16)
        return lax.bitcast_convert_type(w_loc[n], BF16)

    wired = [wire(n) for n in sharded]
    me_chip = 2 * lax.axis_index("x") + lax.axis_index("y")
    packed = _pack(wired)
    gathered = lax.dynamic_update_index_in_dim(_gather_chips(packed, name="gather_weights"), packed, me_chip, 0)
    W = dict(w_loc)
    per_chip = [_unpack(gathered[k], [w.shape for w in wired]) for k in range(N_CHIPS)]
    for i, n in enumerate(sharded):
        pieces = [per_chip[k][i] for k in range(N_CHIPS)]
        if n not in matrices:
            pieces = [lax.bitcast_convert_type(q, F32) for q in pieces]
        W[n] = jnp.concatenate(pieces, axis=axes[n])

    loss, grad_x, G = _local_step(x[0], p[:, 0], loss_target[0], W)
    loss = lax.psum(loss, ("x", "y", "c"))

    def slot(k):
        parts = []
        for n in names:
            g = G[n].reshape(W[n].shape)
            if axes[n] is not None:
                size = w_loc[n].shape[axes[n]]
                g = lax.slice_in_dim(g, k * size, (k + 1) * size, axis=axes[n])
            parts.append(g.astype(BF16))
        return _pack(parts)

    contrib = jnp.stack([slot(k) for k in range(N_CHIPS)])
    R = contrib.shape[1]
    H = R // 2
    c = lax.axis_index("c")
    keep = lax.dynamic_slice_in_dim(contrib, c * H, H, axis=1)
    pair = _add2(keep, _pair_swap(contrib, other_half=True, name="grad_pair_swap"), BF16, name="grad_pair_sum")
    from_chips = lax.dynamic_update_index_in_dim(
        _chip_exchange(pair, name="grad_chip_exchange"), lax.dynamic_index_in_dim(pair, me_chip, 0, keepdims=False),
        me_chip, 0)
    half = _sum4(from_chips, name="grad_chip_sum")
    other = _pair_swap(half, name="grad_half_swap")
    gflat = jnp.concatenate([jnp.where(c == 0, half, other), jnp.where(c == 0, other, half)])

    grads = _unpack(gflat, [w_loc[n].shape for n in names])
    steps = [_adamw(w_loc[n], g, m_loc[n], v_loc[n], name=f"adamw_{n}") for n, g in zip(names, grads)]
    return (loss, grad_x[None], *grads, *[s[0] for s in steps], *[s[1] for s in steps], *[s[2] for s in steps])
```
